```python
import jax, jax.numpy as jnp
from jax import lax
import numpy as np

D_MODEL = 1024
BATCH = 8
SEQ = 8192
DEPTH = 2

MIX_WIDTH = D_MODEL
ATTN_WIDTH = MIX_WIDTH // 2
POOL_WIDTH = MIX_WIDTH - ATTN_WIDTH
HEAD_DIM = 64
N_HEADS = ATTN_WIDTH // HEAD_DIM
N_KV_HEADS = 2
GROUP = N_HEADS // N_KV_HEADS
KV_WIDTH = N_KV_HEADS * HEAD_DIM
WINDOW = 128
BLOCK = 128
ROT_DIM = HEAD_DIM // 4
ROPE_THETA = 500000.0
POOL_WINDOWS = (2, 4, 8, 16)
N_POOL_GROUPS = len(POOL_WINDOWS)
POOL_GROUP_WIDTH = POOL_WIDTH // N_POOL_GROUPS
IN_WIDTH = ATTN_WIDTH + 2 * KV_WIDTH + POOL_WIDTH
D_FF = ((int(np.ceil(8 * D_MODEL / 3)) + 255) // 256) * 256
N_MOD = 6
EPS = 1e-6
NEG_INF = -1e30

kernel_name = "hybrid_swa_sink_pool_swiglu_block"


def rms_norm(x, g):
    xf = x.astype(jnp.float32)
    y = xf * lax.rsqrt(jnp.mean(xf * xf, axis=-1, keepdims=True) + EPS)
    return (y * g.astype(jnp.float32)).astype(x.dtype)


def partial_rotary(t, positions):
    inv_freq = ROPE_THETA ** (-jnp.arange(0, ROT_DIM, 2, dtype=jnp.float32) / ROT_DIM)
    ang = positions.astype(jnp.float32)[:, :, None] * inv_freq
    cos = jnp.cos(ang)[:, :, None, :]
    sin = jnp.sin(ang)[:, :, None, :]
    tf = t.astype(jnp.float32)
    half = ROT_DIM // 2
    t1, t2, rest = tf[..., :half], tf[..., half:ROT_DIM], tf[..., ROT_DIM:]
    rot = jnp.concatenate([t1 * cos - t2 * sin, t2 * cos + t1 * sin, rest], axis=-1)
    return rot.astype(t.dtype)


def sliding_window_attention_with_sinks(q, k, v, sinks):
    B, S = q.shape[0], q.shape[1]
    nb = S // BLOCK
    qb = q.reshape(B, nb, BLOCK, N_KV_HEADS, GROUP, HEAD_DIM)
    kb = k.reshape(B, nb, BLOCK, N_KV_HEADS, HEAD_DIM)
    vb = v.reshape(B, nb, BLOCK, N_KV_HEADS, HEAD_DIM)
    pad = ((0, 0), (1, 0), (0, 0), (0, 0), (0, 0))
    k_cat = jnp.concatenate([jnp.pad(kb, pad)[:, :-1], kb], axis=2)
    v_cat = jnp.concatenate([jnp.pad(vb, pad)[:, :-1], vb], axis=2)
    scores = jnp.einsum("bnqkgd,bnskd->bnkgqs", qb, k_cat).astype(jnp.float32)
    scores = scores * (HEAD_DIM ** -0.5)
    qi = jnp.arange(BLOCK)[:, None]
    kj = jnp.arange(2 * BLOCK)[None, :]
    diff = qi + BLOCK - kj
    blk = jnp.arange(nb)[:, None, None]
    key_abs = blk * BLOCK + kj[None] - BLOCK
    valid = (diff[None] >= 0) & (diff[None] < WINDOW) & (key_abs >= 0)
    scores = jnp.where(valid[None, :, None, None], scores, NEG_INF)
    sink = jnp.broadcast_to(
        sinks.astype(jnp.float32).reshape(1, 1, N_KV_HEADS, GROUP, 1, 1),
        scores.shape[:-1] + (1,))
    probs = jax.nn.softmax(jnp.concatenate([scores, sink], axis=-1), axis=-1)[..., :-1]
    out = jnp.einsum("bnkgqs,bnskd->bnqkgd", probs.astype(v.dtype), v_cat)
    return out.reshape(B, S, N_HEADS * HEAD_DIM)


def causal_pool_mixer(u, pool_w, pool_scale):
    S = u.shape[1]
    t = jnp.arange(S, dtype=jnp.float32)[None, :, None]
    outs = []
    for gi, w in enumerate(POOL_WINDOWS):
        ug = u[..., gi * POOL_GROUP_WIDTH:(gi + 1) * POOL_GROUP_WIDTH].astype(jnp.float32)
        cs = jnp.pad(jnp.cumsum(ug, axis=1), ((0, 0), (1, 0), (0, 0)))
        upper = cs[:, 1:]
        lower = jnp.pad(cs, ((0, 0), (w - 1, 0), (0, 0)))[:, :S]
        count = jnp.minimum(t + 1.0, float(w))
        pooled = (upper - lower) / count - ug
        outs.append(jnp.einsum("bsc,cd->bsd", pooled.astype(u.dtype), pool_w[gi]))
    return jnp.concatenate(outs, axis=-1) * pool_scale


def _fwd_setup_inputs(seed: int = 0) -> dict:
    key = jax.random.key(seed)
    ks = jax.random.split(key, 20)
    f32 = jnp.float32
    def nrm(k, shape, scale):
        return jax.random.normal(k, shape, f32) * scale
    x = jax.random.normal(ks[0], (BATCH, SEQ, D_MODEL), f32)
    c = jax.random.normal(ks[1], (BATCH, D_MODEL), f32)
    offsets = jax.random.randint(ks[2], (BATCH, 1), 0, 4096, dtype=jnp.int32)
    positions = (offsets + jnp.arange(SEQ, dtype=jnp.int32)[None, :]).astype(jnp.int32)
    return {
        "x": x,
        "c": c,
        "positions": positions,
        "ada_w": nrm(ks[3], (DEPTH, D_MODEL, N_MOD * D_MODEL), D_MODEL ** -0.5),
        "ada_b": nrm(ks[4], (DEPTH, N_MOD * D_MODEL), 0.02),
        "w_in": nrm(ks[5], (DEPTH, D_MODEL, IN_WIDTH), D_MODEL ** -0.5),
        "b_in": nrm(ks[6], (DEPTH, IN_WIDTH), 0.02),
        "sinks": nrm(ks[7], (DEPTH, N_HEADS), 1.0),
        "pool_w": nrm(ks[8], (DEPTH, N_POOL_GROUPS, POOL_GROUP_WIDTH, POOL_GROUP_WIDTH), POOL_GROUP_WIDTH ** -0.5),
        "pool_scale": 1.0 + nrm(ks[9], (DEPTH, POOL_WIDTH), 0.1),
        "w_out": nrm(ks[10], (DEPTH, MIX_WIDTH, D_MODEL), MIX_WIDTH ** -0.5),
        "w_gate": nrm(ks[11], (DEPTH, D_MODEL, D_FF), D_MODEL ** -0.5),
        "w_up": nrm(ks[12], (DEPTH, D_MODEL, D_FF), D_MODEL ** -0.5),
        "w_down": nrm(ks[13], (DEPTH, D_FF, D_MODEL), D_FF ** -0.5),
        "g_pre_mix": 1.0 + nrm(ks[14], (DEPTH, D_MODEL), 0.02),
        "g_post_mix": 1.0 + nrm(ks[15], (DEPTH, D_MODEL), 0.02),
        "g_pre_ffn": 1.0 + nrm(ks[16], (DEPTH, D_MODEL), 0.02),
        "g_post_ffn": 1.0 + nrm(ks[17], (DEPTH, D_MODEL), 0.02),
    }


def _fwd_reference(x, c, positions, ada_w, ada_b, w_in, b_in, sinks, pool_w, pool_scale,
              w_out, w_gate, w_up, w_down, g_pre_mix, g_post_mix, g_pre_ffn, g_post_ffn):
    B, S = x.shape[0], x.shape[1]
    c_act = jax.nn.silu(c)
    for l in range(DEPTH):
        mod = c_act @ ada_w[l] + ada_b[l]
        shift_m, scale_m, gate_m, shift_f, scale_f, gate_f = [
            m[:, None, :] for m in jnp.split(mod, N_MOD, axis=-1)]

        h = rms_norm(x, g_pre_mix[l]) * (1.0 + scale_m) + shift_m
        proj = h @ w_in[l] + b_in[l]
        q, k, v, u = jnp.split(
            proj, [ATTN_WIDTH, ATTN_WIDTH + KV_WIDTH, ATTN_WIDTH + 2 * KV_WIDTH], axis=-1)
        q = partial_rotary(q.reshape(B, S, N_HEADS, HEAD_DIM), positions)
        k = partial_rotary(k.reshape(B, S, N_KV_HEADS, HEAD_DIM), positions)
        v = v.reshape(B, S, N_KV_HEADS, HEAD_DIM)
        attn_out = sliding_window_attention_with_sinks(q, k, v, sinks[l])
        pool_out = causal_pool_mixer(u, pool_w[l], pool_scale[l])
        mix = jnp.concatenate([attn_out, pool_out], axis=-1) @ w_out[l]
        x = x + gate_m * rms_norm(mix, g_post_mix[l])

        h = rms_norm(x, g_pre_ffn[l]) * (1.0 + scale_f) + shift_f
        f = (jax.nn.silu(h @ w_gate[l]) * (h @ w_up[l])) @ w_down[l]
        x = x + gate_f * rms_norm(f, g_post_ffn[l])
    return x


import jax as _jax
import jax.numpy as _jnp

TWIN_FORMAT = 'train_step'
FWD_PARAMS = ['x', 'c', 'positions', 'ada_w', 'ada_b', 'w_in', 'b_in', 'sinks', 'pool_w', 'pool_scale', 'w_out', 'w_gate', 'w_up', 'w_down', 'g_pre_mix', 'g_post_mix', 'g_pre_ffn', 'g_post_ffn']
TWIN_WEIGHTS = ['ada_w', 'ada_b', 'w_in', 'b_in', 'sinks', 'pool_w', 'pool_scale', 'w_out', 'w_gate', 'w_up', 'w_down', 'g_pre_mix', 'g_post_mix', 'g_pre_ffn', 'g_post_ffn']
TWIN_DIFF_INPUT = 'x'
TWIN_INPUTS = ['x', 'c', 'positions', 'ada_w', 'ada_b', 'w_in', 'b_in', 'sinks', 'pool_w', 'pool_scale', 'w_out', 'w_gate', 'w_up', 'w_down', 'g_pre_mix', 'g_post_mix', 'g_pre_ffn', 'g_post_ffn', 'loss_target', 'm_ada_w', 'm_ada_b', 'm_w_in', 'm_b_in', 'm_sinks', 'm_pool_w', 'm_pool_scale', 'm_w_out', 'm_w_gate', 'm_w_up', 'm_w_down', 'm_g_pre_mix', 'm_g_post_mix', 'm_g_pre_ffn', 'm_g_post_ffn', 'v_ada_w', 'v_ada_b', 'v_w_in', 'v_b_in', 'v_sinks', 'v_pool_w', 'v_pool_scale', 'v_w_out', 'v_w_gate', 'v_w_up', 'v_w_down', 'v_g_pre_mix', 'v_g_post_mix', 'v_g_pre_ffn', 'v_g_post_ffn']
TWIN_OUTPUTS = ['loss', 'grad_x', 'grad_ada_w', 'grad_ada_b', 'grad_w_in', 'grad_b_in', 'grad_sinks', 'grad_pool_w', 'grad_pool_scale', 'grad_w_out', 'grad_w_gate', 'grad_w_up', 'grad_w_down', 'grad_g_pre_mix', 'grad_g_post_mix', 'grad_g_pre_ffn', 'grad_g_post_ffn', 'delta_ada_w', 'delta_ada_b', 'delta_w_in', 'delta_b_in', 'delta_sinks', 'delta_pool_w', 'delta_pool_scale', 'delta_w_out', 'delta_w_gate', 'delta_w_up', 'delta_w_down', 'delta_g_pre_mix', 'delta_g_post_mix', 'delta_g_pre_ffn', 'delta_g_post_ffn', 'new_m_ada_w', 'new_m_ada_b', 'new_m_w_in', 'new_m_b_in', 'new_m_sinks', 'new_m_pool_w', 'new_m_pool_scale', 'new_m_w_out', 'new_m_w_gate', 'new_m_w_up', 'new_m_w_down', 'new_m_g_pre_mix', 'new_m_g_post_mix', 'new_m_g_pre_ffn', 'new_m_g_post_ffn', 'new_v_ada_w', 'new_v_ada_b', 'new_v_w_in', 'new_v_b_in', 'new_v_sinks', 'new_v_pool_w', 'new_v_pool_scale', 'new_v_w_out', 'new_v_w_gate', 'new_v_w_up', 'new_v_w_down', 'new_v_g_pre_mix', 'new_v_g_post_mix', 'new_v_g_pre_ffn', 'new_v_g_post_ffn']
TWIN_LEAF_KINDS = {'loss': 'loss', 'grad_x': 'grad_x', 'grad_ada_w': 'grad_w', 'grad_ada_b': 'grad_w', 'grad_w_in': 'grad_w', 'grad_b_in': 'grad_w', 'grad_sinks': 'grad_w', 'grad_pool_w': 'grad_w', 'grad_pool_scale': 'grad_w', 'grad_w_out': 'grad_w', 'grad_w_gate': 'grad_w', 'grad_w_up': 'grad_w', 'grad_w_down': 'grad_w', 'grad_g_pre_mix': 'grad_w', 'grad_g_post_mix': 'grad_w', 'grad_g_pre_ffn': 'grad_w', 'grad_g_post_ffn': 'grad_w', 'delta_ada_w': 'delta_w', 'delta_ada_b': 'delta_w', 'delta_w_in': 'delta_w', 'delta_b_in': 'delta_w', 'delta_sinks': 'delta_w', 'delta_pool_w': 'delta_w', 'delta_pool_scale': 'delta_w', 'delta_w_out': 'delta_w', 'delta_w_gate': 'delta_w', 'delta_w_up': 'delta_w', 'delta_w_down': 'delta_w', 'delta_g_pre_mix': 'delta_w', 'delta_g_post_mix': 'delta_w', 'delta_g_pre_ffn': 'delta_w', 'delta_g_post_ffn': 'delta_w', 'new_m_ada_w': 'new_m', 'new_m_ada_b': 'new_m', 'new_m_w_in': 'new_m', 'new_m_b_in': 'new_m', 'new_m_sinks': 'new_m', 'new_m_pool_w': 'new_m', 'new_m_pool_scale': 'new_m', 'new_m_w_out': 'new_m', 'new_m_w_gate': 'new_m', 'new_m_w_up': 'new_m', 'new_m_w_down': 'new_m', 'new_m_g_pre_mix': 'new_m', 'new_m_g_post_mix': 'new_m', 'new_m_g_pre_ffn': 'new_m', 'new_m_g_post_ffn': 'new_m', 'new_v_ada_w': 'new_v', 'new_v_ada_b': 'new_v', 'new_v_w_in': 'new_v', 'new_v_b_in': 'new_v', 'new_v_sinks': 'new_v', 'new_v_pool_w': 'new_v', 'new_v_pool_scale': 'new_v', 'new_v_w_out': 'new_v', 'new_v_w_gate': 'new_v', 'new_v_w_up': 'new_v', 'new_v_w_down': 'new_v', 'new_v_g_pre_mix': 'new_v', 'new_v_g_post_mix': 'new_v', 'new_v_g_pre_ffn': 'new_v', 'new_v_g_post_ffn': 'new_v'}


def _forward(args):
    return _fwd_reference(*[args[k] for k in FWD_PARAMS])


def _output_shape():
    out = _jax.eval_shape(lambda: _forward(_fwd_setup_inputs(0)))
    return out.shape, out.dtype

N_MICROBATCH = 1
ADAM_LR = 0.001
ADAM_B1 = 0.9
ADAM_B2 = 0.999
ADAM_EPS = 1e-08
ADAM_WD = 0.01
ADAM_STEP = 10
PER_EXAMPLE_BATCH_AXIS = {'x': 0, 'c': 0, 'positions': 0, 'loss_target': 0}
SHARED_INPUTS = []
_WEIGHT_DTYPES = {'ada_w': _jnp.float32, 'ada_b': _jnp.float32, 'w_in': _jnp.float32, 'b_in': _jnp.float32, 'sinks': _jnp.float32, 'pool_w': _jnp.float32, 'pool_scale': _jnp.float32, 'w_out': _jnp.float32, 'w_gate': _jnp.float32, 'w_up': _jnp.float32, 'w_down': _jnp.float32, 'g_pre_mix': _jnp.float32, 'g_post_mix': _jnp.float32, 'g_pre_ffn': _jnp.float32, 'g_post_ffn': _jnp.float32}
MOMENT_SCALE = {'ada_w': 8.364660e+00, 'ada_b': 1.486669e+01, 'w_in': 5.553934e+00, 'b_in': 9.081950e+00, 'sinks': 2.161226e-01, 'pool_w': 9.627262e-01, 'pool_scale': 1.359620e+00, 'w_out': 6.643575e+00, 'w_gate': 1.384818e+00, 'w_up': 1.975086e+00, 'w_down': 3.263971e+00, 'g_pre_mix': 1.795214e+00, 'g_post_mix': 3.166700e+01, 'g_pre_ffn': 1.738780e+00, 'g_post_ffn': 3.010001e+01}


def _to_microbatches(a, axis):
    t = _jnp.moveaxis(a, axis, 0)
    t = t.reshape((N_MICROBATCH, t.shape[0] // N_MICROBATCH) + t.shape[1:])
    return _jnp.moveaxis(t, 1, axis + 1)


def setup_inputs(seed: int = 0) -> dict:
    inp = _fwd_setup_inputs(seed)
    key = _jax.random.fold_in(_jax.random.key(seed), 7919)
    shape, _ = _output_shape()
    out = dict(inp)
    out["loss_target"] = _jax.random.normal(_jax.random.fold_in(key, 0), shape, _jnp.float32)
    for i, name in enumerate(TWIN_WEIGHTS):
        w = inp[name].astype(_jnp.float32)
        if MOMENT_SCALE is None:
            s = _jnp.sqrt(_jnp.mean(_jnp.square(w)) + 1e-30)
        else:
            s = MOMENT_SCALE[name]
        km, kv = _jax.random.split(_jax.random.fold_in(key, i + 1))
        out[name] = w
        out["m_" + name] = s * _jax.random.normal(km, w.shape, _jnp.float32)
        out["v_" + name] = (s * s) * _jax.random.uniform(kv, w.shape, _jnp.float32, 0.5, 1.5)
    if N_MICROBATCH > 1:
        for name, axis in PER_EXAMPLE_BATCH_AXIS.items():
            out[name] = _to_microbatches(out[name], axis)
    return {'x': out['x'], 'c': out['c'], 'positions': out['positions'], 'ada_w': out['ada_w'], 'ada_b': out['ada_b'], 'w_in': out['w_in'], 'b_in': out['b_in'], 'sinks': out['sinks'], 'pool_w': out['pool_w'], 'pool_scale': out['pool_scale'], 'w_out': out['w_out'], 'w_gate': out['w_gate'], 'w_up': out['w_up'], 'w_down': out['w_down'], 'g_pre_mix': out['g_pre_mix'], 'g_post_mix': out['g_post_mix'], 'g_pre_ffn': out['g_pre_ffn'], 'g_post_ffn': out['g_post_ffn'], 'loss_target': out['loss_target'], 'm_ada_w': out['m_ada_w'], 'm_ada_b': out['m_ada_b'], 'm_w_in': out['m_w_in'], 'm_b_in': out['m_b_in'], 'm_sinks': out['m_sinks'], 'm_pool_w': out['m_pool_w'], 'm_pool_scale': out['m_pool_scale'], 'm_w_out': out['m_w_out'], 'm_w_gate': out['m_w_gate'], 'm_w_up': out['m_w_up'], 'm_w_down': out['m_w_down'], 'm_g_pre_mix': out['m_g_pre_mix'], 'm_g_post_mix': out['m_g_post_mix'], 'm_g_pre_ffn': out['m_g_pre_ffn'], 'm_g_post_ffn': out['m_g_post_ffn'], 'v_ada_w': out['v_ada_w'], 'v_ada_b': out['v_ada_b'], 'v_w_in': out['v_w_in'], 'v_b_in': out['v_b_in'], 'v_sinks': out['v_sinks'], 'v_pool_w': out['v_pool_w'], 'v_pool_scale': out['v_pool_scale'], 'v_w_out': out['v_w_out'], 'v_w_gate': out['v_w_gate'], 'v_w_up': out['v_w_up'], 'v_w_down': out['v_w_down'], 'v_g_pre_mix': out['v_g_pre_mix'], 'v_g_post_mix': out['v_g_post_mix'], 'v_g_pre_ffn': out['v_g_pre_ffn'], 'v_g_post_ffn': out['v_g_post_ffn']}


def _loss(weights, diff, rest, loss_target):
    with _jax.named_scope("forward"):
        args = {**rest, TWIN_DIFF_INPUT: diff, **{k: w.astype(_WEIGHT_DTYPES[k]) for k, w in weights.items()}}
        y = _forward(args)
    with _jax.named_scope("loss_head"):
        err = _jnp.square(y.astype(_jnp.float32) - loss_target)
        return 0.5 * _jnp.sum(_jnp.mean(err, axis=-1)) if err.ndim else 0.5 * err


def _adamw(w, g, m, v):
    m = ADAM_B1 * m + (1.0 - ADAM_B1) * g
    v = ADAM_B2 * v + (1.0 - ADAM_B2) * _jnp.square(g)
    m_hat = m / (1.0 - ADAM_B1 ** ADAM_STEP)
    v_hat = v / (1.0 - ADAM_B2 ** ADAM_STEP)
    delta = -ADAM_LR * (m_hat / (_jnp.sqrt(v_hat) + ADAM_EPS) + ADAM_WD * w)
    return delta, m, v


def reference(x, c, positions, ada_w, ada_b, w_in, b_in, sinks, pool_w, pool_scale, w_out, w_gate, w_up, w_down, g_pre_mix, g_post_mix, g_pre_ffn, g_post_ffn, loss_target, m_ada_w, m_ada_b, m_w_in, m_b_in, m_sinks, m_pool_w, m_pool_scale, m_w_out, m_w_gate, m_w_up, m_w_down, m_g_pre_mix, m_g_post_mix, m_g_pre_ffn, m_g_post_ffn, v_ada_w, v_ada_b, v_w_in, v_b_in, v_sinks, v_pool_w, v_pool_scale, v_w_out, v_w_gate, v_w_up, v_w_down, v_g_pre_mix, v_g_post_mix, v_g_pre_ffn, v_g_post_ffn):
    given = dict(x=x, c=c, positions=positions, ada_w=ada_w, ada_b=ada_b, w_in=w_in, b_in=b_in, sinks=sinks, pool_w=pool_w, pool_scale=pool_scale, w_out=w_out, w_gate=w_gate, w_up=w_up, w_down=w_down, g_pre_mix=g_pre_mix, g_post_mix=g_post_mix, g_pre_ffn=g_pre_ffn, g_post_ffn=g_post_ffn, loss_target=loss_target, m_ada_w=m_ada_w, m_ada_b=m_ada_b, m_w_in=m_w_in, m_b_in=m_b_in, m_sinks=m_sinks, m_pool_w=m_pool_w, m_pool_scale=m_pool_scale, m_w_out=m_w_out, m_w_gate=m_w_gate, m_w_up=m_w_up, m_w_down=m_w_down, m_g_pre_mix=m_g_pre_mix, m_g_post_mix=m_g_post_mix, m_g_pre_ffn=m_g_pre_ffn, m_g_post_ffn=m_g_post_ffn, v_ada_w=v_ada_w, v_ada_b=v_ada_b, v_w_in=v_w_in, v_b_in=v_b_in, v_sinks=v_sinks, v_pool_w=v_pool_w, v_pool_scale=v_pool_scale, v_w_out=v_w_out, v_w_gate=v_w_gate, v_w_up=v_w_up, v_w_down=v_w_down, v_g_pre_mix=v_g_pre_mix, v_g_post_mix=v_g_post_mix, v_g_pre_ffn=v_g_pre_ffn, v_g_post_ffn=v_g_post_ffn)
    weights = {n: given[n] for n in TWIN_WEIGHTS}
    shared = {n: given[n] for n in SHARED_INPUTS}
    per_example = {n: given[n] for n in ['x', 'c', 'positions']}
    grad_fn = _jax.value_and_grad(_loss, argnums=(0, 1))

    def one_microbatch(ex, loss_target):
        ex = dict(ex)
        diff = ex.pop(TWIN_DIFF_INPUT)
        return grad_fn(weights, diff, {**shared, **ex}, loss_target)

    if N_MICROBATCH == 1:
        loss, (grad_w, grad_x) = one_microbatch(per_example, given["loss_target"])
    else:
        def body(carry, xs):
            loss_sum, grad_sum = carry
            l_k, (gw_k, gx_k) = one_microbatch(xs[0], xs[1])
            with _jax.named_scope("update"):
                return (loss_sum + l_k, _jax.tree.map(_jnp.add, grad_sum, gw_k)), gx_k

        init = (_jnp.zeros((), _jnp.float32), _jax.tree.map(_jnp.zeros_like, weights))
        (loss, grad_w), grad_x = _jax.lax.scan(body, init, (per_example, given["loss_target"]))
    with _jax.named_scope("update"):
        delta_w, new_m, new_v = {}, {}, {}
        for n in TWIN_WEIGHTS:
            delta_w[n], new_m[n], new_v[n] = _adamw(weights[n], grad_w[n], given["m_" + n], given["v_" + n])
    return (loss, grad_x, *[grad_w[n] for n in TWIN_WEIGHTS], *[delta_w[n] for n in TWIN_WEIGHTS],
            *[new_m[n] for n in TWIN_WEIGHTS], *[new_v[n] for n in TWIN_WEIGHTS])
```

```python
import jax
import jax.numpy as jnp
from jax import lax
from jax.experimental import pallas as pl
from jax.experimental.pallas import tpu as pltpu

F32 = jnp.float32
BF16 = jnp.bfloat16

N_DEV = 8
DEPTH = 2
D_MODEL = 1024
HEAD_DIM = 64
N_HEADS = 8
N_KV_HEADS = 2
GROUP = N_HEADS // N_KV_HEADS
ATTN_WIDTH = N_HEADS * HEAD_DIM
KV_WIDTH = N_KV_HEADS * HEAD_DIM
POOL_WIDTH = 512
POOL_WINDOWS = (2, 4, 8, 16)
POOL_GROUP_WIDTH = 128
POOL_HALO = 16
IN_WIDTH = ATTN_WIDTH + 2 * KV_WIDTH + POOL_WIDTH
D_FF = 2816
N_MOD = 6
BLOCK = 128
ROT_DIM = 16
ROPE_THETA = 500000.0
EPS = 1e-6
NEG_INF = -1e30
Q_SCALE = HEAD_DIM ** -0.5

ADAM_LR = 0.001
ADAM_B1 = 0.9
ADAM_B2 = 0.999
ADAM_EPS = 1e-08
ADAM_WD = 0.01
ADAM_STEP = 10

LANES = 128
SEQ_TILE = 512
FF_CHUNK = 256
VMEM_LIMIT = 56 * 1024 * 1024
MESH = pl.DeviceIdType.MESH

NT = (((1,), (1,)), ((), ()))
TN = (((0,), (0,)), ((), ()))


def _dot(a, b, dims=None):
    if dims is None:
        return jnp.dot(a, b, preferred_element_type=F32)
    return lax.dot_general(a, b, dims, preferred_element_type=F32)


def _cparams(n_axes):
    return pltpu.CompilerParams(dimension_semantics=("arbitrary",) * n_axes, vmem_limit_bytes=VMEM_LIMIT)


def _resident(shape):
    zeros = (0,) * len(shape)
    return pl.BlockSpec(shape, lambda *_: zeros, pipeline_mode=pl.Buffered(1))


def _rows(*vectors, width):
    rows = [jnp.reshape(v, (1, width)).astype(F32) for v in vectors]
    rows.append(jnp.zeros((8 - len(rows), width), F32))
    return jnp.concatenate(rows, axis=0)


def _rsqrt_ms(x):
    return lax.rsqrt(jnp.mean(x * x, axis=-1, keepdims=True) + EPS)


def _colsum(x):
    return jnp.sum(x, axis=0, keepdims=True)


def _seq_tile(s):
    return min(s, SEQ_TILE)


def _mesh_place():
    x, y, c = lax.axis_index("x"), lax.axis_index("y"), lax.axis_index("c")
    return x, y, c


def _flip(place, k):
    x, y, c = place
    return (1 - x if k & 4 else x, 1 - y if k & 2 else y, 1 - c if k & 1 else c)


def _index(place):
    x, y, c = place
    return 4 * x + 2 * y + c


def _allgather_vmem(block, name):
    r, c = block.shape

    def body(x_ref, out_ref, send_sems, recv_sems, local_sem):
        me = _mesh_place()
        mine = pltpu.make_async_copy(x_ref, out_ref.at[_index(me)], local_sem)
        mine.start()

        def copy(k):
            return pltpu.make_async_remote_copy(
                src_ref=x_ref, dst_ref=out_ref.at[_index(me)], send_sem=send_sems.at[k - 1], recv_sem=recv_sems.at[k - 1],
                device_id=_flip(me, k), device_id_type=MESH)

        def arrival(k):
            return pltpu.make_async_remote_copy(
                src_ref=x_ref, dst_ref=out_ref.at[_index(_flip(me, k))], send_sem=send_sems.at[k - 1],
                recv_sem=recv_sems.at[k - 1], device_id=_flip(me, k), device_id_type=MESH)

        for k in range(1, N_DEV):
            copy(k).start()
        for k in range(1, N_DEV):
            arrival(k).wait_recv()
        for k in range(1, N_DEV):
            copy(k).wait_send()
        mine.wait()

    return pl.pallas_call(
        body, name=name,
        out_shape=jax.ShapeDtypeStruct((N_DEV, r, c), block.dtype),
        in_specs=[pl.BlockSpec(memory_space=pltpu.VMEM)],
        out_specs=pl.BlockSpec(memory_space=pltpu.VMEM),
        scratch_shapes=[pltpu.SemaphoreType.DMA((N_DEV - 1,)), pltpu.SemaphoreType.DMA((N_DEV - 1,)),
                        pltpu.SemaphoreType.DMA],
    )(block)


def _allgather_hbm(shards, name):
    n = len(shards)

    def body(*refs):
        x_refs, out_refs = refs[:n], refs[n:2 * n]
        send_sems, recv_sems, local_sems = refs[2 * n:]
        x, y, c = _mesh_place()
        me, sibling = (x, y, c), (x, y, 1 - c)
        chips = [(1 - x, y), (x, 1 - y), (1 - x, 1 - y)]

        def copy(k, a, block, to, from_input=False):
            rows = out_refs[a].at[_index(block)]
            return pltpu.make_async_remote_copy(
                src_ref=x_refs[a] if from_input else rows, dst_ref=rows,
                send_sem=send_sems.at[k, a], recv_sem=recv_sems.at[k, a], device_id=to, device_id_type=MESH)

        mine = [pltpu.make_async_copy(x_refs[a], out_refs[a].at[_index(me)], local_sems.at[a]) for a in range(n)]
        for cp in mine:
            cp.start()
        first = [copy(0, a, me, sibling, from_input=True) for a in range(n)]
        first += [copy(1 + j, a, me, (*chip, c), from_input=True) for j, chip in enumerate(chips) for a in range(n)]
        for cp in first:
            cp.start()
        passed = []
        for j, chip in enumerate(chips):
            for a in range(n):
                copy(1 + j, a, (*chip, c), me).wait_recv()
                fwd = copy(4 + j, a, (*chip, c), sibling)
                fwd.start()
                passed.append(fwd)
        for a in range(n):
            copy(0, a, sibling, me).wait_recv()
        for j, chip in enumerate(chips):
            for a in range(n):
                copy(4 + j, a, (*chip, 1 - c), me).wait_recv()
        for cp in first + passed:
            cp.wait_send()
        for cp in mine:
            cp.wait()

    any_spec = pl.BlockSpec(memory_space=pl.ANY)
    return pl.pallas_call(
        body, name=name,
        out_shape=[jax.ShapeDtypeStruct((N_DEV,) + s.shape, s.dtype) for s in shards],
        in_specs=[any_spec] * n, out_specs=[any_spec] * n,
        scratch_shapes=[pltpu.SemaphoreType.DMA((N_DEV - 1, n)), pltpu.SemaphoreType.DMA((N_DEV - 1, n)),
                        pltpu.SemaphoreType.DMA((n,))],
    )(*shards)


def _scatter_hbm(stacks, name):
    n = len(stacks)

    def body(*refs):
        g_refs, out_refs = refs[:n], refs[n:2 * n]
        send_sems, recv_sems = refs[2 * n:]
        me = _mesh_place()

        def copy(k, a):
            peer = _flip(me, k)
            return pltpu.make_async_remote_copy(
                src_ref=g_refs[a].at[_index(peer)], dst_ref=out_refs[a].at[k - 1],
                send_sem=send_sems.at[k - 1, a], recv_sem=recv_sems.at[k - 1, a], device_id=peer, device_id_type=MESH)

        copies = [copy(k, a) for k in range(1, N_DEV) for a in range(n)]
        for cp in copies:
            cp.start()
        for cp in copies:
            cp.wait_recv()
        for cp in copies:
            cp.wait_send()

    any_spec = pl.BlockSpec(memory_space=pl.ANY)
    return pl.pallas_call(
        body, name=name,
        out_shape=[jax.ShapeDtypeStruct((N_DEV - 1,) + s.shape[1:], s.dtype) for s in stacks],
        in_specs=[any_spec] * n, out_specs=[any_spec] * n,
        scratch_shapes=[pltpu.SemaphoreType.DMA((N_DEV - 1, n)), pltpu.SemaphoreType.DMA((N_DEV - 1, n))],
    )(*stacks)


def _mod_fwd(c_all, ada_w):
    cols = ada_w.shape[2]

    def body(c_ref, w_ref, mp_ref, act_ref):
        c = c_ref[...]
        act = c * jax.nn.sigmoid(c)
        act_ref[...] = act
        mp_ref[0] = _dot(act.astype(BF16), w_ref[0].astype(BF16))

    return pl.pallas_call(
        body, name="mod_fwd", grid=(DEPTH,),
        in_specs=[pl.BlockSpec((N_DEV, D_MODEL), lambda l: (0, 0)),
                  pl.BlockSpec((1, D_MODEL, cols), lambda l: (l, 0, 0))],
        out_specs=[pl.BlockSpec((1, N_DEV, cols), lambda l: (l, 0, 0)),
                   pl.BlockSpec((N_DEV, D_MODEL), lambda l: (0, 0))],
        out_shape=[jax.ShapeDtypeStruct((DEPTH, N_DEV, cols), F32), jax.ShapeDtypeStruct((N_DEV, D_MODEL), F32)],
        compiler_params=_cparams(1),
    )(c_all, ada_w)


def _ada_grad(c_act, dmod_cols):
    cols = dmod_cols.shape[2]

    def body(act_ref, dm_ref, g_ref):
        g_ref[0] = _dot(act_ref[...].astype(BF16), dm_ref[0].astype(BF16), TN)

    return pl.pallas_call(
        body, name="ada_grad", grid=(DEPTH,),
        in_specs=[pl.BlockSpec((N_DEV, D_MODEL), lambda l: (0, 0)),
                  pl.BlockSpec((1, N_DEV, cols), lambda l: (l, 0, 0))],
        out_specs=pl.BlockSpec((1, D_MODEL, cols), lambda l: (l, 0, 0)),
        out_shape=jax.ShapeDtypeStruct((DEPTH, D_MODEL, cols), F32),
        compiler_params=_cparams(1),
    )(c_act, dmod_cols)


def _rotate(t, cos, sin_lo, sin_hi):
    return t * cos + pltpu.roll(t, LANES - 8, 1) * sin_lo + pltpu.roll(t, 8, 1) * sin_hi


def _rotate_bwd(t, cos, sin_lo, sin_hi):
    return t * cos + pltpu.roll(t * sin_lo, 8, 1) + pltpu.roll(t * sin_hi, LANES - 8, 1)


def _fwd_in(x, vec, w_in_t, b_in, rot, name):
    s = x.shape[0]
    ts = _seq_tile(s)

    def body(x_ref, vec_ref, w_ref, b_ref, cos_ref, lo_ref, hi_ref, q_ref, k_ref, v_ref, u_ref):
        xt = x_ref[...]
        h = xt * _rsqrt_ms(xt) * vec_ref[0:1, :] * vec_ref[1:2, :] + vec_ref[2:3, :]
        proj = _dot(h.astype(BF16), w_ref[...], NT) + b_ref[0:1, :]
        cos, lo, hi = cos_ref[...], lo_ref[...], hi_ref[...]
        for j in range(ATTN_WIDTH // LANES):
            q_ref[:, j * LANES:(j + 1) * LANES] = (
                _rotate(proj[:, j * LANES:(j + 1) * LANES], cos, lo, hi) * Q_SCALE).astype(BF16)
        k_ref[...] = _rotate(proj[:, ATTN_WIDTH:ATTN_WIDTH + KV_WIDTH], cos, lo, hi).astype(BF16)
        v_ref[...] = proj[:, ATTN_WIDTH + KV_WIDTH:ATTN_WIDTH + 2 * KV_WIDTH].astype(BF16)
        u_ref[...] = proj[:, ATTN_WIDTH + 2 * KV_WIDTH:]

    row = lambda w: pl.BlockSpec((ts, w), lambda i: (i, 0))
    return pl.pallas_call(
        body, name=name, grid=(s // ts,),
        in_specs=[row(D_MODEL), _resident((8, D_MODEL)), _resident((IN_WIDTH, D_MODEL)), _resident((8, IN_WIDTH)),
                  row(LANES), row(LANES), row(LANES)],
        out_specs=[row(ATTN_WIDTH), row(KV_WIDTH), row(KV_WIDTH), row(POOL_WIDTH)],
        out_shape=[jax.ShapeDtypeStruct((s, ATTN_WIDTH), BF16), jax.ShapeDtypeStruct((s, KV_WIDTH), BF16),
                   jax.ShapeDtypeStruct((s, KV_WIDTH), BF16), jax.ShapeDtypeStruct((s, POOL_WIDTH), F32)],
        compiler_params=_cparams(1),
    )(x, vec, w_in_t, b_in, *rot)


def _band_mask(first_block):
    row = lax.broadcasted_iota(jnp.int32, (BLOCK, 2 * BLOCK), 0)
    col = lax.broadcasted_iota(jnp.int32, (BLOCK, 2 * BLOCK), 1)
    d = col - row
    return (d >= 1) & (d <= BLOCK) & ((col >= BLOCK) | jnp.logical_not(first_block))


def _head(ref, h):
    return ref[:, h * HEAD_DIM:(h + 1) * HEAD_DIM]


def _kv_cat(prev_ref, cur_ref, kh):
    return jnp.concatenate([_head(prev_ref, kh), _head(cur_ref, kh)], axis=0)


def _attn_fwd(sinks, q, k, v, name):
    s = q.shape[0]
    nb = s // BLOCK
    cur = lambda w: pl.BlockSpec((BLOCK, w), lambda i: (i, 0))
    prev = lambda w: pl.BlockSpec((BLOCK, w), lambda i: (jnp.maximum(i - 1, 0), 0))

    def body(sink_ref, q_ref, kc_ref, kp_ref, vc_ref, vp_ref, o_ref, l_ref):
        valid = _band_mask(pl.program_id(0) == 0)
        for kh in range(N_KV_HEADS):
            kcat, vcat = _kv_cat(kp_ref, kc_ref, kh), _kv_cat(vp_ref, vc_ref, kh)
            for g in range(GROUP):
                h = kh * GROUP + g
                sc = jnp.where(valid, _dot(_head(q_ref, h), kcat, NT), NEG_INF)
                sink = sink_ref[h]
                m = jnp.maximum(jnp.max(sc, axis=1, keepdims=True), sink)
                p = jnp.exp(sc - m)
                den = jnp.sum(p, axis=1, keepdims=True) + jnp.exp(sink - m)
                o_ref[:, h * HEAD_DIM:(h + 1) * HEAD_DIM] = _dot((p / den).astype(BF16), vcat).astype(BF16)
                l_ref[:, h:h + 1] = m + jnp.log(den)

    return pl.pallas_call(
        body, name=name, grid=(nb,),
        in_specs=[pl.BlockSpec(memory_space=pltpu.SMEM), cur(ATTN_WIDTH), cur(KV_WIDTH), prev(KV_WIDTH),
                  cur(KV_WIDTH), prev(KV_WIDTH)],
        out_specs=[cur(ATTN_WIDTH), cur(N_HEADS)],
        out_shape=[jax.ShapeDtypeStruct((s, ATTN_WIDTH), BF16), jax.ShapeDtypeStruct((s, N_HEADS), F32)],
        compiler_params=_cparams(1),
    )(sinks, q, k, k, v, v)


def _pool_counts(tile, ts):
    t = (tile * ts + lax.broadcasted_iota(jnp.int32, (ts, 1), 0) + 1).astype(F32)
    return [jnp.minimum(t, float(w)) for w in POOL_WINDOWS]


def _pooled(u, halo, counts, gi):
    cols = slice(gi * POOL_GROUP_WIDTH, (gi + 1) * POOL_GROUP_WIDTH)
    acc = jnp.concatenate([halo[:, cols], u[:, cols]], axis=0)
    shift = 1
    while shift < POOL_WINDOWS[gi]:
        acc = acc + pltpu.roll(acc, shift, 0)
        shift *= 2
    return acc[POOL_HALO:, :] / counts[gi] - u[:, cols]


def _fwd_out(ao, u, pool_w, psc, w_out, x, vec, name):
    s = x.shape[0]
    ts = _seq_tile(s)
    hb = ts // POOL_HALO

    def body(ao_ref, u_ref, uh_ref, pw_ref, psc_ref, w_ref, x_ref, vec_ref, x1_ref, mix_ref):
        i = pl.program_id(0)
        u_t = u_ref[...]
        halo = jnp.where(i > 0, uh_ref[...], 0.0)
        counts = _pool_counts(i, ts)
        mix = _dot(ao_ref[...], w_ref[0:ATTN_WIDTH, :])
        for gi in range(len(POOL_WINDOWS)):
            cols = slice(gi * POOL_GROUP_WIDTH, (gi + 1) * POOL_GROUP_WIDTH)
            og = _dot(_pooled(u_t, halo, counts, gi).astype(BF16), pw_ref[gi]) * psc_ref[0:1, cols]
            mix = mix + _dot(og.astype(BF16), w_ref[ATTN_WIDTH + gi * POOL_GROUP_WIDTH:
                                                    ATTN_WIDTH + (gi + 1) * POOL_GROUP_WIDTH, :])
        mix_ref[...] = mix
        x1_ref[...] = x_ref[...] + vec_ref[0:1, :] * (mix * _rsqrt_ms(mix) * vec_ref[1:2, :])

    row = lambda w: pl.BlockSpec((ts, w), lambda i: (i, 0))
    return pl.pallas_call(
        body, name=name, grid=(s // ts,),
        in_specs=[row(ATTN_WIDTH), row(POOL_WIDTH),
                  pl.BlockSpec((POOL_HALO, POOL_WIDTH), lambda i: (jnp.maximum(i * hb - 1, 0), 0)),
                  _resident(pool_w.shape), _resident((8, POOL_WIDTH)), _resident((D_MODEL, D_MODEL)),
                  row(D_MODEL), _resident((8, D_MODEL))],
        out_specs=[row(D_MODEL), row(D_MODEL)],
        out_shape=[jax.ShapeDtypeStruct((s, D_MODEL), F32), jax.ShapeDtypeStruct((s, D_MODEL), F32)],
        compiler_params=_cparams(1),
    )(ao, u, u, pool_w, psc, w_out, x, vec)


def _ffn_fwd(x1, vec, w_gate_t, w_up_t, w_down, name):
    s = x1.shape[0]
    ts = _seq_tile(s)

    def body(x_ref, vec_ref, wg_ref, wu_ref, wd_ref, x2_ref, f_ref, gt_ref, up_ref, act_ref, h2_ref):
        xt = x_ref[...]
        h2 = (xt * _rsqrt_ms(xt) * vec_ref[0:1, :] * vec_ref[1:2, :] + vec_ref[2:3, :]).astype(BF16)
        h2_ref[...] = h2
        f = jnp.zeros((ts, D_MODEL), F32)
        for c in range(D_FF // FF_CHUNK):
            cols = slice(c * FF_CHUNK, (c + 1) * FF_CHUNK)
            g = _dot(h2, wg_ref[cols, :], NT)
            up = _dot(h2, wu_ref[cols, :], NT)
            act = ((g * jax.nn.sigmoid(g)) * up).astype(BF16)
            gt_ref[:, cols] = g.astype(BF16)
            up_ref[:, cols] = up.astype(BF16)
            act_ref[:, cols] = act
            f = f + _dot(act, wd_ref[cols, :])
        f_ref[...] = f
        x2_ref[...] = xt + vec_ref[3:4, :] * (f * _rsqrt_ms(f) * vec_ref[4:5, :])

    row = lambda w: pl.BlockSpec((ts, w), lambda i: (i, 0))
    wide = jax.ShapeDtypeStruct((s, D_FF), BF16)
    return pl.pallas_call(
        body, name=name, grid=(s // ts,),
        in_specs=[row(D_MODEL), _resident((8, D_MODEL)), _resident((D_FF, D_MODEL)), _resident((D_FF, D_MODEL)),
                  _resident((D_FF, D_MODEL))],
        out_specs=[row(D_MODEL), row(D_MODEL), row(D_FF), row(D_FF), row(D_FF), row(D_MODEL)],
        out_shape=[jax.ShapeDtypeStruct((s, D_MODEL), F32), jax.ShapeDtypeStruct((s, D_MODEL), F32), wide, wide, wide,
                   jax.ShapeDtypeStruct((s, D_MODEL), BF16)],
        compiler_params=_cparams(1),
    )(x1, vec, w_gate_t, w_up_t, w_down)


def _loss_grad(y, target, name):
    s = y.shape[0]
    ts = _seq_tile(s)

    def body(y_ref, t_ref, dy_ref, loss_ref):
        @pl.when(pl.program_id(0) == 0)
        def _():
            loss_ref[...] = jnp.zeros_like(loss_ref)

        diff = y_ref[...] - t_ref[...]
        dy_ref[...] = diff / D_MODEL
        loss_ref[...] += 0.5 * jnp.sum(jnp.mean(diff * diff, axis=-1, keepdims=True))

    row = pl.BlockSpec((ts, D_MODEL), lambda i: (i, 0))
    return pl.pallas_call(
        body, name=name, grid=(s // ts,),
        in_specs=[row, row], out_specs=[row, pl.BlockSpec((8, LANES), lambda i: (0, 0))],
        out_shape=[jax.ShapeDtypeStruct((s, D_MODEL), F32), jax.ShapeDtypeStruct((8, LANES), F32)],
        compiler_params=_cparams(1),
    )(y, target)


def _norm_bwd(dy_hat, x_hat, r):
    return r * (dy_hat - x_hat * jnp.mean(dy_hat * x_hat, axis=-1, keepdims=True))


def _accumulate_rows(ref, rows):
    for j, val in enumerate(rows):
        ref[j:j + 1, :] += val


def _ffn_bwd(dx2, x1, f, gt, up, vec, w_gate_t, w_up_t, w_down, name):
    s = x1.shape[0]
    ts = _seq_tile(s) // 2

    def body(dx2_ref, x1_ref, f_ref, gt_ref, up_ref, vec_ref, wg_ref, wu_ref, wd_ref,
             dx1_ref, dgt_ref, dup_ref, df_ref, cs_ref):
        @pl.when(pl.program_id(0) == 0)
        def _():
            cs_ref[...] = jnp.zeros_like(cs_ref)

        dx2_t, f_t = dx2_ref[...], f_ref[...]
        g_pre, one_scale, gate, g_post = vec_ref[0:1, :], vec_ref[1:2, :], vec_ref[3:4, :], vec_ref[4:5, :]
        rf = _rsqrt_ms(f_t)
        f_hat = f_t * rf
        df = _norm_bwd(dx2_t * gate * g_post, f_hat, rf).astype(BF16)
        df_ref[...] = df
        dh2 = jnp.zeros((ts, D_MODEL), F32)
        for c in range(D_FF // FF_CHUNK):
            cols = slice(c * FF_CHUNK, (c + 1) * FF_CHUNK)
            dact = _dot(df, wd_ref[cols, :], NT)
            g, u_t = gt_ref[:, cols].astype(F32), up_ref[:, cols].astype(F32)
            sg = jax.nn.sigmoid(g)
            dg = (dact * u_t * (sg * (1.0 + g * (1.0 - sg)))).astype(BF16)
            du = (dact * (g * sg)).astype(BF16)
            dgt_ref[:, cols] = dg
            dup_ref[:, cols] = du
            dh2 = dh2 + _dot(dg, wg_ref[cols, :]) + _dot(du, wu_ref[cols, :])
        x1_t = x1_ref[...]
        r1 = _rsqrt_ms(x1_t)
        x_hat = x1_t * r1
        dx1_ref[...] = dx2_t + _norm_bwd(dh2 * g_pre * one_scale, x_hat, r1)
        _accumulate_rows(cs_ref, [_colsum(dh2), _colsum(dh2 * (x_hat * g_pre)), _colsum(dh2 * one_scale * x_hat),
                                  _colsum(dx2_t * (f_hat * g_post)), _colsum(dx2_t * gate * f_hat)])

    row = lambda w: pl.BlockSpec((ts, w), lambda i: (i, 0))
    wide = jax.ShapeDtypeStruct((s, D_FF), BF16)
    return pl.pallas_call(
        body, name=name, grid=(s // ts,),
        in_specs=[row(D_MODEL), row(D_MODEL), row(D_MODEL), row(D_FF), row(D_FF), _resident((8, D_MODEL)),
                  _resident((D_FF, D_MODEL)), _resident((D_FF, D_MODEL)), _resident((D_FF, D_MODEL))],
        out_specs=[row(D_MODEL), row(D_FF), row(D_FF), row(D_MODEL), pl.BlockSpec((8, D_MODEL), lambda i: (0, 0))],
        out_shape=[jax.ShapeDtypeStruct((s, D_MODEL), F32), wide, wide, jax.ShapeDtypeStruct((s, D_MODEL), BF16),
                   jax.ShapeDtypeStruct((8, D_MODEL), F32)],
        compiler_params=_cparams(1),
    )(dx2, x1, f, gt, up, vec, w_gate_t, w_up_t, w_down)


def _weight_grad(a, b, name):
    s, m = a.shape
    n = b.shape[1]
    tk = min(s, 2 * SEQ_TILE)
    steps = s // tk

    def body(a_ref, b_ref, o_ref, acc_ref):
        i = pl.program_id(0)

        @pl.when(i == 0)
        def _():
            acc_ref[...] = jnp.zeros_like(acc_ref)

        b_t = b_ref[...]
        for c in range(m // FF_CHUNK):
            rows = slice(c * FF_CHUNK, (c + 1) * FF_CHUNK)
            acc_ref[rows, :] += _dot(a_ref[:, rows], b_t, TN)

        @pl.when(i == steps - 1)
        def _():
            o_ref[...] = acc_ref[...].astype(BF16)

    return pl.pallas_call(
        body, name=name, grid=(steps,),
        in_specs=[pl.BlockSpec((tk, m), lambda i: (i, 0)), pl.BlockSpec((tk, n), lambda i: (i, 0))],
        out_specs=_resident((m, n)),
        out_shape=jax.ShapeDtypeStruct((m, n), BF16),
        scratch_shapes=[pltpu.VMEM((m, n), F32)],
        compiler_params=_cparams(1),
    )(a, b)


def _out_bwd(dx1, mix, ao, u, pool_w, psc, w_out, vec, name):
    s = dx1.shape[0]
    ts = _seq_tile(s)
    nt = s // ts
    hb = ts // POOL_HALO
    ng = len(POOL_WINDOWS)

    def body(dx1_ref, mix_ref, ao_ref, u_ref, uh_ref, pw_ref, psc_ref, w_ref, vec_ref,
             do_ref, du_ref, cat_ref, dmix_ref, cs_ref, dpw_ref, dpsc_ref, carry_ref):
        i = pl.program_id(0)
        tile = nt - 1 - i

        @pl.when(i == 0)
        def _():
            cs_ref[...] = jnp.zeros_like(cs_ref)
            dpw_ref[...] = jnp.zeros_like(dpw_ref)
            dpsc_ref[...] = jnp.zeros_like(dpsc_ref)
            carry_ref[...] = jnp.zeros_like(carry_ref)

        dx1_t, mix_t = dx1_ref[...], mix_ref[...]
        gate, g_post = vec_ref[0:1, :], vec_ref[1:2, :]
        rm = _rsqrt_ms(mix_t)
        m_hat = mix_t * rm
        dmix = _norm_bwd(dx1_t * gate * g_post, m_hat, rm).astype(BF16)
        dmix_ref[...] = dmix
        _accumulate_rows(cs_ref, [_colsum(dx1_t * (m_hat * g_post)), _colsum(dx1_t * gate * m_hat)])
        do_ref[...] = _dot(dmix, w_ref[0:ATTN_WIDTH, :], NT).astype(BF16)
        cat_ref[:, 0:ATTN_WIDTH] = ao_ref[...]

        u_t = u_ref[...]
        halo = jnp.where(tile > 0, uh_ref[...], 0.0)
        counts = _pool_counts(tile, ts)
        for gi in range(ng):
            cols = slice(gi * POOL_GROUP_WIDTH, (gi + 1) * POOL_GROUP_WIDTH)
            wide_cols = slice(ATTN_WIDTH + gi * POOL_GROUP_WIDTH, ATTN_WIDTH + (gi + 1) * POOL_GROUP_WIDTH)
            scale = psc_ref[0:1, cols]
            pooled = _pooled(u_t, halo, counts, gi).astype(BF16)
            og = _dot(pooled, pw_ref[gi])
            cat_ref[:, wide_cols] = (og * scale).astype(BF16)
            d_out = _dot(dmix, w_ref[wide_cols, :], NT)
            dpsc_ref[0:1, cols] += _colsum(d_out * og)
            d_og = (d_out * scale).astype(BF16)
            dpw_ref[gi] += _dot(pooled, d_og, TN)
            d_pooled = _dot(d_og, pw_ref[gi], NT)
            spread = d_pooled / counts[gi]
            acc = jnp.concatenate([spread, carry_ref[:, cols]], axis=0)
            shift = 1
            while shift < POOL_WINDOWS[gi]:
                acc = acc + pltpu.roll(acc, ts + POOL_HALO - shift, 0)
                shift *= 2
            du_ref[:, cols] = acc[0:ts, :] - d_pooled
            carry_ref[:, cols] = spread[0:POOL_HALO, :]

    row = lambda w: pl.BlockSpec((ts, w), lambda i: (nt - 1 - i, 0))
    fixed = lambda shape: pl.BlockSpec(shape, lambda i: (0,) * len(shape))
    return pl.pallas_call(
        body, name=name, grid=(nt,),
        in_specs=[row(D_MODEL), row(D_MODEL), row(ATTN_WIDTH), row(POOL_WIDTH),
                  pl.BlockSpec((POOL_HALO, POOL_WIDTH), lambda i: (jnp.maximum((nt - 1 - i) * hb - 1, 0), 0)),
                  _resident(pool_w.shape), _resident((8, POOL_WIDTH)), _resident((D_MODEL, D_MODEL)),
                  _resident((8, D_MODEL))],
        out_specs=[row(ATTN_WIDTH), row(POOL_WIDTH), row(D_MODEL), row(D_MODEL), fixed((8, D_MODEL)),
                   fixed(pool_w.shape), fixed((8, POOL_WIDTH))],
        out_shape=[jax.ShapeDtypeStruct((s, ATTN_WIDTH), BF16), jax.ShapeDtypeStruct((s, POOL_WIDTH), F32),
                   jax.ShapeDtypeStruct((s, D_MODEL), BF16), jax.ShapeDtypeStruct((s, D_MODEL), BF16),
                   jax.ShapeDtypeStruct((8, D_MODEL), F32), jax.ShapeDtypeStruct(pool_w.shape, F32),
                   jax.ShapeDtypeStruct((8, POOL_WIDTH), F32)],
        scratch_shapes=[pltpu.VMEM((POOL_HALO, POOL_WIDTH), F32)],
        compiler_params=_cparams(1),
    )(dx1, mix, ao, u, u, pool_w, psc, w_out, vec)


def _attn_bwd(sinks, q, do, lse, k, v, name):
    s = q.shape[0]
    nb = s // BLOCK
    cur = lambda w: pl.BlockSpec((BLOCK, w), lambda i: (i, 0))
    prev = lambda w: pl.BlockSpec((BLOCK, w), lambda i: (jnp.maximum(i - 1, 0), 0))

    def body(sink_ref, q_ref, do_ref, l_ref, kc_ref, kp_ref, vc_ref, vp_ref,
             dq_ref, dkc_ref, dkp_ref, dvc_ref, dvp_ref, dsink_ref):
        @pl.when(pl.program_id(0) == 0)
        def _():
            dsink_ref[...] = jnp.zeros_like(dsink_ref)

        valid = _band_mask(pl.program_id(0) == 0)
        for kh in range(N_KV_HEADS):
            kcat, vcat = _kv_cat(kp_ref, kc_ref, kh), _kv_cat(vp_ref, vc_ref, kh)
            dk = jnp.zeros((2 * BLOCK, HEAD_DIM), F32)
            dv = jnp.zeros((2 * BLOCK, HEAD_DIM), F32)
            for g in range(GROUP):
                h = kh * GROUP + g
                qh, doh, lse_h = _head(q_ref, h), _head(do_ref, h), l_ref[:, h:h + 1]
                p = jnp.where(valid, jnp.exp(_dot(qh, kcat, NT) - lse_h), 0.0)
                dp = _dot(doh, vcat, NT)
                delta = jnp.sum(p * dp, axis=1, keepdims=True)
                dsink_ref[h:h + 1, :] += -jnp.sum(jnp.exp(sink_ref[h] - lse_h) * delta)
                ds = (p * (dp - delta)).astype(BF16)
                dq_ref[:, h * HEAD_DIM:(h + 1) * HEAD_DIM] = _dot(ds, kcat) * Q_SCALE
                dk = dk + _dot(ds, qh, TN)
                dv = dv + _dot(p.astype(BF16), doh, TN)
            hcols = slice(kh * HEAD_DIM, (kh + 1) * HEAD_DIM)
            dkp_ref[:, hcols] = dk[0:BLOCK, :]
            dkc_ref[:, hcols] = dk[BLOCK:, :]
            dvp_ref[:, hcols] = dv[0:BLOCK, :]
            dvc_ref[:, hcols] = dv[BLOCK:, :]

    kv = jax.ShapeDtypeStruct((s, KV_WIDTH), F32)
    return pl.pallas_call(
        body, name=name, grid=(nb,),
        in_specs=[pl.BlockSpec(memory_space=pltpu.SMEM), cur(ATTN_WIDTH), cur(ATTN_WIDTH), cur(N_HEADS),
                  cur(KV_WIDTH), prev(KV_WIDTH), cur(KV_WIDTH), prev(KV_WIDTH)],
        out_specs=[cur(ATTN_WIDTH), cur(KV_WIDTH), cur(KV_WIDTH), cur(KV_WIDTH), cur(KV_WIDTH),
                   pl.BlockSpec((8, LANES), lambda i: (0, 0))],
        out_shape=[jax.ShapeDtypeStruct((s, ATTN_WIDTH), F32), kv, kv, kv, kv, jax.ShapeDtypeStruct((8, LANES), F32)],
        compiler_params=_cparams(1),
    )(sinks, q, do, lse, k, k, v, v)


def _in_bwd(dq, dkc, dkp, dvc, dvp, du, x, dx1, vec, w_in_t, rot, name):
    s = x.shape[0]
    ts = _seq_tile(s)
    nt = s // ts
    bpt = ts // BLOCK

    def body(dq_ref, dkc_ref, dkp_ref, dkn_ref, dvc_ref, dvp_ref, dvn_ref, du_ref, x_ref, dx1_ref, vec_ref, w_ref,
             cos_ref, lo_ref, hi_ref, dx_ref, dproj_ref, h_ref, cs_ref, db_ref):
        i = pl.program_id(0)

        @pl.when(i == 0)
        def _():
            cs_ref[...] = jnp.zeros_like(cs_ref)
            db_ref[...] = jnp.zeros_like(db_ref)

        cos, lo, hi = cos_ref[...], lo_ref[...], hi_ref[...]

        def with_next_block(cur_ref, prev_ref, next_ref):
            nxt = jnp.where(i < nt - 1, next_ref[...], 0.0)
            later = nxt if bpt == 1 else jnp.concatenate([prev_ref[BLOCK:, :], nxt], axis=0)
            return cur_ref[...] + later

        pieces = [_rotate_bwd(dq_ref[:, j * LANES:(j + 1) * LANES], cos, lo, hi) for j in range(ATTN_WIDTH // LANES)]
        pieces.append(_rotate_bwd(with_next_block(dkc_ref, dkp_ref, dkn_ref), cos, lo, hi))
        pieces.append(with_next_block(dvc_ref, dvp_ref, dvn_ref))
        pieces.append(du_ref[...])
        dproj = jnp.concatenate(pieces, axis=1)
        db_ref[0:1, :] += _colsum(dproj)
        dproj_b = dproj.astype(BF16)
        dproj_ref[...] = dproj_b
        dh = _dot(dproj_b, w_ref[...])

        xt = x_ref[...]
        g_pre, one_scale = vec_ref[0:1, :], vec_ref[1:2, :]
        r = _rsqrt_ms(xt)
        x_hat = xt * r
        h_ref[...] = (x_hat * g_pre * one_scale + vec_ref[2:3, :]).astype(BF16)
        dx_ref[...] = dx1_ref[...] + _norm_bwd(dh * g_pre * one_scale, x_hat, r)
        _accumulate_rows(cs_ref, [_colsum(dh), _colsum(dh * (x_hat * g_pre)), _colsum(dh * one_scale * x_hat)])

    row = lambda w: pl.BlockSpec((ts, w), lambda i: (i, 0))
    nxt = pl.BlockSpec((BLOCK, KV_WIDTH), lambda i: (jnp.minimum((i + 1) * bpt, s // BLOCK - 1), 0))
    fixed = lambda shape: pl.BlockSpec(shape, lambda i: (0,) * len(shape))
    return pl.pallas_call(
        body, name=name, grid=(nt,),
        in_specs=[row(ATTN_WIDTH), row(KV_WIDTH), row(KV_WIDTH), nxt, row(KV_WIDTH), row(KV_WIDTH), nxt,
                  row(POOL_WIDTH), row(D_MODEL), row(D_MODEL), _resident((8, D_MODEL)), _resident((IN_WIDTH, D_MODEL)),
                  row(LANES), row(LANES), row(LANES)],
        out_specs=[row(D_MODEL), row(IN_WIDTH), row(D_MODEL), fixed((8, D_MODEL)), fixed((8, IN_WIDTH))],
        out_shape=[jax.ShapeDtypeStruct((s, D_MODEL), F32), jax.ShapeDtypeStruct((s, IN_WIDTH), BF16),
                   jax.ShapeDtypeStruct((s, D_MODEL), BF16), jax.ShapeDtypeStruct((8, D_MODEL), F32),
                   jax.ShapeDtypeStruct((8, IN_WIDTH), F32)],
        compiler_params=_cparams(1),
    )(dq, dkc, dkp, dkp, dvc, dvp, dvp, du, x, dx1, vec, w_in_t, *rot)


def _row_tile(rows):
    for t in (512, 352, 256, 176, 160, 128, 64, 32, 16, 8):
        if rows % t == 0:
            return t
    raise ValueError(f"no row tile for {rows} rows")


def _sum_parts(own, others, name):
    r, c = own.shape
    n_other = others.shape[0]
    tr = _row_tile(r)

    def body(own_ref, oth_ref, o_ref):
        acc = own_ref[...].astype(F32)
        for k in range(n_other):
            acc = acc + oth_ref[k].astype(F32)
        o_ref[...] = acc

    return pl.pallas_call(
        body, name=name, grid=(r // tr,),
        in_specs=[pl.BlockSpec((tr, c), lambda i: (i, 0)), pl.BlockSpec((n_other, tr, c), lambda i: (0, i, 0))],
        out_specs=pl.BlockSpec((tr, c), lambda i: (i, 0)),
        out_shape=jax.ShapeDtypeStruct((r, c), F32),
        compiler_params=_cparams(1),
    )(own, others)


def _adamw(w, g, m, v, name):
    r, c = w.shape
    tr = _row_tile(r)

    def body(w_ref, g_ref, m_ref, v_ref, d_ref, nm_ref, nv_ref):
        g_t = g_ref[...]
        m_new = ADAM_B1 * m_ref[...] + (1.0 - ADAM_B1) * g_t
        v_new = ADAM_B2 * v_ref[...] + (1.0 - ADAM_B2) * (g_t * g_t)
        m_hat = m_new / (1.0 - ADAM_B1 ** ADAM_STEP)
        v_hat = v_new / (1.0 - ADAM_B2 ** ADAM_STEP)
        d_ref[...] = -ADAM_LR * (m_hat / (jnp.sqrt(v_hat) + ADAM_EPS) + ADAM_WD * w_ref[...])
        nm_ref[...] = m_new
        nv_ref[...] = v_new

    spec = pl.BlockSpec((tr, c), lambda i: (i, 0))
    shape = jax.ShapeDtypeStruct((r, c), F32)
    return pl.pallas_call(
        body, name=name, grid=(r // tr,), in_specs=[spec] * 4, out_specs=[spec] * 3, out_shape=[shape] * 3,
        compiler_params=_cparams(1),
    )(w, g, m, v)


SMALL_ROWS = 160


def _pack_small(dmod, gains, b_in, pool_scale, sinks, pool_w):
    pad = lambda a, width: jnp.pad(a, ((0, 0), (0, width - a.shape[1])))
    parts = [dmod.reshape(DEPTH * N_MOD, D_MODEL), gains.reshape(4 * DEPTH, D_MODEL),
             pad(b_in, 2 * D_MODEL).reshape(2 * DEPTH, D_MODEL), pool_scale.reshape(1, D_MODEL),
             pad(sinks.reshape(1, DEPTH * N_HEADS), D_MODEL), pool_w.reshape(-1, D_MODEL)]
    packed = jnp.concatenate(parts, axis=0)
    return jnp.pad(packed, ((0, SMALL_ROWS - packed.shape[0]), (0, 0)))


def _unpack_small(p):
    r = DEPTH * N_MOD
    dmod = p[0:r].reshape(DEPTH, N_MOD * D_MODEL)
    gains = p[r:r + 4 * DEPTH].reshape(4, DEPTH, D_MODEL)
    r += 4 * DEPTH
    b_in = p[r:r + 2 * DEPTH].reshape(DEPTH, 2 * D_MODEL)[:, :IN_WIDTH]
    r += 2 * DEPTH
    pool_scale = p[r].reshape(DEPTH, POOL_WIDTH)
    sinks = p[r + 1, :DEPTH * N_HEADS].reshape(DEPTH, N_HEADS)
    pool_w = p[r + 2:r + 2 + 128].reshape(DEPTH, len(POOL_WINDOWS), POOL_GROUP_WIDTH, POOL_GROUP_WIDTH)
    return dmod, gains, b_in, pool_scale, sinks, pool_w


def _rotary_tables(positions):
    inv_freq = ROPE_THETA ** (-jnp.arange(0, ROT_DIM, 2, dtype=F32) / ROT_DIM)
    ang = positions.reshape(-1).astype(F32)[:, None] * inv_freq
    cos, sin = jnp.cos(ang), jnp.sin(ang)
    s = ang.shape[0]
    rest = HEAD_DIM - ROT_DIM
    head = lambda a, b, fill: jnp.concatenate([a, b, jnp.full((s, rest), fill, F32)], axis=1)
    zero = jnp.zeros_like(sin)
    pair = lambda t: jnp.concatenate([t, t], axis=1)
    return pair(head(cos, cos, 1.0)), pair(head(-sin, zero, 0.0)), pair(head(zero, sin, 0.0))


def kernel(x, c, positions, ada_w, ada_b, w_in, b_in, sinks, pool_w, pool_scale, w_out, w_gate, w_up, w_down, g_pre_mix, g_post_mix, g_pre_ffn, g_post_ffn, loss_target, m_ada_w, m_ada_b, m_w_in, m_b_in, m_sinks, m_pool_w, m_pool_scale, m_w_out, m_w_gate, m_w_up, m_w_down, m_g_pre_mix, m_g_post_mix, m_g_pre_ffn, m_g_post_ffn, v_ada_w, v_ada_b, v_w_in, v_b_in, v_sinks, v_pool_w, v_pool_scale, v_w_out, v_w_gate, v_w_up, v_w_down, v_g_pre_mix, v_g_post_mix, v_g_pre_ffn, v_g_post_ffn):
    me = _index(_mesh_place())
    x0 = x[0]
    target = loss_target[0]
    rot = _rotary_tables(positions)
    ada_cols = ada_w.shape[2]

    c_all = _allgather_vmem(jnp.broadcast_to(c, (8, D_MODEL)), "gather_c")[:, 0, :]
    mod_part, c_act = _mod_fwd(c_all, ada_w)
    mod_all = _allgather_vmem(mod_part.reshape(DEPTH * N_DEV, ada_cols), "gather_mod")
    mod_mine = lax.dynamic_slice_in_dim(mod_all.reshape(N_DEV, DEPTH, N_DEV, ada_cols), me, 1, axis=2)[:, :, 0, :]
    mod = jnp.transpose(mod_mine, (1, 0, 2)).reshape(DEPTH, N_MOD * D_MODEL) + ada_b
    mod = mod.reshape(DEPTH, N_MOD, D_MODEL)

    shards = []
    for l in range(DEPTH):
        shards += [w_in[l].T.astype(BF16), w_out[l].astype(BF16), w_gate[l].T.astype(BF16), w_up[l].T.astype(BF16),
                   w_down[l].astype(BF16)]
    full = [g.reshape(-1, D_MODEL) for g in _allgather_hbm(shards, "gather_weights")]
    pool_w_b = pool_w.astype(BF16)

    saved = []
    xl = x0
    for l in range(DEPTH):
        w_in_t, w_out_f, w_gate_t, w_up_t, w_down_f = full[5 * l:5 * l + 5]
        vec_in = _rows(g_pre_mix[l], 1.0 + mod[l, 1], mod[l, 0], width=D_MODEL)
        vec_out = _rows(mod[l, 2], g_post_mix[l], width=D_MODEL)
        vec_ffn = _rows(g_pre_ffn[l], 1.0 + mod[l, 4], mod[l, 3], mod[l, 5], g_post_ffn[l], width=D_MODEL)
        psc = _rows(pool_scale[l], width=POOL_WIDTH)
        q, k, v, u = _fwd_in(xl, vec_in, w_in_t, _rows(b_in[l], width=IN_WIDTH), rot, f"fwd_in_{l}")
        ao, lse = _attn_fwd(sinks[l], q, k, v, f"attn_fwd_{l}")
        x1, mix = _fwd_out(ao, u, pool_w_b[l], psc, w_out_f, xl, vec_out, f"fwd_out_{l}")
        x2, f, gt, up, act, h2 = _ffn_fwd(x1, vec_ffn, w_gate_t, w_up_t, w_down_f, f"ffn_fwd_{l}")
        saved.append((xl, q, k, v, u, ao, lse, x1, mix, f, gt, up, act, h2, vec_in, vec_out, vec_ffn, psc))
        xl = x2

    dx, loss_part = _loss_grad(xl, target, "loss_grad")
    loss = lax.psum(loss_part[0, 0], ("x", "y", "c"))

    weight_grads = [None] * (5 * DEPTH)
    dmod = [None] * DEPTH
    d_gains = [[None] * DEPTH for _ in range(4)]
    d_b_in, d_pool_scale, d_sinks, d_pool_w = [None] * DEPTH, [None] * DEPTH, [None] * DEPTH, [None] * DEPTH
    for l in reversed(range(DEPTH)):
        w_in_t, w_out_f, w_gate_t, w_up_t, w_down_f = full[5 * l:5 * l + 5]
        xin, q, k, v, u, ao, lse, x1, mix, f, gt, up, act, h2, vec_in, vec_out, vec_ffn, psc = saved[l]
        dx1, dgt, dup, df, cs_f = _ffn_bwd(dx, x1, f, gt, up, vec_ffn, w_gate_t, w_up_t, w_down_f, f"ffn_bwd_{l}")
        weight_grads[5 * l + 2] = _weight_grad(dgt, h2, f"grad_w_gate_{l}")
        weight_grads[5 * l + 3] = _weight_grad(dup, h2, f"grad_w_up_{l}")
        weight_grads[5 * l + 4] = _weight_grad(act, df, f"grad_w_down_{l}")
        do, du, cat, dmix, cs_o, dpw, dpsc = _out_bwd(dx1, mix, ao, u, pool_w_b[l], psc, w_out_f, vec_out, f"out_bwd_{l}")
        weight_grads[5 * l + 1] = _weight_grad(cat, dmix, f"grad_w_out_{l}")
        dq, dkc, dkp, dvc, dvp, dsk = _attn_bwd(sinks[l], q, do, lse, k, v, f"attn_bwd_{l}")
        dx, dproj, h, cs_i, db = _in_bwd(dq, dkc, dkp, dvc, dvp, du, xin, dx1, vec_in, w_in_t, rot, f"in_bwd_{l}")
        weight_grads[5 * l + 0] = _weight_grad(dproj, h, f"grad_w_in_{l}")
        dmod[l] = jnp.concatenate([cs_i[0], cs_i[1], cs_o[0], cs_f[0], cs_f[1], cs_f[3]])
        d_gains[0][l], d_gains[1][l], d_gains[2][l], d_gains[3][l] = cs_i[2], cs_o[1], cs_f[2], cs_f[4]
        d_b_in[l], d_pool_scale[l], d_sinks[l], d_pool_w[l] = db[0], dpsc[0], dsk[:, 0], dpw
    grad_x = dx[None]

    small = _pack_small(jnp.stack(dmod), jnp.stack([jnp.stack(g) for g in d_gains]), jnp.stack(d_b_in),
                        jnp.stack(d_pool_scale), jnp.stack(d_sinks), jnp.stack(d_pool_w))
    small_all = _allgather_vmem(small, "gather_small")
    small_sum = _sum_parts(small_all[0], small_all[1:], "sum_small")
    g_ada_b, g_gains, g_b_in, g_pool_scale, g_sinks, g_pool_w = _unpack_small(small_sum)
    gains = jnp.stack([g_pre_mix, g_post_mix, g_pre_ffn, g_post_ffn])
    m_gains = jnp.stack([m_g_pre_mix, m_g_post_mix, m_g_pre_ffn, m_g_post_ffn])
    v_gains = jnp.stack([v_g_pre_mix, v_g_post_mix, v_g_pre_ffn, v_g_post_ffn])
    small_step = _adamw(_pack_small(ada_b, gains, b_in, pool_scale, sinks, pool_w), small_sum,
                        _pack_small(m_ada_b, m_gains, m_b_in, m_pool_scale, m_sinks, m_pool_w),
                        _pack_small(v_ada_b, v_gains, v_b_in, v_pool_scale, v_sinks, v_pool_w), "adamw_small")
    small_out = [_unpack_small(p) for p in small_step]

    dmod_all = small_all[:, 0:DEPTH * N_MOD, :].reshape(N_DEV, DEPTH, N_MOD * D_MODEL)
    dmod_cols = lax.dynamic_slice_in_dim(dmod_all.reshape(N_DEV, DEPTH, N_DEV, ada_cols), me, 1, axis=2)[:, :, 0, :]
    g_ada_w = _ada_grad(c_act, jnp.transpose(dmod_cols, (1, 0, 2)))

    stacks = [g.reshape(N_DEV, -1, D_MODEL) for g in weight_grads]
    others = _scatter_hbm(stacks, "scatter_grads")
    summed = []
    for a, (stack, oth) in enumerate(zip(stacks, others)):
        own = lax.dynamic_index_in_dim(stack, me, axis=0, keepdims=False)
        summed.append(_sum_parts(own, oth, f"sum_grads_{a}"))
    per_kind = [jnp.stack([summed[5 * l + j] for l in range(DEPTH)]) for j in range(5)]
    g_w_in, g_w_out, g_w_gate, g_w_up, g_w_down = per_kind
    g_w_in, g_w_gate, g_w_up = (jnp.transpose(g, (0, 2, 1)) for g in (g_w_in, g_w_gate, g_w_up))

    def step(w, g, m, v, name):
        flat = lambda a: a.reshape(-1, a.shape[-1])
        return [o.reshape(w.shape) for o in _adamw(flat(w), flat(g), flat(m), flat(v), name)]

    big = {
        "ada_w": (g_ada_w, step(ada_w, g_ada_w, m_ada_w, v_ada_w, "adamw_ada_w")),
        "w_in": (g_w_in, step(w_in, g_w_in, m_w_in, v_w_in, "adamw_w_in")),
        "w_out": (g_w_out, step(w_out, g_w_out, m_w_out, v_w_out, "adamw_w_out")),
        "w_gate": (g_w_gate, step(w_gate, g_w_gate, m_w_gate, v_w_gate, "adamw_w_gate")),
        "w_up": (g_w_up, step(w_up, g_w_up, m_w_up, v_w_up, "adamw_w_up")),
        "w_down": (g_w_down, step(w_down, g_w_down, m_w_down, v_w_down, "adamw_w_down")),
    }

    def ordered(pick_big, pick_small):
        ada_b_, gains_, b_in_, pool_scale_, sinks_, pool_w_ = pick_small
        return [pick_big("ada_w"), ada_b_, pick_big("w_in"), b_in_, sinks_, pool_w_, pool_scale_, pick_big("w_out"),
                pick_big("w_gate"), pick_big("w_up"), pick_big("w_down"), gains_[0], gains_[1], gains_[2], gains_[3]]

    grads = ordered(lambda n: big[n][0], (g_ada_b, g_gains, g_b_in, g_pool_scale, g_sinks, g_pool_w))
    deltas = ordered(lambda n: big[n][1][0], small_out[0])
    new_m = ordered(lambda n: big[n][1][1], small_out[1])
    new_v = ordered(lambda n: big[n][1][2], small_out[2])
    return (loss, grad_x, *grads, *deltas, *new_m, *new_v)
```

```python
import jax
import jax.numpy as jnp
from jax import lax
from jax.experimental import pallas as pl
from jax.experimental.pallas import tpu as pltpu

F32 = jnp.float32
BF16 = jnp.bfloat16

N_DEV = 8
DEPTH = 2
D_MODEL = 1024
HEAD_DIM = 64
N_HEADS = 8
N_KV_HEADS = 2
GROUP = N_HEADS // N_KV_HEADS
ATTN_WIDTH = N_HEADS * HEAD_DIM
KV_WIDTH = N_KV_HEADS * HEAD_DIM
POOL_WIDTH = 512
POOL_WINDOWS = (2, 4, 8, 16)
POOL_GROUP_WIDTH = 128
POOL_HALO = 16
IN_WIDTH = ATTN_WIDTH + 2 * KV_WIDTH + POOL_WIDTH
D_FF = 2816
N_MOD = 6
BLOCK = 128
ROT_DIM = 16
ROPE_THETA = 500000.0
EPS = 1e-6
NEG_INF = -1e30
Q_SCALE = HEAD_DIM ** -0.5

ADAM_LR = 0.001
ADAM_B1 = 0.9
ADAM_B2 = 0.999
ADAM_EPS = 1e-08
ADAM_WD = 0.01
ADAM_STEP = 10

LANES = 128
SEQ_TILE = 512
FF_CHUNK = 256
VMEM_LIMIT = 56 * 1024 * 1024
MESH = pl.DeviceIdType.MESH

NT = (((1,), (1,)), ((), ()))
TN = (((0,), (0,)), ((), ()))


def _dot(a, b, dims=None):
    if dims is None:
        return jnp.dot(a, b, preferred_element_type=F32)
    return lax.dot_general(a, b, dims, preferred_element_type=F32)


def _cparams(n_axes):
    return pltpu.CompilerParams(dimension_semantics=("arbitrary",) * n_axes, vmem_limit_bytes=VMEM_LIMIT)


def _resident(shape):
    zeros = (0,) * len(shape)
    return pl.BlockSpec(shape, lambda *_: zeros, pipeline_mode=pl.Buffered(1))


def _rows(*vectors, width):
    rows = [jnp.reshape(v, (1, width)).astype(F32) for v in vectors]
    rows.append(jnp.zeros((8 - len(rows), width), F32))
    return jnp.concatenate(rows, axis=0)


def _rsqrt_ms(x):
    return lax.rsqrt(jnp.mean(x * x, axis=-1, keepdims=True) + EPS)


def _colsum(x):
    return jnp.sum(x, axis=0, keepdims=True)


def _seq_tile(s):
    return min(s, SEQ_TILE)


def _mesh_place():
    x, y, c = lax.axis_index("x"), lax.axis_index("y"), lax.axis_index("c")
    return x, y, c


def _flip(place, k):
    x, y, c = place
    return (1 - x if k & 4 else x, 1 - y if k & 2 else y, 1 - c if k & 1 else c)


def _index(place):
    x, y, c = place
    return 4 * x + 2 * y + c


def _allgather_vmem(block, name):
    r, c = block.shape

    def body(x_ref, out_ref, send_sems, recv_sems, local_sem):
        me = _mesh_place()
        mine = pltpu.make_async_copy(x_ref, out_ref.at[_index(me)], local_sem)
        mine.start()

        def copy(k):
            return pltpu.make_async_remote_copy(
                src_ref=x_ref, dst_ref=out_ref.at[_index(me)], send_sem=send_sems.at[k - 1], recv_sem=recv_sems.at[k - 1],
                device_id=_flip(me, k), device_id_type=MESH)

        def arrival(k):
            return pltpu.make_async_remote_copy(
                src_ref=x_ref, dst_ref=out_ref.at[_index(_flip(me, k))], send_sem=send_sems.at[k - 1],
                recv_sem=recv_sems.at[k - 1], device_id=_flip(me, k), device_id_type=MESH)

        for k in range(1, N_DEV):
            copy(k).start()
        for k in range(1, N_DEV):
            arrival(k).wait_recv()
        for k in range(1, N_DEV):
            copy(k).wait_send()
        mine.wait()

    return pl.pallas_call(
        body, name=name,
        out_shape=jax.ShapeDtypeStruct((N_DEV, r, c), block.dtype),
        in_specs=[pl.BlockSpec(memory_space=pltpu.VMEM)],
        out_specs=pl.BlockSpec(memory_space=pltpu.VMEM),
        scratch_shapes=[pltpu.SemaphoreType.DMA((N_DEV - 1,)), pltpu.SemaphoreType.DMA((N_DEV - 1,)),
                        pltpu.SemaphoreType.DMA],
    )(block)


class _Gather:
    def __init__(self, shards):
        n = len(shards)
        self.operands = list(shards)
        self.out_shape = [jax.ShapeDtypeStruct((N_DEV,) + s.shape, s.dtype) for s in shards]
        self.scratch = [pltpu.SemaphoreType.DMA((N_DEV - 1, n)), pltpu.SemaphoreType.DMA((N_DEV - 1, n)),
                        pltpu.SemaphoreType.DMA((n,))]

    def _copies(self, x_refs, out_refs, sems):
        send_sems, recv_sems, local_sems = sems
        n = len(x_refs)
        x, y, c = _mesh_place()
        me, sibling = (x, y, c), (x, y, 1 - c)
        chips = [(1 - x, y), (x, 1 - y), (1 - x, 1 - y)]

        def copy(k, a, block, to, from_input=False):
            rows = out_refs[a].at[_index(block)]
            return pltpu.make_async_remote_copy(
                src_ref=x_refs[a] if from_input else rows, dst_ref=rows,
                send_sem=send_sems.at[k, a], recv_sem=recv_sems.at[k, a], device_id=to, device_id_type=MESH)

        mine = [pltpu.make_async_copy(x_refs[a], out_refs[a].at[_index(me)], local_sems.at[a]) for a in range(n)]
        first = [copy(0, a, me, sibling, from_input=True) for a in range(n)]
        first += [copy(1 + j, a, me, (*chip, c), from_input=True) for j, chip in enumerate(chips) for a in range(n)]
        over_ici = [copy(1 + j, a, (*chip, c), me) for j, chip in enumerate(chips) for a in range(n)]
        passed = [copy(4 + j, a, (*chip, c), sibling) for j, chip in enumerate(chips) for a in range(n)]
        from_sibling = [copy(0, a, sibling, me) for a in range(n)]
        from_sibling += [copy(4 + j, a, (*chip, 1 - c), me) for j, chip in enumerate(chips) for a in range(n)]
        return mine, first, over_ici, passed, from_sibling

    def begin(self, x_refs, out_refs, sems):
        mine, first, _, _, _ = self._copies(x_refs, out_refs, sems)
        for cp in mine + first:
            cp.start()

    def middle(self, x_refs, out_refs, sems):
        _, _, over_ici, passed, _ = self._copies(x_refs, out_refs, sems)
        for arrived, onward in zip(over_ici, passed):
            arrived.wait_recv()
            onward.start()

    def end(self, x_refs, out_refs, sems):
        mine, first, _, passed, from_sibling = self._copies(x_refs, out_refs, sems)
        for cp in from_sibling:
            cp.wait_recv()
        for cp in first + passed:
            cp.wait_send()
        for cp in mine:
            cp.wait()


class _Scatter:
    def __init__(self, stacks):
        n = len(stacks)
        self.operands = list(stacks)
        self.out_shape = [jax.ShapeDtypeStruct((N_DEV - 1,) + s.shape[1:], s.dtype) for s in stacks]
        self.scratch = [pltpu.SemaphoreType.DMA((N_DEV - 1, n)), pltpu.SemaphoreType.DMA((N_DEV - 1, n))]

    def _copies(self, g_refs, out_refs, sems):
        send_sems, recv_sems = sems
        me = _mesh_place()

        def copy(k, a):
            peer = _flip(me, k)
            return pltpu.make_async_remote_copy(
                src_ref=g_refs[a].at[_index(peer)], dst_ref=out_refs[a].at[k - 1],
                send_sem=send_sems.at[k - 1, a], recv_sem=recv_sems.at[k - 1, a], device_id=peer, device_id_type=MESH)

        return [copy(k, a) for k in range(1, N_DEV) for a in range(len(g_refs))]

    def begin(self, g_refs, out_refs, sems):
        for cp in self._copies(g_refs, out_refs, sems):
            cp.start()

    def middle(self, g_refs, out_refs, sems):
        pass

    def end(self, g_refs, out_refs, sems):
        copies = self._copies(g_refs, out_refs, sems)
        for cp in copies:
            cp.wait_recv()
        for cp in copies:
            cp.wait_send()


def _call(body, args, *, name, grid, in_specs, out_specs, out_shape, scratch_shapes=(), exchanges=()):
    if not exchanges:
        outs = pl.pallas_call(body, name=name, grid=grid, in_specs=in_specs, out_specs=out_specs, out_shape=out_shape,
                              scratch_shapes=list(scratch_shapes), compiler_params=_cparams(len(grid)))(*args)
        return outs, []
    (steps,) = grid
    n_in, n_out, n_scr = len(in_specs), len(out_specs), len(scratch_shapes)
    ex_in = [len(ex.operands) for ex in exchanges]
    ex_out = [len(ex.out_shape) for ex in exchanges]
    ex_scr = [len(ex.scratch) for ex in exchanges]

    def split(refs, counts):
        parts, pos = [], 0
        for cnt in counts:
            parts.append(refs[pos:pos + cnt])
            pos += cnt
        return parts

    def wrapped(*refs):
        ins, xin, outs, xout, scr, xscr = split(refs, [n_in, sum(ex_in), n_out, sum(ex_out), n_scr, sum(ex_scr)])
        bound = list(zip(exchanges, split(xin, ex_in), split(xout, ex_out), split(xscr, ex_scr)))
        step = pl.program_id(0)

        def phase(method, at):
            @pl.when(step == at)
            def _():
                for ex, i_refs, o_refs, sems in bound:
                    getattr(ex, method)(i_refs, o_refs, sems)

        phase("begin", 0)
        phase("middle", (3 * steps) // 4)
        body(*ins, *outs, *scr)
        phase("end", steps - 1)

    any_spec = pl.BlockSpec(memory_space=pl.ANY)
    results = pl.pallas_call(
        wrapped, name=name, grid=grid,
        in_specs=list(in_specs) + [any_spec] * sum(ex_in),
        out_specs=list(out_specs) + [any_spec] * sum(ex_out),
        out_shape=list(out_shape) + [s for ex in exchanges for s in ex.out_shape],
        scratch_shapes=list(scratch_shapes) + [s for ex in exchanges for s in ex.scratch],
        compiler_params=_cparams(1),
    )(*args, *[a for ex in exchanges for a in ex.operands])
    return results[:n_out], split(results[n_out:], ex_out)


def _exchange_alone(exchange, name):
    def body(flag_ref):
        flag_ref[...] = jnp.zeros_like(flag_ref)

    _, (outs,) = _call(body, [], name=name, grid=(1,), in_specs=[], out_specs=[pl.BlockSpec((8, LANES), lambda i: (0, 0))],
                       out_shape=[jax.ShapeDtypeStruct((8, LANES), F32)], exchanges=[exchange])
    return outs


def _mod_fwd(c_all, ada_w):
    cols = ada_w.shape[2]

    def body(c_ref, w_ref, mp_ref, act_ref):
        c = c_ref[...]
        act = c * jax.nn.sigmoid(c)
        act_ref[...] = act
        mp_ref[0] = _dot(act.astype(BF16), w_ref[0].astype(BF16))

    return pl.pallas_call(
        body, name="mod_fwd", grid=(DEPTH,),
        in_specs=[pl.BlockSpec((N_DEV, D_MODEL), lambda l: (0, 0)),
                  pl.BlockSpec((1, D_MODEL, cols), lambda l: (l, 0, 0))],
        out_specs=[pl.BlockSpec((1, N_DEV, cols), lambda l: (l, 0, 0)),
                   pl.BlockSpec((N_DEV, D_MODEL), lambda l: (0, 0))],
        out_shape=[jax.ShapeDtypeStruct((DEPTH, N_DEV, cols), F32), jax.ShapeDtypeStruct((N_DEV, D_MODEL), F32)],
        compiler_params=_cparams(1),
    )(c_all, ada_w)


def _ada_grad(c_act, dmod_cols):
    cols = dmod_cols.shape[2]

    def body(act_ref, dm_ref, g_ref):
        g_ref[0] = _dot(act_ref[...].astype(BF16), dm_ref[0].astype(BF16), TN)

    return pl.pallas_call(
        body, name="ada_grad", grid=(DEPTH,),
        in_specs=[pl.BlockSpec((N_DEV, D_MODEL), lambda l: (0, 0)),
                  pl.BlockSpec((1, N_DEV, cols), lambda l: (l, 0, 0))],
        out_specs=pl.BlockSpec((1, D_MODEL, cols), lambda l: (l, 0, 0)),
        out_shape=jax.ShapeDtypeStruct((DEPTH, D_MODEL, cols), F32),
        compiler_params=_cparams(1),
    )(c_act, dmod_cols)


def _rotate(t, cos, sin_lo, sin_hi):
    return t * cos + pltpu.roll(t, LANES - 8, 1) * sin_lo + pltpu.roll(t, 8, 1) * sin_hi


def _rotate_bwd(t, cos, sin_lo, sin_hi):
    return t * cos + pltpu.roll(t * sin_lo, 8, 1) + pltpu.roll(t * sin_hi, LANES - 8, 1)


def _fwd_in(x, vec, w_in_t, b_in, rot, name, exchanges=()):
    s = x.shape[0]
    ts = _seq_tile(s)

    def body(x_ref, vec_ref, w_ref, b_ref, cos_ref, lo_ref, hi_ref, q_ref, k_ref, v_ref, u_ref):
        xt = x_ref[...]
        h = xt * _rsqrt_ms(xt) * vec_ref[0:1, :] * vec_ref[1:2, :] + vec_ref[2:3, :]
        proj = _dot(h.astype(BF16), w_ref[...], NT) + b_ref[0:1, :]
        cos, lo, hi = cos_ref[...], lo_ref[...], hi_ref[...]
        for j in range(ATTN_WIDTH // LANES):
            q_ref[:, j * LANES:(j + 1) * LANES] = (
                _rotate(proj[:, j * LANES:(j + 1) * LANES], cos, lo, hi) * Q_SCALE).astype(BF16)
        k_ref[...] = _rotate(proj[:, ATTN_WIDTH:ATTN_WIDTH + KV_WIDTH], cos, lo, hi).astype(BF16)
        v_ref[...] = proj[:, ATTN_WIDTH + KV_WIDTH:ATTN_WIDTH + 2 * KV_WIDTH].astype(BF16)
        u_ref[...] = proj[:, ATTN_WIDTH + 2 * KV_WIDTH:]

    row = lambda w: pl.BlockSpec((ts, w), lambda i: (i, 0))
    return _call(
        body, [x, vec, w_in_t, b_in, *rot], name=name, grid=(s // ts,),
        in_specs=[row(D_MODEL), _resident((8, D_MODEL)), _resident((IN_WIDTH, D_MODEL)), _resident((8, IN_WIDTH)),
                  row(LANES), row(LANES), row(LANES)],
        out_specs=[row(ATTN_WIDTH), row(KV_WIDTH), row(KV_WIDTH), row(POOL_WIDTH)],
        out_shape=[jax.ShapeDtypeStruct((s, ATTN_WIDTH), BF16), jax.ShapeDtypeStruct((s, KV_WIDTH), BF16),
                   jax.ShapeDtypeStruct((s, KV_WIDTH), BF16), jax.ShapeDtypeStruct((s, POOL_WIDTH), F32)],
        exchanges=exchanges)


def _band_mask(first_block):
    row = lax.broadcasted_iota(jnp.int32, (BLOCK, 2 * BLOCK), 0)
    col = lax.broadcasted_iota(jnp.int32, (BLOCK, 2 * BLOCK), 1)
    d = col - row
    return (d >= 1) & (d <= BLOCK) & ((col >= BLOCK) | jnp.logical_not(first_block))


def _head(ref, h):
    return ref[:, h * HEAD_DIM:(h + 1) * HEAD_DIM]


def _kv_cat(prev_ref, cur_ref, kh):
    return jnp.concatenate([_head(prev_ref, kh), _head(cur_ref, kh)], axis=0)


def _attn_fwd(sinks, q, k, v, name, exchanges=()):
    s = q.shape[0]
    nb = s // BLOCK
    cur = lambda w: pl.BlockSpec((BLOCK, w), lambda i: (i, 0))
    prev = lambda w: pl.BlockSpec((BLOCK, w), lambda i: (jnp.maximum(i - 1, 0), 0))

    def body(sink_ref, q_ref, kc_ref, kp_ref, vc_ref, vp_ref, o_ref, l_ref):
        valid = _band_mask(pl.program_id(0) == 0)
        for kh in range(N_KV_HEADS):
            kcat, vcat = _kv_cat(kp_ref, kc_ref, kh), _kv_cat(vp_ref, vc_ref, kh)
            for g in range(GROUP):
                h = kh * GROUP + g
                sc = jnp.where(valid, _dot(_head(q_ref, h), kcat, NT), NEG_INF)
                sink = sink_ref[h]
                m = jnp.maximum(jnp.max(sc, axis=1, keepdims=True), sink)
                p = jnp.exp(sc - m)
                den = jnp.sum(p, axis=1, keepdims=True) + jnp.exp(sink - m)
                o_ref[:, h * HEAD_DIM:(h + 1) * HEAD_DIM] = _dot((p / den).astype(BF16), vcat).astype(BF16)
                l_ref[:, h:h + 1] = m + jnp.log(den)

    return _call(
        body, [sinks, q, k, k, v, v], name=name, grid=(nb,),
        in_specs=[pl.BlockSpec(memory_space=pltpu.SMEM), cur(ATTN_WIDTH), cur(KV_WIDTH), prev(KV_WIDTH),
                  cur(KV_WIDTH), prev(KV_WIDTH)],
        out_specs=[cur(ATTN_WIDTH), cur(N_HEADS)],
        out_shape=[jax.ShapeDtypeStruct((s, ATTN_WIDTH), BF16), jax.ShapeDtypeStruct((s, N_HEADS), F32)],
        exchanges=exchanges)


def _pool_counts(tile, ts):
    t = (tile * ts + lax.broadcasted_iota(jnp.int32, (ts, 1), 0) + 1).astype(F32)
    return [jnp.minimum(t, float(w)) for w in POOL_WINDOWS]


def _pooled(u, halo, counts, gi):
    cols = slice(gi * POOL_GROUP_WIDTH, (gi + 1) * POOL_GROUP_WIDTH)
    acc = jnp.concatenate([halo[:, cols], u[:, cols]], axis=0)
    shift = 1
    while shift < POOL_WINDOWS[gi]:
        acc = acc + pltpu.roll(acc, shift, 0)
        shift *= 2
    return acc[POOL_HALO:, :] / counts[gi] - u[:, cols]


def _fwd_out(ao, u, pool_w, psc, w_out, x, vec, name):
    s = x.shape[0]
    ts = _seq_tile(s)
    hb = ts // POOL_HALO

    def body(ao_ref, u_ref, uh_ref, pw_ref, psc_ref, w_ref, x_ref, vec_ref, x1_ref, mix_ref):
        i = pl.program_id(0)
        u_t = u_ref[...]
        halo = jnp.where(i > 0, uh_ref[...], 0.0)
        counts = _pool_counts(i, ts)
        mix = _dot(ao_ref[...], w_ref[0:ATTN_WIDTH, :])
        for gi in range(len(POOL_WINDOWS)):
            cols = slice(gi * POOL_GROUP_WIDTH, (gi + 1) * POOL_GROUP_WIDTH)
            og = _dot(_pooled(u_t, halo, counts, gi).astype(BF16), pw_ref[gi]) * psc_ref[0:1, cols]
            mix = mix + _dot(og.astype(BF16), w_ref[ATTN_WIDTH + gi * POOL_GROUP_WIDTH:
                                                    ATTN_WIDTH + (gi + 1) * POOL_GROUP_WIDTH, :])
        mix_ref[...] = mix
        x1_ref[...] = x_ref[...] + vec_ref[0:1, :] * (mix * _rsqrt_ms(mix) * vec_ref[1:2, :])

    row = lambda w: pl.BlockSpec((ts, w), lambda i: (i, 0))
    return pl.pallas_call(
        body, name=name, grid=(s // ts,),
        in_specs=[row(ATTN_WIDTH), row(POOL_WIDTH),
                  pl.BlockSpec((POOL_HALO, POOL_WIDTH), lambda i: (jnp.maximum(i * hb - 1, 0), 0)),
                  _resident(pool_w.shape), _resident((8, POOL_WIDTH)), _resident((D_MODEL, D_MODEL)),
                  row(D_MODEL), _resident((8, D_MODEL))],
        out_specs=[row(D_MODEL), row(D_MODEL)],
        out_shape=[jax.ShapeDtypeStruct((s, D_MODEL), F32), jax.ShapeDtypeStruct((s, D_MODEL), F32)],
        compiler_params=_cparams(1),
    )(ao, u, u, pool_w, psc, w_out, x, vec)


def _ffn_fwd(x1, vec, w_gate_t, w_up_t, w_down, name, exchanges=()):
    s = x1.shape[0]
    ts = _seq_tile(s)

    def body(x_ref, vec_ref, wg_ref, wu_ref, wd_ref, x2_ref, f_ref, gt_ref, up_ref, act_ref, h2_ref):
        xt = x_ref[...]
        h2 = (xt * _rsqrt_ms(xt) * vec_ref[0:1, :] * vec_ref[1:2, :] + vec_ref[2:3, :]).astype(BF16)
        h2_ref[...] = h2
        f = jnp.zeros((ts, D_MODEL), F32)
        for c in range(D_FF // FF_CHUNK):
            cols = slice(c * FF_CHUNK, (c + 1) * FF_CHUNK)
            g = _dot(h2, wg_ref[cols, :], NT)
            up = _dot(h2, wu_ref[cols, :], NT)
            act = ((g * jax.nn.sigmoid(g)) * up).astype(BF16)
            gt_ref[:, cols] = g.astype(BF16)
            up_ref[:, cols] = up.astype(BF16)
            act_ref[:, cols] = act
            f = f + _dot(act, wd_ref[cols, :])
        f_ref[...] = f
        x2_ref[...] = xt + vec_ref[3:4, :] * (f * _rsqrt_ms(f) * vec_ref[4:5, :])

    row = lambda w: pl.BlockSpec((ts, w), lambda i: (i, 0))
    wide = jax.ShapeDtypeStruct((s, D_FF), BF16)
    return _call(
        body, [x1, vec, w_gate_t, w_up_t, w_down], name=name, grid=(s // ts,),
        in_specs=[row(D_MODEL), _resident((8, D_MODEL)), _resident((D_FF, D_MODEL)), _resident((D_FF, D_MODEL)),
                  _resident((D_FF, D_MODEL))],
        out_specs=[row(D_MODEL), row(D_MODEL), row(D_FF), row(D_FF), row(D_FF), row(D_MODEL)],
        out_shape=[jax.ShapeDtypeStruct((s, D_MODEL), F32), jax.ShapeDtypeStruct((s, D_MODEL), F32), wide, wide, wide,
                   jax.ShapeDtypeStruct((s, D_MODEL), BF16)],
        exchanges=exchanges)


def _loss_grad(y, target, name):
    s = y.shape[0]
    ts = _seq_tile(s)

    def body(y_ref, t_ref, dy_ref, loss_ref):
        @pl.when(pl.program_id(0) == 0)
        def _():
            loss_ref[...] = jnp.zeros_like(loss_ref)

        diff = y_ref[...] - t_ref[...]
        dy_ref[...] = diff / D_MODEL
        loss_ref[...] += 0.5 * jnp.sum(jnp.mean(diff * diff, axis=-1, keepdims=True))

    row = pl.BlockSpec((ts, D_MODEL), lambda i: (i, 0))
    return pl.pallas_call(
        body, name=name, grid=(s // ts,),
        in_specs=[row, row], out_specs=[row, pl.BlockSpec((8, LANES), lambda i: (0, 0))],
        out_shape=[jax.ShapeDtypeStruct((s, D_MODEL), F32), jax.ShapeDtypeStruct((8, LANES), F32)],
        compiler_params=_cparams(1),
    )(y, target)


def _norm_bwd(dy_hat, x_hat, r):
    return r * (dy_hat - x_hat * jnp.mean(dy_hat * x_hat, axis=-1, keepdims=True))


def _accumulate_rows(ref, rows):
    for j, val in enumerate(rows):
        ref[j:j + 1, :] += val


def _ffn_bwd(dx2, x1, f, gt, up, vec, w_gate_t, w_up_t, w_down, name, exchanges=()):
    s = x1.shape[0]
    ts = _seq_tile(s) // 2

    def body(dx2_ref, x1_ref, f_ref, gt_ref, up_ref, vec_ref, wg_ref, wu_ref, wd_ref,
             dx1_ref, dgt_ref, dup_ref, df_ref, cs_ref):
        @pl.when(pl.program_id(0) == 0)
        def _():
            cs_ref[...] = jnp.zeros_like(cs_ref)

        dx2_t, f_t = dx2_ref[...], f_ref[...]
        g_pre, one_scale, gate, g_post = vec_ref[0:1, :], vec_ref[1:2, :], vec_ref[3:4, :], vec_ref[4:5, :]
        rf = _rsqrt_ms(f_t)
        f_hat = f_t * rf
        df = _norm_bwd(dx2_t * gate * g_post, f_hat, rf).astype(BF16)
        df_ref[...] = df
        dh2 = jnp.zeros((ts, D_MODEL), F32)
        for c in range(D_FF // FF_CHUNK):
            cols = slice(c * FF_CHUNK, (c + 1) * FF_CHUNK)
            dact = _dot(df, wd_ref[cols, :], NT)
            g, u_t = gt_ref[:, cols].astype(F32), up_ref[:, cols].astype(F32)
            sg = jax.nn.sigmoid(g)
            dg = (dact * u_t * (sg * (1.0 + g * (1.0 - sg)))).astype(BF16)
            du = (dact * (g * sg)).astype(BF16)
            dgt_ref[:, cols] = dg
            dup_ref[:, cols] = du
            dh2 = dh2 + _dot(dg, wg_ref[cols, :]) + _dot(du, wu_ref[cols, :])
        x1_t = x1_ref[...]
        r1 = _rsqrt_ms(x1_t)
        x_hat = x1_t * r1
        dx1_ref[...] = dx2_t + _norm_bwd(dh2 * g_pre * one_scale, x_hat, r1)
        _accumulate_rows(cs_ref, [_colsum(dh2), _colsum(dh2 * (x_hat * g_pre)), _colsum(dh2 * one_scale * x_hat),
                                  _colsum(dx2_t * (f_hat * g_post)), _colsum(dx2_t * gate * f_hat)])

    row = lambda w: pl.BlockSpec((ts, w), lambda i: (i, 0))
    wide = jax.ShapeDtypeStruct((s, D_FF), BF16)
    return _call(
        body, [dx2, x1, f, gt, up, vec, w_gate_t, w_up_t, w_down], name=name, grid=(s // ts,),
        in_specs=[row(D_MODEL), row(D_MODEL), row(D_MODEL), row(D_FF), row(D_FF), _resident((8, D_MODEL)),
                  _resident((D_FF, D_MODEL)), _resident((D_FF, D_MODEL)), _resident((D_FF, D_MODEL))],
        out_specs=[row(D_MODEL), row(D_FF), row(D_FF), row(D_MODEL), pl.BlockSpec((8, D_MODEL), lambda i: (0, 0))],
        out_shape=[jax.ShapeDtypeStruct((s, D_MODEL), F32), wide, wide, jax.ShapeDtypeStruct((s, D_MODEL), BF16),
                   jax.ShapeDtypeStruct((8, D_MODEL), F32)],
        exchanges=exchanges)


def _weight_grad(a, b, name, exchanges=()):
    s, m = a.shape
    n = b.shape[1]
    tk = min(s, 2 * SEQ_TILE)
    steps = s // tk

    def body(a_ref, b_ref, o_ref, acc_ref):
        i = pl.program_id(0)

        @pl.when(i == 0)
        def _():
            acc_ref[...] = jnp.zeros_like(acc_ref)

        b_t = b_ref[...]
        for c in range(m // FF_CHUNK):
            rows = slice(c * FF_CHUNK, (c + 1) * FF_CHUNK)
            acc_ref[rows, :] += _dot(a_ref[:, rows], b_t, TN)

        @pl.when(i == steps - 1)
        def _():
            o_ref[...] = acc_ref[...].astype(BF16)

    (grad,), ex_outs = _call(
        body, [a, b], name=name, grid=(steps,),
        in_specs=[pl.BlockSpec((tk, m), lambda i: (i, 0)), pl.BlockSpec((tk, n), lambda i: (i, 0))],
        out_specs=[_resident((m, n))],
        out_shape=[jax.ShapeDtypeStruct((m, n), BF16)],
        scratch_shapes=[pltpu.VMEM((m, n), F32)],
        exchanges=exchanges)
    return grad, ex_outs


def _out_bwd(dx1, mix, ao, u, pool_w, psc, w_out, vec, name):
    s = dx1.shape[0]
    ts = _seq_tile(s)
    nt = s // ts
    hb = ts // POOL_HALO
    ng = len(POOL_WINDOWS)

    def body(dx1_ref, mix_ref, ao_ref, u_ref, uh_ref, pw_ref, psc_ref, w_ref, vec_ref,
             do_ref, du_ref, cat_ref, dmix_ref, cs_ref, dpw_ref, dpsc_ref, carry_ref):
        i = pl.program_id(0)
        tile = nt - 1 - i

        @pl.when(i == 0)
        def _():
            cs_ref[...] = jnp.zeros_like(cs_ref)
            dpw_ref[...] = jnp.zeros_like(dpw_ref)
            dpsc_ref[...] = jnp.zeros_like(dpsc_ref)
            carry_ref[...] = jnp.zeros_like(carry_ref)

        dx1_t, mix_t = dx1_ref[...], mix_ref[...]
        gate, g_post = vec_ref[0:1, :], vec_ref[1:2, :]
        rm = _rsqrt_ms(mix_t)
        m_hat = mix_t * rm
        dmix = _norm_bwd(dx1_t * gate * g_post, m_hat, rm).astype(BF16)
        dmix_ref[...] = dmix
        _accumulate_rows(cs_ref, [_colsum(dx1_t * (m_hat * g_post)), _colsum(dx1_t * gate * m_hat)])
        do_ref[...] = _dot(dmix, w_ref[0:ATTN_WIDTH, :], NT).astype(BF16)
        cat_ref[:, 0:ATTN_WIDTH] = ao_ref[...]

        u_t = u_ref[...]
        halo = jnp.where(tile > 0, uh_ref[...], 0.0)
        counts = _pool_counts(tile, ts)
        for gi in range(ng):
            cols = slice(gi * POOL_GROUP_WIDTH, (gi + 1) * POOL_GROUP_WIDTH)
            wide_cols = slice(ATTN_WIDTH + gi * POOL_GROUP_WIDTH, ATTN_WIDTH + (gi + 1) * POOL_GROUP_WIDTH)
            scale = psc_ref[0:1, cols]
            pooled = _pooled(u_t, halo, counts, gi).astype(BF16)
            og = _dot(pooled, pw_ref[gi])
            cat_ref[:, wide_cols] = (og * scale).astype(BF16)
            d_out = _dot(dmix, w_ref[wide_cols, :], NT)
            dpsc_ref[0:1, cols] += _colsum(d_out * og)
            d_og = (d_out * scale).astype(BF16)
            dpw_ref[gi] += _dot(pooled, d_og, TN)
            d_pooled = _dot(d_og, pw_ref[gi], NT)
            spread = d_pooled / counts[gi]
            acc = jnp.concatenate([spread, carry_ref[:, cols]], axis=0)
            shift = 1
            while shift < POOL_WINDOWS[gi]:
                acc = acc + pltpu.roll(acc, ts + POOL_HALO - shift, 0)
                shift *= 2
            du_ref[:, cols] = acc[0:ts, :] - d_pooled
            carry_ref[:, cols] = spread[0:POOL_HALO, :]

    row = lambda w: pl.BlockSpec((ts, w), lambda i: (nt - 1 - i, 0))
    fixed = lambda shape: pl.BlockSpec(shape, lambda i: (0,) * len(shape))
    return pl.pallas_call(
        body, name=name, grid=(nt,),
        in_specs=[row(D_MODEL), row(D_MODEL), row(ATTN_WIDTH), row(POOL_WIDTH),
                  pl.BlockSpec((POOL_HALO, POOL_WIDTH), lambda i: (jnp.maximum((nt - 1 - i) * hb - 1, 0), 0)),
                  _resident(pool_w.shape), _resident((8, POOL_WIDTH)), _resident((D_MODEL, D_MODEL)),
                  _resident((8, D_MODEL))],
        out_specs=[row(ATTN_WIDTH), row(POOL_WIDTH), row(D_MODEL), row(D_MODEL), fixed((8, D_MODEL)),
                   fixed(pool_w.shape), fixed((8, POOL_WIDTH))],
        out_shape=[jax.ShapeDtypeStruct((s, ATTN_WIDTH), BF16), jax.ShapeDtypeStruct((s, POOL_WIDTH), F32),
                   jax.ShapeDtypeStruct((s, D_MODEL), BF16), jax.ShapeDtypeStruct((s, D_MODEL), BF16),
                   jax.ShapeDtypeStruct((8, D_MODEL), F32), jax.ShapeDtypeStruct(pool_w.shape, F32),
                   jax.ShapeDtypeStruct((8, POOL_WIDTH), F32)],
        scratch_shapes=[pltpu.VMEM((POOL_HALO, POOL_WIDTH), F32)],
        compiler_params=_cparams(1),
    )(dx1, mix, ao, u, u, pool_w, psc, w_out, vec)


def _attn_bwd(sinks, q, do, lse, k, v, name, exchanges=()):
    s = q.shape[0]
    nb = s // BLOCK
    cur = lambda w: pl.BlockSpec((BLOCK, w), lambda i: (i, 0))
    prev = lambda w: pl.BlockSpec((BLOCK, w), lambda i: (jnp.maximum(i - 1, 0), 0))

    def body(sink_ref, q_ref, do_ref, l_ref, kc_ref, kp_ref, vc_ref, vp_ref,
             dq_ref, dkc_ref, dkp_ref, dvc_ref, dvp_ref, dsink_ref):
        @pl.when(pl.program_id(0) == 0)
        def _():
            dsink_ref[...] = jnp.zeros_like(dsink_ref)

        valid = _band_mask(pl.program_id(0) == 0)
        for kh in range(N_KV_HEADS):
            kcat, vcat = _kv_cat(kp_ref, kc_ref, kh), _kv_cat(vp_ref, vc_ref, kh)
            dk = jnp.zeros((2 * BLOCK, HEAD_DIM), F32)
            dv = jnp.zeros((2 * BLOCK, HEAD_DIM), F32)
            for g in range(GROUP):
                h = kh * GROUP + g
                qh, doh, lse_h = _head(q_ref, h), _head(do_ref, h), l_ref[:, h:h + 1]
                p = jnp.where(valid, jnp.exp(_dot(qh, kcat, NT) - lse_h), 0.0)
                dp = _dot(doh, vcat, NT)
                delta = jnp.sum(p * dp, axis=1, keepdims=True)
                dsink_ref[h:h + 1, :] += -jnp.sum(jnp.exp(sink_ref[h] - lse_h) * delta)
                ds = (p * (dp - delta)).astype(BF16)
                dq_ref[:, h * HEAD_DIM:(h + 1) * HEAD_DIM] = _dot(ds, kcat) * Q_SCALE
                dk = dk + _dot(ds, qh, TN)
                dv = dv + _dot(p.astype(BF16), doh, TN)
            hcols = slice(kh * HEAD_DIM, (kh + 1) * HEAD_DIM)
            dkp_ref[:, hcols] = dk[0:BLOCK, :]
            dkc_ref[:, hcols] = dk[BLOCK:, :]
            dvp_ref[:, hcols] = dv[0:BLOCK, :]
            dvc_ref[:, hcols] = dv[BLOCK:, :]

    kv = jax.ShapeDtypeStruct((s, KV_WIDTH), F32)
    return _call(
        body, [sinks, q, do, lse, k, k, v, v], name=name, grid=(nb,),
        in_specs=[pl.BlockSpec(memory_space=pltpu.SMEM), cur(ATTN_WIDTH), cur(ATTN_WIDTH), cur(N_HEADS),
                  cur(KV_WIDTH), prev(KV_WIDTH), cur(KV_WIDTH), prev(KV_WIDTH)],
        out_specs=[cur(ATTN_WIDTH), cur(KV_WIDTH), cur(KV_WIDTH), cur(KV_WIDTH), cur(KV_WIDTH),
                   pl.BlockSpec((8, LANES), lambda i: (0, 0))],
        out_shape=[jax.ShapeDtypeStruct((s, ATTN_WIDTH), F32), kv, kv, kv, kv, jax.ShapeDtypeStruct((8, LANES), F32)],
        exchanges=exchanges)


def _in_bwd(dq, dkc, dkp, dvc, dvp, du, x, dx1, vec, w_in_t, rot, name):
    s = x.shape[0]
    ts = _seq_tile(s)
    nt = s // ts
    bpt = ts // BLOCK

    def body(dq_ref, dkc_ref, dkp_ref, dkn_ref, dvc_ref, dvp_ref, dvn_ref, du_ref, x_ref, dx1_ref, vec_ref, w_ref,
             cos_ref, lo_ref, hi_ref, dx_ref, dproj_ref, h_ref, cs_ref, db_ref):
        i = pl.program_id(0)

        @pl.when(i == 0)
        def _():
            cs_ref[...] = jnp.zeros_like(cs_ref)
            db_ref[...] = jnp.zeros_like(db_ref)

        cos, lo, hi = cos_ref[...], lo_ref[...], hi_ref[...]

        def with_next_block(cur_ref, prev_ref, next_ref):
            nxt = jnp.where(i < nt - 1, next_ref[...], 0.0)
            later = nxt if bpt == 1 else jnp.concatenate([prev_ref[BLOCK:, :], nxt], axis=0)
            return cur_ref[...] + later

        pieces = [_rotate_bwd(dq_ref[:, j * LANES:(j + 1) * LANES], cos, lo, hi) for j in range(ATTN_WIDTH // LANES)]
        pieces.append(_rotate_bwd(with_next_block(dkc_ref, dkp_ref, dkn_ref), cos, lo, hi))
        pieces.append(with_next_block(dvc_ref, dvp_ref, dvn_ref))
        pieces.append(du_ref[...])
        dproj = jnp.concatenate(pieces, axis=1)
        db_ref[0:1, :] += _colsum(dproj)
        dproj_b = dproj.astype(BF16)
        dproj_ref[...] = dproj_b
        dh = _dot(dproj_b, w_ref[...])

        xt = x_ref[...]
        g_pre, one_scale = vec_ref[0:1, :], vec_ref[1:2, :]
        r = _rsqrt_ms(xt)
        x_hat = xt * r
        h_ref[...] = (x_hat * g_pre * one_scale + vec_ref[2:3, :]).astype(BF16)
        dx_ref[...] = dx1_ref[...] + _norm_bwd(dh * g_pre * one_scale, x_hat, r)
        _accumulate_rows(cs_ref, [_colsum(dh), _colsum(dh * (x_hat * g_pre)), _colsum(dh * one_scale * x_hat)])

    row = lambda w: pl.BlockSpec((ts, w), lambda i: (i, 0))
    nxt = pl.BlockSpec((BLOCK, KV_WIDTH), lambda i: (jnp.minimum((i + 1) * bpt, s // BLOCK - 1), 0))
    fixed = lambda shape: pl.BlockSpec(shape, lambda i: (0,) * len(shape))
    return pl.pallas_call(
        body, name=name, grid=(nt,),
        in_specs=[row(ATTN_WIDTH), row(KV_WIDTH), row(KV_WIDTH), nxt, row(KV_WIDTH), row(KV_WIDTH), nxt,
                  row(POOL_WIDTH), row(D_MODEL), row(D_MODEL), _resident((8, D_MODEL)), _resident((IN_WIDTH, D_MODEL)),
                  row(LANES), row(LANES), row(LANES)],
        out_specs=[row(D_MODEL), row(IN_WIDTH), row(D_MODEL), fixed((8, D_MODEL)), fixed((8, IN_WIDTH))],
        out_shape=[jax.ShapeDtypeStruct((s, D_MODEL), F32), jax.ShapeDtypeStruct((s, IN_WIDTH), BF16),
                   jax.ShapeDtypeStruct((s, D_MODEL), BF16), jax.ShapeDtypeStruct((8, D_MODEL), F32),
                   jax.ShapeDtypeStruct((8, IN_WIDTH), F32)],
        compiler_params=_cparams(1),
    )(dq, dkc, dkp, dkp, dvc, dvp, dvp, du, x, dx1, vec, w_in_t, *rot)


def _row_tile(rows):
    for t in (512, 352, 256, 176, 160, 128, 64, 32, 16, 8):
        if rows % t == 0:
            return t
    raise ValueError(f"no row tile for {rows} rows")


def _sum_parts(own, others, name):
    r, c = own.shape
    n_other = others.shape[0]
    tr = _row_tile(r)

    def body(own_ref, oth_ref, o_ref):
        acc = own_ref[...].astype(F32)
        for k in range(n_other):
            acc = acc + oth_ref[k].astype(F32)
        o_ref[...] = acc

    return pl.pallas_call(
        body, name=name, grid=(r // tr,),
        in_specs=[pl.BlockSpec((tr, c), lambda i: (i, 0)), pl.BlockSpec((n_other, tr, c), lambda i: (0, i, 0))],
        out_specs=pl.BlockSpec((tr, c), lambda i: (i, 0)),
        out_shape=jax.ShapeDtypeStruct((r, c), F32),
        compiler_params=_cparams(1),
    )(own, others)


def _adamw(w, g, m, v, name):
    r, c = w.shape
    tr = _row_tile(r)

    def body(w_ref, g_ref, m_ref, v_ref, d_ref, nm_ref, nv_ref):
        g_t = g_ref[...]
        m_new = ADAM_B1 * m_ref[...] + (1.0 - ADAM_B1) * g_t
        v_new = ADAM_B2 * v_ref[...] + (1.0 - ADAM_B2) * (g_t * g_t)
        m_hat = m_new / (1.0 - ADAM_B1 ** ADAM_STEP)
        v_hat = v_new / (1.0 - ADAM_B2 ** ADAM_STEP)
        d_ref[...] = -ADAM_LR * (m_hat / (jnp.sqrt(v_hat) + ADAM_EPS) + ADAM_WD * w_ref[...])
        nm_ref[...] = m_new
        nv_ref[...] = v_new

    spec = pl.BlockSpec((tr, c), lambda i: (i, 0))
    shape = jax.ShapeDtypeStruct((r, c), F32)
    return pl.pallas_call(
        body, name=name, grid=(r // tr,), in_specs=[spec] * 4, out_specs=[spec] * 3, out_shape=[shape] * 3,
        compiler_params=_cparams(1),
    )(w, g, m, v)


SMALL_ROWS = 160


def _pack_small(dmod, gains, b_in, pool_scale, sinks, pool_w):
    pad = lambda a, width: jnp.pad(a, ((0, 0), (0, width - a.shape[1])))
    parts = [dmod.reshape(DEPTH * N_MOD, D_MODEL), gains.reshape(4 * DEPTH, D_MODEL),
             pad(b_in, 2 * D_MODEL).reshape(2 * DEPTH, D_MODEL), pool_scale.reshape(1, D_MODEL),
             pad(sinks.reshape(1, DEPTH * N_HEADS), D_MODEL), pool_w.reshape(-1, D_MODEL)]
    packed = jnp.concatenate(parts, axis=0)
    return jnp.pad(packed, ((0, SMALL_ROWS - packed.shape[0]), (0, 0)))


def _unpack_small(p):
    r = DEPTH * N_MOD
    dmod = p[0:r].reshape(DEPTH, N_MOD * D_MODEL)
    gains = p[r:r + 4 * DEPTH].reshape(4, DEPTH, D_MODEL)
    r += 4 * DEPTH
    b_in = p[r:r + 2 * DEPTH].reshape(DEPTH, 2 * D_MODEL)[:, :IN_WIDTH]
    r += 2 * DEPTH
    pool_scale = p[r].reshape(DEPTH, POOL_WIDTH)
    sinks = p[r + 1, :DEPTH * N_HEADS].reshape(DEPTH, N_HEADS)
    pool_w = p[r + 2:r + 2 + 128].reshape(DEPTH, len(POOL_WINDOWS), POOL_GROUP_WIDTH, POOL_GROUP_WIDTH)
    return dmod, gains, b_in, pool_scale, sinks, pool_w


LAYER_ROWS = 80


def _pack_layer(dmod, gains, b_in, pool_scale, sinks, pool_w):
    misc = jnp.concatenate([pool_scale, sinks, jnp.zeros((D_MODEL - POOL_WIDTH - N_HEADS,), F32)])
    parts = [dmod.reshape(N_MOD, D_MODEL), gains, jnp.pad(b_in, (0, 2 * D_MODEL - IN_WIDTH)).reshape(2, D_MODEL),
             misc.reshape(1, D_MODEL), pool_w.reshape(-1, D_MODEL)]
    packed = jnp.concatenate(parts, axis=0)
    return jnp.pad(packed, ((0, LAYER_ROWS - packed.shape[0]), (0, 0)))


def _unpack_layer(p):
    r = N_MOD + 4
    b_in = p[r:r + 2].reshape(2 * D_MODEL)[:IN_WIDTH]
    pool_w = p[r + 3:r + 3 + 64].reshape(len(POOL_WINDOWS), POOL_GROUP_WIDTH, POOL_GROUP_WIDTH)
    return (p[0:N_MOD].reshape(N_MOD * D_MODEL), p[N_MOD:r], b_in, p[r + 2, :POOL_WIDTH],
            p[r + 2, POOL_WIDTH:POOL_WIDTH + N_HEADS], pool_w)


def _rotary_tables(positions):
    inv_freq = ROPE_THETA ** (-jnp.arange(0, ROT_DIM, 2, dtype=F32) / ROT_DIM)
    ang = positions.reshape(-1).astype(F32)[:, None] * inv_freq
    cos, sin = jnp.cos(ang), jnp.sin(ang)
    s = ang.shape[0]
    rest = HEAD_DIM - ROT_DIM
    head = lambda a, b, fill: jnp.concatenate([a, b, jnp.full((s, rest), fill, F32)], axis=1)
    zero = jnp.zeros_like(sin)
    pair = lambda t: jnp.concatenate([t, t], axis=1)
    return pair(head(cos, cos, 1.0)), pair(head(-sin, zero, 0.0)), pair(head(zero, sin, 0.0))


def kernel(x, c, positions, ada_w, ada_b, w_in, b_in, sinks, pool_w, pool_scale, w_out, w_gate, w_up, w_down, g_pre_mix, g_post_mix, g_pre_ffn, g_post_ffn, loss_target, m_ada_w, m_ada_b, m_w_in, m_b_in, m_sinks, m_pool_w, m_pool_scale, m_w_out, m_w_gate, m_w_up, m_w_down, m_g_pre_mix, m_g_post_mix, m_g_pre_ffn, m_g_post_ffn, v_ada_w, v_ada_b, v_w_in, v_b_in, v_sinks, v_pool_w, v_pool_scale, v_w_out, v_w_gate, v_w_up, v_w_down, v_g_pre_mix, v_g_post_mix, v_g_pre_ffn, v_g_post_ffn):
    me = _index(_mesh_place())
    x0 = x[0]
    target = loss_target[0]
    rot = _rotary_tables(positions)
    ada_cols = ada_w.shape[2]

    c_all = _allgather_vmem(jnp.broadcast_to(c, (8, D_MODEL)), "gather_c")[:, 0, :]
    mod_part, c_act = _mod_fwd(c_all, ada_w)
    mod_all = _allgather_vmem(mod_part.reshape(DEPTH * N_DEV, ada_cols), "gather_mod")
    mod_mine = lax.dynamic_slice_in_dim(mod_all.reshape(N_DEV, DEPTH, N_DEV, ada_cols), me, 1, axis=2)[:, :, 0, :]
    mod = jnp.transpose(mod_mine, (1, 0, 2)).reshape(DEPTH, N_MOD * D_MODEL) + ada_b
    mod = mod.reshape(DEPTH, N_MOD, D_MODEL)

    shards = []
    for l in range(DEPTH):
        shards += [w_in[l].T.astype(BF16), w_out[l].astype(BF16), w_gate[l].T.astype(BF16), w_up[l].T.astype(BF16),
                   w_down[l].astype(BF16)]
    whole = lambda gathered: [g.reshape(-1, D_MODEL) for g in gathered]
    full = [None] * (5 * DEPTH)
    full[0:2] = whole(_exchange_alone(_Gather(shards[0:2]), "gather_w_in_out_0"))
    hosted = {("fwd_in", 0): (_Gather(shards[4:5]), [4]), ("attn_fwd", 0): (_Gather(shards[2:4]), [2, 3]),
              ("ffn_fwd", 0): (_Gather(shards[5:10]), [5, 6, 7, 8, 9])}

    def take(kind, l, ex_outs):
        if (kind, l) in hosted:
            for slot, g in zip(hosted[kind, l][1], whole(ex_outs[0])):
                full[slot] = g

    def beside(kind, l):
        return [hosted[kind, l][0]] if (kind, l) in hosted else []

    pool_w_b = pool_w.astype(BF16)

    saved = []
    xl = x0
    for l in range(DEPTH):
        vec_in = _rows(g_pre_mix[l], 1.0 + mod[l, 1], mod[l, 0], width=D_MODEL)
        vec_out = _rows(mod[l, 2], g_post_mix[l], width=D_MODEL)
        vec_ffn = _rows(g_pre_ffn[l], 1.0 + mod[l, 4], mod[l, 3], mod[l, 5], g_post_ffn[l], width=D_MODEL)
        psc = _rows(pool_scale[l], width=POOL_WIDTH)
        (q, k, v, u), ex = _fwd_in(xl, vec_in, full[5 * l], _rows(b_in[l], width=IN_WIDTH), rot, f"fwd_in_{l}",
                                   beside("fwd_in", l))
        take("fwd_in", l, ex)
        (ao, lse), ex = _attn_fwd(sinks[l], q, k, v, f"attn_fwd_{l}", beside("attn_fwd", l))
        take("attn_fwd", l, ex)
        x1, mix = _fwd_out(ao, u, pool_w_b[l], psc, full[5 * l + 1], xl, vec_out, f"fwd_out_{l}")
        (x2, f, gt, up, act, h2), ex = _ffn_fwd(x1, vec_ffn, full[5 * l + 2], full[5 * l + 3], full[5 * l + 4],
                                                f"ffn_fwd_{l}", beside("ffn_fwd", l))
        take("ffn_fwd", l, ex)
        saved.append((xl, q, k, v, u, ao, lse, x1, mix, f, gt, up, act, h2, vec_in, vec_out, vec_ffn, psc))
        xl = x2

    dx, loss_part = _loss_grad(xl, target, "loss_grad")
    loss = lax.psum(loss_part[0, 0], ("x", "y", "c"))

    stacks = [None] * (5 * DEPTH)
    others = [None] * (5 * DEPTH)
    small_all = [None] * DEPTH
    ready_stacks, ready_small = [], []

    def leaving():
        exs, notes = [], []
        if ready_stacks:
            exs.append(_Scatter([stacks[a] for a in ready_stacks]))
            notes.append(("stacks", list(ready_stacks)))
            ready_stacks.clear()
        if ready_small:
            exs.append(_Gather([ready_small[0][1]]))
            notes.append(("small", ready_small[0][0]))
            ready_small.clear()
        return exs, notes

    def arrived(notes, ex_outs):
        for (kind, what), outs in zip(notes, ex_outs):
            if kind == "stacks":
                for a, o in zip(what, outs):
                    others[a] = o
            else:
                small_all[what] = outs[0]

    def grad_ready(a, grad):
        stacks[a] = grad.reshape(N_DEV, -1, D_MODEL)
        ready_stacks.append(a)

    for l in reversed(range(DEPTH)):
        w_in_t, w_out_f, w_gate_t, w_up_t, w_down_f = full[5 * l:5 * l + 5]
        xin, q, k, v, u, ao, lse, x1, mix, f, gt, up, act, h2, vec_in, vec_out, vec_ffn, psc = saved[l]
        exs, notes = leaving()
        (dx1, dgt, dup, df, cs_f), ex = _ffn_bwd(dx, x1, f, gt, up, vec_ffn, w_gate_t, w_up_t, w_down_f,
                                                 f"ffn_bwd_{l}", exs)
        arrived(notes, ex)
        grad_ready(5 * l + 2, _weight_grad(dgt, h2, f"grad_w_gate_{l}")[0])
        grad_ready(5 * l + 3, _weight_grad(dup, h2, f"grad_w_up_{l}")[0])
        grad_ready(5 * l + 4, _weight_grad(act, df, f"grad_w_down_{l}")[0])
        do, du, cat, dmix, cs_o, dpw, dpsc = _out_bwd(dx1, mix, ao, u, pool_w_b[l], psc, w_out_f, vec_out, f"out_bwd_{l}")
        grad_ready(5 * l + 1, _weight_grad(cat, dmix, f"grad_w_out_{l}")[0])
        exs, notes = leaving()
        (dq, dkc, dkp, dvc, dvp, dsk), ex = _attn_bwd(sinks[l], q, do, lse, k, v, f"attn_bwd_{l}", exs)
        arrived(notes, ex)
        dx, dproj, h, cs_i, db = _in_bwd(dq, dkc, dkp, dvc, dvp, du, xin, dx1, vec_in, w_in_t, rot, f"in_bwd_{l}")
        d_mod = jnp.concatenate([cs_i[0], cs_i[1], cs_o[0], cs_f[0], cs_f[1], cs_f[3]])
        d_gain = jnp.stack([cs_i[2], cs_o[1], cs_f[2], cs_f[4]])
        ready_small.append((l, _pack_layer(d_mod, d_gain, db[0], dpsc[0], dsk[:, 0], dpw)))
        exs, notes = leaving() if l == 0 else ([], [])
        grad, ex = _weight_grad(dproj, h, f"grad_w_in_{l}", exs)
        arrived(notes, ex)
        grad_ready(5 * l, grad)
    grad_x = dx[None]
    exs, notes = leaving()
    arrived(notes, [_exchange_alone(ex, "scatter_last") for ex in exs])

    layer_sums = [_unpack_layer(_sum_parts(small_all[l][0], small_all[l][1:], f"sum_small_{l}")) for l in range(DEPTH)]
    g_ada_b, g_gains, g_b_in, g_pool_scale, g_sinks, g_pool_w = (
        jnp.stack([layer_sums[l][j] for l in range(DEPTH)], axis=1 if j == 1 else 0) for j in range(6))
    small_sum = _pack_small(g_ada_b, g_gains, g_b_in, g_pool_scale, g_sinks, g_pool_w)
    gains = jnp.stack([g_pre_mix, g_post_mix, g_pre_ffn, g_post_ffn])
    m_gains = jnp.stack([m_g_pre_mix, m_g_post_mix, m_g_pre_ffn, m_g_post_ffn])
    v_gains = jnp.stack([v_g_pre_mix, v_g_post_mix, v_g_pre_ffn, v_g_post_ffn])
    small_step = _adamw(_pack_small(ada_b, gains, b_in, pool_scale, sinks, pool_w), small_sum,
                        _pack_small(m_ada_b, m_gains, m_b_in, m_pool_scale, m_sinks, m_pool_w),
                        _pack_small(v_ada_b, v_gains, v_b_in, v_pool_scale, v_sinks, v_pool_w), "adamw_small")
    small_out = [_unpack_small(p) for p in small_step]

    dmod_all = jnp.stack([small_all[l][:, 0:N_MOD, :].reshape(N_DEV, N_DEV, ada_cols) for l in range(DEPTH)])
    dmod_cols = lax.dynamic_slice_in_dim(dmod_all, me, 1, axis=2)[:, :, 0, :]
    g_ada_w = _ada_grad(c_act, dmod_cols)

    summed = []
    for a, (stack, oth) in enumerate(zip(stacks, others)):
        own = lax.dynamic_index_in_dim(stack, me, axis=0, keepdims=False)
        summed.append(_sum_parts(own, oth, f"sum_grads_{a}"))
    per_kind = [jnp.stack([summed[5 * l + j] for l in range(DEPTH)]) for j in range(5)]
    g_w_in, g_w_out, g_w_gate, g_w_up, g_w_down = per_kind
    g_w_in, g_w_gate, g_w_up = (jnp.transpose(g, (0, 2, 1)) for g in (g_w_in, g_w_gate, g_w_up))

    def step(w, g, m, v, name):
        flat = lambda a: a.reshape(-1, a.shape[-1])
        return [o.reshape(w.shape) for o in _adamw(flat(w), flat(g), flat(m), flat(v), name)]

    big = {
        "ada_w": (g_ada_w, step(ada_w, g_ada_w, m_ada_w, v_ada_w, "adamw_ada_w")),
        "w_in": (g_w_in, step(w_in, g_w_in, m_w_in, v_w_in, "adamw_w_in")),
        "w_out": (g_w_out, step(w_out, g_w_out, m_w_out, v_w_out, "adamw_w_out")),
        "w_gate": (g_w_gate, step(w_gate, g_w_gate, m_w_gate, v_w_gate, "adamw_w_gate")),
        "w_up": (g_w_up, step(w_up, g_w_up, m_w_up, v_w_up, "adamw_w_up")),
        "w_down": (g_w_down, step(w_down, g_w_down, m_w_down, v_w_down, "adamw_w_down")),
    }

    def ordered(pick_big, pick_small):
        ada_b_, gains_, b_in_, pool_scale_, sinks_, pool_w_ = pick_small
        return [pick_big("ada_w"), ada_b_, pick_big("w_in"), b_in_, sinks_, pool_w_, pool_scale_, pick_big("w_out"),
                pick_big("w_gate"), pick_big("w_up"), pick_big("w_down"), gains_[0], gains_[1], gains_[2], gains_[3]]

    grads = ordered(lambda n: big[n][0], (g_ada_b, g_gains, g_b_in, g_pool_scale, g_sinks, g_pool_w))
    deltas = ordered(lambda n: big[n][1][0], small_out[0])
    new_m = ordered(lambda n: big[n][1][1], small_out[1])
    new_v = ordered(lambda n: big[n][1][2], small_out[2])
    return (loss, grad_x, *grads, *deltas, *new_m, *new_v)
```

```python
import jax
import jax.numpy as jnp
from jax import lax
from jax.experimental import pallas as pl
from jax.experimental.pallas import tpu as pltpu

F32 = jnp.float32
BF16 = jnp.bfloat16

N_DEV = 8
DEPTH = 2
D_MODEL = 1024
HEAD_DIM = 64
N_HEADS = 8
N_KV_HEADS = 2
GROUP = N_HEADS // N_KV_HEADS
ATTN_WIDTH = N_HEADS * HEAD_DIM
KV_WIDTH = N_KV_HEADS * HEAD_DIM
POOL_WIDTH = 512
POOL_WINDOWS = (2, 4, 8, 16)
POOL_GROUP_WIDTH = 128
POOL_HALO = 16
IN_WIDTH = ATTN_WIDTH + 2 * KV_WIDTH + POOL_WIDTH
D_FF = 2816
N_MOD = 6
BLOCK = 128
ROT_DIM = 16
ROPE_THETA = 500000.0
EPS = 1e-6
NEG_INF = -1e30
Q_SCALE = HEAD_DIM ** -0.5

ADAM_LR = 0.001
ADAM_B1 = 0.9
ADAM_B2 = 0.999
ADAM_EPS = 1e-08
ADAM_WD = 0.01
ADAM_STEP = 10

LANES = 128
SEQ_TILE = 512
FF_CHUNK = 256
VMEM_LIMIT = 56 * 1024 * 1024
MESH = pl.DeviceIdType.MESH

NT = (((1,), (1,)), ((), ()))
TN = (((0,), (0,)), ((), ()))


def _dot(a, b, dims=None):
    if dims is None:
        return jnp.dot(a, b, preferred_element_type=F32)
    return lax.dot_general(a, b, dims, preferred_element_type=F32)


def _cparams(n_axes):
    return pltpu.CompilerParams(dimension_semantics=("arbitrary",) * n_axes, vmem_limit_bytes=VMEM_LIMIT)


def _resident(shape):
    zeros = (0,) * len(shape)
    return pl.BlockSpec(shape, lambda *_: zeros, pipeline_mode=pl.Buffered(1))


def _rows(*vectors, width):
    rows = [jnp.reshape(v, (1, width)).astype(F32) for v in vectors]
    rows.append(jnp.zeros((8 - len(rows), width), F32))
    return jnp.concatenate(rows, axis=0)


def _rsqrt_ms(x):
    return lax.rsqrt(jnp.mean(x * x, axis=-1, keepdims=True) + EPS)


def _colsum(x):
    return jnp.sum(x, axis=0, keepdims=True)


def _seq_tile(s):
    return min(s, SEQ_TILE)


def _mesh_place():
    x, y, c = lax.axis_index("x"), lax.axis_index("y"), lax.axis_index("c")
    return x, y, c


def _flip(place, k):
    x, y, c = place
    return (1 - x if k & 4 else x, 1 - y if k & 2 else y, 1 - c if k & 1 else c)


def _index(place):
    x, y, c = place
    return 4 * x + 2 * y + c


def _allgather_vmem(block, name):
    r, c = block.shape

    def body(x_ref, out_ref, send_sems, recv_sems, local_sem):
        me = _mesh_place()
        mine = pltpu.make_async_copy(x_ref, out_ref.at[_index(me)], local_sem)
        mine.start()

        def copy(k):
            return pltpu.make_async_remote_copy(
                src_ref=x_ref, dst_ref=out_ref.at[_index(me)], send_sem=send_sems.at[k - 1], recv_sem=recv_sems.at[k - 1],
                device_id=_flip(me, k), device_id_type=MESH)

        def arrival(k):
            return pltpu.make_async_remote_copy(
                src_ref=x_ref, dst_ref=out_ref.at[_index(_flip(me, k))], send_sem=send_sems.at[k - 1],
                recv_sem=recv_sems.at[k - 1], device_id=_flip(me, k), device_id_type=MESH)

        for k in range(1, N_DEV):
            copy(k).start()
        for k in range(1, N_DEV):
            arrival(k).wait_recv()
        for k in range(1, N_DEV):
            copy(k).wait_send()
        mine.wait()

    return pl.pallas_call(
        body, name=name,
        out_shape=jax.ShapeDtypeStruct((N_DEV, r, c), block.dtype),
        in_specs=[pl.BlockSpec(memory_space=pltpu.VMEM)],
        out_specs=pl.BlockSpec(memory_space=pltpu.VMEM),
        scratch_shapes=[pltpu.SemaphoreType.DMA((N_DEV - 1,)), pltpu.SemaphoreType.DMA((N_DEV - 1,)),
                        pltpu.SemaphoreType.DMA],
    )(block)


class _Gather:
    def __init__(self, shards):
        n = len(shards)
        self.operands = list(shards)
        self.out_shape = [jax.ShapeDtypeStruct((N_DEV,) + s.shape, s.dtype) for s in shards]
        self.scratch = [pltpu.SemaphoreType.DMA((N_DEV - 1, n)), pltpu.SemaphoreType.DMA((N_DEV - 1, n)),
                        pltpu.SemaphoreType.DMA((n,))]

    def _copies(self, x_refs, out_refs, sems):
        send_sems, recv_sems, local_sems = sems
        n = len(x_refs)
        x, y, c = _mesh_place()
        me, sibling = (x, y, c), (x, y, 1 - c)
        chips = [(1 - x, y), (x, 1 - y), (1 - x, 1 - y)]

        def copy(k, a, block, to, from_input=False):
            rows = out_refs[a].at[_index(block)]
            return pltpu.make_async_remote_copy(
                src_ref=x_refs[a] if from_input else rows, dst_ref=rows,
                send_sem=send_sems.at[k, a], recv_sem=recv_sems.at[k, a], device_id=to, device_id_type=MESH)

        mine = [pltpu.make_async_copy(x_refs[a], out_refs[a].at[_index(me)], local_sems.at[a]) for a in range(n)]
        first = [copy(0, a, me, sibling, from_input=True) for a in range(n)]
        first += [copy(1 + j, a, me, (*chip, c), from_input=True) for j, chip in enumerate(chips) for a in range(n)]
        over_ici = [copy(1 + j, a, (*chip, c), me) for j, chip in enumerate(chips) for a in range(n)]
        passed = [copy(4 + j, a, (*chip, c), sibling) for j, chip in enumerate(chips) for a in range(n)]
        from_sibling = [copy(0, a, sibling, me) for a in range(n)]
        from_sibling += [copy(4 + j, a, (*chip, 1 - c), me) for j, chip in enumerate(chips) for a in range(n)]
        return mine, first, over_ici, passed, from_sibling

    def begin(self, x_refs, out_refs, sems):
        mine, first, _, _, _ = self._copies(x_refs, out_refs, sems)
        for cp in mine + first:
            cp.start()

    def middle(self, x_refs, out_refs, sems):
        _, _, over_ici, passed, _ = self._copies(x_refs, out_refs, sems)
        for arrived, onward in zip(over_ici, passed):
            arrived.wait_recv()
            onward.start()

    def end(self, x_refs, out_refs, sems):
        mine, first, _, passed, from_sibling = self._copies(x_refs, out_refs, sems)
        for cp in from_sibling:
            cp.wait_recv()
        for cp in first + passed:
            cp.wait_send()
        for cp in mine:
            cp.wait()


class _Scatter:
    def __init__(self, stacks):
        n = len(stacks)
        self.operands = list(stacks)
        self.out_shape = [jax.ShapeDtypeStruct((N_DEV - 1,) + s.shape[1:], s.dtype) for s in stacks]
        self.scratch = [pltpu.SemaphoreType.DMA((N_DEV - 1, n)), pltpu.SemaphoreType.DMA((N_DEV - 1, n))]

    def _copies(self, g_refs, out_refs, sems):
        send_sems, recv_sems = sems
        me = _mesh_place()

        def copy(k, a):
            peer = _flip(me, k)
            return pltpu.make_async_remote_copy(
                src_ref=g_refs[a].at[_index(peer)], dst_ref=out_refs[a].at[k - 1],
                send_sem=send_sems.at[k - 1, a], recv_sem=recv_sems.at[k - 1, a], device_id=peer, device_id_type=MESH)

        return [copy(k, a) for k in range(1, N_DEV) for a in range(len(g_refs))]

    def begin(self, g_refs, out_refs, sems):
        for cp in self._copies(g_refs, out_refs, sems):
            cp.start()

    def middle(self, g_refs, out_refs, sems):
        pass

    def end(self, g_refs, out_refs, sems):
        copies = self._copies(g_refs, out_refs, sems)
        for cp in copies:
            cp.wait_recv()
        for cp in copies:
            cp.wait_send()


def _call(body, args, *, name, grid, in_specs, out_specs, out_shape, scratch_shapes=(), exchanges=()):
    if not exchanges:
        outs = pl.pallas_call(body, name=name, grid=grid, in_specs=in_specs, out_specs=out_specs, out_shape=out_shape,
                              scratch_shapes=list(scratch_shapes), compiler_params=_cparams(len(grid)))(*args)
        return outs, []
    (steps,) = grid
    n_in, n_out, n_scr = len(in_specs), len(out_specs), len(scratch_shapes)
    ex_in = [len(ex.operands) for ex in exchanges]
    ex_out = [len(ex.out_shape) for ex in exchanges]
    ex_scr = [len(ex.scratch) for ex in exchanges]

    def split(refs, counts):
        parts, pos = [], 0
        for cnt in counts:
            parts.append(refs[pos:pos + cnt])
            pos += cnt
        return parts

    def wrapped(*refs):
        ins, xin, outs, xout, scr, xscr = split(refs, [n_in, sum(ex_in), n_out, sum(ex_out), n_scr, sum(ex_scr)])
        bound = list(zip(exchanges, split(xin, ex_in), split(xout, ex_out), split(xscr, ex_scr)))
        step = pl.program_id(0)

        def phase(method, at):
            @pl.when(step == at)
            def _():
                for ex, i_refs, o_refs, sems in bound:
                    getattr(ex, method)(i_refs, o_refs, sems)

        phase("begin", 0)
        phase("middle", (3 * steps) // 4)
        body(*ins, *outs, *scr)
        phase("end", steps - 1)

    any_spec = pl.BlockSpec(memory_space=pl.ANY)
    results = pl.pallas_call(
        wrapped, name=name, grid=grid,
        in_specs=list(in_specs) + [any_spec] * sum(ex_in),
        out_specs=list(out_specs) + [any_spec] * sum(ex_out),
        out_shape=list(out_shape) + [s for ex in exchanges for s in ex.out_shape],
        scratch_shapes=list(scratch_shapes) + [s for ex in exchanges for s in ex.scratch],
        compiler_params=_cparams(1),
    )(*args, *[a for ex in exchanges for a in ex.operands])
    return results[:n_out], split(results[n_out:], ex_out)


def _exchange_alone(exchange, name):
    def body(flag_ref):
        flag_ref[...] = jnp.zeros_like(flag_ref)

    _, (outs,) = _call(body, [], name=name, grid=(1,), in_specs=[], out_specs=[pl.BlockSpec((8, LANES), lambda i: (0, 0))],
                       out_shape=[jax.ShapeDtypeStruct((8, LANES), F32)], exchanges=[exchange])
    return outs


def _mod_fwd(c_all, ada_w):
    cols = ada_w.shape[2]

    def body(c_ref, w_ref, mp_ref, act_ref):
        c = c_ref[...]
        act = c * jax.nn.sigmoid(c)
        act_ref[...] = act
        mp_ref[0] = _dot(act.astype(BF16), w_ref[0].astype(BF16))

    return pl.pallas_call(
        body, name="mod_fwd", grid=(DEPTH,),
        in_specs=[pl.BlockSpec((N_DEV, D_MODEL), lambda l: (0, 0)),
                  pl.BlockSpec((1, D_MODEL, cols), lambda l: (l, 0, 0))],
        out_specs=[pl.BlockSpec((1, N_DEV, cols), lambda l: (l, 0, 0)),
                   pl.BlockSpec((N_DEV, D_MODEL), lambda l: (0, 0))],
        out_shape=[jax.ShapeDtypeStruct((DEPTH, N_DEV, cols), F32), jax.ShapeDtypeStruct((N_DEV, D_MODEL), F32)],
        compiler_params=_cparams(1),
    )(c_all, ada_w)


def _ada_grad(c_act, dmod_cols):
    cols = dmod_cols.shape[2]

    def body(act_ref, dm_ref, g_ref):
        g_ref[0] = _dot(act_ref[...].astype(BF16), dm_ref[0].astype(BF16), TN)

    return pl.pallas_call(
        body, name="ada_grad", grid=(DEPTH,),
        in_specs=[pl.BlockSpec((N_DEV, D_MODEL), lambda l: (0, 0)),
                  pl.BlockSpec((1, N_DEV, cols), lambda l: (l, 0, 0))],
        out_specs=pl.BlockSpec((1, D_MODEL, cols), lambda l: (l, 0, 0)),
        out_shape=jax.ShapeDtypeStruct((DEPTH, D_MODEL, cols), F32),
        compiler_params=_cparams(1),
    )(c_act, dmod_cols)


def _rotate(t, cos, sin_lo, sin_hi):
    return t * cos + pltpu.roll(t, LANES - 8, 1) * sin_lo + pltpu.roll(t, 8, 1) * sin_hi


def _rotate_bwd(t, cos, sin_lo, sin_hi):
    return t * cos + pltpu.roll(t * sin_lo, 8, 1) + pltpu.roll(t * sin_hi, LANES - 8, 1)


def _fwd_in(x, vec, w_in_t, b_in, rot, name, exchanges=()):
    s = x.shape[0]
    ts = _seq_tile(s)

    def body(x_ref, vec_ref, w_ref, b_ref, cos_ref, lo_ref, hi_ref, q_ref, k_ref, v_ref, u_ref):
        xt = x_ref[...]
        h = xt * _rsqrt_ms(xt) * vec_ref[0:1, :] * vec_ref[1:2, :] + vec_ref[2:3, :]
        proj = _dot(h.astype(BF16), w_ref[...], NT) + b_ref[0:1, :]
        cos, lo, hi = cos_ref[...], lo_ref[...], hi_ref[...]
        for j in range(ATTN_WIDTH // LANES):
            q_ref[:, j * LANES:(j + 1) * LANES] = (
                _rotate(proj[:, j * LANES:(j + 1) * LANES], cos, lo, hi) * Q_SCALE).astype(BF16)
        k_ref[...] = _rotate(proj[:, ATTN_WIDTH:ATTN_WIDTH + KV_WIDTH], cos, lo, hi).astype(BF16)
        v_ref[...] = proj[:, ATTN_WIDTH + KV_WIDTH:ATTN_WIDTH + 2 * KV_WIDTH].astype(BF16)
        u_ref[...] = proj[:, ATTN_WIDTH + 2 * KV_WIDTH:]

    row = lambda w: pl.BlockSpec((ts, w), lambda i: (i, 0))
    return _call(
        body, [x, vec, w_in_t, b_in, *rot], name=name, grid=(s // ts,),
        in_specs=[row(D_MODEL), _resident((8, D_MODEL)), _resident((IN_WIDTH, D_MODEL)), _resident((8, IN_WIDTH)),
                  row(LANES), row(LANES), row(LANES)],
        out_specs=[row(ATTN_WIDTH), row(KV_WIDTH), row(KV_WIDTH), row(POOL_WIDTH)],
        out_shape=[jax.ShapeDtypeStruct((s, ATTN_WIDTH), BF16), jax.ShapeDtypeStruct((s, KV_WIDTH), BF16),
                   jax.ShapeDtypeStruct((s, KV_WIDTH), BF16), jax.ShapeDtypeStruct((s, POOL_WIDTH), F32)],
        exchanges=exchanges)


def _band_mask(first_block):
    row = lax.broadcasted_iota(jnp.int32, (BLOCK, 2 * BLOCK), 0)
    col = lax.broadcasted_iota(jnp.int32, (BLOCK, 2 * BLOCK), 1)
    d = col - row
    return (d >= 1) & (d <= BLOCK) & ((col >= BLOCK) | jnp.logical_not(first_block))


def _head(ref, h):
    return ref[:, h * HEAD_DIM:(h + 1) * HEAD_DIM]


def _kv_cat(prev_ref, cur_ref, kh):
    return jnp.concatenate([_head(prev_ref, kh), _head(cur_ref, kh)], axis=0)


def _attn_fwd(sinks, q, k, v, name, exchanges=()):
    s = q.shape[0]
    nb = s // BLOCK
    cur = lambda w: pl.BlockSpec((BLOCK, w), lambda i: (i, 0))
    prev = lambda w: pl.BlockSpec((BLOCK, w), lambda i: (jnp.maximum(i - 1, 0), 0))

    def body(sink_ref, q_ref, kc_ref, kp_ref, vc_ref, vp_ref, o_ref, l_ref):
        valid = _band_mask(pl.program_id(0) == 0)
        for kh in range(N_KV_HEADS):
            kcat, vcat = _kv_cat(kp_ref, kc_ref, kh), _kv_cat(vp_ref, vc_ref, kh)
            for g in range(GROUP):
                h = kh * GROUP + g
                sc = jnp.where(valid, _dot(_head(q_ref, h), kcat, NT), NEG_INF)
                sink = sink_ref[h]
                m = jnp.maximum(jnp.max(sc, axis=1, keepdims=True), sink)
                p = jnp.exp(sc - m)
                den = jnp.sum(p, axis=1, keepdims=True) + jnp.exp(sink - m)
                o_ref[:, h * HEAD_DIM:(h + 1) * HEAD_DIM] = _dot((p / den).astype(BF16), vcat).astype(BF16)
                l_ref[:, h:h + 1] = m + jnp.log(den)

    return _call(
        body, [sinks, q, k, k, v, v], name=name, grid=(nb,),
        in_specs=[pl.BlockSpec(memory_space=pltpu.SMEM), cur(ATTN_WIDTH), cur(KV_WIDTH), prev(KV_WIDTH),
                  cur(KV_WIDTH), prev(KV_WIDTH)],
        out_specs=[cur(ATTN_WIDTH), cur(N_HEADS)],
        out_shape=[jax.ShapeDtypeStruct((s, ATTN_WIDTH), BF16), jax.ShapeDtypeStruct((s, N_HEADS), F32)],
        exchanges=exchanges)


def _pool_counts(tile, ts):
    t = (tile * ts + lax.broadcasted_iota(jnp.int32, (ts, 1), 0) + 1).astype(F32)
    return [jnp.minimum(t, float(w)) for w in POOL_WINDOWS]


def _pooled(u, halo, counts, gi):
    cols = slice(gi * POOL_GROUP_WIDTH, (gi + 1) * POOL_GROUP_WIDTH)
    acc = jnp.concatenate([halo[:, cols], u[:, cols]], axis=0)
    shift = 1
    while shift < POOL_WINDOWS[gi]:
        acc = acc + pltpu.roll(acc, shift, 0)
        shift *= 2
    return acc[POOL_HALO:, :] / counts[gi] - u[:, cols]


def _fwd_out(ao, u, pool_w, psc, w_out, x, vec, name):
    s = x.shape[0]
    ts = _seq_tile(s)
    hb = ts // POOL_HALO

    def body(ao_ref, u_ref, uh_ref, pw_ref, psc_ref, w_ref, x_ref, vec_ref, x1_ref, mix_ref):
        i = pl.program_id(0)
        u_t = u_ref[...]
        halo = jnp.where(i > 0, uh_ref[...], 0.0)
        counts = _pool_counts(i, ts)
        mix = _dot(ao_ref[...], w_ref[0:ATTN_WIDTH, :])
        for gi in range(len(POOL_WINDOWS)):
            cols = slice(gi * POOL_GROUP_WIDTH, (gi + 1) * POOL_GROUP_WIDTH)
            og = _dot(_pooled(u_t, halo, counts, gi).astype(BF16), pw_ref[gi]) * psc_ref[0:1, cols]
            mix = mix + _dot(og.astype(BF16), w_ref[ATTN_WIDTH + gi * POOL_GROUP_WIDTH:
                                                    ATTN_WIDTH + (gi + 1) * POOL_GROUP_WIDTH, :])
        mix_ref[...] = mix
        x1_ref[...] = x_ref[...] + vec_ref[0:1, :] * (mix * _rsqrt_ms(mix) * vec_ref[1:2, :])

    row = lambda w: pl.BlockSpec((ts, w), lambda i: (i, 0))
    return pl.pallas_call(
        body, name=name, grid=(s // ts,),
        in_specs=[row(ATTN_WIDTH), row(POOL_WIDTH),
                  pl.BlockSpec((POOL_HALO, POOL_WIDTH), lambda i: (jnp.maximum(i * hb - 1, 0), 0)),
                  _resident(pool_w.shape), _resident((8, POOL_WIDTH)), _resident((D_MODEL, D_MODEL)),
                  row(D_MODEL), _resident((8, D_MODEL))],
        out_specs=[row(D_MODEL), row(D_MODEL)],
        out_shape=[jax.ShapeDtypeStruct((s, D_MODEL), F32), jax.ShapeDtypeStruct((s, D_MODEL), F32)],
        compiler_params=_cparams(1),
    )(ao, u, u, pool_w, psc, w_out, x, vec)


def _ffn_fwd(x1, vec, w_gate_t, w_up_t, w_down, name, exchanges=(), target=None):
    s = x1.shape[0]
    ts = _seq_tile(s)

    def body(x_ref, *refs):
        if target is None:
            vec_ref, wg_ref, wu_ref, wd_ref, x2_ref, f_ref, gt_ref, up_ref, act_ref, h2_ref = refs
        else:
            t_ref, vec_ref, wg_ref, wu_ref, wd_ref, x2_ref, f_ref, gt_ref, up_ref, act_ref, h2_ref, loss_ref = refs
        xt = x_ref[...]
        h2 = (xt * _rsqrt_ms(xt) * vec_ref[0:1, :] * vec_ref[1:2, :] + vec_ref[2:3, :]).astype(BF16)
        h2_ref[...] = h2
        f = jnp.zeros((ts, D_MODEL), F32)
        for c in range(D_FF // FF_CHUNK):
            cols = slice(c * FF_CHUNK, (c + 1) * FF_CHUNK)
            g = _dot(h2, wg_ref[cols, :], NT)
            up = _dot(h2, wu_ref[cols, :], NT)
            act = ((g * jax.nn.sigmoid(g)) * up).astype(BF16)
            gt_ref[:, cols] = g.astype(BF16)
            up_ref[:, cols] = up.astype(BF16)
            act_ref[:, cols] = act
            f = f + _dot(act, wd_ref[cols, :])
        f_ref[...] = f
        x2 = xt + vec_ref[3:4, :] * (f * _rsqrt_ms(f) * vec_ref[4:5, :])
        if target is None:
            x2_ref[...] = x2
        else:
            @pl.when(pl.program_id(0) == 0)
            def _():
                loss_ref[...] = jnp.zeros_like(loss_ref)

            diff = x2 - t_ref[...]
            x2_ref[...] = diff / D_MODEL
            loss_ref[...] += 0.5 * jnp.sum(jnp.mean(diff * diff, axis=-1, keepdims=True))

    row = lambda w: pl.BlockSpec((ts, w), lambda i: (i, 0))
    wide = jax.ShapeDtypeStruct((s, D_FF), BF16)
    with_loss = target is not None
    return _call(
        body, [x1] + ([target] if with_loss else []) + [vec, w_gate_t, w_up_t, w_down], name=name, grid=(s // ts,),
        in_specs=[row(D_MODEL)] * (2 if with_loss else 1) + [
            _resident((8, D_MODEL)), _resident((D_FF, D_MODEL)), _resident((D_FF, D_MODEL)), _resident((D_FF, D_MODEL))],
        out_specs=[row(D_MODEL), row(D_MODEL), row(D_FF), row(D_FF), row(D_FF), row(D_MODEL)] + (
            [pl.BlockSpec((8, LANES), lambda i: (0, 0))] if with_loss else []),
        out_shape=[jax.ShapeDtypeStruct((s, D_MODEL), F32), jax.ShapeDtypeStruct((s, D_MODEL), F32), wide, wide, wide,
                   jax.ShapeDtypeStruct((s, D_MODEL), BF16)] + (
            [jax.ShapeDtypeStruct((8, LANES), F32)] if with_loss else []),
        exchanges=exchanges)


def _norm_bwd(dy_hat, x_hat, r):
    return r * (dy_hat - x_hat * jnp.mean(dy_hat * x_hat, axis=-1, keepdims=True))


def _accumulate_rows(ref, rows):
    for j, val in enumerate(rows):
        ref[j:j + 1, :] += val


def _ffn_bwd_act(dx2, f, gt, up, vec, w_down, name, exchanges=()):
    s = dx2.shape[0]
    ts = _seq_tile(s)

    def body(dx2_ref, f_ref, gt_ref, up_ref, vec_ref, wd_ref, dgt_ref, dup_ref, df_ref, cs_ref):
        @pl.when(pl.program_id(0) == 0)
        def _():
            cs_ref[...] = jnp.zeros_like(cs_ref)

        dx2_t, f_t = dx2_ref[...], f_ref[...]
        gate, g_post = vec_ref[3:4, :], vec_ref[4:5, :]
        rf = _rsqrt_ms(f_t)
        f_hat = f_t * rf
        df = _norm_bwd(dx2_t * gate * g_post, f_hat, rf).astype(BF16)
        df_ref[...] = df
        _accumulate_rows(cs_ref, [_colsum(dx2_t * (f_hat * g_post)), _colsum(dx2_t * gate * f_hat)])
        for c in range(D_FF // FF_CHUNK):
            cols = slice(c * FF_CHUNK, (c + 1) * FF_CHUNK)
            dact = _dot(df, wd_ref[cols, :], NT)
            g, u_t = gt_ref[:, cols].astype(F32), up_ref[:, cols].astype(F32)
            sg = jax.nn.sigmoid(g)
            dgt_ref[:, cols] = (dact * u_t * (sg * (1.0 + g * (1.0 - sg)))).astype(BF16)
            dup_ref[:, cols] = (dact * (g * sg)).astype(BF16)

    row = lambda w: pl.BlockSpec((ts, w), lambda i: (i, 0))
    wide = jax.ShapeDtypeStruct((s, D_FF), BF16)
    return _call(
        body, [dx2, f, gt, up, vec, w_down], name=name, grid=(s // ts,),
        in_specs=[row(D_MODEL), row(D_MODEL), row(D_FF), row(D_FF), _resident((8, D_MODEL)), _resident((D_FF, D_MODEL))],
        out_specs=[row(D_FF), row(D_FF), row(D_MODEL), pl.BlockSpec((8, D_MODEL), lambda i: (0, 0))],
        out_shape=[wide, wide, jax.ShapeDtypeStruct((s, D_MODEL), BF16), jax.ShapeDtypeStruct((8, D_MODEL), F32)],
        exchanges=exchanges)


def _ffn_bwd_in(dx2, x1, dgt, dup, vec, w_gate_t, w_up_t, name, exchanges=()):
    s = x1.shape[0]
    ts = _seq_tile(s)

    def body(dx2_ref, x1_ref, dgt_ref, dup_ref, vec_ref, wg_ref, wu_ref, dx1_ref, cs_ref):
        @pl.when(pl.program_id(0) == 0)
        def _():
            cs_ref[...] = jnp.zeros_like(cs_ref)

        g_pre, one_scale = vec_ref[0:1, :], vec_ref[1:2, :]
        dh2 = _dot(dgt_ref[...], wg_ref[...]) + _dot(dup_ref[...], wu_ref[...])
        x1_t = x1_ref[...]
        r1 = _rsqrt_ms(x1_t)
        x_hat = x1_t * r1
        dx1_ref[...] = dx2_ref[...] + _norm_bwd(dh2 * g_pre * one_scale, x_hat, r1)
        _accumulate_rows(cs_ref, [_colsum(dh2), _colsum(dh2 * (x_hat * g_pre)), _colsum(dh2 * one_scale * x_hat)])

    row = lambda w: pl.BlockSpec((ts, w), lambda i: (i, 0))
    return _call(
        body, [dx2, x1, dgt, dup, vec, w_gate_t, w_up_t], name=name, grid=(s // ts,),
        in_specs=[row(D_MODEL), row(D_MODEL), row(D_FF), row(D_FF), _resident((8, D_MODEL)),
                  _resident((D_FF, D_MODEL)), _resident((D_FF, D_MODEL))],
        out_specs=[row(D_MODEL), pl.BlockSpec((8, D_MODEL), lambda i: (0, 0))],
        out_shape=[jax.ShapeDtypeStruct((s, D_MODEL), F32), jax.ShapeDtypeStruct((8, D_MODEL), F32)],
        exchanges=exchanges)


def _weight_grad(a, b, name, exchanges=()):
    s, m = a.shape
    n = b.shape[1]
    tk = min(s, 2 * SEQ_TILE)
    steps = s // tk

    def body(a_ref, b_ref, o_ref, acc_ref):
        i = pl.program_id(0)

        @pl.when(i == 0)
        def _():
            acc_ref[...] = jnp.zeros_like(acc_ref)

        b_t = b_ref[...]
        for c in range(m // FF_CHUNK):
            rows = slice(c * FF_CHUNK, (c + 1) * FF_CHUNK)
            acc_ref[rows, :] += _dot(a_ref[:, rows], b_t, TN)

        @pl.when(i == steps - 1)
        def _():
            o_ref[...] = acc_ref[...].astype(BF16)

    (grad,), ex_outs = _call(
        body, [a, b], name=name, grid=(steps,),
        in_specs=[pl.BlockSpec((tk, m), lambda i: (i, 0)), pl.BlockSpec((tk, n), lambda i: (i, 0))],
        out_specs=[_resident((m, n))],
        out_shape=[jax.ShapeDtypeStruct((m, n), BF16)],
        scratch_shapes=[pltpu.VMEM((m, n), F32)],
        exchanges=exchanges)
    return grad, ex_outs


def _out_bwd(dx1, mix, ao, u, pool_w, psc, w_out, vec, name):
    s = dx1.shape[0]
    ts = _seq_tile(s)
    nt = s // ts
    hb = ts // POOL_HALO
    ng = len(POOL_WINDOWS)

    def body(dx1_ref, mix_ref, ao_ref, u_ref, uh_ref, pw_ref, psc_ref, w_ref, vec_ref,
             do_ref, du_ref, cat_ref, dmix_ref, cs_ref, dpw_ref, dpsc_ref, carry_ref):
        i = pl.program_id(0)
        tile = nt - 1 - i

        @pl.when(i == 0)
        def _():
            cs_ref[...] = jnp.zeros_like(cs_ref)
            dpw_ref[...] = jnp.zeros_like(dpw_ref)
            dpsc_ref[...] = jnp.zeros_like(dpsc_ref)
            carry_ref[...] = jnp.zeros_like(carry_ref)

        dx1_t, mix_t = dx1_ref[...], mix_ref[...]
        gate, g_post = vec_ref[0:1, :], vec_ref[1:2, :]
        rm = _rsqrt_ms(mix_t)
        m_hat = mix_t * rm
        dmix = _norm_bwd(dx1_t * gate * g_post, m_hat, rm).astype(BF16)
        dmix_ref[...] = dmix
        _accumulate_rows(cs_ref, [_colsum(dx1_t * (m_hat * g_post)), _colsum(dx1_t * gate * m_hat)])
        do_ref[...] = _dot(dmix, w_ref[0:ATTN_WIDTH, :], NT).astype(BF16)
        cat_ref[:, 0:ATTN_WIDTH] = ao_ref[...]

        u_t = u_ref[...]
        halo = jnp.where(tile > 0, uh_ref[...], 0.0)
        counts = _pool_counts(tile, ts)
        for gi in range(ng):
            cols = slice(gi * POOL_GROUP_WIDTH, (gi + 1) * POOL_GROUP_WIDTH)
            wide_cols = slice(ATTN_WIDTH + gi * POOL_GROUP_WIDTH, ATTN_WIDTH + (gi + 1) * POOL_GROUP_WIDTH)
            scale = psc_ref[0:1, cols]
            pooled = _pooled(u_t, halo, counts, gi).astype(BF16)
            og = _dot(pooled, pw_ref[gi])
            cat_ref[:, wide_cols] = (og * scale).astype(BF16)
            d_out = _dot(dmix, w_ref[wide_cols, :], NT)
            dpsc_ref[0:1, cols] += _colsum(d_out * og)
            d_og = (d_out * scale).astype(BF16)
            dpw_ref[gi] += _dot(pooled, d_og, TN)
            d_pooled = _dot(d_og, pw_ref[gi], NT)
            spread = d_pooled / counts[gi]
            acc = jnp.concatenate([spread, carry_ref[:, cols]], axis=0)
            shift = 1
            while shift < POOL_WINDOWS[gi]:
                acc = acc + pltpu.roll(acc, ts + POOL_HALO - shift, 0)
                shift *= 2
            du_ref[:, cols] = acc[0:ts, :] - d_pooled
            carry_ref[:, cols] = spread[0:POOL_HALO, :]

    row = lambda w: pl.BlockSpec((ts, w), lambda i: (nt - 1 - i, 0))
    fixed = lambda shape: pl.BlockSpec(shape, lambda i: (0,) * len(shape))
    return pl.pallas_call(
        body, name=name, grid=(nt,),
        in_specs=[row(D_MODEL), row(D_MODEL), row(ATTN_WIDTH), row(POOL_WIDTH),
                  pl.BlockSpec((POOL_HALO, POOL_WIDTH), lambda i: (jnp.maximum((nt - 1 - i) * hb - 1, 0), 0)),
                  _resident(pool_w.shape), _resident((8, POOL_WIDTH)), _resident((D_MODEL, D_MODEL)),
                  _resident((8, D_MODEL))],
        out_specs=[row(ATTN_WIDTH), row(POOL_WIDTH), row(D_MODEL), row(D_MODEL), fixed((8, D_MODEL)),
                   fixed(pool_w.shape), fixed((8, POOL_WIDTH))],
        out_shape=[jax.ShapeDtypeStruct((s, ATTN_WIDTH), BF16), jax.ShapeDtypeStruct((s, POOL_WIDTH), F32),
                   jax.ShapeDtypeStruct((s, D_MODEL), BF16), jax.ShapeDtypeStruct((s, D_MODEL), BF16),
                   jax.ShapeDtypeStruct((8, D_MODEL), F32), jax.ShapeDtypeStruct(pool_w.shape, F32),
                   jax.ShapeDtypeStruct((8, POOL_WIDTH), F32)],
        scratch_shapes=[pltpu.VMEM((POOL_HALO, POOL_WIDTH), F32)],
        compiler_params=_cparams(1),
    )(dx1, mix, ao, u, u, pool_w, psc, w_out, vec)


def _attn_bwd(sinks, q, do, lse, k, v, name, exchanges=()):
    s = q.shape[0]
    nb = s // BLOCK
    cur = lambda w: pl.BlockSpec((BLOCK, w), lambda i: (i, 0))
    prev = lambda w: pl.BlockSpec((BLOCK, w), lambda i: (jnp.maximum(i - 1, 0), 0))

    def body(sink_ref, q_ref, do_ref, l_ref, kc_ref, kp_ref, vc_ref, vp_ref,
             dq_ref, dkc_ref, dkp_ref, dvc_ref, dvp_ref, dsink_ref):
        @pl.when(pl.program_id(0) == 0)
        def _():
            dsink_ref[...] = jnp.zeros_like(dsink_ref)

        valid = _band_mask(pl.program_id(0) == 0)
        for kh in range(N_KV_HEADS):
            kcat, vcat = _kv_cat(kp_ref, kc_ref, kh), _kv_cat(vp_ref, vc_ref, kh)
            dk = jnp.zeros((2 * BLOCK, HEAD_DIM), F32)
            dv = jnp.zeros((2 * BLOCK, HEAD_DIM), F32)
            for g in range(GROUP):
                h = kh * GROUP + g
                qh, doh, lse_h = _head(q_ref, h), _head(do_ref, h), l_ref[:, h:h + 1]
                p = jnp.where(valid, jnp.exp(_dot(qh, kcat, NT) - lse_h), 0.0)
                dp = _dot(doh, vcat, NT)
                delta = jnp.sum(p * dp, axis=1, keepdims=True)
                dsink_ref[h:h + 1, :] += -jnp.sum(jnp.exp(sink_ref[h] - lse_h) * delta)
                ds = (p * (dp - delta)).astype(BF16)
                dq_ref[:, h * HEAD_DIM:(h + 1) * HEAD_DIM] = _dot(ds, kcat) * Q_SCALE
                dk = dk + _dot(ds, qh, TN)
                dv = dv + _dot(p.astype(BF16), doh, TN)
            hcols = slice(kh * HEAD_DIM, (kh + 1) * HEAD_DIM)
            dkp_ref[:, hcols] = dk[0:BLOCK, :]
            dkc_ref[:, hcols] = dk[BLOCK:, :]
            dvp_ref[:, hcols] = dv[0:BLOCK, :]
            dvc_ref[:, hcols] = dv[BLOCK:, :]

    kv = jax.ShapeDtypeStruct((s, KV_WIDTH), F32)
    return _call(
        body, [sinks, q, do, lse, k, k, v, v], name=name, grid=(nb,),
        in_specs=[pl.BlockSpec(memory_space=pltpu.SMEM), cur(ATTN_WIDTH), cur(ATTN_WIDTH), cur(N_HEADS),
                  cur(KV_WIDTH), prev(KV_WIDTH), cur(KV_WIDTH), prev(KV_WIDTH)],
        out_specs=[cur(ATTN_WIDTH), cur(KV_WIDTH), cur(KV_WIDTH), cur(KV_WIDTH), cur(KV_WIDTH),
                   pl.BlockSpec((8, LANES), lambda i: (0, 0))],
        out_shape=[jax.ShapeDtypeStruct((s, ATTN_WIDTH), F32), kv, kv, kv, kv, jax.ShapeDtypeStruct((8, LANES), F32)],
        exchanges=exchanges)


def _in_bwd(dq, dkc, dkp, dvc, dvp, du, x, dx1, vec, w_in_t, rot, name):
    s = x.shape[0]
    ts = _seq_tile(s)
    nt = s // ts
    bpt = ts // BLOCK

    def body(dq_ref, dkc_ref, dkp_ref, dkn_ref, dvc_ref, dvp_ref, dvn_ref, du_ref, x_ref, dx1_ref, vec_ref, w_ref,
             cos_ref, lo_ref, hi_ref, dx_ref, dproj_ref, h_ref, cs_ref, db_ref):
        i = pl.program_id(0)

        @pl.when(i == 0)
        def _():
            cs_ref[...] = jnp.zeros_like(cs_ref)
            db_ref[...] = jnp.zeros_like(db_ref)

        cos, lo, hi = cos_ref[...], lo_ref[...], hi_ref[...]

        def with_next_block(cur_ref, prev_ref, next_ref):
            nxt = jnp.where(i < nt - 1, next_ref[...], 0.0)
            later = nxt if bpt == 1 else jnp.concatenate([prev_ref[BLOCK:, :], nxt], axis=0)
            return cur_ref[...] + later

        pieces = [_rotate_bwd(dq_ref[:, j * LANES:(j + 1) * LANES], cos, lo, hi) for j in range(ATTN_WIDTH // LANES)]
        pieces.append(_rotate_bwd(with_next_block(dkc_ref, dkp_ref, dkn_ref), cos, lo, hi))
        pieces.append(with_next_block(dvc_ref, dvp_ref, dvn_ref))
        pieces.append(du_ref[...])
        dproj = jnp.concatenate(pieces, axis=1)
        db_ref[0:1, :] += _colsum(dproj)
        dproj_b = dproj.astype(BF16)
        dproj_ref[...] = dproj_b
        dh = _dot(dproj_b, w_ref[...])

        xt = x_ref[...]
        g_pre, one_scale = vec_ref[0:1, :], vec_ref[1:2, :]
        r = _rsqrt_ms(xt)
        x_hat = xt * r
        h_ref[...] = (x_hat * g_pre * one_scale + vec_ref[2:3, :]).astype(BF16)
        dx_ref[...] = dx1_ref[...] + _norm_bwd(dh * g_pre * one_scale, x_hat, r)
        _accumulate_rows(cs_ref, [_colsum(dh), _colsum(dh * (x_hat * g_pre)), _colsum(dh * one_scale * x_hat)])

    row = lambda w: pl.BlockSpec((ts, w), lambda i: (i, 0))
    nxt = pl.BlockSpec((BLOCK, KV_WIDTH), lambda i: (jnp.minimum((i + 1) * bpt, s // BLOCK - 1), 0))
    fixed = lambda shape: pl.BlockSpec(shape, lambda i: (0,) * len(shape))
    return pl.pallas_call(
        body, name=name, grid=(nt,),
        in_specs=[row(ATTN_WIDTH), row(KV_WIDTH), row(KV_WIDTH), nxt, row(KV_WIDTH), row(KV_WIDTH), nxt,
                  row(POOL_WIDTH), row(D_MODEL), row(D_MODEL), _resident((8, D_MODEL)), _resident((IN_WIDTH, D_MODEL)),
                  row(LANES), row(LANES), row(LANES)],
        out_specs=[row(D_MODEL), row(IN_WIDTH), row(D_MODEL), fixed((8, D_MODEL)), fixed((8, IN_WIDTH))],
        out_shape=[jax.ShapeDtypeStruct((s, D_MODEL), F32), jax.ShapeDtypeStruct((s, IN_WIDTH), BF16),
                   jax.ShapeDtypeStruct((s, D_MODEL), BF16), jax.ShapeDtypeStruct((8, D_MODEL), F32),
                   jax.ShapeDtypeStruct((8, IN_WIDTH), F32)],
        compiler_params=_cparams(1),
    )(dq, dkc, dkp, dkp, dvc, dvp, dvp, du, x, dx1, vec, w_in_t, *rot)


def _row_tile(rows):
    for t in (512, 352, 256, 176, 160, 128, 64, 32, 16, 8):
        if rows % t == 0:
            return t
    raise ValueError(f"no row tile for {rows} rows")


def _sum_parts(own, others, name):
    r, c = own.shape
    n_other = others.shape[0]
    tr = _row_tile(r)

    def body(own_ref, oth_ref, o_ref):
        acc = own_ref[...].astype(F32)
        for k in range(n_other):
            acc = acc + oth_ref[k].astype(F32)
        o_ref[...] = acc

    return pl.pallas_call(
        body, name=name, grid=(r // tr,),
        in_specs=[pl.BlockSpec((tr, c), lambda i: (i, 0)), pl.BlockSpec((n_other, tr, c), lambda i: (0, i, 0))],
        out_specs=pl.BlockSpec((tr, c), lambda i: (i, 0)),
        out_shape=jax.ShapeDtypeStruct((r, c), F32),
        compiler_params=_cparams(1),
    )(own, others)


def _adamw(w, g, m, v, name):
    r, c = w.shape
    tr = _row_tile(r)

    def body(w_ref, g_ref, m_ref, v_ref, d_ref, nm_ref, nv_ref):
        g_t = g_ref[...]
        m_new = ADAM_B1 * m_ref[...] + (1.0 - ADAM_B1) * g_t
        v_new = ADAM_B2 * v_ref[...] + (1.0 - ADAM_B2) * (g_t * g_t)
        m_hat = m_new / (1.0 - ADAM_B1 ** ADAM_STEP)
        v_hat = v_new / (1.0 - ADAM_B2 ** ADAM_STEP)
        d_ref[...] = -ADAM_LR * (m_hat / (jnp.sqrt(v_hat) + ADAM_EPS) + ADAM_WD * w_ref[...])
        nm_ref[...] = m_new
        nv_ref[...] = v_new

    spec = pl.BlockSpec((tr, c), lambda i: (i, 0))
    shape = jax.ShapeDtypeStruct((r, c), F32)
    return pl.pallas_call(
        body, name=name, grid=(r // tr,), in_specs=[spec] * 4, out_specs=[spec] * 3, out_shape=[shape] * 3,
        compiler_params=_cparams(1),
    )(w, g, m, v)


SMALL_ROWS = 160


def _pack_small(dmod, gains, b_in, pool_scale, sinks, pool_w):
    pad = lambda a, width: jnp.pad(a, ((0, 0), (0, width - a.shape[1])))
    parts = [dmod.reshape(DEPTH * N_MOD, D_MODEL), gains.reshape(4 * DEPTH, D_MODEL),
             pad(b_in, 2 * D_MODEL).reshape(2 * DEPTH, D_MODEL), pool_scale.reshape(1, D_MODEL),
             pad(sinks.reshape(1, DEPTH * N_HEADS), D_MODEL), pool_w.reshape(-1, D_MODEL)]
    packed = jnp.concatenate(parts, axis=0)
    return jnp.pad(packed, ((0, SMALL_ROWS - packed.shape[0]), (0, 0)))


def _unpack_small(p):
    r = DEPTH * N_MOD
    dmod = p[0:r].reshape(DEPTH, N_MOD * D_MODEL)
    gains = p[r:r + 4 * DEPTH].reshape(4, DEPTH, D_MODEL)
    r += 4 * DEPTH
    b_in = p[r:r + 2 * DEPTH].reshape(DEPTH, 2 * D_MODEL)[:, :IN_WIDTH]
    r += 2 * DEPTH
    pool_scale = p[r].reshape(DEPTH, POOL_WIDTH)
    sinks = p[r + 1, :DEPTH * N_HEADS].reshape(DEPTH, N_HEADS)
    pool_w = p[r + 2:r + 2 + 128].reshape(DEPTH, len(POOL_WINDOWS), POOL_GROUP_WIDTH, POOL_GROUP_WIDTH)
    return dmod, gains, b_in, pool_scale, sinks, pool_w


LAYER_ROWS = 80


def _pack_layer(dmod, gains, b_in, pool_scale, sinks, pool_w):
    misc = jnp.concatenate([pool_scale, sinks, jnp.zeros((D_MODEL - POOL_WIDTH - N_HEADS,), F32)])
    parts = [dmod.reshape(N_MOD, D_MODEL), gains, jnp.pad(b_in, (0, 2 * D_MODEL - IN_WIDTH)).reshape(2, D_MODEL),
             misc.reshape(1, D_MODEL), pool_w.reshape(-1, D_MODEL)]
    packed = jnp.concatenate(parts, axis=0)
    return jnp.pad(packed, ((0, LAYER_ROWS - packed.shape[0]), (0, 0)))


def _unpack_layer(p):
    r = N_MOD + 4
    b_in = p[r:r + 2].reshape(2 * D_MODEL)[:IN_WIDTH]
    pool_w = p[r + 3:r + 3 + 64].reshape(len(POOL_WINDOWS), POOL_GROUP_WIDTH, POOL_GROUP_WIDTH)
    return (p[0:N_MOD].reshape(N_MOD * D_MODEL), p[N_MOD:r], b_in, p[r + 2, :POOL_WIDTH],
            p[r + 2, POOL_WIDTH:POOL_WIDTH + N_HEADS], pool_w)


def _rotary_tables(positions):
    inv_freq = ROPE_THETA ** (-jnp.arange(0, ROT_DIM, 2, dtype=F32) / ROT_DIM)
    half = ROT_DIM // 2
    head_freq = jnp.concatenate([inv_freq, inv_freq, jnp.zeros((HEAD_DIM - ROT_DIM,), F32)])
    lane_freq = jnp.concatenate([head_freq, head_freq])
    ang = positions.reshape(-1).astype(F32)[:, None] * lane_freq[None, :]
    cos, sin = jnp.cos(ang), jnp.sin(ang)
    in_head = lax.broadcasted_iota(jnp.int32, (1, LANES), 1) % HEAD_DIM
    return cos, jnp.where(in_head < half, -sin, 0.0), jnp.where((in_head >= half) & (in_head < ROT_DIM), sin, 0.0)


def kernel(x, c, positions, ada_w, ada_b, w_in, b_in, sinks, pool_w, pool_scale, w_out, w_gate, w_up, w_down, g_pre_mix, g_post_mix, g_pre_ffn, g_post_ffn, loss_target, m_ada_w, m_ada_b, m_w_in, m_b_in, m_sinks, m_pool_w, m_pool_scale, m_w_out, m_w_gate, m_w_up, m_w_down, m_g_pre_mix, m_g_post_mix, m_g_pre_ffn, m_g_post_ffn, v_ada_w, v_ada_b, v_w_in, v_b_in, v_sinks, v_pool_w, v_pool_scale, v_w_out, v_w_gate, v_w_up, v_w_down, v_g_pre_mix, v_g_post_mix, v_g_pre_ffn, v_g_post_ffn):
    me = _index(_mesh_place())
    x0 = x[0]
    target = loss_target[0]
    rot = _rotary_tables(positions)
    ada_cols = ada_w.shape[2]

    c_all = _allgather_vmem(jnp.broadcast_to(c, (8, D_MODEL)), "gather_c")[:, 0, :]
    mod_part, c_act = _mod_fwd(c_all, ada_w)
    mod_all = _allgather_vmem(mod_part.reshape(DEPTH * N_DEV, ada_cols), "gather_mod")
    mod_mine = lax.dynamic_slice_in_dim(mod_all.reshape(N_DEV, DEPTH, N_DEV, ada_cols), me, 1, axis=2)[:, :, 0, :]
    mod = jnp.transpose(mod_mine, (1, 0, 2)).reshape(DEPTH, N_MOD * D_MODEL) + ada_b
    mod = mod.reshape(DEPTH, N_MOD, D_MODEL)

    shards = []
    for l in range(DEPTH):
        shards += [w_in[l].T.astype(BF16), w_out[l].astype(BF16), w_gate[l].T.astype(BF16), w_up[l].T.astype(BF16),
                   w_down[l].astype(BF16)]
    whole = lambda gathered: [g.reshape(-1, D_MODEL) for g in gathered]
    full = [None] * (5 * DEPTH)
    full[0:2] = whole(_exchange_alone(_Gather(shards[0:2]), "gather_w_in_out_0"))
    hosted = {("fwd_in", 0): (_Gather(shards[4:5]), [4]), ("attn_fwd", 0): (_Gather(shards[2:4]), [2, 3]),
              ("ffn_fwd", 0): (_Gather(shards[5:10]), [5, 6, 7, 8, 9])}

    def take(kind, l, ex_outs):
        if (kind, l) in hosted:
            for slot, g in zip(hosted[kind, l][1], whole(ex_outs[0])):
                full[slot] = g

    def beside(kind, l):
        return [hosted[kind, l][0]] if (kind, l) in hosted else []

    pool_w_b = pool_w.astype(BF16)

    saved = []
    xl = x0
    for l in range(DEPTH):
        vec_in = _rows(g_pre_mix[l], 1.0 + mod[l, 1], mod[l, 0], width=D_MODEL)
        vec_out = _rows(mod[l, 2], g_post_mix[l], width=D_MODEL)
        vec_ffn = _rows(g_pre_ffn[l], 1.0 + mod[l, 4], mod[l, 3], mod[l, 5], g_post_ffn[l], width=D_MODEL)
        psc = _rows(pool_scale[l], width=POOL_WIDTH)
        (q, k, v, u), ex = _fwd_in(xl, vec_in, full[5 * l], _rows(b_in[l], width=IN_WIDTH), rot, f"fwd_in_{l}",
                                   beside("fwd_in", l))
        take("fwd_in", l, ex)
        (ao, lse), ex = _attn_fwd(sinks[l], q, k, v, f"attn_fwd_{l}", beside("attn_fwd", l))
        take("attn_fwd", l, ex)
        x1, mix = _fwd_out(ao, u, pool_w_b[l], psc, full[5 * l + 1], xl, vec_out, f"fwd_out_{l}")
        (x2, f, gt, up, act, h2, *loss_part), ex = _ffn_fwd(
            x1, vec_ffn, full[5 * l + 2], full[5 * l + 3], full[5 * l + 4], f"ffn_fwd_{l}", beside("ffn_fwd", l),
            target=target if l == DEPTH - 1 else None)
        take("ffn_fwd", l, ex)
        saved.append((xl, q, k, v, u, ao, lse, x1, mix, f, gt, up, act, h2, vec_in, vec_out, vec_ffn, psc))
        xl = x2

    dx = xl
    loss = lax.psum(loss_part[0][0, 0], ("x", "y", "c"))

    stacks = [None] * (5 * DEPTH)
    others = [None] * (5 * DEPTH)
    small_all = [None] * DEPTH
    ready_stacks, ready_small = [], []

    def leaving():
        exs, notes = [], []
        if ready_stacks:
            exs.append(_Scatter([stacks[a] for a in ready_stacks]))
            notes.append(("stacks", list(ready_stacks)))
            ready_stacks.clear()
        if ready_small:
            exs.append(_Gather([ready_small[0][1]]))
            notes.append(("small", ready_small[0][0]))
            ready_small.clear()
        return exs, notes

    def arrived(notes, ex_outs):
        for (kind, what), outs in zip(notes, ex_outs):
            if kind == "stacks":
                for a, o in zip(what, outs):
                    others[a] = o
            else:
                small_all[what] = outs[0]

    def grad_ready(a, grad):
        stacks[a] = grad.reshape(N_DEV, -1, D_MODEL)
        ready_stacks.append(a)

    for l in reversed(range(DEPTH)):
        w_in_t, w_out_f, w_gate_t, w_up_t, w_down_f = full[5 * l:5 * l + 5]
        xin, q, k, v, u, ao, lse, x1, mix, f, gt, up, act, h2, vec_in, vec_out, vec_ffn, psc = saved[l]
        exs, notes = leaving()
        (dgt, dup, df, cs_a), ex = _ffn_bwd_act(dx, f, gt, up, vec_ffn, w_down_f, f"ffn_bwd_act_{l}", exs)
        arrived(notes, ex)
        (dx1, cs_f), _ = _ffn_bwd_in(dx, x1, dgt, dup, vec_ffn, w_gate_t, w_up_t, f"ffn_bwd_in_{l}")
        grad_ready(5 * l + 2, _weight_grad(dgt, h2, f"grad_w_gate_{l}")[0])
        grad_ready(5 * l + 3, _weight_grad(dup, h2, f"grad_w_up_{l}")[0])
        grad_ready(5 * l + 4, _weight_grad(act, df, f"grad_w_down_{l}")[0])
        do, du, cat, dmix, cs_o, dpw, dpsc = _out_bwd(dx1, mix, ao, u, pool_w_b[l], psc, w_out_f, vec_out, f"out_bwd_{l}")
        grad_ready(5 * l + 1, _weight_grad(cat, dmix, f"grad_w_out_{l}")[0])
        exs, notes = leaving()
        (dq, dkc, dkp, dvc, dvp, dsk), ex = _attn_bwd(sinks[l], q, do, lse, k, v, f"attn_bwd_{l}", exs)
        arrived(notes, ex)
        dx, dproj, h, cs_i, db = _in_bwd(dq, dkc, dkp, dvc, dvp, du, xin, dx1, vec_in, w_in_t, rot, f"in_bwd_{l}")
        d_mod = jnp.concatenate([cs_i[0], cs_i[1], cs_o[0], cs_f[0], cs_f[1], cs_a[0]])
        d_gain = jnp.stack([cs_i[2], cs_o[1], cs_f[2], cs_a[1]])
        ready_small.append((l, _pack_layer(d_mod, d_gain, db[0], dpsc[0], dsk[:, 0], dpw)))
        exs, notes = leaving() if l == 0 else ([], [])
        grad, ex = _weight_grad(dproj, h, f"grad_w_in_{l}", exs)
        arrived(notes, ex)
        grad_ready(5 * l, grad)
    grad_x = dx[None]
    exs, notes = leaving()
    arrived(notes, [_exchange_alone(ex, "scatter_last") for ex in exs])

    layer_sums = [_unpack_layer(_sum_parts(small_all[l][0], small_all[l][1:], f"sum_small_{l}")) for l in range(DEPTH)]
    g_ada_b, g_gains, g_b_in, g_pool_scale, g_sinks, g_pool_w = (
        jnp.stack([layer_sums[l][j] for l in range(DEPTH)], axis=1 if j == 1 else 0) for j in range(6))
    small_sum = _pack_small(g_ada_b, g_gains, g_b_in, g_pool_scale, g_sinks, g_pool_w)
    gains = jnp.stack([g_pre_mix, g_post_mix, g_pre_ffn, g_post_ffn])
    m_gains = jnp.stack([m_g_pre_mix, m_g_post_mix, m_g_pre_ffn, m_g_post_ffn])
    v_gains = jnp.stack([v_g_pre_mix, v_g_post_mix, v_g_pre_ffn, v_g_post_ffn])
    small_step = _adamw(_pack_small(ada_b, gains, b_in, pool_scale, sinks, pool_w), small_sum,
                        _pack_small(m_ada_b, m_gains, m_b_in, m_pool_scale, m_sinks, m_pool_w),
                        _pack_small(v_ada_b, v_gains, v_b_in, v_pool_scale, v_sinks, v_pool_w), "adamw_small")
    small_out = [_unpack_small(p) for p in small_step]

    dmod_all = jnp.stack([small_all[l][:, 0:N_MOD, :].reshape(N_DEV, N_DEV, ada_cols) for l in range(DEPTH)])
    dmod_cols = lax.dynamic_slice_in_dim(dmod_all, me, 1, axis=2)[:, :, 0, :]
    g_ada_w = _ada_grad(c_act, dmod_cols)

    summed = []
    for a, (stack, oth) in enumerate(zip(stacks, others)):
        own = lax.dynamic_index_in_dim(stack, me, axis=0, keepdims=False)
        summed.append(_sum_parts(own, oth, f"sum_grads_{a}"))
    per_kind = [jnp.stack([summed[5 * l + j] for l in range(DEPTH)]) for j in range(5)]
    g_w_in, g_w_out, g_w_gate, g_w_up, g_w_down = per_kind
    g_w_in, g_w_gate, g_w_up = (jnp.transpose(g, (0, 2, 1)) for g in (g_w_in, g_w_gate, g_w_up))

    def step(w, g, m, v, name):
        flat = lambda a: a.reshape(-1, a.shape[-1])
        return [o.reshape(w.shape) for o in _adamw(flat(w), flat(g), flat(m), flat(v), name)]

    big = {
        "ada_w": (g_ada_w, step(ada_w, g_ada_w, m_ada_w, v_ada_w, "adamw_ada_w")),
        "w_in": (g_w_in, step(w_in, g_w_in, m_w_in, v_w_in, "adamw_w_in")),
        "w_out": (g_w_out, step(w_out, g_w_out, m_w_out, v_w_out, "adamw_w_out")),
        "w_gate": (g_w_gate, step(w_gate, g_w_gate, m_w_gate, v_w_gate, "adamw_w_gate")),
        "w_up": (g_w_up, step(w_up, g_w_up, m_w_up, v_w_up, "adamw_w_up")),
        "w_down": (g_w_down, step(w_down, g_w_down, m_w_down, v_w_down, "adamw_w_down")),
    }

    def ordered(pick_big, pick_small):
        ada_b_, gains_, b_in_, pool_scale_, sinks_, pool_w_ = pick_small
        return [pick_big("ada_w"), ada_b_, pick_big("w_in"), b_in_, sinks_, pool_w_, pool_scale_, pick_big("w_out"),
                pick_big("w_gate"), pick_big("w_up"), pick_big("w_down"), gains_[0], gains_[1], gains_[2], gains_[3]]

    grads = ordered(lambda n: big[n][0], (g_ada_b, g_gains, g_b_in, g_pool_scale, g_sinks, g_pool_w))
    deltas = ordered(lambda n: big[n][1][0], small_out[0])
    new_m = ordered(lambda n: big[n][1][1], small_out[1])
    new_v = ordered(lambda n: big[n][1][2], small_out[2])
    return (loss, grad_x, *grads, *deltas, *new_m, *new_v)
```

```python
import jax
import jax.numpy as jnp
from jax import lax
from jax.experimental import pallas as pl
from jax.experimental.pallas import tpu as pltpu

F32 = jnp.float32
BF16 = jnp.bfloat16

N_DEV = 8
DEPTH = 2
D_MODEL = 1024
HEAD_DIM = 64
N_HEADS = 8
N_KV_HEADS = 2
GROUP = N_HEADS // N_KV_HEADS
ATTN_WIDTH = N_HEADS * HEAD_DIM
KV_WIDTH = N_KV_HEADS * HEAD_DIM
POOL_WIDTH = 512
POOL_WINDOWS = (2, 4, 8, 16)
POOL_GROUP_WIDTH = 128
POOL_HALO = 16
IN_WIDTH = ATTN_WIDTH + 2 * KV_WIDTH + POOL_WIDTH
D_FF = 2816
N_MOD = 6
BLOCK = 128
ROT_DIM = 16
ROPE_THETA = 500000.0
EPS = 1e-6
NEG_INF = -1e30
Q_SCALE = HEAD_DIM ** -0.5

ADAM_LR = 0.001
ADAM_B1 = 0.9
ADAM_B2 = 0.999
ADAM_EPS = 1e-08
ADAM_WD = 0.01
ADAM_STEP = 10

LANES = 128
SEQ_TILE = 512
FF_CHUNK = 256
VMEM_LIMIT = 56 * 1024 * 1024
MESH = pl.DeviceIdType.MESH

NT = (((1,), (1,)), ((), ()))
TN = (((0,), (0,)), ((), ()))


def _dot(a, b, dims=None):
    if dims is None:
        return jnp.dot(a, b, preferred_element_type=F32)
    return lax.dot_general(a, b, dims, preferred_element_type=F32)


def _cparams(n_axes):
    return pltpu.CompilerParams(dimension_semantics=("arbitrary",) * n_axes, vmem_limit_bytes=VMEM_LIMIT)


def _resident(shape):
    zeros = (0,) * len(shape)
    return pl.BlockSpec(shape, lambda *_: zeros, pipeline_mode=pl.Buffered(1))


def _rows(*vectors, width):
    rows = [jnp.reshape(v, (1, width)).astype(F32) for v in vectors]
    rows.append(jnp.zeros((8 - len(rows), width), F32))
    return jnp.concatenate(rows, axis=0)


def _rsqrt_ms(x):
    return lax.rsqrt(jnp.mean(x * x, axis=-1, keepdims=True) + EPS)


def _colsum(x):
    return jnp.sum(x, axis=0, keepdims=True)


def _seq_tile(s):
    return min(s, SEQ_TILE)


def _mesh_place():
    x, y, c = lax.axis_index("x"), lax.axis_index("y"), lax.axis_index("c")
    return x, y, c


def _flip(place, k):
    x, y, c = place
    return (1 - x if k & 4 else x, 1 - y if k & 2 else y, 1 - c if k & 1 else c)


def _index(place):
    x, y, c = place
    return 4 * x + 2 * y + c


def _allgather_vmem(block, name):
    r, c = block.shape

    def body(x_ref, out_ref, send_sems, recv_sems, local_sem):
        me = _mesh_place()
        mine = pltpu.make_async_copy(x_ref, out_ref.at[_index(me)], local_sem)
        mine.start()

        def copy(k):
            return pltpu.make_async_remote_copy(
                src_ref=x_ref, dst_ref=out_ref.at[_index(me)], send_sem=send_sems.at[k - 1], recv_sem=recv_sems.at[k - 1],
                device_id=_flip(me, k), device_id_type=MESH)

        def arrival(k):
            return pltpu.make_async_remote_copy(
                src_ref=x_ref, dst_ref=out_ref.at[_index(_flip(me, k))], send_sem=send_sems.at[k - 1],
                recv_sem=recv_sems.at[k - 1], device_id=_flip(me, k), device_id_type=MESH)

        for k in range(1, N_DEV):
            copy(k).start()
        for k in range(1, N_DEV):
            arrival(k).wait_recv()
        for k in range(1, N_DEV):
            copy(k).wait_send()
        mine.wait()

    return pl.pallas_call(
        body, name=name,
        out_shape=jax.ShapeDtypeStruct((N_DEV, r, c), block.dtype),
        in_specs=[pl.BlockSpec(memory_space=pltpu.VMEM)],
        out_specs=pl.BlockSpec(memory_space=pltpu.VMEM),
        scratch_shapes=[pltpu.SemaphoreType.DMA((N_DEV - 1,)), pltpu.SemaphoreType.DMA((N_DEV - 1,)),
                        pltpu.SemaphoreType.DMA],
    )(block)


class _Gather:
    def __init__(self, shards):
        n = len(shards)
        self.operands = list(shards)
        self.out_shape = [jax.ShapeDtypeStruct((N_DEV,) + s.shape, s.dtype) for s in shards]
        self.scratch = [pltpu.SemaphoreType.DMA((N_DEV - 1, n)), pltpu.SemaphoreType.DMA((N_DEV - 1, n)),
                        pltpu.SemaphoreType.DMA((n,))]

    def _copies(self, x_refs, out_refs, sems):
        send_sems, recv_sems, local_sems = sems
        n = len(x_refs)
        x, y, c = _mesh_place()
        me, sibling = (x, y, c), (x, y, 1 - c)
        chips = [(1 - x, y), (x, 1 - y), (1 - x, 1 - y)]

        def copy(k, a, block, to, from_input=False):
            rows = out_refs[a].at[_index(block)]
            return pltpu.make_async_remote_copy(
                src_ref=x_refs[a] if from_input else rows, dst_ref=rows,
                send_sem=send_sems.at[k, a], recv_sem=recv_sems.at[k, a], device_id=to, device_id_type=MESH)

        mine = [pltpu.make_async_copy(x_refs[a], out_refs[a].at[_index(me)], local_sems.at[a]) for a in range(n)]
        first = [copy(0, a, me, sibling, from_input=True) for a in range(n)]
        first += [copy(1 + j, a, me, (*chip, c), from_input=True) for j, chip in enumerate(chips) for a in range(n)]
        over_ici = [copy(1 + j, a, (*chip, c), me) for j, chip in enumerate(chips) for a in range(n)]
        passed = [copy(4 + j, a, (*chip, c), sibling) for j, chip in enumerate(chips) for a in range(n)]
        from_sibling = [copy(0, a, sibling, me) for a in range(n)]
        from_sibling += [copy(4 + j, a, (*chip, 1 - c), me) for j, chip in enumerate(chips) for a in range(n)]
        return mine, first, over_ici, passed, from_sibling

    def begin(self, x_refs, out_refs, sems):
        mine, first, _, _, _ = self._copies(x_refs, out_refs, sems)
        for cp in mine + first:
            cp.start()

    def middle(self, x_refs, out_refs, sems):
        _, _, over_ici, passed, _ = self._copies(x_refs, out_refs, sems)
        for arrived, onward in zip(over_ici, passed):
            arrived.wait_recv()
            onward.start()

    def end(self, x_refs, out_refs, sems):
        mine, first, _, passed, from_sibling = self._copies(x_refs, out_refs, sems)
        for cp in from_sibling:
            cp.wait_recv()
        for cp in first + passed:
            cp.wait_send()
        for cp in mine:
            cp.wait()


class _Scatter:
    def __init__(self, stacks):
        n = len(stacks)
        self.operands = list(stacks)
        self.out_shape = [jax.ShapeDtypeStruct((N_DEV - 1,) + s.shape[1:], s.dtype) for s in stacks]
        self.scratch = [pltpu.SemaphoreType.DMA((N_DEV - 1, n)), pltpu.SemaphoreType.DMA((N_DEV - 1, n))]

    def _copies(self, g_refs, out_refs, sems):
        send_sems, recv_sems = sems
        me = _mesh_place()

        def copy(k, a):
            peer = _flip(me, k)
            return pltpu.make_async_remote_copy(
                src_ref=g_refs[a].at[_index(peer)], dst_ref=out_refs[a].at[k - 1],
                send_sem=send_sems.at[k - 1, a], recv_sem=recv_sems.at[k - 1, a], device_id=peer, device_id_type=MESH)

        return [copy(k, a) for k in range(1, N_DEV) for a in range(len(g_refs))]

    def begin(self, g_refs, out_refs, sems):
        for cp in self._copies(g_refs, out_refs, sems):
            cp.start()

    def middle(self, g_refs, out_refs, sems):
        pass

    def end(self, g_refs, out_refs, sems):
        copies = self._copies(g_refs, out_refs, sems)
        for cp in copies:
            cp.wait_recv()
        for cp in copies:
            cp.wait_send()


def _call(body, args, *, name, grid, in_specs, out_specs, out_shape, scratch_shapes=(), exchanges=()):
    if not exchanges:
        outs = pl.pallas_call(body, name=name, grid=grid, in_specs=in_specs, out_specs=out_specs, out_shape=out_shape,
                              scratch_shapes=list(scratch_shapes), compiler_params=_cparams(len(grid)))(*args)
        return outs, []
    (steps,) = grid
    n_in, n_out, n_scr = len(in_specs), len(out_specs), len(scratch_shapes)
    ex_in = [len(ex.operands) for ex in exchanges]
    ex_out = [len(ex.out_shape) for ex in exchanges]
    ex_scr = [len(ex.scratch) for ex in exchanges]

    def split(refs, counts):
        parts, pos = [], 0
        for cnt in counts:
            parts.append(refs[pos:pos + cnt])
            pos += cnt
        return parts

    def wrapped(*refs):
        ins, xin, outs, xout, scr, xscr = split(refs, [n_in, sum(ex_in), n_out, sum(ex_out), n_scr, sum(ex_scr)])
        bound = list(zip(exchanges, split(xin, ex_in), split(xout, ex_out), split(xscr, ex_scr)))
        step = pl.program_id(0)

        def phase(method, at):
            @pl.when(step == at)
            def _():
                for ex, i_refs, o_refs, sems in bound:
                    getattr(ex, method)(i_refs, o_refs, sems)

        phase("begin", 0)
        phase("middle", (3 * steps) // 4)
        body(*ins, *outs, *scr)
        phase("end", steps - 1)

    any_spec = pl.BlockSpec(memory_space=pl.ANY)
    results = pl.pallas_call(
        wrapped, name=name, grid=grid,
        in_specs=list(in_specs) + [any_spec] * sum(ex_in),
        out_specs=list(out_specs) + [any_spec] * sum(ex_out),
        out_shape=list(out_shape) + [s for ex in exchanges for s in ex.out_shape],
        scratch_shapes=list(scratch_shapes) + [s for ex in exchanges for s in ex.scratch],
        compiler_params=_cparams(1),
    )(*args, *[a for ex in exchanges for a in ex.operands])
    return results[:n_out], split(results[n_out:], ex_out)


def _exchange_alone(exchange, name):
    def body(flag_ref):
        flag_ref[...] = jnp.zeros_like(flag_ref)

    _, (outs,) = _call(body, [], name=name, grid=(1,), in_specs=[], out_specs=[pl.BlockSpec((8, LANES), lambda i: (0, 0))],
                       out_shape=[jax.ShapeDtypeStruct((8, LANES), F32)], exchanges=[exchange])
    return outs


def _mod_fwd(c_all, ada_w):
    cols = ada_w.shape[2]

    def body(c_ref, w_ref, mp_ref, act_ref):
        c = c_ref[...]
        act = c * jax.nn.sigmoid(c)
        act_ref[...] = act
        mp_ref[0] = _dot(act.astype(BF16), w_ref[0].astype(BF16))

    return pl.pallas_call(
        body, name="mod_fwd", grid=(DEPTH,),
        in_specs=[pl.BlockSpec((N_DEV, D_MODEL), lambda l: (0, 0)),
                  pl.BlockSpec((1, D_MODEL, cols), lambda l: (l, 0, 0))],
        out_specs=[pl.BlockSpec((1, N_DEV, cols), lambda l: (l, 0, 0)),
                   pl.BlockSpec((N_DEV, D_MODEL), lambda l: (0, 0))],
        out_shape=[jax.ShapeDtypeStruct((DEPTH, N_DEV, cols), F32), jax.ShapeDtypeStruct((N_DEV, D_MODEL), F32)],
        compiler_params=_cparams(1),
    )(c_all, ada_w)


def _ada_grad(c_act, dmod_cols):
    cols = dmod_cols.shape[2]

    def body(act_ref, dm_ref, g_ref):
        g_ref[0] = _dot(act_ref[...].astype(BF16), dm_ref[0].astype(BF16), TN)

    return pl.pallas_call(
        body, name="ada_grad", grid=(DEPTH,),
        in_specs=[pl.BlockSpec((N_DEV, D_MODEL), lambda l: (0, 0)),
                  pl.BlockSpec((1, N_DEV, cols), lambda l: (l, 0, 0))],
        out_specs=pl.BlockSpec((1, D_MODEL, cols), lambda l: (l, 0, 0)),
        out_shape=jax.ShapeDtypeStruct((DEPTH, D_MODEL, cols), F32),
        compiler_params=_cparams(1),
    )(c_act, dmod_cols)


def _rotate(t, cos, sin_lo, sin_hi):
    return t * cos + pltpu.roll(t, LANES - 8, 1) * sin_lo + pltpu.roll(t, 8, 1) * sin_hi


def _rotate_bwd(t, cos, sin_lo, sin_hi):
    return t * cos + pltpu.roll(t * sin_lo, 8, 1) + pltpu.roll(t * sin_hi, LANES - 8, 1)


def _fwd_in(x, vec, w_in_t, b_in, rot, name, exchanges=()):
    s = x.shape[0]
    ts = _seq_tile(s)

    def body(x_ref, vec_ref, w_ref, b_ref, cos_ref, lo_ref, hi_ref, q_ref, k_ref, v_ref, u_ref):
        xt = x_ref[...]
        h = xt * _rsqrt_ms(xt) * vec_ref[0:1, :] * vec_ref[1:2, :] + vec_ref[2:3, :]
        proj = _dot(h.astype(BF16), w_ref[...], NT) + b_ref[0:1, :]
        cos, lo, hi = cos_ref[...], lo_ref[...], hi_ref[...]
        for j in range(ATTN_WIDTH // LANES):
            q_ref[:, j * LANES:(j + 1) * LANES] = (
                _rotate(proj[:, j * LANES:(j + 1) * LANES], cos, lo, hi) * Q_SCALE).astype(BF16)
        k_ref[...] = _rotate(proj[:, ATTN_WIDTH:ATTN_WIDTH + KV_WIDTH], cos, lo, hi).astype(BF16)
        v_ref[...] = proj[:, ATTN_WIDTH + KV_WIDTH:ATTN_WIDTH + 2 * KV_WIDTH].astype(BF16)
        u_ref[...] = proj[:, ATTN_WIDTH + 2 * KV_WIDTH:]

    row = lambda w: pl.BlockSpec((ts, w), lambda i: (i, 0))
    return _call(
        body, [x, vec, w_in_t, b_in, *rot], name=name, grid=(s // ts,),
        in_specs=[row(D_MODEL), _resident((8, D_MODEL)), _resident((IN_WIDTH, D_MODEL)), _resident((8, IN_WIDTH)),
                  row(LANES), row(LANES), row(LANES)],
        out_specs=[row(ATTN_WIDTH), row(KV_WIDTH), row(KV_WIDTH), row(POOL_WIDTH)],
        out_shape=[jax.ShapeDtypeStruct((s, ATTN_WIDTH), BF16), jax.ShapeDtypeStruct((s, KV_WIDTH), BF16),
                   jax.ShapeDtypeStruct((s, KV_WIDTH), BF16), jax.ShapeDtypeStruct((s, POOL_WIDTH), F32)],
        exchanges=exchanges)


def _band_mask(first_block):
    row = lax.broadcasted_iota(jnp.int32, (BLOCK, 2 * BLOCK), 0)
    col = lax.broadcasted_iota(jnp.int32, (BLOCK, 2 * BLOCK), 1)
    d = col - row
    return (d >= 1) & (d <= BLOCK) & ((col >= BLOCK) | jnp.logical_not(first_block))


def _head(ref, h):
    return ref[:, h * HEAD_DIM:(h + 1) * HEAD_DIM]


def _kv_cat(prev_ref, cur_ref, kh):
    return jnp.concatenate([_head(prev_ref, kh), _head(cur_ref, kh)], axis=0)


def _attn_fwd(sinks, q, k, v, name, exchanges=()):
    s = q.shape[0]
    nb = s // BLOCK
    cur = lambda w: pl.BlockSpec((BLOCK, w), lambda i: (i, 0))
    prev = lambda w: pl.BlockSpec((BLOCK, w), lambda i: (jnp.maximum(i - 1, 0), 0))

    def body(sink_ref, q_ref, kc_ref, kp_ref, vc_ref, vp_ref, o_ref, l_ref):
        valid = _band_mask(pl.program_id(0) == 0)
        for kh in range(N_KV_HEADS):
            kcat, vcat = _kv_cat(kp_ref, kc_ref, kh), _kv_cat(vp_ref, vc_ref, kh)
            for g in range(GROUP):
                h = kh * GROUP + g
                sc = jnp.where(valid, _dot(_head(q_ref, h), kcat, NT), NEG_INF)
                sink = sink_ref[h]
                m = jnp.maximum(jnp.max(sc, axis=1, keepdims=True), sink)
                p = jnp.exp(sc - m)
                den = jnp.sum(p, axis=1, keepdims=True) + jnp.exp(sink - m)
                o_ref[:, h * HEAD_DIM:(h + 1) * HEAD_DIM] = _dot((p / den).astype(BF16), vcat).astype(BF16)
                l_ref[:, h:h + 1] = m + jnp.log(den)

    return _call(
        body, [sinks, q, k, k, v, v], name=name, grid=(nb,),
        in_specs=[pl.BlockSpec(memory_space=pltpu.SMEM), cur(ATTN_WIDTH), cur(KV_WIDTH), prev(KV_WIDTH),
                  cur(KV_WIDTH), prev(KV_WIDTH)],
        out_specs=[cur(ATTN_WIDTH), cur(N_HEADS)],
        out_shape=[jax.ShapeDtypeStruct((s, ATTN_WIDTH), BF16), jax.ShapeDtypeStruct((s, N_HEADS), F32)],
        exchanges=exchanges)


def _band_mask_t(first_block):
    key = lax.broadcasted_iota(jnp.int32, (2 * BLOCK, 2 * BLOCK), 0)
    qry = lax.broadcasted_iota(jnp.int32, (2 * BLOCK, 2 * BLOCK), 1) & (BLOCK - 1)
    d = key - qry
    return (d >= 1) & (d <= BLOCK) & ((key >= BLOCK) | jnp.logical_not(first_block))


def _placed(both, kh, b):
    lane = lax.broadcasted_iota(jnp.int32, (1, LANES), 1)
    src = both if kh == b else pltpu.roll(both, HEAD_DIM, 1)
    return jnp.where((lane >= b * HEAD_DIM) & (lane < (b + 1) * HEAD_DIM), src, jnp.zeros_like(src))


def _pair(ref, kh, r):
    j = 2 * kh + r
    return ref[:, j * LANES:(j + 1) * LANES]


def _sink_row(sink_ref, kh, b):
    lane = lax.broadcasted_iota(jnp.int32, (1, 2 * BLOCK), 1)
    return jnp.where(lane < BLOCK, sink_ref[GROUP * kh + b], sink_ref[GROUP * kh + 2 + b])


def _attn_fwd_t(sinks, q, k, v, name, exchanges=()):
    s = q.shape[0]
    nb = s // BLOCK
    cur = lambda w: pl.BlockSpec((BLOCK, w), lambda i: (i, 0))
    prev = lambda w: pl.BlockSpec((BLOCK, w), lambda i: (jnp.maximum(i - 1, 0), 0))

    def body(sink_ref, q_ref, kc_ref, kp_ref, vc_ref, vp_ref, o_ref, l_ref):
        valid = _band_mask_t(pl.program_id(0) == 0)
        k_both = jnp.concatenate([kp_ref[...], kc_ref[...]], axis=0)
        v_both = jnp.concatenate([vp_ref[...], vc_ref[...]], axis=0)
        for kh in range(N_KV_HEADS):
            qg = jnp.concatenate([_pair(q_ref, kh, 0), _pair(q_ref, kh, 1)], axis=0)
            keys = jnp.concatenate([_placed(k_both, kh, 0), _placed(k_both, kh, 1)], axis=0)
            st = _dot(keys, qg, NT)
            out = jnp.zeros((2 * BLOCK, LANES), F32)
            for b in range(2):
                sc = jnp.where(valid, st[b * 2 * BLOCK:(b + 1) * 2 * BLOCK, :], NEG_INF)
                sink = _sink_row(sink_ref, kh, b)
                m = jnp.maximum(jnp.max(sc, axis=0, keepdims=True), sink)
                p = jnp.exp(sc - m)
                den = jnp.sum(p, axis=0, keepdims=True) + jnp.exp(sink - m)
                out = out + _dot((p * (1.0 / den)).astype(BF16), _placed(v_both, kh, b), TN)
                lse = m + jnp.log(den)
                for r in range(2):
                    h = GROUP * kh + 2 * r + b
                    l_ref[h:h + 1, :] = lse[:, r * BLOCK:(r + 1) * BLOCK]
            for r in range(2):
                j = 2 * kh + r
                o_ref[:, j * LANES:(j + 1) * LANES] = out[r * BLOCK:(r + 1) * BLOCK, :].astype(BF16)

    return _call(
        body, [sinks, q, k, k, v, v], name=name, grid=(nb,),
        in_specs=[pl.BlockSpec(memory_space=pltpu.SMEM), cur(ATTN_WIDTH), cur(KV_WIDTH), prev(KV_WIDTH),
                  cur(KV_WIDTH), prev(KV_WIDTH)],
        out_specs=[cur(ATTN_WIDTH), pl.BlockSpec((N_HEADS, BLOCK), lambda i: (0, i))],
        out_shape=[jax.ShapeDtypeStruct((s, ATTN_WIDTH), BF16), jax.ShapeDtypeStruct((N_HEADS, s), F32)],
        exchanges=exchanges)


def _attn_bwd_t(sinks, q, do, lse, k, v, name, exchanges=()):
    s = q.shape[0]
    nb = s // BLOCK
    cur = lambda w: pl.BlockSpec((BLOCK, w), lambda i: (i, 0))
    prev = lambda w: pl.BlockSpec((BLOCK, w), lambda i: (jnp.maximum(i - 1, 0), 0))

    def body(sink_ref, q_ref, do_ref, l_ref, kc_ref, kp_ref, vc_ref, vp_ref,
             dq_ref, dkc_ref, dkp_ref, dvc_ref, dvp_ref, dsink_ref):
        @pl.when(pl.program_id(0) == 0)
        def _():
            dsink_ref[...] = jnp.zeros_like(dsink_ref)

        valid = _band_mask_t(pl.program_id(0) == 0)
        k_both = jnp.concatenate([kp_ref[...], kc_ref[...]], axis=0)
        v_both = jnp.concatenate([vp_ref[...], vc_ref[...]], axis=0)
        lane = lax.broadcasted_iota(jnp.int32, (1, LANES), 1)
        col = lax.broadcasted_iota(jnp.int32, (1, 2 * BLOCK), 1)
        dk_heads, dv_heads = [], []
        for kh in range(N_KV_HEADS):
            qg = jnp.concatenate([_pair(q_ref, kh, 0), _pair(q_ref, kh, 1)], axis=0)
            dog = jnp.concatenate([_pair(do_ref, kh, 0), _pair(do_ref, kh, 1)], axis=0)
            k_placed = [_placed(k_both, kh, b) for b in range(2)]
            v_placed = [_placed(v_both, kh, b) for b in range(2)]
            st = _dot(jnp.concatenate(k_placed, axis=0), qg, NT)
            dpt = _dot(jnp.concatenate(v_placed, axis=0), dog, NT)
            dqg = jnp.zeros((2 * BLOCK, LANES), F32)
            dk_b, dv_b = [], []
            for b in range(2):
                rows = slice(b * 2 * BLOCK, (b + 1) * 2 * BLOCK)
                heads = [GROUP * kh + 2 * r + b for r in range(2)]
                lse_row = jnp.concatenate([l_ref[h:h + 1, :] for h in heads], axis=1)
                p = jnp.where(valid, jnp.exp(st[rows, :] - lse_row), 0.0)
                dp = dpt[rows, :]
                delta = jnp.sum(p * dp, axis=0, keepdims=True)
                sink_pull = jnp.exp(_sink_row(sink_ref, kh, b) - lse_row) * delta
                dsink_ref[heads[0]:heads[0] + 1, :] += -jnp.sum(jnp.where(col < BLOCK, sink_pull, 0.0))
                dsink_ref[heads[1]:heads[1] + 1, :] += -jnp.sum(jnp.where(col < BLOCK, 0.0, sink_pull))
                ds = (p * (dp - delta)).astype(BF16)
                dqg = dqg + _dot(ds, k_placed[b], TN)
                dk_b.append(_dot(ds, qg))
                dv_b.append(_dot(p.astype(BF16), dog))
            for parts, total in ((dk_b, dk_heads), (dv_b, dv_heads)):
                kept = jnp.where(lane < HEAD_DIM, parts[0], parts[1])
                total.append(kept + pltpu.roll(kept, HEAD_DIM, 1))
            for r in range(2):
                j = 2 * kh + r
                dq_ref[:, j * LANES:(j + 1) * LANES] = dqg[r * BLOCK:(r + 1) * BLOCK, :] * Q_SCALE
        dk = jnp.where(lane < HEAD_DIM, dk_heads[0], dk_heads[1])
        dv = jnp.where(lane < HEAD_DIM, dv_heads[0], dv_heads[1])
        dkp_ref[...] = dk[0:BLOCK, :]
        dkc_ref[...] = dk[BLOCK:, :]
        dvp_ref[...] = dv[0:BLOCK, :]
        dvc_ref[...] = dv[BLOCK:, :]

    kv = jax.ShapeDtypeStruct((s, KV_WIDTH), F32)
    return _call(
        body, [sinks, q, do, lse, k, k, v, v], name=name, grid=(nb,),
        in_specs=[pl.BlockSpec(memory_space=pltpu.SMEM), cur(ATTN_WIDTH), cur(ATTN_WIDTH),
                  pl.BlockSpec((N_HEADS, BLOCK), lambda i: (0, i)),
                  cur(KV_WIDTH), prev(KV_WIDTH), cur(KV_WIDTH), prev(KV_WIDTH)],
        out_specs=[cur(ATTN_WIDTH), cur(KV_WIDTH), cur(KV_WIDTH), cur(KV_WIDTH), cur(KV_WIDTH),
                   pl.BlockSpec((8, LANES), lambda i: (0, 0))],
        out_shape=[jax.ShapeDtypeStruct((s, ATTN_WIDTH), F32), kv, kv, kv, kv, jax.ShapeDtypeStruct((8, LANES), F32)],
        exchanges=exchanges)


def _pool_counts(tile, ts):
    t = (tile * ts + lax.broadcasted_iota(jnp.int32, (ts, 1), 0) + 1).astype(F32)
    return [jnp.minimum(t, float(w)) for w in POOL_WINDOWS]


def _pooled(u, halo, counts, gi):
    cols = slice(gi * POOL_GROUP_WIDTH, (gi + 1) * POOL_GROUP_WIDTH)
    acc = jnp.concatenate([halo[:, cols], u[:, cols]], axis=0)
    shift = 1
    while shift < POOL_WINDOWS[gi]:
        acc = acc + pltpu.roll(acc, shift, 0)
        shift *= 2
    return acc[POOL_HALO:, :] / counts[gi] - u[:, cols]


def _fwd_out(ao, u, pool_w, psc, w_out, x, vec, name):
    s = x.shape[0]
    ts = _seq_tile(s)
    hb = ts // POOL_HALO

    def body(ao_ref, u_ref, uh_ref, pw_ref, psc_ref, w_ref, x_ref, vec_ref, x1_ref, mix_ref):
        i = pl.program_id(0)
        u_t = u_ref[...]
        halo = jnp.where(i > 0, uh_ref[...], 0.0)
        counts = _pool_counts(i, ts)
        mix = _dot(ao_ref[...], w_ref[0:ATTN_WIDTH, :])
        for gi in range(len(POOL_WINDOWS)):
            cols = slice(gi * POOL_GROUP_WIDTH, (gi + 1) * POOL_GROUP_WIDTH)
            og = _dot(_pooled(u_t, halo, counts, gi).astype(BF16), pw_ref[gi]) * psc_ref[0:1, cols]
            mix = mix + _dot(og.astype(BF16), w_ref[ATTN_WIDTH + gi * POOL_GROUP_WIDTH:
                                                    ATTN_WIDTH + (gi + 1) * POOL_GROUP_WIDTH, :])
        mix_ref[...] = mix
        x1_ref[...] = x_ref[...] + vec_ref[0:1, :] * (mix * _rsqrt_ms(mix) * vec_ref[1:2, :])

    row = lambda w: pl.BlockSpec((ts, w), lambda i: (i, 0))
    return pl.pallas_call(
        body, name=name, grid=(s // ts,),
        in_specs=[row(ATTN_WIDTH), row(POOL_WIDTH),
                  pl.BlockSpec((POOL_HALO, POOL_WIDTH), lambda i: (jnp.maximum(i * hb - 1, 0), 0)),
                  _resident(pool_w.shape), _resident((8, POOL_WIDTH)), _resident((D_MODEL, D_MODEL)),
                  row(D_MODEL), _resident((8, D_MODEL))],
        out_specs=[row(D_MODEL), row(D_MODEL)],
        out_shape=[jax.ShapeDtypeStruct((s, D_MODEL), F32), jax.ShapeDtypeStruct((s, D_MODEL), F32)],
        compiler_params=_cparams(1),
    )(ao, u, u, pool_w, psc, w_out, x, vec)


def _ffn_fwd(x1, vec, w_gate_t, w_up_t, w_down, name, exchanges=(), target=None):
    s = x1.shape[0]
    ts = _seq_tile(s)

    def body(x_ref, *refs):
        if target is None:
            vec_ref, wg_ref, wu_ref, wd_ref, x2_ref, f_ref, gt_ref, up_ref, act_ref, h2_ref = refs
        else:
            t_ref, vec_ref, wg_ref, wu_ref, wd_ref, x2_ref, f_ref, gt_ref, up_ref, act_ref, h2_ref, loss_ref = refs
        xt = x_ref[...]
        h2 = (xt * _rsqrt_ms(xt) * vec_ref[0:1, :] * vec_ref[1:2, :] + vec_ref[2:3, :]).astype(BF16)
        h2_ref[...] = h2
        f = jnp.zeros((ts, D_MODEL), F32)
        for c in range(D_FF // FF_CHUNK):
            cols = slice(c * FF_CHUNK, (c + 1) * FF_CHUNK)
            g = _dot(h2, wg_ref[cols, :], NT)
            up = _dot(h2, wu_ref[cols, :], NT)
            act = ((g * jax.nn.sigmoid(g)) * up).astype(BF16)
            gt_ref[:, cols] = g.astype(BF16)
            up_ref[:, cols] = up.astype(BF16)
            act_ref[:, cols] = act
            f = f + _dot(act, wd_ref[cols, :])
        f_ref[...] = f
        x2 = xt + vec_ref[3:4, :] * (f * _rsqrt_ms(f) * vec_ref[4:5, :])
        if target is None:
            x2_ref[...] = x2
        else:
            @pl.when(pl.program_id(0) == 0)
            def _():
                loss_ref[...] = jnp.zeros_like(loss_ref)

            diff = x2 - t_ref[...]
            x2_ref[...] = diff / D_MODEL
            loss_ref[...] += 0.5 * jnp.sum(jnp.mean(diff * diff, axis=-1, keepdims=True))

    row = lambda w: pl.BlockSpec((ts, w), lambda i: (i, 0))
    wide = jax.ShapeDtypeStruct((s, D_FF), BF16)
    with_loss = target is not None
    return _call(
        body, [x1] + ([target] if with_loss else []) + [vec, w_gate_t, w_up_t, w_down], name=name, grid=(s // ts,),
        in_specs=[row(D_MODEL)] * (2 if with_loss else 1) + [
            _resident((8, D_MODEL)), _resident((D_FF, D_MODEL)), _resident((D_FF, D_MODEL)), _resident((D_FF, D_MODEL))],
        out_specs=[row(D_MODEL), row(D_MODEL), row(D_FF), row(D_FF), row(D_FF), row(D_MODEL)] + (
            [pl.BlockSpec((8, LANES), lambda i: (0, 0))] if with_loss else []),
        out_shape=[jax.ShapeDtypeStruct((s, D_MODEL), F32), jax.ShapeDtypeStruct((s, D_MODEL), F32), wide, wide, wide,
                   jax.ShapeDtypeStruct((s, D_MODEL), BF16)] + (
            [jax.ShapeDtypeStruct((8, LANES), F32)] if with_loss else []),
        exchanges=exchanges)


def _norm_bwd(dy_hat, x_hat, r):
    return r * (dy_hat - x_hat * jnp.mean(dy_hat * x_hat, axis=-1, keepdims=True))


def _accumulate_rows(ref, rows):
    for j, val in enumerate(rows):
        ref[j:j + 1, :] += val


def _ffn_bwd_act(dx2, f, gt, up, vec, w_down, name, exchanges=()):
    s = dx2.shape[0]
    ts = _seq_tile(s)

    def body(dx2_ref, f_ref, gt_ref, up_ref, vec_ref, wd_ref, dgt_ref, dup_ref, df_ref, cs_ref):
        @pl.when(pl.program_id(0) == 0)
        def _():
            cs_ref[...] = jnp.zeros_like(cs_ref)

        dx2_t, f_t = dx2_ref[...], f_ref[...]
        gate, g_post = vec_ref[3:4, :], vec_ref[4:5, :]
        rf = _rsqrt_ms(f_t)
        f_hat = f_t * rf
        df = _norm_bwd(dx2_t * gate * g_post, f_hat, rf).astype(BF16)
        df_ref[...] = df
        _accumulate_rows(cs_ref, [_colsum(dx2_t * (f_hat * g_post)), _colsum(dx2_t * gate * f_hat)])
        for c in range(D_FF // FF_CHUNK):
            cols = slice(c * FF_CHUNK, (c + 1) * FF_CHUNK)
            dact = _dot(df, wd_ref[cols, :], NT)
            g, u_t = gt_ref[:, cols].astype(F32), up_ref[:, cols].astype(F32)
            sg = jax.nn.sigmoid(g)
            dgt_ref[:, cols] = (dact * u_t * (sg * (1.0 + g * (1.0 - sg)))).astype(BF16)
            dup_ref[:, cols] = (dact * (g * sg)).astype(BF16)

    row = lambda w: pl.BlockSpec((ts, w), lambda i: (i, 0))
    wide = jax.ShapeDtypeStruct((s, D_FF), BF16)
    return _call(
        body, [dx2, f, gt, up, vec, w_down], name=name, grid=(s // ts,),
        in_specs=[row(D_MODEL), row(D_MODEL), row(D_FF), row(D_FF), _resident((8, D_MODEL)), _resident((D_FF, D_MODEL))],
        out_specs=[row(D_FF), row(D_FF), row(D_MODEL), pl.BlockSpec((8, D_MODEL), lambda i: (0, 0))],
        out_shape=[wide, wide, jax.ShapeDtypeStruct((s, D_MODEL), BF16), jax.ShapeDtypeStruct((8, D_MODEL), F32)],
        exchanges=exchanges)


def _ffn_bwd_in(dx2, x1, dgt, dup, vec, w_gate_t, w_up_t, name, exchanges=()):
    s = x1.shape[0]
    ts = _seq_tile(s)

    def body(dx2_ref, x1_ref, dgt_ref, dup_ref, vec_ref, wg_ref, wu_ref, dx1_ref, cs_ref):
        @pl.when(pl.program_id(0) == 0)
        def _():
            cs_ref[...] = jnp.zeros_like(cs_ref)

        g_pre, one_scale = vec_ref[0:1, :], vec_ref[1:2, :]
        dh2 = _dot(dgt_ref[...], wg_ref[...]) + _dot(dup_ref[...], wu_ref[...])
        x1_t = x1_ref[...]
        r1 = _rsqrt_ms(x1_t)
        x_hat = x1_t * r1
        dx1_ref[...] = dx2_ref[...] + _norm_bwd(dh2 * g_pre * one_scale, x_hat, r1)
        _accumulate_rows(cs_ref, [_colsum(dh2), _colsum(dh2 * (x_hat * g_pre)), _colsum(dh2 * one_scale * x_hat)])

    row = lambda w: pl.BlockSpec((ts, w), lambda i: (i, 0))
    return _call(
        body, [dx2, x1, dgt, dup, vec, w_gate_t, w_up_t], name=name, grid=(s // ts,),
        in_specs=[row(D_MODEL), row(D_MODEL), row(D_FF), row(D_FF), _resident((8, D_MODEL)),
                  _resident((D_FF, D_MODEL)), _resident((D_FF, D_MODEL))],
        out_specs=[row(D_MODEL), pl.BlockSpec((8, D_MODEL), lambda i: (0, 0))],
        out_shape=[jax.ShapeDtypeStruct((s, D_MODEL), F32), jax.ShapeDtypeStruct((8, D_MODEL), F32)],
        exchanges=exchanges)


def _weight_grad(a, b, name, exchanges=()):
    s, m = a.shape
    n = b.shape[1]
    tk = min(s, 2 * SEQ_TILE)
    steps = s // tk

    def body(a_ref, b_ref, o_ref, acc_ref):
        i = pl.program_id(0)

        @pl.when(i == 0)
        def _():
            acc_ref[...] = jnp.zeros_like(acc_ref)

        b_t = b_ref[...]
        for c in range(m // FF_CHUNK):
            rows = slice(c * FF_CHUNK, (c + 1) * FF_CHUNK)
            acc_ref[rows, :] += _dot(a_ref[:, rows], b_t, TN)

        @pl.when(i == steps - 1)
        def _():
            o_ref[...] = acc_ref[...].astype(BF16)

    (grad,), ex_outs = _call(
        body, [a, b], name=name, grid=(steps,),
        in_specs=[pl.BlockSpec((tk, m), lambda i: (i, 0)), pl.BlockSpec((tk, n), lambda i: (i, 0))],
        out_specs=[_resident((m, n))],
        out_shape=[jax.ShapeDtypeStruct((m, n), BF16)],
        scratch_shapes=[pltpu.VMEM((m, n), F32)],
        exchanges=exchanges)
    return grad, ex_outs


def _out_bwd(dx1, mix, ao, u, pool_w, psc, w_out, vec, name):
    s = dx1.shape[0]
    ts = _seq_tile(s)
    nt = s // ts
    hb = ts // POOL_HALO
    ng = len(POOL_WINDOWS)

    def body(dx1_ref, mix_ref, ao_ref, u_ref, uh_ref, pw_ref, psc_ref, w_ref, vec_ref,
             do_ref, du_ref, cat_ref, dmix_ref, cs_ref, dpw_ref, dpsc_ref, carry_ref):
        i = pl.program_id(0)
        tile = nt - 1 - i

        @pl.when(i == 0)
        def _():
            cs_ref[...] = jnp.zeros_like(cs_ref)
            dpw_ref[...] = jnp.zeros_like(dpw_ref)
            dpsc_ref[...] = jnp.zeros_like(dpsc_ref)
            carry_ref[...] = jnp.zeros_like(carry_ref)

        dx1_t, mix_t = dx1_ref[...], mix_ref[...]
        gate, g_post = vec_ref[0:1, :], vec_ref[1:2, :]
        rm = _rsqrt_ms(mix_t)
        m_hat = mix_t * rm
        dmix = _norm_bwd(dx1_t * gate * g_post, m_hat, rm).astype(BF16)
        dmix_ref[...] = dmix
        _accumulate_rows(cs_ref, [_colsum(dx1_t * (m_hat * g_post)), _colsum(dx1_t * gate * m_hat)])
        do_ref[...] = _dot(dmix, w_ref[0:ATTN_WIDTH, :], NT).astype(BF16)
        cat_ref[:, 0:ATTN_WIDTH] = ao_ref[...]

        u_t = u_ref[...]
        halo = jnp.where(tile > 0, uh_ref[...], 0.0)
        counts = _pool_counts(tile, ts)
        for gi in range(ng):
            cols = slice(gi * POOL_GROUP_WIDTH, (gi + 1) * POOL_GROUP_WIDTH)
            wide_cols = slice(ATTN_WIDTH + gi * POOL_GROUP_WIDTH, ATTN_WIDTH + (gi + 1) * POOL_GROUP_WIDTH)
            scale = psc_ref[0:1, cols]
            pooled = _pooled(u_t, halo, counts, gi).astype(BF16)
            og = _dot(pooled, pw_ref[gi])
            cat_ref[:, wide_cols] = (og * scale).astype(BF16)
            d_out = _dot(dmix, w_ref[wide_cols, :], NT)
            dpsc_ref[0:1, cols] += _colsum(d_out * og)
            d_og = (d_out * scale).astype(BF16)
            dpw_ref[gi] += _dot(pooled, d_og, TN)
            d_pooled = _dot(d_og, pw_ref[gi], NT)
            spread = d_pooled / counts[gi]
            acc = jnp.concatenate([spread, carry_ref[:, cols]], axis=0)
            shift = 1
            while shift < POOL_WINDOWS[gi]:
                acc = acc + pltpu.roll(acc, ts + POOL_HALO - shift, 0)
                shift *= 2
            du_ref[:, cols] = acc[0:ts, :] - d_pooled
            carry_ref[:, cols] = spread[0:POOL_HALO, :]

    row = lambda w: pl.BlockSpec((ts, w), lambda i: (nt - 1 - i, 0))
    fixed = lambda shape: pl.BlockSpec(shape, lambda i: (0,) * len(shape))
    return pl.pallas_call(
        body, name=name, grid=(nt,),
        in_specs=[row(D_MODEL), row(D_MODEL), row(ATTN_WIDTH), row(POOL_WIDTH),
                  pl.BlockSpec((POOL_HALO, POOL_WIDTH), lambda i: (jnp.maximum((nt - 1 - i) * hb - 1, 0), 0)),
                  _resident(pool_w.shape), _resident((8, POOL_WIDTH)), _resident((D_MODEL, D_MODEL)),
                  _resident((8, D_MODEL))],
        out_specs=[row(ATTN_WIDTH), row(POOL_WIDTH), row(D_MODEL), row(D_MODEL), fixed((8, D_MODEL)),
                   fixed(pool_w.shape), fixed((8, POOL_WIDTH))],
        out_shape=[jax.ShapeDtypeStruct((s, ATTN_WIDTH), BF16), jax.ShapeDtypeStruct((s, POOL_WIDTH), F32),
                   jax.ShapeDtypeStruct((s, D_MODEL), BF16), jax.ShapeDtypeStruct((s, D_MODEL), BF16),
                   jax.ShapeDtypeStruct((8, D_MODEL), F32), jax.ShapeDtypeStruct(pool_w.shape, F32),
                   jax.ShapeDtypeStruct((8, POOL_WIDTH), F32)],
        scratch_shapes=[pltpu.VMEM((POOL_HALO, POOL_WIDTH), F32)],
        compiler_params=_cparams(1),
    )(dx1, mix, ao, u, u, pool_w, psc, w_out, vec)


def _attn_bwd(sinks, q, do, lse, k, v, name, exchanges=()):
    s = q.shape[0]
    nb = s // BLOCK
    cur = lambda w: pl.BlockSpec((BLOCK, w), lambda i: (i, 0))
    prev = lambda w: pl.BlockSpec((BLOCK, w), lambda i: (jnp.maximum(i - 1, 0), 0))

    def body(sink_ref, q_ref, do_ref, l_ref, kc_ref, kp_ref, vc_ref, vp_ref,
             dq_ref, dkc_ref, dkp_ref, dvc_ref, dvp_ref, dsink_ref):
        @pl.when(pl.program_id(0) == 0)
        def _():
            dsink_ref[...] = jnp.zeros_like(dsink_ref)

        valid = _band_mask(pl.program_id(0) == 0)
        for kh in range(N_KV_HEADS):
            kcat, vcat = _kv_cat(kp_ref, kc_ref, kh), _kv_cat(vp_ref, vc_ref, kh)
            dk = jnp.zeros((2 * BLOCK, HEAD_DIM), F32)
            dv = jnp.zeros((2 * BLOCK, HEAD_DIM), F32)
            for g in range(GROUP):
                h = kh * GROUP + g
                qh, doh, lse_h = _head(q_ref, h), _head(do_ref, h), l_ref[:, h:h + 1]
                p = jnp.where(valid, jnp.exp(_dot(qh, kcat, NT) - lse_h), 0.0)
                dp = _dot(doh, vcat, NT)
                delta = jnp.sum(p * dp, axis=1, keepdims=True)
                dsink_ref[h:h + 1, :] += -jnp.sum(jnp.exp(sink_ref[h] - lse_h) * delta)
                ds = (p * (dp - delta)).astype(BF16)
                dq_ref[:, h * HEAD_DIM:(h + 1) * HEAD_DIM] = _dot(ds, kcat) * Q_SCALE
                dk = dk + _dot(ds, qh, TN)
                dv = dv + _dot(p.astype(BF16), doh, TN)
            hcols = slice(kh * HEAD_DIM, (kh + 1) * HEAD_DIM)
            dkp_ref[:, hcols] = dk[0:BLOCK, :]
            dkc_ref[:, hcols] = dk[BLOCK:, :]
            dvp_ref[:, hcols] = dv[0:BLOCK, :]
            dvc_ref[:, hcols] = dv[BLOCK:, :]

    kv = jax.ShapeDtypeStruct((s, KV_WIDTH), F32)
    return _call(
        body, [sinks, q, do, lse, k, k, v, v], name=name, grid=(nb,),
        in_specs=[pl.BlockSpec(memory_space=pltpu.SMEM), cur(ATTN_WIDTH), cur(ATTN_WIDTH), cur(N_HEADS),
                  cur(KV_WIDTH), prev(KV_WIDTH), cur(KV_WIDTH), prev(KV_WIDTH)],
        out_specs=[cur(ATTN_WIDTH), cur(KV_WIDTH), cur(KV_WIDTH), cur(KV_WIDTH), cur(KV_WIDTH),
                   pl.BlockSpec((8, LANES), lambda i: (0, 0))],
        out_shape=[jax.ShapeDtypeStruct((s, ATTN_WIDTH), F32), kv, kv, kv, kv, jax.ShapeDtypeStruct((8, LANES), F32)],
        exchanges=exchanges)


def _in_bwd(dq, dkc, dkp, dvc, dvp, du, x, dx1, vec, w_in_t, rot, name):
    s = x.shape[0]
    ts = _seq_tile(s)
    nt = s // ts
    bpt = ts // BLOCK

    def body(dq_ref, dkc_ref, dkp_ref, dkn_ref, dvc_ref, dvp_ref, dvn_ref, du_ref, x_ref, dx1_ref, vec_ref, w_ref,
             cos_ref, lo_ref, hi_ref, dx_ref, dproj_ref, h_ref, cs_ref, db_ref):
        i = pl.program_id(0)

        @pl.when(i == 0)
        def _():
            cs_ref[...] = jnp.zeros_like(cs_ref)
            db_ref[...] = jnp.zeros_like(db_ref)

        cos, lo, hi = cos_ref[...], lo_ref[...], hi_ref[...]

        def with_next_block(cur_ref, prev_ref, next_ref):
            nxt = jnp.where(i < nt - 1, next_ref[...], 0.0)
            later = nxt if bpt == 1 else jnp.concatenate([prev_ref[BLOCK:, :], nxt], axis=0)
            return cur_ref[...] + later

        pieces = [_rotate_bwd(dq_ref[:, j * LANES:(j + 1) * LANES], cos, lo, hi) for j in range(ATTN_WIDTH // LANES)]
        pieces.append(_rotate_bwd(with_next_block(dkc_ref, dkp_ref, dkn_ref), cos, lo, hi))
        pieces.append(with_next_block(dvc_ref, dvp_ref, dvn_ref))
        pieces.append(du_ref[...])
        dproj = jnp.concatenate(pieces, axis=1)
        db_ref[0:1, :] += _colsum(dproj)
        dproj_b = dproj.astype(BF16)
        dproj_ref[...] = dproj_b
        dh = _dot(dproj_b, w_ref[...])

        xt = x_ref[...]
        g_pre, one_scale = vec_ref[0:1, :], vec_ref[1:2, :]
        r = _rsqrt_ms(xt)
        x_hat = xt * r
        h_ref[...] = (x_hat * g_pre * one_scale + vec_ref[2:3, :]).astype(BF16)
        dx_ref[...] = dx1_ref[...] + _norm_bwd(dh * g_pre * one_scale, x_hat, r)
        _accumulate_rows(cs_ref, [_colsum(dh), _colsum(dh * (x_hat * g_pre)), _colsum(dh * one_scale * x_hat)])

    row = lambda w: pl.BlockSpec((ts, w), lambda i: (i, 0))
    nxt = pl.BlockSpec((BLOCK, KV_WIDTH), lambda i: (jnp.minimum((i + 1) * bpt, s // BLOCK - 1), 0))
    fixed = lambda shape: pl.BlockSpec(shape, lambda i: (0,) * len(shape))
    return pl.pallas_call(
        body, name=name, grid=(nt,),
        in_specs=[row(ATTN_WIDTH), row(KV_WIDTH), row(KV_WIDTH), nxt, row(KV_WIDTH), row(KV_WIDTH), nxt,
                  row(POOL_WIDTH), row(D_MODEL), row(D_MODEL), _resident((8, D_MODEL)), _resident((IN_WIDTH, D_MODEL)),
                  row(LANES), row(LANES), row(LANES)],
        out_specs=[row(D_MODEL), row(IN_WIDTH), row(D_MODEL), fixed((8, D_MODEL)), fixed((8, IN_WIDTH))],
        out_shape=[jax.ShapeDtypeStruct((s, D_MODEL), F32), jax.ShapeDtypeStruct((s, IN_WIDTH), BF16),
                   jax.ShapeDtypeStruct((s, D_MODEL), BF16), jax.ShapeDtypeStruct((8, D_MODEL), F32),
                   jax.ShapeDtypeStruct((8, IN_WIDTH), F32)],
        compiler_params=_cparams(1),
    )(dq, dkc, dkp, dkp, dvc, dvp, dvp, du, x, dx1, vec, w_in_t, *rot)


def _row_tile(rows):
    for t in (512, 352, 256, 176, 160, 128, 64, 32, 16, 8):
        if rows % t == 0:
            return t
    raise ValueError(f"no row tile for {rows} rows")


def _sum_parts(own, others, name):
    r, c = own.shape
    n_other = others.shape[0]
    tr = _row_tile(r)

    def body(own_ref, oth_ref, o_ref):
        acc = own_ref[...].astype(F32)
        for k in range(n_other):
            acc = acc + oth_ref[k].astype(F32)
        o_ref[...] = acc

    return pl.pallas_call(
        body, name=name, grid=(r // tr,),
        in_specs=[pl.BlockSpec((tr, c), lambda i: (i, 0)), pl.BlockSpec((n_other, tr, c), lambda i: (0, i, 0))],
        out_specs=pl.BlockSpec((tr, c), lambda i: (i, 0)),
        out_shape=jax.ShapeDtypeStruct((r, c), F32),
        compiler_params=_cparams(1),
    )(own, others)


def _adamw(w, g, m, v, name):
    r, c = w.shape
    tr = _row_tile(r)

    def body(w_ref, g_ref, m_ref, v_ref, d_ref, nm_ref, nv_ref):
        g_t = g_ref[...]
        m_new = ADAM_B1 * m_ref[...] + (1.0 - ADAM_B1) * g_t
        v_new = ADAM_B2 * v_ref[...] + (1.0 - ADAM_B2) * (g_t * g_t)
        m_hat = m_new / (1.0 - ADAM_B1 ** ADAM_STEP)
        v_hat = v_new / (1.0 - ADAM_B2 ** ADAM_STEP)
        d_ref[...] = -ADAM_LR * (m_hat / (jnp.sqrt(v_hat) + ADAM_EPS) + ADAM_WD * w_ref[...])
        nm_ref[...] = m_new
        nv_ref[...] = v_new

    spec = pl.BlockSpec((tr, c), lambda i: (i, 0))
    shape = jax.ShapeDtypeStruct((r, c), F32)
    return pl.pallas_call(
        body, name=name, grid=(r // tr,), in_specs=[spec] * 4, out_specs=[spec] * 3, out_shape=[shape] * 3,
        compiler_params=_cparams(1),
    )(w, g, m, v)


SMALL_ROWS = 160


def _pack_small(dmod, gains, b_in, pool_scale, sinks, pool_w):
    pad = lambda a, width: jnp.pad(a, ((0, 0), (0, width - a.shape[1])))
    parts = [dmod.reshape(DEPTH * N_MOD, D_MODEL), gains.reshape(4 * DEPTH, D_MODEL),
             pad(b_in, 2 * D_MODEL).reshape(2 * DEPTH, D_MODEL), pool_scale.reshape(1, D_MODEL),
             pad(sinks.reshape(1, DEPTH * N_HEADS), D_MODEL), pool_w.reshape(-1, D_MODEL)]
    packed = jnp.concatenate(parts, axis=0)
    return jnp.pad(packed, ((0, SMALL_ROWS - packed.shape[0]), (0, 0)))


def _unpack_small(p):
    r = DEPTH * N_MOD
    dmod = p[0:r].reshape(DEPTH, N_MOD * D_MODEL)
    gains = p[r:r + 4 * DEPTH].reshape(4, DEPTH, D_MODEL)
    r += 4 * DEPTH
    b_in = p[r:r + 2 * DEPTH].reshape(DEPTH, 2 * D_MODEL)[:, :IN_WIDTH]
    r += 2 * DEPTH
    pool_scale = p[r].reshape(DEPTH, POOL_WIDTH)
    sinks = p[r + 1, :DEPTH * N_HEADS].reshape(DEPTH, N_HEADS)
    pool_w = p[r + 2:r + 2 + 128].reshape(DEPTH, len(POOL_WINDOWS), POOL_GROUP_WIDTH, POOL_GROUP_WIDTH)
    return dmod, gains, b_in, pool_scale, sinks, pool_w


LAYER_ROWS = 80


def _pack_layer(dmod, gains, b_in, pool_scale, sinks, pool_w):
    misc = jnp.concatenate([pool_scale, sinks, jnp.zeros((D_MODEL - POOL_WIDTH - N_HEADS,), F32)])
    parts = [dmod.reshape(N_MOD, D_MODEL), gains, jnp.pad(b_in, (0, 2 * D_MODEL - IN_WIDTH)).reshape(2, D_MODEL),
             misc.reshape(1, D_MODEL), pool_w.reshape(-1, D_MODEL)]
    packed = jnp.concatenate(parts, axis=0)
    return jnp.pad(packed, ((0, LAYER_ROWS - packed.shape[0]), (0, 0)))


def _unpack_layer(p):
    r = N_MOD + 4
    b_in = p[r:r + 2].reshape(2 * D_MODEL)[:IN_WIDTH]
    pool_w = p[r + 3:r + 3 + 64].reshape(len(POOL_WINDOWS), POOL_GROUP_WIDTH, POOL_GROUP_WIDTH)
    return (p[0:N_MOD].reshape(N_MOD * D_MODEL), p[N_MOD:r], b_in, p[r + 2, :POOL_WIDTH],
            p[r + 2, POOL_WIDTH:POOL_WIDTH + N_HEADS], pool_w)


def _rotary_tables(positions):
    inv_freq = ROPE_THETA ** (-jnp.arange(0, ROT_DIM, 2, dtype=F32) / ROT_DIM)
    half = ROT_DIM // 2
    head_freq = jnp.concatenate([inv_freq, inv_freq, jnp.zeros((HEAD_DIM - ROT_DIM,), F32)])
    lane_freq = jnp.concatenate([head_freq, head_freq])
    ang = positions.reshape(-1).astype(F32)[:, None] * lane_freq[None, :]
    cos, sin = jnp.cos(ang), jnp.sin(ang)
    in_head = lax.broadcasted_iota(jnp.int32, (1, LANES), 1) % HEAD_DIM
    return cos, jnp.where(in_head < half, -sin, 0.0), jnp.where((in_head >= half) & (in_head < ROT_DIM), sin, 0.0)


def kernel(x, c, positions, ada_w, ada_b, w_in, b_in, sinks, pool_w, pool_scale, w_out, w_gate, w_up, w_down, g_pre_mix, g_post_mix, g_pre_ffn, g_post_ffn, loss_target, m_ada_w, m_ada_b, m_w_in, m_b_in, m_sinks, m_pool_w, m_pool_scale, m_w_out, m_w_gate, m_w_up, m_w_down, m_g_pre_mix, m_g_post_mix, m_g_pre_ffn, m_g_post_ffn, v_ada_w, v_ada_b, v_w_in, v_b_in, v_sinks, v_pool_w, v_pool_scale, v_w_out, v_w_gate, v_w_up, v_w_down, v_g_pre_mix, v_g_post_mix, v_g_pre_ffn, v_g_post_ffn):
    me = _index(_mesh_place())
    x0 = x[0]
    target = loss_target[0]
    rot = _rotary_tables(positions)
    ada_cols = ada_w.shape[2]

    c_all = _allgather_vmem(jnp.broadcast_to(c, (8, D_MODEL)), "gather_c")[:, 0, :]
    mod_part, c_act = _mod_fwd(c_all, ada_w)
    mod_all = _allgather_vmem(mod_part.reshape(DEPTH * N_DEV, ada_cols), "gather_mod")
    mod_mine = lax.dynamic_slice_in_dim(mod_all.reshape(N_DEV, DEPTH, N_DEV, ada_cols), me, 1, axis=2)[:, :, 0, :]
    mod = jnp.transpose(mod_mine, (1, 0, 2)).reshape(DEPTH, N_MOD * D_MODEL) + ada_b
    mod = mod.reshape(DEPTH, N_MOD, D_MODEL)

    shards = []
    for l in range(DEPTH):
        shards += [w_in[l].T.astype(BF16), w_out[l].astype(BF16), w_gate[l].T.astype(BF16), w_up[l].T.astype(BF16),
                   w_down[l].astype(BF16)]
    whole = lambda gathered: [g.reshape(-1, D_MODEL) for g in gathered]
    full = [None] * (5 * DEPTH)
    full[0:2] = whole(_exchange_alone(_Gather(shards[0:2]), "gather_w_in_out_0"))
    hosted = {("fwd_in", 0): (_Gather(shards[4:5]), [4]), ("attn_fwd", 0): (_Gather(shards[2:4]), [2, 3]),
              ("ffn_fwd", 0): (_Gather(shards[5:10]), [5, 6, 7, 8, 9])}

    def take(kind, l, ex_outs):
        if (kind, l) in hosted:
            for slot, g in zip(hosted[kind, l][1], whole(ex_outs[0])):
                full[slot] = g

    def beside(kind, l):
        return [hosted[kind, l][0]] if (kind, l) in hosted else []

    pool_w_b = pool_w.astype(BF16)

    saved = []
    xl = x0
    for l in range(DEPTH):
        vec_in = _rows(g_pre_mix[l], 1.0 + mod[l, 1], mod[l, 0], width=D_MODEL)
        vec_out = _rows(mod[l, 2], g_post_mix[l], width=D_MODEL)
        vec_ffn = _rows(g_pre_ffn[l], 1.0 + mod[l, 4], mod[l, 3], mod[l, 5], g_post_ffn[l], width=D_MODEL)
        psc = _rows(pool_scale[l], width=POOL_WIDTH)
        (q, k, v, u), ex = _fwd_in(xl, vec_in, full[5 * l], _rows(b_in[l], width=IN_WIDTH), rot, f"fwd_in_{l}",
                                   beside("fwd_in", l))
        take("fwd_in", l, ex)
        (ao, lse), ex = _attn_fwd_t(sinks[l], q, k, v, f"attn_fwd_{l}", beside("attn_fwd", l))
        take("attn_fwd", l, ex)
        x1, mix = _fwd_out(ao, u, pool_w_b[l], psc, full[5 * l + 1], xl, vec_out, f"fwd_out_{l}")
        (x2, f, gt, up, act, h2, *loss_part), ex = _ffn_fwd(
            x1, vec_ffn, full[5 * l + 2], full[5 * l + 3], full[5 * l + 4], f"ffn_fwd_{l}", beside("ffn_fwd", l),
            target=target if l == DEPTH - 1 else None)
        take("ffn_fwd", l, ex)
        saved.append((xl, q, k, v, u, ao, lse, x1, mix, f, gt, up, act, h2, vec_in, vec_out, vec_ffn, psc))
        xl = x2

    dx = xl
    loss = lax.psum(loss_part[0][0, 0], ("x", "y", "c"))

    stacks = [None] * (5 * DEPTH)
    others = [None] * (5 * DEPTH)
    small_all = [None] * DEPTH
    ready_stacks, ready_small = [], []

    def leaving():
        exs, notes = [], []
        if ready_stacks:
            exs.append(_Scatter([stacks[a] for a in ready_stacks]))
            notes.append(("stacks", list(ready_stacks)))
            ready_stacks.clear()
        if ready_small:
            exs.append(_Gather([ready_small[0][1]]))
            notes.append(("small", ready_small[0][0]))
            ready_small.clear()
        return exs, notes

    def arrived(notes, ex_outs):
        for (kind, what), outs in zip(notes, ex_outs):
            if kind == "stacks":
                for a, o in zip(what, outs):
                    others[a] = o
            else:
                small_all[what] = outs[0]

    def grad_ready(a, grad):
        stacks[a] = grad.reshape(N_DEV, -1, D_MODEL)
        ready_stacks.append(a)

    for l in reversed(range(DEPTH)):
        w_in_t, w_out_f, w_gate_t, w_up_t, w_down_f = full[5 * l:5 * l + 5]
        xin, q, k, v, u, ao, lse, x1, mix, f, gt, up, act, h2, vec_in, vec_out, vec_ffn, psc = saved[l]
        exs, notes = leaving()
        (dgt, dup, df, cs_a), ex = _ffn_bwd_act(dx, f, gt, up, vec_ffn, w_down_f, f"ffn_bwd_act_{l}", exs)
        arrived(notes, ex)
        (dx1, cs_f), _ = _ffn_bwd_in(dx, x1, dgt, dup, vec_ffn, w_gate_t, w_up_t, f"ffn_bwd_in_{l}")
        grad_ready(5 * l + 2, _weight_grad(dgt, h2, f"grad_w_gate_{l}")[0])
        grad_ready(5 * l + 3, _weight_grad(dup, h2, f"grad_w_up_{l}")[0])
        grad_ready(5 * l + 4, _weight_grad(act, df, f"grad_w_down_{l}")[0])
        do, du, cat, dmix, cs_o, dpw, dpsc = _out_bwd(dx1, mix, ao, u, pool_w_b[l], psc, w_out_f, vec_out, f"out_bwd_{l}")
        grad_ready(5 * l + 1, _weight_grad(cat, dmix, f"grad_w_out_{l}")[0])
        exs, notes = leaving()
        (dq, dkc, dkp, dvc, dvp, dsk), ex = _attn_bwd_t(sinks[l], q, do, lse, k, v, f"attn_bwd_{l}", exs)
        arrived(notes, ex)
        dx, dproj, h, cs_i, db = _in_bwd(dq, dkc, dkp, dvc, dvp, du, xin, dx1, vec_in, w_in_t, rot, f"in_bwd_{l}")
        d_mod = jnp.concatenate([cs_i[0], cs_i[1], cs_o[0], cs_f[0], cs_f[1], cs_a[0]])
        d_gain = jnp.stack([cs_i[2], cs_o[1], cs_f[2], cs_a[1]])
        ready_small.append((l, _pack_layer(d_mod, d_gain, db[0], dpsc[0], dsk[:, 0], dpw)))
        exs, notes = leaving() if l == 0 else ([], [])
        grad, ex = _weight_grad(dproj, h, f"grad_w_in_{l}", exs)
        arrived(notes, ex)
        grad_ready(5 * l, grad)
    grad_x = dx[None]
    exs, notes = leaving()
    arrived(notes, [_exchange_alone(ex, "scatter_last") for ex in exs])

    layer_sums = [_unpack_layer(_sum_parts(small_all[l][0], small_all[l][1:], f"sum_small_{l}")) for l in range(DEPTH)]
    g_ada_b, g_gains, g_b_in, g_pool_scale, g_sinks, g_pool_w = (
        jnp.stack([layer_sums[l][j] for l in range(DEPTH)], axis=1 if j == 1 else 0) for j in range(6))
    small_sum = _pack_small(g_ada_b, g_gains, g_b_in, g_pool_scale, g_sinks, g_pool_w)
    gains = jnp.stack([g_pre_mix, g_post_mix, g_pre_ffn, g_post_ffn])
    m_gains = jnp.stack([m_g_pre_mix, m_g_post_mix, m_g_pre_ffn, m_g_post_ffn])
    v_gains = jnp.stack([v_g_pre_mix, v_g_post_mix, v_g_pre_ffn, v_g_post_ffn])
    small_step = _adamw(_pack_small(ada_b, gains, b_in, pool_scale, sinks, pool_w), small_sum,
                        _pack_small(m_ada_b, m_gains, m_b_in, m_pool_scale, m_sinks, m_pool_w),
                        _pack_small(v_ada_b, v_gains, v_b_in, v_pool_scale, v_sinks, v_pool_w), "adamw_small")
    small_out = [_unpack_small(p) for p in small_step]

    dmod_all = jnp.stack([small_all[l][:, 0:N_MOD, :].reshape(N_DEV, N_DEV, ada_cols) for l in range(DEPTH)])
    dmod_cols = lax.dynamic_slice_in_dim(dmod_all, me, 1, axis=2)[:, :, 0, :]
    g_ada_w = _ada_grad(c_act, dmod_cols)

    summed = []
    for a, (stack, oth) in enumerate(zip(stacks, others)):
        own = lax.dynamic_index_in_dim(stack, me, axis=0, keepdims=False)
        summed.append(_sum_parts(own, oth, f"sum_grads_{a}"))
    per_kind = [jnp.stack([summed[5 * l + j] for l in range(DEPTH)]) for j in range(5)]
    g_w_in, g_w_out, g_w_gate, g_w_up, g_w_down = per_kind
    g_w_in, g_w_gate, g_w_up = (jnp.transpose(g, (0, 2, 1)) for g in (g_w_in, g_w_gate, g_w_up))

    def step(w, g, m, v, name):
        flat = lambda a: a.reshape(-1, a.shape[-1])
        return [o.reshape(w.shape) for o in _adamw(flat(w), flat(g), flat(m), flat(v), name)]

    big = {
        "ada_w": (g_ada_w, step(ada_w, g_ada_w, m_ada_w, v_ada_w, "adamw_ada_w")),
        "w_in": (g_w_in, step(w_in, g_w_in, m_w_in, v_w_in, "adamw_w_in")),
        "w_out": (g_w_out, step(w_out, g_w_out, m_w_out, v_w_out, "adamw_w_out")),
        "w_gate": (g_w_gate, step(w_gate, g_w_gate, m_w_gate, v_w_gate, "adamw_w_gate")),
        "w_up": (g_w_up, step(w_up, g_w_up, m_w_up, v_w_up, "adamw_w_up")),
        "w_down": (g_w_down, step(w_down, g_w_down, m_w_down, v_w_down, "adamw_w_down")),
    }

    def ordered(pick_big, pick_small):
        ada_b_, gains_, b_in_, pool_scale_, sinks_, pool_w_ = pick_small
        return [pick_big("ada_w"), ada_b_, pick_big("w_in"), b_in_, sinks_, pool_w_, pool_scale_, pick_big("w_out"),
                pick_big("w_gate"), pick_big("w_up"), pick_big("w_down"), gains_[0], gains_[1], gains_[2], gains_[3]]

    grads = ordered(lambda n: big[n][0], (g_ada_b, g_gains, g_b_in, g_pool_scale, g_sinks, g_pool_w))
    deltas = ordered(lambda n: big[n][1][0], small_out[0])
    new_m = ordered(lambda n: big[n][1][1], small_out[1])
    new_v = ordered(lambda n: big[n][1][2], small_out[2])
    return (loss, grad_x, *grads, *deltas, *new_m, *new_v)
```

```python
import jax
import jax.numpy as jnp
from jax import lax
from jax.experimental import pallas as pl
from jax.experimental.pallas import tpu as pltpu

F32 = jnp.float32
BF16 = jnp.bfloat16

N_DEV = 8
DEPTH = 2
D_MODEL = 1024
HEAD_DIM = 64
N_HEADS = 8
N_KV_HEADS = 2
GROUP = N_HEADS // N_KV_HEADS
ATTN_WIDTH = N_HEADS * HEAD_DIM
KV_WIDTH = N_KV_HEADS * HEAD_DIM
POOL_WIDTH = 512
POOL_WINDOWS = (2, 4, 8, 16)
POOL_GROUP_WIDTH = 128
POOL_HALO = 16
IN_WIDTH = ATTN_WIDTH + 2 * KV_WIDTH + POOL_WIDTH
D_FF = 2816
N_MOD = 6
BLOCK = 128
ROT_DIM = 16
ROPE_THETA = 500000.0
EPS = 1e-6
NEG_INF = -1e30
Q_SCALE = HEAD_DIM ** -0.5

ADAM_LR = 0.001
ADAM_B1 = 0.9
ADAM_B2 = 0.999
ADAM_EPS = 1e-08
ADAM_WD = 0.01
ADAM_STEP = 10

LANES = 128
SEQ_TILE = 512
FF_CHUNK = 256
VMEM_LIMIT = 56 * 1024 * 1024
MESH = pl.DeviceIdType.MESH

NT = (((1,), (1,)), ((), ()))
TN = (((0,), (0,)), ((), ()))


def _dot(a, b, dims=None):
    if dims is None:
        return jnp.dot(a, b, preferred_element_type=F32)
    return lax.dot_general(a, b, dims, preferred_element_type=F32)


def _cparams(n_axes):
    return pltpu.CompilerParams(dimension_semantics=("arbitrary",) * n_axes, vmem_limit_bytes=VMEM_LIMIT)


def _resident(shape):
    zeros = (0,) * len(shape)
    return pl.BlockSpec(shape, lambda *_: zeros, pipeline_mode=pl.Buffered(1))


def _rows(*vectors, width):
    rows = [jnp.reshape(v, (1, width)).astype(F32) for v in vectors]
    rows.append(jnp.zeros((8 - len(rows), width), F32))
    return jnp.concatenate(rows, axis=0)


def _rsqrt_ms(x):
    return lax.rsqrt(jnp.mean(x * x, axis=-1, keepdims=True) + EPS)


def _colsum(x):
    return jnp.sum(x, axis=0, keepdims=True)


def _seq_tile(s):
    return min(s, SEQ_TILE)


def _mesh_place():
    x, y, c = lax.axis_index("x"), lax.axis_index("y"), lax.axis_index("c")
    return x, y, c


def _flip(place, k):
    x, y, c = place
    return (1 - x if k & 4 else x, 1 - y if k & 2 else y, 1 - c if k & 1 else c)


def _index(place):
    x, y, c = place
    return 4 * x + 2 * y + c


def _allgather_vmem(block, name):
    r, c = block.shape

    def body(x_ref, out_ref, send_sems, recv_sems, local_sem):
        me = _mesh_place()
        mine = pltpu.make_async_copy(x_ref, out_ref.at[_index(me)], local_sem)
        mine.start()

        def copy(k):
            return pltpu.make_async_remote_copy(
                src_ref=x_ref, dst_ref=out_ref.at[_index(me)], send_sem=send_sems.at[k - 1], recv_sem=recv_sems.at[k - 1],
                device_id=_flip(me, k), device_id_type=MESH)

        def arrival(k):
            return pltpu.make_async_remote_copy(
                src_ref=x_ref, dst_ref=out_ref.at[_index(_flip(me, k))], send_sem=send_sems.at[k - 1],
                recv_sem=recv_sems.at[k - 1], device_id=_flip(me, k), device_id_type=MESH)

        for k in range(1, N_DEV):
            copy(k).start()
        for k in range(1, N_DEV):
            arrival(k).wait_recv()
        for k in range(1, N_DEV):
            copy(k).wait_send()
        mine.wait()

    return pl.pallas_call(
        body, name=name,
        out_shape=jax.ShapeDtypeStruct((N_DEV, r, c), block.dtype),
        in_specs=[pl.BlockSpec(memory_space=pltpu.VMEM)],
        out_specs=pl.BlockSpec(memory_space=pltpu.VMEM),
        scratch_shapes=[pltpu.SemaphoreType.DMA((N_DEV - 1,)), pltpu.SemaphoreType.DMA((N_DEV - 1,)),
                        pltpu.SemaphoreType.DMA],
    )(block)


class _Gather:
    def __init__(self, shards):
        n = len(shards)
        self.operands = list(shards)
        self.out_shape = [jax.ShapeDtypeStruct((N_DEV,) + s.shape, s.dtype) for s in shards]
        self.scratch = [pltpu.SemaphoreType.DMA((N_DEV - 1, n)), pltpu.SemaphoreType.DMA((N_DEV - 1, n)),
                        pltpu.SemaphoreType.DMA((n,))]

    def _copies(self, x_refs, out_refs, sems):
        send_sems, recv_sems, local_sems = sems
        n = len(x_refs)
        x, y, c = _mesh_place()
        me, sibling = (x, y, c), (x, y, 1 - c)
        chips = [(1 - x, y), (x, 1 - y), (1 - x, 1 - y)]

        def copy(k, a, block, to, from_input=False):
            rows = out_refs[a].at[_index(block)]
            return pltpu.make_async_remote_copy(
                src_ref=x_refs[a] if from_input else rows, dst_ref=rows,
                send_sem=send_sems.at[k, a], recv_sem=recv_sems.at[k, a], device_id=to, device_id_type=MESH)

        mine = [pltpu.make_async_copy(x_refs[a], out_refs[a].at[_index(me)], local_sems.at[a]) for a in range(n)]
        first = [copy(0, a, me, sibling, from_input=True) for a in range(n)]
        first += [copy(1 + j, a, me, (*chip, c), from_input=True) for j, chip in enumerate(chips) for a in range(n)]
        over_ici = [copy(1 + j, a, (*chip, c), me) for j, chip in enumerate(chips) for a in range(n)]
        passed = [copy(4 + j, a, (*chip, c), sibling) for j, chip in enumerate(chips) for a in range(n)]
        from_sibling = [copy(0, a, sibling, me) for a in range(n)]
        from_sibling += [copy(4 + j, a, (*chip, 1 - c), me) for j, chip in enumerate(chips) for a in range(n)]
        return mine, first, over_ici, passed, from_sibling

    def begin(self, x_refs, out_refs, sems):
        mine, first, _, _, _ = self._copies(x_refs, out_refs, sems)
        for cp in mine + first:
            cp.start()

    def middle(self, x_refs, out_refs, sems):
        _, _, over_ici, passed, _ = self._copies(x_refs, out_refs, sems)
        for arrived, onward in zip(over_ici, passed):
            arrived.wait_recv()
            onward.start()

    def end(self, x_refs, out_refs, sems):
        mine, first, _, passed, from_sibling = self._copies(x_refs, out_refs, sems)
        for cp in from_sibling:
            cp.wait_recv()
        for cp in first + passed:
            cp.wait_send()
        for cp in mine:
            cp.wait()


class _Scatter:
    def __init__(self, stacks):
        n = len(stacks)
        self.operands = list(stacks)
        self.out_shape = [jax.ShapeDtypeStruct((N_DEV - 1,) + s.shape[1:], s.dtype) for s in stacks]
        self.scratch = [pltpu.SemaphoreType.DMA((N_DEV - 1, n)), pltpu.SemaphoreType.DMA((N_DEV - 1, n))]

    def _copies(self, g_refs, out_refs, sems):
        send_sems, recv_sems = sems
        me = _mesh_place()

        def copy(k, a):
            peer = _flip(me, k)
            return pltpu.make_async_remote_copy(
                src_ref=g_refs[a].at[_index(peer)], dst_ref=out_refs[a].at[k - 1],
                send_sem=send_sems.at[k - 1, a], recv_sem=recv_sems.at[k - 1, a], device_id=peer, device_id_type=MESH)

        return [copy(k, a) for k in range(1, N_DEV) for a in range(len(g_refs))]

    def begin(self, g_refs, out_refs, sems):
        for cp in self._copies(g_refs, out_refs, sems):
            cp.start()

    def middle(self, g_refs, out_refs, sems):
        pass

    def end(self, g_refs, out_refs, sems):
        copies = self._copies(g_refs, out_refs, sems)
        for cp in copies:
            cp.wait_recv()
        for cp in copies:
            cp.wait_send()


def _call(body, args, *, name, grid, in_specs, out_specs, out_shape, scratch_shapes=(), exchanges=()):
    if not exchanges:
        outs = pl.pallas_call(body, name=name, grid=grid, in_specs=in_specs, out_specs=out_specs, out_shape=out_shape,
                              scratch_shapes=list(scratch_shapes), compiler_params=_cparams(len(grid)))(*args)
        return outs, []
    (steps,) = grid
    n_in, n_out, n_scr = len(in_specs), len(out_specs), len(scratch_shapes)
    ex_in = [len(ex.operands) for ex in exchanges]
    ex_out = [len(ex.out_shape) for ex in exchanges]
    ex_scr = [len(ex.scratch) for ex in exchanges]

    def split(refs, counts):
        parts, pos = [], 0
        for cnt in counts:
            parts.append(refs[pos:pos + cnt])
            pos += cnt
        return parts

    def wrapped(*refs):
        ins, xin, outs, xout, scr, xscr = split(refs, [n_in, sum(ex_in), n_out, sum(ex_out), n_scr, sum(ex_scr)])
        bound = list(zip(exchanges, split(xin, ex_in), split(xout, ex_out), split(xscr, ex_scr)))
        step = pl.program_id(0)

        def phase(method, at):
            @pl.when(step == at)
            def _():
                for ex, i_refs, o_refs, sems in bound:
                    getattr(ex, method)(i_refs, o_refs, sems)

        phase("begin", 0)
        phase("middle", (3 * steps) // 4)
        body(*ins, *outs, *scr)
        phase("end", steps - 1)

    any_spec = pl.BlockSpec(memory_space=pl.ANY)
    results = pl.pallas_call(
        wrapped, name=name, grid=grid,
        in_specs=list(in_specs) + [any_spec] * sum(ex_in),
        out_specs=list(out_specs) + [any_spec] * sum(ex_out),
        out_shape=list(out_shape) + [s for ex in exchanges for s in ex.out_shape],
        scratch_shapes=list(scratch_shapes) + [s for ex in exchanges for s in ex.scratch],
        compiler_params=_cparams(1),
    )(*args, *[a for ex in exchanges for a in ex.operands])
    return results[:n_out], split(results[n_out:], ex_out)


def _exchange_alone(exchange, name):
    def body(flag_ref):
        flag_ref[...] = jnp.zeros_like(flag_ref)

    _, (outs,) = _call(body, [], name=name, grid=(1,), in_specs=[], out_specs=[pl.BlockSpec((8, LANES), lambda i: (0, 0))],
                       out_shape=[jax.ShapeDtypeStruct((8, LANES), F32)], exchanges=[exchange])
    return outs


def _mod_fwd(c_all, ada_w):
    cols = ada_w.shape[2]

    def body(c_ref, w_ref, mp_ref, act_ref):
        c = c_ref[...]
        act = c * jax.nn.sigmoid(c)
        act_ref[...] = act
        mp_ref[0] = _dot(act.astype(BF16), w_ref[0].astype(BF16))

    return pl.pallas_call(
        body, name="mod_fwd", grid=(DEPTH,),
        in_specs=[pl.BlockSpec((N_DEV, D_MODEL), lambda l: (0, 0)),
                  pl.BlockSpec((1, D_MODEL, cols), lambda l: (l, 0, 0))],
        out_specs=[pl.BlockSpec((1, N_DEV, cols), lambda l: (l, 0, 0)),
                   pl.BlockSpec((N_DEV, D_MODEL), lambda l: (0, 0))],
        out_shape=[jax.ShapeDtypeStruct((DEPTH, N_DEV, cols), F32), jax.ShapeDtypeStruct((N_DEV, D_MODEL), F32)],
        compiler_params=_cparams(1),
    )(c_all, ada_w)


def _ada_grad(c_act, dmod_cols):
    cols = dmod_cols.shape[2]

    def body(act_ref, dm_ref, g_ref):
        g_ref[0] = _dot(act_ref[...].astype(BF16), dm_ref[0].astype(BF16), TN)

    return pl.pallas_call(
        body, name="ada_grad", grid=(DEPTH,),
        in_specs=[pl.BlockSpec((N_DEV, D_MODEL), lambda l: (0, 0)),
                  pl.BlockSpec((1, N_DEV, cols), lambda l: (l, 0, 0))],
        out_specs=pl.BlockSpec((1, D_MODEL, cols), lambda l: (l, 0, 0)),
        out_shape=jax.ShapeDtypeStruct((DEPTH, D_MODEL, cols), F32),
        compiler_params=_cparams(1),
    )(c_act, dmod_cols)


def _rotate(t, cos, sin_lo, sin_hi):
    return t * cos + pltpu.roll(t, LANES - 8, 1) * sin_lo + pltpu.roll(t, 8, 1) * sin_hi


def _rotate_bwd(t, cos, sin_lo, sin_hi):
    return t * cos + pltpu.roll(t * sin_lo, 8, 1) + pltpu.roll(t * sin_hi, LANES - 8, 1)


def _fwd_in(x, vec, w_in_t, b_in, rot, name, exchanges=()):
    s = x.shape[0]
    ts = _seq_tile(s)

    def body(x_ref, vec_ref, w_ref, b_ref, cos_ref, lo_ref, hi_ref, q_ref, k_ref, v_ref, u_ref):
        xt = x_ref[...]
        h = xt * _rsqrt_ms(xt) * vec_ref[0:1, :] * vec_ref[1:2, :] + vec_ref[2:3, :]
        proj = _dot(h.astype(BF16), w_ref[...], NT) + b_ref[0:1, :]
        cos, lo, hi = cos_ref[...], lo_ref[...], hi_ref[...]
        for j in range(ATTN_WIDTH // LANES):
            q_ref[:, j * LANES:(j + 1) * LANES] = (
                _rotate(proj[:, j * LANES:(j + 1) * LANES], cos, lo, hi) * Q_SCALE).astype(BF16)
        k_ref[...] = _rotate(proj[:, ATTN_WIDTH:ATTN_WIDTH + KV_WIDTH], cos, lo, hi).astype(BF16)
        v_ref[...] = proj[:, ATTN_WIDTH + KV_WIDTH:ATTN_WIDTH + 2 * KV_WIDTH].astype(BF16)
        u_ref[...] = proj[:, ATTN_WIDTH + 2 * KV_WIDTH:]

    row = lambda w: pl.BlockSpec((ts, w), lambda i: (i, 0))
    return _call(
        body, [x, vec, w_in_t, b_in, *rot], name=name, grid=(s // ts,),
        in_specs=[row(D_MODEL), _resident((8, D_MODEL)), _resident((IN_WIDTH, D_MODEL)), _resident((8, IN_WIDTH)),
                  row(LANES), row(LANES), row(LANES)],
        out_specs=[row(ATTN_WIDTH), row(KV_WIDTH), row(KV_WIDTH), row(POOL_WIDTH)],
        out_shape=[jax.ShapeDtypeStruct((s, ATTN_WIDTH), BF16), jax.ShapeDtypeStruct((s, KV_WIDTH), BF16),
                   jax.ShapeDtypeStruct((s, KV_WIDTH), BF16), jax.ShapeDtypeStruct((s, POOL_WIDTH), F32)],
        exchanges=exchanges)


def _band_mask_t(first_block):
    key = lax.broadcasted_iota(jnp.int32, (2 * BLOCK, 2 * BLOCK), 0)
    qry = lax.broadcasted_iota(jnp.int32, (2 * BLOCK, 2 * BLOCK), 1) & (BLOCK - 1)
    d = key - qry
    return (d >= 1) & (d <= BLOCK) & ((key >= BLOCK) | jnp.logical_not(first_block))


def _placed(both, kh, b):
    lane = lax.broadcasted_iota(jnp.int32, (1, LANES), 1)
    src = both if kh == b else pltpu.roll(both, HEAD_DIM, 1)
    return jnp.where((lane >= b * HEAD_DIM) & (lane < (b + 1) * HEAD_DIM), src, jnp.zeros_like(src))


def _pair(ref, kh, r):
    j = 2 * kh + r
    return ref[:, j * LANES:(j + 1) * LANES]


def _sink_row(sink_ref, kh, b):
    lane = lax.broadcasted_iota(jnp.int32, (1, 2 * BLOCK), 1)
    return jnp.where(lane < BLOCK, sink_ref[GROUP * kh + b], sink_ref[GROUP * kh + 2 + b])


def _attn_fwd_t(sinks, q, k, v, name, exchanges=()):
    s = q.shape[0]
    nb = s // BLOCK
    cur = lambda w: pl.BlockSpec((BLOCK, w), lambda i: (i, 0))
    prev = lambda w: pl.BlockSpec((BLOCK, w), lambda i: (jnp.maximum(i - 1, 0), 0))

    def body(sink_ref, q_ref, kc_ref, kp_ref, vc_ref, vp_ref, o_ref, l_ref):
        valid = _band_mask_t(pl.program_id(0) == 0)
        k_both = jnp.concatenate([kp_ref[...], kc_ref[...]], axis=0)
        v_both = jnp.concatenate([vp_ref[...], vc_ref[...]], axis=0)
        for kh in range(N_KV_HEADS):
            qg = jnp.concatenate([_pair(q_ref, kh, 0), _pair(q_ref, kh, 1)], axis=0)
            keys = jnp.concatenate([_placed(k_both, kh, 0), _placed(k_both, kh, 1)], axis=0)
            st = _dot(keys, qg, NT)
            out = jnp.zeros((2 * BLOCK, LANES), F32)
            for b in range(2):
                sc = jnp.where(valid, st[b * 2 * BLOCK:(b + 1) * 2 * BLOCK, :], NEG_INF)
                sink = _sink_row(sink_ref, kh, b)
                m = jnp.maximum(jnp.max(sc, axis=0, keepdims=True), sink)
                p = jnp.exp(sc - m)
                den = jnp.sum(p, axis=0, keepdims=True) + jnp.exp(sink - m)
                out = out + _dot((p * (1.0 / den)).astype(BF16), _placed(v_both, kh, b), TN)
                lse = m + jnp.log(den)
                for r in range(2):
                    h = GROUP * kh + 2 * r + b
                    l_ref[h:h + 1, :] = lse[:, r * BLOCK:(r + 1) * BLOCK]
            for r in range(2):
                j = 2 * kh + r
                o_ref[:, j * LANES:(j + 1) * LANES] = out[r * BLOCK:(r + 1) * BLOCK, :].astype(BF16)

    return _call(
        body, [sinks, q, k, k, v, v], name=name, grid=(nb,),
        in_specs=[pl.BlockSpec(memory_space=pltpu.SMEM), cur(ATTN_WIDTH), cur(KV_WIDTH), prev(KV_WIDTH),
                  cur(KV_WIDTH), prev(KV_WIDTH)],
        out_specs=[cur(ATTN_WIDTH), pl.BlockSpec((N_HEADS, BLOCK), lambda i: (0, i))],
        out_shape=[jax.ShapeDtypeStruct((s, ATTN_WIDTH), BF16), jax.ShapeDtypeStruct((N_HEADS, s), F32)],
        exchanges=exchanges)


def _attn_bwd_t(sinks, q, do, lse, k, v, name, exchanges=()):
    s = q.shape[0]
    nb = s // BLOCK
    cur = lambda w: pl.BlockSpec((BLOCK, w), lambda i: (i, 0))
    prev = lambda w: pl.BlockSpec((BLOCK, w), lambda i: (jnp.maximum(i - 1, 0), 0))

    def body(sink_ref, q_ref, do_ref, l_ref, kc_ref, kp_ref, vc_ref, vp_ref,
             dq_ref, dkc_ref, dkp_ref, dvc_ref, dvp_ref, dsink_ref):
        @pl.when(pl.program_id(0) == 0)
        def _():
            dsink_ref[...] = jnp.zeros_like(dsink_ref)

        valid = _band_mask_t(pl.program_id(0) == 0)
        k_both = jnp.concatenate([kp_ref[...], kc_ref[...]], axis=0)
        v_both = jnp.concatenate([vp_ref[...], vc_ref[...]], axis=0)
        lane = lax.broadcasted_iota(jnp.int32, (1, LANES), 1)
        col = lax.broadcasted_iota(jnp.int32, (1, 2 * BLOCK), 1)
        dk_heads, dv_heads = [], []
        for kh in range(N_KV_HEADS):
            qg = jnp.concatenate([_pair(q_ref, kh, 0), _pair(q_ref, kh, 1)], axis=0)
            dog = jnp.concatenate([_pair(do_ref, kh, 0), _pair(do_ref, kh, 1)], axis=0)
            k_placed = [_placed(k_both, kh, b) for b in range(2)]
            v_placed = [_placed(v_both, kh, b) for b in range(2)]
            st = _dot(jnp.concatenate(k_placed, axis=0), qg, NT)
            dpt = _dot(jnp.concatenate(v_placed, axis=0), dog, NT)
            dqg = jnp.zeros((2 * BLOCK, LANES), F32)
            dk_b, dv_b = [], []
            for b in range(2):
                rows = slice(b * 2 * BLOCK, (b + 1) * 2 * BLOCK)
                heads = [GROUP * kh + 2 * r + b for r in range(2)]
                lse_row = jnp.concatenate([l_ref[h:h + 1, :] for h in heads], axis=1)
                p = jnp.where(valid, jnp.exp(st[rows, :] - lse_row), 0.0)
                dp = dpt[rows, :]
                delta = jnp.sum(p * dp, axis=0, keepdims=True)
                sink_pull = jnp.exp(_sink_row(sink_ref, kh, b) - lse_row) * delta
                dsink_ref[heads[0]:heads[0] + 1, :] += -jnp.sum(jnp.where(col < BLOCK, sink_pull, 0.0))
                dsink_ref[heads[1]:heads[1] + 1, :] += -jnp.sum(jnp.where(col < BLOCK, 0.0, sink_pull))
                ds = (p * (dp - delta)).astype(BF16)
                dqg = dqg + _dot(ds, k_placed[b], TN)
                dk_b.append(_dot(ds, qg))
                dv_b.append(_dot(p.astype(BF16), dog))
            for parts, total in ((dk_b, dk_heads), (dv_b, dv_heads)):
                kept = jnp.where(lane < HEAD_DIM, parts[0], parts[1])
                total.append(kept + pltpu.roll(kept, HEAD_DIM, 1))
            for r in range(2):
                j = 2 * kh + r
                dq_ref[:, j * LANES:(j + 1) * LANES] = dqg[r * BLOCK:(r + 1) * BLOCK, :] * Q_SCALE
        dk = jnp.where(lane < HEAD_DIM, dk_heads[0], dk_heads[1])
        dv = jnp.where(lane < HEAD_DIM, dv_heads[0], dv_heads[1])
        dkp_ref[...] = dk[0:BLOCK, :]
        dkc_ref[...] = dk[BLOCK:, :]
        dvp_ref[...] = dv[0:BLOCK, :]
        dvc_ref[...] = dv[BLOCK:, :]

    kv = jax.ShapeDtypeStruct((s, KV_WIDTH), F32)
    return _call(
        body, [sinks, q, do, lse, k, k, v, v], name=name, grid=(nb,),
        in_specs=[pl.BlockSpec(memory_space=pltpu.SMEM), cur(ATTN_WIDTH), cur(ATTN_WIDTH),
                  pl.BlockSpec((N_HEADS, BLOCK), lambda i: (0, i)),
                  cur(KV_WIDTH), prev(KV_WIDTH), cur(KV_WIDTH), prev(KV_WIDTH)],
        out_specs=[cur(ATTN_WIDTH), cur(KV_WIDTH), cur(KV_WIDTH), cur(KV_WIDTH), cur(KV_WIDTH),
                   pl.BlockSpec((8, LANES), lambda i: (0, 0))],
        out_shape=[jax.ShapeDtypeStruct((s, ATTN_WIDTH), F32), kv, kv, kv, kv, jax.ShapeDtypeStruct((8, LANES), F32)],
        exchanges=exchanges)


def _pool_counts(tile, ts):
    t = (tile * ts + lax.broadcasted_iota(jnp.int32, (ts, 1), 0) + 1).astype(F32)
    return [jnp.minimum(t, float(w)) for w in POOL_WINDOWS]


def _pooled(u, halo, counts, gi):
    cols = slice(gi * POOL_GROUP_WIDTH, (gi + 1) * POOL_GROUP_WIDTH)
    acc = jnp.concatenate([halo[:, cols], u[:, cols]], axis=0)
    shift = 1
    while shift < POOL_WINDOWS[gi]:
        acc = acc + pltpu.roll(acc, shift, 0)
        shift *= 2
    return acc[POOL_HALO:, :] / counts[gi] - u[:, cols]


def _fwd_out(ao, u, pool_w, psc, w_out, x, vec, name):
    s = x.shape[0]
    ts = _seq_tile(s)
    hb = ts // POOL_HALO

    def body(ao_ref, u_ref, uh_ref, pw_ref, psc_ref, w_ref, x_ref, vec_ref, x1_ref, mix_ref):
        i = pl.program_id(0)
        u_t = u_ref[...]
        halo = jnp.where(i > 0, uh_ref[...], 0.0)
        counts = _pool_counts(i, ts)
        mix = _dot(ao_ref[...], w_ref[0:ATTN_WIDTH, :])
        for gi in range(len(POOL_WINDOWS)):
            cols = slice(gi * POOL_GROUP_WIDTH, (gi + 1) * POOL_GROUP_WIDTH)
            og = _dot(_pooled(u_t, halo, counts, gi).astype(BF16), pw_ref[gi]) * psc_ref[0:1, cols]
            mix = mix + _dot(og.astype(BF16), w_ref[ATTN_WIDTH + gi * POOL_GROUP_WIDTH:
                                                    ATTN_WIDTH + (gi + 1) * POOL_GROUP_WIDTH, :])
        mix_ref[...] = mix
        x1_ref[...] = x_ref[...] + vec_ref[0:1, :] * (mix * _rsqrt_ms(mix) * vec_ref[1:2, :])

    row = lambda w: pl.BlockSpec((ts, w), lambda i: (i, 0))
    return pl.pallas_call(
        body, name=name, grid=(s // ts,),
        in_specs=[row(ATTN_WIDTH), row(POOL_WIDTH),
                  pl.BlockSpec((POOL_HALO, POOL_WIDTH), lambda i: (jnp.maximum(i * hb - 1, 0), 0)),
                  _resident(pool_w.shape), _resident((8, POOL_WIDTH)), _resident((D_MODEL, D_MODEL)),
                  row(D_MODEL), _resident((8, D_MODEL))],
        out_specs=[row(D_MODEL), row(D_MODEL)],
        out_shape=[jax.ShapeDtypeStruct((s, D_MODEL), F32), jax.ShapeDtypeStruct((s, D_MODEL), F32)],
        compiler_params=_cparams(1),
    )(ao, u, u, pool_w, psc, w_out, x, vec)


def _ffn_fwd(x1, vec, w_gate_t, w_up_t, w_down, name, exchanges=(), target=None):
    s = x1.shape[0]
    ts = _seq_tile(s)

    def body(x_ref, *refs):
        if target is None:
            vec_ref, wg_ref, wu_ref, wd_ref, x2_ref, f_ref, gt_ref, up_ref, h2_ref = refs
        else:
            t_ref, vec_ref, wg_ref, wu_ref, wd_ref, x2_ref, f_ref, gt_ref, up_ref, h2_ref, loss_ref = refs
        xt = x_ref[...]
        h2 = (xt * _rsqrt_ms(xt) * vec_ref[0:1, :] * vec_ref[1:2, :] + vec_ref[2:3, :]).astype(BF16)
        h2_ref[...] = h2
        f = jnp.zeros((ts, D_MODEL), F32)
        for c in range(D_FF // FF_CHUNK):
            cols = slice(c * FF_CHUNK, (c + 1) * FF_CHUNK)
            g = _dot(h2, wg_ref[cols, :], NT)
            up = _dot(h2, wu_ref[cols, :], NT)
            act = ((g * jax.nn.sigmoid(g)) * up).astype(BF16)
            gt_ref[:, cols] = g.astype(BF16)
            up_ref[:, cols] = up.astype(BF16)
            f = f + _dot(act, wd_ref[cols, :])
        f_ref[...] = f
        x2 = xt + vec_ref[3:4, :] * (f * _rsqrt_ms(f) * vec_ref[4:5, :])
        if target is None:
            x2_ref[...] = x2
        else:
            @pl.when(pl.program_id(0) == 0)
            def _():
                loss_ref[...] = jnp.zeros_like(loss_ref)

            diff = x2 - t_ref[...]
            x2_ref[...] = diff / D_MODEL
            loss_ref[...] += 0.5 * jnp.sum(jnp.mean(diff * diff, axis=-1, keepdims=True))

    row = lambda w: pl.BlockSpec((ts, w), lambda i: (i, 0))
    wide = jax.ShapeDtypeStruct((s, D_FF), BF16)
    with_loss = target is not None
    return _call(
        body, [x1] + ([target] if with_loss else []) + [vec, w_gate_t, w_up_t, w_down], name=name, grid=(s // ts,),
        in_specs=[row(D_MODEL)] * (2 if with_loss else 1) + [
            _resident((8, D_MODEL)), _resident((D_FF, D_MODEL)), _resident((D_FF, D_MODEL)), _resident((D_FF, D_MODEL))],
        out_specs=[row(D_MODEL), row(D_MODEL), row(D_FF), row(D_FF), row(D_MODEL)] + (
            [pl.BlockSpec((8, LANES), lambda i: (0, 0))] if with_loss else []),
        out_shape=[jax.ShapeDtypeStruct((s, D_MODEL), F32), jax.ShapeDtypeStruct((s, D_MODEL), F32), wide, wide,
                   jax.ShapeDtypeStruct((s, D_MODEL), BF16)] + (
            [jax.ShapeDtypeStruct((8, LANES), F32)] if with_loss else []),
        exchanges=exchanges)


def _norm_bwd(dy_hat, x_hat, r):
    return r * (dy_hat - x_hat * jnp.mean(dy_hat * x_hat, axis=-1, keepdims=True))


def _accumulate_rows(ref, rows):
    for j, val in enumerate(rows):
        ref[j:j + 1, :] += val


def _ffn_bwd_act(dx2, f, gt, up, vec, w_down, name, exchanges=()):
    s = dx2.shape[0]
    ts = _seq_tile(s)

    def body(dx2_ref, f_ref, gt_ref, up_ref, vec_ref, wd_ref, dgt_ref, dup_ref, dwd_ref, cs_ref):
        @pl.when(pl.program_id(0) == 0)
        def _():
            cs_ref[...] = jnp.zeros_like(cs_ref)
            dwd_ref[...] = jnp.zeros_like(dwd_ref)

        dx2_t, f_t = dx2_ref[...], f_ref[...]
        gate, g_post = vec_ref[3:4, :], vec_ref[4:5, :]
        rf = _rsqrt_ms(f_t)
        f_hat = f_t * rf
        df = _norm_bwd(dx2_t * gate * g_post, f_hat, rf).astype(BF16)
        _accumulate_rows(cs_ref, [_colsum(dx2_t * (f_hat * g_post)), _colsum(dx2_t * gate * f_hat)])
        for c in range(D_FF // FF_CHUNK):
            cols = slice(c * FF_CHUNK, (c + 1) * FF_CHUNK)
            dact = _dot(df, wd_ref[cols, :], NT)
            g, u_t = gt_ref[:, cols].astype(F32), up_ref[:, cols].astype(F32)
            sg = jax.nn.sigmoid(g)
            silu = g * sg
            dgt_ref[:, cols] = (dact * u_t * (sg * (1.0 + g * (1.0 - sg)))).astype(BF16)
            dup_ref[:, cols] = (dact * silu).astype(BF16)
            dwd_ref[cols, :] += _dot((silu * u_t).astype(BF16), df, TN)

    row = lambda w: pl.BlockSpec((ts, w), lambda i: (i, 0))
    wide = jax.ShapeDtypeStruct((s, D_FF), BF16)
    return _call(
        body, [dx2, f, gt, up, vec, w_down], name=name, grid=(s // ts,),
        in_specs=[row(D_MODEL), row(D_MODEL), row(D_FF), row(D_FF), _resident((8, D_MODEL)), _resident((D_FF, D_MODEL))],
        out_specs=[row(D_FF), row(D_FF), _resident((D_FF, D_MODEL)), pl.BlockSpec((8, D_MODEL), lambda i: (0, 0))],
        out_shape=[wide, wide, jax.ShapeDtypeStruct((D_FF, D_MODEL), F32), jax.ShapeDtypeStruct((8, D_MODEL), F32)],
        exchanges=exchanges)


def _ffn_bwd_in(dx2, x1, dgt, dup, vec, w_gate_t, w_up_t, name, exchanges=()):
    s = x1.shape[0]
    ts = _seq_tile(s)

    def body(dx2_ref, x1_ref, dgt_ref, dup_ref, vec_ref, wg_ref, wu_ref, dx1_ref, cs_ref):
        @pl.when(pl.program_id(0) == 0)
        def _():
            cs_ref[...] = jnp.zeros_like(cs_ref)

        g_pre, one_scale = vec_ref[0:1, :], vec_ref[1:2, :]
        dh2 = _dot(dgt_ref[...], wg_ref[...]) + _dot(dup_ref[...], wu_ref[...])
        x1_t = x1_ref[...]
        r1 = _rsqrt_ms(x1_t)
        x_hat = x1_t * r1
        dx1_ref[...] = dx2_ref[...] + _norm_bwd(dh2 * g_pre * one_scale, x_hat, r1)
        _accumulate_rows(cs_ref, [_colsum(dh2), _colsum(dh2 * (x_hat * g_pre)), _colsum(dh2 * one_scale * x_hat)])

    row = lambda w: pl.BlockSpec((ts, w), lambda i: (i, 0))
    return _call(
        body, [dx2, x1, dgt, dup, vec, w_gate_t, w_up_t], name=name, grid=(s // ts,),
        in_specs=[row(D_MODEL), row(D_MODEL), row(D_FF), row(D_FF), _resident((8, D_MODEL)),
                  _resident((D_FF, D_MODEL)), _resident((D_FF, D_MODEL))],
        out_specs=[row(D_MODEL), pl.BlockSpec((8, D_MODEL), lambda i: (0, 0))],
        out_shape=[jax.ShapeDtypeStruct((s, D_MODEL), F32), jax.ShapeDtypeStruct((8, D_MODEL), F32)],
        exchanges=exchanges)


def _weight_grad(a, b, name, exchanges=()):
    s, m = a.shape
    n = b.shape[1]
    tk = min(s, 2 * SEQ_TILE)
    steps = s // tk

    def body(a_ref, b_ref, o_ref, acc_ref):
        i = pl.program_id(0)

        @pl.when(i == 0)
        def _():
            acc_ref[...] = jnp.zeros_like(acc_ref)

        b_t = b_ref[...]
        for c in range(m // FF_CHUNK):
            rows = slice(c * FF_CHUNK, (c + 1) * FF_CHUNK)
            acc_ref[rows, :] += _dot(a_ref[:, rows], b_t, TN)

        @pl.when(i == steps - 1)
        def _():
            o_ref[...] = acc_ref[...].astype(BF16)

    (grad,), ex_outs = _call(
        body, [a, b], name=name, grid=(steps,),
        in_specs=[pl.BlockSpec((tk, m), lambda i: (i, 0)), pl.BlockSpec((tk, n), lambda i: (i, 0))],
        out_specs=[_resident((m, n))],
        out_shape=[jax.ShapeDtypeStruct((m, n), BF16)],
        scratch_shapes=[pltpu.VMEM((m, n), F32)],
        exchanges=exchanges)
    return grad, ex_outs


def _out_bwd(dx1, mix, ao, u, pool_w, psc, w_out, vec, name, exchanges=()):
    s = dx1.shape[0]
    ts = _seq_tile(s)
    nt = s // ts
    hb = ts // POOL_HALO
    ng = len(POOL_WINDOWS)

    def body(dx1_ref, mix_ref, ao_ref, u_ref, uh_ref, pw_ref, psc_ref, w_ref, vec_ref,
             do_ref, du_ref, cat_ref, dmix_ref, cs_ref, dpw_ref, dpsc_ref, carry_ref):
        i = pl.program_id(0)
        tile = nt - 1 - i

        @pl.when(i == 0)
        def _():
            cs_ref[...] = jnp.zeros_like(cs_ref)
            dpw_ref[...] = jnp.zeros_like(dpw_ref)
            dpsc_ref[...] = jnp.zeros_like(dpsc_ref)
            carry_ref[...] = jnp.zeros_like(carry_ref)

        dx1_t, mix_t = dx1_ref[...], mix_ref[...]
        gate, g_post = vec_ref[0:1, :], vec_ref[1:2, :]
        rm = _rsqrt_ms(mix_t)
        m_hat = mix_t * rm
        dmix = _norm_bwd(dx1_t * gate * g_post, m_hat, rm).astype(BF16)
        dmix_ref[...] = dmix
        _accumulate_rows(cs_ref, [_colsum(dx1_t * (m_hat * g_post)), _colsum(dx1_t * gate * m_hat)])
        do_ref[...] = _dot(dmix, w_ref[0:ATTN_WIDTH, :], NT).astype(BF16)
        cat_ref[:, 0:ATTN_WIDTH] = ao_ref[...]

        u_t = u_ref[...]
        halo = jnp.where(tile > 0, uh_ref[...], 0.0)
        counts = _pool_counts(tile, ts)
        for gi in range(ng):
            cols = slice(gi * POOL_GROUP_WIDTH, (gi + 1) * POOL_GROUP_WIDTH)
            wide_cols = slice(ATTN_WIDTH + gi * POOL_GROUP_WIDTH, ATTN_WIDTH + (gi + 1) * POOL_GROUP_WIDTH)
            scale = psc_ref[0:1, cols]
            pooled = _pooled(u_t, halo, counts, gi).astype(BF16)
            og = _dot(pooled, pw_ref[gi])
            cat_ref[:, wide_cols] = (og * scale).astype(BF16)
            d_out = _dot(dmix, w_ref[wide_cols, :], NT)
            dpsc_ref[0:1, cols] += _colsum(d_out * og)
            d_og = (d_out * scale).astype(BF16)
            dpw_ref[gi] += _dot(pooled, d_og, TN)
            d_pooled = _dot(d_og, pw_ref[gi], NT)
            spread = d_pooled / counts[gi]
            acc = jnp.concatenate([spread, carry_ref[:, cols]], axis=0)
            shift = 1
            while shift < POOL_WINDOWS[gi]:
                acc = acc + pltpu.roll(acc, ts + POOL_HALO - shift, 0)
                shift *= 2
            du_ref[:, cols] = acc[0:ts, :] - d_pooled
            carry_ref[:, cols] = spread[0:POOL_HALO, :]

    row = lambda w: pl.BlockSpec((ts, w), lambda i: (nt - 1 - i, 0))
    fixed = lambda shape: pl.BlockSpec(shape, lambda i: (0,) * len(shape))
    return _call(
        body, [dx1, mix, ao, u, u, pool_w, psc, w_out, vec], name=name, grid=(nt,),
        in_specs=[row(D_MODEL), row(D_MODEL), row(ATTN_WIDTH), row(POOL_WIDTH),
                  pl.BlockSpec((POOL_HALO, POOL_WIDTH), lambda i: (jnp.maximum((nt - 1 - i) * hb - 1, 0), 0)),
                  _resident(pool_w.shape), _resident((8, POOL_WIDTH)), _resident((D_MODEL, D_MODEL)),
                  _resident((8, D_MODEL))],
        out_specs=[row(ATTN_WIDTH), row(POOL_WIDTH), row(D_MODEL), row(D_MODEL), fixed((8, D_MODEL)),
                   fixed(pool_w.shape), fixed((8, POOL_WIDTH))],
        out_shape=[jax.ShapeDtypeStruct((s, ATTN_WIDTH), BF16), jax.ShapeDtypeStruct((s, POOL_WIDTH), F32),
                   jax.ShapeDtypeStruct((s, D_MODEL), BF16), jax.ShapeDtypeStruct((s, D_MODEL), BF16),
                   jax.ShapeDtypeStruct((8, D_MODEL), F32), jax.ShapeDtypeStruct(pool_w.shape, F32),
                   jax.ShapeDtypeStruct((8, POOL_WIDTH), F32)],
        scratch_shapes=[pltpu.VMEM((POOL_HALO, POOL_WIDTH), F32)],
        exchanges=exchanges)


def _in_bwd(dq, dkc, dkp, dvc, dvp, du, x, dx1, vec, w_in_t, rot, name):
    s = x.shape[0]
    ts = _seq_tile(s)
    nt = s // ts
    bpt = ts // BLOCK

    def body(dq_ref, dkc_ref, dkp_ref, dkn_ref, dvc_ref, dvp_ref, dvn_ref, du_ref, x_ref, dx1_ref, vec_ref, w_ref,
             cos_ref, lo_ref, hi_ref, dx_ref, dproj_ref, h_ref, cs_ref, db_ref):
        i = pl.program_id(0)

        @pl.when(i == 0)
        def _():
            cs_ref[...] = jnp.zeros_like(cs_ref)
            db_ref[...] = jnp.zeros_like(db_ref)

        cos, lo, hi = cos_ref[...], lo_ref[...], hi_ref[...]

        def with_next_block(cur_ref, prev_ref, next_ref):
            nxt = jnp.where(i < nt - 1, next_ref[...], 0.0)
            later = nxt if bpt == 1 else jnp.concatenate([prev_ref[BLOCK:, :], nxt], axis=0)
            return cur_ref[...] + later

        pieces = [_rotate_bwd(dq_ref[:, j * LANES:(j + 1) * LANES], cos, lo, hi) for j in range(ATTN_WIDTH // LANES)]
        pieces.append(_rotate_bwd(with_next_block(dkc_ref, dkp_ref, dkn_ref), cos, lo, hi))
        pieces.append(with_next_block(dvc_ref, dvp_ref, dvn_ref))
        pieces.append(du_ref[...])
        dproj = jnp.concatenate(pieces, axis=1)
        db_ref[0:1, :] += _colsum(dproj)
        dproj_b = dproj.astype(BF16)
        dproj_ref[...] = dproj_b
        dh = _dot(dproj_b, w_ref[...])

        xt = x_ref[...]
        g_pre, one_scale = vec_ref[0:1, :], vec_ref[1:2, :]
        r = _rsqrt_ms(xt)
        x_hat = xt * r
        h_ref[...] = (x_hat * g_pre * one_scale + vec_ref[2:3, :]).astype(BF16)
        dx_ref[...] = dx1_ref[...] + _norm_bwd(dh * g_pre * one_scale, x_hat, r)
        _accumulate_rows(cs_ref, [_colsum(dh), _colsum(dh * (x_hat * g_pre)), _colsum(dh * one_scale * x_hat)])

    row = lambda w: pl.BlockSpec((ts, w), lambda i: (i, 0))
    nxt = pl.BlockSpec((BLOCK, KV_WIDTH), lambda i: (jnp.minimum((i + 1) * bpt, s // BLOCK - 1), 0))
    fixed = lambda shape: pl.BlockSpec(shape, lambda i: (0,) * len(shape))
    return pl.pallas_call(
        body, name=name, grid=(nt,),
        in_specs=[row(ATTN_WIDTH), row(KV_WIDTH), row(KV_WIDTH), nxt, row(KV_WIDTH), row(KV_WIDTH), nxt,
                  row(POOL_WIDTH), row(D_MODEL), row(D_MODEL), _resident((8, D_MODEL)), _resident((IN_WIDTH, D_MODEL)),
                  row(LANES), row(LANES), row(LANES)],
        out_specs=[row(D_MODEL), row(IN_WIDTH), row(D_MODEL), fixed((8, D_MODEL)), fixed((8, IN_WIDTH))],
        out_shape=[jax.ShapeDtypeStruct((s, D_MODEL), F32), jax.ShapeDtypeStruct((s, IN_WIDTH), BF16),
                   jax.ShapeDtypeStruct((s, D_MODEL), BF16), jax.ShapeDtypeStruct((8, D_MODEL), F32),
                   jax.ShapeDtypeStruct((8, IN_WIDTH), F32)],
        compiler_params=_cparams(1),
    )(dq, dkc, dkp, dkp, dvc, dvp, dvp, du, x, dx1, vec, w_in_t, *rot)


def _row_tile(rows):
    for t in (512, 352, 256, 176, 160, 128, 64, 32, 16, 8):
        if rows % t == 0:
            return t
    raise ValueError(f"no row tile for {rows} rows")


def _sum_parts(own, others, name):
    r, c = own.shape
    n_other = others.shape[0]
    tr = _row_tile(r)

    def body(own_ref, oth_ref, o_ref):
        acc = own_ref[...].astype(F32)
        for k in range(n_other):
            acc = acc + oth_ref[k].astype(F32)
        o_ref[...] = acc

    return pl.pallas_call(
        body, name=name, grid=(r // tr,),
        in_specs=[pl.BlockSpec((tr, c), lambda i: (i, 0)), pl.BlockSpec((n_other, tr, c), lambda i: (0, i, 0))],
        out_specs=pl.BlockSpec((tr, c), lambda i: (i, 0)),
        out_shape=jax.ShapeDtypeStruct((r, c), F32),
        compiler_params=_cparams(1),
    )(own, others)


def _adamw(w, g, m, v, name):
    r, c = w.shape
    tr = _row_tile(r)

    def body(w_ref, g_ref, m_ref, v_ref, d_ref, nm_ref, nv_ref):
        g_t = g_ref[...]
        m_new = ADAM_B1 * m_ref[...] + (1.0 - ADAM_B1) * g_t
        v_new = ADAM_B2 * v_ref[...] + (1.0 - ADAM_B2) * (g_t * g_t)
        m_hat = m_new / (1.0 - ADAM_B1 ** ADAM_STEP)
        v_hat = v_new / (1.0 - ADAM_B2 ** ADAM_STEP)
        d_ref[...] = -ADAM_LR * (m_hat / (jnp.sqrt(v_hat) + ADAM_EPS) + ADAM_WD * w_ref[...])
        nm_ref[...] = m_new
        nv_ref[...] = v_new

    spec = pl.BlockSpec((tr, c), lambda i: (i, 0))
    shape = jax.ShapeDtypeStruct((r, c), F32)
    return pl.pallas_call(
        body, name=name, grid=(r // tr,), in_specs=[spec] * 4, out_specs=[spec] * 3, out_shape=[shape] * 3,
        compiler_params=_cparams(1),
    )(w, g, m, v)


SMALL_ROWS = 160


def _pack_small(dmod, gains, b_in, pool_scale, sinks, pool_w):
    pad = lambda a, width: jnp.pad(a, ((0, 0), (0, width - a.shape[1])))
    parts = [dmod.reshape(DEPTH * N_MOD, D_MODEL), gains.reshape(4 * DEPTH, D_MODEL),
             pad(b_in, 2 * D_MODEL).reshape(2 * DEPTH, D_MODEL), pool_scale.reshape(1, D_MODEL),
             pad(sinks.reshape(1, DEPTH * N_HEADS), D_MODEL), pool_w.reshape(-1, D_MODEL)]
    packed = jnp.concatenate(parts, axis=0)
    return jnp.pad(packed, ((0, SMALL_ROWS - packed.shape[0]), (0, 0)))


def _unpack_small(p):
    r = DEPTH * N_MOD
    dmod = p[0:r].reshape(DEPTH, N_MOD * D_MODEL)
    gains = p[r:r + 4 * DEPTH].reshape(4, DEPTH, D_MODEL)
    r += 4 * DEPTH
    b_in = p[r:r + 2 * DEPTH].reshape(DEPTH, 2 * D_MODEL)[:, :IN_WIDTH]
    r += 2 * DEPTH
    pool_scale = p[r].reshape(DEPTH, POOL_WIDTH)
    sinks = p[r + 1, :DEPTH * N_HEADS].reshape(DEPTH, N_HEADS)
    pool_w = p[r + 2:r + 2 + 128].reshape(DEPTH, len(POOL_WINDOWS), POOL_GROUP_WIDTH, POOL_GROUP_WIDTH)
    return dmod, gains, b_in, pool_scale, sinks, pool_w


LAYER_ROWS = 80


def _pack_layer(dmod, gains, b_in, pool_scale, sinks, pool_w):
    misc = jnp.concatenate([pool_scale, sinks, jnp.zeros((D_MODEL - POOL_WIDTH - N_HEADS,), F32)])
    parts = [dmod.reshape(N_MOD, D_MODEL), gains, jnp.pad(b_in, (0, 2 * D_MODEL - IN_WIDTH)).reshape(2, D_MODEL),
             misc.reshape(1, D_MODEL), pool_w.reshape(-1, D_MODEL)]
    packed = jnp.concatenate(parts, axis=0)
    return jnp.pad(packed, ((0, LAYER_ROWS - packed.shape[0]), (0, 0)))


def _unpack_layer(p):
    r = N_MOD + 4
    b_in = p[r:r + 2].reshape(2 * D_MODEL)[:IN_WIDTH]
    pool_w = p[r + 3:r + 3 + 64].reshape(len(POOL_WINDOWS), POOL_GROUP_WIDTH, POOL_GROUP_WIDTH)
    return (p[0:N_MOD].reshape(N_MOD * D_MODEL), p[N_MOD:r], b_in, p[r + 2, :POOL_WIDTH],
            p[r + 2, POOL_WIDTH:POOL_WIDTH + N_HEADS], pool_w)


def _rotary_tables(positions):
    inv_freq = ROPE_THETA ** (-jnp.arange(0, ROT_DIM, 2, dtype=F32) / ROT_DIM)
    half = ROT_DIM // 2
    head_freq = jnp.concatenate([inv_freq, inv_freq, jnp.zeros((HEAD_DIM - ROT_DIM,), F32)])
    lane_freq = jnp.concatenate([head_freq, head_freq])
    ang = positions.reshape(-1).astype(F32)[:, None] * lane_freq[None, :]
    cos, sin = jnp.cos(ang), jnp.sin(ang)
    in_head = lax.broadcasted_iota(jnp.int32, (1, LANES), 1) % HEAD_DIM
    return cos, jnp.where(in_head < half, -sin, 0.0), jnp.where((in_head >= half) & (in_head < ROT_DIM), sin, 0.0)


def kernel(x, c, positions, ada_w, ada_b, w_in, b_in, sinks, pool_w, pool_scale, w_out, w_gate, w_up, w_down, g_pre_mix, g_post_mix, g_pre_ffn, g_post_ffn, loss_target, m_ada_w, m_ada_b, m_w_in, m_b_in, m_sinks, m_pool_w, m_pool_scale, m_w_out, m_w_gate, m_w_up, m_w_down, m_g_pre_mix, m_g_post_mix, m_g_pre_ffn, m_g_post_ffn, v_ada_w, v_ada_b, v_w_in, v_b_in, v_sinks, v_pool_w, v_pool_scale, v_w_out, v_w_gate, v_w_up, v_w_down, v_g_pre_mix, v_g_post_mix, v_g_pre_ffn, v_g_post_ffn):
    me = _index(_mesh_place())
    x0 = x[0]
    target = loss_target[0]
    rot = _rotary_tables(positions)
    ada_cols = ada_w.shape[2]

    c_all = _allgather_vmem(jnp.broadcast_to(c, (8, D_MODEL)), "gather_c")[:, 0, :]
    mod_part, c_act = _mod_fwd(c_all, ada_w)
    mod_all = _allgather_vmem(mod_part.reshape(DEPTH * N_DEV, ada_cols), "gather_mod")
    mod_mine = lax.dynamic_slice_in_dim(mod_all.reshape(N_DEV, DEPTH, N_DEV, ada_cols), me, 1, axis=2)[:, :, 0, :]
    mod = jnp.transpose(mod_mine, (1, 0, 2)).reshape(DEPTH, N_MOD * D_MODEL) + ada_b
    mod = mod.reshape(DEPTH, N_MOD, D_MODEL)

    shards = []
    for l in range(DEPTH):
        shards += [w_in[l].T.astype(BF16), w_out[l].astype(BF16), w_gate[l].T.astype(BF16), w_up[l].T.astype(BF16),
                   w_down[l].astype(BF16)]
    whole = lambda gathered: [g.reshape(-1, D_MODEL) for g in gathered]
    full = [None] * (5 * DEPTH)
    full[0:2] = whole(_exchange_alone(_Gather(shards[0:2]), "gather_w_in_out_0"))
    hosted = {("fwd_in", 0): (_Gather(shards[4:5]), [4]), ("attn_fwd", 0): (_Gather(shards[2:4]), [2, 3]),
              ("ffn_fwd", 0): (_Gather(shards[5:10]), [5, 6, 7, 8, 9])}

    def take(kind, l, ex_outs):
        if (kind, l) in hosted:
            for slot, g in zip(hosted[kind, l][1], whole(ex_outs[0])):
                full[slot] = g

    def beside(kind, l):
        return [hosted[kind, l][0]] if (kind, l) in hosted else []

    pool_w_b = pool_w.astype(BF16)

    saved = []
    xl = x0
    for l in range(DEPTH):
        vec_in = _rows(g_pre_mix[l], 1.0 + mod[l, 1], mod[l, 0], width=D_MODEL)
        vec_out = _rows(mod[l, 2], g_post_mix[l], width=D_MODEL)
        vec_ffn = _rows(g_pre_ffn[l], 1.0 + mod[l, 4], mod[l, 3], mod[l, 5], g_post_ffn[l], width=D_MODEL)
        psc = _rows(pool_scale[l], width=POOL_WIDTH)
        (q, k, v, u), ex = _fwd_in(xl, vec_in, full[5 * l], _rows(b_in[l], width=IN_WIDTH), rot, f"fwd_in_{l}",
                                   beside("fwd_in", l))
        take("fwd_in", l, ex)
        (ao, lse), ex = _attn_fwd_t(sinks[l], q, k, v, f"attn_fwd_{l}", beside("attn_fwd", l))
        take("attn_fwd", l, ex)
        x1, mix = _fwd_out(ao, u, pool_w_b[l], psc, full[5 * l + 1], xl, vec_out, f"fwd_out_{l}")
        (x2, f, gt, up, h2, *loss_part), ex = _ffn_fwd(
            x1, vec_ffn, full[5 * l + 2], full[5 * l + 3], full[5 * l + 4], f"ffn_fwd_{l}", beside("ffn_fwd", l),
            target=target if l == DEPTH - 1 else None)
        take("ffn_fwd", l, ex)
        saved.append((xl, q, k, v, u, ao, lse, x1, mix, f, gt, up, h2, vec_in, vec_out, vec_ffn, psc))
        xl = x2

    dx = xl
    loss = lax.psum(loss_part[0][0, 0], ("x", "y", "c"))

    stacks = [None] * (5 * DEPTH)
    others = [None] * (5 * DEPTH)
    small_all = [None] * DEPTH
    ready_stacks, ready_small = [], []

    def leaving():
        exs, notes = [], []
        if ready_stacks:
            exs.append(_Scatter([stacks[a] for a in ready_stacks]))
            notes.append(("stacks", list(ready_stacks)))
            ready_stacks.clear()
        if ready_small:
            exs.append(_Gather([ready_small[0][1]]))
            notes.append(("small", ready_small[0][0]))
            ready_small.clear()
        return exs, notes

    def arrived(notes, ex_outs):
        for (kind, what), outs in zip(notes, ex_outs):
            if kind == "stacks":
                for a, o in zip(what, outs):
                    others[a] = o
            else:
                small_all[what] = outs[0]

    def grad_ready(a, grad):
        stacks[a] = grad.reshape(N_DEV, -1, D_MODEL)
        ready_stacks.append(a)

    def hosting(kernel_fn, *args):
        exs, notes = leaving()
        outs, ex_outs = kernel_fn(*args, exs)
        arrived(notes, ex_outs)
        return outs

    for l in reversed(range(DEPTH)):
        w_in_t, w_out_f, w_gate_t, w_up_t, w_down_f = full[5 * l:5 * l + 5]
        xin, q, k, v, u, ao, lse, x1, mix, f, gt, up, h2, vec_in, vec_out, vec_ffn, psc = saved[l]
        dgt, dup, d_w_down, cs_a = hosting(_ffn_bwd_act, dx, f, gt, up, vec_ffn, w_down_f, f"ffn_bwd_act_{l}")
        grad_ready(5 * l + 4, d_w_down.astype(BF16))
        dx1, cs_f = hosting(_ffn_bwd_in, dx, x1, dgt, dup, vec_ffn, w_gate_t, w_up_t, f"ffn_bwd_in_{l}")
        grad_ready(5 * l + 2, hosting(_weight_grad, dgt, h2, f"grad_w_gate_{l}"))
        grad_ready(5 * l + 3, hosting(_weight_grad, dup, h2, f"grad_w_up_{l}"))
        do, du, cat, dmix, cs_o, dpw, dpsc = hosting(_out_bwd, dx1, mix, ao, u, pool_w_b[l], psc, w_out_f, vec_out,
                                                     f"out_bwd_{l}")
        grad_ready(5 * l + 1, hosting(_weight_grad, cat, dmix, f"grad_w_out_{l}"))
        dq, dkc, dkp, dvc, dvp, dsk = hosting(_attn_bwd_t, sinks[l], q, do, lse, k, v, f"attn_bwd_{l}")
        dx, dproj, h, cs_i, db = _in_bwd(dq, dkc, dkp, dvc, dvp, du, xin, dx1, vec_in, w_in_t, rot, f"in_bwd_{l}")
        d_mod = jnp.concatenate([cs_i[0], cs_i[1], cs_o[0], cs_f[0], cs_f[1], cs_a[0]])
        d_gain = jnp.stack([cs_i[2], cs_o[1], cs_f[2], cs_a[1]])
        ready_small.append((l, _pack_layer(d_mod, d_gain, db[0], dpsc[0], dsk[:, 0], dpw)))
        grad_ready(5 * l, hosting(_weight_grad, dproj, h, f"grad_w_in_{l}"))
    grad_x = dx[None]
    exs, notes = leaving()
    arrived(notes, [_exchange_alone(ex, "scatter_last") for ex in exs])

    layer_sums = [_unpack_layer(_sum_parts(small_all[l][0], small_all[l][1:], f"sum_small_{l}")) for l in range(DEPTH)]
    g_ada_b, g_gains, g_b_in, g_pool_scale, g_sinks, g_pool_w = (
        jnp.stack([layer_sums[l][j] for l in range(DEPTH)], axis=1 if j == 1 else 0) for j in range(6))
    small_sum = _pack_small(g_ada_b, g_gains, g_b_in, g_pool_scale, g_sinks, g_pool_w)
    gains = jnp.stack([g_pre_mix, g_post_mix, g_pre_ffn, g_post_ffn])
    m_gains = jnp.stack([m_g_pre_mix, m_g_post_mix, m_g_pre_ffn, m_g_post_ffn])
    v_gains = jnp.stack([v_g_pre_mix, v_g_post_mix, v_g_pre_ffn, v_g_post_ffn])
    small_step = _adamw(_pack_small(ada_b, gains, b_in, pool_scale, sinks, pool_w), small_sum,
                        _pack_small(m_ada_b, m_gains, m_b_in, m_pool_scale, m_sinks, m_pool_w),
                        _pack_small(v_ada_b, v_gains, v_b_in, v_pool_scale, v_sinks, v_pool_w), "adamw_small")
    small_out = [_unpack_small(p) for p in small_step]

    dmod_all = jnp.stack([small_all[l][:, 0:N_MOD, :].reshape(N_DEV, N_DEV, ada_cols) for l in range(DEPTH)])
    dmod_cols = lax.dynamic_slice_in_dim(dmod_all, me, 1, axis=2)[:, :, 0, :]
    g_ada_w = _ada_grad(c_act, dmod_cols)

    summed = []
    for a, (stack, oth) in enumerate(zip(stacks, others)):
        own = lax.dynamic_index_in_dim(stack, me, axis=0, keepdims=False)
        summed.append(_sum_parts(own, oth, f"sum_grads_{a}"))
    per_kind = [jnp.stack([summed[5 * l + j] for l in range(DEPTH)]) for j in range(5)]
    g_w_in, g_w_out, g_w_gate, g_w_up, g_w_down = per_kind
    g_w_in, g_w_gate, g_w_up = (jnp.transpose(g, (0, 2, 1)) for g in (g_w_in, g_w_gate, g_w_up))

    def step(w, g, m, v, name):
        flat = lambda a: a.reshape(-1, a.shape[-1])
        return [o.reshape(w.shape) for o in _adamw(flat(w), flat(g), flat(m), flat(v), name)]

    big = {
        "ada_w": (g_ada_w, step(ada_w, g_ada_w, m_ada_w, v_ada_w, "adamw_ada_w")),
        "w_in": (g_w_in, step(w_in, g_w_in, m_w_in, v_w_in, "adamw_w_in")),
        "w_out": (g_w_out, step(w_out, g_w_out, m_w_out, v_w_out, "adamw_w_out")),
        "w_gate": (g_w_gate, step(w_gate, g_w_gate, m_w_gate, v_w_gate, "adamw_w_gate")),
        "w_up": (g_w_up, step(w_up, g_w_up, m_w_up, v_w_up, "adamw_w_up")),
        "w_down": (g_w_down, step(w_down, g_w_down, m_w_down, v_w_down, "adamw_w_down")),
    }

    def ordered(pick_big, pick_small):
        ada_b_, gains_, b_in_, pool_scale_, sinks_, pool_w_ = pick_small
        return [pick_big("ada_w"), ada_b_, pick_big("w_in"), b_in_, sinks_, pool_w_, pool_scale_, pick_big("w_out"),
                pick_big("w_gate"), pick_big("w_up"), pick_big("w_down"), gains_[0], gains_[1], gains_[2], gains_[3]]

    grads = ordered(lambda n: big[n][0], (g_ada_b, g_gains, g_b_in, g_pool_scale, g_sinks, g_pool_w))
    deltas = ordered(lambda n: big[n][1][0], small_out[0])
    new_m = ordered(lambda n: big[n][1][1], small_out[1])
    new_v = ordered(lambda n: big[n][1][2], small_out[2])
    return (loss, grad_x, *grads, *deltas, *new_m, *new_v)
```

```python
import jax
import jax.numpy as jnp
from jax import lax
from jax.experimental import pallas as pl
from jax.experimental.pallas import tpu as pltpu

F32 = jnp.float32
BF16 = jnp.bfloat16

N_DEV = 8
DEPTH = 2
D_MODEL = 1024
HEAD_DIM = 64
N_HEADS = 8
N_KV_HEADS = 2
GROUP = N_HEADS // N_KV_HEADS
ATTN_WIDTH = N_HEADS * HEAD_DIM
KV_WIDTH = N_KV_HEADS * HEAD_DIM
POOL_WIDTH = 512
POOL_WINDOWS = (2, 4, 8, 16)
POOL_GROUP_WIDTH = 128
POOL_HALO = 16
IN_WIDTH = ATTN_WIDTH + 2 * KV_WIDTH + POOL_WIDTH
D_FF = 2816
N_MOD = 6
BLOCK = 128
ROT_DIM = 16
ROPE_THETA = 500000.0
EPS = 1e-6
NEG_INF = -1e30
Q_SCALE = HEAD_DIM ** -0.5

ADAM_LR = 0.001
ADAM_B1 = 0.9
ADAM_B2 = 0.999
ADAM_EPS = 1e-08
ADAM_WD = 0.01
ADAM_STEP = 10

LANES = 128
SEQ_TILE = 512
FF_CHUNK = 256
VMEM_LIMIT = 56 * 1024 * 1024
MESH = pl.DeviceIdType.MESH

NT = (((1,), (1,)), ((), ()))
TN = (((0,), (0,)), ((), ()))


def _dot(a, b, dims=None):
    if dims is None:
        return jnp.dot(a, b, preferred_element_type=F32)
    return lax.dot_general(a, b, dims, preferred_element_type=F32)


def _cparams(n_axes):
    return pltpu.CompilerParams(dimension_semantics=("arbitrary",) * n_axes, vmem_limit_bytes=VMEM_LIMIT)


def _resident(shape):
    zeros = (0,) * len(shape)
    return pl.BlockSpec(shape, lambda *_: zeros, pipeline_mode=pl.Buffered(1))


def _rows(*vectors, width):
    rows = [jnp.reshape(v, (1, width)).astype(F32) for v in vectors]
    rows.append(jnp.zeros((8 - len(rows), width), F32))
    return jnp.concatenate(rows, axis=0)


def _rsqrt_ms(x):
    return lax.rsqrt(jnp.mean(x * x, axis=-1, keepdims=True) + EPS)


def _colsum(x):
    return jnp.sum(x, axis=0, keepdims=True)


def _seq_tile(s):
    return min(s, SEQ_TILE)


def _mesh_place():
    x, y, c = lax.axis_index("x"), lax.axis_index("y"), lax.axis_index("c")
    return x, y, c


def _flip(place, k):
    x, y, c = place
    return (1 - x if k & 4 else x, 1 - y if k & 2 else y, 1 - c if k & 1 else c)


def _index(place):
    x, y, c = place
    return 4 * x + 2 * y + c


def _allgather_vmem(block, name):
    r, c = block.shape

    def body(x_ref, out_ref, send_sems, recv_sems, local_sem):
        me = _mesh_place()
        mine = pltpu.make_async_copy(x_ref, out_ref.at[_index(me)], local_sem)
        mine.start()

        def copy(k):
            return pltpu.make_async_remote_copy(
                src_ref=x_ref, dst_ref=out_ref.at[_index(me)], send_sem=send_sems.at[k - 1], recv_sem=recv_sems.at[k - 1],
                device_id=_flip(me, k), device_id_type=MESH)

        def arrival(k):
            return pltpu.make_async_remote_copy(
                src_ref=x_ref, dst_ref=out_ref.at[_index(_flip(me, k))], send_sem=send_sems.at[k - 1],
                recv_sem=recv_sems.at[k - 1], device_id=_flip(me, k), device_id_type=MESH)

        for k in range(1, N_DEV):
            copy(k).start()
        for k in range(1, N_DEV):
            arrival(k).wait_recv()
        for k in range(1, N_DEV):
            copy(k).wait_send()
        mine.wait()

    return pl.pallas_call(
        body, name=name,
        out_shape=jax.ShapeDtypeStruct((N_DEV, r, c), block.dtype),
        in_specs=[pl.BlockSpec(memory_space=pltpu.VMEM)],
        out_specs=pl.BlockSpec(memory_space=pltpu.VMEM),
        scratch_shapes=[pltpu.SemaphoreType.DMA((N_DEV - 1,)), pltpu.SemaphoreType.DMA((N_DEV - 1,)),
                        pltpu.SemaphoreType.DMA],
    )(block)


class _Gather:
    def __init__(self, shards):
        n = len(shards)
        self.operands = list(shards)
        self.out_shape = [jax.ShapeDtypeStruct((N_DEV,) + s.shape, s.dtype) for s in shards]
        self.scratch = [pltpu.SemaphoreType.DMA((N_DEV - 1, n)), pltpu.SemaphoreType.DMA((N_DEV - 1, n)),
                        pltpu.SemaphoreType.DMA((n,))]

    def _copies(self, x_refs, out_refs, sems):
        send_sems, recv_sems, local_sems = sems
        n = len(x_refs)
        x, y, c = _mesh_place()
        me, sibling = (x, y, c), (x, y, 1 - c)
        chips = [(1 - x, y), (x, 1 - y), (1 - x, 1 - y)]

        def copy(k, a, block, to, from_input=False):
            rows = out_refs[a].at[_index(block)]
            return pltpu.make_async_remote_copy(
                src_ref=x_refs[a] if from_input else rows, dst_ref=rows,
                send_sem=send_sems.at[k, a], recv_sem=recv_sems.at[k, a], device_id=to, device_id_type=MESH)

        mine = [pltpu.make_async_copy(x_refs[a], out_refs[a].at[_index(me)], local_sems.at[a]) for a in range(n)]
        first = [copy(0, a, me, sibling, from_input=True) for a in range(n)]
        first += [copy(1 + j, a, me, (*chip, c), from_input=True) for j, chip in enumerate(chips) for a in range(n)]
        over_ici = [copy(1 + j, a, (*chip, c), me) for j, chip in enumerate(chips) for a in range(n)]
        passed = [copy(4 + j, a, (*chip, c), sibling) for j, chip in enumerate(chips) for a in range(n)]
        from_sibling = [copy(0, a, sibling, me) for a in range(n)]
        from_sibling += [copy(4 + j, a, (*chip, 1 - c), me) for j, chip in enumerate(chips) for a in range(n)]
        return mine, first, over_ici, passed, from_sibling

    def begin(self, x_refs, out_refs, sems):
        mine, first, _, _, _ = self._copies(x_refs, out_refs, sems)
        for cp in mine + first:
            cp.start()

    def middle(self, x_refs, out_refs, sems):
        _, _, over_ici, passed, _ = self._copies(x_refs, out_refs, sems)
        for arrived, onward in zip(over_ici, passed):
            arrived.wait_recv()
            onward.start()

    def end(self, x_refs, out_refs, sems):
        mine, first, _, passed, from_sibling = self._copies(x_refs, out_refs, sems)
        for cp in from_sibling:
            cp.wait_recv()
        for cp in first + passed:
            cp.wait_send()
        for cp in mine:
            cp.wait()


class _Scatter:
    def __init__(self, stacks):
        n = len(stacks)
        self.operands = list(stacks)
        self.out_shape = [jax.ShapeDtypeStruct((N_DEV - 1,) + s.shape[1:], s.dtype) for s in stacks]
        self.scratch = [pltpu.SemaphoreType.DMA((N_DEV - 1, n)), pltpu.SemaphoreType.DMA((N_DEV - 1, n))]

    def _copies(self, g_refs, out_refs, sems):
        send_sems, recv_sems = sems
        me = _mesh_place()

        def copy(k, a):
            peer = _flip(me, k)
            return pltpu.make_async_remote_copy(
                src_ref=g_refs[a].at[_index(peer)], dst_ref=out_refs[a].at[k - 1],
                send_sem=send_sems.at[k - 1, a], recv_sem=recv_sems.at[k - 1, a], device_id=peer, device_id_type=MESH)

        return [copy(k, a) for k in range(1, N_DEV) for a in range(len(g_refs))]

    def begin(self, g_refs, out_refs, sems):
        for cp in self._copies(g_refs, out_refs, sems):
            cp.start()

    def middle(self, g_refs, out_refs, sems):
        pass

    def end(self, g_refs, out_refs, sems):
        copies = self._copies(g_refs, out_refs, sems)
        for cp in copies:
            cp.wait_recv()
        for cp in copies:
            cp.wait_send()


def _call(body, args, *, name, grid, in_specs, out_specs, out_shape, scratch_shapes=(), exchanges=()):
    if not exchanges:
        outs = pl.pallas_call(body, name=name, grid=grid, in_specs=in_specs, out_specs=out_specs, out_shape=out_shape,
                              scratch_shapes=list(scratch_shapes), compiler_params=_cparams(len(grid)))(*args)
        return outs, []
    (steps,) = grid
    n_in, n_out, n_scr = len(in_specs), len(out_specs), len(scratch_shapes)
    ex_in = [len(ex.operands) for ex in exchanges]
    ex_out = [len(ex.out_shape) for ex in exchanges]
    ex_scr = [len(ex.scratch) for ex in exchanges]

    def split(refs, counts):
        parts, pos = [], 0
        for cnt in counts:
            parts.append(refs[pos:pos + cnt])
            pos += cnt
        return parts

    def wrapped(*refs):
        ins, xin, outs, xout, scr, xscr = split(refs, [n_in, sum(ex_in), n_out, sum(ex_out), n_scr, sum(ex_scr)])
        bound = list(zip(exchanges, split(xin, ex_in), split(xout, ex_out), split(xscr, ex_scr)))
        step = pl.program_id(0)

        def phase(method, at):
            @pl.when(step == at)
            def _():
                for ex, i_refs, o_refs, sems in bound:
                    getattr(ex, method)(i_refs, o_refs, sems)

        phase("begin", 0)
        phase("middle", (3 * steps) // 4)
        body(*ins, *outs, *scr)
        phase("end", steps - 1)

    any_spec = pl.BlockSpec(memory_space=pl.ANY)
    results = pl.pallas_call(
        wrapped, name=name, grid=grid,
        in_specs=list(in_specs) + [any_spec] * sum(ex_in),
        out_specs=list(out_specs) + [any_spec] * sum(ex_out),
        out_shape=list(out_shape) + [s for ex in exchanges for s in ex.out_shape],
        scratch_shapes=list(scratch_shapes) + [s for ex in exchanges for s in ex.scratch],
        compiler_params=_cparams(1),
    )(*args, *[a for ex in exchanges for a in ex.operands])
    return results[:n_out], split(results[n_out:], ex_out)


def _exchanges_alone(exchanges, name):
    def body(flag_ref):
        flag_ref[...] = jnp.zeros_like(flag_ref)

    _, ex_outs = _call(body, [], name=name, grid=(1,), in_specs=[], out_specs=[pl.BlockSpec((8, LANES), lambda i: (0, 0))],
                       out_shape=[jax.ShapeDtypeStruct((8, LANES), F32)], exchanges=exchanges)
    return ex_outs


def _mod_fwd(c_all, ada_w):
    cols = ada_w.shape[2]

    def body(c_ref, w_ref, mp_ref, act_ref):
        c = c_ref[...]
        act = c * jax.nn.sigmoid(c)
        act_ref[...] = act
        mp_ref[0] = _dot(act.astype(BF16), w_ref[0].astype(BF16))

    return pl.pallas_call(
        body, name="mod_fwd", grid=(DEPTH,),
        in_specs=[pl.BlockSpec((N_DEV, D_MODEL), lambda l: (0, 0)),
                  pl.BlockSpec((1, D_MODEL, cols), lambda l: (l, 0, 0))],
        out_specs=[pl.BlockSpec((1, N_DEV, cols), lambda l: (l, 0, 0)),
                   pl.BlockSpec((N_DEV, D_MODEL), lambda l: (0, 0))],
        out_shape=[jax.ShapeDtypeStruct((DEPTH, N_DEV, cols), F32), jax.ShapeDtypeStruct((N_DEV, D_MODEL), F32)],
        compiler_params=_cparams(1),
    )(c_all, ada_w)


def _ada_grad(c_act, dmod_cols):
    cols = dmod_cols.shape[2]

    def body(act_ref, dm_ref, g_ref):
        g_ref[0] = _dot(act_ref[...].astype(BF16), dm_ref[0].astype(BF16), TN)

    return pl.pallas_call(
        body, name="ada_grad", grid=(DEPTH,),
        in_specs=[pl.BlockSpec((N_DEV, D_MODEL), lambda l: (0, 0)),
                  pl.BlockSpec((1, N_DEV, cols), lambda l: (l, 0, 0))],
        out_specs=pl.BlockSpec((1, D_MODEL, cols), lambda l: (l, 0, 0)),
        out_shape=jax.ShapeDtypeStruct((DEPTH, D_MODEL, cols), F32),
        compiler_params=_cparams(1),
    )(c_act, dmod_cols)


def _rotate(t, cos, sin_lo, sin_hi):
    return t * cos + pltpu.roll(t, LANES - 8, 1) * sin_lo + pltpu.roll(t, 8, 1) * sin_hi


def _rotate_bwd(t, cos, sin_lo, sin_hi):
    return t * cos + pltpu.roll(t * sin_lo, 8, 1) + pltpu.roll(t * sin_hi, LANES - 8, 1)


def _fwd_in(x, vec, w_in_t, b_in, rot, name, exchanges=()):
    s = x.shape[0]
    ts = _seq_tile(s)

    def body(x_ref, vec_ref, w_ref, b_ref, cos_ref, lo_ref, hi_ref, q_ref, k_ref, v_ref, u_ref):
        xt = x_ref[...]
        h = xt * _rsqrt_ms(xt) * vec_ref[0:1, :] * vec_ref[1:2, :] + vec_ref[2:3, :]
        proj = _dot(h.astype(BF16), w_ref[...], NT) + b_ref[0:1, :]
        cos, lo, hi = cos_ref[...], lo_ref[...], hi_ref[...]
        for j in range(ATTN_WIDTH // LANES):
            q_ref[:, j * LANES:(j + 1) * LANES] = (
                _rotate(proj[:, j * LANES:(j + 1) * LANES], cos, lo, hi) * Q_SCALE).astype(BF16)
        k_ref[...] = _rotate(proj[:, ATTN_WIDTH:ATTN_WIDTH + KV_WIDTH], cos, lo, hi).astype(BF16)
        v_ref[...] = proj[:, ATTN_WIDTH + KV_WIDTH:ATTN_WIDTH + 2 * KV_WIDTH].astype(BF16)
        u_ref[...] = proj[:, ATTN_WIDTH + 2 * KV_WIDTH:]

    row = lambda w: pl.BlockSpec((ts, w), lambda i: (i, 0))
    return _call(
        body, [x, vec, w_in_t, b_in, *rot], name=name, grid=(s // ts,),
        in_specs=[row(D_MODEL), _resident((8, D_MODEL)), _resident((IN_WIDTH, D_MODEL)), _resident((8, IN_WIDTH)),
                  row(LANES), row(LANES), row(LANES)],
        out_specs=[row(ATTN_WIDTH), row(KV_WIDTH), row(KV_WIDTH), row(POOL_WIDTH)],
        out_shape=[jax.ShapeDtypeStruct((s, ATTN_WIDTH), BF16), jax.ShapeDtypeStruct((s, KV_WIDTH), BF16),
                   jax.ShapeDtypeStruct((s, KV_WIDTH), BF16), jax.ShapeDtypeStruct((s, POOL_WIDTH), F32)],
        exchanges=exchanges)


def _band_mask_t(first_block):
    key = lax.broadcasted_iota(jnp.int32, (2 * BLOCK, 2 * BLOCK), 0)
    qry = lax.broadcasted_iota(jnp.int32, (2 * BLOCK, 2 * BLOCK), 1) & (BLOCK - 1)
    d = key - qry
    return (d >= 1) & (d <= BLOCK) & ((key >= BLOCK) | jnp.logical_not(first_block))


def _placed(both, kh, b):
    lane = lax.broadcasted_iota(jnp.int32, (1, LANES), 1)
    src = both if kh == b else pltpu.roll(both, HEAD_DIM, 1)
    return jnp.where((lane >= b * HEAD_DIM) & (lane < (b + 1) * HEAD_DIM), src, jnp.zeros_like(src))


def _pair(ref, kh, r):
    j = 2 * kh + r
    return ref[:, j * LANES:(j + 1) * LANES]


def _sink_row(sink_ref, kh, b):
    lane = lax.broadcasted_iota(jnp.int32, (1, 2 * BLOCK), 1)
    return jnp.where(lane < BLOCK, sink_ref[GROUP * kh + b], sink_ref[GROUP * kh + 2 + b])


def _attn_fwd_t(sinks, q, k, v, name, exchanges=()):
    s = q.shape[0]
    nb = s // BLOCK
    cur = lambda w: pl.BlockSpec((BLOCK, w), lambda i: (i, 0))
    prev = lambda w: pl.BlockSpec((BLOCK, w), lambda i: (jnp.maximum(i - 1, 0), 0))

    def body(sink_ref, q_ref, kc_ref, kp_ref, vc_ref, vp_ref, o_ref, l_ref):
        valid = _band_mask_t(pl.program_id(0) == 0)
        k_both = jnp.concatenate([kp_ref[...], kc_ref[...]], axis=0)
        v_both = jnp.concatenate([vp_ref[...], vc_ref[...]], axis=0)
        for kh in range(N_KV_HEADS):
            qg = jnp.concatenate([_pair(q_ref, kh, 0), _pair(q_ref, kh, 1)], axis=0)
            keys = jnp.concatenate([_placed(k_both, kh, 0), _placed(k_both, kh, 1)], axis=0)
            st = _dot(keys, qg, NT)
            out = jnp.zeros((2 * BLOCK, LANES), F32)
            for b in range(2):
                sc = jnp.where(valid, st[b * 2 * BLOCK:(b + 1) * 2 * BLOCK, :], NEG_INF)
                sink = _sink_row(sink_ref, kh, b)
                m = jnp.maximum(jnp.max(sc, axis=0, keepdims=True), sink)
                p = jnp.exp(sc - m)
                den = jnp.sum(p, axis=0, keepdims=True) + jnp.exp(sink - m)
                out = out + _dot((p * (1.0 / den)).astype(BF16), _placed(v_both, kh, b), TN)
                lse = m + jnp.log(den)
                for r in range(2):
                    h = GROUP * kh + 2 * r + b
                    l_ref[h:h + 1, :] = lse[:, r * BLOCK:(r + 1) * BLOCK]
            for r in range(2):
                j = 2 * kh + r
                o_ref[:, j * LANES:(j + 1) * LANES] = out[r * BLOCK:(r + 1) * BLOCK, :].astype(BF16)

    return _call(
        body, [sinks, q, k, k, v, v], name=name, grid=(nb,),
        in_specs=[pl.BlockSpec(memory_space=pltpu.SMEM), cur(ATTN_WIDTH), cur(KV_WIDTH), prev(KV_WIDTH),
                  cur(KV_WIDTH), prev(KV_WIDTH)],
        out_specs=[cur(ATTN_WIDTH), pl.BlockSpec((N_HEADS, BLOCK), lambda i: (0, i))],
        out_shape=[jax.ShapeDtypeStruct((s, ATTN_WIDTH), BF16), jax.ShapeDtypeStruct((N_HEADS, s), F32)],
        exchanges=exchanges)


def _attn_bwd_t(sinks, q, do, lse, k, v, name, exchanges=()):
    s = q.shape[0]
    nb = s // BLOCK
    cur = lambda w: pl.BlockSpec((BLOCK, w), lambda i: (i, 0))
    prev = lambda w: pl.BlockSpec((BLOCK, w), lambda i: (jnp.maximum(i - 1, 0), 0))

    def body(sink_ref, q_ref, do_ref, l_ref, kc_ref, kp_ref, vc_ref, vp_ref,
             dq_ref, dkc_ref, dkp_ref, dvc_ref, dvp_ref, dsink_ref):
        @pl.when(pl.program_id(0) == 0)
        def _():
            dsink_ref[...] = jnp.zeros_like(dsink_ref)

        valid = _band_mask_t(pl.program_id(0) == 0)
        k_both = jnp.concatenate([kp_ref[...], kc_ref[...]], axis=0)
        v_both = jnp.concatenate([vp_ref[...], vc_ref[...]], axis=0)
        lane = lax.broadcasted_iota(jnp.int32, (1, LANES), 1)
        col = lax.broadcasted_iota(jnp.int32, (1, 2 * BLOCK), 1)
        dk_heads, dv_heads = [], []
        for kh in range(N_KV_HEADS):
            qg = jnp.concatenate([_pair(q_ref, kh, 0), _pair(q_ref, kh, 1)], axis=0)
            dog = jnp.concatenate([_pair(do_ref, kh, 0), _pair(do_ref, kh, 1)], axis=0)
            k_placed = [_placed(k_both, kh, b) for b in range(2)]
            v_placed = [_placed(v_both, kh, b) for b in range(2)]
            st = _dot(jnp.concatenate(k_placed, axis=0), qg, NT)
            dpt = _dot(jnp.concatenate(v_placed, axis=0), dog, NT)
            dqg = jnp.zeros((2 * BLOCK, LANES), F32)
            dk_b, dv_b = [], []
            for b in range(2):
                rows = slice(b * 2 * BLOCK, (b + 1) * 2 * BLOCK)
                heads = [GROUP * kh + 2 * r + b for r in range(2)]
                lse_row = jnp.concatenate([l_ref[h:h + 1, :] for h in heads], axis=1)
                p = jnp.where(valid, jnp.exp(st[rows, :] - lse_row), 0.0)
                dp = dpt[rows, :]
                delta = jnp.sum(p * dp, axis=0, keepdims=True)
                sink_pull = jnp.exp(_sink_row(sink_ref, kh, b) - lse_row) * delta
                dsink_ref[heads[0]:heads[0] + 1, :] += -jnp.sum(jnp.where(col < BLOCK, sink_pull, 0.0))
                dsink_ref[heads[1]:heads[1] + 1, :] += -jnp.sum(jnp.where(col < BLOCK, 0.0, sink_pull))
                ds = (p * (dp - delta)).astype(BF16)
                dqg = dqg + _dot(ds, k_placed[b], TN)
                dk_b.append(_dot(ds, qg))
                dv_b.append(_dot(p.astype(BF16), dog))
            for parts, total in ((dk_b, dk_heads), (dv_b, dv_heads)):
                kept = jnp.where(lane < HEAD_DIM, parts[0], parts[1])
                total.append(kept + pltpu.roll(kept, HEAD_DIM, 1))
            for r in range(2):
                j = 2 * kh + r
                dq_ref[:, j * LANES:(j + 1) * LANES] = dqg[r * BLOCK:(r + 1) * BLOCK, :] * Q_SCALE
        dk = jnp.where(lane < HEAD_DIM, dk_heads[0], dk_heads[1])
        dv = jnp.where(lane < HEAD_DIM, dv_heads[0], dv_heads[1])
        dkp_ref[...] = dk[0:BLOCK, :]
        dkc_ref[...] = dk[BLOCK:, :]
        dvp_ref[...] = dv[0:BLOCK, :]
        dvc_ref[...] = dv[BLOCK:, :]

    kv = jax.ShapeDtypeStruct((s, KV_WIDTH), F32)
    return _call(
        body, [sinks, q, do, lse, k, k, v, v], name=name, grid=(nb,),
        in_specs=[pl.BlockSpec(memory_space=pltpu.SMEM), cur(ATTN_WIDTH), cur(ATTN_WIDTH),
                  pl.BlockSpec((N_HEADS, BLOCK), lambda i: (0, i)),
                  cur(KV_WIDTH), prev(KV_WIDTH), cur(KV_WIDTH), prev(KV_WIDTH)],
        out_specs=[cur(ATTN_WIDTH), cur(KV_WIDTH), cur(KV_WIDTH), cur(KV_WIDTH), cur(KV_WIDTH),
                   pl.BlockSpec((8, LANES), lambda i: (0, 0))],
        out_shape=[jax.ShapeDtypeStruct((s, ATTN_WIDTH), F32), kv, kv, kv, kv, jax.ShapeDtypeStruct((8, LANES), F32)],
        exchanges=exchanges)


def _pool_counts(tile, ts):
    t = (tile * ts + lax.broadcasted_iota(jnp.int32, (ts, 1), 0) + 1).astype(F32)
    return [jnp.minimum(t, float(w)) for w in POOL_WINDOWS]


def _pooled(u, halo, counts, gi):
    cols = slice(gi * POOL_GROUP_WIDTH, (gi + 1) * POOL_GROUP_WIDTH)
    acc = jnp.concatenate([halo[:, cols], u[:, cols]], axis=0)
    shift = 1
    while shift < POOL_WINDOWS[gi]:
        acc = acc + pltpu.roll(acc, shift, 0)
        shift *= 2
    return acc[POOL_HALO:, :] / counts[gi] - u[:, cols]


def _fwd_out(ao, u, pool_w, psc, w_out, x, vec, name):
    s = x.shape[0]
    ts = _seq_tile(s)
    hb = ts // POOL_HALO

    def body(ao_ref, u_ref, uh_ref, pw_ref, psc_ref, w_ref, x_ref, vec_ref, x1_ref, mix_ref):
        i = pl.program_id(0)
        u_t = u_ref[...]
        halo = jnp.where(i > 0, uh_ref[...], 0.0)
        counts = _pool_counts(i, ts)
        mix = _dot(ao_ref[...], w_ref[0:ATTN_WIDTH, :])
        for gi in range(len(POOL_WINDOWS)):
            cols = slice(gi * POOL_GROUP_WIDTH, (gi + 1) * POOL_GROUP_WIDTH)
            og = _dot(_pooled(u_t, halo, counts, gi).astype(BF16), pw_ref[gi]) * psc_ref[0:1, cols]
            mix = mix + _dot(og.astype(BF16), w_ref[ATTN_WIDTH + gi * POOL_GROUP_WIDTH:
                                                    ATTN_WIDTH + (gi + 1) * POOL_GROUP_WIDTH, :])
        mix_ref[...] = mix
        x1_ref[...] = x_ref[...] + vec_ref[0:1, :] * (mix * _rsqrt_ms(mix) * vec_ref[1:2, :])

    row = lambda w: pl.BlockSpec((ts, w), lambda i: (i, 0))
    return pl.pallas_call(
        body, name=name, grid=(s // ts,),
        in_specs=[row(ATTN_WIDTH), row(POOL_WIDTH),
                  pl.BlockSpec((POOL_HALO, POOL_WIDTH), lambda i: (jnp.maximum(i * hb - 1, 0), 0)),
                  _resident(pool_w.shape), _resident((8, POOL_WIDTH)), _resident((D_MODEL, D_MODEL)),
                  row(D_MODEL), _resident((8, D_MODEL))],
        out_specs=[row(D_MODEL), row(D_MODEL)],
        out_shape=[jax.ShapeDtypeStruct((s, D_MODEL), F32), jax.ShapeDtypeStruct((s, D_MODEL), F32)],
        compiler_params=_cparams(1),
    )(ao, u, u, pool_w, psc, w_out, x, vec)


def _ffn_fwd(x1, vec, w_gate_t, w_up_t, w_down, name, exchanges=(), target=None):
    s = x1.shape[0]
    ts = _seq_tile(s)

    def body(x_ref, *refs):
        if target is None:
            vec_ref, wg_ref, wu_ref, wd_ref, x2_ref, f_ref, gt_ref, up_ref, h2_ref = refs
        else:
            t_ref, vec_ref, wg_ref, wu_ref, wd_ref, x2_ref, f_ref, gt_ref, up_ref, h2_ref, loss_ref = refs
        xt = x_ref[...]
        h2 = (xt * _rsqrt_ms(xt) * vec_ref[0:1, :] * vec_ref[1:2, :] + vec_ref[2:3, :]).astype(BF16)
        h2_ref[...] = h2
        f = jnp.zeros((ts, D_MODEL), F32)
        for c in range(D_FF // FF_CHUNK):
            cols = slice(c * FF_CHUNK, (c + 1) * FF_CHUNK)
            g = _dot(h2, wg_ref[cols, :], NT)
            up = _dot(h2, wu_ref[cols, :], NT)
            act = ((g * jax.nn.sigmoid(g)) * up).astype(BF16)
            gt_ref[:, cols] = g.astype(BF16)
            up_ref[:, cols] = up.astype(BF16)
            f = f + _dot(act, wd_ref[cols, :])
        f_ref[...] = f
        x2 = xt + vec_ref[3:4, :] * (f * _rsqrt_ms(f) * vec_ref[4:5, :])
        if target is None:
            x2_ref[...] = x2
        else:
            @pl.when(pl.program_id(0) == 0)
            def _():
                loss_ref[...] = jnp.zeros_like(loss_ref)

            diff = x2 - t_ref[...]
            x2_ref[...] = diff / D_MODEL
            loss_ref[...] += 0.5 * jnp.sum(jnp.mean(diff * diff, axis=-1, keepdims=True))

    row = lambda w: pl.BlockSpec((ts, w), lambda i: (i, 0))
    wide = jax.ShapeDtypeStruct((s, D_FF), BF16)
    with_loss = target is not None
    return _call(
        body, [x1] + ([target] if with_loss else []) + [vec, w_gate_t, w_up_t, w_down], name=name, grid=(s // ts,),
        in_specs=[row(D_MODEL)] * (2 if with_loss else 1) + [
            _resident((8, D_MODEL)), _resident((D_FF, D_MODEL)), _resident((D_FF, D_MODEL)), _resident((D_FF, D_MODEL))],
        out_specs=[row(D_MODEL), row(D_MODEL), row(D_FF), row(D_FF), row(D_MODEL)] + (
            [pl.BlockSpec((8, LANES), lambda i: (0, 0))] if with_loss else []),
        out_shape=[jax.ShapeDtypeStruct((s, D_MODEL), F32), jax.ShapeDtypeStruct((s, D_MODEL), F32), wide, wide,
                   jax.ShapeDtypeStruct((s, D_MODEL), BF16)] + (
            [jax.ShapeDtypeStruct((8, LANES), F32)] if with_loss else []),
        exchanges=exchanges)


def _norm_bwd(dy_hat, x_hat, r):
    return r * (dy_hat - x_hat * jnp.mean(dy_hat * x_hat, axis=-1, keepdims=True))


def _accumulate_rows(ref, rows):
    for j, val in enumerate(rows):
        ref[j:j + 1, :] += val


def _ffn_bwd_act(dx2, f, gt, up, vec, w_down, name, exchanges=()):
    s = dx2.shape[0]
    ts = _seq_tile(s)

    def body(dx2_ref, f_ref, gt_ref, up_ref, vec_ref, wd_ref, dgt_ref, dup_ref, dwd_ref, cs_ref):
        @pl.when(pl.program_id(0) == 0)
        def _():
            cs_ref[...] = jnp.zeros_like(cs_ref)
            dwd_ref[...] = jnp.zeros_like(dwd_ref)

        dx2_t, f_t = dx2_ref[...], f_ref[...]
        gate, g_post = vec_ref[3:4, :], vec_ref[4:5, :]
        rf = _rsqrt_ms(f_t)
        f_hat = f_t * rf
        df = _norm_bwd(dx2_t * gate * g_post, f_hat, rf).astype(BF16)
        _accumulate_rows(cs_ref, [_colsum(dx2_t * (f_hat * g_post)), _colsum(dx2_t * gate * f_hat)])
        for c in range(D_FF // FF_CHUNK):
            cols = slice(c * FF_CHUNK, (c + 1) * FF_CHUNK)
            dact = _dot(df, wd_ref[cols, :], NT)
            g, u_t = gt_ref[:, cols].astype(F32), up_ref[:, cols].astype(F32)
            sg = jax.nn.sigmoid(g)
            silu = g * sg
            dgt_ref[:, cols] = (dact * u_t * (sg * (1.0 + g * (1.0 - sg)))).astype(BF16)
            dup_ref[:, cols] = (dact * silu).astype(BF16)
            dwd_ref[cols, :] += _dot((silu * u_t).astype(BF16), df, TN)

    row = lambda w: pl.BlockSpec((ts, w), lambda i: (i, 0))
    wide = jax.ShapeDtypeStruct((s, D_FF), BF16)
    return _call(
        body, [dx2, f, gt, up, vec, w_down], name=name, grid=(s // ts,),
        in_specs=[row(D_MODEL), row(D_MODEL), row(D_FF), row(D_FF), _resident((8, D_MODEL)), _resident((D_FF, D_MODEL))],
        out_specs=[row(D_FF), row(D_FF), _resident((D_FF, D_MODEL)), pl.BlockSpec((8, D_MODEL), lambda i: (0, 0))],
        out_shape=[wide, wide, jax.ShapeDtypeStruct((D_FF, D_MODEL), F32), jax.ShapeDtypeStruct((8, D_MODEL), F32)],
        exchanges=exchanges)


def _ffn_bwd_in(dx2, x1, dgt, dup, vec, w_gate_t, w_up_t, name, exchanges=()):
    s = x1.shape[0]
    ts = _seq_tile(s)

    def body(dx2_ref, x1_ref, dgt_ref, dup_ref, vec_ref, wg_ref, wu_ref, dx1_ref, cs_ref):
        @pl.when(pl.program_id(0) == 0)
        def _():
            cs_ref[...] = jnp.zeros_like(cs_ref)

        g_pre, one_scale = vec_ref[0:1, :], vec_ref[1:2, :]
        dh2 = _dot(dgt_ref[...], wg_ref[...]) + _dot(dup_ref[...], wu_ref[...])
        x1_t = x1_ref[...]
        r1 = _rsqrt_ms(x1_t)
        x_hat = x1_t * r1
        dx1_ref[...] = dx2_ref[...] + _norm_bwd(dh2 * g_pre * one_scale, x_hat, r1)
        _accumulate_rows(cs_ref, [_colsum(dh2), _colsum(dh2 * (x_hat * g_pre)), _colsum(dh2 * one_scale * x_hat)])

    row = lambda w: pl.BlockSpec((ts, w), lambda i: (i, 0))
    return _call(
        body, [dx2, x1, dgt, dup, vec, w_gate_t, w_up_t], name=name, grid=(s // ts,),
        in_specs=[row(D_MODEL), row(D_MODEL), row(D_FF), row(D_FF), _resident((8, D_MODEL)),
                  _resident((D_FF, D_MODEL)), _resident((D_FF, D_MODEL))],
        out_specs=[row(D_MODEL), pl.BlockSpec((8, D_MODEL), lambda i: (0, 0))],
        out_shape=[jax.ShapeDtypeStruct((s, D_MODEL), F32), jax.ShapeDtypeStruct((8, D_MODEL), F32)],
        exchanges=exchanges)


def _weight_grad(a, b, name, exchanges=()):
    s, m = a.shape
    n = b.shape[1]
    tk = min(s, 2 * SEQ_TILE)
    steps = s // tk

    def body(a_ref, b_ref, o_ref, acc_ref):
        i = pl.program_id(0)

        @pl.when(i == 0)
        def _():
            acc_ref[...] = jnp.zeros_like(acc_ref)

        b_t = b_ref[...]
        for c in range(m // FF_CHUNK):
            rows = slice(c * FF_CHUNK, (c + 1) * FF_CHUNK)
            acc_ref[rows, :] += _dot(a_ref[:, rows], b_t, TN)

        @pl.when(i == steps - 1)
        def _():
            o_ref[...] = acc_ref[...].astype(BF16)

    (grad,), ex_outs = _call(
        body, [a, b], name=name, grid=(steps,),
        in_specs=[pl.BlockSpec((tk, m), lambda i: (i, 0)), pl.BlockSpec((tk, n), lambda i: (i, 0))],
        out_specs=[_resident((m, n))],
        out_shape=[jax.ShapeDtypeStruct((m, n), BF16)],
        scratch_shapes=[pltpu.VMEM((m, n), F32)],
        exchanges=exchanges)
    return grad, ex_outs


def _out_bwd(dx1, mix, ao, u, pool_w, psc, w_out, vec, name, exchanges=()):
    s = dx1.shape[0]
    ts = _seq_tile(s)
    nt = s // ts
    hb = ts // POOL_HALO
    ng = len(POOL_WINDOWS)

    def body(dx1_ref, mix_ref, ao_ref, u_ref, uh_ref, pw_ref, psc_ref, w_ref, vec_ref,
             do_ref, du_ref, dw_ref, cs_ref, dpw_ref, dpsc_ref, carry_ref):
        i = pl.program_id(0)
        tile = nt - 1 - i

        @pl.when(i == 0)
        def _():
            dw_ref[...] = jnp.zeros_like(dw_ref)
            cs_ref[...] = jnp.zeros_like(cs_ref)
            dpw_ref[...] = jnp.zeros_like(dpw_ref)
            dpsc_ref[...] = jnp.zeros_like(dpsc_ref)
            carry_ref[...] = jnp.zeros_like(carry_ref)

        dx1_t, mix_t = dx1_ref[...], mix_ref[...]
        gate, g_post = vec_ref[0:1, :], vec_ref[1:2, :]
        rm = _rsqrt_ms(mix_t)
        m_hat = mix_t * rm
        dmix = _norm_bwd(dx1_t * gate * g_post, m_hat, rm).astype(BF16)
        _accumulate_rows(cs_ref, [_colsum(dx1_t * (m_hat * g_post)), _colsum(dx1_t * gate * m_hat)])
        do_ref[...] = _dot(dmix, w_ref[0:ATTN_WIDTH, :], NT).astype(BF16)
        dw_ref[0:ATTN_WIDTH, :] += _dot(ao_ref[...], dmix, TN)

        u_t = u_ref[...]
        halo = jnp.where(tile > 0, uh_ref[...], 0.0)
        counts = _pool_counts(tile, ts)
        for gi in range(ng):
            cols = slice(gi * POOL_GROUP_WIDTH, (gi + 1) * POOL_GROUP_WIDTH)
            wide_cols = slice(ATTN_WIDTH + gi * POOL_GROUP_WIDTH, ATTN_WIDTH + (gi + 1) * POOL_GROUP_WIDTH)
            scale = psc_ref[0:1, cols]
            pooled = _pooled(u_t, halo, counts, gi).astype(BF16)
            og = _dot(pooled, pw_ref[gi])
            dw_ref[wide_cols, :] += _dot((og * scale).astype(BF16), dmix, TN)
            d_out = _dot(dmix, w_ref[wide_cols, :], NT)
            dpsc_ref[0:1, cols] += _colsum(d_out * og)
            d_og = (d_out * scale).astype(BF16)
            dpw_ref[gi] += _dot(pooled, d_og, TN)
            d_pooled = _dot(d_og, pw_ref[gi], NT)
            spread = d_pooled / counts[gi]
            acc = jnp.concatenate([spread, carry_ref[:, cols]], axis=0)
            shift = 1
            while shift < POOL_WINDOWS[gi]:
                acc = acc + pltpu.roll(acc, ts + POOL_HALO - shift, 0)
                shift *= 2
            du_ref[:, cols] = acc[0:ts, :] - d_pooled
            carry_ref[:, cols] = spread[0:POOL_HALO, :]

    row = lambda w: pl.BlockSpec((ts, w), lambda i: (nt - 1 - i, 0))
    fixed = lambda shape: pl.BlockSpec(shape, lambda i: (0,) * len(shape))
    return _call(
        body, [dx1, mix, ao, u, u, pool_w, psc, w_out, vec], name=name, grid=(nt,),
        in_specs=[row(D_MODEL), row(D_MODEL), row(ATTN_WIDTH), row(POOL_WIDTH),
                  pl.BlockSpec((POOL_HALO, POOL_WIDTH), lambda i: (jnp.maximum((nt - 1 - i) * hb - 1, 0), 0)),
                  _resident(pool_w.shape), _resident((8, POOL_WIDTH)), _resident((D_MODEL, D_MODEL)),
                  _resident((8, D_MODEL))],
        out_specs=[row(ATTN_WIDTH), row(POOL_WIDTH), _resident((D_MODEL, D_MODEL)), fixed((8, D_MODEL)),
                   fixed(pool_w.shape), fixed((8, POOL_WIDTH))],
        out_shape=[jax.ShapeDtypeStruct((s, ATTN_WIDTH), BF16), jax.ShapeDtypeStruct((s, POOL_WIDTH), F32),
                   jax.ShapeDtypeStruct((D_MODEL, D_MODEL), F32),
                   jax.ShapeDtypeStruct((8, D_MODEL), F32), jax.ShapeDtypeStruct(pool_w.shape, F32),
                   jax.ShapeDtypeStruct((8, POOL_WIDTH), F32)],
        scratch_shapes=[pltpu.VMEM((POOL_HALO, POOL_WIDTH), F32)],
        exchanges=exchanges)


def _in_bwd(dq, dkc, dkp, dvc, dvp, du, x, dx1, vec, w_in_t, rot, name, exchanges=()):
    s = x.shape[0]
    ts = _seq_tile(s)
    nt = s // ts
    bpt = ts // BLOCK

    def body(dq_ref, dkc_ref, dkp_ref, dkn_ref, dvc_ref, dvp_ref, dvn_ref, du_ref, x_ref, dx1_ref, vec_ref, w_ref,
             cos_ref, lo_ref, hi_ref, dx_ref, dw_ref, cs_ref, db_ref):
        i = pl.program_id(0)

        @pl.when(i == 0)
        def _():
            dw_ref[...] = jnp.zeros_like(dw_ref)
            cs_ref[...] = jnp.zeros_like(cs_ref)
            db_ref[...] = jnp.zeros_like(db_ref)

        cos, lo, hi = cos_ref[...], lo_ref[...], hi_ref[...]

        def with_next_block(cur_ref, prev_ref, next_ref):
            nxt = jnp.where(i < nt - 1, next_ref[...], 0.0)
            later = nxt if bpt == 1 else jnp.concatenate([prev_ref[BLOCK:, :], nxt], axis=0)
            return cur_ref[...] + later

        pieces = [_rotate_bwd(dq_ref[:, j * LANES:(j + 1) * LANES], cos, lo, hi) for j in range(ATTN_WIDTH // LANES)]
        pieces.append(_rotate_bwd(with_next_block(dkc_ref, dkp_ref, dkn_ref), cos, lo, hi))
        pieces.append(with_next_block(dvc_ref, dvp_ref, dvn_ref))
        pieces.append(du_ref[...])
        dproj = jnp.concatenate(pieces, axis=1)
        db_ref[0:1, :] += _colsum(dproj)
        dproj_b = dproj.astype(BF16)
        dh = _dot(dproj_b, w_ref[...])

        xt = x_ref[...]
        g_pre, one_scale = vec_ref[0:1, :], vec_ref[1:2, :]
        r = _rsqrt_ms(xt)
        x_hat = xt * r
        dw_ref[...] += _dot(dproj_b, (x_hat * g_pre * one_scale + vec_ref[2:3, :]).astype(BF16), TN)
        dx_ref[...] = dx1_ref[...] + _norm_bwd(dh * g_pre * one_scale, x_hat, r)
        _accumulate_rows(cs_ref, [_colsum(dh), _colsum(dh * (x_hat * g_pre)), _colsum(dh * one_scale * x_hat)])

    row = lambda w: pl.BlockSpec((ts, w), lambda i: (i, 0))
    nxt = pl.BlockSpec((BLOCK, KV_WIDTH), lambda i: (jnp.minimum((i + 1) * bpt, s // BLOCK - 1), 0))
    fixed = lambda shape: pl.BlockSpec(shape, lambda i: (0,) * len(shape))
    return _call(
        body, [dq, dkc, dkp, dkp, dvc, dvp, dvp, du, x, dx1, vec, w_in_t, *rot], name=name, grid=(nt,),
        in_specs=[row(ATTN_WIDTH), row(KV_WIDTH), row(KV_WIDTH), nxt, row(KV_WIDTH), row(KV_WIDTH), nxt,
                  row(POOL_WIDTH), row(D_MODEL), row(D_MODEL), _resident((8, D_MODEL)), _resident((IN_WIDTH, D_MODEL)),
                  row(LANES), row(LANES), row(LANES)],
        out_specs=[row(D_MODEL), _resident((IN_WIDTH, D_MODEL)), fixed((8, D_MODEL)), fixed((8, IN_WIDTH))],
        out_shape=[jax.ShapeDtypeStruct((s, D_MODEL), F32), jax.ShapeDtypeStruct((IN_WIDTH, D_MODEL), F32),
                   jax.ShapeDtypeStruct((8, D_MODEL), F32), jax.ShapeDtypeStruct((8, IN_WIDTH), F32)],
        exchanges=exchanges)


def _row_tile(rows):
    for t in (512, 352, 256, 176, 160, 128, 64, 32, 16, 8):
        if rows % t == 0:
            return t
    raise ValueError(f"no row tile for {rows} rows")


def _sum_parts(own, others, name):
    r, c = own.shape
    n_other = others.shape[0]
    tr = _row_tile(r)

    def body(own_ref, oth_ref, o_ref):
        acc = own_ref[...].astype(F32)
        for k in range(n_other):
            acc = acc + oth_ref[k].astype(F32)
        o_ref[...] = acc

    return pl.pallas_call(
        body, name=name, grid=(r // tr,),
        in_specs=[pl.BlockSpec((tr, c), lambda i: (i, 0)), pl.BlockSpec((n_other, tr, c), lambda i: (0, i, 0))],
        out_specs=pl.BlockSpec((tr, c), lambda i: (i, 0)),
        out_shape=jax.ShapeDtypeStruct((r, c), F32),
        compiler_params=_cparams(1),
    )(own, others)


def _adamw(w, g, m, v, name):
    r, c = w.shape
    tr = _row_tile(r)

    def body(w_ref, g_ref, m_ref, v_ref, d_ref, nm_ref, nv_ref):
        g_t = g_ref[...]
        m_new = ADAM_B1 * m_ref[...] + (1.0 - ADAM_B1) * g_t
        v_new = ADAM_B2 * v_ref[...] + (1.0 - ADAM_B2) * (g_t * g_t)
        m_hat = m_new / (1.0 - ADAM_B1 ** ADAM_STEP)
        v_hat = v_new / (1.0 - ADAM_B2 ** ADAM_STEP)
        d_ref[...] = -ADAM_LR * (m_hat / (jnp.sqrt(v_hat) + ADAM_EPS) + ADAM_WD * w_ref[...])
        nm_ref[...] = m_new
        nv_ref[...] = v_new

    spec = pl.BlockSpec((tr, c), lambda i: (i, 0))
    shape = jax.ShapeDtypeStruct((r, c), F32)
    return pl.pallas_call(
        body, name=name, grid=(r // tr,), in_specs=[spec] * 4, out_specs=[spec] * 3, out_shape=[shape] * 3,
        compiler_params=_cparams(1),
    )(w, g, m, v)


SMALL_ROWS = 160


def _pack_small(dmod, gains, b_in, pool_scale, sinks, pool_w):
    pad = lambda a, width: jnp.pad(a, ((0, 0), (0, width - a.shape[1])))
    parts = [dmod.reshape(DEPTH * N_MOD, D_MODEL), gains.reshape(4 * DEPTH, D_MODEL),
             pad(b_in, 2 * D_MODEL).reshape(2 * DEPTH, D_MODEL), pool_scale.reshape(1, D_MODEL),
             pad(sinks.reshape(1, DEPTH * N_HEADS), D_MODEL), pool_w.reshape(-1, D_MODEL)]
    packed = jnp.concatenate(parts, axis=0)
    return jnp.pad(packed, ((0, SMALL_ROWS - packed.shape[0]), (0, 0)))


def _unpack_small(p):
    r = DEPTH * N_MOD
    dmod = p[0:r].reshape(DEPTH, N_MOD * D_MODEL)
    gains = p[r:r + 4 * DEPTH].reshape(4, DEPTH, D_MODEL)
    r += 4 * DEPTH
    b_in = p[r:r + 2 * DEPTH].reshape(DEPTH, 2 * D_MODEL)[:, :IN_WIDTH]
    r += 2 * DEPTH
    pool_scale = p[r].reshape(DEPTH, POOL_WIDTH)
    sinks = p[r + 1, :DEPTH * N_HEADS].reshape(DEPTH, N_HEADS)
    pool_w = p[r + 2:r + 2 + 128].reshape(DEPTH, len(POOL_WINDOWS), POOL_GROUP_WIDTH, POOL_GROUP_WIDTH)
    return dmod, gains, b_in, pool_scale, sinks, pool_w


LAYER_ROWS = 80


def _pack_layer(dmod, gains, b_in, pool_scale, sinks, pool_w):
    misc = jnp.concatenate([pool_scale, sinks, jnp.zeros((D_MODEL - POOL_WIDTH - N_HEADS,), F32)])
    parts = [dmod.reshape(N_MOD, D_MODEL), gains, jnp.pad(b_in, (0, 2 * D_MODEL - IN_WIDTH)).reshape(2, D_MODEL),
             misc.reshape(1, D_MODEL), pool_w.reshape(-1, D_MODEL)]
    packed = jnp.concatenate(parts, axis=0)
    return jnp.pad(packed, ((0, LAYER_ROWS - packed.shape[0]), (0, 0)))


def _unpack_layer(p):
    r = N_MOD + 4
    b_in = p[r:r + 2].reshape(2 * D_MODEL)[:IN_WIDTH]
    pool_w = p[r + 3:r + 3 + 64].reshape(len(POOL_WINDOWS), POOL_GROUP_WIDTH, POOL_GROUP_WIDTH)
    return (p[0:N_MOD].reshape(N_MOD * D_MODEL), p[N_MOD:r], b_in, p[r + 2, :POOL_WIDTH],
            p[r + 2, POOL_WIDTH:POOL_WIDTH + N_HEADS], pool_w)


def _rotary_tables(positions):
    inv_freq = ROPE_THETA ** (-jnp.arange(0, ROT_DIM, 2, dtype=F32) / ROT_DIM)
    half = ROT_DIM // 2
    head_freq = jnp.concatenate([inv_freq, inv_freq, jnp.zeros((HEAD_DIM - ROT_DIM,), F32)])
    lane_freq = jnp.concatenate([head_freq, head_freq])
    ang = positions.reshape(-1).astype(F32)[:, None] * lane_freq[None, :]
    cos, sin = jnp.cos(ang), jnp.sin(ang)
    in_head = lax.broadcasted_iota(jnp.int32, (1, LANES), 1) % HEAD_DIM
    return cos, jnp.where(in_head < half, -sin, 0.0), jnp.where((in_head >= half) & (in_head < ROT_DIM), sin, 0.0)


def kernel(x, c, positions, ada_w, ada_b, w_in, b_in, sinks, pool_w, pool_scale, w_out, w_gate, w_up, w_down, g_pre_mix, g_post_mix, g_pre_ffn, g_post_ffn, loss_target, m_ada_w, m_ada_b, m_w_in, m_b_in, m_sinks, m_pool_w, m_pool_scale, m_w_out, m_w_gate, m_w_up, m_w_down, m_g_pre_mix, m_g_post_mix, m_g_pre_ffn, m_g_post_ffn, v_ada_w, v_ada_b, v_w_in, v_b_in, v_sinks, v_pool_w, v_pool_scale, v_w_out, v_w_gate, v_w_up, v_w_down, v_g_pre_mix, v_g_post_mix, v_g_pre_ffn, v_g_post_ffn):
    me = _index(_mesh_place())
    x0 = x[0]
    target = loss_target[0]
    rot = _rotary_tables(positions)
    ada_cols = ada_w.shape[2]

    c_all = _allgather_vmem(jnp.broadcast_to(c, (8, D_MODEL)), "gather_c")[:, 0, :]
    mod_part, c_act = _mod_fwd(c_all, ada_w)
    mod_all = _allgather_vmem(mod_part.reshape(DEPTH * N_DEV, ada_cols), "gather_mod")
    mod_mine = lax.dynamic_slice_in_dim(mod_all.reshape(N_DEV, DEPTH, N_DEV, ada_cols), me, 1, axis=2)[:, :, 0, :]
    mod = jnp.transpose(mod_mine, (1, 0, 2)).reshape(DEPTH, N_MOD * D_MODEL) + ada_b
    mod = mod.reshape(DEPTH, N_MOD, D_MODEL)

    shards = []
    for l in range(DEPTH):
        shards += [w_in[l].T.astype(BF16), w_out[l].astype(BF16), w_gate[l].T.astype(BF16), w_up[l].T.astype(BF16),
                   w_down[l].astype(BF16)]
    whole = lambda gathered: [g.reshape(-1, D_MODEL) for g in gathered]
    full = [None] * (5 * DEPTH)
    full[0:2] = whole(_exchanges_alone([_Gather(shards[0:2])], "gather_w_in_out_0")[0])
    hosted = {("fwd_in", 0): (_Gather(shards[4:5]), [4]), ("attn_fwd", 0): (_Gather(shards[2:4]), [2, 3]),
              ("ffn_fwd", 0): (_Gather(shards[5:10]), [5, 6, 7, 8, 9])}

    def take(kind, l, ex_outs):
        if (kind, l) in hosted:
            for slot, g in zip(hosted[kind, l][1], whole(ex_outs[0])):
                full[slot] = g

    def beside(kind, l):
        return [hosted[kind, l][0]] if (kind, l) in hosted else []

    pool_w_b = pool_w.astype(BF16)

    saved = []
    xl = x0
    for l in range(DEPTH):
        vec_in = _rows(g_pre_mix[l], 1.0 + mod[l, 1], mod[l, 0], width=D_MODEL)
        vec_out = _rows(mod[l, 2], g_post_mix[l], width=D_MODEL)
        vec_ffn = _rows(g_pre_ffn[l], 1.0 + mod[l, 4], mod[l, 3], mod[l, 5], g_post_ffn[l], width=D_MODEL)
        psc = _rows(pool_scale[l], width=POOL_WIDTH)
        (q, k, v, u), ex = _fwd_in(xl, vec_in, full[5 * l], _rows(b_in[l], width=IN_WIDTH), rot, f"fwd_in_{l}",
                                   beside("fwd_in", l))
        take("fwd_in", l, ex)
        (ao, lse), ex = _attn_fwd_t(sinks[l], q, k, v, f"attn_fwd_{l}", beside("attn_fwd", l))
        take("attn_fwd", l, ex)
        x1, mix = _fwd_out(ao, u, pool_w_b[l], psc, full[5 * l + 1], xl, vec_out, f"fwd_out_{l}")
        (x2, f, gt, up, h2, *loss_part), ex = _ffn_fwd(
            x1, vec_ffn, full[5 * l + 2], full[5 * l + 3], full[5 * l + 4], f"ffn_fwd_{l}", beside("ffn_fwd", l),
            target=target if l == DEPTH - 1 else None)
        take("ffn_fwd", l, ex)
        saved.append((xl, q, k, v, u, ao, lse, x1, mix, f, gt, up, h2, vec_in, vec_out, vec_ffn, psc))
        xl = x2

    dx = xl
    loss = lax.psum(loss_part[0][0, 0], ("x", "y", "c"))

    stacks = [None] * (5 * DEPTH)
    others = [None] * (5 * DEPTH)
    small_all = [None] * DEPTH
    ready_stacks, ready_small = [], []

    def leaving():
        exs, notes = [], []
        if ready_stacks:
            exs.append(_Scatter([stacks[a] for a in ready_stacks]))
            notes.append(("stacks", list(ready_stacks)))
            ready_stacks.clear()
        if ready_small:
            exs.append(_Gather([ready_small[0][1]]))
            notes.append(("small", ready_small[0][0]))
            ready_small.clear()
        return exs, notes

    def arrived(notes, ex_outs):
        for (kind, what), outs in zip(notes, ex_outs):
            if kind == "stacks":
                for a, o in zip(what, outs):
                    others[a] = o
            else:
                small_all[what] = outs[0]

    def grad_ready(a, grad):
        stacks[a] = grad.reshape(N_DEV, -1, D_MODEL)
        ready_stacks.append(a)

    def hosting(kernel_fn, *args):
        exs, notes = leaving()
        outs, ex_outs = kernel_fn(*args, exs)
        arrived(notes, ex_outs)
        return outs

    for l in reversed(range(DEPTH)):
        w_in_t, w_out_f, w_gate_t, w_up_t, w_down_f = full[5 * l:5 * l + 5]
        xin, q, k, v, u, ao, lse, x1, mix, f, gt, up, h2, vec_in, vec_out, vec_ffn, psc = saved[l]
        dgt, dup, d_w_down, cs_a = hosting(_ffn_bwd_act, dx, f, gt, up, vec_ffn, w_down_f, f"ffn_bwd_act_{l}")
        grad_ready(5 * l + 4, d_w_down.astype(BF16))
        dx1, cs_f = hosting(_ffn_bwd_in, dx, x1, dgt, dup, vec_ffn, w_gate_t, w_up_t, f"ffn_bwd_in_{l}")
        grad_ready(5 * l + 2, hosting(_weight_grad, dgt, h2, f"grad_w_gate_{l}"))
        grad_ready(5 * l + 3, hosting(_weight_grad, dup, h2, f"grad_w_up_{l}"))
        do, du, d_w_out, cs_o, dpw, dpsc = hosting(_out_bwd, dx1, mix, ao, u, pool_w_b[l], psc, w_out_f, vec_out,
                                                   f"out_bwd_{l}")
        grad_ready(5 * l + 1, d_w_out.astype(BF16))
        dq, dkc, dkp, dvc, dvp, dsk = hosting(_attn_bwd_t, sinks[l], q, do, lse, k, v, f"attn_bwd_{l}")
        dx, d_w_in_t, cs_i, db = hosting(_in_bwd, dq, dkc, dkp, dvc, dvp, du, xin, dx1, vec_in, w_in_t, rot,
                                         f"in_bwd_{l}")
        grad_ready(5 * l, d_w_in_t.astype(BF16))
        d_mod = jnp.concatenate([cs_i[0], cs_i[1], cs_o[0], cs_f[0], cs_f[1], cs_a[0]])
        d_gain = jnp.stack([cs_i[2], cs_o[1], cs_f[2], cs_a[1]])
        ready_small.append((l, _pack_layer(d_mod, d_gain, db[0], dpsc[0], dsk[:, 0], dpw)))
    grad_x = dx[None]
    exs, notes = leaving()
    arrived(notes, _exchanges_alone(exs, "exchange_last"))

    layer_sums = [_unpack_layer(_sum_parts(small_all[l][0], small_all[l][1:], f"sum_small_{l}")) for l in range(DEPTH)]
    g_ada_b, g_gains, g_b_in, g_pool_scale, g_sinks, g_pool_w = (
        jnp.stack([layer_sums[l][j] for l in range(DEPTH)], axis=1 if j == 1 else 0) for j in range(6))
    small_sum = _pack_small(g_ada_b, g_gains, g_b_in, g_pool_scale, g_sinks, g_pool_w)
    gains = jnp.stack([g_pre_mix, g_post_mix, g_pre_ffn, g_post_ffn])
    m_gains = jnp.stack([m_g_pre_mix, m_g_post_mix, m_g_pre_ffn, m_g_post_ffn])
    v_gains = jnp.stack([v_g_pre_mix, v_g_post_mix, v_g_pre_ffn, v_g_post_ffn])
    small_step = _adamw(_pack_small(ada_b, gains, b_in, pool_scale, sinks, pool_w), small_sum,
                        _pack_small(m_ada_b, m_gains, m_b_in, m_pool_scale, m_sinks, m_pool_w),
                        _pack_small(v_ada_b, v_gains, v_b_in, v_pool_scale, v_sinks, v_pool_w), "adamw_small")
    small_out = [_unpack_small(p) for p in small_step]

    dmod_all = jnp.stack([small_all[l][:, 0:N_MOD, :].reshape(N_DEV, N_DEV, ada_cols) for l in range(DEPTH)])
    dmod_cols = lax.dynamic_slice_in_dim(dmod_all, me, 1, axis=2)[:, :, 0, :]
    g_ada_w = _ada_grad(c_act, dmod_cols)

    summed = []
    for a, (stack, oth) in enumerate(zip(stacks, others)):
        own = lax.dynamic_index_in_dim(stack, me, axis=0, keepdims=False)
        summed.append(_sum_parts(own, oth, f"sum_grads_{a}"))
    per_kind = [jnp.stack([summed[5 * l + j] for l in range(DEPTH)]) for j in range(5)]
    g_w_in, g_w_out, g_w_gate, g_w_up, g_w_down = per_kind
    g_w_in, g_w_gate, g_w_up = (jnp.transpose(g, (0, 2, 1)) for g in (g_w_in, g_w_gate, g_w_up))

    def step(w, g, m, v, name):
        flat = lambda a: a.reshape(-1, a.shape[-1])
        return [o.reshape(w.shape) for o in _adamw(flat(w), flat(g), flat(m), flat(v), name)]

    big = {
        "ada_w": (g_ada_w, step(ada_w, g_ada_w, m_ada_w, v_ada_w, "adamw_ada_w")),
        "w_in": (g_w_in, step(w_in, g_w_in, m_w_in, v_w_in, "adamw_w_in")),
        "w_out": (g_w_out, step(w_out, g_w_out, m_w_out, v_w_out, "adamw_w_out")),
        "w_gate": (g_w_gate, step(w_gate, g_w_gate, m_w_gate, v_w_gate, "adamw_w_gate")),
        "w_up": (g_w_up, step(w_up, g_w_up, m_w_up, v_w_up, "adamw_w_up")),
        "w_down": (g_w_down, step(w_down, g_w_down, m_w_down, v_w_down, "adamw_w_down")),
    }

    def ordered(pick_big, pick_small):
        ada_b_, gains_, b_in_, pool_scale_, sinks_, pool_w_ = pick_small
        return [pick_big("ada_w"), ada_b_, pick_big("w_in"), b_in_, sinks_, pool_w_, pool_scale_, pick_big("w_out"),
                pick_big("w_gate"), pick_big("w_up"), pick_big("w_down"), gains_[0], gains_[1], gains_[2], gains_[3]]

    grads = ordered(lambda n: big[n][0], (g_ada_b, g_gains, g_b_in, g_pool_scale, g_sinks, g_pool_w))
    deltas = ordered(lambda n: big[n][1][0], small_out[0])
    new_m = ordered(lambda n: big[n][1][1], small_out[1])
    new_v = ordered(lambda n: big[n][1][2], small_out[2])
    return (loss, grad_x, *grads, *deltas, *new_m, *new_v)
```

```python
import jax
import jax.numpy as jnp
from jax import lax
from jax.experimental import pallas as pl
from jax.experimental.pallas import tpu as pltpu

F32 = jnp.float32
BF16 = jnp.bfloat16

N_DEV = 8
DEPTH = 2
D_MODEL = 1024
HEAD_DIM = 64
N_HEADS = 8
N_KV_HEADS = 2
GROUP = N_HEADS // N_KV_HEADS
ATTN_WIDTH = N_HEADS * HEAD_DIM
KV_WIDTH = N_KV_HEADS * HEAD_DIM
POOL_WIDTH = 512
POOL_WINDOWS = (2, 4, 8, 16)
POOL_GROUP_WIDTH = 128
POOL_HALO = 16
IN_WIDTH = ATTN_WIDTH + 2 * KV_WIDTH + POOL_WIDTH
D_FF = 2816
N_MOD = 6
BLOCK = 128
ROT_DIM = 16
ROPE_THETA = 500000.0
EPS = 1e-6
NEG_INF = -1e30
Q_SCALE = HEAD_DIM ** -0.5

ADAM_LR = 0.001
ADAM_B1 = 0.9
ADAM_B2 = 0.999
ADAM_EPS = 1e-08
ADAM_WD = 0.01
ADAM_STEP = 10

LANES = 128
SEQ_TILE = 512
ATTN_BLOCKS = 2
FF_CHUNK = 256
VMEM_LIMIT = 56 * 1024 * 1024
MESH = pl.DeviceIdType.MESH

NT = (((1,), (1,)), ((), ()))
TN = (((0,), (0,)), ((), ()))


def _dot(a, b, dims=None):
    if dims is None:
        return jnp.dot(a, b, preferred_element_type=F32)
    return lax.dot_general(a, b, dims, preferred_element_type=F32)


def _cparams(n_axes):
    return pltpu.CompilerParams(dimension_semantics=("arbitrary",) * n_axes, vmem_limit_bytes=VMEM_LIMIT)


def _resident(shape):
    zeros = (0,) * len(shape)
    return pl.BlockSpec(shape, lambda *_: zeros, pipeline_mode=pl.Buffered(1))


def _rows(*vectors, width):
    rows = [jnp.reshape(v, (1, width)).astype(F32) for v in vectors]
    rows.append(jnp.zeros((8 - len(rows), width), F32))
    return jnp.concatenate(rows, axis=0)


def _rsqrt_ms(x):
    return lax.rsqrt(jnp.mean(x * x, axis=-1, keepdims=True) + EPS)


def _colsum(x):
    return jnp.sum(x, axis=0, keepdims=True)


def _seq_tile(s, tiles=1):
    return min(s, tiles * SEQ_TILE)


def _mesh_place():
    x, y, c = lax.axis_index("x"), lax.axis_index("y"), lax.axis_index("c")
    return x, y, c


def _flip(place, k):
    x, y, c = place
    return (1 - x if k & 4 else x, 1 - y if k & 2 else y, 1 - c if k & 1 else c)


def _index(place):
    x, y, c = place
    return 4 * x + 2 * y + c


def _allgather_vmem(block, name):
    r, c = block.shape

    def body(x_ref, out_ref, send_sems, recv_sems, local_sem):
        me = _mesh_place()
        mine = pltpu.make_async_copy(x_ref, out_ref.at[_index(me)], local_sem)
        mine.start()

        def copy(k):
            return pltpu.make_async_remote_copy(
                src_ref=x_ref, dst_ref=out_ref.at[_index(me)], send_sem=send_sems.at[k - 1], recv_sem=recv_sems.at[k - 1],
                device_id=_flip(me, k), device_id_type=MESH)

        def arrival(k):
            return pltpu.make_async_remote_copy(
                src_ref=x_ref, dst_ref=out_ref.at[_index(_flip(me, k))], send_sem=send_sems.at[k - 1],
                recv_sem=recv_sems.at[k - 1], device_id=_flip(me, k), device_id_type=MESH)

        for k in range(1, N_DEV):
            copy(k).start()
        for k in range(1, N_DEV):
            arrival(k).wait_recv()
        for k in range(1, N_DEV):
            copy(k).wait_send()
        mine.wait()

    return pl.pallas_call(
        body, name=name,
        out_shape=jax.ShapeDtypeStruct((N_DEV, r, c), block.dtype),
        in_specs=[pl.BlockSpec(memory_space=pltpu.VMEM)],
        out_specs=pl.BlockSpec(memory_space=pltpu.VMEM),
        scratch_shapes=[pltpu.SemaphoreType.DMA((N_DEV - 1,)), pltpu.SemaphoreType.DMA((N_DEV - 1,)),
                        pltpu.SemaphoreType.DMA],
    )(block)


class _Gather:
    def __init__(self, shards):
        n = len(shards)
        self.operands = list(shards)
        self.out_shape = [jax.ShapeDtypeStruct((N_DEV,) + s.shape, s.dtype) for s in shards]
        self.scratch = [pltpu.SemaphoreType.DMA((N_DEV - 1, n)), pltpu.SemaphoreType.DMA((N_DEV - 1, n)),
                        pltpu.SemaphoreType.DMA((n,))]

    def _copies(self, x_refs, out_refs, sems):
        send_sems, recv_sems, local_sems = sems
        n = len(x_refs)
        x, y, c = _mesh_place()
        me, sibling = (x, y, c), (x, y, 1 - c)
        chips = [(1 - x, y), (x, 1 - y), (1 - x, 1 - y)]

        def copy(k, a, block, to, from_input=False):
            rows = out_refs[a].at[_index(block)]
            return pltpu.make_async_remote_copy(
                src_ref=x_refs[a] if from_input else rows, dst_ref=rows,
                send_sem=send_sems.at[k, a], recv_sem=recv_sems.at[k, a], device_id=to, device_id_type=MESH)

        mine = [pltpu.make_async_copy(x_refs[a], out_refs[a].at[_index(me)], local_sems.at[a]) for a in range(n)]
        first = [copy(0, a, me, sibling, from_input=True) for a in range(n)]
        first += [copy(1 + j, a, me, (*chip, c), from_input=True) for j, chip in enumerate(chips) for a in range(n)]
        over_ici = [copy(1 + j, a, (*chip, c), me) for j, chip in enumerate(chips) for a in range(n)]
        passed = [copy(4 + j, a, (*chip, c), sibling) for j, chip in enumerate(chips) for a in range(n)]
        from_sibling = [copy(0, a, sibling, me) for a in range(n)]
        from_sibling += [copy(4 + j, a, (*chip, 1 - c), me) for j, chip in enumerate(chips) for a in range(n)]
        return mine, first, over_ici, passed, from_sibling

    def begin(self, x_refs, out_refs, sems):
        mine, first, _, _, _ = self._copies(x_refs, out_refs, sems)
        for cp in mine + first:
            cp.start()

    def middle(self, x_refs, out_refs, sems):
        _, _, over_ici, passed, _ = self._copies(x_refs, out_refs, sems)
        for arrived, onward in zip(over_ici, passed):
            arrived.wait_recv()
            onward.start()

    def end(self, x_refs, out_refs, sems):
        mine, first, _, passed, from_sibling = self._copies(x_refs, out_refs, sems)
        for cp in from_sibling:
            cp.wait_recv()
        for cp in first + passed:
            cp.wait_send()
        for cp in mine:
            cp.wait()


class _Scatter:
    def __init__(self, stacks):
        n = len(stacks)
        self.operands = list(stacks)
        self.out_shape = [jax.ShapeDtypeStruct((N_DEV - 1,) + s.shape[1:], s.dtype) for s in stacks]
        self.scratch = [pltpu.SemaphoreType.DMA((N_DEV - 1, n)), pltpu.SemaphoreType.DMA((N_DEV - 1, n))]

    def _copies(self, g_refs, out_refs, sems):
        send_sems, recv_sems = sems
        me = _mesh_place()

        def copy(k, a):
            peer = _flip(me, k)
            return pltpu.make_async_remote_copy(
                src_ref=g_refs[a].at[_index(peer)], dst_ref=out_refs[a].at[k - 1],
                send_sem=send_sems.at[k - 1, a], recv_sem=recv_sems.at[k - 1, a], device_id=peer, device_id_type=MESH)

        return [copy(k, a) for k in range(1, N_DEV) for a in range(len(g_refs))]

    def begin(self, g_refs, out_refs, sems):
        for cp in self._copies(g_refs, out_refs, sems):
            cp.start()

    def middle(self, g_refs, out_refs, sems):
        pass

    def end(self, g_refs, out_refs, sems):
        copies = self._copies(g_refs, out_refs, sems)
        for cp in copies:
            cp.wait_recv()
        for cp in copies:
            cp.wait_send()


def _call(body, args, *, name, grid, in_specs, out_specs, out_shape, scratch_shapes=(), exchanges=()):
    if not exchanges:
        outs = pl.pallas_call(body, name=name, grid=grid, in_specs=in_specs, out_specs=out_specs, out_shape=out_shape,
                              scratch_shapes=list(scratch_shapes), compiler_params=_cparams(len(grid)))(*args)
        return outs, []
    (steps,) = grid
    n_in, n_out, n_scr = len(in_specs), len(out_specs), len(scratch_shapes)
    ex_in = [len(ex.operands) for ex in exchanges]
    ex_out = [len(ex.out_shape) for ex in exchanges]
    ex_scr = [len(ex.scratch) for ex in exchanges]

    def split(refs, counts):
        parts, pos = [], 0
        for cnt in counts:
            parts.append(refs[pos:pos + cnt])
            pos += cnt
        return parts

    def wrapped(*refs):
        ins, xin, outs, xout, scr, xscr = split(refs, [n_in, sum(ex_in), n_out, sum(ex_out), n_scr, sum(ex_scr)])
        bound = list(zip(exchanges, split(xin, ex_in), split(xout, ex_out), split(xscr, ex_scr)))
        step = pl.program_id(0)

        def phase(method, at):
            @pl.when(step == at)
            def _():
                for ex, i_refs, o_refs, sems in bound:
                    getattr(ex, method)(i_refs, o_refs, sems)

        phase("begin", 0)
        phase("middle", (3 * steps) // 4)
        body(*ins, *outs, *scr)
        phase("end", steps - 1)

    any_spec = pl.BlockSpec(memory_space=pl.ANY)
    results = pl.pallas_call(
        wrapped, name=name, grid=grid,
        in_specs=list(in_specs) + [any_spec] * sum(ex_in),
        out_specs=list(out_specs) + [any_spec] * sum(ex_out),
        out_shape=list(out_shape) + [s for ex in exchanges for s in ex.out_shape],
        scratch_shapes=list(scratch_shapes) + [s for ex in exchanges for s in ex.scratch],
        compiler_params=_cparams(1),
    )(*args, *[a for ex in exchanges for a in ex.operands])
    return results[:n_out], split(results[n_out:], ex_out)


def _exchanges_alone(exchanges, name):
    def body(flag_ref):
        flag_ref[...] = jnp.zeros_like(flag_ref)

    _, ex_outs = _call(body, [], name=name, grid=(1,), in_specs=[], out_specs=[pl.BlockSpec((8, LANES), lambda i: (0, 0))],
                       out_shape=[jax.ShapeDtypeStruct((8, LANES), F32)], exchanges=exchanges)
    return ex_outs


def _mod_fwd(c_all, ada_w):
    cols = ada_w.shape[2]

    def body(c_ref, w_ref, mp_ref, act_ref):
        c = c_ref[...]
        act = c * jax.nn.sigmoid(c)
        act_ref[...] = act
        mp_ref[0] = _dot(act.astype(BF16), w_ref[0].astype(BF16))

    return pl.pallas_call(
        body, name="mod_fwd", grid=(DEPTH,),
        in_specs=[pl.BlockSpec((N_DEV, D_MODEL), lambda l: (0, 0)),
                  pl.BlockSpec((1, D_MODEL, cols), lambda l: (l, 0, 0))],
        out_specs=[pl.BlockSpec((1, N_DEV, cols), lambda l: (l, 0, 0)),
                   pl.BlockSpec((N_DEV, D_MODEL), lambda l: (0, 0))],
        out_shape=[jax.ShapeDtypeStruct((DEPTH, N_DEV, cols), F32), jax.ShapeDtypeStruct((N_DEV, D_MODEL), F32)],
        compiler_params=_cparams(1),
    )(c_all, ada_w)


def _ada_grad(c_act, dmod_cols):
    cols = dmod_cols.shape[2]

    def body(act_ref, dm_ref, g_ref):
        g_ref[0] = _dot(act_ref[...].astype(BF16), dm_ref[0].astype(BF16), TN)

    return pl.pallas_call(
        body, name="ada_grad", grid=(DEPTH,),
        in_specs=[pl.BlockSpec((N_DEV, D_MODEL), lambda l: (0, 0)),
                  pl.BlockSpec((1, N_DEV, cols), lambda l: (l, 0, 0))],
        out_specs=pl.BlockSpec((1, D_MODEL, cols), lambda l: (l, 0, 0)),
        out_shape=jax.ShapeDtypeStruct((DEPTH, D_MODEL, cols), F32),
        compiler_params=_cparams(1),
    )(c_act, dmod_cols)


def _rotate(t, cos, sin_lo, sin_hi):
    return t * cos + pltpu.roll(t, LANES - 8, 1) * sin_lo + pltpu.roll(t, 8, 1) * sin_hi


def _rotate_bwd(t, cos, sin_lo, sin_hi):
    return t * cos + pltpu.roll(t * sin_lo, 8, 1) + pltpu.roll(t * sin_hi, LANES - 8, 1)


def _fwd_in(x, vec, w_in_t, b_in, rot, name, exchanges=()):
    s = x.shape[0]
    ts = _seq_tile(s, 2)

    def body(x_ref, vec_ref, w_ref, b_ref, cos_ref, lo_ref, hi_ref, q_ref, k_ref, v_ref, u_ref):
        xt = x_ref[...]
        h = xt * _rsqrt_ms(xt) * vec_ref[0:1, :] * vec_ref[1:2, :] + vec_ref[2:3, :]
        proj = _dot(h.astype(BF16), w_ref[...], NT) + b_ref[0:1, :]
        cos, lo, hi = cos_ref[...], lo_ref[...], hi_ref[...]
        for j in range(ATTN_WIDTH // LANES):
            q_ref[:, j * LANES:(j + 1) * LANES] = (
                _rotate(proj[:, j * LANES:(j + 1) * LANES], cos, lo, hi) * Q_SCALE).astype(BF16)
        k_ref[...] = _rotate(proj[:, ATTN_WIDTH:ATTN_WIDTH + KV_WIDTH], cos, lo, hi).astype(BF16)
        v_ref[...] = proj[:, ATTN_WIDTH + KV_WIDTH:ATTN_WIDTH + 2 * KV_WIDTH].astype(BF16)
        u_ref[...] = proj[:, ATTN_WIDTH + 2 * KV_WIDTH:]

    row = lambda w: pl.BlockSpec((ts, w), lambda i: (i, 0))
    return _call(
        body, [x, vec, w_in_t, b_in, *rot], name=name, grid=(s // ts,),
        in_specs=[row(D_MODEL), _resident((8, D_MODEL)), _resident((IN_WIDTH, D_MODEL)), _resident((8, IN_WIDTH)),
                  row(LANES), row(LANES), row(LANES)],
        out_specs=[row(ATTN_WIDTH), row(KV_WIDTH), row(KV_WIDTH), row(POOL_WIDTH)],
        out_shape=[jax.ShapeDtypeStruct((s, ATTN_WIDTH), BF16), jax.ShapeDtypeStruct((s, KV_WIDTH), BF16),
                   jax.ShapeDtypeStruct((s, KV_WIDTH), BF16), jax.ShapeDtypeStruct((s, POOL_WIDTH), F32)],
        exchanges=exchanges)


def _band_mask_t(first_block):
    key = lax.broadcasted_iota(jnp.int32, (2 * BLOCK, 2 * BLOCK), 0)
    qry = lax.broadcasted_iota(jnp.int32, (2 * BLOCK, 2 * BLOCK), 1) & (BLOCK - 1)
    d = key - qry
    return (d >= 1) & (d <= BLOCK) & ((key >= BLOCK) | jnp.logical_not(first_block))


def _placed(both, kh, b):
    lane = lax.broadcasted_iota(jnp.int32, (1, LANES), 1)
    src = both if kh == b else pltpu.roll(both, HEAD_DIM, 1)
    return jnp.where((lane >= b * HEAD_DIM) & (lane < (b + 1) * HEAD_DIM), src, jnp.zeros_like(src))


def _pair(ref, rows, kh, r):
    j = 2 * kh + r
    return ref[rows, j * LANES:(j + 1) * LANES]


def _attn_specs(s):
    qb = min(ATTN_BLOCKS, s // BLOCK)
    cur = lambda w: pl.BlockSpec((qb * BLOCK, w), lambda i: (i, 0))
    prev = lambda w: pl.BlockSpec((BLOCK, w), lambda i: (jnp.maximum(i * qb - 1, 0), 0))
    return qb, cur, prev, pl.BlockSpec((N_HEADS, qb * BLOCK), lambda i: (0, i))


def _kv_window(prev_ref, cur_ref, jb):
    before = prev_ref[...] if jb == 0 else cur_ref[(jb - 1) * BLOCK:jb * BLOCK, :]
    return jnp.concatenate([before, cur_ref[jb * BLOCK:(jb + 1) * BLOCK, :]], axis=0)


def _sink_row(sink_ref, kh, b):
    lane = lax.broadcasted_iota(jnp.int32, (1, 2 * BLOCK), 1)
    return jnp.where(lane < BLOCK, sink_ref[GROUP * kh + b], sink_ref[GROUP * kh + 2 + b])


def _attn_fwd_t(sinks, q, k, v, name, exchanges=()):
    s = q.shape[0]
    qb, cur, prev, lse_spec = _attn_specs(s)

    def block(jb, sink_ref, q_ref, kc_ref, kp_ref, vc_ref, vp_ref, o_ref, l_ref):
        rows = slice(jb * BLOCK, (jb + 1) * BLOCK)
        valid = _band_mask_t(pl.program_id(0) == 0 if jb == 0 else False)
        k_both, v_both = _kv_window(kp_ref, kc_ref, jb), _kv_window(vp_ref, vc_ref, jb)
        for kh in range(N_KV_HEADS):
            qg = jnp.concatenate([_pair(q_ref, rows, kh, 0), _pair(q_ref, rows, kh, 1)], axis=0)
            keys = jnp.concatenate([_placed(k_both, kh, 0), _placed(k_both, kh, 1)], axis=0)
            st = _dot(keys, qg, NT)
            out = jnp.zeros((2 * BLOCK, LANES), F32)
            for b in range(2):
                sc = jnp.where(valid, st[b * 2 * BLOCK:(b + 1) * 2 * BLOCK, :], NEG_INF)
                sink = _sink_row(sink_ref, kh, b)
                m = jnp.maximum(jnp.max(sc, axis=0, keepdims=True), sink)
                p = jnp.exp(sc - m)
                den = jnp.sum(p, axis=0, keepdims=True) + jnp.exp(sink - m)
                out = out + _dot((p * (1.0 / den)).astype(BF16), _placed(v_both, kh, b), TN)
                lse = m + jnp.log(den)
                for r in range(2):
                    h = GROUP * kh + 2 * r + b
                    l_ref[h:h + 1, rows] = lse[:, r * BLOCK:(r + 1) * BLOCK]
            for r in range(2):
                j = 2 * kh + r
                o_ref[rows, j * LANES:(j + 1) * LANES] = out[r * BLOCK:(r + 1) * BLOCK, :].astype(BF16)

    def body(*refs):
        for jb in range(qb):
            block(jb, *refs)

    return _call(
        body, [sinks, q, k, k, v, v], name=name, grid=(s // (qb * BLOCK),),
        in_specs=[pl.BlockSpec(memory_space=pltpu.SMEM), cur(ATTN_WIDTH), cur(KV_WIDTH), prev(KV_WIDTH),
                  cur(KV_WIDTH), prev(KV_WIDTH)],
        out_specs=[cur(ATTN_WIDTH), lse_spec],
        out_shape=[jax.ShapeDtypeStruct((s, ATTN_WIDTH), BF16), jax.ShapeDtypeStruct((N_HEADS, s), F32)],
        exchanges=exchanges)


def _attn_bwd_t(sinks, q, do, lse, k, v, name, exchanges=()):
    s = q.shape[0]
    qb, cur, prev, lse_spec = _attn_specs(s)

    def block(jb, sink_ref, q_ref, do_ref, l_ref, kc_ref, kp_ref, vc_ref, vp_ref,
              dq_ref, dkc_ref, dkp_ref, dvc_ref, dvp_ref, dsink_ref):
        rows = slice(jb * BLOCK, (jb + 1) * BLOCK)
        valid = _band_mask_t(pl.program_id(0) == 0 if jb == 0 else False)
        k_both, v_both = _kv_window(kp_ref, kc_ref, jb), _kv_window(vp_ref, vc_ref, jb)
        lane = lax.broadcasted_iota(jnp.int32, (1, LANES), 1)
        col = lax.broadcasted_iota(jnp.int32, (1, 2 * BLOCK), 1)
        dk_heads, dv_heads = [], []
        for kh in range(N_KV_HEADS):
            qg = jnp.concatenate([_pair(q_ref, rows, kh, 0), _pair(q_ref, rows, kh, 1)], axis=0)
            dog = jnp.concatenate([_pair(do_ref, rows, kh, 0), _pair(do_ref, rows, kh, 1)], axis=0)
            k_placed = [_placed(k_both, kh, b) for b in range(2)]
            v_placed = [_placed(v_both, kh, b) for b in range(2)]
            st = _dot(jnp.concatenate(k_placed, axis=0), qg, NT)
            dpt = _dot(jnp.concatenate(v_placed, axis=0), dog, NT)
            dqg = jnp.zeros((2 * BLOCK, LANES), F32)
            dk_b, dv_b = [], []
            for b in range(2):
                part = slice(b * 2 * BLOCK, (b + 1) * 2 * BLOCK)
                heads = [GROUP * kh + 2 * r + b for r in range(2)]
                lse_row = jnp.concatenate([l_ref[h:h + 1, rows] for h in heads], axis=1)
                p = jnp.where(valid, jnp.exp(st[part, :] - lse_row), 0.0)
                dp = dpt[part, :]
                delta = jnp.sum(p * dp, axis=0, keepdims=True)
                sink_pull = jnp.exp(_sink_row(sink_ref, kh, b) - lse_row) * delta
                dsink_ref[heads[0]:heads[0] + 1, :] += -jnp.sum(jnp.where(col < BLOCK, sink_pull, 0.0))
                dsink_ref[heads[1]:heads[1] + 1, :] += -jnp.sum(jnp.where(col < BLOCK, 0.0, sink_pull))
                ds = (p * (dp - delta)).astype(BF16)
                dqg = dqg + _dot(ds, k_placed[b], TN)
                dk_b.append(_dot(ds, qg))
                dv_b.append(_dot(p.astype(BF16), dog))
            for parts, total in ((dk_b, dk_heads), (dv_b, dv_heads)):
                kept = jnp.where(lane < HEAD_DIM, parts[0], parts[1])
                total.append(kept + pltpu.roll(kept, HEAD_DIM, 1))
            for r in range(2):
                j = 2 * kh + r
                dq_ref[rows, j * LANES:(j + 1) * LANES] = dqg[r * BLOCK:(r + 1) * BLOCK, :] * Q_SCALE
        dk = jnp.where(lane < HEAD_DIM, dk_heads[0], dk_heads[1])
        dv = jnp.where(lane < HEAD_DIM, dv_heads[0], dv_heads[1])
        dkp_ref[rows, :] = dk[0:BLOCK, :]
        dkc_ref[rows, :] = dk[BLOCK:, :]
        dvp_ref[rows, :] = dv[0:BLOCK, :]
        dvc_ref[rows, :] = dv[BLOCK:, :]

    def body(*refs):
        @pl.when(pl.program_id(0) == 0)
        def _():
            refs[-1][...] = jnp.zeros_like(refs[-1])

        for jb in range(qb):
            block(jb, *refs)

    kv = jax.ShapeDtypeStruct((s, KV_WIDTH), F32)
    return _call(
        body, [sinks, q, do, lse, k, k, v, v], name=name, grid=(s // (qb * BLOCK),),
        in_specs=[pl.BlockSpec(memory_space=pltpu.SMEM), cur(ATTN_WIDTH), cur(ATTN_WIDTH), lse_spec,
                  cur(KV_WIDTH), prev(KV_WIDTH), cur(KV_WIDTH), prev(KV_WIDTH)],
        out_specs=[cur(ATTN_WIDTH), cur(KV_WIDTH), cur(KV_WIDTH), cur(KV_WIDTH), cur(KV_WIDTH),
                   pl.BlockSpec((8, LANES), lambda i: (0, 0))],
        out_shape=[jax.ShapeDtypeStruct((s, ATTN_WIDTH), F32), kv, kv, kv, kv, jax.ShapeDtypeStruct((8, LANES), F32)],
        exchanges=exchanges)


def _pool_counts(tile, ts):
    t = (tile * ts + lax.broadcasted_iota(jnp.int32, (ts, 1), 0) + 1).astype(F32)
    return [jnp.minimum(t, float(w)) for w in POOL_WINDOWS]


def _pooled(u, halo, counts, gi):
    cols = slice(gi * POOL_GROUP_WIDTH, (gi + 1) * POOL_GROUP_WIDTH)
    acc = jnp.concatenate([halo[:, cols], u[:, cols]], axis=0)
    shift = 1
    while shift < POOL_WINDOWS[gi]:
        acc = acc + pltpu.roll(acc, shift, 0)
        shift *= 2
    return acc[POOL_HALO:, :] / counts[gi] - u[:, cols]


def _fwd_out(ao, u, pool_w, psc, w_out, x, vec, name):
    s = x.shape[0]
    ts = _seq_tile(s, 2)
    hb = ts // POOL_HALO

    def body(ao_ref, u_ref, uh_ref, pw_ref, psc_ref, w_ref, x_ref, vec_ref, x1_ref, mix_ref):
        i = pl.program_id(0)
        u_t = u_ref[...]
        halo = jnp.where(i > 0, uh_ref[...], 0.0)
        counts = _pool_counts(i, ts)
        mix = _dot(ao_ref[...], w_ref[0:ATTN_WIDTH, :])
        for gi in range(len(POOL_WINDOWS)):
            cols = slice(gi * POOL_GROUP_WIDTH, (gi + 1) * POOL_GROUP_WIDTH)
            og = _dot(_pooled(u_t, halo, counts, gi).astype(BF16), pw_ref[gi]) * psc_ref[0:1, cols]
            mix = mix + _dot(og.astype(BF16), w_ref[ATTN_WIDTH + gi * POOL_GROUP_WIDTH:
                                                    ATTN_WIDTH + (gi + 1) * POOL_GROUP_WIDTH, :])
        mix_ref[...] = mix
        x1_ref[...] = x_ref[...] + vec_ref[0:1, :] * (mix * _rsqrt_ms(mix) * vec_ref[1:2, :])

    row = lambda w: pl.BlockSpec((ts, w), lambda i: (i, 0))
    return pl.pallas_call(
        body, name=name, grid=(s // ts,),
        in_specs=[row(ATTN_WIDTH), row(POOL_WIDTH),
                  pl.BlockSpec((POOL_HALO, POOL_WIDTH), lambda i: (jnp.maximum(i * hb - 1, 0), 0)),
                  _resident(pool_w.shape), _resident((8, POOL_WIDTH)), _resident((D_MODEL, D_MODEL)),
                  row(D_MODEL), _resident((8, D_MODEL))],
        out_specs=[row(D_MODEL), row(D_MODEL)],
        out_shape=[jax.ShapeDtypeStruct((s, D_MODEL), F32), jax.ShapeDtypeStruct((s, D_MODEL), F32)],
        compiler_params=_cparams(1),
    )(ao, u, u, pool_w, psc, w_out, x, vec)


def _ffn_fwd(x1, vec, w_gate_t, w_up_t, w_down, name, exchanges=(), target=None):
    s = x1.shape[0]
    ts = _seq_tile(s)

    def body(x_ref, *refs):
        if target is None:
            vec_ref, wg_ref, wu_ref, wd_ref, x2_ref, f_ref, gt_ref, up_ref, h2_ref = refs
        else:
            t_ref, vec_ref, wg_ref, wu_ref, wd_ref, x2_ref, f_ref, gt_ref, up_ref, h2_ref, loss_ref = refs
        xt = x_ref[...]
        h2 = (xt * _rsqrt_ms(xt) * vec_ref[0:1, :] * vec_ref[1:2, :] + vec_ref[2:3, :]).astype(BF16)
        h2_ref[...] = h2
        f = jnp.zeros((ts, D_MODEL), F32)
        for c in range(D_FF // FF_CHUNK):
            cols = slice(c * FF_CHUNK, (c + 1) * FF_CHUNK)
            g = _dot(h2, wg_ref[cols, :], NT)
            up = _dot(h2, wu_ref[cols, :], NT)
            act = ((g * jax.nn.sigmoid(g)) * up).astype(BF16)
            gt_ref[:, cols] = g.astype(BF16)
            up_ref[:, cols] = up.astype(BF16)
            f = f + _dot(act, wd_ref[cols, :])
        f_ref[...] = f
        x2 = xt + vec_ref[3:4, :] * (f * _rsqrt_ms(f) * vec_ref[4:5, :])
        if target is None:
            x2_ref[...] = x2
        else:
            @pl.when(pl.program_id(0) == 0)
            def _():
                loss_ref[...] = jnp.zeros_like(loss_ref)

            diff = x2 - t_ref[...]
            x2_ref[...] = diff / D_MODEL
            loss_ref[...] += 0.5 * jnp.sum(jnp.mean(diff * diff, axis=-1, keepdims=True))

    row = lambda w: pl.BlockSpec((ts, w), lambda i: (i, 0))
    wide = jax.ShapeDtypeStruct((s, D_FF), BF16)
    with_loss = target is not None
    return _call(
        body, [x1] + ([target] if with_loss else []) + [vec, w_gate_t, w_up_t, w_down], name=name, grid=(s // ts,),
        in_specs=[row(D_MODEL)] * (2 if with_loss else 1) + [
            _resident((8, D_MODEL)), _resident((D_FF, D_MODEL)), _resident((D_FF, D_MODEL)), _resident((D_FF, D_MODEL))],
        out_specs=[row(D_MODEL), row(D_MODEL), row(D_FF), row(D_FF), row(D_MODEL)] + (
            [pl.BlockSpec((8, LANES), lambda i: (0, 0))] if with_loss else []),
        out_shape=[jax.ShapeDtypeStruct((s, D_MODEL), F32), jax.ShapeDtypeStruct((s, D_MODEL), F32), wide, wide,
                   jax.ShapeDtypeStruct((s, D_MODEL), BF16)] + (
            [jax.ShapeDtypeStruct((8, LANES), F32)] if with_loss else []),
        exchanges=exchanges)


def _norm_bwd(dy_hat, x_hat, r):
    return r * (dy_hat - x_hat * jnp.mean(dy_hat * x_hat, axis=-1, keepdims=True))


def _accumulate_rows(ref, rows):
    for j, val in enumerate(rows):
        ref[j:j + 1, :] += val


def _ffn_bwd_act(dx2, f, gt, up, vec, w_down, name, exchanges=()):
    s = dx2.shape[0]
    ts = _seq_tile(s)

    def body(dx2_ref, f_ref, gt_ref, up_ref, vec_ref, wd_ref, dgt_ref, dup_ref, dwd_ref, cs_ref):
        @pl.when(pl.program_id(0) == 0)
        def _():
            cs_ref[...] = jnp.zeros_like(cs_ref)
            dwd_ref[...] = jnp.zeros_like(dwd_ref)

        dx2_t, f_t = dx2_ref[...], f_ref[...]
        gate, g_post = vec_ref[3:4, :], vec_ref[4:5, :]
        rf = _rsqrt_ms(f_t)
        f_hat = f_t * rf
        df = _norm_bwd(dx2_t * gate * g_post, f_hat, rf).astype(BF16)
        _accumulate_rows(cs_ref, [_colsum(dx2_t * (f_hat * g_post)), _colsum(dx2_t * gate * f_hat)])
        for c in range(D_FF // FF_CHUNK):
            cols = slice(c * FF_CHUNK, (c + 1) * FF_CHUNK)
            dact = _dot(df, wd_ref[cols, :], NT)
            g, u_t = gt_ref[:, cols].astype(F32), up_ref[:, cols].astype(F32)
            sg = jax.nn.sigmoid(g)
            silu = g * sg
            dgt_ref[:, cols] = (dact * u_t * (sg * (1.0 + g * (1.0 - sg)))).astype(BF16)
            dup_ref[:, cols] = (dact * silu).astype(BF16)
            dwd_ref[cols, :] += _dot((silu * u_t).astype(BF16), df, TN)

    row = lambda w: pl.BlockSpec((ts, w), lambda i: (i, 0))
    wide = jax.ShapeDtypeStruct((s, D_FF), BF16)
    return _call(
        body, [dx2, f, gt, up, vec, w_down], name=name, grid=(s // ts,),
        in_specs=[row(D_MODEL), row(D_MODEL), row(D_FF), row(D_FF), _resident((8, D_MODEL)), _resident((D_FF, D_MODEL))],
        out_specs=[row(D_FF), row(D_FF), _resident((D_FF, D_MODEL)), pl.BlockSpec((8, D_MODEL), lambda i: (0, 0))],
        out_shape=[wide, wide, jax.ShapeDtypeStruct((D_FF, D_MODEL), F32), jax.ShapeDtypeStruct((8, D_MODEL), F32)],
        exchanges=exchanges)


def _ffn_bwd_in(dx2, x1, dgt, dup, vec, w_gate_t, w_up_t, name, exchanges=()):
    s = x1.shape[0]
    ts = _seq_tile(s)

    def body(dx2_ref, x1_ref, dgt_ref, dup_ref, vec_ref, wg_ref, wu_ref, dx1_ref, cs_ref):
        @pl.when(pl.program_id(0) == 0)
        def _():
            cs_ref[...] = jnp.zeros_like(cs_ref)

        g_pre, one_scale = vec_ref[0:1, :], vec_ref[1:2, :]
        dh2 = _dot(dgt_ref[...], wg_ref[...]) + _dot(dup_ref[...], wu_ref[...])
        x1_t = x1_ref[...]
        r1 = _rsqrt_ms(x1_t)
        x_hat = x1_t * r1
        dx1_ref[...] = dx2_ref[...] + _norm_bwd(dh2 * g_pre * one_scale, x_hat, r1)
        _accumulate_rows(cs_ref, [_colsum(dh2), _colsum(dh2 * (x_hat * g_pre)), _colsum(dh2 * one_scale * x_hat)])

    row = lambda w: pl.BlockSpec((ts, w), lambda i: (i, 0))
    return _call(
        body, [dx2, x1, dgt, dup, vec, w_gate_t, w_up_t], name=name, grid=(s // ts,),
        in_specs=[row(D_MODEL), row(D_MODEL), row(D_FF), row(D_FF), _resident((8, D_MODEL)),
                  _resident((D_FF, D_MODEL)), _resident((D_FF, D_MODEL))],
        out_specs=[row(D_MODEL), pl.BlockSpec((8, D_MODEL), lambda i: (0, 0))],
        out_shape=[jax.ShapeDtypeStruct((s, D_MODEL), F32), jax.ShapeDtypeStruct((8, D_MODEL), F32)],
        exchanges=exchanges)


def _weight_grad(a, b, name, exchanges=()):
    s, m = a.shape
    n = b.shape[1]
    tk = min(s, 2 * SEQ_TILE)
    steps = s // tk

    def body(a_ref, b_ref, o_ref, acc_ref):
        i = pl.program_id(0)

        @pl.when(i == 0)
        def _():
            acc_ref[...] = jnp.zeros_like(acc_ref)

        b_t = b_ref[...]
        for c in range(m // FF_CHUNK):
            rows = slice(c * FF_CHUNK, (c + 1) * FF_CHUNK)
            acc_ref[rows, :] += _dot(a_ref[:, rows], b_t, TN)

        @pl.when(i == steps - 1)
        def _():
            o_ref[...] = acc_ref[...].astype(BF16)

    (grad,), ex_outs = _call(
        body, [a, b], name=name, grid=(steps,),
        in_specs=[pl.BlockSpec((tk, m), lambda i: (i, 0)), pl.BlockSpec((tk, n), lambda i: (i, 0))],
        out_specs=[_resident((m, n))],
        out_shape=[jax.ShapeDtypeStruct((m, n), BF16)],
        scratch_shapes=[pltpu.VMEM((m, n), F32)],
        exchanges=exchanges)
    return grad, ex_outs


def _out_bwd(dx1, mix, ao, u, pool_w, psc, w_out, vec, name, exchanges=()):
    s = dx1.shape[0]
    ts = _seq_tile(s)
    nt = s // ts
    hb = ts // POOL_HALO
    ng = len(POOL_WINDOWS)

    def body(dx1_ref, mix_ref, ao_ref, u_ref, uh_ref, pw_ref, psc_ref, w_ref, vec_ref,
             do_ref, du_ref, dw_ref, cs_ref, dpw_ref, dpsc_ref, carry_ref):
        i = pl.program_id(0)
        tile = nt - 1 - i

        @pl.when(i == 0)
        def _():
            dw_ref[...] = jnp.zeros_like(dw_ref)
            cs_ref[...] = jnp.zeros_like(cs_ref)
            dpw_ref[...] = jnp.zeros_like(dpw_ref)
            dpsc_ref[...] = jnp.zeros_like(dpsc_ref)
            carry_ref[...] = jnp.zeros_like(carry_ref)

        dx1_t, mix_t = dx1_ref[...], mix_ref[...]
        gate, g_post = vec_ref[0:1, :], vec_ref[1:2, :]
        rm = _rsqrt_ms(mix_t)
        m_hat = mix_t * rm
        dmix = _norm_bwd(dx1_t * gate * g_post, m_hat, rm).astype(BF16)
        _accumulate_rows(cs_ref, [_colsum(dx1_t * (m_hat * g_post)), _colsum(dx1_t * gate * m_hat)])
        do_ref[...] = _dot(dmix, w_ref[0:ATTN_WIDTH, :], NT).astype(BF16)
        dw_ref[0:ATTN_WIDTH, :] += _dot(ao_ref[...], dmix, TN)

        u_t = u_ref[...]
        halo = jnp.where(tile > 0, uh_ref[...], 0.0)
        counts = _pool_counts(tile, ts)
        for gi in range(ng):
            cols = slice(gi * POOL_GROUP_WIDTH, (gi + 1) * POOL_GROUP_WIDTH)
            wide_cols = slice(ATTN_WIDTH + gi * POOL_GROUP_WIDTH, ATTN_WIDTH + (gi + 1) * POOL_GROUP_WIDTH)
            scale = psc_ref[0:1, cols]
            pooled = _pooled(u_t, halo, counts, gi).astype(BF16)
            og = _dot(pooled, pw_ref[gi])
            dw_ref[wide_cols, :] += _dot((og * scale).astype(BF16), dmix, TN)
            d_out = _dot(dmix, w_ref[wide_cols, :], NT)
            dpsc_ref[0:1, cols] += _colsum(d_out * og)
            d_og = (d_out * scale).astype(BF16)
            dpw_ref[gi] += _dot(pooled, d_og, TN)
            d_pooled = _dot(d_og, pw_ref[gi], NT)
            spread = d_pooled / counts[gi]
            acc = jnp.concatenate([spread, carry_ref[:, cols]], axis=0)
            shift = 1
            while shift < POOL_WINDOWS[gi]:
                acc = acc + pltpu.roll(acc, ts + POOL_HALO - shift, 0)
                shift *= 2
            du_ref[:, cols] = acc[0:ts, :] - d_pooled
            carry_ref[:, cols] = spread[0:POOL_HALO, :]

    row = lambda w: pl.BlockSpec((ts, w), lambda i: (nt - 1 - i, 0))
    fixed = lambda shape: pl.BlockSpec(shape, lambda i: (0,) * len(shape))
    return _call(
        body, [dx1, mix, ao, u, u, pool_w, psc, w_out, vec], name=name, grid=(nt,),
        in_specs=[row(D_MODEL), row(D_MODEL), row(ATTN_WIDTH), row(POOL_WIDTH),
                  pl.BlockSpec((POOL_HALO, POOL_WIDTH), lambda i: (jnp.maximum((nt - 1 - i) * hb - 1, 0), 0)),
                  _resident(pool_w.shape), _resident((8, POOL_WIDTH)), _resident((D_MODEL, D_MODEL)),
                  _resident((8, D_MODEL))],
        out_specs=[row(ATTN_WIDTH), row(POOL_WIDTH), _resident((D_MODEL, D_MODEL)), fixed((8, D_MODEL)),
                   fixed(pool_w.shape), fixed((8, POOL_WIDTH))],
        out_shape=[jax.ShapeDtypeStruct((s, ATTN_WIDTH), BF16), jax.ShapeDtypeStruct((s, POOL_WIDTH), F32),
                   jax.ShapeDtypeStruct((D_MODEL, D_MODEL), F32),
                   jax.ShapeDtypeStruct((8, D_MODEL), F32), jax.ShapeDtypeStruct(pool_w.shape, F32),
                   jax.ShapeDtypeStruct((8, POOL_WIDTH), F32)],
        scratch_shapes=[pltpu.VMEM((POOL_HALO, POOL_WIDTH), F32)],
        exchanges=exchanges)


def _in_bwd(dq, dkc, dkp, dvc, dvp, du, x, dx1, vec, w_in_t, rot, name, exchanges=()):
    s = x.shape[0]
    ts = _seq_tile(s)
    nt = s // ts
    bpt = ts // BLOCK

    def body(dq_ref, dkc_ref, dkp_ref, dkn_ref, dvc_ref, dvp_ref, dvn_ref, du_ref, x_ref, dx1_ref, vec_ref, w_ref,
             cos_ref, lo_ref, hi_ref, dx_ref, dw_ref, cs_ref, db_ref):
        i = pl.program_id(0)

        @pl.when(i == 0)
        def _():
            dw_ref[...] = jnp.zeros_like(dw_ref)
            cs_ref[...] = jnp.zeros_like(cs_ref)
            db_ref[...] = jnp.zeros_like(db_ref)

        cos, lo, hi = cos_ref[...], lo_ref[...], hi_ref[...]

        def with_next_block(cur_ref, prev_ref, next_ref):
            nxt = jnp.where(i < nt - 1, next_ref[...], 0.0)
            later = nxt if bpt == 1 else jnp.concatenate([prev_ref[BLOCK:, :], nxt], axis=0)
            return cur_ref[...] + later

        pieces = [_rotate_bwd(dq_ref[:, j * LANES:(j + 1) * LANES], cos, lo, hi) for j in range(ATTN_WIDTH // LANES)]
        pieces.append(_rotate_bwd(with_next_block(dkc_ref, dkp_ref, dkn_ref), cos, lo, hi))
        pieces.append(with_next_block(dvc_ref, dvp_ref, dvn_ref))
        pieces.append(du_ref[...])
        dproj = jnp.concatenate(pieces, axis=1)
        db_ref[0:1, :] += _colsum(dproj)
        dproj_b = dproj.astype(BF16)
        dh = _dot(dproj_b, w_ref[...])

        xt = x_ref[...]
        g_pre, one_scale = vec_ref[0:1, :], vec_ref[1:2, :]
        r = _rsqrt_ms(xt)
        x_hat = xt * r
        dw_ref[...] += _dot(dproj_b, (x_hat * g_pre * one_scale + vec_ref[2:3, :]).astype(BF16), TN)
        dx_ref[...] = dx1_ref[...] + _norm_bwd(dh * g_pre * one_scale, x_hat, r)
        _accumulate_rows(cs_ref, [_colsum(dh), _colsum(dh * (x_hat * g_pre)), _colsum(dh * one_scale * x_hat)])

    row = lambda w: pl.BlockSpec((ts, w), lambda i: (i, 0))
    nxt = pl.BlockSpec((BLOCK, KV_WIDTH), lambda i: (jnp.minimum((i + 1) * bpt, s // BLOCK - 1), 0))
    fixed = lambda shape: pl.BlockSpec(shape, lambda i: (0,) * len(shape))
    return _call(
        body, [dq, dkc, dkp, dkp, dvc, dvp, dvp, du, x, dx1, vec, w_in_t, *rot], name=name, grid=(nt,),
        in_specs=[row(ATTN_WIDTH), row(KV_WIDTH), row(KV_WIDTH), nxt, row(KV_WIDTH), row(KV_WIDTH), nxt,
                  row(POOL_WIDTH), row(D_MODEL), row(D_MODEL), _resident((8, D_MODEL)), _resident((IN_WIDTH, D_MODEL)),
                  row(LANES), row(LANES), row(LANES)],
        out_specs=[row(D_MODEL), _resident((IN_WIDTH, D_MODEL)), fixed((8, D_MODEL)), fixed((8, IN_WIDTH))],
        out_shape=[jax.ShapeDtypeStruct((s, D_MODEL), F32), jax.ShapeDtypeStruct((IN_WIDTH, D_MODEL), F32),
                   jax.ShapeDtypeStruct((8, D_MODEL), F32), jax.ShapeDtypeStruct((8, IN_WIDTH), F32)],
        exchanges=exchanges)


def _row_tile(rows):
    for t in (512, 352, 256, 176, 160, 128, 64, 32, 16, 8):
        if rows % t == 0:
            return t
    raise ValueError(f"no row tile for {rows} rows")


def _sum_parts(own, others, name):
    r, c = own.shape
    n_other = others.shape[0]
    tr = _row_tile(r)

    def body(own_ref, oth_ref, o_ref):
        acc = own_ref[...].astype(F32)
        for k in range(n_other):
            acc = acc + oth_ref[k].astype(F32)
        o_ref[...] = acc

    return pl.pallas_call(
        body, name=name, grid=(r // tr,),
        in_specs=[pl.BlockSpec((tr, c), lambda i: (i, 0)), pl.BlockSpec((n_other, tr, c), lambda i: (0, i, 0))],
        out_specs=pl.BlockSpec((tr, c), lambda i: (i, 0)),
        out_shape=jax.ShapeDtypeStruct((r, c), F32),
        compiler_params=_cparams(1),
    )(own, others)


def _adamw(w, g, m, v, name):
    r, c = w.shape
    tr = _row_tile(r)

    def body(w_ref, g_ref, m_ref, v_ref, d_ref, nm_ref, nv_ref):
        g_t = g_ref[...]
        m_new = ADAM_B1 * m_ref[...] + (1.0 - ADAM_B1) * g_t
        v_new = ADAM_B2 * v_ref[...] + (1.0 - ADAM_B2) * (g_t * g_t)
        m_hat = m_new / (1.0 - ADAM_B1 ** ADAM_STEP)
        v_hat = v_new / (1.0 - ADAM_B2 ** ADAM_STEP)
        d_ref[...] = -ADAM_LR * (m_hat / (jnp.sqrt(v_hat) + ADAM_EPS) + ADAM_WD * w_ref[...])
        nm_ref[...] = m_new
        nv_ref[...] = v_new

    spec = pl.BlockSpec((tr, c), lambda i: (i, 0))
    shape = jax.ShapeDtypeStruct((r, c), F32)
    return pl.pallas_call(
        body, name=name, grid=(r // tr,), in_specs=[spec] * 4, out_specs=[spec] * 3, out_shape=[shape] * 3,
        compiler_params=_cparams(1),
    )(w, g, m, v)


SMALL_ROWS = 160


def _pack_small(dmod, gains, b_in, pool_scale, sinks, pool_w):
    pad = lambda a, width: jnp.pad(a, ((0, 0), (0, width - a.shape[1])))
    parts = [dmod.reshape(DEPTH * N_MOD, D_MODEL), gains.reshape(4 * DEPTH, D_MODEL),
             pad(b_in, 2 * D_MODEL).reshape(2 * DEPTH, D_MODEL), pool_scale.reshape(1, D_MODEL),
             pad(sinks.reshape(1, DEPTH * N_HEADS), D_MODEL), pool_w.reshape(-1, D_MODEL)]
    packed = jnp.concatenate(parts, axis=0)
    return jnp.pad(packed, ((0, SMALL_ROWS - packed.shape[0]), (0, 0)))


def _unpack_small(p):
    r = DEPTH * N_MOD
    dmod = p[0:r].reshape(DEPTH, N_MOD * D_MODEL)
    gains = p[r:r + 4 * DEPTH].reshape(4, DEPTH, D_MODEL)
    r += 4 * DEPTH
    b_in = p[r:r + 2 * DEPTH].reshape(DEPTH, 2 * D_MODEL)[:, :IN_WIDTH]
    r += 2 * DEPTH
    pool_scale = p[r].reshape(DEPTH, POOL_WIDTH)
    sinks = p[r + 1, :DEPTH * N_HEADS].reshape(DEPTH, N_HEADS)
    pool_w = p[r + 2:r + 2 + 128].reshape(DEPTH, len(POOL_WINDOWS), POOL_GROUP_WIDTH, POOL_GROUP_WIDTH)
    return dmod, gains, b_in, pool_scale, sinks, pool_w


LAYER_ROWS = 80


def _pack_layer(dmod, gains, b_in, pool_scale, sinks, pool_w):
    misc = jnp.concatenate([pool_scale, sinks, jnp.zeros((D_MODEL - POOL_WIDTH - N_HEADS,), F32)])
    parts = [dmod.reshape(N_MOD, D_MODEL), gains, jnp.pad(b_in, (0, 2 * D_MODEL - IN_WIDTH)).reshape(2, D_MODEL),
             misc.reshape(1, D_MODEL), pool_w.reshape(-1, D_MODEL)]
    packed = jnp.concatenate(parts, axis=0)
    return jnp.pad(packed, ((0, LAYER_ROWS - packed.shape[0]), (0, 0)))


def _unpack_layer(p):
    r = N_MOD + 4
    b_in = p[r:r + 2].reshape(2 * D_MODEL)[:IN_WIDTH]
    pool_w = p[r + 3:r + 3 + 64].reshape(len(POOL_WINDOWS), POOL_GROUP_WIDTH, POOL_GROUP_WIDTH)
    return (p[0:N_MOD].reshape(N_MOD * D_MODEL), p[N_MOD:r], b_in, p[r + 2, :POOL_WIDTH],
            p[r + 2, POOL_WIDTH:POOL_WIDTH + N_HEADS], pool_w)


def _rotary_tables(positions):
    inv_freq = ROPE_THETA ** (-jnp.arange(0, ROT_DIM, 2, dtype=F32) / ROT_DIM)
    half = ROT_DIM // 2
    head_freq = jnp.concatenate([inv_freq, inv_freq, jnp.zeros((HEAD_DIM - ROT_DIM,), F32)])
    lane_freq = jnp.concatenate([head_freq, head_freq])
    ang = positions.reshape(-1).astype(F32)[:, None] * lane_freq[None, :]
    cos, sin = jnp.cos(ang), jnp.sin(ang)
    in_head = lax.broadcasted_iota(jnp.int32, (1, LANES), 1) % HEAD_DIM
    return cos, jnp.where(in_head < half, -sin, 0.0), jnp.where((in_head >= half) & (in_head < ROT_DIM), sin, 0.0)


def kernel(x, c, positions, ada_w, ada_b, w_in, b_in, sinks, pool_w, pool_scale, w_out, w_gate, w_up, w_down, g_pre_mix, g_post_mix, g_pre_ffn, g_post_ffn, loss_target, m_ada_w, m_ada_b, m_w_in, m_b_in, m_sinks, m_pool_w, m_pool_scale, m_w_out, m_w_gate, m_w_up, m_w_down, m_g_pre_mix, m_g_post_mix, m_g_pre_ffn, m_g_post_ffn, v_ada_w, v_ada_b, v_w_in, v_b_in, v_sinks, v_pool_w, v_pool_scale, v_w_out, v_w_gate, v_w_up, v_w_down, v_g_pre_mix, v_g_post_mix, v_g_pre_ffn, v_g_post_ffn):
    me = _index(_mesh_place())
    x0 = x[0]
    target = loss_target[0]
    rot = _rotary_tables(positions)
    ada_cols = ada_w.shape[2]

    shards = []
    for l in range(DEPTH):
        shards += [w_in[l].T.astype(BF16), w_out[l].astype(BF16), w_gate[l].T.astype(BF16), w_up[l].T.astype(BF16),
                   w_down[l].astype(BF16)]
    whole = lambda gathered: [g.reshape(-1, D_MODEL) for g in gathered]
    full = [None] * (5 * DEPTH)
    first = _exchanges_alone([_Gather(shards[0:2] + [jnp.broadcast_to(c, (8, D_MODEL))])], "gather_first")[0]
    full[0:2] = whole(first[0:2])

    mod_part, c_act = _mod_fwd(first[2][:, 0, :], ada_w)
    mod_all = _allgather_vmem(mod_part.reshape(DEPTH * N_DEV, ada_cols), "gather_mod")
    mod_mine = lax.dynamic_slice_in_dim(mod_all.reshape(N_DEV, DEPTH, N_DEV, ada_cols), me, 1, axis=2)[:, :, 0, :]
    mod = jnp.transpose(mod_mine, (1, 0, 2)).reshape(DEPTH, N_MOD * D_MODEL) + ada_b
    mod = mod.reshape(DEPTH, N_MOD, D_MODEL)

    hosted = {("fwd_in", 0): (_Gather(shards[4:5]), [4]), ("attn_fwd", 0): (_Gather(shards[2:4]), [2, 3]),
              ("ffn_fwd", 0): (_Gather(shards[5:10]), [5, 6, 7, 8, 9])}

    def take(kind, l, ex_outs):
        if (kind, l) in hosted:
            for slot, g in zip(hosted[kind, l][1], whole(ex_outs[0])):
                full[slot] = g

    def beside(kind, l):
        return [hosted[kind, l][0]] if (kind, l) in hosted else []

    pool_w_b = pool_w.astype(BF16)

    saved = []
    xl = x0
    for l in range(DEPTH):
        vec_in = _rows(g_pre_mix[l], 1.0 + mod[l, 1], mod[l, 0], width=D_MODEL)
        vec_out = _rows(mod[l, 2], g_post_mix[l], width=D_MODEL)
        vec_ffn = _rows(g_pre_ffn[l], 1.0 + mod[l, 4], mod[l, 3], mod[l, 5], g_post_ffn[l], width=D_MODEL)
        psc = _rows(pool_scale[l], width=POOL_WIDTH)
        (q, k, v, u), ex = _fwd_in(xl, vec_in, full[5 * l], _rows(b_in[l], width=IN_WIDTH), rot, f"fwd_in_{l}",
                                   beside("fwd_in", l))
        take("fwd_in", l, ex)
        (ao, lse), ex = _attn_fwd_t(sinks[l], q, k, v, f"attn_fwd_{l}", beside("attn_fwd", l))
        take("attn_fwd", l, ex)
        x1, mix = _fwd_out(ao, u, pool_w_b[l], psc, full[5 * l + 1], xl, vec_out, f"fwd_out_{l}")
        (x2, f, gt, up, h2, *loss_part), ex = _ffn_fwd(
            x1, vec_ffn, full[5 * l + 2], full[5 * l + 3], full[5 * l + 4], f"ffn_fwd_{l}", beside("ffn_fwd", l),
            target=target if l == DEPTH - 1 else None)
        take("ffn_fwd", l, ex)
        saved.append((xl, q, k, v, u, ao, lse, x1, mix, f, gt, up, h2, vec_in, vec_out, vec_ffn, psc))
        xl = x2

    dx = xl
    loss = lax.psum(loss_part[0][0, 0], ("x", "y", "c"))

    stacks = [None] * (5 * DEPTH)
    others = [None] * (5 * DEPTH)
    small_all = [None] * DEPTH
    ready_stacks, ready_small = [], []

    def leaving():
        exs, notes = [], []
        if ready_stacks:
            exs.append(_Scatter([stacks[a] for a in ready_stacks]))
            notes.append(("stacks", list(ready_stacks)))
            ready_stacks.clear()
        if ready_small:
            exs.append(_Gather([ready_small[0][1]]))
            notes.append(("small", ready_small[0][0]))
            ready_small.clear()
        return exs, notes

    def arrived(notes, ex_outs):
        for (kind, what), outs in zip(notes, ex_outs):
            if kind == "stacks":
                for a, o in zip(what, outs):
                    others[a] = o
            else:
                small_all[what] = outs[0]

    def grad_ready(a, grad):
        stacks[a] = grad.reshape(N_DEV, -1, D_MODEL)
        ready_stacks.append(a)

    def hosting(kernel_fn, *args):
        exs, notes = leaving()
        outs, ex_outs = kernel_fn(*args, exs)
        arrived(notes, ex_outs)
        return outs

    for l in reversed(range(DEPTH)):
        w_in_t, w_out_f, w_gate_t, w_up_t, w_down_f = full[5 * l:5 * l + 5]
        xin, q, k, v, u, ao, lse, x1, mix, f, gt, up, h2, vec_in, vec_out, vec_ffn, psc = saved[l]
        dgt, dup, d_w_down, cs_a = hosting(_ffn_bwd_act, dx, f, gt, up, vec_ffn, w_down_f, f"ffn_bwd_act_{l}")
        grad_ready(5 * l + 4, d_w_down.astype(BF16))
        dx1, cs_f = hosting(_ffn_bwd_in, dx, x1, dgt, dup, vec_ffn, w_gate_t, w_up_t, f"ffn_bwd_in_{l}")
        grad_ready(5 * l + 2, hosting(_weight_grad, dgt, h2, f"grad_w_gate_{l}"))
        grad_ready(5 * l + 3, hosting(_weight_grad, dup, h2, f"grad_w_up_{l}"))
        do, du, d_w_out, cs_o, dpw, dpsc = hosting(_out_bwd, dx1, mix, ao, u, pool_w_b[l], psc, w_out_f, vec_out,
                                                   f"out_bwd_{l}")
        grad_ready(5 * l + 1, d_w_out.astype(BF16))
        dq, dkc, dkp, dvc, dvp, dsk = hosting(_attn_bwd_t, sinks[l], q, do, lse, k, v, f"attn_bwd_{l}")
        dx, d_w_in_t, cs_i, db = hosting(_in_bwd, dq, dkc, dkp, dvc, dvp, du, xin, dx1, vec_in, w_in_t, rot,
                                         f"in_bwd_{l}")
        grad_ready(5 * l, d_w_in_t.astype(BF16))
        d_mod = jnp.concatenate([cs_i[0], cs_i[1], cs_o[0], cs_f[0], cs_f[1], cs_a[0]])
        d_gain = jnp.stack([cs_i[2], cs_o[1], cs_f[2], cs_a[1]])
        ready_small.append((l, _pack_layer(d_mod, d_gain, db[0], dpsc[0], dsk[:, 0], dpw)))
    grad_x = dx[None]
    exs, notes = leaving()
    arrived(notes, _exchanges_alone(exs, "exchange_last"))

    layer_sums = [_unpack_layer(_sum_parts(small_all[l][0], small_all[l][1:], f"sum_small_{l}")) for l in range(DEPTH)]
    g_ada_b, g_gains, g_b_in, g_pool_scale, g_sinks, g_pool_w = (
        jnp.stack([layer_sums[l][j] for l in range(DEPTH)], axis=1 if j == 1 else 0) for j in range(6))
    small_sum = _pack_small(g_ada_b, g_gains, g_b_in, g_pool_scale, g_sinks, g_pool_w)
    gains = jnp.stack([g_pre_mix, g_post_mix, g_pre_ffn, g_post_ffn])
    m_gains = jnp.stack([m_g_pre_mix, m_g_post_mix, m_g_pre_ffn, m_g_post_ffn])
    v_gains = jnp.stack([v_g_pre_mix, v_g_post_mix, v_g_pre_ffn, v_g_post_ffn])
    small_step = _adamw(_pack_small(ada_b, gains, b_in, pool_scale, sinks, pool_w), small_sum,
                        _pack_small(m_ada_b, m_gains, m_b_in, m_pool_scale, m_sinks, m_pool_w),
                        _pack_small(v_ada_b, v_gains, v_b_in, v_pool_scale, v_sinks, v_pool_w), "adamw_small")
    small_out = [_unpack_small(p) for p in small_step]

    dmod_all = jnp.stack([small_all[l][:, 0:N_MOD, :].reshape(N_DEV, N_DEV, ada_cols) for l in range(DEPTH)])
    dmod_cols = lax.dynamic_slice_in_dim(dmod_all, me, 1, axis=2)[:, :, 0, :]
    g_ada_w = _ada_grad(c_act, dmod_cols)

    summed = []
    for a, (stack, oth) in enumerate(zip(stacks, others)):
        own = lax.dynamic_index_in_dim(stack, me, axis=0, keepdims=False)
        summed.append(_sum_parts(own, oth, f"sum_grads_{a}"))
    per_kind = [jnp.stack([summed[5 * l + j] for l in range(DEPTH)]) for j in range(5)]
    g_w_in, g_w_out, g_w_gate, g_w_up, g_w_down = per_kind
    g_w_in, g_w_gate, g_w_up = (jnp.transpose(g, (0, 2, 1)) for g in (g_w_in, g_w_gate, g_w_up))

    def step(w, g, m, v, name):
        flat = lambda a: a.reshape(-1, a.shape[-1])
        return [o.reshape(w.shape) for o in _adamw(flat(w), flat(g), flat(m), flat(v), name)]

    big = {
        "ada_w": (g_ada_w, step(ada_w, g_ada_w, m_ada_w, v_ada_w, "adamw_ada_w")),
        "w_in": (g_w_in, step(w_in, g_w_in, m_w_in, v_w_in, "adamw_w_in")),
        "w_out": (g_w_out, step(w_out, g_w_out, m_w_out, v_w_out, "adamw_w_out")),
        "w_gate": (g_w_gate, step(w_gate, g_w_gate, m_w_gate, v_w_gate, "adamw_w_gate")),
        "w_up": (g_w_up, step(w_up, g_w_up, m_w_up, v_w_up, "adamw_w_up")),
        "w_down": (g_w_down, step(w_down, g_w_down, m_w_down, v_w_down, "adamw_w_down")),
    }

    def ordered(pick_big, pick_small):
        ada_b_, gains_, b_in_, pool_scale_, sinks_, pool_w_ = pick_small
        return [pick_big("ada_w"), ada_b_, pick_big("w_in"), b_in_, sinks_, pool_w_, pool_scale_, pick_big("w_out"),
                pick_big("w_gate"), pick_big("w_up"), pick_big("w_down"), gains_[0], gains_[1], gains_[2], gains_[3]]

    grads = ordered(lambda n: big[n][0], (g_ada_b, g_gains, g_b_in, g_pool_scale, g_sinks, g_pool_w))
    deltas = ordered(lambda n: big[n][1][0], small_out[0])
    new_m = ordered(lambda n: big[n][1][1], small_out[1])
    new_v = ordered(lambda n: big[n][1][2], small_out[2])
    return (loss, grad_x, *grads, *deltas, *new_m, *new_v)
```

```python
import jax
import jax.numpy as jnp
from jax import lax
from jax.experimental import pallas as pl
from jax.experimental.pallas import tpu as pltpu

F32 = jnp.float32
BF16 = jnp.bfloat16

N_DEV = 8
DEPTH = 2
D_MODEL = 1024
HEAD_DIM = 64
N_HEADS = 8
N_KV_HEADS = 2
GROUP = N_HEADS // N_KV_HEADS
ATTN_WIDTH = N_HEADS * HEAD_DIM
KV_WIDTH = N_KV_HEADS * HEAD_DIM
POOL_WIDTH = 512
POOL_WINDOWS = (2, 4, 8, 16)
POOL_GROUP_WIDTH = 128
POOL_HALO = 16
IN_WIDTH = ATTN_WIDTH + 2 * KV_WIDTH + POOL_WIDTH
D_FF = 2816
N_MOD = 6
BLOCK = 128
ROT_DIM = 16
ROPE_THETA = 500000.0
EPS = 1e-6
NEG_INF = -1e30
Q_SCALE = HEAD_DIM ** -0.5

ADAM_LR = 0.001
ADAM_B1 = 0.9
ADAM_B2 = 0.999
ADAM_EPS = 1e-08
ADAM_WD = 0.01
ADAM_STEP = 10

LANES = 128
SEQ_TILE = 512
ATTN_BLOCKS = 2
FF_CHUNK = 256
VMEM_LIMIT = 56 * 1024 * 1024
MESH = pl.DeviceIdType.MESH

NT = (((1,), (1,)), ((), ()))
TN = (((0,), (0,)), ((), ()))


def _dot(a, b, dims=None):
    if dims is None:
        return jnp.dot(a, b, preferred_element_type=F32)
    return lax.dot_general(a, b, dims, preferred_element_type=F32)


def _cparams(n_axes):
    return pltpu.CompilerParams(dimension_semantics=("arbitrary",) * n_axes, vmem_limit_bytes=VMEM_LIMIT)


def _resident(shape):
    zeros = (0,) * len(shape)
    return pl.BlockSpec(shape, lambda *_: zeros, pipeline_mode=pl.Buffered(1))


def _rows(*vectors, width):
    rows = [jnp.reshape(v, (1, width)).astype(F32) for v in vectors]
    rows.append(jnp.zeros((8 - len(rows), width), F32))
    return jnp.concatenate(rows, axis=0)


def _rsqrt_ms(x):
    return lax.rsqrt(jnp.mean(x * x, axis=-1, keepdims=True) + EPS)


def _colsum(x):
    return jnp.sum(x, axis=0, keepdims=True)


def _seq_tile(s, tiles=1):
    return min(s, tiles * SEQ_TILE)


def _mesh_place():
    x, y, c = lax.axis_index("x"), lax.axis_index("y"), lax.axis_index("c")
    return x, y, c


def _flip(place, k):
    x, y, c = place
    return (1 - x if k & 4 else x, 1 - y if k & 2 else y, 1 - c if k & 1 else c)


def _index(place):
    x, y, c = place
    return 4 * x + 2 * y + c


def _allgather_vmem(block, name):
    r, c = block.shape

    def body(x_ref, out_ref, send_sems, recv_sems, local_sem):
        me = _mesh_place()
        mine = pltpu.make_async_copy(x_ref, out_ref.at[_index(me)], local_sem)
        mine.start()

        def copy(k):
            return pltpu.make_async_remote_copy(
                src_ref=x_ref, dst_ref=out_ref.at[_index(me)], send_sem=send_sems.at[k - 1], recv_sem=recv_sems.at[k - 1],
                device_id=_flip(me, k), device_id_type=MESH)

        def arrival(k):
            return pltpu.make_async_remote_copy(
                src_ref=x_ref, dst_ref=out_ref.at[_index(_flip(me, k))], send_sem=send_sems.at[k - 1],
                recv_sem=recv_sems.at[k - 1], device_id=_flip(me, k), device_id_type=MESH)

        for k in range(1, N_DEV):
            copy(k).start()
        for k in range(1, N_DEV):
            arrival(k).wait_recv()
        for k in range(1, N_DEV):
            copy(k).wait_send()
        mine.wait()

    return pl.pallas_call(
        body, name=name,
        out_shape=jax.ShapeDtypeStruct((N_DEV, r, c), block.dtype),
        in_specs=[pl.BlockSpec(memory_space=pltpu.VMEM)],
        out_specs=pl.BlockSpec(memory_space=pltpu.VMEM),
        scratch_shapes=[pltpu.SemaphoreType.DMA((N_DEV - 1,)), pltpu.SemaphoreType.DMA((N_DEV - 1,)),
                        pltpu.SemaphoreType.DMA],
    )(block)


class _Gather:
    def __init__(self, shards):
        n = len(shards)
        self.operands = list(shards)
        self.out_shape = [jax.ShapeDtypeStruct((N_DEV,) + s.shape, s.dtype) for s in shards]
        self.scratch = [pltpu.SemaphoreType.DMA((N_DEV - 1, n)), pltpu.SemaphoreType.DMA((N_DEV - 1, n)),
                        pltpu.SemaphoreType.DMA((n,))]

    def _copies(self, x_refs, out_refs, sems):
        send_sems, recv_sems, local_sems = sems
        n = len(x_refs)
        x, y, c = _mesh_place()
        me, sibling = (x, y, c), (x, y, 1 - c)
        chips = [(1 - x, y), (x, 1 - y), (1 - x, 1 - y)]

        def copy(k, a, block, to, from_input=False):
            rows = out_refs[a].at[_index(block)]
            return pltpu.make_async_remote_copy(
                src_ref=x_refs[a] if from_input else rows, dst_ref=rows,
                send_sem=send_sems.at[k, a], recv_sem=recv_sems.at[k, a], device_id=to, device_id_type=MESH)

        mine = [pltpu.make_async_copy(x_refs[a], out_refs[a].at[_index(me)], local_sems.at[a]) for a in range(n)]
        first = [copy(0, a, me, sibling, from_input=True) for a in range(n)]
        first += [copy(1 + j, a, me, (*chip, c), from_input=True) for j, chip in enumerate(chips) for a in range(n)]
        over_ici = [copy(1 + j, a, (*chip, c), me) for j, chip in enumerate(chips) for a in range(n)]
        passed = [copy(4 + j, a, (*chip, c), sibling) for j, chip in enumerate(chips) for a in range(n)]
        from_sibling = [copy(0, a, sibling, me) for a in range(n)]
        from_sibling += [copy(4 + j, a, (*chip, 1 - c), me) for j, chip in enumerate(chips) for a in range(n)]
        return mine, first, over_ici, passed, from_sibling

    def begin(self, x_refs, out_refs, sems):
        mine, first, _, _, _ = self._copies(x_refs, out_refs, sems)
        for cp in mine + first:
            cp.start()

    def middle(self, x_refs, out_refs, sems):
        _, _, over_ici, passed, _ = self._copies(x_refs, out_refs, sems)
        for arrived, onward in zip(over_ici, passed):
            arrived.wait_recv()
            onward.start()

    def end(self, x_refs, out_refs, sems):
        mine, first, _, passed, from_sibling = self._copies(x_refs, out_refs, sems)
        for cp in from_sibling:
            cp.wait_recv()
        for cp in first + passed:
            cp.wait_send()
        for cp in mine:
            cp.wait()


class _Scatter:
    OWN = N_DEV - 1

    def __init__(self, stacks):
        n = len(stacks)
        self.operands = list(stacks)
        self.out_shape = [jax.ShapeDtypeStruct(s.shape, s.dtype) for s in stacks]
        self.scratch = [pltpu.SemaphoreType.DMA((N_DEV - 1, n)), pltpu.SemaphoreType.DMA((N_DEV - 1, n)),
                        pltpu.SemaphoreType.DMA((n,))]

    def _copies(self, g_refs, out_refs, sems):
        send_sems, recv_sems, local_sems = sems
        me = _mesh_place()

        def copy(k, a):
            peer = _flip(me, k)
            return pltpu.make_async_remote_copy(
                src_ref=g_refs[a].at[_index(peer)], dst_ref=out_refs[a].at[k - 1],
                send_sem=send_sems.at[k - 1, a], recv_sem=recv_sems.at[k - 1, a], device_id=peer, device_id_type=MESH)

        mine = [pltpu.make_async_copy(g_refs[a].at[_index(me)], out_refs[a].at[self.OWN], local_sems.at[a])
                for a in range(len(g_refs))]
        return mine, [copy(k, a) for k in range(1, N_DEV) for a in range(len(g_refs))]

    def begin(self, g_refs, out_refs, sems):
        mine, copies = self._copies(g_refs, out_refs, sems)
        for cp in mine + copies:
            cp.start()

    def middle(self, g_refs, out_refs, sems):
        pass

    def end(self, g_refs, out_refs, sems):
        mine, copies = self._copies(g_refs, out_refs, sems)
        for cp in copies:
            cp.wait_recv()
        for cp in copies:
            cp.wait_send()
        for cp in mine:
            cp.wait()


def _call(body, args, *, name, grid, in_specs, out_specs, out_shape, scratch_shapes=(), exchanges=()):
    if not exchanges:
        outs = pl.pallas_call(body, name=name, grid=grid, in_specs=in_specs, out_specs=out_specs, out_shape=out_shape,
                              scratch_shapes=list(scratch_shapes), compiler_params=_cparams(len(grid)))(*args)
        return outs, []
    (steps,) = grid
    n_in, n_out, n_scr = len(in_specs), len(out_specs), len(scratch_shapes)
    ex_in = [len(ex.operands) for ex in exchanges]
    ex_out = [len(ex.out_shape) for ex in exchanges]
    ex_scr = [len(ex.scratch) for ex in exchanges]

    def split(refs, counts):
        parts, pos = [], 0
        for cnt in counts:
            parts.append(refs[pos:pos + cnt])
            pos += cnt
        return parts

    def wrapped(*refs):
        ins, xin, outs, xout, scr, xscr = split(refs, [n_in, sum(ex_in), n_out, sum(ex_out), n_scr, sum(ex_scr)])
        bound = list(zip(exchanges, split(xin, ex_in), split(xout, ex_out), split(xscr, ex_scr)))
        step = pl.program_id(0)

        def phase(method, at):
            @pl.when(step == at)
            def _():
                for ex, i_refs, o_refs, sems in bound:
                    getattr(ex, method)(i_refs, o_refs, sems)

        phase("begin", 0)
        phase("middle", (3 * steps) // 4)
        body(*ins, *outs, *scr)
        phase("end", steps - 1)

    any_spec = pl.BlockSpec(memory_space=pl.ANY)
    results = pl.pallas_call(
        wrapped, name=name, grid=grid,
        in_specs=list(in_specs) + [any_spec] * sum(ex_in),
        out_specs=list(out_specs) + [any_spec] * sum(ex_out),
        out_shape=list(out_shape) + [s for ex in exchanges for s in ex.out_shape],
        scratch_shapes=list(scratch_shapes) + [s for ex in exchanges for s in ex.scratch],
        compiler_params=_cparams(1),
    )(*args, *[a for ex in exchanges for a in ex.operands])
    return results[:n_out], split(results[n_out:], ex_out)


def _exchanges_alone(exchanges, name):
    def body(flag_ref):
        flag_ref[...] = jnp.zeros_like(flag_ref)

    _, ex_outs = _call(body, [], name=name, grid=(1,), in_specs=[], out_specs=[pl.BlockSpec((8, LANES), lambda i: (0, 0))],
                       out_shape=[jax.ShapeDtypeStruct((8, LANES), F32)], exchanges=exchanges)
    return ex_outs


def _mod_fwd(c_all, ada_w):
    cols = ada_w.shape[2]

    def body(c_ref, w_ref, mp_ref, act_ref):
        c = c_ref[...]
        act = c * jax.nn.sigmoid(c)
        act_ref[...] = act
        mp_ref[0] = _dot(act.astype(BF16), w_ref[0].astype(BF16))

    return pl.pallas_call(
        body, name="mod_fwd", grid=(DEPTH,),
        in_specs=[pl.BlockSpec((N_DEV, D_MODEL), lambda l: (0, 0)),
                  pl.BlockSpec((1, D_MODEL, cols), lambda l: (l, 0, 0))],
        out_specs=[pl.BlockSpec((1, N_DEV, cols), lambda l: (l, 0, 0)),
                   pl.BlockSpec((N_DEV, D_MODEL), lambda l: (0, 0))],
        out_shape=[jax.ShapeDtypeStruct((DEPTH, N_DEV, cols), F32), jax.ShapeDtypeStruct((N_DEV, D_MODEL), F32)],
        compiler_params=_cparams(1),
    )(c_all, ada_w)


def _ada_grad(c_act, dmod_cols):
    cols = dmod_cols.shape[2]

    def body(act_ref, dm_ref, g_ref):
        g_ref[0] = _dot(act_ref[...].astype(BF16), dm_ref[0].astype(BF16), TN)

    return pl.pallas_call(
        body, name="ada_grad", grid=(DEPTH,),
        in_specs=[pl.BlockSpec((N_DEV, D_MODEL), lambda l: (0, 0)),
                  pl.BlockSpec((1, N_DEV, cols), lambda l: (l, 0, 0))],
        out_specs=pl.BlockSpec((1, D_MODEL, cols), lambda l: (l, 0, 0)),
        out_shape=jax.ShapeDtypeStruct((DEPTH, D_MODEL, cols), F32),
        compiler_params=_cparams(1),
    )(c_act, dmod_cols)


def _rotate(t, cos, sin_lo, sin_hi):
    return t * cos + pltpu.roll(t, LANES - 8, 1) * sin_lo + pltpu.roll(t, 8, 1) * sin_hi


def _rotate_bwd(t, cos, sin_lo, sin_hi):
    return t * cos + pltpu.roll(t * sin_lo, 8, 1) + pltpu.roll(t * sin_hi, LANES - 8, 1)


def _fwd_in(x, vec, w_in_t, b_in, rot, name, exchanges=()):
    s = x.shape[0]
    ts = _seq_tile(s, 2)

    def body(x_ref, vec_ref, w_ref, b_ref, cos_ref, lo_ref, hi_ref, q_ref, k_ref, v_ref, u_ref):
        xt = x_ref[...]
        h = xt * _rsqrt_ms(xt) * vec_ref[0:1, :] * vec_ref[1:2, :] + vec_ref[2:3, :]
        proj = _dot(h.astype(BF16), w_ref[...], NT) + b_ref[0:1, :]
        cos, lo, hi = cos_ref[...], lo_ref[...], hi_ref[...]
        for j in range(ATTN_WIDTH // LANES):
            q_ref[:, j * LANES:(j + 1) * LANES] = (
                _rotate(proj[:, j * LANES:(j + 1) * LANES], cos, lo, hi) * Q_SCALE).astype(BF16)
        k_ref[...] = _rotate(proj[:, ATTN_WIDTH:ATTN_WIDTH + KV_WIDTH], cos, lo, hi).astype(BF16)
        v_ref[...] = proj[:, ATTN_WIDTH + KV_WIDTH:ATTN_WIDTH + 2 * KV_WIDTH].astype(BF16)
        u_ref[...] = proj[:, ATTN_WIDTH + 2 * KV_WIDTH:]

    row = lambda w: pl.BlockSpec((ts, w), lambda i: (i, 0))
    return _call(
        body, [x, vec, w_in_t, b_in, *rot], name=name, grid=(s // ts,),
        in_specs=[row(D_MODEL), _resident((8, D_MODEL)), _resident((IN_WIDTH, D_MODEL)), _resident((8, IN_WIDTH)),
                  row(LANES), row(LANES), row(LANES)],
        out_specs=[row(ATTN_WIDTH), row(KV_WIDTH), row(KV_WIDTH), row(POOL_WIDTH)],
        out_shape=[jax.ShapeDtypeStruct((s, ATTN_WIDTH), BF16), jax.ShapeDtypeStruct((s, KV_WIDTH), BF16),
                   jax.ShapeDtypeStruct((s, KV_WIDTH), BF16), jax.ShapeDtypeStruct((s, POOL_WIDTH), F32)],
        exchanges=exchanges)


def _band_mask_t(first_block):
    key = lax.broadcasted_iota(jnp.int32, (2 * BLOCK, 2 * BLOCK), 0)
    qry = lax.broadcasted_iota(jnp.int32, (2 * BLOCK, 2 * BLOCK), 1) & (BLOCK - 1)
    d = key - qry
    return (d >= 1) & (d <= BLOCK) & ((key >= BLOCK) | jnp.logical_not(first_block))


def _placed(both, kh, b):
    lane = lax.broadcasted_iota(jnp.int32, (1, LANES), 1)
    src = both if kh == b else pltpu.roll(both, HEAD_DIM, 1)
    return jnp.where((lane >= b * HEAD_DIM) & (lane < (b + 1) * HEAD_DIM), src, jnp.zeros_like(src))


def _pair(ref, rows, kh, r):
    j = 2 * kh + r
    return ref[rows, j * LANES:(j + 1) * LANES]


def _attn_specs(s):
    qb = min(ATTN_BLOCKS, s // BLOCK)
    cur = lambda w: pl.BlockSpec((qb * BLOCK, w), lambda i: (i, 0))
    prev = lambda w: pl.BlockSpec((BLOCK, w), lambda i: (jnp.maximum(i * qb - 1, 0), 0))
    return qb, cur, prev, pl.BlockSpec((N_HEADS, qb * BLOCK), lambda i: (0, i))


def _kv_window(prev_ref, cur_ref, jb):
    before = prev_ref[...] if jb == 0 else cur_ref[(jb - 1) * BLOCK:jb * BLOCK, :]
    return jnp.concatenate([before, cur_ref[jb * BLOCK:(jb + 1) * BLOCK, :]], axis=0)


def _sink_row(sink_ref, kh, b):
    lane = lax.broadcasted_iota(jnp.int32, (1, 2 * BLOCK), 1)
    return jnp.where(lane < BLOCK, sink_ref[GROUP * kh + b], sink_ref[GROUP * kh + 2 + b])


def _attn_fwd_t(sinks, q, k, v, name, exchanges=()):
    s = q.shape[0]
    qb, cur, prev, lse_spec = _attn_specs(s)

    def block(jb, sink_ref, q_ref, kc_ref, kp_ref, vc_ref, vp_ref, o_ref, l_ref):
        rows = slice(jb * BLOCK, (jb + 1) * BLOCK)
        valid = _band_mask_t(pl.program_id(0) == 0 if jb == 0 else False)
        k_both, v_both = _kv_window(kp_ref, kc_ref, jb), _kv_window(vp_ref, vc_ref, jb)
        for kh in range(N_KV_HEADS):
            qg = jnp.concatenate([_pair(q_ref, rows, kh, 0), _pair(q_ref, rows, kh, 1)], axis=0)
            keys = jnp.concatenate([_placed(k_both, kh, 0), _placed(k_both, kh, 1)], axis=0)
            st = _dot(keys, qg, NT)
            out = jnp.zeros((2 * BLOCK, LANES), F32)
            for b in range(2):
                sc = jnp.where(valid, st[b * 2 * BLOCK:(b + 1) * 2 * BLOCK, :], NEG_INF)
                sink = _sink_row(sink_ref, kh, b)
                m = jnp.maximum(jnp.max(sc, axis=0, keepdims=True), sink)
                p = jnp.exp(sc - m)
                den = jnp.sum(p, axis=0, keepdims=True) + jnp.exp(sink - m)
                out = out + _dot((p * (1.0 / den)).astype(BF16), _placed(v_both, kh, b), TN)
                lse = m + jnp.log(den)
                for r in range(2):
                    h = GROUP * kh + 2 * r + b
                    l_ref[h:h + 1, rows] = lse[:, r * BLOCK:(r + 1) * BLOCK]
            for r in range(2):
                j = 2 * kh + r
                o_ref[rows, j * LANES:(j + 1) * LANES] = out[r * BLOCK:(r + 1) * BLOCK, :].astype(BF16)

    def body(*refs):
        for jb in range(qb):
            block(jb, *refs)

    return _call(
        body, [sinks, q, k, k, v, v], name=name, grid=(s // (qb * BLOCK),),
        in_specs=[pl.BlockSpec(memory_space=pltpu.SMEM), cur(ATTN_WIDTH), cur(KV_WIDTH), prev(KV_WIDTH),
                  cur(KV_WIDTH), prev(KV_WIDTH)],
        out_specs=[cur(ATTN_WIDTH), lse_spec],
        out_shape=[jax.ShapeDtypeStruct((s, ATTN_WIDTH), BF16), jax.ShapeDtypeStruct((N_HEADS, s), F32)],
        exchanges=exchanges)


def _attn_bwd_t(sinks, q, do, lse, k, v, name, exchanges=()):
    s = q.shape[0]
    qb, cur, prev, lse_spec = _attn_specs(s)

    def block(jb, sink_ref, q_ref, do_ref, l_ref, kc_ref, kp_ref, vc_ref, vp_ref,
              dq_ref, dkc_ref, dkp_ref, dvc_ref, dvp_ref, dsink_ref):
        rows = slice(jb * BLOCK, (jb + 1) * BLOCK)
        valid = _band_mask_t(pl.program_id(0) == 0 if jb == 0 else False)
        k_both, v_both = _kv_window(kp_ref, kc_ref, jb), _kv_window(vp_ref, vc_ref, jb)
        lane = lax.broadcasted_iota(jnp.int32, (1, LANES), 1)
        col = lax.broadcasted_iota(jnp.int32, (1, 2 * BLOCK), 1)
        dk_heads, dv_heads = [], []
        for kh in range(N_KV_HEADS):
            qg = jnp.concatenate([_pair(q_ref, rows, kh, 0), _pair(q_ref, rows, kh, 1)], axis=0)
            dog = jnp.concatenate([_pair(do_ref, rows, kh, 0), _pair(do_ref, rows, kh, 1)], axis=0)
            k_placed = [_placed(k_both, kh, b) for b in range(2)]
            v_placed = [_placed(v_both, kh, b) for b in range(2)]
            st = _dot(jnp.concatenate(k_placed, axis=0), qg, NT)
            dpt = _dot(jnp.concatenate(v_placed, axis=0), dog, NT)
            dqg = jnp.zeros((2 * BLOCK, LANES), F32)
            dk_b, dv_b = [], []
            for b in range(2):
                part = slice(b * 2 * BLOCK, (b + 1) * 2 * BLOCK)
                heads = [GROUP * kh + 2 * r + b for r in range(2)]
                lse_row = jnp.concatenate([l_ref[h:h + 1, rows] for h in heads], axis=1)
                p = jnp.where(valid, jnp.exp(st[part, :] - lse_row), 0.0)
                dp = dpt[part, :]
                delta = jnp.sum(p * dp, axis=0, keepdims=True)
                sink_pull = jnp.exp(_sink_row(sink_ref, kh, b) - lse_row) * delta
                dsink_ref[heads[0]:heads[0] + 1, :] += -jnp.sum(jnp.where(col < BLOCK, sink_pull, 0.0))
                dsink_ref[heads[1]:heads[1] + 1, :] += -jnp.sum(jnp.where(col < BLOCK, 0.0, sink_pull))
                ds = (p * (dp - delta)).astype(BF16)
                dqg = dqg + _dot(ds, k_placed[b], TN)
                dk_b.append(_dot(ds, qg))
                dv_b.append(_dot(p.astype(BF16), dog))
            for parts, total in ((dk_b, dk_heads), (dv_b, dv_heads)):
                kept = jnp.where(lane < HEAD_DIM, parts[0], parts[1])
                total.append(kept + pltpu.roll(kept, HEAD_DIM, 1))
            for r in range(2):
                j = 2 * kh + r
                dq_ref[rows, j * LANES:(j + 1) * LANES] = dqg[r * BLOCK:(r + 1) * BLOCK, :] * Q_SCALE
        dk = jnp.where(lane < HEAD_DIM, dk_heads[0], dk_heads[1])
        dv = jnp.where(lane < HEAD_DIM, dv_heads[0], dv_heads[1])
        dkp_ref[rows, :] = dk[0:BLOCK, :]
        dkc_ref[rows, :] = dk[BLOCK:, :]
        dvp_ref[rows, :] = dv[0:BLOCK, :]
        dvc_ref[rows, :] = dv[BLOCK:, :]

    def body(*refs):
        @pl.when(pl.program_id(0) == 0)
        def _():
            refs[-1][...] = jnp.zeros_like(refs[-1])

        for jb in range(qb):
            block(jb, *refs)

    kv = jax.ShapeDtypeStruct((s, KV_WIDTH), F32)
    return _call(
        body, [sinks, q, do, lse, k, k, v, v], name=name, grid=(s // (qb * BLOCK),),
        in_specs=[pl.BlockSpec(memory_space=pltpu.SMEM), cur(ATTN_WIDTH), cur(ATTN_WIDTH), lse_spec,
                  cur(KV_WIDTH), prev(KV_WIDTH), cur(KV_WIDTH), prev(KV_WIDTH)],
        out_specs=[cur(ATTN_WIDTH), cur(KV_WIDTH), cur(KV_WIDTH), cur(KV_WIDTH), cur(KV_WIDTH),
                   pl.BlockSpec((8, LANES), lambda i: (0, 0))],
        out_shape=[jax.ShapeDtypeStruct((s, ATTN_WIDTH), F32), kv, kv, kv, kv, jax.ShapeDtypeStruct((8, LANES), F32)],
        exchanges=exchanges)


def _pool_counts(tile, ts):
    t = (tile * ts + lax.broadcasted_iota(jnp.int32, (ts, 1), 0) + 1).astype(F32)
    return [jnp.minimum(t, float(w)) for w in POOL_WINDOWS]


def _pooled(u, halo, counts, gi):
    cols = slice(gi * POOL_GROUP_WIDTH, (gi + 1) * POOL_GROUP_WIDTH)
    acc = jnp.concatenate([halo[:, cols], u[:, cols]], axis=0)
    shift = 1
    while shift < POOL_WINDOWS[gi]:
        acc = acc + pltpu.roll(acc, shift, 0)
        shift *= 2
    return acc[POOL_HALO:, :] / counts[gi] - u[:, cols]


def _fwd_out(ao, u, pool_w, psc, w_out, x, vec, name, exchanges=()):
    s = x.shape[0]
    ts = _seq_tile(s, 2)
    hb = ts // POOL_HALO

    def body(ao_ref, u_ref, uh_ref, pw_ref, psc_ref, w_ref, x_ref, vec_ref, x1_ref, mix_ref):
        i = pl.program_id(0)
        u_t = u_ref[...]
        halo = jnp.where(i > 0, uh_ref[...], 0.0)
        counts = _pool_counts(i, ts)
        mix = _dot(ao_ref[...], w_ref[0:ATTN_WIDTH, :])
        for gi in range(len(POOL_WINDOWS)):
            cols = slice(gi * POOL_GROUP_WIDTH, (gi + 1) * POOL_GROUP_WIDTH)
            og = _dot(_pooled(u_t, halo, counts, gi).astype(BF16), pw_ref[gi]) * psc_ref[0:1, cols]
            mix = mix + _dot(og.astype(BF16), w_ref[ATTN_WIDTH + gi * POOL_GROUP_WIDTH:
                                                    ATTN_WIDTH + (gi + 1) * POOL_GROUP_WIDTH, :])
        mix_ref[...] = mix
        x1_ref[...] = x_ref[...] + vec_ref[0:1, :] * (mix * _rsqrt_ms(mix) * vec_ref[1:2, :])

    row = lambda w: pl.BlockSpec((ts, w), lambda i: (i, 0))
    return _call(
        body, [ao, u, u, pool_w, psc, w_out, x, vec], name=name, grid=(s // ts,),
        in_specs=[row(ATTN_WIDTH), row(POOL_WIDTH),
                  pl.BlockSpec((POOL_HALO, POOL_WIDTH), lambda i: (jnp.maximum(i * hb - 1, 0), 0)),
                  _resident(pool_w.shape), _resident((8, POOL_WIDTH)), _resident((D_MODEL, D_MODEL)),
                  row(D_MODEL), _resident((8, D_MODEL))],
        out_specs=[row(D_MODEL), row(D_MODEL)],
        out_shape=[jax.ShapeDtypeStruct((s, D_MODEL), F32), jax.ShapeDtypeStruct((s, D_MODEL), F32)],
        exchanges=exchanges)


def _ffn_fwd(x1, vec, w_gate_t, w_up_t, w_down, name, exchanges=(), target=None):
    s = x1.shape[0]
    ts = _seq_tile(s)

    def body(x_ref, *refs):
        if target is None:
            vec_ref, wg_ref, wu_ref, wd_ref, x2_ref, f_ref, gt_ref, up_ref, h2_ref = refs
        else:
            t_ref, vec_ref, wg_ref, wu_ref, wd_ref, x2_ref, f_ref, gt_ref, up_ref, h2_ref, loss_ref = refs
        xt = x_ref[...]
        h2 = (xt * _rsqrt_ms(xt) * vec_ref[0:1, :] * vec_ref[1:2, :] + vec_ref[2:3, :]).astype(BF16)
        h2_ref[...] = h2
        f = jnp.zeros((ts, D_MODEL), F32)
        for c in range(D_FF // FF_CHUNK):
            cols = slice(c * FF_CHUNK, (c + 1) * FF_CHUNK)
            g = _dot(h2, wg_ref[cols, :], NT)
            up = _dot(h2, wu_ref[cols, :], NT)
            act = ((g * jax.nn.sigmoid(g)) * up).astype(BF16)
            gt_ref[:, cols] = g.astype(BF16)
            up_ref[:, cols] = up.astype(BF16)
            f = f + _dot(act, wd_ref[cols, :])
        f_ref[...] = f
        x2 = xt + vec_ref[3:4, :] * (f * _rsqrt_ms(f) * vec_ref[4:5, :])
        if target is None:
            x2_ref[...] = x2
        else:
            @pl.when(pl.program_id(0) == 0)
            def _():
                loss_ref[...] = jnp.zeros_like(loss_ref)

            diff = x2 - t_ref[...]
            x2_ref[...] = diff / D_MODEL
            loss_ref[...] += 0.5 * jnp.sum(jnp.mean(diff * diff, axis=-1, keepdims=True))

    row = lambda w: pl.BlockSpec((ts, w), lambda i: (i, 0))
    wide = jax.ShapeDtypeStruct((s, D_FF), BF16)
    with_loss = target is not None
    return _call(
        body, [x1] + ([target] if with_loss else []) + [vec, w_gate_t, w_up_t, w_down], name=name, grid=(s // ts,),
        in_specs=[row(D_MODEL)] * (2 if with_loss else 1) + [
            _resident((8, D_MODEL)), _resident((D_FF, D_MODEL)), _resident((D_FF, D_MODEL)), _resident((D_FF, D_MODEL))],
        out_specs=[row(D_MODEL), row(D_MODEL), row(D_FF), row(D_FF), row(D_MODEL)] + (
            [pl.BlockSpec((8, LANES), lambda i: (0, 0))] if with_loss else []),
        out_shape=[jax.ShapeDtypeStruct((s, D_MODEL), F32), jax.ShapeDtypeStruct((s, D_MODEL), F32), wide, wide,
                   jax.ShapeDtypeStruct((s, D_MODEL), BF16)] + (
            [jax.ShapeDtypeStruct((8, LANES), F32)] if with_loss else []),
        exchanges=exchanges)


def _norm_bwd(dy_hat, x_hat, r):
    return r * (dy_hat - x_hat * jnp.mean(dy_hat * x_hat, axis=-1, keepdims=True))


def _accumulate_rows(ref, rows):
    for j, val in enumerate(rows):
        ref[j:j + 1, :] += val


def _ffn_bwd_act(dx2, f, gt, up, vec, w_down, name, exchanges=()):
    s = dx2.shape[0]
    ts = _seq_tile(s)

    def body(dx2_ref, f_ref, gt_ref, up_ref, vec_ref, wd_ref, dgt_ref, dup_ref, dwd_ref, cs_ref):
        @pl.when(pl.program_id(0) == 0)
        def _():
            cs_ref[...] = jnp.zeros_like(cs_ref)
            dwd_ref[...] = jnp.zeros_like(dwd_ref)

        dx2_t, f_t = dx2_ref[...], f_ref[...]
        gate, g_post = vec_ref[3:4, :], vec_ref[4:5, :]
        rf = _rsqrt_ms(f_t)
        f_hat = f_t * rf
        df = _norm_bwd(dx2_t * gate * g_post, f_hat, rf).astype(BF16)
        _accumulate_rows(cs_ref, [_colsum(dx2_t * (f_hat * g_post)), _colsum(dx2_t * gate * f_hat)])
        for c in range(D_FF // FF_CHUNK):
            cols = slice(c * FF_CHUNK, (c + 1) * FF_CHUNK)
            dact = _dot(df, wd_ref[cols, :], NT)
            g, u_t = gt_ref[:, cols].astype(F32), up_ref[:, cols].astype(F32)
            sg = jax.nn.sigmoid(g)
            silu = g * sg
            dgt_ref[:, cols] = (dact * u_t * (sg * (1.0 + g * (1.0 - sg)))).astype(BF16)
            dup_ref[:, cols] = (dact * silu).astype(BF16)
            dwd_ref[cols, :] += _dot((silu * u_t).astype(BF16), df, TN)

    row = lambda w: pl.BlockSpec((ts, w), lambda i: (i, 0))
    wide = jax.ShapeDtypeStruct((s, D_FF), BF16)
    return _call(
        body, [dx2, f, gt, up, vec, w_down], name=name, grid=(s // ts,),
        in_specs=[row(D_MODEL), row(D_MODEL), row(D_FF), row(D_FF), _resident((8, D_MODEL)), _resident((D_FF, D_MODEL))],
        out_specs=[row(D_FF), row(D_FF), _resident((D_FF, D_MODEL)), pl.BlockSpec((8, D_MODEL), lambda i: (0, 0))],
        out_shape=[wide, wide, jax.ShapeDtypeStruct((D_FF, D_MODEL), F32), jax.ShapeDtypeStruct((8, D_MODEL), F32)],
        exchanges=exchanges)


def _ffn_bwd_in(dx2, x1, dgt, dup, vec, w_gate_t, w_up_t, name, exchanges=()):
    s = x1.shape[0]
    ts = _seq_tile(s)

    def body(dx2_ref, x1_ref, dgt_ref, dup_ref, vec_ref, wg_ref, wu_ref, dx1_ref, cs_ref):
        @pl.when(pl.program_id(0) == 0)
        def _():
            cs_ref[...] = jnp.zeros_like(cs_ref)

        g_pre, one_scale = vec_ref[0:1, :], vec_ref[1:2, :]
        dh2 = _dot(dgt_ref[...], wg_ref[...]) + _dot(dup_ref[...], wu_ref[...])
        x1_t = x1_ref[...]
        r1 = _rsqrt_ms(x1_t)
        x_hat = x1_t * r1
        dx1_ref[...] = dx2_ref[...] + _norm_bwd(dh2 * g_pre * one_scale, x_hat, r1)
        _accumulate_rows(cs_ref, [_colsum(dh2), _colsum(dh2 * (x_hat * g_pre)), _colsum(dh2 * one_scale * x_hat)])

    row = lambda w: pl.BlockSpec((ts, w), lambda i: (i, 0))
    return _call(
        body, [dx2, x1, dgt, dup, vec, w_gate_t, w_up_t], name=name, grid=(s // ts,),
        in_specs=[row(D_MODEL), row(D_MODEL), row(D_FF), row(D_FF), _resident((8, D_MODEL)),
                  _resident((D_FF, D_MODEL)), _resident((D_FF, D_MODEL))],
        out_specs=[row(D_MODEL), pl.BlockSpec((8, D_MODEL), lambda i: (0, 0))],
        out_shape=[jax.ShapeDtypeStruct((s, D_MODEL), F32), jax.ShapeDtypeStruct((8, D_MODEL), F32)],
        exchanges=exchanges)


def _weight_grad(a, b, name, exchanges=()):
    s, m = a.shape
    n = b.shape[1]
    tk = min(s, 2 * SEQ_TILE)
    steps = s // tk

    def body(a_ref, b_ref, o_ref, acc_ref):
        i = pl.program_id(0)

        @pl.when(i == 0)
        def _():
            acc_ref[...] = jnp.zeros_like(acc_ref)

        b_t = b_ref[...]
        for c in range(m // FF_CHUNK):
            rows = slice(c * FF_CHUNK, (c + 1) * FF_CHUNK)
            acc_ref[rows, :] += _dot(a_ref[:, rows], b_t, TN)

        @pl.when(i == steps - 1)
        def _():
            o_ref[...] = acc_ref[...].astype(BF16)

    (grad,), ex_outs = _call(
        body, [a, b], name=name, grid=(steps,),
        in_specs=[pl.BlockSpec((tk, m), lambda i: (i, 0)), pl.BlockSpec((tk, n), lambda i: (i, 0))],
        out_specs=[_resident((m, n))],
        out_shape=[jax.ShapeDtypeStruct((m, n), BF16)],
        scratch_shapes=[pltpu.VMEM((m, n), F32)],
        exchanges=exchanges)
    return grad, ex_outs


def _out_bwd(dx1, mix, ao, u, pool_w, psc, w_out, vec, name, exchanges=()):
    s = dx1.shape[0]
    ts = _seq_tile(s)
    nt = s // ts
    hb = ts // POOL_HALO
    ng = len(POOL_WINDOWS)

    def body(dx1_ref, mix_ref, ao_ref, u_ref, uh_ref, pw_ref, psc_ref, w_ref, vec_ref,
             do_ref, du_ref, dw_out_ref, cs_ref, dpw_ref, dpsc_ref, carry_ref, dw_ref):
        i = pl.program_id(0)
        tile = nt - 1 - i

        @pl.when(i == 0)
        def _():
            dw_ref[...] = jnp.zeros_like(dw_ref)
            cs_ref[...] = jnp.zeros_like(cs_ref)
            dpw_ref[...] = jnp.zeros_like(dpw_ref)
            dpsc_ref[...] = jnp.zeros_like(dpsc_ref)
            carry_ref[...] = jnp.zeros_like(carry_ref)

        dx1_t, mix_t = dx1_ref[...], mix_ref[...]
        gate, g_post = vec_ref[0:1, :], vec_ref[1:2, :]
        rm = _rsqrt_ms(mix_t)
        m_hat = mix_t * rm
        dmix = _norm_bwd(dx1_t * gate * g_post, m_hat, rm).astype(BF16)
        _accumulate_rows(cs_ref, [_colsum(dx1_t * (m_hat * g_post)), _colsum(dx1_t * gate * m_hat)])
        do_ref[...] = _dot(dmix, w_ref[0:ATTN_WIDTH, :], NT).astype(BF16)
        dw_ref[0:ATTN_WIDTH, :] += _dot(ao_ref[...], dmix, TN)

        u_t = u_ref[...]
        halo = jnp.where(tile > 0, uh_ref[...], 0.0)
        counts = _pool_counts(tile, ts)
        for gi in range(ng):
            cols = slice(gi * POOL_GROUP_WIDTH, (gi + 1) * POOL_GROUP_WIDTH)
            wide_cols = slice(ATTN_WIDTH + gi * POOL_GROUP_WIDTH, ATTN_WIDTH + (gi + 1) * POOL_GROUP_WIDTH)
            scale = psc_ref[0:1, cols]
            pooled = _pooled(u_t, halo, counts, gi).astype(BF16)
            og = _dot(pooled, pw_ref[gi])
            dw_ref[wide_cols, :] += _dot((og * scale).astype(BF16), dmix, TN)
            d_out = _dot(dmix, w_ref[wide_cols, :], NT)
            dpsc_ref[0:1, cols] += _colsum(d_out * og)
            d_og = (d_out * scale).astype(BF16)
            dpw_ref[gi] += _dot(pooled, d_og, TN)
            d_pooled = _dot(d_og, pw_ref[gi], NT)
            spread = d_pooled / counts[gi]
            acc = jnp.concatenate([spread, carry_ref[:, cols]], axis=0)
            shift = 1
            while shift < POOL_WINDOWS[gi]:
                acc = acc + pltpu.roll(acc, ts + POOL_HALO - shift, 0)
                shift *= 2
            du_ref[:, cols] = acc[0:ts, :] - d_pooled
            carry_ref[:, cols] = spread[0:POOL_HALO, :]

        @pl.when(i == nt - 1)
        def _():
            dw_out_ref[...] = dw_ref[...].astype(BF16)

    row = lambda w: pl.BlockSpec((ts, w), lambda i: (nt - 1 - i, 0))
    fixed = lambda shape: pl.BlockSpec(shape, lambda i: (0,) * len(shape))
    return _call(
        body, [dx1, mix, ao, u, u, pool_w, psc, w_out, vec], name=name, grid=(nt,),
        in_specs=[row(D_MODEL), row(D_MODEL), row(ATTN_WIDTH), row(POOL_WIDTH),
                  pl.BlockSpec((POOL_HALO, POOL_WIDTH), lambda i: (jnp.maximum((nt - 1 - i) * hb - 1, 0), 0)),
                  _resident(pool_w.shape), _resident((8, POOL_WIDTH)), _resident((D_MODEL, D_MODEL)),
                  _resident((8, D_MODEL))],
        out_specs=[row(ATTN_WIDTH), row(POOL_WIDTH), _resident((D_MODEL, D_MODEL)), fixed((8, D_MODEL)),
                   fixed(pool_w.shape), fixed((8, POOL_WIDTH))],
        out_shape=[jax.ShapeDtypeStruct((s, ATTN_WIDTH), BF16), jax.ShapeDtypeStruct((s, POOL_WIDTH), F32),
                   jax.ShapeDtypeStruct((D_MODEL, D_MODEL), BF16),
                   jax.ShapeDtypeStruct((8, D_MODEL), F32), jax.ShapeDtypeStruct(pool_w.shape, F32),
                   jax.ShapeDtypeStruct((8, POOL_WIDTH), F32)],
        scratch_shapes=[pltpu.VMEM((POOL_HALO, POOL_WIDTH), F32), pltpu.VMEM((D_MODEL, D_MODEL), F32)],
        exchanges=exchanges)


def _in_bwd(dq, dkc, dkp, dvc, dvp, du, x, dx1, vec, w_in_t, rot, name, exchanges=()):
    s = x.shape[0]
    ts = _seq_tile(s)
    nt = s // ts
    bpt = ts // BLOCK

    def body(dq_ref, dkc_ref, dkp_ref, dkn_ref, dvc_ref, dvp_ref, dvn_ref, du_ref, x_ref, dx1_ref, vec_ref, w_ref,
             cos_ref, lo_ref, hi_ref, dx_ref, dw_out_ref, cs_ref, db_ref, dw_ref):
        i = pl.program_id(0)

        @pl.when(i == 0)
        def _():
            dw_ref[...] = jnp.zeros_like(dw_ref)
            cs_ref[...] = jnp.zeros_like(cs_ref)
            db_ref[...] = jnp.zeros_like(db_ref)

        cos, lo, hi = cos_ref[...], lo_ref[...], hi_ref[...]

        def with_next_block(cur_ref, prev_ref, next_ref):
            nxt = jnp.where(i < nt - 1, next_ref[...], 0.0)
            later = nxt if bpt == 1 else jnp.concatenate([prev_ref[BLOCK:, :], nxt], axis=0)
            return cur_ref[...] + later

        pieces = [_rotate_bwd(dq_ref[:, j * LANES:(j + 1) * LANES], cos, lo, hi) for j in range(ATTN_WIDTH // LANES)]
        pieces.append(_rotate_bwd(with_next_block(dkc_ref, dkp_ref, dkn_ref), cos, lo, hi))
        pieces.append(with_next_block(dvc_ref, dvp_ref, dvn_ref))
        pieces.append(du_ref[...])
        dproj = jnp.concatenate(pieces, axis=1)
        db_ref[0:1, :] += _colsum(dproj)
        dproj_b = dproj.astype(BF16)
        dh = _dot(dproj_b, w_ref[...])

        xt = x_ref[...]
        g_pre, one_scale = vec_ref[0:1, :], vec_ref[1:2, :]
        r = _rsqrt_ms(xt)
        x_hat = xt * r
        dw_ref[...] += _dot(dproj_b, (x_hat * g_pre * one_scale + vec_ref[2:3, :]).astype(BF16), TN)
        dx_ref[...] = dx1_ref[...] + _norm_bwd(dh * g_pre * one_scale, x_hat, r)
        _accumulate_rows(cs_ref, [_colsum(dh), _colsum(dh * (x_hat * g_pre)), _colsum(dh * one_scale * x_hat)])

        @pl.when(i == nt - 1)
        def _():
            dw_out_ref[...] = dw_ref[...].astype(BF16)

    row = lambda w: pl.BlockSpec((ts, w), lambda i: (i, 0))
    nxt = pl.BlockSpec((BLOCK, KV_WIDTH), lambda i: (jnp.minimum((i + 1) * bpt, s // BLOCK - 1), 0))
    fixed = lambda shape: pl.BlockSpec(shape, lambda i: (0,) * len(shape))
    return _call(
        body, [dq, dkc, dkp, dkp, dvc, dvp, dvp, du, x, dx1, vec, w_in_t, *rot], name=name, grid=(nt,),
        in_specs=[row(ATTN_WIDTH), row(KV_WIDTH), row(KV_WIDTH), nxt, row(KV_WIDTH), row(KV_WIDTH), nxt,
                  row(POOL_WIDTH), row(D_MODEL), row(D_MODEL), _resident((8, D_MODEL)), _resident((IN_WIDTH, D_MODEL)),
                  row(LANES), row(LANES), row(LANES)],
        out_specs=[row(D_MODEL), _resident((IN_WIDTH, D_MODEL)), fixed((8, D_MODEL)), fixed((8, IN_WIDTH))],
        out_shape=[jax.ShapeDtypeStruct((s, D_MODEL), F32), jax.ShapeDtypeStruct((IN_WIDTH, D_MODEL), BF16),
                   jax.ShapeDtypeStruct((8, D_MODEL), F32), jax.ShapeDtypeStruct((8, IN_WIDTH), F32)],
        scratch_shapes=[pltpu.VMEM((IN_WIDTH, D_MODEL), F32)],
        exchanges=exchanges)


def _row_tile(rows):
    for t in (512, 352, 256, 176, 160, 128, 64, 32, 16, 8):
        if rows % t == 0:
            return t
    raise ValueError(f"no row tile for {rows} rows")


def _sum_parts(parts, first, name):
    n, r, c = parts.shape
    tr = _row_tile(r)

    def body(p_ref, o_ref):
        acc = p_ref[first].astype(F32)
        for k in range(n):
            if k != first:
                acc = acc + p_ref[k].astype(F32)
        o_ref[...] = acc

    return pl.pallas_call(
        body, name=name, grid=(r // tr,),
        in_specs=[pl.BlockSpec((n, tr, c), lambda i: (0, i, 0))],
        out_specs=pl.BlockSpec((tr, c), lambda i: (i, 0)),
        out_shape=jax.ShapeDtypeStruct((r, c), F32),
        compiler_params=_cparams(1),
    )(parts)


def _adamw(w, g, m, v, name):
    r, c = w.shape
    tr = _row_tile(r)

    def body(w_ref, g_ref, m_ref, v_ref, d_ref, nm_ref, nv_ref):
        g_t = g_ref[...]
        m_new = ADAM_B1 * m_ref[...] + (1.0 - ADAM_B1) * g_t
        v_new = ADAM_B2 * v_ref[...] + (1.0 - ADAM_B2) * (g_t * g_t)
        m_hat = m_new / (1.0 - ADAM_B1 ** ADAM_STEP)
        v_hat = v_new / (1.0 - ADAM_B2 ** ADAM_STEP)
        d_ref[...] = -ADAM_LR * (m_hat / (jnp.sqrt(v_hat) + ADAM_EPS) + ADAM_WD * w_ref[...])
        nm_ref[...] = m_new
        nv_ref[...] = v_new

    spec = pl.BlockSpec((tr, c), lambda i: (i, 0))
    shape = jax.ShapeDtypeStruct((r, c), F32)
    return pl.pallas_call(
        body, name=name, grid=(r // tr,), in_specs=[spec] * 4, out_specs=[spec] * 3, out_shape=[shape] * 3,
        compiler_params=_cparams(1),
    )(w, g, m, v)


SMALL_ROWS = 160


def _pack_small(dmod, gains, b_in, pool_scale, sinks, pool_w):
    pad = lambda a, width: jnp.pad(a, ((0, 0), (0, width - a.shape[1])))
    parts = [dmod.reshape(DEPTH * N_MOD, D_MODEL), gains.reshape(4 * DEPTH, D_MODEL),
             pad(b_in, 2 * D_MODEL).reshape(2 * DEPTH, D_MODEL), pool_scale.reshape(1, D_MODEL),
             pad(sinks.reshape(1, DEPTH * N_HEADS), D_MODEL), pool_w.reshape(-1, D_MODEL)]
    packed = jnp.concatenate(parts, axis=0)
    return jnp.pad(packed, ((0, SMALL_ROWS - packed.shape[0]), (0, 0)))


def _unpack_small(p):
    r = DEPTH * N_MOD
    dmod = p[0:r].reshape(DEPTH, N_MOD * D_MODEL)
    gains = p[r:r + 4 * DEPTH].reshape(4, DEPTH, D_MODEL)
    r += 4 * DEPTH
    b_in = p[r:r + 2 * DEPTH].reshape(DEPTH, 2 * D_MODEL)[:, :IN_WIDTH]
    r += 2 * DEPTH
    pool_scale = p[r].reshape(DEPTH, POOL_WIDTH)
    sinks = p[r + 1, :DEPTH * N_HEADS].reshape(DEPTH, N_HEADS)
    pool_w = p[r + 2:r + 2 + 128].reshape(DEPTH, len(POOL_WINDOWS), POOL_GROUP_WIDTH, POOL_GROUP_WIDTH)
    return dmod, gains, b_in, pool_scale, sinks, pool_w


LAYER_ROWS = 80


def _pack_layer(dmod, gains, b_in, pool_scale, sinks, pool_w):
    misc = jnp.concatenate([pool_scale, sinks, jnp.zeros((D_MODEL - POOL_WIDTH - N_HEADS,), F32)])
    parts = [dmod.reshape(N_MOD, D_MODEL), gains, jnp.pad(b_in, (0, 2 * D_MODEL - IN_WIDTH)).reshape(2, D_MODEL),
             misc.reshape(1, D_MODEL), pool_w.reshape(-1, D_MODEL)]
    packed = jnp.concatenate(parts, axis=0)
    return jnp.pad(packed, ((0, LAYER_ROWS - packed.shape[0]), (0, 0)))


def _unpack_layer(p):
    r = N_MOD + 4
    b_in = p[r:r + 2].reshape(2 * D_MODEL)[:IN_WIDTH]
    pool_w = p[r + 3:r + 3 + 64].reshape(len(POOL_WINDOWS), POOL_GROUP_WIDTH, POOL_GROUP_WIDTH)
    return (p[0:N_MOD].reshape(N_MOD * D_MODEL), p[N_MOD:r], b_in, p[r + 2, :POOL_WIDTH],
            p[r + 2, POOL_WIDTH:POOL_WIDTH + N_HEADS], pool_w)


def _rotary_tables(positions):
    inv_freq = ROPE_THETA ** (-jnp.arange(0, ROT_DIM, 2, dtype=F32) / ROT_DIM)
    half = ROT_DIM // 2
    head_freq = jnp.concatenate([inv_freq, inv_freq, jnp.zeros((HEAD_DIM - ROT_DIM,), F32)])
    lane_freq = jnp.concatenate([head_freq, head_freq])
    ang = positions.reshape(-1).astype(F32)[:, None] * lane_freq[None, :]
    cos, sin = jnp.cos(ang), jnp.sin(ang)
    in_head = lax.broadcasted_iota(jnp.int32, (1, LANES), 1) % HEAD_DIM
    return cos, jnp.where(in_head < half, -sin, 0.0), jnp.where((in_head >= half) & (in_head < ROT_DIM), sin, 0.0)


def kernel(x, c, positions, ada_w, ada_b, w_in, b_in, sinks, pool_w, pool_scale, w_out, w_gate, w_up, w_down, g_pre_mix, g_post_mix, g_pre_ffn, g_post_ffn, loss_target, m_ada_w, m_ada_b, m_w_in, m_b_in, m_sinks, m_pool_w, m_pool_scale, m_w_out, m_w_gate, m_w_up, m_w_down, m_g_pre_mix, m_g_post_mix, m_g_pre_ffn, m_g_post_ffn, v_ada_w, v_ada_b, v_w_in, v_b_in, v_sinks, v_pool_w, v_pool_scale, v_w_out, v_w_gate, v_w_up, v_w_down, v_g_pre_mix, v_g_post_mix, v_g_pre_ffn, v_g_post_ffn):
    me = _index(_mesh_place())
    x0 = x[0]
    target = loss_target[0]
    rot = _rotary_tables(positions)
    ada_cols = ada_w.shape[2]

    shards = []
    for l in range(DEPTH):
        shards += [w_in[l].T.astype(BF16), w_out[l].astype(BF16), w_gate[l].T.astype(BF16), w_up[l].T.astype(BF16),
                   w_down[l].astype(BF16)]
    whole = lambda gathered: [g.reshape(-1, D_MODEL) for g in gathered]
    full = [None] * (5 * DEPTH)
    first = _exchanges_alone([_Gather(shards[0:2] + [jnp.broadcast_to(c, (8, D_MODEL))])], "gather_first")[0]
    full[0:2] = whole(first[0:2])

    mod_part, c_act = _mod_fwd(first[2][:, 0, :], ada_w)
    mod_all = _allgather_vmem(mod_part.reshape(DEPTH * N_DEV, ada_cols), "gather_mod")
    mod_mine = lax.dynamic_slice_in_dim(mod_all.reshape(N_DEV, DEPTH, N_DEV, ada_cols), me, 1, axis=2)[:, :, 0, :]
    mod = jnp.transpose(mod_mine, (1, 0, 2)).reshape(DEPTH, N_MOD * D_MODEL) + ada_b
    mod = mod.reshape(DEPTH, N_MOD, D_MODEL)

    hosted = {("fwd_in", 0): (_Gather(shards[2:3]), [2]), ("attn_fwd", 0): (_Gather(shards[3:4]), [3]),
              ("fwd_out", 0): (_Gather(shards[4:5]), [4]),
              ("ffn_fwd", 0): (_Gather(shards[5:10]), [5, 6, 7, 8, 9])}

    def take(kind, l, ex_outs):
        if (kind, l) in hosted:
            for slot, g in zip(hosted[kind, l][1], whole(ex_outs[0])):
                full[slot] = g

    def beside(kind, l):
        return [hosted[kind, l][0]] if (kind, l) in hosted else []

    pool_w_b = pool_w.astype(BF16)

    saved = []
    xl = x0
    for l in range(DEPTH):
        vec_in = _rows(g_pre_mix[l], 1.0 + mod[l, 1], mod[l, 0], width=D_MODEL)
        vec_out = _rows(mod[l, 2], g_post_mix[l], width=D_MODEL)
        vec_ffn = _rows(g_pre_ffn[l], 1.0 + mod[l, 4], mod[l, 3], mod[l, 5], g_post_ffn[l], width=D_MODEL)
        psc = _rows(pool_scale[l], width=POOL_WIDTH)
        (q, k, v, u), ex = _fwd_in(xl, vec_in, full[5 * l], _rows(b_in[l], width=IN_WIDTH), rot, f"fwd_in_{l}",
                                   beside("fwd_in", l))
        take("fwd_in", l, ex)
        (ao, lse), ex = _attn_fwd_t(sinks[l], q, k, v, f"attn_fwd_{l}", beside("attn_fwd", l))
        take("attn_fwd", l, ex)
        (x1, mix), ex = _fwd_out(ao, u, pool_w_b[l], psc, full[5 * l + 1], xl, vec_out, f"fwd_out_{l}",
                                 beside("fwd_out", l))
        take("fwd_out", l, ex)
        (x2, f, gt, up, h2, *loss_part), ex = _ffn_fwd(
            x1, vec_ffn, full[5 * l + 2], full[5 * l + 3], full[5 * l + 4], f"ffn_fwd_{l}", beside("ffn_fwd", l),
            target=target if l == DEPTH - 1 else None)
        take("ffn_fwd", l, ex)
        saved.append((xl, q, k, v, u, ao, lse, x1, mix, f, gt, up, h2, vec_in, vec_out, vec_ffn, psc))
        xl = x2

    dx = xl
    loss = lax.psum(loss_part[0][0, 0], ("x", "y", "c"))

    stacks = [None] * (5 * DEPTH)
    parts = [None] * (5 * DEPTH)
    small_all = [None] * DEPTH
    ready_stacks, ready_small = [], []

    def leaving(row_budget):
        exs, notes = [], []
        taken = []
        while ready_stacks and stacks[ready_stacks[0]].shape[1] <= row_budget:
            row_budget -= stacks[ready_stacks[0]].shape[1]
            taken.append(ready_stacks.pop(0))
        if taken:
            exs.append(_Scatter([stacks[a] for a in taken]))
            notes.append(("stacks", taken))
        if ready_small and row_budget >= 0:
            exs.append(_Gather([ready_small[0][1]]))
            notes.append(("small", ready_small[0][0]))
            ready_small.clear()
        return exs, notes

    def arrived(notes, ex_outs):
        for (kind, what), outs in zip(notes, ex_outs):
            if kind == "stacks":
                for a, o in zip(what, outs):
                    parts[a] = o
            else:
                small_all[what] = outs[0]

    def grad_ready(a, grad):
        stacks[a] = grad.reshape(N_DEV, -1, D_MODEL)
        ready_stacks.append(a)

    def hosting(row_budget, kernel_fn, *args):
        exs, notes = leaving(row_budget)
        outs, ex_outs = kernel_fn(*args, exs)
        arrived(notes, ex_outs)
        return outs

    for l in reversed(range(DEPTH)):
        w_in_t, w_out_f, w_gate_t, w_up_t, w_down_f = full[5 * l:5 * l + 5]
        xin, q, k, v, u, ao, lse, x1, mix, f, gt, up, h2, vec_in, vec_out, vec_ffn, psc = saved[l]
        dgt, dup, d_w_down, cs_a = hosting(600, _ffn_bwd_act, dx, f, gt, up, vec_ffn, w_down_f, f"ffn_bwd_act_{l}")
        grad_ready(5 * l + 4, d_w_down.astype(BF16))
        dx1, cs_f = hosting(500, _ffn_bwd_in, dx, x1, dgt, dup, vec_ffn, w_gate_t, w_up_t, f"ffn_bwd_in_{l}")
        grad_ready(5 * l + 2, hosting(-1, _weight_grad, dgt, h2, f"grad_w_gate_{l}"))
        grad_ready(5 * l + 3, hosting(-1, _weight_grad, dup, h2, f"grad_w_up_{l}"))
        do, du, d_w_out, cs_o, dpw, dpsc = hosting(500, _out_bwd, dx1, mix, ao, u, pool_w_b[l], psc, w_out_f, vec_out,
                                                   f"out_bwd_{l}")
        grad_ready(5 * l + 1, d_w_out)
        dq, dkc, dkp, dvc, dvp, dsk = hosting(400, _attn_bwd_t, sinks[l], q, do, lse, k, v, f"attn_bwd_{l}")
        dx, d_w_in_t, cs_i, db = hosting(250, _in_bwd, dq, dkc, dkp, dvc, dvp, du, xin, dx1, vec_in, w_in_t, rot,
                                         f"in_bwd_{l}")
        grad_ready(5 * l, d_w_in_t)
        d_mod = jnp.concatenate([cs_i[0], cs_i[1], cs_o[0], cs_f[0], cs_f[1], cs_a[0]])
        d_gain = jnp.stack([cs_i[2], cs_o[1], cs_f[2], cs_a[1]])
        ready_small.append((l, _pack_layer(d_mod, d_gain, db[0], dpsc[0], dsk[:, 0], dpw)))
    grad_x = dx[None]
    exs, notes = leaving(N_DEV * D_FF)
    arrived(notes, _exchanges_alone(exs, "exchange_last"))

    layer_sums = [_unpack_layer(_sum_parts(small_all[l], 0, f"sum_small_{l}")) for l in range(DEPTH)]
    g_ada_b, g_gains, g_b_in, g_pool_scale, g_sinks, g_pool_w = (
        jnp.stack([layer_sums[l][j] for l in range(DEPTH)], axis=1 if j == 1 else 0) for j in range(6))
    small_sum = _pack_small(g_ada_b, g_gains, g_b_in, g_pool_scale, g_sinks, g_pool_w)
    gains = jnp.stack([g_pre_mix, g_post_mix, g_pre_ffn, g_post_ffn])
    m_gains = jnp.stack([m_g_pre_mix, m_g_post_mix, m_g_pre_ffn, m_g_post_ffn])
    v_gains = jnp.stack([v_g_pre_mix, v_g_post_mix, v_g_pre_ffn, v_g_post_ffn])
    small_step = _adamw(_pack_small(ada_b, gains, b_in, pool_scale, sinks, pool_w), small_sum,
                        _pack_small(m_ada_b, m_gains, m_b_in, m_pool_scale, m_sinks, m_pool_w),
                        _pack_small(v_ada_b, v_gains, v_b_in, v_pool_scale, v_sinks, v_pool_w), "adamw_small")
    small_out = [_unpack_small(p) for p in small_step]

    dmod_all = jnp.stack([small_all[l][:, 0:N_MOD, :].reshape(N_DEV, N_DEV, ada_cols) for l in range(DEPTH)])
    dmod_cols = lax.dynamic_slice_in_dim(dmod_all, me, 1, axis=2)[:, :, 0, :]
    g_ada_w = _ada_grad(c_act, dmod_cols)

    summed = [_sum_parts(parts[a], _Scatter.OWN, f"sum_grads_{a}") for a in range(5 * DEPTH)]
    per_kind = [jnp.stack([summed[5 * l + j] for l in range(DEPTH)]) for j in range(5)]
    g_w_in, g_w_out, g_w_gate, g_w_up, g_w_down = per_kind
    g_w_in, g_w_gate, g_w_up = (jnp.transpose(g, (0, 2, 1)) for g in (g_w_in, g_w_gate, g_w_up))

    def step(w, g, m, v, name):
        flat = lambda a: a.reshape(-1, a.shape[-1])
        return [o.reshape(w.shape) for o in _adamw(flat(w), flat(g), flat(m), flat(v), name)]

    big = {
        "ada_w": (g_ada_w, step(ada_w, g_ada_w, m_ada_w, v_ada_w, "adamw_ada_w")),
        "w_in": (g_w_in, step(w_in, g_w_in, m_w_in, v_w_in, "adamw_w_in")),
        "w_out": (g_w_out, step(w_out, g_w_out, m_w_out, v_w_out, "adamw_w_out")),
        "w_gate": (g_w_gate, step(w_gate, g_w_gate, m_w_gate, v_w_gate, "adamw_w_gate")),
        "w_up": (g_w_up, step(w_up, g_w_up, m_w_up, v_w_up, "adamw_w_up")),
        "w_down": (g_w_down, step(w_down, g_w_down, m_w_down, v_w_down, "adamw_w_down")),
    }

    def ordered(pick_big, pick_small):
        ada_b_, gains_, b_in_, pool_scale_, sinks_, pool_w_ = pick_small
        return [pick_big("ada_w"), ada_b_, pick_big("w_in"), b_in_, sinks_, pool_w_, pool_scale_, pick_big("w_out"),
                pick_big("w_gate"), pick_big("w_up"), pick_big("w_down"), gains_[0], gains_[1], gains_[2], gains_[3]]

    grads = ordered(lambda n: big[n][0], (g_ada_b, g_gains, g_b_in, g_pool_scale, g_sinks, g_pool_w))
    deltas = ordered(lambda n: big[n][1][0], small_out[0])
    new_m = ordered(lambda n: big[n][1][1], small_out[1])
    new_v = ordered(lambda n: big[n][1][2], small_out[2])
    return (loss, grad_x, *grads, *deltas, *new_m, *new_v)
```

```python
import jax
import jax.numpy as jnp
from jax import lax
from jax.experimental import pallas as pl
from jax.experimental.pallas import tpu as pltpu

F32 = jnp.float32
BF16 = jnp.bfloat16

N_DEV = 8
DEPTH = 2
D_MODEL = 1024
HEAD_DIM = 64
N_HEADS = 8
N_KV_HEADS = 2
GROUP = N_HEADS // N_KV_HEADS
ATTN_WIDTH = N_HEADS * HEAD_DIM
KV_WIDTH = N_KV_HEADS * HEAD_DIM
POOL_WIDTH = 512
POOL_WINDOWS = (2, 4, 8, 16)
POOL_GROUP_WIDTH = 128
POOL_HALO = 16
IN_WIDTH = ATTN_WIDTH + 2 * KV_WIDTH + POOL_WIDTH
D_FF = 2816
N_MOD = 6
BLOCK = 128
ROT_DIM = 16
ROPE_THETA = 500000.0
EPS = 1e-6
NEG_INF = -1e30
Q_SCALE = HEAD_DIM ** -0.5

ADAM_LR = 0.001
ADAM_B1 = 0.9
ADAM_B2 = 0.999
ADAM_EPS = 1e-08
ADAM_WD = 0.01
ADAM_STEP = 10

LANES = 128
SEQ_TILE = 512
ATTN_BLOCKS = 2
FF_CHUNK = 256
VMEM_LIMIT = 56 * 1024 * 1024
MESH = pl.DeviceIdType.MESH

NT = (((1,), (1,)), ((), ()))
TN = (((0,), (0,)), ((), ()))


def _dot(a, b, dims=None):
    if dims is None:
        return jnp.dot(a, b, preferred_element_type=F32)
    return lax.dot_general(a, b, dims, preferred_element_type=F32)


def _cparams(n_axes):
    return pltpu.CompilerParams(dimension_semantics=("arbitrary",) * n_axes, vmem_limit_bytes=VMEM_LIMIT)


def _resident(shape):
    zeros = (0,) * len(shape)
    return pl.BlockSpec(shape, lambda *_: zeros, pipeline_mode=pl.Buffered(1))


def _rows(*vectors, width):
    rows = [jnp.reshape(v, (1, width)).astype(F32) for v in vectors]
    rows.append(jnp.zeros((8 - len(rows), width), F32))
    return jnp.concatenate(rows, axis=0)


def _rsqrt_ms(x):
    return lax.rsqrt(jnp.mean(x * x, axis=-1, keepdims=True) + EPS)


def _colsum(x):
    return jnp.sum(x, axis=0, keepdims=True)


def _seq_tile(s, tiles=1):
    return min(s, tiles * SEQ_TILE)


def _mesh_place():
    x, y, c = lax.axis_index("x"), lax.axis_index("y"), lax.axis_index("c")
    return x, y, c


def _flip(place, k):
    x, y, c = place
    return (1 - x if k & 4 else x, 1 - y if k & 2 else y, 1 - c if k & 1 else c)


def _index(place):
    x, y, c = place
    return 4 * x + 2 * y + c


def _allgather_vmem(block, name):
    r, c = block.shape

    def body(x_ref, out_ref, send_sems, recv_sems, local_sem):
        me = _mesh_place()
        mine = pltpu.make_async_copy(x_ref, out_ref.at[_index(me)], local_sem)
        mine.start()

        def copy(k):
            return pltpu.make_async_remote_copy(
                src_ref=x_ref, dst_ref=out_ref.at[_index(me)], send_sem=send_sems.at[k - 1], recv_sem=recv_sems.at[k - 1],
                device_id=_flip(me, k), device_id_type=MESH)

        def arrival(k):
            return pltpu.make_async_remote_copy(
                src_ref=x_ref, dst_ref=out_ref.at[_index(_flip(me, k))], send_sem=send_sems.at[k - 1],
                recv_sem=recv_sems.at[k - 1], device_id=_flip(me, k), device_id_type=MESH)

        for k in range(1, N_DEV):
            copy(k).start()
        for k in range(1, N_DEV):
            arrival(k).wait_recv()
        for k in range(1, N_DEV):
            copy(k).wait_send()
        mine.wait()

    return pl.pallas_call(
        body, name=name,
        out_shape=jax.ShapeDtypeStruct((N_DEV, r, c), block.dtype),
        in_specs=[pl.BlockSpec(memory_space=pltpu.VMEM)],
        out_specs=pl.BlockSpec(memory_space=pltpu.VMEM),
        scratch_shapes=[pltpu.SemaphoreType.DMA((N_DEV - 1,)), pltpu.SemaphoreType.DMA((N_DEV - 1,)),
                        pltpu.SemaphoreType.DMA],
    )(block)


class _Gather:
    def __init__(self, shards):
        n = len(shards)
        self.operands = list(shards)
        self.out_shape = [jax.ShapeDtypeStruct((N_DEV,) + s.shape, s.dtype) for s in shards]
        self.scratch = [pltpu.SemaphoreType.DMA((N_DEV - 1, n)), pltpu.SemaphoreType.DMA((N_DEV - 1, n)),
                        pltpu.SemaphoreType.DMA((n,))]

    def _copies(self, x_refs, out_refs, sems):
        send_sems, recv_sems, local_sems = sems
        n = len(x_refs)
        x, y, c = _mesh_place()
        me, sibling = (x, y, c), (x, y, 1 - c)
        chips = [(1 - x, y), (x, 1 - y), (1 - x, 1 - y)]

        def copy(k, a, block, to, from_input=False):
            rows = out_refs[a].at[_index(block)]
            return pltpu.make_async_remote_copy(
                src_ref=x_refs[a] if from_input else rows, dst_ref=rows,
                send_sem=send_sems.at[k, a], recv_sem=recv_sems.at[k, a], device_id=to, device_id_type=MESH)

        mine = [pltpu.make_async_copy(x_refs[a], out_refs[a].at[_index(me)], local_sems.at[a]) for a in range(n)]
        first = [copy(0, a, me, sibling, from_input=True) for a in range(n)]
        first += [copy(1 + j, a, me, (*chip, c), from_input=True) for j, chip in enumerate(chips) for a in range(n)]
        over_ici = [copy(1 + j, a, (*chip, c), me) for j, chip in enumerate(chips) for a in range(n)]
        passed = [copy(4 + j, a, (*chip, c), sibling) for j, chip in enumerate(chips) for a in range(n)]
        from_sibling = [copy(0, a, sibling, me) for a in range(n)]
        from_sibling += [copy(4 + j, a, (*chip, 1 - c), me) for j, chip in enumerate(chips) for a in range(n)]
        return mine, first, over_ici, passed, from_sibling

    def begin(self, x_refs, out_refs, sems):
        mine, first, _, _, _ = self._copies(x_refs, out_refs, sems)
        for cp in mine + first:
            cp.start()

    def middle(self, x_refs, out_refs, sems):
        _, _, over_ici, passed, _ = self._copies(x_refs, out_refs, sems)
        for arrived, onward in zip(over_ici, passed):
            arrived.wait_recv()
            onward.start()

    def end(self, x_refs, out_refs, sems):
        mine, first, _, passed, from_sibling = self._copies(x_refs, out_refs, sems)
        for cp in from_sibling:
            cp.wait_recv()
        for cp in first + passed:
            cp.wait_send()
        for cp in mine:
            cp.wait()


class _Scatter:
    OWN = N_DEV - 1

    def __init__(self, stacks):
        n = len(stacks)
        self.operands = list(stacks)
        self.out_shape = [jax.ShapeDtypeStruct(s.shape, s.dtype) for s in stacks]
        self.scratch = [pltpu.SemaphoreType.DMA((N_DEV - 1, n)), pltpu.SemaphoreType.DMA((N_DEV - 1, n)),
                        pltpu.SemaphoreType.DMA((n,))]

    def _copies(self, g_refs, out_refs, sems):
        send_sems, recv_sems, local_sems = sems
        me = _mesh_place()

        def copy(k, a):
            peer = _flip(me, k)
            return pltpu.make_async_remote_copy(
                src_ref=g_refs[a].at[_index(peer)], dst_ref=out_refs[a].at[k - 1],
                send_sem=send_sems.at[k - 1, a], recv_sem=recv_sems.at[k - 1, a], device_id=peer, device_id_type=MESH)

        mine = [pltpu.make_async_copy(g_refs[a].at[_index(me)], out_refs[a].at[self.OWN], local_sems.at[a])
                for a in range(len(g_refs))]
        return mine, [copy(k, a) for k in range(1, N_DEV) for a in range(len(g_refs))]

    def begin(self, g_refs, out_refs, sems):
        mine, copies = self._copies(g_refs, out_refs, sems)
        for cp in mine + copies:
            cp.start()

    def middle(self, g_refs, out_refs, sems):
        pass

    def end(self, g_refs, out_refs, sems):
        mine, copies = self._copies(g_refs, out_refs, sems)
        for cp in copies:
            cp.wait_recv()
        for cp in copies:
            cp.wait_send()
        for cp in mine:
            cp.wait()


def _call(body, args, *, name, grid, in_specs, out_specs, out_shape, scratch_shapes=(), exchanges=()):
    if not exchanges:
        outs = pl.pallas_call(body, name=name, grid=grid, in_specs=in_specs, out_specs=out_specs, out_shape=out_shape,
                              scratch_shapes=list(scratch_shapes), compiler_params=_cparams(len(grid)))(*args)
        return outs, []
    (steps,) = grid
    n_in, n_out, n_scr = len(in_specs), len(out_specs), len(scratch_shapes)
    ex_in = [len(ex.operands) for ex in exchanges]
    ex_out = [len(ex.out_shape) for ex in exchanges]
    ex_scr = [len(ex.scratch) for ex in exchanges]

    def split(refs, counts):
        parts, pos = [], 0
        for cnt in counts:
            parts.append(refs[pos:pos + cnt])
            pos += cnt
        return parts

    def wrapped(*refs):
        ins, xin, outs, xout, scr, xscr = split(refs, [n_in, sum(ex_in), n_out, sum(ex_out), n_scr, sum(ex_scr)])
        bound = list(zip(exchanges, split(xin, ex_in), split(xout, ex_out), split(xscr, ex_scr)))
        step = pl.program_id(0)

        def phase(method, at):
            @pl.when(step == at)
            def _():
                for ex, i_refs, o_refs, sems in bound:
                    getattr(ex, method)(i_refs, o_refs, sems)

        phase("begin", 0)
        phase("middle", (3 * steps) // 4)
        body(*ins, *outs, *scr)
        phase("end", steps - 1)

    any_spec = pl.BlockSpec(memory_space=pl.ANY)
    results = pl.pallas_call(
        wrapped, name=name, grid=grid,
        in_specs=list(in_specs) + [any_spec] * sum(ex_in),
        out_specs=list(out_specs) + [any_spec] * sum(ex_out),
        out_shape=list(out_shape) + [s for ex in exchanges for s in ex.out_shape],
        scratch_shapes=list(scratch_shapes) + [s for ex in exchanges for s in ex.scratch],
        compiler_params=_cparams(1),
    )(*args, *[a for ex in exchanges for a in ex.operands])
    return results[:n_out], split(results[n_out:], ex_out)


def _exchanges_alone(exchanges, name):
    def body(flag_ref):
        flag_ref[...] = jnp.zeros_like(flag_ref)

    _, ex_outs = _call(body, [], name=name, grid=(1,), in_specs=[], out_specs=[pl.BlockSpec((8, LANES), lambda i: (0, 0))],
                       out_shape=[jax.ShapeDtypeStruct((8, LANES), F32)], exchanges=exchanges)
    return ex_outs


def _mod_fwd(c_all, ada_w):
    cols = ada_w.shape[2]

    def body(c_ref, w_ref, mp_ref, act_ref):
        c = c_ref[...]
        act = c * jax.nn.sigmoid(c)
        act_ref[...] = act
        mp_ref[0] = _dot(act.astype(BF16), w_ref[0].astype(BF16))

    return pl.pallas_call(
        body, name="mod_fwd", grid=(DEPTH,),
        in_specs=[pl.BlockSpec((N_DEV, D_MODEL), lambda l: (0, 0)),
                  pl.BlockSpec((1, D_MODEL, cols), lambda l: (l, 0, 0))],
        out_specs=[pl.BlockSpec((1, N_DEV, cols), lambda l: (l, 0, 0)),
                   pl.BlockSpec((N_DEV, D_MODEL), lambda l: (0, 0))],
        out_shape=[jax.ShapeDtypeStruct((DEPTH, N_DEV, cols), F32), jax.ShapeDtypeStruct((N_DEV, D_MODEL), F32)],
        compiler_params=_cparams(1),
    )(c_all, ada_w)


def _ada_grad(c_act, dmod_cols):
    cols = dmod_cols.shape[2]

    def body(act_ref, dm_ref, g_ref):
        g_ref[0] = _dot(act_ref[...].astype(BF16), dm_ref[0].astype(BF16), TN)

    return pl.pallas_call(
        body, name="ada_grad", grid=(DEPTH,),
        in_specs=[pl.BlockSpec((N_DEV, D_MODEL), lambda l: (0, 0)),
                  pl.BlockSpec((1, N_DEV, cols), lambda l: (l, 0, 0))],
        out_specs=pl.BlockSpec((1, D_MODEL, cols), lambda l: (l, 0, 0)),
        out_shape=jax.ShapeDtypeStruct((DEPTH, D_MODEL, cols), F32),
        compiler_params=_cparams(1),
    )(c_act, dmod_cols)


def _rotate(t, cos, sin_lo, sin_hi):
    return t * cos + pltpu.roll(t, LANES - 8, 1) * sin_lo + pltpu.roll(t, 8, 1) * sin_hi


def _rotate_bwd(t, cos, sin_lo, sin_hi):
    return t * cos + pltpu.roll(t * sin_lo, 8, 1) + pltpu.roll(t * sin_hi, LANES - 8, 1)


def _fwd_in(x, vec, w_in_t, b_in, rot, name, exchanges=()):
    s = x.shape[0]
    ts = _seq_tile(s, 2)

    def body(x_ref, vec_ref, w_ref, b_ref, cos_ref, lo_ref, hi_ref, q_ref, k_ref, v_ref, u_ref):
        xt = x_ref[...]
        h = xt * _rsqrt_ms(xt) * vec_ref[0:1, :] * vec_ref[1:2, :] + vec_ref[2:3, :]
        proj = _dot(h.astype(BF16), w_ref[...], NT) + b_ref[0:1, :]
        cos, lo, hi = cos_ref[...], lo_ref[...], hi_ref[...]
        for j in range(ATTN_WIDTH // LANES):
            q_ref[:, j * LANES:(j + 1) * LANES] = (
                _rotate(proj[:, j * LANES:(j + 1) * LANES], cos, lo, hi) * Q_SCALE).astype(BF16)
        k_ref[...] = _rotate(proj[:, ATTN_WIDTH:ATTN_WIDTH + KV_WIDTH], cos, lo, hi).astype(BF16)
        v_ref[...] = proj[:, ATTN_WIDTH + KV_WIDTH:ATTN_WIDTH + 2 * KV_WIDTH].astype(BF16)
        u_ref[...] = proj[:, ATTN_WIDTH + 2 * KV_WIDTH:]

    row = lambda w: pl.BlockSpec((ts, w), lambda i: (i, 0))
    return _call(
        body, [x, vec, w_in_t, b_in, *rot], name=name, grid=(s // ts,),
        in_specs=[row(D_MODEL), _resident((8, D_MODEL)), _resident((IN_WIDTH, D_MODEL)), _resident((8, IN_WIDTH)),
                  row(LANES), row(LANES), row(LANES)],
        out_specs=[row(ATTN_WIDTH), row(KV_WIDTH), row(KV_WIDTH), row(POOL_WIDTH)],
        out_shape=[jax.ShapeDtypeStruct((s, ATTN_WIDTH), BF16), jax.ShapeDtypeStruct((s, KV_WIDTH), BF16),
                   jax.ShapeDtypeStruct((s, KV_WIDTH), BF16), jax.ShapeDtypeStruct((s, POOL_WIDTH), F32)],
        exchanges=exchanges)


def _band_mask_t(first_block):
    key = lax.broadcasted_iota(jnp.int32, (2 * BLOCK, 2 * BLOCK), 0)
    qry = lax.broadcasted_iota(jnp.int32, (2 * BLOCK, 2 * BLOCK), 1) & (BLOCK - 1)
    d = key - qry
    return (d >= 1) & (d <= BLOCK) & ((key >= BLOCK) | jnp.logical_not(first_block))


def _placed(both, kh, b):
    lane = lax.broadcasted_iota(jnp.int32, (1, LANES), 1)
    src = both if kh == b else pltpu.roll(both, HEAD_DIM, 1)
    return jnp.where((lane >= b * HEAD_DIM) & (lane < (b + 1) * HEAD_DIM), src, jnp.zeros_like(src))


def _pair(ref, rows, kh, r):
    j = 2 * kh + r
    return ref[rows, j * LANES:(j + 1) * LANES]


def _attn_specs(s):
    qb = min(ATTN_BLOCKS, s // BLOCK)
    cur = lambda w: pl.BlockSpec((qb * BLOCK, w), lambda i: (i, 0))
    prev = lambda w: pl.BlockSpec((BLOCK, w), lambda i: (jnp.maximum(i * qb - 1, 0), 0))
    return qb, cur, prev, pl.BlockSpec((N_HEADS, qb * BLOCK), lambda i: (0, i))


def _kv_window(prev_ref, cur_ref, jb):
    before = prev_ref[...] if jb == 0 else cur_ref[(jb - 1) * BLOCK:jb * BLOCK, :]
    return jnp.concatenate([before, cur_ref[jb * BLOCK:(jb + 1) * BLOCK, :]], axis=0)


def _sink_row(sink_ref, kh, b):
    lane = lax.broadcasted_iota(jnp.int32, (1, 2 * BLOCK), 1)
    return jnp.where(lane < BLOCK, sink_ref[GROUP * kh + b], sink_ref[GROUP * kh + 2 + b])


def _attn_fwd_t(sinks, q, k, v, name, exchanges=()):
    s = q.shape[0]
    qb, cur, prev, lse_spec = _attn_specs(s)

    def block(jb, sink_ref, q_ref, kc_ref, kp_ref, vc_ref, vp_ref, o_ref, l_ref):
        rows = slice(jb * BLOCK, (jb + 1) * BLOCK)
        valid = _band_mask_t(pl.program_id(0) == 0 if jb == 0 else False)
        k_both, v_both = _kv_window(kp_ref, kc_ref, jb), _kv_window(vp_ref, vc_ref, jb)
        for kh in range(N_KV_HEADS):
            qg = jnp.concatenate([_pair(q_ref, rows, kh, 0), _pair(q_ref, rows, kh, 1)], axis=0)
            keys = jnp.concatenate([_placed(k_both, kh, 0), _placed(k_both, kh, 1)], axis=0)
            st = _dot(keys, qg, NT)
            out = jnp.zeros((2 * BLOCK, LANES), F32)
            for b in range(2):
                sc = jnp.where(valid, st[b * 2 * BLOCK:(b + 1) * 2 * BLOCK, :], NEG_INF)
                sink = _sink_row(sink_ref, kh, b)
                m = jnp.maximum(jnp.max(sc, axis=0, keepdims=True), sink)
                p = jnp.exp(sc - m)
                den = jnp.sum(p, axis=0, keepdims=True) + jnp.exp(sink - m)
                out = out + _dot((p * (1.0 / den)).astype(BF16), _placed(v_both, kh, b), TN)
                lse = m + jnp.log(den)
                for r in range(2):
                    h = GROUP * kh + 2 * r + b
                    l_ref[h:h + 1, rows] = lse[:, r * BLOCK:(r + 1) * BLOCK]
            for r in range(2):
                j = 2 * kh + r
                o_ref[rows, j * LANES:(j + 1) * LANES] = out[r * BLOCK:(r + 1) * BLOCK, :].astype(BF16)

    def body(*refs):
        for jb in range(qb):
            block(jb, *refs)

    return _call(
        body, [sinks, q, k, k, v, v], name=name, grid=(s // (qb * BLOCK),),
        in_specs=[pl.BlockSpec(memory_space=pltpu.SMEM), cur(ATTN_WIDTH), cur(KV_WIDTH), prev(KV_WIDTH),
                  cur(KV_WIDTH), prev(KV_WIDTH)],
        out_specs=[cur(ATTN_WIDTH), lse_spec],
        out_shape=[jax.ShapeDtypeStruct((s, ATTN_WIDTH), BF16), jax.ShapeDtypeStruct((N_HEADS, s), F32)],
        exchanges=exchanges)


def _attn_bwd_t(sinks, q, do, lse, k, v, name, exchanges=()):
    s = q.shape[0]
    qb, cur, prev, lse_spec = _attn_specs(s)

    def block(jb, sink_ref, q_ref, do_ref, l_ref, kc_ref, kp_ref, vc_ref, vp_ref,
              dq_ref, dkc_ref, dkp_ref, dvc_ref, dvp_ref, dsink_ref):
        rows = slice(jb * BLOCK, (jb + 1) * BLOCK)
        valid = _band_mask_t(pl.program_id(0) == 0 if jb == 0 else False)
        k_both, v_both = _kv_window(kp_ref, kc_ref, jb), _kv_window(vp_ref, vc_ref, jb)
        lane = lax.broadcasted_iota(jnp.int32, (1, LANES), 1)
        col = lax.broadcasted_iota(jnp.int32, (1, 2 * BLOCK), 1)
        dk_heads, dv_heads = [], []
        for kh in range(N_KV_HEADS):
            qg = jnp.concatenate([_pair(q_ref, rows, kh, 0), _pair(q_ref, rows, kh, 1)], axis=0)
            dog = jnp.concatenate([_pair(do_ref, rows, kh, 0), _pair(do_ref, rows, kh, 1)], axis=0)
            k_placed = [_placed(k_both, kh, b) for b in range(2)]
            v_placed = [_placed(v_both, kh, b) for b in range(2)]
            st = _dot(jnp.concatenate(k_placed, axis=0), qg, NT)
            dpt = _dot(jnp.concatenate(v_placed, axis=0), dog, NT)
            dqg = jnp.zeros((2 * BLOCK, LANES), F32)
            dk_b, dv_b = [], []
            for b in range(2):
                part = slice(b * 2 * BLOCK, (b + 1) * 2 * BLOCK)
                heads = [GROUP * kh + 2 * r + b for r in range(2)]
                lse_row = jnp.concatenate([l_ref[h:h + 1, rows] for h in heads], axis=1)
                p = jnp.where(valid, jnp.exp(st[part, :] - lse_row), 0.0)
                dp = dpt[part, :]
                delta = jnp.sum(p * dp, axis=0, keepdims=True)
                sink_pull = jnp.exp(_sink_row(sink_ref, kh, b) - lse_row) * delta
                dsink_ref[heads[0]:heads[0] + 1, :] += -jnp.sum(jnp.where(col < BLOCK, sink_pull, 0.0))
                dsink_ref[heads[1]:heads[1] + 1, :] += -jnp.sum(jnp.where(col < BLOCK, 0.0, sink_pull))
                ds = (p * (dp - delta)).astype(BF16)
                dqg = dqg + _dot(ds, k_placed[b], TN)
                dk_b.append(_dot(ds, qg))
                dv_b.append(_dot(p.astype(BF16), dog))
            for parts, total in ((dk_b, dk_heads), (dv_b, dv_heads)):
                kept = jnp.where(lane < HEAD_DIM, parts[0], parts[1])
                total.append(kept + pltpu.roll(kept, HEAD_DIM, 1))
            for r in range(2):
                j = 2 * kh + r
                dq_ref[rows, j * LANES:(j + 1) * LANES] = dqg[r * BLOCK:(r + 1) * BLOCK, :] * Q_SCALE
        dk = jnp.where(lane < HEAD_DIM, dk_heads[0], dk_heads[1])
        dv = jnp.where(lane < HEAD_DIM, dv_heads[0], dv_heads[1])
        dkp_ref[rows, :] = dk[0:BLOCK, :]
        dkc_ref[rows, :] = dk[BLOCK:, :]
        dvp_ref[rows, :] = dv[0:BLOCK, :]
        dvc_ref[rows, :] = dv[BLOCK:, :]

    def body(*refs):
        @pl.when(pl.program_id(0) == 0)
        def _():
            refs[-1][...] = jnp.zeros_like(refs[-1])

        for jb in range(qb):
            block(jb, *refs)

    kv = jax.ShapeDtypeStruct((s, KV_WIDTH), F32)
    return _call(
        body, [sinks, q, do, lse, k, k, v, v], name=name, grid=(s // (qb * BLOCK),),
        in_specs=[pl.BlockSpec(memory_space=pltpu.SMEM), cur(ATTN_WIDTH), cur(ATTN_WIDTH), lse_spec,
                  cur(KV_WIDTH), prev(KV_WIDTH), cur(KV_WIDTH), prev(KV_WIDTH)],
        out_specs=[cur(ATTN_WIDTH), cur(KV_WIDTH), cur(KV_WIDTH), cur(KV_WIDTH), cur(KV_WIDTH),
                   pl.BlockSpec((8, LANES), lambda i: (0, 0))],
        out_shape=[jax.ShapeDtypeStruct((s, ATTN_WIDTH), F32), kv, kv, kv, kv, jax.ShapeDtypeStruct((8, LANES), F32)],
        exchanges=exchanges)


def _pool_counts(tile, ts):
    t = (tile * ts + lax.broadcasted_iota(jnp.int32, (ts, 1), 0) + 1).astype(F32)
    return [jnp.minimum(t, float(w)) for w in POOL_WINDOWS]


def _pooled(u, halo, counts, gi):
    cols = slice(gi * POOL_GROUP_WIDTH, (gi + 1) * POOL_GROUP_WIDTH)
    acc = jnp.concatenate([halo[:, cols], u[:, cols]], axis=0)
    shift = 1
    while shift < POOL_WINDOWS[gi]:
        acc = acc + pltpu.roll(acc, shift, 0)
        shift *= 2
    return acc[POOL_HALO:, :] / counts[gi] - u[:, cols]


def _fwd_out(ao, u, pool_w, psc, w_out, x, vec, name, exchanges=()):
    s = x.shape[0]
    ts = _seq_tile(s, 2)
    hb = ts // POOL_HALO

    def body(ao_ref, u_ref, uh_ref, pw_ref, psc_ref, w_ref, x_ref, vec_ref, x1_ref, mix_ref):
        i = pl.program_id(0)
        u_t = u_ref[...]
        halo = jnp.where(i > 0, uh_ref[...], 0.0)
        counts = _pool_counts(i, ts)
        mix = _dot(ao_ref[...], w_ref[0:ATTN_WIDTH, :])
        for gi in range(len(POOL_WINDOWS)):
            cols = slice(gi * POOL_GROUP_WIDTH, (gi + 1) * POOL_GROUP_WIDTH)
            og = _dot(_pooled(u_t, halo, counts, gi).astype(BF16), pw_ref[gi]) * psc_ref[0:1, cols]
            mix = mix + _dot(og.astype(BF16), w_ref[ATTN_WIDTH + gi * POOL_GROUP_WIDTH:
                                                    ATTN_WIDTH + (gi + 1) * POOL_GROUP_WIDTH, :])
        mix_ref[...] = mix
        x1_ref[...] = x_ref[...] + vec_ref[0:1, :] * (mix * _rsqrt_ms(mix) * vec_ref[1:2, :])

    row = lambda w: pl.BlockSpec((ts, w), lambda i: (i, 0))
    return _call(
        body, [ao, u, u, pool_w, psc, w_out, x, vec], name=name, grid=(s // ts,),
        in_specs=[row(ATTN_WIDTH), row(POOL_WIDTH),
                  pl.BlockSpec((POOL_HALO, POOL_WIDTH), lambda i: (jnp.maximum(i * hb - 1, 0), 0)),
                  _resident(pool_w.shape), _resident((8, POOL_WIDTH)), _resident((D_MODEL, D_MODEL)),
                  row(D_MODEL), _resident((8, D_MODEL))],
        out_specs=[row(D_MODEL), row(D_MODEL)],
        out_shape=[jax.ShapeDtypeStruct((s, D_MODEL), F32), jax.ShapeDtypeStruct((s, D_MODEL), F32)],
        exchanges=exchanges)


def _ffn_fwd(x1, vec, w_gate_t, w_up_t, w_down, name, exchanges=(), target=None):
    s = x1.shape[0]
    ts = _seq_tile(s)

    def body(x_ref, *refs):
        if target is None:
            vec_ref, wg_ref, wu_ref, wd_ref, x2_ref, f_ref, gt_ref, up_ref, h2_ref = refs
        else:
            t_ref, vec_ref, wg_ref, wu_ref, wd_ref, x2_ref, f_ref, gt_ref, up_ref, h2_ref, loss_ref = refs
        xt = x_ref[...]
        h2 = (xt * _rsqrt_ms(xt) * vec_ref[0:1, :] * vec_ref[1:2, :] + vec_ref[2:3, :]).astype(BF16)
        h2_ref[...] = h2
        f = jnp.zeros((ts, D_MODEL), F32)
        for c in range(D_FF // FF_CHUNK):
            cols = slice(c * FF_CHUNK, (c + 1) * FF_CHUNK)
            g = _dot(h2, wg_ref[cols, :], NT)
            up = _dot(h2, wu_ref[cols, :], NT)
            act = ((g * jax.nn.sigmoid(g)) * up).astype(BF16)
            gt_ref[:, cols] = g.astype(BF16)
            up_ref[:, cols] = up.astype(BF16)
            f = f + _dot(act, wd_ref[cols, :])
        f_ref[...] = f
        x2 = xt + vec_ref[3:4, :] * (f * _rsqrt_ms(f) * vec_ref[4:5, :])
        if target is None:
            x2_ref[...] = x2
        else:
            @pl.when(pl.program_id(0) == 0)
            def _():
                loss_ref[...] = jnp.zeros_like(loss_ref)

            diff = x2 - t_ref[...]
            x2_ref[...] = diff / D_MODEL
            loss_ref[...] += 0.5 * jnp.sum(jnp.mean(diff * diff, axis=-1, keepdims=True))

    row = lambda w: pl.BlockSpec((ts, w), lambda i: (i, 0))
    wide = jax.ShapeDtypeStruct((s, D_FF), BF16)
    with_loss = target is not None
    return _call(
        body, [x1] + ([target] if with_loss else []) + [vec, w_gate_t, w_up_t, w_down], name=name, grid=(s // ts,),
        in_specs=[row(D_MODEL)] * (2 if with_loss else 1) + [
            _resident((8, D_MODEL)), _resident((D_FF, D_MODEL)), _resident((D_FF, D_MODEL)), _resident((D_FF, D_MODEL))],
        out_specs=[row(D_MODEL), row(D_MODEL), row(D_FF), row(D_FF), row(D_MODEL)] + (
            [pl.BlockSpec((8, LANES), lambda i: (0, 0))] if with_loss else []),
        out_shape=[jax.ShapeDtypeStruct((s, D_MODEL), F32), jax.ShapeDtypeStruct((s, D_MODEL), F32), wide, wide,
                   jax.ShapeDtypeStruct((s, D_MODEL), BF16)] + (
            [jax.ShapeDtypeStruct((8, LANES), F32)] if with_loss else []),
        exchanges=exchanges)


def _norm_bwd(dy_hat, x_hat, r):
    return r * (dy_hat - x_hat * jnp.mean(dy_hat * x_hat, axis=-1, keepdims=True))


def _accumulate_rows(ref, rows):
    for j, val in enumerate(rows):
        ref[j:j + 1, :] += val


def _ffn_bwd_act(dx2, f, gt, up, vec, w_down, name, exchanges=()):
    s = dx2.shape[0]
    ts = _seq_tile(s)

    def body(dx2_ref, f_ref, gt_ref, up_ref, vec_ref, wd_ref, dgt_ref, dup_ref, dwd_ref, cs_ref):
        @pl.when(pl.program_id(0) == 0)
        def _():
            cs_ref[...] = jnp.zeros_like(cs_ref)
            dwd_ref[...] = jnp.zeros_like(dwd_ref)

        dx2_t, f_t = dx2_ref[...], f_ref[...]
        gate, g_post = vec_ref[3:4, :], vec_ref[4:5, :]
        rf = _rsqrt_ms(f_t)
        f_hat = f_t * rf
        df = _norm_bwd(dx2_t * gate * g_post, f_hat, rf).astype(BF16)
        _accumulate_rows(cs_ref, [_colsum(dx2_t * (f_hat * g_post)), _colsum(dx2_t * gate * f_hat)])
        for c in range(D_FF // FF_CHUNK):
            cols = slice(c * FF_CHUNK, (c + 1) * FF_CHUNK)
            dact = _dot(df, wd_ref[cols, :], NT)
            g, u_t = gt_ref[:, cols].astype(F32), up_ref[:, cols].astype(F32)
            sg = 0.5 * jnp.tanh(0.5 * g) + 0.5
            silu = g * sg
            dgt_ref[:, cols] = (dact * u_t * (sg * (1.0 + g * (1.0 - sg)))).astype(BF16)
            dup_ref[:, cols] = (dact * silu).astype(BF16)
            dwd_ref[cols, :] += _dot((silu * u_t).astype(BF16), df, TN)

    row = lambda w: pl.BlockSpec((ts, w), lambda i: (i, 0))
    wide = jax.ShapeDtypeStruct((s, D_FF), BF16)
    return _call(
        body, [dx2, f, gt, up, vec, w_down], name=name, grid=(s // ts,),
        in_specs=[row(D_MODEL), row(D_MODEL), row(D_FF), row(D_FF), _resident((8, D_MODEL)), _resident((D_FF, D_MODEL))],
        out_specs=[row(D_FF), row(D_FF), _resident((D_FF, D_MODEL)), pl.BlockSpec((8, D_MODEL), lambda i: (0, 0))],
        out_shape=[wide, wide, jax.ShapeDtypeStruct((D_FF, D_MODEL), F32), jax.ShapeDtypeStruct((8, D_MODEL), F32)],
        exchanges=exchanges)


def _ffn_bwd_in(dx2, x1, dgt, dup, vec, w_gate_t, w_up_t, name, exchanges=()):
    s = x1.shape[0]
    ts = _seq_tile(s)

    def body(dx2_ref, x1_ref, dgt_ref, dup_ref, vec_ref, wg_ref, wu_ref, dx1_ref, cs_ref):
        @pl.when(pl.program_id(0) == 0)
        def _():
            cs_ref[...] = jnp.zeros_like(cs_ref)

        g_pre, one_scale = vec_ref[0:1, :], vec_ref[1:2, :]
        dh2 = _dot(dgt_ref[...], wg_ref[...]) + _dot(dup_ref[...], wu_ref[...])
        x1_t = x1_ref[...]
        r1 = _rsqrt_ms(x1_t)
        x_hat = x1_t * r1
        dx1_ref[...] = dx2_ref[...] + _norm_bwd(dh2 * g_pre * one_scale, x_hat, r1)
        _accumulate_rows(cs_ref, [_colsum(dh2), _colsum(dh2 * (x_hat * g_pre)), _colsum(dh2 * one_scale * x_hat)])

    row = lambda w: pl.BlockSpec((ts, w), lambda i: (i, 0))
    return _call(
        body, [dx2, x1, dgt, dup, vec, w_gate_t, w_up_t], name=name, grid=(s // ts,),
        in_specs=[row(D_MODEL), row(D_MODEL), row(D_FF), row(D_FF), _resident((8, D_MODEL)),
                  _resident((D_FF, D_MODEL)), _resident((D_FF, D_MODEL))],
        out_specs=[row(D_MODEL), pl.BlockSpec((8, D_MODEL), lambda i: (0, 0))],
        out_shape=[jax.ShapeDtypeStruct((s, D_MODEL), F32), jax.ShapeDtypeStruct((8, D_MODEL), F32)],
        exchanges=exchanges)


def _weight_grad(a, b, name, exchanges=()):
    s, m = a.shape
    n = b.shape[1]
    tk = min(s, 2 * SEQ_TILE)
    steps = s // tk

    def body(a_ref, b_ref, o_ref, acc_ref):
        i = pl.program_id(0)

        @pl.when(i == 0)
        def _():
            acc_ref[...] = jnp.zeros_like(acc_ref)

        b_t = b_ref[...]
        for c in range(m // FF_CHUNK):
            rows = slice(c * FF_CHUNK, (c + 1) * FF_CHUNK)
            acc_ref[rows, :] += _dot(a_ref[:, rows], b_t, TN)

        @pl.when(i == steps - 1)
        def _():
            o_ref[...] = acc_ref[...].astype(BF16)

    (grad,), ex_outs = _call(
        body, [a, b], name=name, grid=(steps,),
        in_specs=[pl.BlockSpec((tk, m), lambda i: (i, 0)), pl.BlockSpec((tk, n), lambda i: (i, 0))],
        out_specs=[_resident((m, n))],
        out_shape=[jax.ShapeDtypeStruct((m, n), BF16)],
        scratch_shapes=[pltpu.VMEM((m, n), F32)],
        exchanges=exchanges)
    return grad, ex_outs


def _out_bwd(dx1, mix, ao, u, pool_w, psc, w_out, vec, name, exchanges=()):
    s = dx1.shape[0]
    ts = _seq_tile(s)
    nt = s // ts
    hb = ts // POOL_HALO
    ng = len(POOL_WINDOWS)

    def body(dx1_ref, mix_ref, ao_ref, u_ref, uh_ref, pw_ref, psc_ref, w_ref, vec_ref,
             do_ref, du_ref, dw_out_ref, cs_ref, dpw_ref, dpsc_ref, carry_ref, dw_ref):
        i = pl.program_id(0)
        tile = nt - 1 - i

        @pl.when(i == 0)
        def _():
            dw_ref[...] = jnp.zeros_like(dw_ref)
            cs_ref[...] = jnp.zeros_like(cs_ref)
            dpw_ref[...] = jnp.zeros_like(dpw_ref)
            dpsc_ref[...] = jnp.zeros_like(dpsc_ref)
            carry_ref[...] = jnp.zeros_like(carry_ref)

        dx1_t, mix_t = dx1_ref[...], mix_ref[...]
        gate, g_post = vec_ref[0:1, :], vec_ref[1:2, :]
        rm = _rsqrt_ms(mix_t)
        m_hat = mix_t * rm
        dmix = _norm_bwd(dx1_t * gate * g_post, m_hat, rm).astype(BF16)
        _accumulate_rows(cs_ref, [_colsum(dx1_t * (m_hat * g_post)), _colsum(dx1_t * gate * m_hat)])
        d_cat = _dot(dmix, w_ref[...], NT)
        do_ref[...] = d_cat[:, 0:ATTN_WIDTH].astype(BF16)

        u_t = u_ref[...]
        halo = jnp.where(tile > 0, uh_ref[...], 0.0)
        counts = _pool_counts(tile, ts)
        cat = [ao_ref[...]]
        for gi in range(ng):
            cols = slice(gi * POOL_GROUP_WIDTH, (gi + 1) * POOL_GROUP_WIDTH)
            scale = psc_ref[0:1, cols]
            pooled = _pooled(u_t, halo, counts, gi).astype(BF16)
            og = _dot(pooled, pw_ref[gi])
            cat.append((og * scale).astype(BF16))
            d_out = d_cat[:, ATTN_WIDTH + gi * POOL_GROUP_WIDTH:ATTN_WIDTH + (gi + 1) * POOL_GROUP_WIDTH]
            dpsc_ref[0:1, cols] += _colsum(d_out * og)
            d_og = (d_out * scale).astype(BF16)
            dpw_ref[gi] += _dot(pooled, d_og, TN)
            d_pooled = _dot(d_og, pw_ref[gi], NT)
            spread = d_pooled / counts[gi]
            acc = jnp.concatenate([spread, carry_ref[:, cols]], axis=0)
            shift = 1
            while shift < POOL_WINDOWS[gi]:
                acc = acc + pltpu.roll(acc, ts + POOL_HALO - shift, 0)
                shift *= 2
            du_ref[:, cols] = acc[0:ts, :] - d_pooled
            carry_ref[:, cols] = spread[0:POOL_HALO, :]
        dw_ref[...] += _dot(jnp.concatenate(cat, axis=1), dmix, TN)

        @pl.when(i == nt - 1)
        def _():
            dw_out_ref[...] = dw_ref[...].astype(BF16)

    row = lambda w: pl.BlockSpec((ts, w), lambda i: (nt - 1 - i, 0))
    fixed = lambda shape: pl.BlockSpec(shape, lambda i: (0,) * len(shape))
    return _call(
        body, [dx1, mix, ao, u, u, pool_w, psc, w_out, vec], name=name, grid=(nt,),
        in_specs=[row(D_MODEL), row(D_MODEL), row(ATTN_WIDTH), row(POOL_WIDTH),
                  pl.BlockSpec((POOL_HALO, POOL_WIDTH), lambda i: (jnp.maximum((nt - 1 - i) * hb - 1, 0), 0)),
                  _resident(pool_w.shape), _resident((8, POOL_WIDTH)), _resident((D_MODEL, D_MODEL)),
                  _resident((8, D_MODEL))],
        out_specs=[row(ATTN_WIDTH), row(POOL_WIDTH), _resident((D_MODEL, D_MODEL)), fixed((8, D_MODEL)),
                   fixed(pool_w.shape), fixed((8, POOL_WIDTH))],
        out_shape=[jax.ShapeDtypeStruct((s, ATTN_WIDTH), BF16), jax.ShapeDtypeStruct((s, POOL_WIDTH), F32),
                   jax.ShapeDtypeStruct((D_MODEL, D_MODEL), BF16),
                   jax.ShapeDtypeStruct((8, D_MODEL), F32), jax.ShapeDtypeStruct(pool_w.shape, F32),
                   jax.ShapeDtypeStruct((8, POOL_WIDTH), F32)],
        scratch_shapes=[pltpu.VMEM((POOL_HALO, POOL_WIDTH), F32), pltpu.VMEM((D_MODEL, D_MODEL), F32)],
        exchanges=exchanges)


def _in_bwd(dq, dkc, dkp, dvc, dvp, du, x, dx1, vec, w_in_t, rot, name, exchanges=()):
    s = x.shape[0]
    ts = _seq_tile(s)
    nt = s // ts
    bpt = ts // BLOCK

    def body(dq_ref, dkc_ref, dkp_ref, dkn_ref, dvc_ref, dvp_ref, dvn_ref, du_ref, x_ref, dx1_ref, vec_ref, w_ref,
             cos_ref, lo_ref, hi_ref, dx_ref, dw_out_ref, cs_ref, db_ref, dw_ref):
        i = pl.program_id(0)

        @pl.when(i == 0)
        def _():
            dw_ref[...] = jnp.zeros_like(dw_ref)
            cs_ref[...] = jnp.zeros_like(cs_ref)
            db_ref[...] = jnp.zeros_like(db_ref)

        cos, lo, hi = cos_ref[...], lo_ref[...], hi_ref[...]

        def with_next_block(cur_ref, prev_ref, next_ref):
            nxt = jnp.where(i < nt - 1, next_ref[...], 0.0)
            later = nxt if bpt == 1 else jnp.concatenate([prev_ref[BLOCK:, :], nxt], axis=0)
            return cur_ref[...] + later

        pieces = [_rotate_bwd(dq_ref[:, j * LANES:(j + 1) * LANES], cos, lo, hi) for j in range(ATTN_WIDTH // LANES)]
        pieces.append(_rotate_bwd(with_next_block(dkc_ref, dkp_ref, dkn_ref), cos, lo, hi))
        pieces.append(with_next_block(dvc_ref, dvp_ref, dvn_ref))
        pieces.append(du_ref[...])
        dproj = jnp.concatenate(pieces, axis=1)
        db_ref[0:1, :] += _colsum(dproj)
        dproj_b = dproj.astype(BF16)
        dh = _dot(dproj_b, w_ref[...])

        xt = x_ref[...]
        g_pre, one_scale = vec_ref[0:1, :], vec_ref[1:2, :]
        r = _rsqrt_ms(xt)
        x_hat = xt * r
        dw_ref[...] += _dot(dproj_b, (x_hat * g_pre * one_scale + vec_ref[2:3, :]).astype(BF16), TN)
        dx_ref[...] = dx1_ref[...] + _norm_bwd(dh * g_pre * one_scale, x_hat, r)
        _accumulate_rows(cs_ref, [_colsum(dh), _colsum(dh * (x_hat * g_pre)), _colsum(dh * one_scale * x_hat)])

        @pl.when(i == nt - 1)
        def _():
            dw_out_ref[...] = dw_ref[...].astype(BF16)

    row = lambda w: pl.BlockSpec((ts, w), lambda i: (i, 0))
    nxt = pl.BlockSpec((BLOCK, KV_WIDTH), lambda i: (jnp.minimum((i + 1) * bpt, s // BLOCK - 1), 0))
    fixed = lambda shape: pl.BlockSpec(shape, lambda i: (0,) * len(shape))
    return _call(
        body, [dq, dkc, dkp, dkp, dvc, dvp, dvp, du, x, dx1, vec, w_in_t, *rot], name=name, grid=(nt,),
        in_specs=[row(ATTN_WIDTH), row(KV_WIDTH), row(KV_WIDTH), nxt, row(KV_WIDTH), row(KV_WIDTH), nxt,
                  row(POOL_WIDTH), row(D_MODEL), row(D_MODEL), _resident((8, D_MODEL)), _resident((IN_WIDTH, D_MODEL)),
                  row(LANES), row(LANES), row(LANES)],
        out_specs=[row(D_MODEL), _resident((IN_WIDTH, D_MODEL)), fixed((8, D_MODEL)), fixed((8, IN_WIDTH))],
        out_shape=[jax.ShapeDtypeStruct((s, D_MODEL), F32), jax.ShapeDtypeStruct((IN_WIDTH, D_MODEL), BF16),
                   jax.ShapeDtypeStruct((8, D_MODEL), F32), jax.ShapeDtypeStruct((8, IN_WIDTH), F32)],
        scratch_shapes=[pltpu.VMEM((IN_WIDTH, D_MODEL), F32)],
        exchanges=exchanges)


def _row_tile(rows):
    for t in (512, 352, 256, 176, 160, 128, 64, 32, 16, 8):
        if rows % t == 0:
            return t
    raise ValueError(f"no row tile for {rows} rows")


def _sum_parts(parts, first, name):
    n, r, c = parts.shape
    tr = _row_tile(r)

    def body(p_ref, o_ref):
        acc = p_ref[first].astype(F32)
        for k in range(n):
            if k != first:
                acc = acc + p_ref[k].astype(F32)
        o_ref[...] = acc

    return pl.pallas_call(
        body, name=name, grid=(r // tr,),
        in_specs=[pl.BlockSpec((n, tr, c), lambda i: (0, i, 0))],
        out_specs=pl.BlockSpec((tr, c), lambda i: (i, 0)),
        out_shape=jax.ShapeDtypeStruct((r, c), F32),
        compiler_params=_cparams(1),
    )(parts)


def _adamw(w, g, m, v, name):
    r, c = w.shape
    tr = _row_tile(r)

    def body(w_ref, g_ref, m_ref, v_ref, d_ref, nm_ref, nv_ref):
        g_t = g_ref[...]
        m_new = ADAM_B1 * m_ref[...] + (1.0 - ADAM_B1) * g_t
        v_new = ADAM_B2 * v_ref[...] + (1.0 - ADAM_B2) * (g_t * g_t)
        m_hat = m_new / (1.0 - ADAM_B1 ** ADAM_STEP)
        v_hat = v_new / (1.0 - ADAM_B2 ** ADAM_STEP)
        d_ref[...] = -ADAM_LR * (m_hat / (jnp.sqrt(v_hat) + ADAM_EPS) + ADAM_WD * w_ref[...])
        nm_ref[...] = m_new
        nv_ref[...] = v_new

    spec = pl.BlockSpec((tr, c), lambda i: (i, 0))
    shape = jax.ShapeDtypeStruct((r, c), F32)
    return pl.pallas_call(
        body, name=name, grid=(r // tr,), in_specs=[spec] * 4, out_specs=[spec] * 3, out_shape=[shape] * 3,
        compiler_params=_cparams(1),
    )(w, g, m, v)


SMALL_ROWS = 160


def _pack_small(dmod, gains, b_in, pool_scale, sinks, pool_w):
    pad = lambda a, width: jnp.pad(a, ((0, 0), (0, width - a.shape[1])))
    parts = [dmod.reshape(DEPTH * N_MOD, D_MODEL), gains.reshape(4 * DEPTH, D_MODEL),
             pad(b_in, 2 * D_MODEL).reshape(2 * DEPTH, D_MODEL), pool_scale.reshape(1, D_MODEL),
             pad(sinks.reshape(1, DEPTH * N_HEADS), D_MODEL), pool_w.reshape(-1, D_MODEL)]
    packed = jnp.concatenate(parts, axis=0)
    return jnp.pad(packed, ((0, SMALL_ROWS - packed.shape[0]), (0, 0)))


def _unpack_small(p):
    r = DEPTH * N_MOD
    dmod = p[0:r].reshape(DEPTH, N_MOD * D_MODEL)
    gains = p[r:r + 4 * DEPTH].reshape(4, DEPTH, D_MODEL)
    r += 4 * DEPTH
    b_in = p[r:r + 2 * DEPTH].reshape(DEPTH, 2 * D_MODEL)[:, :IN_WIDTH]
    r += 2 * DEPTH
    pool_scale = p[r].reshape(DEPTH, POOL_WIDTH)
    sinks = p[r + 1, :DEPTH * N_HEADS].reshape(DEPTH, N_HEADS)
    pool_w = p[r + 2:r + 2 + 128].reshape(DEPTH, len(POOL_WINDOWS), POOL_GROUP_WIDTH, POOL_GROUP_WIDTH)
    return dmod, gains, b_in, pool_scale, sinks, pool_w


LAYER_ROWS = 80


def _pack_layer(dmod, gains, b_in, pool_scale, sinks, pool_w):
    misc = jnp.concatenate([pool_scale, sinks, jnp.zeros((D_MODEL - POOL_WIDTH - N_HEADS,), F32)])
    parts = [dmod.reshape(N_MOD, D_MODEL), gains, jnp.pad(b_in, (0, 2 * D_MODEL - IN_WIDTH)).reshape(2, D_MODEL),
             misc.reshape(1, D_MODEL), pool_w.reshape(-1, D_MODEL)]
    packed = jnp.concatenate(parts, axis=0)
    return jnp.pad(packed, ((0, LAYER_ROWS - packed.shape[0]), (0, 0)))


def _unpack_layer(p):
    r = N_MOD + 4
    b_in = p[r:r + 2].reshape(2 * D_MODEL)[:IN_WIDTH]
    pool_w = p[r + 3:r + 3 + 64].reshape(len(POOL_WINDOWS), POOL_GROUP_WIDTH, POOL_GROUP_WIDTH)
    return (p[0:N_MOD].reshape(N_MOD * D_MODEL), p[N_MOD:r], b_in, p[r + 2, :POOL_WIDTH],
            p[r + 2, POOL_WIDTH:POOL_WIDTH + N_HEADS], pool_w)


def _rotary_tables(positions):
    inv_freq = ROPE_THETA ** (-jnp.arange(0, ROT_DIM, 2, dtype=F32) / ROT_DIM)
    half = ROT_DIM // 2
    head_freq = jnp.concatenate([inv_freq, inv_freq, jnp.zeros((HEAD_DIM - ROT_DIM,), F32)])
    lane_freq = jnp.concatenate([head_freq, head_freq])
    ang = positions.reshape(-1).astype(F32)[:, None] * lane_freq[None, :]
    cos, sin = jnp.cos(ang), jnp.sin(ang)
    in_head = lax.broadcasted_iota(jnp.int32, (1, LANES), 1) % HEAD_DIM
    return cos, jnp.where(in_head < half, -sin, 0.0), jnp.where((in_head >= half) & (in_head < ROT_DIM), sin, 0.0)


def kernel(x, c, positions, ada_w, ada_b, w_in, b_in, sinks, pool_w, pool_scale, w_out, w_gate, w_up, w_down, g_pre_mix, g_post_mix, g_pre_ffn, g_post_ffn, loss_target, m_ada_w, m_ada_b, m_w_in, m_b_in, m_sinks, m_pool_w, m_pool_scale, m_w_out, m_w_gate, m_w_up, m_w_down, m_g_pre_mix, m_g_post_mix, m_g_pre_ffn, m_g_post_ffn, v_ada_w, v_ada_b, v_w_in, v_b_in, v_sinks, v_pool_w, v_pool_scale, v_w_out, v_w_gate, v_w_up, v_w_down, v_g_pre_mix, v_g_post_mix, v_g_pre_ffn, v_g_post_ffn):
    me = _index(_mesh_place())
    x0 = x[0]
    target = loss_target[0]
    rot = _rotary_tables(positions)
    ada_cols = ada_w.shape[2]

    shards = []
    for l in range(DEPTH):
        shards += [w_in[l].T.astype(BF16), w_out[l].astype(BF16), w_gate[l].T.astype(BF16), w_up[l].T.astype(BF16),
                   w_down[l].astype(BF16)]
    whole = lambda gathered: [g.reshape(-1, D_MODEL) for g in gathered]
    full = [None] * (5 * DEPTH)
    first = _exchanges_alone([_Gather(shards[0:2] + [jnp.broadcast_to(c, (8, D_MODEL))])], "gather_first")[0]
    full[0:2] = whole(first[0:2])

    mod_part, c_act = _mod_fwd(first[2][:, 0, :], ada_w)
    mod_all = _allgather_vmem(mod_part.reshape(DEPTH * N_DEV, ada_cols), "gather_mod")
    mod_mine = lax.dynamic_slice_in_dim(mod_all.reshape(N_DEV, DEPTH, N_DEV, ada_cols), me, 1, axis=2)[:, :, 0, :]
    mod = jnp.transpose(mod_mine, (1, 0, 2)).reshape(DEPTH, N_MOD * D_MODEL) + ada_b
    mod = mod.reshape(DEPTH, N_MOD, D_MODEL)

    hosted = {("fwd_in", 0): (_Gather(shards[2:3]), [2]), ("attn_fwd", 0): (_Gather(shards[3:4]), [3]),
              ("fwd_out", 0): (_Gather(shards[4:5]), [4]),
              ("ffn_fwd", 0): (_Gather(shards[5:10]), [5, 6, 7, 8, 9])}

    def take(kind, l, ex_outs):
        if (kind, l) in hosted:
            for slot, g in zip(hosted[kind, l][1], whole(ex_outs[0])):
                full[slot] = g

    def beside(kind, l):
        return [hosted[kind, l][0]] if (kind, l) in hosted else []

    pool_w_b = pool_w.astype(BF16)

    saved = []
    xl = x0
    for l in range(DEPTH):
        vec_in = _rows(g_pre_mix[l], 1.0 + mod[l, 1], mod[l, 0], width=D_MODEL)
        vec_out = _rows(mod[l, 2], g_post_mix[l], width=D_MODEL)
        vec_ffn = _rows(g_pre_ffn[l], 1.0 + mod[l, 4], mod[l, 3], mod[l, 5], g_post_ffn[l], width=D_MODEL)
        psc = _rows(pool_scale[l], width=POOL_WIDTH)
        (q, k, v, u), ex = _fwd_in(xl, vec_in, full[5 * l], _rows(b_in[l], width=IN_WIDTH), rot, f"fwd_in_{l}",
                                   beside("fwd_in", l))
        take("fwd_in", l, ex)
        (ao, lse), ex = _attn_fwd_t(sinks[l], q, k, v, f"attn_fwd_{l}", beside("attn_fwd", l))
        take("attn_fwd", l, ex)
        (x1, mix), ex = _fwd_out(ao, u, pool_w_b[l], psc, full[5 * l + 1], xl, vec_out, f"fwd_out_{l}",
                                 beside("fwd_out", l))
        take("fwd_out", l, ex)
        (x2, f, gt, up, h2, *loss_part), ex = _ffn_fwd(
            x1, vec_ffn, full[5 * l + 2], full[5 * l + 3], full[5 * l + 4], f"ffn_fwd_{l}", beside("ffn_fwd", l),
            target=target if l == DEPTH - 1 else None)
        take("ffn_fwd", l, ex)
        saved.append((xl, q, k, v, u, ao, lse, x1, mix, f, gt, up, h2, vec_in, vec_out, vec_ffn, psc))
        xl = x2

    dx = xl
    loss = lax.psum(loss_part[0][0, 0], ("x", "y", "c"))

    stacks = [None] * (5 * DEPTH)
    parts = [None] * (5 * DEPTH)
    small_all = [None] * DEPTH
    ready_stacks, ready_small = [], []

    def leaving(row_budget):
        exs, notes = [], []
        taken = []
        while ready_stacks and stacks[ready_stacks[0]].shape[1] <= row_budget:
            row_budget -= stacks[ready_stacks[0]].shape[1]
            taken.append(ready_stacks.pop(0))
        if taken:
            exs.append(_Scatter([stacks[a] for a in taken]))
            notes.append(("stacks", taken))
        if ready_small and row_budget >= 0:
            exs.append(_Gather([ready_small[0][1]]))
            notes.append(("small", ready_small[0][0]))
            ready_small.clear()
        return exs, notes

    def arrived(notes, ex_outs):
        for (kind, what), outs in zip(notes, ex_outs):
            if kind == "stacks":
                for a, o in zip(what, outs):
                    parts[a] = o
            else:
                small_all[what] = outs[0]

    def grad_ready(a, grad):
        stacks[a] = grad.reshape(N_DEV, -1, D_MODEL)
        ready_stacks.append(a)

    def hosting(row_budget, kernel_fn, *args):
        exs, notes = leaving(row_budget)
        outs, ex_outs = kernel_fn(*args, exs)
        arrived(notes, ex_outs)
        return outs

    for l in reversed(range(DEPTH)):
        w_in_t, w_out_f, w_gate_t, w_up_t, w_down_f = full[5 * l:5 * l + 5]
        xin, q, k, v, u, ao, lse, x1, mix, f, gt, up, h2, vec_in, vec_out, vec_ffn, psc = saved[l]
        dgt, dup, d_w_down, cs_a = hosting(600, _ffn_bwd_act, dx, f, gt, up, vec_ffn, w_down_f, f"ffn_bwd_act_{l}")
        grad_ready(5 * l + 4, d_w_down.astype(BF16))
        dx1, cs_f = hosting(500, _ffn_bwd_in, dx, x1, dgt, dup, vec_ffn, w_gate_t, w_up_t, f"ffn_bwd_in_{l}")
        grad_ready(5 * l + 2, hosting(-1, _weight_grad, dgt, h2, f"grad_w_gate_{l}"))
        grad_ready(5 * l + 3, hosting(-1, _weight_grad, dup, h2, f"grad_w_up_{l}"))
        do, du, d_w_out, cs_o, dpw, dpsc = hosting(500, _out_bwd, dx1, mix, ao, u, pool_w_b[l], psc, w_out_f, vec_out,
                                                   f"out_bwd_{l}")
        grad_ready(5 * l + 1, d_w_out)
        dq, dkc, dkp, dvc, dvp, dsk = hosting(480, _attn_bwd_t, sinks[l], q, do, lse, k, v, f"attn_bwd_{l}")
        dx, d_w_in_t, cs_i, db = hosting(-1, _in_bwd, dq, dkc, dkp, dvc, dvp, du, xin, dx1, vec_in, w_in_t, rot,
                                         f"in_bwd_{l}")
        grad_ready(5 * l, d_w_in_t)
        d_mod = jnp.concatenate([cs_i[0], cs_i[1], cs_o[0], cs_f[0], cs_f[1], cs_a[0]])
        d_gain = jnp.stack([cs_i[2], cs_o[1], cs_f[2], cs_a[1]])
        ready_small.append((l, _pack_layer(d_mod, d_gain, db[0], dpsc[0], dsk[:, 0], dpw)))
    grad_x = dx[None]
    exs, notes = leaving(N_DEV * D_FF)
    arrived(notes, _exchanges_alone(exs, "exchange_last"))

    layer_sums = [_unpack_layer(_sum_parts(small_all[l], 0, f"sum_small_{l}")) for l in range(DEPTH)]
    g_ada_b, g_gains, g_b_in, g_pool_scale, g_sinks, g_pool_w = (
        jnp.stack([layer_sums[l][j] for l in range(DEPTH)], axis=1 if j == 1 else 0) for j in range(6))
    small_sum = _pack_small(g_ada_b, g_gains, g_b_in, g_pool_scale, g_sinks, g_pool_w)
    gains = jnp.stack([g_pre_mix, g_post_mix, g_pre_ffn, g_post_ffn])
    m_gains = jnp.stack([m_g_pre_mix, m_g_post_mix, m_g_pre_ffn, m_g_post_ffn])
    v_gains = jnp.stack([v_g_pre_mix, v_g_post_mix, v_g_pre_ffn, v_g_post_ffn])
    small_step = _adamw(_pack_small(ada_b, gains, b_in, pool_scale, sinks, pool_w), small_sum,
                        _pack_small(m_ada_b, m_gains, m_b_in, m_pool_scale, m_sinks, m_pool_w),
                        _pack_small(v_ada_b, v_gains, v_b_in, v_pool_scale, v_sinks, v_pool_w), "adamw_small")
    small_out = [_unpack_small(p) for p in small_step]

    dmod_all = jnp.stack([small_all[l][:, 0:N_MOD, :].reshape(N_DEV, N_DEV, ada_cols) for l in range(DEPTH)])
    dmod_cols = lax.dynamic_slice_in_dim(dmod_all, me, 1, axis=2)[:, :, 0, :]
    g_ada_w = _ada_grad(c_act, dmod_cols)

    summed = [_sum_parts(parts[a], _Scatter.OWN, f"sum_grads_{a}") for a in range(5 * DEPTH)]
    per_kind = [jnp.stack([summed[5 * l + j] for l in range(DEPTH)]) for j in range(5)]
    g_w_in, g_w_out, g_w_gate, g_w_up, g_w_down = per_kind
    g_w_in, g_w_gate, g_w_up = (jnp.transpose(g, (0, 2, 1)) for g in (g_w_in, g_w_gate, g_w_up))

    def step(w, g, m, v, name):
        flat = lambda a: a.reshape(-1, a.shape[-1])
        return [o.reshape(w.shape) for o in _adamw(flat(w), flat(g), flat(m), flat(v), name)]

    big = {
        "ada_w": (g_ada_w, step(ada_w, g_ada_w, m_ada_w, v_ada_w, "adamw_ada_w")),
        "w_in": (g_w_in, step(w_in, g_w_in, m_w_in, v_w_in, "adamw_w_in")),
        "w_out": (g_w_out, step(w_out, g_w_out, m_w_out, v_w_out, "adamw_w_out")),
        "w_gate": (g_w_gate, step(w_gate, g_w_gate, m_w_gate, v_w_gate, "adamw_w_gate")),
        "w_up": (g_w_up, step(w_up, g_w_up, m_w_up, v_w_up, "adamw_w_up")),
        "w_down": (g_w_down, step(w_down, g_w_down, m_w_down, v_w_down, "adamw_w_down")),
    }

    def ordered(pick_big, pick_small):
        ada_b_, gains_, b_in_, pool_scale_, sinks_, pool_w_ = pick_small
        return [pick_big("ada_w"), ada_b_, pick_big("w_in"), b_in_, sinks_, pool_w_, pool_scale_, pick_big("w_out"),
                pick_big("w_gate"), pick_big("w_up"), pick_big("w_down"), gains_[0], gains_[1], gains_[2], gains_[3]]

    grads = ordered(lambda n: big[n][0], (g_ada_b, g_gains, g_b_in, g_pool_scale, g_sinks, g_pool_w))
    deltas = ordered(lambda n: big[n][1][0], small_out[0])
    new_m = ordered(lambda n: big[n][1][1], small_out[1])
    new_v = ordered(lambda n: big[n][1][2], small_out[2])
    return (loss, grad_x, *grads, *deltas, *new_m, *new_v)
```

```python
import jax
import jax.numpy as jnp
from jax import lax
from jax.experimental import pallas as pl
from jax.experimental.pallas import tpu as pltpu

F32 = jnp.float32
BF16 = jnp.bfloat16

N_DEV = 8
DEPTH = 2
D_MODEL = 1024
HEAD_DIM = 64
N_HEADS = 8
N_KV_HEADS = 2
GROUP = N_HEADS // N_KV_HEADS
ATTN_WIDTH = N_HEADS * HEAD_DIM
KV_WIDTH = N_KV_HEADS * HEAD_DIM
POOL_WIDTH = 512
POOL_WINDOWS = (2, 4, 8, 16)
POOL_GROUP_WIDTH = 128
POOL_HALO = 16
IN_WIDTH = ATTN_WIDTH + 2 * KV_WIDTH + POOL_WIDTH
D_FF = 2816
N_MOD = 6
BLOCK = 128
ROT_DIM = 16
ROPE_THETA = 500000.0
EPS = 1e-6
NEG_INF = -1e30
Q_SCALE = HEAD_DIM ** -0.5

ADAM_LR = 0.001
ADAM_B1 = 0.9
ADAM_B2 = 0.999
ADAM_EPS = 1e-08
ADAM_WD = 0.01
ADAM_STEP = 10

LANES = 128
SEQ_TILE = 512
ATTN_BLOCKS = 2
FF_CHUNK = 256
VMEM_LIMIT = 56 * 1024 * 1024
MESH = pl.DeviceIdType.MESH

NT = (((1,), (1,)), ((), ()))
TN = (((0,), (0,)), ((), ()))


def _dot(a, b, dims=None):
    if dims is None:
        return jnp.dot(a, b, preferred_element_type=F32)
    return lax.dot_general(a, b, dims, preferred_element_type=F32)


def _cparams(n_axes):
    return pltpu.CompilerParams(dimension_semantics=("arbitrary",) * n_axes, vmem_limit_bytes=VMEM_LIMIT)


def _resident(shape):
    zeros = (0,) * len(shape)
    return pl.BlockSpec(shape, lambda *_: zeros, pipeline_mode=pl.Buffered(1))


def _rows(*vectors, width):
    rows = [jnp.reshape(v, (1, width)).astype(F32) for v in vectors]
    rows.append(jnp.zeros((8 - len(rows), width), F32))
    return jnp.concatenate(rows, axis=0)


def _rsqrt_ms(x):
    return lax.rsqrt(jnp.mean(x * x, axis=-1, keepdims=True) + EPS)


def _colsum(x):
    return jnp.sum(x, axis=0, keepdims=True)


def _seq_tile(s, tiles=1):
    return min(s, tiles * SEQ_TILE)


def _mesh_place():
    x, y, c = lax.axis_index("x"), lax.axis_index("y"), lax.axis_index("c")
    return x, y, c


def _flip(place, k):
    x, y, c = place
    return (1 - x if k & 4 else x, 1 - y if k & 2 else y, 1 - c if k & 1 else c)


def _index(place):
    x, y, c = place
    return 4 * x + 2 * y + c


def _allgather_vmem(block, name):
    r, c = block.shape

    def body(x_ref, out_ref, send_sems, recv_sems, local_sem):
        me = _mesh_place()
        mine = pltpu.make_async_copy(x_ref, out_ref.at[_index(me)], local_sem)
        mine.start()

        def copy(k):
            return pltpu.make_async_remote_copy(
                src_ref=x_ref, dst_ref=out_ref.at[_index(me)], send_sem=send_sems.at[k - 1], recv_sem=recv_sems.at[k - 1],
                device_id=_flip(me, k), device_id_type=MESH)

        def arrival(k):
            return pltpu.make_async_remote_copy(
                src_ref=x_ref, dst_ref=out_ref.at[_index(_flip(me, k))], send_sem=send_sems.at[k - 1],
                recv_sem=recv_sems.at[k - 1], device_id=_flip(me, k), device_id_type=MESH)

        for k in range(1, N_DEV):
            copy(k).start()
        for k in range(1, N_DEV):
            arrival(k).wait_recv()
        for k in range(1, N_DEV):
            copy(k).wait_send()
        mine.wait()

    return pl.pallas_call(
        body, name=name,
        out_shape=jax.ShapeDtypeStruct((N_DEV, r, c), block.dtype),
        in_specs=[pl.BlockSpec(memory_space=pltpu.VMEM)],
        out_specs=pl.BlockSpec(memory_space=pltpu.VMEM),
        scratch_shapes=[pltpu.SemaphoreType.DMA((N_DEV - 1,)), pltpu.SemaphoreType.DMA((N_DEV - 1,)),
                        pltpu.SemaphoreType.DMA],
    )(block)


class _Gather:
    def __init__(self, shards):
        n = len(shards)
        self.operands = list(shards)
        self.out_shape = [jax.ShapeDtypeStruct((N_DEV,) + s.shape, s.dtype) for s in shards]
        self.scratch = [pltpu.SemaphoreType.DMA((N_DEV - 1, n)), pltpu.SemaphoreType.DMA((N_DEV - 1, n)),
                        pltpu.SemaphoreType.DMA((n,))]

    def _copies(self, x_refs, out_refs, sems):
        send_sems, recv_sems, local_sems = sems
        n = len(x_refs)
        x, y, c = _mesh_place()
        me, sibling = (x, y, c), (x, y, 1 - c)
        chips = [(1 - x, y), (x, 1 - y), (1 - x, 1 - y)]

        def copy(k, a, block, to, from_input=False):
            rows = out_refs[a].at[_index(block)]
            return pltpu.make_async_remote_copy(
                src_ref=x_refs[a] if from_input else rows, dst_ref=rows,
                send_sem=send_sems.at[k, a], recv_sem=recv_sems.at[k, a], device_id=to, device_id_type=MESH)

        mine = [pltpu.make_async_copy(x_refs[a], out_refs[a].at[_index(me)], local_sems.at[a]) for a in range(n)]
        first = [copy(0, a, me, sibling, from_input=True) for a in range(n)]
        first += [copy(1 + j, a, me, (*chip, c), from_input=True) for j, chip in enumerate(chips) for a in range(n)]
        over_ici = [copy(1 + j, a, (*chip, c), me) for j, chip in enumerate(chips) for a in range(n)]
        passed = [copy(4 + j, a, (*chip, c), sibling) for j, chip in enumerate(chips) for a in range(n)]
        from_sibling = [copy(0, a, sibling, me) for a in range(n)]
        from_sibling += [copy(4 + j, a, (*chip, 1 - c), me) for j, chip in enumerate(chips) for a in range(n)]
        return mine, first, over_ici, passed, from_sibling

    def begin(self, x_refs, out_refs, sems):
        mine, first, _, _, _ = self._copies(x_refs, out_refs, sems)
        for cp in mine + first:
            cp.start()

    def middle(self, x_refs, out_refs, sems):
        _, _, over_ici, passed, _ = self._copies(x_refs, out_refs, sems)
        for arrived, onward in zip(over_ici, passed):
            arrived.wait_recv()
            onward.start()

    def end(self, x_refs, out_refs, sems):
        mine, first, _, passed, from_sibling = self._copies(x_refs, out_refs, sems)
        for cp in from_sibling:
            cp.wait_recv()
        for cp in first + passed:
            cp.wait_send()
        for cp in mine:
            cp.wait()


class _Scatter:
    OWN = N_DEV - 1

    def __init__(self, stacks):
        n = len(stacks)
        self.operands = list(stacks)
        self.out_shape = [jax.ShapeDtypeStruct(s.shape, s.dtype) for s in stacks]
        self.scratch = [pltpu.SemaphoreType.DMA((N_DEV - 1, n)), pltpu.SemaphoreType.DMA((N_DEV - 1, n)),
                        pltpu.SemaphoreType.DMA((n,))]

    def _copies(self, g_refs, out_refs, sems):
        send_sems, recv_sems, local_sems = sems
        me = _mesh_place()

        def copy(k, a):
            peer = _flip(me, k)
            return pltpu.make_async_remote_copy(
                src_ref=g_refs[a].at[_index(peer)], dst_ref=out_refs[a].at[k - 1],
                send_sem=send_sems.at[k - 1, a], recv_sem=recv_sems.at[k - 1, a], device_id=peer, device_id_type=MESH)

        mine = [pltpu.make_async_copy(g_refs[a].at[_index(me)], out_refs[a].at[self.OWN], local_sems.at[a])
                for a in range(len(g_refs))]
        return mine, [copy(k, a) for k in range(1, N_DEV) for a in range(len(g_refs))]

    def begin(self, g_refs, out_refs, sems):
        mine, copies = self._copies(g_refs, out_refs, sems)
        for cp in mine + copies:
            cp.start()

    def middle(self, g_refs, out_refs, sems):
        pass

    def end(self, g_refs, out_refs, sems):
        mine, copies = self._copies(g_refs, out_refs, sems)
        for cp in copies:
            cp.wait_recv()
        for cp in copies:
            cp.wait_send()
        for cp in mine:
            cp.wait()


def _call(body, args, *, name, grid, in_specs, out_specs, out_shape, scratch_shapes=(), exchanges=()):
    if not exchanges:
        outs = pl.pallas_call(body, name=name, grid=grid, in_specs=in_specs, out_specs=out_specs, out_shape=out_shape,
                              scratch_shapes=list(scratch_shapes), compiler_params=_cparams(len(grid)))(*args)
        return outs, []
    (steps,) = grid
    n_in, n_out, n_scr = len(in_specs), len(out_specs), len(scratch_shapes)
    ex_in = [len(ex.operands) for ex in exchanges]
    ex_out = [len(ex.out_shape) for ex in exchanges]
    ex_scr = [len(ex.scratch) for ex in exchanges]

    def split(refs, counts):
        parts, pos = [], 0
        for cnt in counts:
            parts.append(refs[pos:pos + cnt])
            pos += cnt
        return parts

    def wrapped(*refs):
        ins, xin, outs, xout, scr, xscr = split(refs, [n_in, sum(ex_in), n_out, sum(ex_out), n_scr, sum(ex_scr)])
        bound = list(zip(exchanges, split(xin, ex_in), split(xout, ex_out), split(xscr, ex_scr)))
        step = pl.program_id(0)

        def phase(method, at):
            @pl.when(step == at)
            def _():
                for ex, i_refs, o_refs, sems in bound:
                    getattr(ex, method)(i_refs, o_refs, sems)

        phase("begin", 0)
        phase("middle", (3 * steps) // 4)
        body(*ins, *outs, *scr)
        phase("end", steps - 1)

    any_spec = pl.BlockSpec(memory_space=pl.ANY)
    results = pl.pallas_call(
        wrapped, name=name, grid=grid,
        in_specs=list(in_specs) + [any_spec] * sum(ex_in),
        out_specs=list(out_specs) + [any_spec] * sum(ex_out),
        out_shape=list(out_shape) + [s for ex in exchanges for s in ex.out_shape],
        scratch_shapes=list(scratch_shapes) + [s for ex in exchanges for s in ex.scratch],
        compiler_params=_cparams(1),
    )(*args, *[a for ex in exchanges for a in ex.operands])
    return results[:n_out], split(results[n_out:], ex_out)


def _exchanges_alone(exchanges, name):
    def body(flag_ref):
        flag_ref[...] = jnp.zeros_like(flag_ref)

    _, ex_outs = _call(body, [], name=name, grid=(1,), in_specs=[], out_specs=[pl.BlockSpec((8, LANES), lambda i: (0, 0))],
                       out_shape=[jax.ShapeDtypeStruct((8, LANES), F32)], exchanges=exchanges)
    return ex_outs


def _mod_fwd(c_all, ada_w):
    cols = ada_w.shape[2]

    def body(c_ref, w_ref, mp_ref, act_ref):
        c = c_ref[...]
        act = c * jax.nn.sigmoid(c)
        act_ref[...] = act
        mp_ref[0] = _dot(act.astype(BF16), w_ref[0].astype(BF16))

    return pl.pallas_call(
        body, name="mod_fwd", grid=(DEPTH,),
        in_specs=[pl.BlockSpec((N_DEV, D_MODEL), lambda l: (0, 0)),
                  pl.BlockSpec((1, D_MODEL, cols), lambda l: (l, 0, 0))],
        out_specs=[pl.BlockSpec((1, N_DEV, cols), lambda l: (l, 0, 0)),
                   pl.BlockSpec((N_DEV, D_MODEL), lambda l: (0, 0))],
        out_shape=[jax.ShapeDtypeStruct((DEPTH, N_DEV, cols), F32), jax.ShapeDtypeStruct((N_DEV, D_MODEL), F32)],
        compiler_params=_cparams(1),
    )(c_all, ada_w)


def _ada_grad(c_act, dmod_cols):
    cols = dmod_cols.shape[2]

    def body(act_ref, dm_ref, g_ref):
        g_ref[0] = _dot(act_ref[...].astype(BF16), dm_ref[0].astype(BF16), TN)

    return pl.pallas_call(
        body, name="ada_grad", grid=(DEPTH,),
        in_specs=[pl.BlockSpec((N_DEV, D_MODEL), lambda l: (0, 0)),
                  pl.BlockSpec((1, N_DEV, cols), lambda l: (l, 0, 0))],
        out_specs=pl.BlockSpec((1, D_MODEL, cols), lambda l: (l, 0, 0)),
        out_shape=jax.ShapeDtypeStruct((DEPTH, D_MODEL, cols), F32),
        compiler_params=_cparams(1),
    )(c_act, dmod_cols)


def _rotate(t, cos, sin_lo, sin_hi):
    return t * cos + pltpu.roll(t, LANES - 8, 1) * sin_lo + pltpu.roll(t, 8, 1) * sin_hi


def _rotate_bwd(t, cos, sin_lo, sin_hi):
    return t * cos + pltpu.roll(t * sin_lo, 8, 1) + pltpu.roll(t * sin_hi, LANES - 8, 1)


def _fwd_in(x, vec, w_in_t, b_in, rot, name, exchanges=()):
    s = x.shape[0]
    ts = _seq_tile(s, 2)

    def body(x_ref, vec_ref, w_ref, b_ref, cos_ref, lo_ref, hi_ref, q_ref, k_ref, v_ref, u_ref):
        xt = x_ref[...]
        h = xt * _rsqrt_ms(xt) * vec_ref[0:1, :] * vec_ref[1:2, :] + vec_ref[2:3, :]
        proj = _dot(h.astype(BF16), w_ref[...], NT) + b_ref[0:1, :]
        cos, lo, hi = cos_ref[...], lo_ref[...], hi_ref[...]
        for j in range(ATTN_WIDTH // LANES):
            q_ref[:, j * LANES:(j + 1) * LANES] = (
                _rotate(proj[:, j * LANES:(j + 1) * LANES], cos, lo, hi) * Q_SCALE).astype(BF16)
        k_ref[...] = _rotate(proj[:, ATTN_WIDTH:ATTN_WIDTH + KV_WIDTH], cos, lo, hi).astype(BF16)
        v_ref[...] = proj[:, ATTN_WIDTH + KV_WIDTH:ATTN_WIDTH + 2 * KV_WIDTH].astype(BF16)
        u_ref[...] = proj[:, ATTN_WIDTH + 2 * KV_WIDTH:]

    row = lambda w: pl.BlockSpec((ts, w), lambda i: (i, 0))
    return _call(
        body, [x, vec, w_in_t, b_in, *rot], name=name, grid=(s // ts,),
        in_specs=[row(D_MODEL), _resident((8, D_MODEL)), _resident((IN_WIDTH, D_MODEL)), _resident((8, IN_WIDTH)),
                  row(LANES), row(LANES), row(LANES)],
        out_specs=[row(ATTN_WIDTH), row(KV_WIDTH), row(KV_WIDTH), row(POOL_WIDTH)],
        out_shape=[jax.ShapeDtypeStruct((s, ATTN_WIDTH), BF16), jax.ShapeDtypeStruct((s, KV_WIDTH), BF16),
                   jax.ShapeDtypeStruct((s, KV_WIDTH), BF16), jax.ShapeDtypeStruct((s, POOL_WIDTH), F32)],
        exchanges=exchanges)


def _band_mask_t(first_block):
    key = lax.broadcasted_iota(jnp.int32, (2 * BLOCK, 2 * BLOCK), 0)
    qry = lax.broadcasted_iota(jnp.int32, (2 * BLOCK, 2 * BLOCK), 1) & (BLOCK - 1)
    d = key - qry
    return (d >= 1) & (d <= BLOCK) & ((key >= BLOCK) | jnp.logical_not(first_block))


def _placed(both, kh, b):
    lane = lax.broadcasted_iota(jnp.int32, (1, LANES), 1)
    src = both if kh == b else pltpu.roll(both, HEAD_DIM, 1)
    return jnp.where((lane >= b * HEAD_DIM) & (lane < (b + 1) * HEAD_DIM), src, jnp.zeros_like(src))


def _pair(ref, rows, kh, r):
    j = 2 * kh + r
    return ref[rows, j * LANES:(j + 1) * LANES]


def _attn_specs(s):
    qb = min(ATTN_BLOCKS, s // BLOCK)
    cur = lambda w: pl.BlockSpec((qb * BLOCK, w), lambda i: (i, 0))
    prev = lambda w: pl.BlockSpec((BLOCK, w), lambda i: (jnp.maximum(i * qb - 1, 0), 0))
    return qb, cur, prev, pl.BlockSpec((N_HEADS, qb * BLOCK), lambda i: (0, i))


def _kv_window(prev_ref, cur_ref, jb):
    before = prev_ref[...] if jb == 0 else cur_ref[(jb - 1) * BLOCK:jb * BLOCK, :]
    return jnp.concatenate([before, cur_ref[jb * BLOCK:(jb + 1) * BLOCK, :]], axis=0)


def _sink_row(sink_ref, kh, b):
    lane = lax.broadcasted_iota(jnp.int32, (1, 2 * BLOCK), 1)
    return jnp.where(lane < BLOCK, sink_ref[GROUP * kh + b], sink_ref[GROUP * kh + 2 + b])


def _attn_fwd_t(sinks, q, k, v, name, exchanges=()):
    s = q.shape[0]
    qb, cur, prev, lse_spec = _attn_specs(s)

    def block(jb, sink_ref, q_ref, kc_ref, kp_ref, vc_ref, vp_ref, o_ref, l_ref):
        rows = slice(jb * BLOCK, (jb + 1) * BLOCK)
        valid = _band_mask_t(pl.program_id(0) == 0 if jb == 0 else False)
        k_both, v_both = _kv_window(kp_ref, kc_ref, jb), _kv_window(vp_ref, vc_ref, jb)
        for kh in range(N_KV_HEADS):
            qg = jnp.concatenate([_pair(q_ref, rows, kh, 0), _pair(q_ref, rows, kh, 1)], axis=0)
            keys = jnp.concatenate([_placed(k_both, kh, 0), _placed(k_both, kh, 1)], axis=0)
            st = _dot(keys, qg, NT)
            out = jnp.zeros((2 * BLOCK, LANES), F32)
            for b in range(2):
                sc = jnp.where(valid, st[b * 2 * BLOCK:(b + 1) * 2 * BLOCK, :], NEG_INF)
                sink = _sink_row(sink_ref, kh, b)
                m = jnp.maximum(jnp.max(sc, axis=0, keepdims=True), sink)
                p = jnp.exp(sc - m)
                den = jnp.sum(p, axis=0, keepdims=True) + jnp.exp(sink - m)
                out = out + _dot((p * (1.0 / den)).astype(BF16), _placed(v_both, kh, b), TN)
                lse = m + jnp.log(den)
                for r in range(2):
                    h = GROUP * kh + 2 * r + b
                    l_ref[h:h + 1, rows] = lse[:, r * BLOCK:(r + 1) * BLOCK]
            for r in range(2):
                j = 2 * kh + r
                o_ref[rows, j * LANES:(j + 1) * LANES] = out[r * BLOCK:(r + 1) * BLOCK, :].astype(BF16)

    def body(*refs):
        for jb in range(qb):
            block(jb, *refs)

    return _call(
        body, [sinks, q, k, k, v, v], name=name, grid=(s // (qb * BLOCK),),
        in_specs=[pl.BlockSpec(memory_space=pltpu.SMEM), cur(ATTN_WIDTH), cur(KV_WIDTH), prev(KV_WIDTH),
                  cur(KV_WIDTH), prev(KV_WIDTH)],
        out_specs=[cur(ATTN_WIDTH), lse_spec],
        out_shape=[jax.ShapeDtypeStruct((s, ATTN_WIDTH), BF16), jax.ShapeDtypeStruct((N_HEADS, s), F32)],
        exchanges=exchanges)


def _attn_bwd_t(sinks, q, do, lse, k, v, name, exchanges=()):
    s = q.shape[0]
    qb, cur, prev, lse_spec = _attn_specs(s)

    def block(jb, sink_ref, q_ref, do_ref, l_ref, kc_ref, kp_ref, vc_ref, vp_ref,
              dq_ref, dkc_ref, dkp_ref, dvc_ref, dvp_ref, dsink_ref):
        rows = slice(jb * BLOCK, (jb + 1) * BLOCK)
        valid = _band_mask_t(pl.program_id(0) == 0 if jb == 0 else False)
        k_both, v_both = _kv_window(kp_ref, kc_ref, jb), _kv_window(vp_ref, vc_ref, jb)
        lane = lax.broadcasted_iota(jnp.int32, (1, LANES), 1)
        col = lax.broadcasted_iota(jnp.int32, (1, 2 * BLOCK), 1)
        dk_heads, dv_heads = [], []
        for kh in range(N_KV_HEADS):
            qg = jnp.concatenate([_pair(q_ref, rows, kh, 0), _pair(q_ref, rows, kh, 1)], axis=0)
            dog = jnp.concatenate([_pair(do_ref, rows, kh, 0), _pair(do_ref, rows, kh, 1)], axis=0)
            k_placed = [_placed(k_both, kh, b) for b in range(2)]
            v_placed = [_placed(v_both, kh, b) for b in range(2)]
            st = _dot(jnp.concatenate(k_placed, axis=0), qg, NT)
            dpt = _dot(jnp.concatenate(v_placed, axis=0), dog, NT)
            dqg = jnp.zeros((2 * BLOCK, LANES), F32)
            dk_b, dv_b = [], []
            for b in range(2):
                part = slice(b * 2 * BLOCK, (b + 1) * 2 * BLOCK)
                heads = [GROUP * kh + 2 * r + b for r in range(2)]
                lse_row = jnp.concatenate([l_ref[h:h + 1, rows] for h in heads], axis=1)
                p = jnp.where(valid, jnp.exp(st[part, :] - lse_row), 0.0)
                dp = dpt[part, :]
                delta = jnp.sum(p * dp, axis=0, keepdims=True)
                sink_pull = jnp.exp(_sink_row(sink_ref, kh, b) - lse_row) * delta
                dsink_ref[heads[0]:heads[0] + 1, :] += -jnp.sum(jnp.where(col < BLOCK, sink_pull, 0.0))
                dsink_ref[heads[1]:heads[1] + 1, :] += -jnp.sum(jnp.where(col < BLOCK, 0.0, sink_pull))
                ds = (p * (dp - delta)).astype(BF16)
                dqg = dqg + _dot(ds, k_placed[b], TN)
                dk_b.append(_dot(ds, qg))
                dv_b.append(_dot(p.astype(BF16), dog))
            for parts, total in ((dk_b, dk_heads), (dv_b, dv_heads)):
                kept = jnp.where(lane < HEAD_DIM, parts[0], parts[1])
                total.append(kept + pltpu.roll(kept, HEAD_DIM, 1))
            for r in range(2):
                j = 2 * kh + r
                dq_ref[rows, j * LANES:(j + 1) * LANES] = dqg[r * BLOCK:(r + 1) * BLOCK, :] * Q_SCALE
        dk = jnp.where(lane < HEAD_DIM, dk_heads[0], dk_heads[1])
        dv = jnp.where(lane < HEAD_DIM, dv_heads[0], dv_heads[1])
        dkp_ref[rows, :] = dk[0:BLOCK, :]
        dkc_ref[rows, :] = dk[BLOCK:, :]
        dvp_ref[rows, :] = dv[0:BLOCK, :]
        dvc_ref[rows, :] = dv[BLOCK:, :]

    def body(*refs):
        @pl.when(pl.program_id(0) == 0)
        def _():
            refs[-1][...] = jnp.zeros_like(refs[-1])

        for jb in range(qb):
            block(jb, *refs)

    kv = jax.ShapeDtypeStruct((s, KV_WIDTH), F32)
    return _call(
        body, [sinks, q, do, lse, k, k, v, v], name=name, grid=(s // (qb * BLOCK),),
        in_specs=[pl.BlockSpec(memory_space=pltpu.SMEM), cur(ATTN_WIDTH), cur(ATTN_WIDTH), lse_spec,
                  cur(KV_WIDTH), prev(KV_WIDTH), cur(KV_WIDTH), prev(KV_WIDTH)],
        out_specs=[cur(ATTN_WIDTH), cur(KV_WIDTH), cur(KV_WIDTH), cur(KV_WIDTH), cur(KV_WIDTH),
                   pl.BlockSpec((8, LANES), lambda i: (0, 0))],
        out_shape=[jax.ShapeDtypeStruct((s, ATTN_WIDTH), F32), kv, kv, kv, kv, jax.ShapeDtypeStruct((8, LANES), F32)],
        exchanges=exchanges)


def _pool_counts(tile, ts):
    t = (tile * ts + lax.broadcasted_iota(jnp.int32, (ts, 1), 0) + 1).astype(F32)
    return [jnp.minimum(t, float(w)) for w in POOL_WINDOWS]


def _pooled(u, halo, counts, gi):
    cols = slice(gi * POOL_GROUP_WIDTH, (gi + 1) * POOL_GROUP_WIDTH)
    acc = jnp.concatenate([halo[:, cols], u[:, cols]], axis=0)
    shift = 1
    while shift < POOL_WINDOWS[gi]:
        acc = acc + pltpu.roll(acc, shift, 0)
        shift *= 2
    return acc[POOL_HALO:, :] / counts[gi] - u[:, cols]


def _fwd_out(ao, u, pool_w, psc, w_out, x, vec, name, exchanges=()):
    s = x.shape[0]
    ts = _seq_tile(s, 2)
    hb = ts // POOL_HALO

    def body(ao_ref, u_ref, uh_ref, pw_ref, psc_ref, w_ref, x_ref, vec_ref, x1_ref, mix_ref):
        i = pl.program_id(0)
        u_t = u_ref[...]
        halo = jnp.where(i > 0, uh_ref[...], 0.0)
        counts = _pool_counts(i, ts)
        cat = [ao_ref[...]]
        for gi in range(len(POOL_WINDOWS)):
            cols = slice(gi * POOL_GROUP_WIDTH, (gi + 1) * POOL_GROUP_WIDTH)
            og = _dot(_pooled(u_t, halo, counts, gi).astype(BF16), pw_ref[gi]) * psc_ref[0:1, cols]
            cat.append(og.astype(BF16))
        mix = _dot(jnp.concatenate(cat, axis=1), w_ref[...])
        mix_ref[...] = mix
        x1_ref[...] = x_ref[...] + vec_ref[0:1, :] * (mix * _rsqrt_ms(mix) * vec_ref[1:2, :])

    row = lambda w: pl.BlockSpec((ts, w), lambda i: (i, 0))
    return _call(
        body, [ao, u, u, pool_w, psc, w_out, x, vec], name=name, grid=(s // ts,),
        in_specs=[row(ATTN_WIDTH), row(POOL_WIDTH),
                  pl.BlockSpec((POOL_HALO, POOL_WIDTH), lambda i: (jnp.maximum(i * hb - 1, 0), 0)),
                  _resident(pool_w.shape), _resident((8, POOL_WIDTH)), _resident((D_MODEL, D_MODEL)),
                  row(D_MODEL), _resident((8, D_MODEL))],
        out_specs=[row(D_MODEL), row(D_MODEL)],
        out_shape=[jax.ShapeDtypeStruct((s, D_MODEL), F32), jax.ShapeDtypeStruct((s, D_MODEL), F32)],
        exchanges=exchanges)


def _ffn_fwd(x1, vec, w_gate_t, w_up_t, w_down, name, exchanges=(), target=None):
    s = x1.shape[0]
    ts = _seq_tile(s)

    def body(x_ref, *refs):
        if target is None:
            vec_ref, wg_ref, wu_ref, wd_ref, x2_ref, f_ref, gt_ref, up_ref, h2_ref = refs
        else:
            t_ref, vec_ref, wg_ref, wu_ref, wd_ref, x2_ref, f_ref, gt_ref, up_ref, h2_ref, loss_ref = refs
        xt = x_ref[...]
        h2 = (xt * _rsqrt_ms(xt) * vec_ref[0:1, :] * vec_ref[1:2, :] + vec_ref[2:3, :]).astype(BF16)
        h2_ref[...] = h2
        f = jnp.zeros((ts, D_MODEL), F32)
        for c in range(D_FF // FF_CHUNK):
            cols = slice(c * FF_CHUNK, (c + 1) * FF_CHUNK)
            g = _dot(h2, wg_ref[cols, :], NT)
            up = _dot(h2, wu_ref[cols, :], NT)
            act = ((g * jax.nn.sigmoid(g)) * up).astype(BF16)
            gt_ref[:, cols] = g.astype(BF16)
            up_ref[:, cols] = up.astype(BF16)
            f = f + _dot(act, wd_ref[cols, :])
        f_ref[...] = f
        x2 = xt + vec_ref[3:4, :] * (f * _rsqrt_ms(f) * vec_ref[4:5, :])
        if target is None:
            x2_ref[...] = x2
        else:
            @pl.when(pl.program_id(0) == 0)
            def _():
                loss_ref[...] = jnp.zeros_like(loss_ref)

            diff = x2 - t_ref[...]
            x2_ref[...] = diff / D_MODEL
            loss_ref[...] += 0.5 * jnp.sum(jnp.mean(diff * diff, axis=-1, keepdims=True))

    row = lambda w: pl.BlockSpec((ts, w), lambda i: (i, 0))
    wide = jax.ShapeDtypeStruct((s, D_FF), BF16)
    with_loss = target is not None
    return _call(
        body, [x1] + ([target] if with_loss else []) + [vec, w_gate_t, w_up_t, w_down], name=name, grid=(s // ts,),
        in_specs=[row(D_MODEL)] * (2 if with_loss else 1) + [
            _resident((8, D_MODEL)), _resident((D_FF, D_MODEL)), _resident((D_FF, D_MODEL)), _resident((D_FF, D_MODEL))],
        out_specs=[row(D_MODEL), row(D_MODEL), row(D_FF), row(D_FF), row(D_MODEL)] + (
            [pl.BlockSpec((8, LANES), lambda i: (0, 0))] if with_loss else []),
        out_shape=[jax.ShapeDtypeStruct((s, D_MODEL), F32), jax.ShapeDtypeStruct((s, D_MODEL), F32), wide, wide,
                   jax.ShapeDtypeStruct((s, D_MODEL), BF16)] + (
            [jax.ShapeDtypeStruct((8, LANES), F32)] if with_loss else []),
        exchanges=exchanges)


def _norm_bwd(dy_hat, x_hat, r):
    return r * (dy_hat - x_hat * jnp.mean(dy_hat * x_hat, axis=-1, keepdims=True))


def _accumulate_rows(ref, rows):
    for j, val in enumerate(rows):
        ref[j:j + 1, :] += val


def _ffn_bwd_act(dx2, f, gt, up, vec, w_down, name, exchanges=()):
    s = dx2.shape[0]
    ts = _seq_tile(s)

    def body(dx2_ref, f_ref, gt_ref, up_ref, vec_ref, wd_ref, dgt_ref, dup_ref, dwd_ref, cs_ref):
        @pl.when(pl.program_id(0) == 0)
        def _():
            cs_ref[...] = jnp.zeros_like(cs_ref)
            dwd_ref[...] = jnp.zeros_like(dwd_ref)

        dx2_t, f_t = dx2_ref[...], f_ref[...]
        gate, g_post = vec_ref[3:4, :], vec_ref[4:5, :]
        rf = _rsqrt_ms(f_t)
        f_hat = f_t * rf
        df = _norm_bwd(dx2_t * gate * g_post, f_hat, rf).astype(BF16)
        _accumulate_rows(cs_ref, [_colsum(dx2_t * (f_hat * g_post)), _colsum(dx2_t * gate * f_hat)])
        for c in range(D_FF // FF_CHUNK):
            cols = slice(c * FF_CHUNK, (c + 1) * FF_CHUNK)
            dact = _dot(df, wd_ref[cols, :], NT)
            g, u_t = gt_ref[:, cols].astype(F32), up_ref[:, cols].astype(F32)
            sg = 0.5 * jnp.tanh(0.5 * g) + 0.5
            silu = g * sg
            dgt_ref[:, cols] = (dact * u_t * (sg * (1.0 + g * (1.0 - sg)))).astype(BF16)
            dup_ref[:, cols] = (dact * silu).astype(BF16)
            dwd_ref[cols, :] += _dot((silu * u_t).astype(BF16), df, TN)

    row = lambda w: pl.BlockSpec((ts, w), lambda i: (i, 0))
    wide = jax.ShapeDtypeStruct((s, D_FF), BF16)
    return _call(
        body, [dx2, f, gt, up, vec, w_down], name=name, grid=(s // ts,),
        in_specs=[row(D_MODEL), row(D_MODEL), row(D_FF), row(D_FF), _resident((8, D_MODEL)), _resident((D_FF, D_MODEL))],
        out_specs=[row(D_FF), row(D_FF), _resident((D_FF, D_MODEL)), pl.BlockSpec((8, D_MODEL), lambda i: (0, 0))],
        out_shape=[wide, wide, jax.ShapeDtypeStruct((D_FF, D_MODEL), F32), jax.ShapeDtypeStruct((8, D_MODEL), F32)],
        exchanges=exchanges)


def _ffn_bwd_in(dx2, x1, dgt, dup, vec, w_gate_t, w_up_t, name, exchanges=()):
    s = x1.shape[0]
    ts = _seq_tile(s)

    def body(dx2_ref, x1_ref, dgt_ref, dup_ref, vec_ref, wg_ref, wu_ref, dx1_ref, cs_ref):
        @pl.when(pl.program_id(0) == 0)
        def _():
            cs_ref[...] = jnp.zeros_like(cs_ref)

        g_pre, one_scale = vec_ref[0:1, :], vec_ref[1:2, :]
        dh2 = _dot(dgt_ref[...], wg_ref[...]) + _dot(dup_ref[...], wu_ref[...])
        x1_t = x1_ref[...]
        r1 = _rsqrt_ms(x1_t)
        x_hat = x1_t * r1
        dx1_ref[...] = dx2_ref[...] + _norm_bwd(dh2 * g_pre * one_scale, x_hat, r1)
        _accumulate_rows(cs_ref, [_colsum(dh2), _colsum(dh2 * (x_hat * g_pre)), _colsum(dh2 * one_scale * x_hat)])

    row = lambda w: pl.BlockSpec((ts, w), lambda i: (i, 0))
    return _call(
        body, [dx2, x1, dgt, dup, vec, w_gate_t, w_up_t], name=name, grid=(s // ts,),
        in_specs=[row(D_MODEL), row(D_MODEL), row(D_FF), row(D_FF), _resident((8, D_MODEL)),
                  _resident((D_FF, D_MODEL)), _resident((D_FF, D_MODEL))],
        out_specs=[row(D_MODEL), pl.BlockSpec((8, D_MODEL), lambda i: (0, 0))],
        out_shape=[jax.ShapeDtypeStruct((s, D_MODEL), F32), jax.ShapeDtypeStruct((8, D_MODEL), F32)],
        exchanges=exchanges)


def _weight_grad(a, b, name, exchanges=()):
    s, m = a.shape
    n = b.shape[1]
    tk = min(s, 2 * SEQ_TILE)
    steps = s // tk

    def body(a_ref, b_ref, o_ref, acc_ref):
        i = pl.program_id(0)

        @pl.when(i == 0)
        def _():
            acc_ref[...] = jnp.zeros_like(acc_ref)

        b_t = b_ref[...]
        for c in range(m // FF_CHUNK):
            rows = slice(c * FF_CHUNK, (c + 1) * FF_CHUNK)
            acc_ref[rows, :] += _dot(a_ref[:, rows], b_t, TN)

        @pl.when(i == steps - 1)
        def _():
            o_ref[...] = acc_ref[...].astype(BF16)

    (grad,), ex_outs = _call(
        body, [a, b], name=name, grid=(steps,),
        in_specs=[pl.BlockSpec((tk, m), lambda i: (i, 0)), pl.BlockSpec((tk, n), lambda i: (i, 0))],
        out_specs=[_resident((m, n))],
        out_shape=[jax.ShapeDtypeStruct((m, n), BF16)],
        scratch_shapes=[pltpu.VMEM((m, n), F32)],
        exchanges=exchanges)
    return grad, ex_outs


def _out_bwd(dx1, mix, ao, u, pool_w, psc, w_out, vec, name, exchanges=()):
    s = dx1.shape[0]
    ts = _seq_tile(s)
    nt = s // ts
    hb = ts // POOL_HALO
    ng = len(POOL_WINDOWS)

    def body(dx1_ref, mix_ref, ao_ref, u_ref, uh_ref, pw_ref, psc_ref, w_ref, vec_ref,
             do_ref, du_ref, dw_out_ref, cs_ref, dpw_ref, dpsc_ref, carry_ref, dw_ref):
        i = pl.program_id(0)
        tile = nt - 1 - i

        @pl.when(i == 0)
        def _():
            dw_ref[...] = jnp.zeros_like(dw_ref)
            cs_ref[...] = jnp.zeros_like(cs_ref)
            dpw_ref[...] = jnp.zeros_like(dpw_ref)
            dpsc_ref[...] = jnp.zeros_like(dpsc_ref)
            carry_ref[...] = jnp.zeros_like(carry_ref)

        dx1_t, mix_t = dx1_ref[...], mix_ref[...]
        gate, g_post = vec_ref[0:1, :], vec_ref[1:2, :]
        rm = _rsqrt_ms(mix_t)
        m_hat = mix_t * rm
        dmix = _norm_bwd(dx1_t * gate * g_post, m_hat, rm).astype(BF16)
        _accumulate_rows(cs_ref, [_colsum(dx1_t * (m_hat * g_post)), _colsum(dx1_t * gate * m_hat)])
        d_cat = _dot(dmix, w_ref[...], NT)
        do_ref[...] = d_cat[:, 0:ATTN_WIDTH].astype(BF16)

        u_t = u_ref[...]
        halo = jnp.where(tile > 0, uh_ref[...], 0.0)
        counts = _pool_counts(tile, ts)
        cat = [ao_ref[...]]
        for gi in range(ng):
            cols = slice(gi * POOL_GROUP_WIDTH, (gi + 1) * POOL_GROUP_WIDTH)
            scale = psc_ref[0:1, cols]
            pooled = _pooled(u_t, halo, counts, gi).astype(BF16)
            og = _dot(pooled, pw_ref[gi])
            cat.append((og * scale).astype(BF16))
            d_out = d_cat[:, ATTN_WIDTH + gi * POOL_GROUP_WIDTH:ATTN_WIDTH + (gi + 1) * POOL_GROUP_WIDTH]
            dpsc_ref[0:1, cols] += _colsum(d_out * og)
            d_og = (d_out * scale).astype(BF16)
            dpw_ref[gi] += _dot(pooled, d_og, TN)
            d_pooled = _dot(d_og, pw_ref[gi], NT)
            spread = d_pooled / counts[gi]
            acc = jnp.concatenate([spread, carry_ref[:, cols]], axis=0)
            shift = 1
            while shift < POOL_WINDOWS[gi]:
                acc = acc + pltpu.roll(acc, ts + POOL_HALO - shift, 0)
                shift *= 2
            du_ref[:, cols] = acc[0:ts, :] - d_pooled
            carry_ref[:, cols] = spread[0:POOL_HALO, :]
        dw_ref[...] += _dot(jnp.concatenate(cat, axis=1), dmix, TN)

        @pl.when(i == nt - 1)
        def _():
            dw_out_ref[...] = dw_ref[...].astype(BF16)

    row = lambda w: pl.BlockSpec((ts, w), lambda i: (nt - 1 - i, 0))
    fixed = lambda shape: pl.BlockSpec(shape, lambda i: (0,) * len(shape))
    return _call(
        body, [dx1, mix, ao, u, u, pool_w, psc, w_out, vec], name=name, grid=(nt,),
        in_specs=[row(D_MODEL), row(D_MODEL), row(ATTN_WIDTH), row(POOL_WIDTH),
                  pl.BlockSpec((POOL_HALO, POOL_WIDTH), lambda i: (jnp.maximum((nt - 1 - i) * hb - 1, 0), 0)),
                  _resident(pool_w.shape), _resident((8, POOL_WIDTH)), _resident((D_MODEL, D_MODEL)),
                  _resident((8, D_MODEL))],
        out_specs=[row(ATTN_WIDTH), row(POOL_WIDTH), _resident((D_MODEL, D_MODEL)), fixed((8, D_MODEL)),
                   fixed(pool_w.shape), fixed((8, POOL_WIDTH))],
        out_shape=[jax.ShapeDtypeStruct((s, ATTN_WIDTH), BF16), jax.ShapeDtypeStruct((s, POOL_WIDTH), F32),
                   jax.ShapeDtypeStruct((D_MODEL, D_MODEL), BF16),
                   jax.ShapeDtypeStruct((8, D_MODEL), F32), jax.ShapeDtypeStruct(pool_w.shape, F32),
                   jax.ShapeDtypeStruct((8, POOL_WIDTH), F32)],
        scratch_shapes=[pltpu.VMEM((POOL_HALO, POOL_WIDTH), F32), pltpu.VMEM((D_MODEL, D_MODEL), F32)],
        exchanges=exchanges)


def _in_bwd(dq, dkc, dkp, dvc, dvp, du, x, dx1, vec, w_in_t, rot, name, exchanges=()):
    s = x.shape[0]
    ts = _seq_tile(s)
    nt = s // ts
    bpt = ts // BLOCK

    def body(dq_ref, dkc_ref, dkp_ref, dkn_ref, dvc_ref, dvp_ref, dvn_ref, du_ref, x_ref, dx1_ref, vec_ref, w_ref,
             cos_ref, lo_ref, hi_ref, dx_ref, dw_out_ref, cs_ref, db_ref, dw_ref):
        i = pl.program_id(0)

        @pl.when(i == 0)
        def _():
            dw_ref[...] = jnp.zeros_like(dw_ref)
            cs_ref[...] = jnp.zeros_like(cs_ref)
            db_ref[...] = jnp.zeros_like(db_ref)

        cos, lo, hi = cos_ref[...], lo_ref[...], hi_ref[...]

        def with_next_block(cur_ref, prev_ref, next_ref):
            nxt = jnp.where(i < nt - 1, next_ref[...], 0.0)
            later = nxt if bpt == 1 else jnp.concatenate([prev_ref[BLOCK:, :], nxt], axis=0)
            return cur_ref[...] + later

        pieces = [_rotate_bwd(dq_ref[:, j * LANES:(j + 1) * LANES], cos, lo, hi) for j in range(ATTN_WIDTH // LANES)]
        pieces.append(_rotate_bwd(with_next_block(dkc_ref, dkp_ref, dkn_ref), cos, lo, hi))
        pieces.append(with_next_block(dvc_ref, dvp_ref, dvn_ref))
        pieces.append(du_ref[...])
        dproj = jnp.concatenate(pieces, axis=1)
        db_ref[0:1, :] += _colsum(dproj)
        dproj_b = dproj.astype(BF16)
        dh = _dot(dproj_b, w_ref[...])

        xt = x_ref[...]
        g_pre, one_scale = vec_ref[0:1, :], vec_ref[1:2, :]
        r = _rsqrt_ms(xt)
        x_hat = xt * r
        dw_ref[...] += _dot(dproj_b, (x_hat * g_pre * one_scale + vec_ref[2:3, :]).astype(BF16), TN)
        dx_ref[...] = dx1_ref[...] + _norm_bwd(dh * g_pre * one_scale, x_hat, r)
        _accumulate_rows(cs_ref, [_colsum(dh), _colsum(dh * (x_hat * g_pre)), _colsum(dh * one_scale * x_hat)])

        @pl.when(i == nt - 1)
        def _():
            dw_out_ref[...] = dw_ref[...].astype(BF16)

    row = lambda w: pl.BlockSpec((ts, w), lambda i: (i, 0))
    nxt = pl.BlockSpec((BLOCK, KV_WIDTH), lambda i: (jnp.minimum((i + 1) * bpt, s // BLOCK - 1), 0))
    fixed = lambda shape: pl.BlockSpec(shape, lambda i: (0,) * len(shape))
    return _call(
        body, [dq, dkc, dkp, dkp, dvc, dvp, dvp, du, x, dx1, vec, w_in_t, *rot], name=name, grid=(nt,),
        in_specs=[row(ATTN_WIDTH), row(KV_WIDTH), row(KV_WIDTH), nxt, row(KV_WIDTH), row(KV_WIDTH), nxt,
                  row(POOL_WIDTH), row(D_MODEL), row(D_MODEL), _resident((8, D_MODEL)), _resident((IN_WIDTH, D_MODEL)),
                  row(LANES), row(LANES), row(LANES)],
        out_specs=[row(D_MODEL), _resident((IN_WIDTH, D_MODEL)), fixed((8, D_MODEL)), fixed((8, IN_WIDTH))],
        out_shape=[jax.ShapeDtypeStruct((s, D_MODEL), F32), jax.ShapeDtypeStruct((IN_WIDTH, D_MODEL), BF16),
                   jax.ShapeDtypeStruct((8, D_MODEL), F32), jax.ShapeDtypeStruct((8, IN_WIDTH), F32)],
        scratch_shapes=[pltpu.VMEM((IN_WIDTH, D_MODEL), F32)],
        exchanges=exchanges)


def _row_tile(rows):
    for t in (512, 352, 256, 176, 160, 128, 64, 32, 16, 8):
        if rows % t == 0:
            return t
    raise ValueError(f"no row tile for {rows} rows")


def _sum_parts(parts, first, name):
    n, r, c = parts.shape
    tr = _row_tile(r)

    def body(p_ref, o_ref):
        acc = p_ref[first].astype(F32)
        for k in range(n):
            if k != first:
                acc = acc + p_ref[k].astype(F32)
        o_ref[...] = acc

    return pl.pallas_call(
        body, name=name, grid=(r // tr,),
        in_specs=[pl.BlockSpec((n, tr, c), lambda i: (0, i, 0))],
        out_specs=pl.BlockSpec((tr, c), lambda i: (i, 0)),
        out_shape=jax.ShapeDtypeStruct((r, c), F32),
        compiler_params=_cparams(1),
    )(parts)


def _adamw(w, g, m, v, name):
    r, c = w.shape
    tr = _row_tile(r) if r % 8 == 0 else r

    def body(w_ref, g_ref, m_ref, v_ref, d_ref, nm_ref, nv_ref):
        g_t = g_ref[...]
        m_new = ADAM_B1 * m_ref[...] + (1.0 - ADAM_B1) * g_t
        v_new = ADAM_B2 * v_ref[...] + (1.0 - ADAM_B2) * (g_t * g_t)
        m_hat = m_new / (1.0 - ADAM_B1 ** ADAM_STEP)
        v_hat = v_new / (1.0 - ADAM_B2 ** ADAM_STEP)
        d_ref[...] = -ADAM_LR * (m_hat / (jnp.sqrt(v_hat) + ADAM_EPS) + ADAM_WD * w_ref[...])
        nm_ref[...] = m_new
        nv_ref[...] = v_new

    spec = pl.BlockSpec((tr, c), lambda i: (i, 0))
    shape = jax.ShapeDtypeStruct((r, c), F32)
    return pl.pallas_call(
        body, name=name, grid=(r // tr,), in_specs=[spec] * 4, out_specs=[spec] * 3, out_shape=[shape] * 3,
        compiler_params=_cparams(1),
    )(w, g, m, v)


LAYER_ROWS = 80


def _pack_layer(dmod, gains, b_in, pool_scale, sinks, pool_w):
    misc = jnp.concatenate([pool_scale, sinks, jnp.zeros((D_MODEL - POOL_WIDTH - N_HEADS,), F32)])
    parts = [dmod.reshape(N_MOD, D_MODEL), gains, jnp.pad(b_in, (0, 2 * D_MODEL - IN_WIDTH)).reshape(2, D_MODEL),
             misc.reshape(1, D_MODEL), pool_w.reshape(-1, D_MODEL)]
    packed = jnp.concatenate(parts, axis=0)
    return jnp.pad(packed, ((0, LAYER_ROWS - packed.shape[0]), (0, 0)))


def _unpack_layer(p):
    r = N_MOD + 4
    b_in = p[r:r + 2].reshape(2 * D_MODEL)[:IN_WIDTH]
    pool_w = p[r + 3:r + 3 + 64].reshape(len(POOL_WINDOWS), POOL_GROUP_WIDTH, POOL_GROUP_WIDTH)
    return (p[0:N_MOD].reshape(N_MOD * D_MODEL), p[N_MOD:r], b_in, p[r + 2, :POOL_WIDTH],
            p[r + 2, POOL_WIDTH:POOL_WIDTH + N_HEADS], pool_w)


def _rotary_tables(positions):
    inv_freq = ROPE_THETA ** (-jnp.arange(0, ROT_DIM, 2, dtype=F32) / ROT_DIM)
    half = ROT_DIM // 2
    head_freq = jnp.concatenate([inv_freq, inv_freq, jnp.zeros((HEAD_DIM - ROT_DIM,), F32)])
    lane_freq = jnp.concatenate([head_freq, head_freq])
    ang = positions.reshape(-1).astype(F32)[:, None] * lane_freq[None, :]
    cos, sin = jnp.cos(ang), jnp.sin(ang)
    in_head = lax.broadcasted_iota(jnp.int32, (1, LANES), 1) % HEAD_DIM
    return cos, jnp.where(in_head < half, -sin, 0.0), jnp.where((in_head >= half) & (in_head < ROT_DIM), sin, 0.0)


def kernel(x, c, positions, ada_w, ada_b, w_in, b_in, sinks, pool_w, pool_scale, w_out, w_gate, w_up, w_down, g_pre_mix, g_post_mix, g_pre_ffn, g_post_ffn, loss_target, m_ada_w, m_ada_b, m_w_in, m_b_in, m_sinks, m_pool_w, m_pool_scale, m_w_out, m_w_gate, m_w_up, m_w_down, m_g_pre_mix, m_g_post_mix, m_g_pre_ffn, m_g_post_ffn, v_ada_w, v_ada_b, v_w_in, v_b_in, v_sinks, v_pool_w, v_pool_scale, v_w_out, v_w_gate, v_w_up, v_w_down, v_g_pre_mix, v_g_post_mix, v_g_pre_ffn, v_g_post_ffn):
    me = _index(_mesh_place())
    x0 = x[0]
    target = loss_target[0]
    rot = _rotary_tables(positions)
    ada_cols = ada_w.shape[2]

    shards = []
    for l in range(DEPTH):
        shards += [w_in[l].T.astype(BF16), w_out[l].astype(BF16), w_gate[l].T.astype(BF16), w_up[l].T.astype(BF16),
                   w_down[l].astype(BF16)]
    whole = lambda gathered: [g.reshape(-1, D_MODEL) for g in gathered]
    full = [None] * (5 * DEPTH)
    first = _exchanges_alone([_Gather(shards[0:2] + [jnp.broadcast_to(c, (8, D_MODEL))])], "gather_first")[0]
    full[0:2] = whole(first[0:2])

    mod_part, c_act = _mod_fwd(first[2][:, 0, :], ada_w)
    mod_all = _allgather_vmem(mod_part.reshape(DEPTH * N_DEV, ada_cols), "gather_mod")
    mod_mine = lax.dynamic_slice_in_dim(mod_all.reshape(N_DEV, DEPTH, N_DEV, ada_cols), me, 1, axis=2)[:, :, 0, :]
    mod = jnp.transpose(mod_mine, (1, 0, 2)).reshape(DEPTH, N_MOD * D_MODEL) + ada_b
    mod = mod.reshape(DEPTH, N_MOD, D_MODEL)

    hosted = {("fwd_in", 0): (_Gather(shards[2:3]), [2]), ("attn_fwd", 0): (_Gather(shards[3:4]), [3]),
              ("fwd_out", 0): (_Gather(shards[4:5]), [4]),
              ("ffn_fwd", 0): (_Gather(shards[5:10]), [5, 6, 7, 8, 9])}

    def take(kind, l, ex_outs):
        if (kind, l) in hosted:
            for slot, g in zip(hosted[kind, l][1], whole(ex_outs[0])):
                full[slot] = g

    def beside(kind, l):
        return [hosted[kind, l][0]] if (kind, l) in hosted else []

    pool_w_b = pool_w.astype(BF16)

    saved = []
    xl = x0
    for l in range(DEPTH):
        vec_in = _rows(g_pre_mix[l], 1.0 + mod[l, 1], mod[l, 0], width=D_MODEL)
        vec_out = _rows(mod[l, 2], g_post_mix[l], width=D_MODEL)
        vec_ffn = _rows(g_pre_ffn[l], 1.0 + mod[l, 4], mod[l, 3], mod[l, 5], g_post_ffn[l], width=D_MODEL)
        psc = _rows(pool_scale[l], width=POOL_WIDTH)
        (q, k, v, u), ex = _fwd_in(xl, vec_in, full[5 * l], _rows(b_in[l], width=IN_WIDTH), rot, f"fwd_in_{l}",
                                   beside("fwd_in", l))
        take("fwd_in", l, ex)
        (ao, lse), ex = _attn_fwd_t(sinks[l], q, k, v, f"attn_fwd_{l}", beside("attn_fwd", l))
        take("attn_fwd", l, ex)
        (x1, mix), ex = _fwd_out(ao, u, pool_w_b[l], psc, full[5 * l + 1], xl, vec_out, f"fwd_out_{l}",
                                 beside("fwd_out", l))
        take("fwd_out", l, ex)
        (x2, f, gt, up, h2, *loss_part), ex = _ffn_fwd(
            x1, vec_ffn, full[5 * l + 2], full[5 * l + 3], full[5 * l + 4], f"ffn_fwd_{l}", beside("ffn_fwd", l),
            target=target if l == DEPTH - 1 else None)
        take("ffn_fwd", l, ex)
        saved.append((xl, q, k, v, u, ao, lse, x1, mix, f, gt, up, h2, vec_in, vec_out, vec_ffn, psc))
        xl = x2

    dx = xl
    loss = lax.psum(loss_part[0][0, 0], ("x", "y", "c"))

    stacks = [None] * (5 * DEPTH)
    parts = [None] * (5 * DEPTH)
    small_all = [None] * DEPTH
    ready_stacks, ready_small = [], []

    def leaving(row_budget):
        exs, notes = [], []
        taken = []
        while ready_stacks and stacks[ready_stacks[0]].shape[1] <= row_budget:
            row_budget -= stacks[ready_stacks[0]].shape[1]
            taken.append(ready_stacks.pop(0))
        if taken:
            exs.append(_Scatter([stacks[a] for a in taken]))
            notes.append(("stacks", taken))
        if ready_small and row_budget >= 0:
            exs.append(_Gather([ready_small[0][1]]))
            notes.append(("small", ready_small[0][0]))
            ready_small.clear()
        return exs, notes

    def arrived(notes, ex_outs):
        for (kind, what), outs in zip(notes, ex_outs):
            if kind == "stacks":
                for a, o in zip(what, outs):
                    parts[a] = o
            else:
                small_all[what] = outs[0]

    def grad_ready(a, grad):
        stacks[a] = grad.reshape(N_DEV, -1, D_MODEL)
        ready_stacks.append(a)

    def hosting(row_budget, kernel_fn, *args):
        exs, notes = leaving(row_budget)
        outs, ex_outs = kernel_fn(*args, exs)
        arrived(notes, ex_outs)
        return outs

    for l in reversed(range(DEPTH)):
        w_in_t, w_out_f, w_gate_t, w_up_t, w_down_f = full[5 * l:5 * l + 5]
        xin, q, k, v, u, ao, lse, x1, mix, f, gt, up, h2, vec_in, vec_out, vec_ffn, psc = saved[l]
        dgt, dup, d_w_down, cs_a = hosting(600, _ffn_bwd_act, dx, f, gt, up, vec_ffn, w_down_f, f"ffn_bwd_act_{l}")
        grad_ready(5 * l + 4, d_w_down.astype(BF16))
        dx1, cs_f = hosting(500, _ffn_bwd_in, dx, x1, dgt, dup, vec_ffn, w_gate_t, w_up_t, f"ffn_bwd_in_{l}")
        grad_ready(5 * l + 2, hosting(-1, _weight_grad, dgt, h2, f"grad_w_gate_{l}"))
        grad_ready(5 * l + 3, hosting(-1, _weight_grad, dup, h2, f"grad_w_up_{l}"))
        do, du, d_w_out, cs_o, dpw, dpsc = hosting(500, _out_bwd, dx1, mix, ao, u, pool_w_b[l], psc, w_out_f, vec_out,
                                                   f"out_bwd_{l}")
        grad_ready(5 * l + 1, d_w_out)
        dq, dkc, dkp, dvc, dvp, dsk = hosting(480, _attn_bwd_t, sinks[l], q, do, lse, k, v, f"attn_bwd_{l}")
        dx, d_w_in_t, cs_i, db = hosting(-1, _in_bwd, dq, dkc, dkp, dvc, dvp, du, xin, dx1, vec_in, w_in_t, rot,
                                         f"in_bwd_{l}")
        grad_ready(5 * l, d_w_in_t)
        d_mod = jnp.concatenate([cs_i[0], cs_i[1], cs_o[0], cs_f[0], cs_f[1], cs_a[0]])
        d_gain = jnp.stack([cs_i[2], cs_o[1], cs_f[2], cs_a[1]])
        ready_small.append((l, _pack_layer(d_mod, d_gain, db[0], dpsc[0], dsk[:, 0], dpw)))
    grad_x = dx[None]
    exs, notes = leaving(N_DEV * D_FF)
    arrived(notes, _exchanges_alone(exs, "exchange_last"))

    layer_sums = [_unpack_layer(_sum_parts(small_all[l], 0, f"sum_small_{l}")) for l in range(DEPTH)]
    g_ada_b, g_gains, g_b_in, g_pool_scale, g_sinks, g_pool_w = (
        jnp.stack([layer_sums[l][j] for l in range(DEPTH)], axis=1 if j == 1 else 0) for j in range(6))

    dmod_all = jnp.stack([small_all[l][:, 0:N_MOD, :].reshape(N_DEV, N_DEV, ada_cols) for l in range(DEPTH)])
    dmod_cols = lax.dynamic_slice_in_dim(dmod_all, me, 1, axis=2)[:, :, 0, :]
    g_ada_w = _ada_grad(c_act, dmod_cols)

    summed = [_sum_parts(parts[a], _Scatter.OWN, f"sum_grads_{a}") for a in range(5 * DEPTH)]
    per_kind = [jnp.stack([summed[5 * l + j] for l in range(DEPTH)]) for j in range(5)]
    g_w_in, g_w_out, g_w_gate, g_w_up, g_w_down = per_kind
    g_w_in, g_w_gate, g_w_up = (jnp.transpose(g, (0, 2, 1)) for g in (g_w_in, g_w_gate, g_w_up))

    def step(w, g, m, v, name):
        flat = lambda a: a.reshape(-1, a.shape[-1])
        return [o.reshape(w.shape) for o in _adamw(flat(w), flat(g), flat(m), flat(v), name)]

    in_order = [
        ("ada_w", ada_w, g_ada_w, m_ada_w, v_ada_w), ("ada_b", ada_b, g_ada_b, m_ada_b, v_ada_b),
        ("w_in", w_in, g_w_in, m_w_in, v_w_in), ("b_in", b_in, g_b_in, m_b_in, v_b_in),
        ("sinks", sinks, g_sinks, m_sinks, v_sinks), ("pool_w", pool_w, g_pool_w, m_pool_w, v_pool_w),
        ("pool_scale", pool_scale, g_pool_scale, m_pool_scale, v_pool_scale),
        ("w_out", w_out, g_w_out, m_w_out, v_w_out), ("w_gate", w_gate, g_w_gate, m_w_gate, v_w_gate),
        ("w_up", w_up, g_w_up, m_w_up, v_w_up), ("w_down", w_down, g_w_down, m_w_down, v_w_down),
        ("g_pre_mix", g_pre_mix, g_gains[0], m_g_pre_mix, v_g_pre_mix),
        ("g_post_mix", g_post_mix, g_gains[1], m_g_post_mix, v_g_post_mix),
        ("g_pre_ffn", g_pre_ffn, g_gains[2], m_g_pre_ffn, v_g_pre_ffn),
        ("g_post_ffn", g_post_ffn, g_gains[3], m_g_post_ffn, v_g_post_ffn)]
    steps = [step(w, g, m, v, f"adamw_{n}") for n, w, g, m, v in in_order]
    grads = [g for _, _, g, _, _ in in_order]
    deltas, new_m, new_v = ([s[j] for s in steps] for j in range(3))
    return (loss, grad_x, *grads, *deltas, *new_m, *new_v)
```

```python
import jax
import jax.numpy as jnp
from jax import lax
from jax.experimental import pallas as pl
from jax.experimental.pallas import tpu as pltpu

F32 = jnp.float32
BF16 = jnp.bfloat16

N_DEV = 8
DEPTH = 2
D_MODEL = 1024
HEAD_DIM = 64
N_HEADS = 8
N_KV_HEADS = 2
GROUP = N_HEADS // N_KV_HEADS
ATTN_WIDTH = N_HEADS * HEAD_DIM
KV_WIDTH = N_KV_HEADS * HEAD_DIM
POOL_WIDTH = 512
POOL_WINDOWS = (2, 4, 8, 16)
POOL_GROUP_WIDTH = 128
POOL_HALO = 16
IN_WIDTH = ATTN_WIDTH + 2 * KV_WIDTH + POOL_WIDTH
D_FF = 2816
N_MOD = 6
BLOCK = 128
ROT_DIM = 16
ROPE_THETA = 500000.0
EPS = 1e-6
NEG_INF = -1e30
Q_SCALE = HEAD_DIM ** -0.5

ADAM_LR = 0.001
ADAM_B1 = 0.9
ADAM_B2 = 0.999
ADAM_EPS = 1e-08
ADAM_WD = 0.01
ADAM_STEP = 10

LANES = 128
SEQ_TILE = 512
ATTN_BLOCKS = 4
FF_CHUNK = 256
VMEM_LIMIT = 56 * 1024 * 1024
MESH = pl.DeviceIdType.MESH

NT = (((1,), (1,)), ((), ()))
TN = (((0,), (0,)), ((), ()))


def _dot(a, b, dims=None):
    if dims is None:
        return jnp.dot(a, b, preferred_element_type=F32)
    return lax.dot_general(a, b, dims, preferred_element_type=F32)


def _cparams(n_axes):
    return pltpu.CompilerParams(dimension_semantics=("arbitrary",) * n_axes, vmem_limit_bytes=VMEM_LIMIT)


def _resident(shape):
    zeros = (0,) * len(shape)
    return pl.BlockSpec(shape, lambda *_: zeros, pipeline_mode=pl.Buffered(1))


def _rows(*vectors, width):
    rows = [jnp.reshape(v, (1, width)).astype(F32) for v in vectors]
    rows.append(jnp.zeros((8 - len(rows), width), F32))
    return jnp.concatenate(rows, axis=0)


def _rsqrt_ms(x):
    return lax.rsqrt(jnp.mean(x * x, axis=-1, keepdims=True) + EPS)


def _colsum(x):
    return jnp.sum(x, axis=0, keepdims=True)


def _seq_tile(s, tiles=1):
    return min(s, tiles * SEQ_TILE)


def _mesh_place():
    x, y, c = lax.axis_index("x"), lax.axis_index("y"), lax.axis_index("c")
    return x, y, c


def _flip(place, k):
    x, y, c = place
    return (1 - x if k & 4 else x, 1 - y if k & 2 else y, 1 - c if k & 1 else c)


def _index(place):
    x, y, c = place
    return 4 * x + 2 * y + c


def _allgather_vmem(block, name):
    r, c = block.shape

    def body(x_ref, out_ref, send_sems, recv_sems, local_sem):
        me = _mesh_place()
        mine = pltpu.make_async_copy(x_ref, out_ref.at[_index(me)], local_sem)
        mine.start()

        def copy(k):
            return pltpu.make_async_remote_copy(
                src_ref=x_ref, dst_ref=out_ref.at[_index(me)], send_sem=send_sems.at[k - 1], recv_sem=recv_sems.at[k - 1],
                device_id=_flip(me, k), device_id_type=MESH)

        def arrival(k):
            return pltpu.make_async_remote_copy(
                src_ref=x_ref, dst_ref=out_ref.at[_index(_flip(me, k))], send_sem=send_sems.at[k - 1],
                recv_sem=recv_sems.at[k - 1], device_id=_flip(me, k), device_id_type=MESH)

        for k in range(1, N_DEV):
            copy(k).start()
        for k in range(1, N_DEV):
            arrival(k).wait_recv()
        for k in range(1, N_DEV):
            copy(k).wait_send()
        mine.wait()

    return pl.pallas_call(
        body, name=name,
        out_shape=jax.ShapeDtypeStruct((N_DEV, r, c), block.dtype),
        in_specs=[pl.BlockSpec(memory_space=pltpu.VMEM)],
        out_specs=pl.BlockSpec(memory_space=pltpu.VMEM),
        scratch_shapes=[pltpu.SemaphoreType.DMA((N_DEV - 1,)), pltpu.SemaphoreType.DMA((N_DEV - 1,)),
                        pltpu.SemaphoreType.DMA],
    )(block)


class _Gather:
    def __init__(self, shards):
        n = len(shards)
        self.operands = list(shards)
        self.out_shape = [jax.ShapeDtypeStruct((N_DEV,) + s.shape, s.dtype) for s in shards]
        self.scratch = [pltpu.SemaphoreType.DMA((N_DEV - 1, n)), pltpu.SemaphoreType.DMA((N_DEV - 1, n)),
                        pltpu.SemaphoreType.DMA((n,))]

    def _copies(self, x_refs, out_refs, sems):
        send_sems, recv_sems, local_sems = sems
        n = len(x_refs)
        x, y, c = _mesh_place()
        me, sibling = (x, y, c), (x, y, 1 - c)
        chips = [(1 - x, y), (x, 1 - y), (1 - x, 1 - y)]

        def copy(k, a, block, to, from_input=False):
            rows = out_refs[a].at[_index(block)]
            return pltpu.make_async_remote_copy(
                src_ref=x_refs[a] if from_input else rows, dst_ref=rows,
                send_sem=send_sems.at[k, a], recv_sem=recv_sems.at[k, a], device_id=to, device_id_type=MESH)

        mine = [pltpu.make_async_copy(x_refs[a], out_refs[a].at[_index(me)], local_sems.at[a]) for a in range(n)]
        first = [copy(0, a, me, sibling, from_input=True) for a in range(n)]
        first += [copy(1 + j, a, me, (*chip, c), from_input=True) for j, chip in enumerate(chips) for a in range(n)]
        over_ici = [copy(1 + j, a, (*chip, c), me) for j, chip in enumerate(chips) for a in range(n)]
        passed = [copy(4 + j, a, (*chip, c), sibling) for j, chip in enumerate(chips) for a in range(n)]
        from_sibling = [copy(0, a, sibling, me) for a in range(n)]
        from_sibling += [copy(4 + j, a, (*chip, 1 - c), me) for j, chip in enumerate(chips) for a in range(n)]
        return mine, first, over_ici, passed, from_sibling

    def begin(self, x_refs, out_refs, sems):
        mine, first, _, _, _ = self._copies(x_refs, out_refs, sems)
        for cp in mine + first:
            cp.start()

    def middle(self, x_refs, out_refs, sems):
        _, _, over_ici, passed, _ = self._copies(x_refs, out_refs, sems)
        for arrived, onward in zip(over_ici, passed):
            arrived.wait_recv()
            onward.start()

    def end(self, x_refs, out_refs, sems):
        mine, first, _, passed, from_sibling = self._copies(x_refs, out_refs, sems)
        for cp in from_sibling:
            cp.wait_recv()
        for cp in first + passed:
            cp.wait_send()
        for cp in mine:
            cp.wait()


class _Scatter:
    OWN = N_DEV - 1

    def __init__(self, stacks):
        n = len(stacks)
        self.operands = list(stacks)
        self.out_shape = [jax.ShapeDtypeStruct(s.shape, s.dtype) for s in stacks]
        self.scratch = [pltpu.SemaphoreType.DMA((N_DEV - 1, n)), pltpu.SemaphoreType.DMA((N_DEV - 1, n)),
                        pltpu.SemaphoreType.DMA((n,))]

    def _copies(self, g_refs, out_refs, sems):
        send_sems, recv_sems, local_sems = sems
        me = _mesh_place()

        def copy(k, a):
            peer = _flip(me, k)
            return pltpu.make_async_remote_copy(
                src_ref=g_refs[a].at[_index(peer)], dst_ref=out_refs[a].at[k - 1],
                send_sem=send_sems.at[k - 1, a], recv_sem=recv_sems.at[k - 1, a], device_id=peer, device_id_type=MESH)

        mine = [pltpu.make_async_copy(g_refs[a].at[_index(me)], out_refs[a].at[self.OWN], local_sems.at[a])
                for a in range(len(g_refs))]
        return mine, [copy(k, a) for k in range(1, N_DEV) for a in range(len(g_refs))]

    def begin(self, g_refs, out_refs, sems):
        mine, copies = self._copies(g_refs, out_refs, sems)
        for cp in mine + copies:
            cp.start()

    def middle(self, g_refs, out_refs, sems):
        pass

    def end(self, g_refs, out_refs, sems):
        mine, copies = self._copies(g_refs, out_refs, sems)
        for cp in copies:
            cp.wait_recv()
        for cp in copies:
            cp.wait_send()
        for cp in mine:
            cp.wait()


def _call(body, args, *, name, grid, in_specs, out_specs, out_shape, scratch_shapes=(), exchanges=()):
    if not exchanges:
        outs = pl.pallas_call(body, name=name, grid=grid, in_specs=in_specs, out_specs=out_specs, out_shape=out_shape,
                              scratch_shapes=list(scratch_shapes), compiler_params=_cparams(len(grid)))(*args)
        return outs, []
    (steps,) = grid
    n_in, n_out, n_scr = len(in_specs), len(out_specs), len(scratch_shapes)
    ex_in = [len(ex.operands) for ex in exchanges]
    ex_out = [len(ex.out_shape) for ex in exchanges]
    ex_scr = [len(ex.scratch) for ex in exchanges]

    def split(refs, counts):
        parts, pos = [], 0
        for cnt in counts:
            parts.append(refs[pos:pos + cnt])
            pos += cnt
        return parts

    def wrapped(*refs):
        ins, xin, outs, xout, scr, xscr = split(refs, [n_in, sum(ex_in), n_out, sum(ex_out), n_scr, sum(ex_scr)])
        bound = list(zip(exchanges, split(xin, ex_in), split(xout, ex_out), split(xscr, ex_scr)))
        step = pl.program_id(0)

        def phase(method, at):
            @pl.when(step == at)
            def _():
                for ex, i_refs, o_refs, sems in bound:
                    getattr(ex, method)(i_refs, o_refs, sems)

        phase("begin", 0)
        phase("middle", (3 * steps) // 4)
        body(*ins, *outs, *scr)
        phase("end", steps - 1)

    any_spec = pl.BlockSpec(memory_space=pl.ANY)
    results = pl.pallas_call(
        wrapped, name=name, grid=grid,
        in_specs=list(in_specs) + [any_spec] * sum(ex_in),
        out_specs=list(out_specs) + [any_spec] * sum(ex_out),
        out_shape=list(out_shape) + [s for ex in exchanges for s in ex.out_shape],
        scratch_shapes=list(scratch_shapes) + [s for ex in exchanges for s in ex.scratch],
        compiler_params=_cparams(1),
    )(*args, *[a for ex in exchanges for a in ex.operands])
    return results[:n_out], split(results[n_out:], ex_out)


def _exchanges_alone(exchanges, name):
    def body(flag_ref):
        flag_ref[...] = jnp.zeros_like(flag_ref)

    _, ex_outs = _call(body, [], name=name, grid=(1,), in_specs=[], out_specs=[pl.BlockSpec((8, LANES), lambda i: (0, 0))],
                       out_shape=[jax.ShapeDtypeStruct((8, LANES), F32)], exchanges=exchanges)
    return ex_outs


def _mod_fwd(c_all, ada_w):
    cols = ada_w.shape[2]

    def body(c_ref, w_ref, mp_ref, act_ref):
        c = c_ref[...]
        act = c * jax.nn.sigmoid(c)
        act_ref[...] = act
        mp_ref[0] = _dot(act.astype(BF16), w_ref[0].astype(BF16))

    return pl.pallas_call(
        body, name="mod_fwd", grid=(DEPTH,),
        in_specs=[pl.BlockSpec((N_DEV, D_MODEL), lambda l: (0, 0)),
                  pl.BlockSpec((1, D_MODEL, cols), lambda l: (l, 0, 0))],
        out_specs=[pl.BlockSpec((1, N_DEV, cols), lambda l: (l, 0, 0)),
                   pl.BlockSpec((N_DEV, D_MODEL), lambda l: (0, 0))],
        out_shape=[jax.ShapeDtypeStruct((DEPTH, N_DEV, cols), F32), jax.ShapeDtypeStruct((N_DEV, D_MODEL), F32)],
        compiler_params=_cparams(1),
    )(c_all, ada_w)


def _ada_grad(c_act, dmod_cols):
    cols = dmod_cols.shape[2]

    def body(act_ref, dm_ref, g_ref):
        g_ref[0] = _dot(act_ref[...].astype(BF16), dm_ref[0].astype(BF16), TN)

    return pl.pallas_call(
        body, name="ada_grad", grid=(DEPTH,),
        in_specs=[pl.BlockSpec((N_DEV, D_MODEL), lambda l: (0, 0)),
                  pl.BlockSpec((1, N_DEV, cols), lambda l: (l, 0, 0))],
        out_specs=pl.BlockSpec((1, D_MODEL, cols), lambda l: (l, 0, 0)),
        out_shape=jax.ShapeDtypeStruct((DEPTH, D_MODEL, cols), F32),
        compiler_params=_cparams(1),
    )(c_act, dmod_cols)


def _rotate(t, cos, sin_lo, sin_hi):
    return t * cos + pltpu.roll(t, LANES - 8, 1) * sin_lo + pltpu.roll(t, 8, 1) * sin_hi


def _rotate_bwd(t, cos, sin_lo, sin_hi):
    return t * cos + pltpu.roll(t * sin_lo, 8, 1) + pltpu.roll(t * sin_hi, LANES - 8, 1)


def _fwd_in(x, vec, w_in_t, b_in, rot, name, exchanges=()):
    s = x.shape[0]
    ts = _seq_tile(s, 2)

    def body(x_ref, vec_ref, w_ref, b_ref, cos_ref, lo_ref, hi_ref, q_ref, k_ref, v_ref, u_ref):
        xt = x_ref[...]
        h = xt * _rsqrt_ms(xt) * vec_ref[0:1, :] * vec_ref[1:2, :] + vec_ref[2:3, :]
        proj = _dot(h.astype(BF16), w_ref[...], NT) + b_ref[0:1, :]
        cos, lo, hi = cos_ref[...], lo_ref[...], hi_ref[...]
        for j in range(ATTN_WIDTH // LANES):
            q_ref[:, j * LANES:(j + 1) * LANES] = (
                _rotate(proj[:, j * LANES:(j + 1) * LANES], cos, lo, hi) * Q_SCALE).astype(BF16)
        k_ref[...] = _rotate(proj[:, ATTN_WIDTH:ATTN_WIDTH + KV_WIDTH], cos, lo, hi).astype(BF16)
        v_ref[...] = proj[:, ATTN_WIDTH + KV_WIDTH:ATTN_WIDTH + 2 * KV_WIDTH].astype(BF16)
        u_ref[...] = proj[:, ATTN_WIDTH + 2 * KV_WIDTH:]

    row = lambda w: pl.BlockSpec((ts, w), lambda i: (i, 0))
    return _call(
        body, [x, vec, w_in_t, b_in, *rot], name=name, grid=(s // ts,),
        in_specs=[row(D_MODEL), _resident((8, D_MODEL)), _resident((IN_WIDTH, D_MODEL)), _resident((8, IN_WIDTH)),
                  row(LANES), row(LANES), row(LANES)],
        out_specs=[row(ATTN_WIDTH), row(KV_WIDTH), row(KV_WIDTH), row(POOL_WIDTH)],
        out_shape=[jax.ShapeDtypeStruct((s, ATTN_WIDTH), BF16), jax.ShapeDtypeStruct((s, KV_WIDTH), BF16),
                   jax.ShapeDtypeStruct((s, KV_WIDTH), BF16), jax.ShapeDtypeStruct((s, POOL_WIDTH), F32)],
        exchanges=exchanges)


def _band_mask_t(first_block):
    key = lax.broadcasted_iota(jnp.int32, (2 * BLOCK, 2 * BLOCK), 0)
    qry = lax.broadcasted_iota(jnp.int32, (2 * BLOCK, 2 * BLOCK), 1) & (BLOCK - 1)
    d = key - qry
    return (d >= 1) & (d <= BLOCK) & ((key >= BLOCK) | jnp.logical_not(first_block))


def _placed(both, kh, b):
    lane = lax.broadcasted_iota(jnp.int32, (1, LANES), 1)
    src = both if kh == b else pltpu.roll(both, HEAD_DIM, 1)
    return jnp.where((lane >= b * HEAD_DIM) & (lane < (b + 1) * HEAD_DIM), src, jnp.zeros_like(src))


def _pair(ref, rows, kh, r):
    j = 2 * kh + r
    return ref[rows, j * LANES:(j + 1) * LANES]


def _attn_specs(s):
    qb = min(ATTN_BLOCKS, s // BLOCK)
    cur = lambda w: pl.BlockSpec((qb * BLOCK, w), lambda i: (i, 0))
    prev = lambda w: pl.BlockSpec((BLOCK, w), lambda i: (jnp.maximum(i * qb - 1, 0), 0))
    return qb, cur, prev, pl.BlockSpec((N_HEADS, qb * BLOCK), lambda i: (0, i))


def _kv_window(prev_ref, cur_ref, jb):
    before = prev_ref[...] if jb == 0 else cur_ref[(jb - 1) * BLOCK:jb * BLOCK, :]
    return jnp.concatenate([before, cur_ref[jb * BLOCK:(jb + 1) * BLOCK, :]], axis=0)


def _sink_row(sink_ref, kh, b):
    lane = lax.broadcasted_iota(jnp.int32, (1, 2 * BLOCK), 1)
    return jnp.where(lane < BLOCK, sink_ref[GROUP * kh + b], sink_ref[GROUP * kh + 2 + b])


def _attn_fwd_t(sinks, q, k, v, name, exchanges=()):
    s = q.shape[0]
    qb, cur, prev, lse_spec = _attn_specs(s)

    def block(jb, sink_ref, q_ref, kc_ref, kp_ref, vc_ref, vp_ref, o_ref, l_ref):
        rows = slice(jb * BLOCK, (jb + 1) * BLOCK)
        valid = _band_mask_t(pl.program_id(0) == 0 if jb == 0 else False)
        k_both, v_both = _kv_window(kp_ref, kc_ref, jb), _kv_window(vp_ref, vc_ref, jb)
        for kh in range(N_KV_HEADS):
            qg = jnp.concatenate([_pair(q_ref, rows, kh, 0), _pair(q_ref, rows, kh, 1)], axis=0)
            keys = jnp.concatenate([_placed(k_both, kh, 0), _placed(k_both, kh, 1)], axis=0)
            st = _dot(keys, qg, NT)
            out = jnp.zeros((2 * BLOCK, LANES), F32)
            for b in range(2):
                sc = jnp.where(valid, st[b * 2 * BLOCK:(b + 1) * 2 * BLOCK, :], NEG_INF)
                sink = _sink_row(sink_ref, kh, b)
                m = jnp.maximum(jnp.max(sc, axis=0, keepdims=True), sink)
                p = jnp.exp(sc - m)
                den = jnp.sum(p, axis=0, keepdims=True) + jnp.exp(sink - m)
                out = out + _dot((p * (1.0 / den)).astype(BF16), _placed(v_both, kh, b), TN)
                lse = m + jnp.log(den)
                for r in range(2):
                    h = GROUP * kh + 2 * r + b
                    l_ref[h:h + 1, rows] = lse[:, r * BLOCK:(r + 1) * BLOCK]
            for r in range(2):
                j = 2 * kh + r
                o_ref[rows, j * LANES:(j + 1) * LANES] = out[r * BLOCK:(r + 1) * BLOCK, :].astype(BF16)

    def body(*refs):
        for jb in range(qb):
            block(jb, *refs)

    return _call(
        body, [sinks, q, k, k, v, v], name=name, grid=(s // (qb * BLOCK),),
        in_specs=[pl.BlockSpec(memory_space=pltpu.SMEM), cur(ATTN_WIDTH), cur(KV_WIDTH), prev(KV_WIDTH),
                  cur(KV_WIDTH), prev(KV_WIDTH)],
        out_specs=[cur(ATTN_WIDTH), lse_spec],
        out_shape=[jax.ShapeDtypeStruct((s, ATTN_WIDTH), BF16), jax.ShapeDtypeStruct((N_HEADS, s), F32)],
        exchanges=exchanges)


def _attn_bwd_t(sinks, q, do, lse, k, v, name, exchanges=()):
    s = q.shape[0]
    qb, cur, prev, lse_spec = _attn_specs(s)

    def block(jb, sink_ref, q_ref, do_ref, l_ref, kc_ref, kp_ref, vc_ref, vp_ref,
              dq_ref, dkc_ref, dkp_ref, dvc_ref, dvp_ref, dsink_ref):
        rows = slice(jb * BLOCK, (jb + 1) * BLOCK)
        valid = _band_mask_t(pl.program_id(0) == 0 if jb == 0 else False)
        k_both, v_both = _kv_window(kp_ref, kc_ref, jb), _kv_window(vp_ref, vc_ref, jb)
        lane = lax.broadcasted_iota(jnp.int32, (1, LANES), 1)
        col = lax.broadcasted_iota(jnp.int32, (1, 2 * BLOCK), 1)
        dk_heads, dv_heads = [], []
        for kh in range(N_KV_HEADS):
            qg = jnp.concatenate([_pair(q_ref, rows, kh, 0), _pair(q_ref, rows, kh, 1)], axis=0)
            dog = jnp.concatenate([_pair(do_ref, rows, kh, 0), _pair(do_ref, rows, kh, 1)], axis=0)
            k_placed = [_placed(k_both, kh, b) for b in range(2)]
            v_placed = [_placed(v_both, kh, b) for b in range(2)]
            st = _dot(jnp.concatenate(k_placed, axis=0), qg, NT)
            dpt = _dot(jnp.concatenate(v_placed, axis=0), dog, NT)
            dqg = jnp.zeros((2 * BLOCK, LANES), F32)
            dk_b, dv_b = [], []
            for b in range(2):
                part = slice(b * 2 * BLOCK, (b + 1) * 2 * BLOCK)
                heads = [GROUP * kh + 2 * r + b for r in range(2)]
                lse_row = jnp.concatenate([l_ref[h:h + 1, rows] for h in heads], axis=1)
                p = jnp.where(valid, jnp.exp(st[part, :] - lse_row), 0.0)
                dp = dpt[part, :]
                delta = jnp.sum(p * dp, axis=0, keepdims=True)
                sink_pull = jnp.exp(_sink_row(sink_ref, kh, b) - lse_row) * delta
                dsink_ref[heads[0]:heads[0] + 1, :] += -jnp.sum(jnp.where(col < BLOCK, sink_pull, 0.0))
                dsink_ref[heads[1]:heads[1] + 1, :] += -jnp.sum(jnp.where(col < BLOCK, 0.0, sink_pull))
                ds = (p * (dp - delta)).astype(BF16)
                dqg = dqg + _dot(ds, k_placed[b], TN)
                dk_b.append(_dot(ds, qg))
                dv_b.append(_dot(p.astype(BF16), dog))
            for parts, total in ((dk_b, dk_heads), (dv_b, dv_heads)):
                kept = jnp.where(lane < HEAD_DIM, parts[0], parts[1])
                total.append(kept + pltpu.roll(kept, HEAD_DIM, 1))
            for r in range(2):
                j = 2 * kh + r
                dq_ref[rows, j * LANES:(j + 1) * LANES] = dqg[r * BLOCK:(r + 1) * BLOCK, :] * Q_SCALE
        dk = jnp.where(lane < HEAD_DIM, dk_heads[0], dk_heads[1])
        dv = jnp.where(lane < HEAD_DIM, dv_heads[0], dv_heads[1])
        dkp_ref[rows, :] = dk[0:BLOCK, :]
        dkc_ref[rows, :] = dk[BLOCK:, :]
        dvp_ref[rows, :] = dv[0:BLOCK, :]
        dvc_ref[rows, :] = dv[BLOCK:, :]

    def body(*refs):
        @pl.when(pl.program_id(0) == 0)
        def _():
            refs[-1][...] = jnp.zeros_like(refs[-1])

        for jb in range(qb):
            block(jb, *refs)

    kv = jax.ShapeDtypeStruct((s, KV_WIDTH), F32)
    return _call(
        body, [sinks, q, do, lse, k, k, v, v], name=name, grid=(s // (qb * BLOCK),),
        in_specs=[pl.BlockSpec(memory_space=pltpu.SMEM), cur(ATTN_WIDTH), cur(ATTN_WIDTH), lse_spec,
                  cur(KV_WIDTH), prev(KV_WIDTH), cur(KV_WIDTH), prev(KV_WIDTH)],
        out_specs=[cur(ATTN_WIDTH), cur(KV_WIDTH), cur(KV_WIDTH), cur(KV_WIDTH), cur(KV_WIDTH),
                   pl.BlockSpec((8, LANES), lambda i: (0, 0))],
        out_shape=[jax.ShapeDtypeStruct((s, ATTN_WIDTH), F32), kv, kv, kv, kv, jax.ShapeDtypeStruct((8, LANES), F32)],
        exchanges=exchanges)


def _pool_counts(tile, ts):
    t = (tile * ts + lax.broadcasted_iota(jnp.int32, (ts, 1), 0) + 1).astype(F32)
    return [jnp.minimum(t, float(w)) for w in POOL_WINDOWS]


def _pooled(u, halo, counts, gi):
    cols = slice(gi * POOL_GROUP_WIDTH, (gi + 1) * POOL_GROUP_WIDTH)
    acc = jnp.concatenate([halo[:, cols], u[:, cols]], axis=0)
    shift = 1
    while shift < POOL_WINDOWS[gi]:
        acc = acc + pltpu.roll(acc, shift, 0)
        shift *= 2
    return acc[POOL_HALO:, :] / counts[gi] - u[:, cols]


def _fwd_out(ao, u, pool_w, psc, w_out, x, vec, name, exchanges=()):
    s = x.shape[0]
    ts = _seq_tile(s, 2)
    hb = ts // POOL_HALO

    def body(ao_ref, u_ref, uh_ref, pw_ref, psc_ref, w_ref, x_ref, vec_ref, x1_ref, mix_ref):
        i = pl.program_id(0)
        u_t = u_ref[...]
        halo = jnp.where(i > 0, uh_ref[...], 0.0)
        counts = _pool_counts(i, ts)
        cat = [ao_ref[...]]
        for gi in range(len(POOL_WINDOWS)):
            cols = slice(gi * POOL_GROUP_WIDTH, (gi + 1) * POOL_GROUP_WIDTH)
            og = _dot(_pooled(u_t, halo, counts, gi).astype(BF16), pw_ref[gi]) * psc_ref[0:1, cols]
            cat.append(og.astype(BF16))
        mix = _dot(jnp.concatenate(cat, axis=1), w_ref[...])
        mix_ref[...] = mix
        x1_ref[...] = x_ref[...] + vec_ref[0:1, :] * (mix * _rsqrt_ms(mix) * vec_ref[1:2, :])

    row = lambda w: pl.BlockSpec((ts, w), lambda i: (i, 0))
    return _call(
        body, [ao, u, u, pool_w, psc, w_out, x, vec], name=name, grid=(s // ts,),
        in_specs=[row(ATTN_WIDTH), row(POOL_WIDTH),
                  pl.BlockSpec((POOL_HALO, POOL_WIDTH), lambda i: (jnp.maximum(i * hb - 1, 0), 0)),
                  _resident(pool_w.shape), _resident((8, POOL_WIDTH)), _resident((D_MODEL, D_MODEL)),
                  row(D_MODEL), _resident((8, D_MODEL))],
        out_specs=[row(D_MODEL), row(D_MODEL)],
        out_shape=[jax.ShapeDtypeStruct((s, D_MODEL), F32), jax.ShapeDtypeStruct((s, D_MODEL), F32)],
        exchanges=exchanges)


def _ffn_fwd(x1, vec, w_gate_t, w_up_t, w_down, name, exchanges=(), target=None):
    s = x1.shape[0]
    ts = _seq_tile(s)

    def body(x_ref, *refs):
        if target is None:
            vec_ref, wg_ref, wu_ref, wd_ref, x2_ref, f_ref, gt_ref, up_ref, h2_ref = refs
        else:
            t_ref, vec_ref, wg_ref, wu_ref, wd_ref, x2_ref, f_ref, gt_ref, up_ref, h2_ref, loss_ref = refs
        xt = x_ref[...]
        h2 = (xt * _rsqrt_ms(xt) * vec_ref[0:1, :] * vec_ref[1:2, :] + vec_ref[2:3, :]).astype(BF16)
        h2_ref[...] = h2
        f = jnp.zeros((ts, D_MODEL), F32)
        for c in range(D_FF // FF_CHUNK):
            cols = slice(c * FF_CHUNK, (c + 1) * FF_CHUNK)
            g = _dot(h2, wg_ref[cols, :], NT)
            up = _dot(h2, wu_ref[cols, :], NT)
            act = ((g * jax.nn.sigmoid(g)) * up).astype(BF16)
            gt_ref[:, cols] = g.astype(BF16)
            up_ref[:, cols] = up.astype(BF16)
            f = f + _dot(act, wd_ref[cols, :])
        f_ref[...] = f
        x2 = xt + vec_ref[3:4, :] * (f * _rsqrt_ms(f) * vec_ref[4:5, :])
        if target is None:
            x2_ref[...] = x2
        else:
            @pl.when(pl.program_id(0) == 0)
            def _():
                loss_ref[...] = jnp.zeros_like(loss_ref)

            diff = x2 - t_ref[...]
            x2_ref[...] = diff / D_MODEL
            loss_ref[...] += 0.5 * jnp.sum(jnp.mean(diff * diff, axis=-1, keepdims=True))

    row = lambda w: pl.BlockSpec((ts, w), lambda i: (i, 0))
    wide = jax.ShapeDtypeStruct((s, D_FF), BF16)
    with_loss = target is not None
    return _call(
        body, [x1] + ([target] if with_loss else []) + [vec, w_gate_t, w_up_t, w_down], name=name, grid=(s // ts,),
        in_specs=[row(D_MODEL)] * (2 if with_loss else 1) + [
            _resident((8, D_MODEL)), _resident((D_FF, D_MODEL)), _resident((D_FF, D_MODEL)), _resident((D_FF, D_MODEL))],
        out_specs=[row(D_MODEL), row(D_MODEL), row(D_FF), row(D_FF), row(D_MODEL)] + (
            [pl.BlockSpec((8, LANES), lambda i: (0, 0))] if with_loss else []),
        out_shape=[jax.ShapeDtypeStruct((s, D_MODEL), F32), jax.ShapeDtypeStruct((s, D_MODEL), F32), wide, wide,
                   jax.ShapeDtypeStruct((s, D_MODEL), BF16)] + (
            [jax.ShapeDtypeStruct((8, LANES), F32)] if with_loss else []),
        exchanges=exchanges)


def _norm_bwd(dy_hat, x_hat, r):
    return r * (dy_hat - x_hat * jnp.mean(dy_hat * x_hat, axis=-1, keepdims=True))


def _accumulate_rows(ref, rows):
    for j, val in enumerate(rows):
        ref[j:j + 1, :] += val


def _ffn_bwd_act(dx2, f, gt, up, vec, w_down, name, exchanges=()):
    s = dx2.shape[0]
    ts = _seq_tile(s)

    def body(dx2_ref, f_ref, gt_ref, up_ref, vec_ref, wd_ref, dgt_ref, dup_ref, dwd_ref, cs_ref):
        @pl.when(pl.program_id(0) == 0)
        def _():
            cs_ref[...] = jnp.zeros_like(cs_ref)
            dwd_ref[...] = jnp.zeros_like(dwd_ref)

        dx2_t, f_t = dx2_ref[...], f_ref[...]
        gate, g_post = vec_ref[3:4, :], vec_ref[4:5, :]
        rf = _rsqrt_ms(f_t)
        f_hat = f_t * rf
        df = _norm_bwd(dx2_t * gate * g_post, f_hat, rf).astype(BF16)
        _accumulate_rows(cs_ref, [_colsum(dx2_t * (f_hat * g_post)), _colsum(dx2_t * gate * f_hat)])
        for c in range(D_FF // FF_CHUNK):
            cols = slice(c * FF_CHUNK, (c + 1) * FF_CHUNK)
            dact = _dot(df, wd_ref[cols, :], NT)
            g, u_t = gt_ref[:, cols].astype(F32), up_ref[:, cols].astype(F32)
            sg = 0.5 * jnp.tanh(0.5 * g) + 0.5
            silu = g * sg
            dgt_ref[:, cols] = (dact * u_t * (sg * (1.0 + g * (1.0 - sg)))).astype(BF16)
            dup_ref[:, cols] = (dact * silu).astype(BF16)
            dwd_ref[cols, :] += _dot((silu * u_t).astype(BF16), df, TN)

    row = lambda w: pl.BlockSpec((ts, w), lambda i: (i, 0))
    wide = jax.ShapeDtypeStruct((s, D_FF), BF16)
    return _call(
        body, [dx2, f, gt, up, vec, w_down], name=name, grid=(s // ts,),
        in_specs=[row(D_MODEL), row(D_MODEL), row(D_FF), row(D_FF), _resident((8, D_MODEL)), _resident((D_FF, D_MODEL))],
        out_specs=[row(D_FF), row(D_FF), _resident((D_FF, D_MODEL)), pl.BlockSpec((8, D_MODEL), lambda i: (0, 0))],
        out_shape=[wide, wide, jax.ShapeDtypeStruct((D_FF, D_MODEL), F32), jax.ShapeDtypeStruct((8, D_MODEL), F32)],
        exchanges=exchanges)


def _ffn_bwd_in(dx2, x1, dgt, dup, vec, w_gate_t, w_up_t, name, exchanges=()):
    s = x1.shape[0]
    ts = _seq_tile(s)

    def body(dx2_ref, x1_ref, dgt_ref, dup_ref, vec_ref, wg_ref, wu_ref, dx1_ref, cs_ref):
        @pl.when(pl.program_id(0) == 0)
        def _():
            cs_ref[...] = jnp.zeros_like(cs_ref)

        g_pre, one_scale = vec_ref[0:1, :], vec_ref[1:2, :]
        dh2 = _dot(dgt_ref[...], wg_ref[...]) + _dot(dup_ref[...], wu_ref[...])
        x1_t = x1_ref[...]
        r1 = _rsqrt_ms(x1_t)
        x_hat = x1_t * r1
        dx1_ref[...] = dx2_ref[...] + _norm_bwd(dh2 * g_pre * one_scale, x_hat, r1)
        _accumulate_rows(cs_ref, [_colsum(dh2), _colsum(dh2 * (x_hat * g_pre)), _colsum(dh2 * one_scale * x_hat)])

    row = lambda w: pl.BlockSpec((ts, w), lambda i: (i, 0))
    return _call(
        body, [dx2, x1, dgt, dup, vec, w_gate_t, w_up_t], name=name, grid=(s // ts,),
        in_specs=[row(D_MODEL), row(D_MODEL), row(D_FF), row(D_FF), _resident((8, D_MODEL)),
                  _resident((D_FF, D_MODEL)), _resident((D_FF, D_MODEL))],
        out_specs=[row(D_MODEL), pl.BlockSpec((8, D_MODEL), lambda i: (0, 0))],
        out_shape=[jax.ShapeDtypeStruct((s, D_MODEL), F32), jax.ShapeDtypeStruct((8, D_MODEL), F32)],
        exchanges=exchanges)


def _weight_grad(a, b, name, exchanges=()):
    s, m = a.shape
    n = b.shape[1]
    tk = min(s, 2 * SEQ_TILE)
    steps = s // tk

    def body(a_ref, b_ref, o_ref, acc_ref):
        i = pl.program_id(0)

        @pl.when(i == 0)
        def _():
            acc_ref[...] = jnp.zeros_like(acc_ref)

        b_t = b_ref[...]
        for c in range(m // FF_CHUNK):
            rows = slice(c * FF_CHUNK, (c + 1) * FF_CHUNK)
            acc_ref[rows, :] += _dot(a_ref[:, rows], b_t, TN)

        @pl.when(i == steps - 1)
        def _():
            o_ref[...] = acc_ref[...].astype(BF16)

    (grad,), ex_outs = _call(
        body, [a, b], name=name, grid=(steps,),
        in_specs=[pl.BlockSpec((tk, m), lambda i: (i, 0)), pl.BlockSpec((tk, n), lambda i: (i, 0))],
        out_specs=[_resident((m, n))],
        out_shape=[jax.ShapeDtypeStruct((m, n), BF16)],
        scratch_shapes=[pltpu.VMEM((m, n), F32)],
        exchanges=exchanges)
    return grad, ex_outs


def _out_bwd(dx1, mix, ao, u, pool_w, psc, w_out, vec, name, exchanges=()):
    s = dx1.shape[0]
    ts = _seq_tile(s)
    nt = s // ts
    hb = ts // POOL_HALO
    ng = len(POOL_WINDOWS)

    def body(dx1_ref, mix_ref, ao_ref, u_ref, uh_ref, pw_ref, psc_ref, w_ref, vec_ref,
             do_ref, du_ref, dw_out_ref, cs_ref, dpw_ref, dpsc_ref, carry_ref, dw_ref):
        i = pl.program_id(0)
        tile = nt - 1 - i

        @pl.when(i == 0)
        def _():
            dw_ref[...] = jnp.zeros_like(dw_ref)
            cs_ref[...] = jnp.zeros_like(cs_ref)
            dpw_ref[...] = jnp.zeros_like(dpw_ref)
            dpsc_ref[...] = jnp.zeros_like(dpsc_ref)
            carry_ref[...] = jnp.zeros_like(carry_ref)

        dx1_t, mix_t = dx1_ref[...], mix_ref[...]
        gate, g_post = vec_ref[0:1, :], vec_ref[1:2, :]
        rm = _rsqrt_ms(mix_t)
        m_hat = mix_t * rm
        dmix = _norm_bwd(dx1_t * gate * g_post, m_hat, rm).astype(BF16)
        _accumulate_rows(cs_ref, [_colsum(dx1_t * (m_hat * g_post)), _colsum(dx1_t * gate * m_hat)])
        d_cat = _dot(dmix, w_ref[...], NT)
        do_ref[...] = d_cat[:, 0:ATTN_WIDTH].astype(BF16)

        u_t = u_ref[...]
        halo = jnp.where(tile > 0, uh_ref[...], 0.0)
        counts = _pool_counts(tile, ts)
        cat = [ao_ref[...]]
        for gi in range(ng):
            cols = slice(gi * POOL_GROUP_WIDTH, (gi + 1) * POOL_GROUP_WIDTH)
            scale = psc_ref[0:1, cols]
            pooled = _pooled(u_t, halo, counts, gi).astype(BF16)
            og = _dot(pooled, pw_ref[gi])
            cat.append((og * scale).astype(BF16))
            d_out = d_cat[:, ATTN_WIDTH + gi * POOL_GROUP_WIDTH:ATTN_WIDTH + (gi + 1) * POOL_GROUP_WIDTH]
            dpsc_ref[0:1, cols] += _colsum(d_out * og)
            d_og = (d_out * scale).astype(BF16)
            dpw_ref[gi] += _dot(pooled, d_og, TN)
            d_pooled = _dot(d_og, pw_ref[gi], NT)
            spread = d_pooled / counts[gi]
            acc = jnp.concatenate([spread, carry_ref[:, cols]], axis=0)
            shift = 1
            while shift < POOL_WINDOWS[gi]:
                acc = acc + pltpu.roll(acc, ts + POOL_HALO - shift, 0)
                shift *= 2
            du_ref[:, cols] = acc[0:ts, :] - d_pooled
            carry_ref[:, cols] = spread[0:POOL_HALO, :]
        dw_ref[...] += _dot(jnp.concatenate(cat, axis=1), dmix, TN)

        @pl.when(i == nt - 1)
        def _():
            dw_out_ref[...] = dw_ref[...].astype(BF16)

    row = lambda w: pl.BlockSpec((ts, w), lambda i: (nt - 1 - i, 0))
    fixed = lambda shape: pl.BlockSpec(shape, lambda i: (0,) * len(shape))
    return _call(
        body, [dx1, mix, ao, u, u, pool_w, psc, w_out, vec], name=name, grid=(nt,),
        in_specs=[row(D_MODEL), row(D_MODEL), row(ATTN_WIDTH), row(POOL_WIDTH),
                  pl.BlockSpec((POOL_HALO, POOL_WIDTH), lambda i: (jnp.maximum((nt - 1 - i) * hb - 1, 0), 0)),
                  _resident(pool_w.shape), _resident((8, POOL_WIDTH)), _resident((D_MODEL, D_MODEL)),
                  _resident((8, D_MODEL))],
        out_specs=[row(ATTN_WIDTH), row(POOL_WIDTH), _resident((D_MODEL, D_MODEL)), fixed((8, D_MODEL)),
                   fixed(pool_w.shape), fixed((8, POOL_WIDTH))],
        out_shape=[jax.ShapeDtypeStruct((s, ATTN_WIDTH), BF16), jax.ShapeDtypeStruct((s, POOL_WIDTH), F32),
                   jax.ShapeDtypeStruct((D_MODEL, D_MODEL), BF16),
                   jax.ShapeDtypeStruct((8, D_MODEL), F32), jax.ShapeDtypeStruct(pool_w.shape, F32),
                   jax.ShapeDtypeStruct((8, POOL_WIDTH), F32)],
        scratch_shapes=[pltpu.VMEM((POOL_HALO, POOL_WIDTH), F32), pltpu.VMEM((D_MODEL, D_MODEL), F32)],
        exchanges=exchanges)


def _in_bwd(dq, dkc, dkp, dvc, dvp, du, x, dx1, vec, w_in_t, rot, name, exchanges=()):
    s = x.shape[0]
    ts = _seq_tile(s)
    nt = s // ts
    bpt = ts // BLOCK

    def body(dq_ref, dkc_ref, dkp_ref, dkn_ref, dvc_ref, dvp_ref, dvn_ref, du_ref, x_ref, dx1_ref, vec_ref, w_ref,
             cos_ref, lo_ref, hi_ref, dx_ref, dw_out_ref, cs_ref, db_ref, dw_ref):
        i = pl.program_id(0)

        @pl.when(i == 0)
        def _():
            dw_ref[...] = jnp.zeros_like(dw_ref)
            cs_ref[...] = jnp.zeros_like(cs_ref)
            db_ref[...] = jnp.zeros_like(db_ref)

        cos, lo, hi = cos_ref[...], lo_ref[...], hi_ref[...]

        def with_next_block(cur_ref, prev_ref, next_ref):
            nxt = jnp.where(i < nt - 1, next_ref[...], 0.0)
            later = nxt if bpt == 1 else jnp.concatenate([prev_ref[BLOCK:, :], nxt], axis=0)
            return cur_ref[...] + later

        pieces = [_rotate_bwd(dq_ref[:, j * LANES:(j + 1) * LANES], cos, lo, hi) for j in range(ATTN_WIDTH // LANES)]
        pieces.append(_rotate_bwd(with_next_block(dkc_ref, dkp_ref, dkn_ref), cos, lo, hi))
        pieces.append(with_next_block(dvc_ref, dvp_ref, dvn_ref))
        pieces.append(du_ref[...])
        dproj = jnp.concatenate(pieces, axis=1)
        db_ref[0:1, :] += _colsum(dproj)
        dproj_b = dproj.astype(BF16)
        dh = _dot(dproj_b, w_ref[...])

        xt = x_ref[...]
        g_pre, one_scale = vec_ref[0:1, :], vec_ref[1:2, :]
        r = _rsqrt_ms(xt)
        x_hat = xt * r
        dw_ref[...] += _dot(dproj_b, (x_hat * g_pre * one_scale + vec_ref[2:3, :]).astype(BF16), TN)
        dx_ref[...] = dx1_ref[...] + _norm_bwd(dh * g_pre * one_scale, x_hat, r)
        _accumulate_rows(cs_ref, [_colsum(dh), _colsum(dh * (x_hat * g_pre)), _colsum(dh * one_scale * x_hat)])

        @pl.when(i == nt - 1)
        def _():
            dw_out_ref[...] = dw_ref[...].astype(BF16)

    row = lambda w: pl.BlockSpec((ts, w), lambda i: (i, 0))
    nxt = pl.BlockSpec((BLOCK, KV_WIDTH), lambda i: (jnp.minimum((i + 1) * bpt, s // BLOCK - 1), 0))
    fixed = lambda shape: pl.BlockSpec(shape, lambda i: (0,) * len(shape))
    return _call(
        body, [dq, dkc, dkp, dkp, dvc, dvp, dvp, du, x, dx1, vec, w_in_t, *rot], name=name, grid=(nt,),
        in_specs=[row(ATTN_WIDTH), row(KV_WIDTH), row(KV_WIDTH), nxt, row(KV_WIDTH), row(KV_WIDTH), nxt,
                  row(POOL_WIDTH), row(D_MODEL), row(D_MODEL), _resident((8, D_MODEL)), _resident((IN_WIDTH, D_MODEL)),
                  row(LANES), row(LANES), row(LANES)],
        out_specs=[row(D_MODEL), _resident((IN_WIDTH, D_MODEL)), fixed((8, D_MODEL)), fixed((8, IN_WIDTH))],
        out_shape=[jax.ShapeDtypeStruct((s, D_MODEL), F32), jax.ShapeDtypeStruct((IN_WIDTH, D_MODEL), BF16),
                   jax.ShapeDtypeStruct((8, D_MODEL), F32), jax.ShapeDtypeStruct((8, IN_WIDTH), F32)],
        scratch_shapes=[pltpu.VMEM((IN_WIDTH, D_MODEL), F32)],
        exchanges=exchanges)


def _row_tile(rows):
    for t in (512, 352, 256, 176, 160, 128, 64, 32, 16, 8):
        if rows % t == 0:
            return t
    raise ValueError(f"no row tile for {rows} rows")


def _sum_parts(parts, first, name):
    n, r, c = parts.shape
    tr = _row_tile(r)

    def body(p_ref, o_ref):
        acc = p_ref[first].astype(F32)
        for k in range(n):
            if k != first:
                acc = acc + p_ref[k].astype(F32)
        o_ref[...] = acc

    return pl.pallas_call(
        body, name=name, grid=(r // tr,),
        in_specs=[pl.BlockSpec((n, tr, c), lambda i: (0, i, 0))],
        out_specs=pl.BlockSpec((tr, c), lambda i: (i, 0)),
        out_shape=jax.ShapeDtypeStruct((r, c), F32),
        compiler_params=_cparams(1),
    )(parts)


def _adamw(w, g, m, v, name):
    r, c = w.shape
    tr = _row_tile(r) if r % 8 == 0 else r

    def body(w_ref, g_ref, m_ref, v_ref, d_ref, nm_ref, nv_ref):
        g_t = g_ref[...]
        m_new = ADAM_B1 * m_ref[...] + (1.0 - ADAM_B1) * g_t
        v_new = ADAM_B2 * v_ref[...] + (1.0 - ADAM_B2) * (g_t * g_t)
        m_hat = m_new / (1.0 - ADAM_B1 ** ADAM_STEP)
        v_hat = v_new / (1.0 - ADAM_B2 ** ADAM_STEP)
        d_ref[...] = -ADAM_LR * (m_hat / (jnp.sqrt(v_hat) + ADAM_EPS) + ADAM_WD * w_ref[...])
        nm_ref[...] = m_new
        nv_ref[...] = v_new

    spec = pl.BlockSpec((tr, c), lambda i: (i, 0))
    shape = jax.ShapeDtypeStruct((r, c), F32)
    return pl.pallas_call(
        body, name=name, grid=(r // tr,), in_specs=[spec] * 4, out_specs=[spec] * 3, out_shape=[shape] * 3,
        compiler_params=_cparams(1),
    )(w, g, m, v)


LAYER_ROWS = 80


def _pack_layer(dmod, gains, b_in, pool_scale, sinks, pool_w):
    misc = jnp.concatenate([pool_scale, sinks, jnp.zeros((D_MODEL - POOL_WIDTH - N_HEADS,), F32)])
    parts = [dmod.reshape(N_MOD, D_MODEL), gains, jnp.pad(b_in, (0, 2 * D_MODEL - IN_WIDTH)).reshape(2, D_MODEL),
             misc.reshape(1, D_MODEL), pool_w.reshape(-1, D_MODEL)]
    packed = jnp.concatenate(parts, axis=0)
    return jnp.pad(packed, ((0, LAYER_ROWS - packed.shape[0]), (0, 0)))


def _unpack_layer(p):
    r = N_MOD + 4
    b_in = p[r:r + 2].reshape(2 * D_MODEL)[:IN_WIDTH]
    pool_w = p[r + 3:r + 3 + 64].reshape(len(POOL_WINDOWS), POOL_GROUP_WIDTH, POOL_GROUP_WIDTH)
    return (p[0:N_MOD].reshape(N_MOD * D_MODEL), p[N_MOD:r], b_in, p[r + 2, :POOL_WIDTH],
            p[r + 2, POOL_WIDTH:POOL_WIDTH + N_HEADS], pool_w)


def _rotary_tables(positions):
    inv_freq = ROPE_THETA ** (-jnp.arange(0, ROT_DIM, 2, dtype=F32) / ROT_DIM)
    half = ROT_DIM // 2
    head_freq = jnp.concatenate([inv_freq, inv_freq, jnp.zeros((HEAD_DIM - ROT_DIM,), F32)])
    lane_freq = jnp.concatenate([head_freq, head_freq])
    ang = positions.reshape(-1).astype(F32)[:, None] * lane_freq[None, :]
    cos, sin = jnp.cos(ang), jnp.sin(ang)
    in_head = lax.broadcasted_iota(jnp.int32, (1, LANES), 1) % HEAD_DIM
    return cos, jnp.where(in_head < half, -sin, 0.0), jnp.where((in_head >= half) & (in_head < ROT_DIM), sin, 0.0)


def kernel(x, c, positions, ada_w, ada_b, w_in, b_in, sinks, pool_w, pool_scale, w_out, w_gate, w_up, w_down, g_pre_mix, g_post_mix, g_pre_ffn, g_post_ffn, loss_target, m_ada_w, m_ada_b, m_w_in, m_b_in, m_sinks, m_pool_w, m_pool_scale, m_w_out, m_w_gate, m_w_up, m_w_down, m_g_pre_mix, m_g_post_mix, m_g_pre_ffn, m_g_post_ffn, v_ada_w, v_ada_b, v_w_in, v_b_in, v_sinks, v_pool_w, v_pool_scale, v_w_out, v_w_gate, v_w_up, v_w_down, v_g_pre_mix, v_g_post_mix, v_g_pre_ffn, v_g_post_ffn):
    me = _index(_mesh_place())
    x0 = x[0]
    target = loss_target[0]
    rot = _rotary_tables(positions)
    ada_cols = ada_w.shape[2]

    shards = []
    for l in range(DEPTH):
        shards += [w_in[l].T.astype(BF16), w_out[l].astype(BF16), w_gate[l].T.astype(BF16), w_up[l].T.astype(BF16),
                   w_down[l].astype(BF16)]
    whole = lambda gathered: [g.reshape(-1, D_MODEL) for g in gathered]
    full = [None] * (5 * DEPTH)
    first = _exchanges_alone([_Gather(shards[0:2] + [jnp.broadcast_to(c, (8, D_MODEL))])], "gather_first")[0]
    full[0:2] = whole(first[0:2])

    mod_part, c_act = _mod_fwd(first[2][:, 0, :], ada_w)
    mod_all = _allgather_vmem(mod_part.reshape(DEPTH * N_DEV, ada_cols), "gather_mod")
    mod_mine = lax.dynamic_slice_in_dim(mod_all.reshape(N_DEV, DEPTH, N_DEV, ada_cols), me, 1, axis=2)[:, :, 0, :]
    mod = jnp.transpose(mod_mine, (1, 0, 2)).reshape(DEPTH, N_MOD * D_MODEL) + ada_b
    mod = mod.reshape(DEPTH, N_MOD, D_MODEL)

    hosted = {("fwd_in", 0): (_Gather(shards[2:3]), [2]), ("attn_fwd", 0): (_Gather(shards[3:4]), [3]),
              ("fwd_out", 0): (_Gather(shards[4:5]), [4]),
              ("ffn_fwd", 0): (_Gather(shards[5:10]), [5, 6, 7, 8, 9])}

    def take(kind, l, ex_outs):
        if (kind, l) in hosted:
            for slot, g in zip(hosted[kind, l][1], whole(ex_outs[0])):
                full[slot] = g

    def beside(kind, l):
        return [hosted[kind, l][0]] if (kind, l) in hosted else []

    pool_w_b = pool_w.astype(BF16)

    saved = []
    xl = x0
    for l in range(DEPTH):
        vec_in = _rows(g_pre_mix[l], 1.0 + mod[l, 1], mod[l, 0], width=D_MODEL)
        vec_out = _rows(mod[l, 2], g_post_mix[l], width=D_MODEL)
        vec_ffn = _rows(g_pre_ffn[l], 1.0 + mod[l, 4], mod[l, 3], mod[l, 5], g_post_ffn[l], width=D_MODEL)
        psc = _rows(pool_scale[l], width=POOL_WIDTH)
        (q, k, v, u), ex = _fwd_in(xl, vec_in, full[5 * l], _rows(b_in[l], width=IN_WIDTH), rot, f"fwd_in_{l}",
                                   beside("fwd_in", l))
        take("fwd_in", l, ex)
        (ao, lse), ex = _attn_fwd_t(sinks[l], q, k, v, f"attn_fwd_{l}", beside("attn_fwd", l))
        take("attn_fwd", l, ex)
        (x1, mix), ex = _fwd_out(ao, u, pool_w_b[l], psc, full[5 * l + 1], xl, vec_out, f"fwd_out_{l}",
                                 beside("fwd_out", l))
        take("fwd_out", l, ex)
        (x2, f, gt, up, h2, *loss_part), ex = _ffn_fwd(
            x1, vec_ffn, full[5 * l + 2], full[5 * l + 3], full[5 * l + 4], f"ffn_fwd_{l}", beside("ffn_fwd", l),
            target=target if l == DEPTH - 1 else None)
        take("ffn_fwd", l, ex)
        saved.append((xl, q, k, v, u, ao, lse, x1, mix, f, gt, up, h2, vec_in, vec_out, vec_ffn, psc))
        xl = x2

    dx = xl
    loss = lax.psum(loss_part[0][0, 0], ("x", "y", "c"))

    stacks = [None] * (5 * DEPTH)
    parts = [None] * (5 * DEPTH)
    small_all = [None] * DEPTH
    ready_stacks, ready_small = [], []

    def leaving(row_budget):
        exs, notes = [], []
        taken = []
        while ready_stacks and stacks[ready_stacks[0]].shape[1] <= row_budget:
            row_budget -= stacks[ready_stacks[0]].shape[1]
            taken.append(ready_stacks.pop(0))
        if taken:
            exs.append(_Scatter([stacks[a] for a in taken]))
            notes.append(("stacks", taken))
        if ready_small and row_budget >= 0:
            exs.append(_Gather([ready_small[0][1]]))
            notes.append(("small", ready_small[0][0]))
            ready_small.clear()
        return exs, notes

    def arrived(notes, ex_outs):
        for (kind, what), outs in zip(notes, ex_outs):
            if kind == "stacks":
                for a, o in zip(what, outs):
                    parts[a] = o
            else:
                small_all[what] = outs[0]

    def grad_ready(a, grad):
        stacks[a] = grad.reshape(N_DEV, -1, D_MODEL)
        ready_stacks.append(a)

    def hosting(row_budget, kernel_fn, *args):
        exs, notes = leaving(row_budget)
        outs, ex_outs = kernel_fn(*args, exs)
        arrived(notes, ex_outs)
        return outs

    for l in reversed(range(DEPTH)):
        w_in_t, w_out_f, w_gate_t, w_up_t, w_down_f = full[5 * l:5 * l + 5]
        xin, q, k, v, u, ao, lse, x1, mix, f, gt, up, h2, vec_in, vec_out, vec_ffn, psc = saved[l]
        dgt, dup, d_w_down, cs_a = hosting(600, _ffn_bwd_act, dx, f, gt, up, vec_ffn, w_down_f, f"ffn_bwd_act_{l}")
        grad_ready(5 * l + 4, d_w_down.astype(BF16))
        dx1, cs_f = hosting(500, _ffn_bwd_in, dx, x1, dgt, dup, vec_ffn, w_gate_t, w_up_t, f"ffn_bwd_in_{l}")
        grad_ready(5 * l + 2, hosting(-1, _weight_grad, dgt, h2, f"grad_w_gate_{l}"))
        grad_ready(5 * l + 3, hosting(-1, _weight_grad, dup, h2, f"grad_w_up_{l}"))
        do, du, d_w_out, cs_o, dpw, dpsc = hosting(500, _out_bwd, dx1, mix, ao, u, pool_w_b[l], psc, w_out_f, vec_out,
                                                   f"out_bwd_{l}")
        grad_ready(5 * l + 1, d_w_out)
        dq, dkc, dkp, dvc, dvp, dsk = hosting(480, _attn_bwd_t, sinks[l], q, do, lse, k, v, f"attn_bwd_{l}")
        dx, d_w_in_t, cs_i, db = hosting(-1, _in_bwd, dq, dkc, dkp, dvc, dvp, du, xin, dx1, vec_in, w_in_t, rot,
                                         f"in_bwd_{l}")
        grad_ready(5 * l, d_w_in_t)
        d_mod = jnp.concatenate([cs_i[0], cs_i[1], cs_o[0], cs_f[0], cs_f[1], cs_a[0]])
        d_gain = jnp.stack([cs_i[2], cs_o[1], cs_f[2], cs_a[1]])
        ready_small.append((l, _pack_layer(d_mod, d_gain, db[0], dpsc[0], dsk[:, 0], dpw)))
    grad_x = dx[None]
    exs, notes = leaving(N_DEV * D_FF)
    arrived(notes, _exchanges_alone(exs, "exchange_last"))

    layer_sums = [_unpack_layer(_sum_parts(small_all[l], 0, f"sum_small_{l}")) for l in range(DEPTH)]
    g_ada_b, g_gains, g_b_in, g_pool_scale, g_sinks, g_pool_w = (
        jnp.stack([layer_sums[l][j] for l in range(DEPTH)], axis=1 if j == 1 else 0) for j in range(6))

    dmod_all = jnp.stack([small_all[l][:, 0:N_MOD, :].reshape(N_DEV, N_DEV, ada_cols) for l in range(DEPTH)])
    dmod_cols = lax.dynamic_slice_in_dim(dmod_all, me, 1, axis=2)[:, :, 0, :]
    g_ada_w = _ada_grad(c_act, dmod_cols)

    summed = [_sum_parts(parts[a], _Scatter.OWN, f"sum_grads_{a}") for a in range(5 * DEPTH)]
    per_kind = [jnp.stack([summed[5 * l + j] for l in range(DEPTH)]) for j in range(5)]
    g_w_in, g_w_out, g_w_gate, g_w_up, g_w_down = per_kind
    held_transposed = ("w_in", "w_gate", "w_up")
    swap = lambda a: jnp.transpose(a, (0, 2, 1))

    def step(w, g, m, v, name):
        flat = lambda a: a.reshape(-1, a.shape[-1])
        if name in held_transposed:
            outs = _adamw(flat(swap(w)), flat(g), flat(swap(m)), flat(swap(v)), f"adamw_{name}")
            return swap(g), [swap(o.reshape(g.shape)) for o in outs]
        return g, [o.reshape(w.shape) for o in _adamw(flat(w), flat(g), flat(m), flat(v), f"adamw_{name}")]

    in_order = [
        ("ada_w", ada_w, g_ada_w, m_ada_w, v_ada_w), ("ada_b", ada_b, g_ada_b, m_ada_b, v_ada_b),
        ("w_in", w_in, g_w_in, m_w_in, v_w_in), ("b_in", b_in, g_b_in, m_b_in, v_b_in),
        ("sinks", sinks, g_sinks, m_sinks, v_sinks), ("pool_w", pool_w, g_pool_w, m_pool_w, v_pool_w),
        ("pool_scale", pool_scale, g_pool_scale, m_pool_scale, v_pool_scale),
        ("w_out", w_out, g_w_out, m_w_out, v_w_out), ("w_gate", w_gate, g_w_gate, m_w_gate, v_w_gate),
        ("w_up", w_up, g_w_up, m_w_up, v_w_up), ("w_down", w_down, g_w_down, m_w_down, v_w_down),
        ("g_pre_mix", g_pre_mix, g_gains[0], m_g_pre_mix, v_g_pre_mix),
        ("g_post_mix", g_post_mix, g_gains[1], m_g_post_mix, v_g_post_mix),
        ("g_pre_ffn", g_pre_ffn, g_gains[2], m_g_pre_ffn, v_g_pre_ffn),
        ("g_post_ffn", g_post_ffn, g_gains[3], m_g_post_ffn, v_g_post_ffn)]
    steps = [step(w, g, m, v, n) for n, w, g, m, v in in_order]
    grads = [g for g, _ in steps]
    deltas, new_m, new_v = ([s[j] for _, s in steps] for j in range(3))
    return (loss, grad_x, *grads, *deltas, *new_m, *new_v)
```

```python
import jax
import jax.numpy as jnp
from jax import lax
from jax.experimental import pallas as pl
from jax.experimental.pallas import tpu as pltpu

F32 = jnp.float32
BF16 = jnp.bfloat16

N_DEV = 8
DEPTH = 2
D_MODEL = 1024
HEAD_DIM = 64
N_HEADS = 8
N_KV_HEADS = 2
GROUP = N_HEADS // N_KV_HEADS
ATTN_WIDTH = N_HEADS * HEAD_DIM
KV_WIDTH = N_KV_HEADS * HEAD_DIM
POOL_WIDTH = 512
POOL_WINDOWS = (2, 4, 8, 16)
POOL_GROUP_WIDTH = 128
POOL_HALO = 16
IN_WIDTH = ATTN_WIDTH + 2 * KV_WIDTH + POOL_WIDTH
D_FF = 2816
N_MOD = 6
BLOCK = 128
ROT_DIM = 16
ROPE_THETA = 500000.0
EPS = 1e-6
NEG_INF = -1e30
Q_SCALE = HEAD_DIM ** -0.5

ADAM_LR = 0.001
ADAM_B1 = 0.9
ADAM_B2 = 0.999
ADAM_EPS = 1e-08
ADAM_WD = 0.01
ADAM_STEP = 10

LANES = 128
SEQ_TILE = 512
ATTN_BLOCKS = 4
FF_CHUNK = 256
VMEM_LIMIT = 56 * 1024 * 1024
MESH = pl.DeviceIdType.MESH

NT = (((1,), (1,)), ((), ()))
TN = (((0,), (0,)), ((), ()))


def _dot(a, b, dims=None):
    if dims is None:
        return jnp.dot(a, b, preferred_element_type=F32)
    return lax.dot_general(a, b, dims, preferred_element_type=F32)


def _cparams(n_axes):
    return pltpu.CompilerParams(dimension_semantics=("arbitrary",) * n_axes, vmem_limit_bytes=VMEM_LIMIT)


def _resident(shape):
    zeros = (0,) * len(shape)
    return pl.BlockSpec(shape, lambda *_: zeros, pipeline_mode=pl.Buffered(1))


def _rows(*vectors, width):
    rows = [jnp.reshape(v, (1, width)).astype(F32) for v in vectors]
    rows.append(jnp.zeros((8 - len(rows), width), F32))
    return jnp.concatenate(rows, axis=0)


def _rsqrt_ms(x):
    return lax.rsqrt(jnp.mean(x * x, axis=-1, keepdims=True) + EPS)


def _colsum(x):
    return jnp.sum(x, axis=0, keepdims=True)


def _seq_tile(s, tiles=1):
    return min(s, tiles * SEQ_TILE)


def _mesh_place():
    x, y, c = lax.axis_index("x"), lax.axis_index("y"), lax.axis_index("c")
    return x, y, c


def _flip(place, k):
    x, y, c = place
    return (1 - x if k & 4 else x, 1 - y if k & 2 else y, 1 - c if k & 1 else c)


def _index(place):
    x, y, c = place
    return 4 * x + 2 * y + c


def _allgather_vmem(block, name):
    r, c = block.shape

    def body(x_ref, out_ref, send_sems, recv_sems, local_sem):
        me = _mesh_place()
        mine = pltpu.make_async_copy(x_ref, out_ref.at[_index(me)], local_sem)
        mine.start()

        def copy(k):
            return pltpu.make_async_remote_copy(
                src_ref=x_ref, dst_ref=out_ref.at[_index(me)], send_sem=send_sems.at[k - 1], recv_sem=recv_sems.at[k - 1],
                device_id=_flip(me, k), device_id_type=MESH)

        def arrival(k):
            return pltpu.make_async_remote_copy(
                src_ref=x_ref, dst_ref=out_ref.at[_index(_flip(me, k))], send_sem=send_sems.at[k - 1],
                recv_sem=recv_sems.at[k - 1], device_id=_flip(me, k), device_id_type=MESH)

        for k in range(1, N_DEV):
            copy(k).start()
        for k in range(1, N_DEV):
            arrival(k).wait_recv()
        for k in range(1, N_DEV):
            copy(k).wait_send()
        mine.wait()

    return pl.pallas_call(
        body, name=name,
        out_shape=jax.ShapeDtypeStruct((N_DEV, r, c), block.dtype),
        in_specs=[pl.BlockSpec(memory_space=pltpu.VMEM)],
        out_specs=pl.BlockSpec(memory_space=pltpu.VMEM),
        scratch_shapes=[pltpu.SemaphoreType.DMA((N_DEV - 1,)), pltpu.SemaphoreType.DMA((N_DEV - 1,)),
                        pltpu.SemaphoreType.DMA],
    )(block)


class _Gather:
    def __init__(self, shards):
        n = len(shards)
        self.operands = list(shards)
        self.out_shape = [jax.ShapeDtypeStruct((N_DEV,) + s.shape, s.dtype) for s in shards]
        self.scratch = [pltpu.SemaphoreType.DMA((N_DEV - 1, n)), pltpu.SemaphoreType.DMA((N_DEV - 1, n)),
                        pltpu.SemaphoreType.DMA((n,))]

    def _copies(self, x_refs, out_refs, sems):
        send_sems, recv_sems, local_sems = sems
        n = len(x_refs)
        x, y, c = _mesh_place()
        me, sibling = (x, y, c), (x, y, 1 - c)
        chips = [(1 - x, y), (x, 1 - y), (1 - x, 1 - y)]

        def copy(k, a, block, to, from_input=False):
            rows = out_refs[a].at[_index(block)]
            return pltpu.make_async_remote_copy(
                src_ref=x_refs[a] if from_input else rows, dst_ref=rows,
                send_sem=send_sems.at[k, a], recv_sem=recv_sems.at[k, a], device_id=to, device_id_type=MESH)

        mine = [pltpu.make_async_copy(x_refs[a], out_refs[a].at[_index(me)], local_sems.at[a]) for a in range(n)]
        first = [copy(0, a, me, sibling, from_input=True) for a in range(n)]
        first += [copy(1 + j, a, me, (*chip, c), from_input=True) for j, chip in enumerate(chips) for a in range(n)]
        over_ici = [copy(1 + j, a, (*chip, c), me) for j, chip in enumerate(chips) for a in range(n)]
        passed = [copy(4 + j, a, (*chip, c), sibling) for j, chip in enumerate(chips) for a in range(n)]
        from_sibling = [copy(0, a, sibling, me) for a in range(n)]
        from_sibling += [copy(4 + j, a, (*chip, 1 - c), me) for j, chip in enumerate(chips) for a in range(n)]
        return mine, first, over_ici, passed, from_sibling

    def begin(self, x_refs, out_refs, sems):
        mine, first, _, _, _ = self._copies(x_refs, out_refs, sems)
        for cp in mine + first:
            cp.start()

    def middle(self, x_refs, out_refs, sems):
        _, _, over_ici, passed, _ = self._copies(x_refs, out_refs, sems)
        for arrived, onward in zip(over_ici, passed):
            arrived.wait_recv()
            onward.start()

    def end(self, x_refs, out_refs, sems):
        mine, first, _, passed, from_sibling = self._copies(x_refs, out_refs, sems)
        for cp in from_sibling:
            cp.wait_recv()
        for cp in first + passed:
            cp.wait_send()
        for cp in mine:
            cp.wait()


class _Scatter:
    OWN = N_DEV - 1

    def __init__(self, stacks):
        n = len(stacks)
        self.operands = list(stacks)
        self.out_shape = [jax.ShapeDtypeStruct(s.shape, s.dtype) for s in stacks]
        self.scratch = [pltpu.SemaphoreType.DMA((N_DEV - 1, n)), pltpu.SemaphoreType.DMA((N_DEV - 1, n)),
                        pltpu.SemaphoreType.DMA((n,))]

    def _copies(self, g_refs, out_refs, sems):
        send_sems, recv_sems, local_sems = sems
        me = _mesh_place()

        def copy(k, a):
            peer = _flip(me, k)
            return pltpu.make_async_remote_copy(
                src_ref=g_refs[a].at[_index(peer)], dst_ref=out_refs[a].at[k - 1],
                send_sem=send_sems.at[k - 1, a], recv_sem=recv_sems.at[k - 1, a], device_id=peer, device_id_type=MESH)

        mine = [pltpu.make_async_copy(g_refs[a].at[_index(me)], out_refs[a].at[self.OWN], local_sems.at[a])
                for a in range(len(g_refs))]
        return mine, [copy(k, a) for k in range(1, N_DEV) for a in range(len(g_refs))]

    def begin(self, g_refs, out_refs, sems):
        mine, copies = self._copies(g_refs, out_refs, sems)
        for cp in mine + copies:
            cp.start()

    def middle(self, g_refs, out_refs, sems):
        pass

    def end(self, g_refs, out_refs, sems):
        mine, copies = self._copies(g_refs, out_refs, sems)
        for cp in copies:
            cp.wait_recv()
        for cp in copies:
            cp.wait_send()
        for cp in mine:
            cp.wait()


def _call(body, args, *, name, grid, in_specs, out_specs, out_shape, scratch_shapes=(), exchanges=()):
    if not exchanges:
        outs = pl.pallas_call(body, name=name, grid=grid, in_specs=in_specs, out_specs=out_specs, out_shape=out_shape,
                              scratch_shapes=list(scratch_shapes), compiler_params=_cparams(len(grid)))(*args)
        return outs, []
    (steps,) = grid
    n_in, n_out, n_scr = len(in_specs), len(out_specs), len(scratch_shapes)
    ex_in = [len(ex.operands) for ex in exchanges]
    ex_out = [len(ex.out_shape) for ex in exchanges]
    ex_scr = [len(ex.scratch) for ex in exchanges]

    def split(refs, counts):
        parts, pos = [], 0
        for cnt in counts:
            parts.append(refs[pos:pos + cnt])
            pos += cnt
        return parts

    def wrapped(*refs):
        ins, xin, outs, xout, scr, xscr = split(refs, [n_in, sum(ex_in), n_out, sum(ex_out), n_scr, sum(ex_scr)])
        bound = list(zip(exchanges, split(xin, ex_in), split(xout, ex_out), split(xscr, ex_scr)))
        step = pl.program_id(0)

        def phase(method, at):
            @pl.when(step == at)
            def _():
                for ex, i_refs, o_refs, sems in bound:
                    getattr(ex, method)(i_refs, o_refs, sems)

        phase("begin", 0)
        phase("middle", (3 * steps) // 4)
        body(*ins, *outs, *scr)
        phase("end", steps - 1)

    any_spec = pl.BlockSpec(memory_space=pl.ANY)
    results = pl.pallas_call(
        wrapped, name=name, grid=grid,
        in_specs=list(in_specs) + [any_spec] * sum(ex_in),
        out_specs=list(out_specs) + [any_spec] * sum(ex_out),
        out_shape=list(out_shape) + [s for ex in exchanges for s in ex.out_shape],
        scratch_shapes=list(scratch_shapes) + [s for ex in exchanges for s in ex.scratch],
        compiler_params=_cparams(1),
    )(*args, *[a for ex in exchanges for a in ex.operands])
    return results[:n_out], split(results[n_out:], ex_out)


def _exchanges_alone(exchanges, name):
    def body(flag_ref):
        flag_ref[...] = jnp.zeros_like(flag_ref)

    _, ex_outs = _call(body, [], name=name, grid=(1,), in_specs=[], out_specs=[pl.BlockSpec((8, LANES), lambda i: (0, 0))],
                       out_shape=[jax.ShapeDtypeStruct((8, LANES), F32)], exchanges=exchanges)
    return ex_outs


def _mod_fwd(c_all, ada_w):
    cols = ada_w.shape[2]

    def body(c_ref, w_ref, mp_ref, act_ref):
        c = c_ref[...]
        act = c * jax.nn.sigmoid(c)
        act_ref[...] = act
        mp_ref[0] = _dot(act.astype(BF16), w_ref[0].astype(BF16))

    return pl.pallas_call(
        body, name="mod_fwd", grid=(DEPTH,),
        in_specs=[pl.BlockSpec((N_DEV, D_MODEL), lambda l: (0, 0)),
                  pl.BlockSpec((1, D_MODEL, cols), lambda l: (l, 0, 0))],
        out_specs=[pl.BlockSpec((1, N_DEV, cols), lambda l: (l, 0, 0)),
                   pl.BlockSpec((N_DEV, D_MODEL), lambda l: (0, 0))],
        out_shape=[jax.ShapeDtypeStruct((DEPTH, N_DEV, cols), F32), jax.ShapeDtypeStruct((N_DEV, D_MODEL), F32)],
        compiler_params=_cparams(1),
    )(c_all, ada_w)


def _ada_grad(c_act, dmod_cols):
    cols = dmod_cols.shape[2]

    def body(act_ref, dm_ref, g_ref):
        g_ref[0] = _dot(act_ref[...].astype(BF16), dm_ref[0].astype(BF16), TN)

    return pl.pallas_call(
        body, name="ada_grad", grid=(DEPTH,),
        in_specs=[pl.BlockSpec((N_DEV, D_MODEL), lambda l: (0, 0)),
                  pl.BlockSpec((1, N_DEV, cols), lambda l: (l, 0, 0))],
        out_specs=pl.BlockSpec((1, D_MODEL, cols), lambda l: (l, 0, 0)),
        out_shape=jax.ShapeDtypeStruct((DEPTH, D_MODEL, cols), F32),
        compiler_params=_cparams(1),
    )(c_act, dmod_cols)


def _rotate(t, cos, sin_lo, sin_hi):
    return t * cos + pltpu.roll(t, LANES - 8, 1) * sin_lo + pltpu.roll(t, 8, 1) * sin_hi


def _rotate_bwd(t, cos, sin_lo, sin_hi):
    return t * cos + pltpu.roll(t * sin_lo, 8, 1) + pltpu.roll(t * sin_hi, LANES - 8, 1)


def _fwd_in(x, vec, w_in_t, b_in, rot, name, exchanges=()):
    s = x.shape[0]
    ts = _seq_tile(s, 2)

    def body(x_ref, vec_ref, w_ref, b_ref, cos_ref, lo_ref, hi_ref, q_ref, k_ref, v_ref, u_ref):
        xt = x_ref[...]
        h = xt * _rsqrt_ms(xt) * vec_ref[0:1, :] * vec_ref[1:2, :] + vec_ref[2:3, :]
        proj = _dot(h.astype(BF16), w_ref[...], NT) + b_ref[0:1, :]
        cos, lo, hi = cos_ref[...], lo_ref[...], hi_ref[...]
        for j in range(ATTN_WIDTH // LANES):
            q_ref[:, j * LANES:(j + 1) * LANES] = (
                _rotate(proj[:, j * LANES:(j + 1) * LANES], cos, lo, hi) * Q_SCALE).astype(BF16)
        k_ref[...] = _rotate(proj[:, ATTN_WIDTH:ATTN_WIDTH + KV_WIDTH], cos, lo, hi).astype(BF16)
        v_ref[...] = proj[:, ATTN_WIDTH + KV_WIDTH:ATTN_WIDTH + 2 * KV_WIDTH].astype(BF16)
        u_ref[...] = proj[:, ATTN_WIDTH + 2 * KV_WIDTH:]

    row = lambda w: pl.BlockSpec((ts, w), lambda i: (i, 0))
    return _call(
        body, [x, vec, w_in_t, b_in, *rot], name=name, grid=(s // ts,),
        in_specs=[row(D_MODEL), _resident((8, D_MODEL)), _resident((IN_WIDTH, D_MODEL)), _resident((8, IN_WIDTH)),
                  row(LANES), row(LANES), row(LANES)],
        out_specs=[row(ATTN_WIDTH), row(KV_WIDTH), row(KV_WIDTH), row(POOL_WIDTH)],
        out_shape=[jax.ShapeDtypeStruct((s, ATTN_WIDTH), BF16), jax.ShapeDtypeStruct((s, KV_WIDTH), BF16),
                   jax.ShapeDtypeStruct((s, KV_WIDTH), BF16), jax.ShapeDtypeStruct((s, POOL_WIDTH), F32)],
        exchanges=exchanges)


def _band_mask_t(first_block):
    key = lax.broadcasted_iota(jnp.int32, (2 * BLOCK, 2 * BLOCK), 0)
    qry = lax.broadcasted_iota(jnp.int32, (2 * BLOCK, 2 * BLOCK), 1) & (BLOCK - 1)
    d = key - qry
    return (d >= 1) & (d <= BLOCK) & ((key >= BLOCK) | jnp.logical_not(first_block))


def _placed(both, kh, b):
    lane = lax.broadcasted_iota(jnp.int32, (1, LANES), 1)
    src = both if kh == b else pltpu.roll(both, HEAD_DIM, 1)
    return jnp.where((lane >= b * HEAD_DIM) & (lane < (b + 1) * HEAD_DIM), src, jnp.zeros_like(src))


def _pair(ref, rows, kh, r):
    j = 2 * kh + r
    return ref[rows, j * LANES:(j + 1) * LANES]


def _attn_specs(s):
    qb = min(ATTN_BLOCKS, s // BLOCK)
    cur = lambda w: pl.BlockSpec((qb * BLOCK, w), lambda i: (i, 0))
    prev = lambda w: pl.BlockSpec((BLOCK, w), lambda i: (jnp.maximum(i * qb - 1, 0), 0))
    return qb, cur, prev, pl.BlockSpec((N_HEADS, qb * BLOCK), lambda i: (0, i))


def _kv_window(prev_ref, cur_ref, jb):
    before = prev_ref[...] if jb == 0 else cur_ref[(jb - 1) * BLOCK:jb * BLOCK, :]
    return jnp.concatenate([before, cur_ref[jb * BLOCK:(jb + 1) * BLOCK, :]], axis=0)


def _sink_row(sink_ref, kh, b):
    lane = lax.broadcasted_iota(jnp.int32, (1, 2 * BLOCK), 1)
    return jnp.where(lane < BLOCK, sink_ref[GROUP * kh + b], sink_ref[GROUP * kh + 2 + b])


def _attn_fwd_t(sinks, q, k, v, name, exchanges=()):
    s = q.shape[0]
    qb, cur, prev, lse_spec = _attn_specs(s)

    def block(jb, sink_ref, q_ref, kc_ref, kp_ref, vc_ref, vp_ref, o_ref, l_ref):
        rows = slice(jb * BLOCK, (jb + 1) * BLOCK)
        valid = _band_mask_t(pl.program_id(0) == 0 if jb == 0 else False)
        k_both, v_both = _kv_window(kp_ref, kc_ref, jb), _kv_window(vp_ref, vc_ref, jb)
        for kh in range(N_KV_HEADS):
            qg = jnp.concatenate([_pair(q_ref, rows, kh, 0), _pair(q_ref, rows, kh, 1)], axis=0)
            keys = jnp.concatenate([_placed(k_both, kh, 0), _placed(k_both, kh, 1)], axis=0)
            st = _dot(keys, qg, NT)
            out = jnp.zeros((2 * BLOCK, LANES), F32)
            for b in range(2):
                sc = jnp.where(valid, st[b * 2 * BLOCK:(b + 1) * 2 * BLOCK, :], NEG_INF)
                sink = _sink_row(sink_ref, kh, b)
                m = jnp.maximum(jnp.max(sc, axis=0, keepdims=True), sink)
                p = jnp.exp(sc - m)
                den = jnp.sum(p, axis=0, keepdims=True) + jnp.exp(sink - m)
                out = out + _dot((p * (1.0 / den)).astype(BF16), _placed(v_both, kh, b), TN)
                lse = m + jnp.log(den)
                for r in range(2):
                    h = GROUP * kh + 2 * r + b
                    l_ref[h:h + 1, rows] = lse[:, r * BLOCK:(r + 1) * BLOCK]
            for r in range(2):
                j = 2 * kh + r
                o_ref[rows, j * LANES:(j + 1) * LANES] = out[r * BLOCK:(r + 1) * BLOCK, :].astype(BF16)

    def body(*refs):
        for jb in range(qb):
            block(jb, *refs)

    return _call(
        body, [sinks, q, k, k, v, v], name=name, grid=(s // (qb * BLOCK),),
        in_specs=[pl.BlockSpec(memory_space=pltpu.SMEM), cur(ATTN_WIDTH), cur(KV_WIDTH), prev(KV_WIDTH),
                  cur(KV_WIDTH), prev(KV_WIDTH)],
        out_specs=[cur(ATTN_WIDTH), lse_spec],
        out_shape=[jax.ShapeDtypeStruct((s, ATTN_WIDTH), BF16), jax.ShapeDtypeStruct((N_HEADS, s), F32)],
        exchanges=exchanges)


def _attn_bwd_t(sinks, q, do, lse, k, v, name, exchanges=()):
    s = q.shape[0]
    qb, cur, prev, lse_spec = _attn_specs(s)

    def block(jb, sink_ref, q_ref, do_ref, l_ref, kc_ref, kp_ref, vc_ref, vp_ref,
              dq_ref, dkc_ref, dkp_ref, dvc_ref, dvp_ref, dsink_ref):
        rows = slice(jb * BLOCK, (jb + 1) * BLOCK)
        valid = _band_mask_t(pl.program_id(0) == 0 if jb == 0 else False)
        k_both, v_both = _kv_window(kp_ref, kc_ref, jb), _kv_window(vp_ref, vc_ref, jb)
        lane = lax.broadcasted_iota(jnp.int32, (1, LANES), 1)
        col = lax.broadcasted_iota(jnp.int32, (1, 2 * BLOCK), 1)
        dk_heads, dv_heads = [], []
        for kh in range(N_KV_HEADS):
            qg = jnp.concatenate([_pair(q_ref, rows, kh, 0), _pair(q_ref, rows, kh, 1)], axis=0)
            dog = jnp.concatenate([_pair(do_ref, rows, kh, 0), _pair(do_ref, rows, kh, 1)], axis=0)
            k_placed = [_placed(k_both, kh, b) for b in range(2)]
            v_placed = [_placed(v_both, kh, b) for b in range(2)]
            st = _dot(jnp.concatenate(k_placed, axis=0), qg, NT)
            dpt = _dot(jnp.concatenate(v_placed, axis=0), dog, NT)
            dqg = jnp.zeros((2 * BLOCK, LANES), F32)
            dk_b, dv_b = [], []
            for b in range(2):
                part = slice(b * 2 * BLOCK, (b + 1) * 2 * BLOCK)
                heads = [GROUP * kh + 2 * r + b for r in range(2)]
                lse_row = jnp.concatenate([l_ref[h:h + 1, rows] for h in heads], axis=1)
                p = jnp.where(valid, jnp.exp(st[part, :] - lse_row), 0.0)
                dp = dpt[part, :]
                delta = jnp.sum(p * dp, axis=0, keepdims=True)
                sink_pull = jnp.exp(_sink_row(sink_ref, kh, b) - lse_row) * delta
                dsink_ref[heads[0]:heads[0] + 1, :] += -jnp.sum(jnp.where(col < BLOCK, sink_pull, 0.0))
                dsink_ref[heads[1]:heads[1] + 1, :] += -jnp.sum(jnp.where(col < BLOCK, 0.0, sink_pull))
                ds = (p * (dp - delta)).astype(BF16)
                dqg = dqg + _dot(ds, k_placed[b], TN)
                dk_b.append(_dot(ds, qg))
                dv_b.append(_dot(p.astype(BF16), dog))
            for parts, total in ((dk_b, dk_heads), (dv_b, dv_heads)):
                kept = jnp.where(lane < HEAD_DIM, parts[0], parts[1])
                total.append(kept + pltpu.roll(kept, HEAD_DIM, 1))
            for r in range(2):
                j = 2 * kh + r
                dq_ref[rows, j * LANES:(j + 1) * LANES] = dqg[r * BLOCK:(r + 1) * BLOCK, :] * Q_SCALE
        dk = jnp.where(lane < HEAD_DIM, dk_heads[0], dk_heads[1])
        dv = jnp.where(lane < HEAD_DIM, dv_heads[0], dv_heads[1])
        dkp_ref[rows, :] = dk[0:BLOCK, :]
        dkc_ref[rows, :] = dk[BLOCK:, :]
        dvp_ref[rows, :] = dv[0:BLOCK, :]
        dvc_ref[rows, :] = dv[BLOCK:, :]

    def body(*refs):
        @pl.when(pl.program_id(0) == 0)
        def _():
            refs[-1][...] = jnp.zeros_like(refs[-1])

        for jb in range(qb):
            block(jb, *refs)

    kv = jax.ShapeDtypeStruct((s, KV_WIDTH), F32)
    return _call(
        body, [sinks, q, do, lse, k, k, v, v], name=name, grid=(s // (qb * BLOCK),),
        in_specs=[pl.BlockSpec(memory_space=pltpu.SMEM), cur(ATTN_WIDTH), cur(ATTN_WIDTH), lse_spec,
                  cur(KV_WIDTH), prev(KV_WIDTH), cur(KV_WIDTH), prev(KV_WIDTH)],
        out_specs=[cur(ATTN_WIDTH), cur(KV_WIDTH), cur(KV_WIDTH), cur(KV_WIDTH), cur(KV_WIDTH),
                   pl.BlockSpec((8, LANES), lambda i: (0, 0))],
        out_shape=[jax.ShapeDtypeStruct((s, ATTN_WIDTH), F32), kv, kv, kv, kv, jax.ShapeDtypeStruct((8, LANES), F32)],
        exchanges=exchanges)


def _pool_counts(tile, ts):
    t = (tile * ts + lax.broadcasted_iota(jnp.int32, (ts, 1), 0) + 1).astype(F32)
    return [jnp.minimum(t, float(w)) for w in POOL_WINDOWS]


def _pooled(u, halo, counts, gi):
    cols = slice(gi * POOL_GROUP_WIDTH, (gi + 1) * POOL_GROUP_WIDTH)
    acc = jnp.concatenate([halo[:, cols], u[:, cols]], axis=0)
    shift = 1
    while shift < POOL_WINDOWS[gi]:
        acc = acc + pltpu.roll(acc, shift, 0)
        shift *= 2
    return acc[POOL_HALO:, :] / counts[gi] - u[:, cols]


def _fwd_out(ao, u, pool_w, psc, w_out, x, vec, name, exchanges=()):
    s = x.shape[0]
    ts = _seq_tile(s, 2)
    hb = ts // POOL_HALO

    def body(ao_ref, u_ref, uh_ref, pw_ref, psc_ref, w_ref, x_ref, vec_ref, x1_ref, mix_ref):
        i = pl.program_id(0)
        u_t = u_ref[...]
        halo = jnp.where(i > 0, uh_ref[...], 0.0)
        counts = _pool_counts(i, ts)
        cat = [ao_ref[...]]
        for gi in range(len(POOL_WINDOWS)):
            cols = slice(gi * POOL_GROUP_WIDTH, (gi + 1) * POOL_GROUP_WIDTH)
            og = _dot(_pooled(u_t, halo, counts, gi).astype(BF16), pw_ref[gi]) * psc_ref[0:1, cols]
            cat.append(og.astype(BF16))
        mix = _dot(jnp.concatenate(cat, axis=1), w_ref[...])
        mix_ref[...] = mix
        x1_ref[...] = x_ref[...] + vec_ref[0:1, :] * (mix * _rsqrt_ms(mix) * vec_ref[1:2, :])

    row = lambda w: pl.BlockSpec((ts, w), lambda i: (i, 0))
    return _call(
        body, [ao, u, u, pool_w, psc, w_out, x, vec], name=name, grid=(s // ts,),
        in_specs=[row(ATTN_WIDTH), row(POOL_WIDTH),
                  pl.BlockSpec((POOL_HALO, POOL_WIDTH), lambda i: (jnp.maximum(i * hb - 1, 0), 0)),
                  _resident(pool_w.shape), _resident((8, POOL_WIDTH)), _resident((D_MODEL, D_MODEL)),
                  row(D_MODEL), _resident((8, D_MODEL))],
        out_specs=[row(D_MODEL), row(D_MODEL)],
        out_shape=[jax.ShapeDtypeStruct((s, D_MODEL), F32), jax.ShapeDtypeStruct((s, D_MODEL), F32)],
        exchanges=exchanges)


def _ffn_fwd(x1, vec, w_gate_t, w_up_t, w_down, name, exchanges=(), target=None):
    s = x1.shape[0]
    ts = _seq_tile(s)

    def body(x_ref, *refs):
        if target is None:
            vec_ref, wg_ref, wu_ref, wd_ref, x2_ref, f_ref, gt_ref, up_ref, h2_ref = refs
        else:
            t_ref, vec_ref, wg_ref, wu_ref, wd_ref, x2_ref, f_ref, gt_ref, up_ref, h2_ref, loss_ref = refs
        xt = x_ref[...]
        h2 = (xt * _rsqrt_ms(xt) * vec_ref[0:1, :] * vec_ref[1:2, :] + vec_ref[2:3, :]).astype(BF16)
        h2_ref[...] = h2
        f = jnp.zeros((ts, D_MODEL), F32)
        for c in range(D_FF // FF_CHUNK):
            cols = slice(c * FF_CHUNK, (c + 1) * FF_CHUNK)
            g = _dot(h2, wg_ref[cols, :], NT)
            up = _dot(h2, wu_ref[cols, :], NT)
            act = ((g * jax.nn.sigmoid(g)) * up).astype(BF16)
            gt_ref[:, cols] = g.astype(BF16)
            up_ref[:, cols] = up.astype(BF16)
            f = f + _dot(act, wd_ref[cols, :])
        f_ref[...] = f
        x2 = xt + vec_ref[3:4, :] * (f * _rsqrt_ms(f) * vec_ref[4:5, :])
        if target is None:
            x2_ref[...] = x2
        else:
            @pl.when(pl.program_id(0) == 0)
            def _():
                loss_ref[...] = jnp.zeros_like(loss_ref)

            diff = x2 - t_ref[...]
            x2_ref[...] = diff / D_MODEL
            loss_ref[...] += 0.5 * jnp.sum(jnp.mean(diff * diff, axis=-1, keepdims=True))

    row = lambda w: pl.BlockSpec((ts, w), lambda i: (i, 0))
    wide = jax.ShapeDtypeStruct((s, D_FF), BF16)
    with_loss = target is not None
    return _call(
        body, [x1] + ([target] if with_loss else []) + [vec, w_gate_t, w_up_t, w_down], name=name, grid=(s // ts,),
        in_specs=[row(D_MODEL)] * (2 if with_loss else 1) + [
            _resident((8, D_MODEL)), _resident((D_FF, D_MODEL)), _resident((D_FF, D_MODEL)), _resident((D_FF, D_MODEL))],
        out_specs=[row(D_MODEL), row(D_MODEL), row(D_FF), row(D_FF), row(D_MODEL)] + (
            [pl.BlockSpec((8, LANES), lambda i: (0, 0))] if with_loss else []),
        out_shape=[jax.ShapeDtypeStruct((s, D_MODEL), F32), jax.ShapeDtypeStruct((s, D_MODEL), F32), wide, wide,
                   jax.ShapeDtypeStruct((s, D_MODEL), BF16)] + (
            [jax.ShapeDtypeStruct((8, LANES), F32)] if with_loss else []),
        exchanges=exchanges)


def _norm_bwd(dy_hat, x_hat, r):
    return r * (dy_hat - x_hat * jnp.mean(dy_hat * x_hat, axis=-1, keepdims=True))


def _accumulate_rows(ref, rows):
    for j, val in enumerate(rows):
        ref[j:j + 1, :] += val


def _ffn_bwd_act(dx2, f, gt, up, vec, w_down, name, exchanges=()):
    s = dx2.shape[0]
    ts = _seq_tile(s)

    def body(dx2_ref, f_ref, gt_ref, up_ref, vec_ref, wd_ref, dgt_ref, dup_ref, dwd_ref, cs_ref):
        @pl.when(pl.program_id(0) == 0)
        def _():
            cs_ref[...] = jnp.zeros_like(cs_ref)
            dwd_ref[...] = jnp.zeros_like(dwd_ref)

        dx2_t, f_t = dx2_ref[...], f_ref[...]
        gate, g_post = vec_ref[3:4, :], vec_ref[4:5, :]
        rf = _rsqrt_ms(f_t)
        f_hat = f_t * rf
        df = _norm_bwd(dx2_t * gate * g_post, f_hat, rf).astype(BF16)
        _accumulate_rows(cs_ref, [_colsum(dx2_t * (f_hat * g_post)), _colsum(dx2_t * gate * f_hat)])
        for c in range(D_FF // FF_CHUNK):
            cols = slice(c * FF_CHUNK, (c + 1) * FF_CHUNK)
            dact = _dot(df, wd_ref[cols, :], NT)
            g, u_t = gt_ref[:, cols].astype(F32), up_ref[:, cols].astype(F32)
            sg = 0.5 * jnp.tanh(0.5 * g) + 0.5
            silu = g * sg
            dgt_ref[:, cols] = (dact * u_t * (sg * (1.0 + g * (1.0 - sg)))).astype(BF16)
            dup_ref[:, cols] = (dact * silu).astype(BF16)
            dwd_ref[cols, :] += _dot((silu * u_t).astype(BF16), df, TN)

    row = lambda w: pl.BlockSpec((ts, w), lambda i: (i, 0))
    wide = jax.ShapeDtypeStruct((s, D_FF), BF16)
    return _call(
        body, [dx2, f, gt, up, vec, w_down], name=name, grid=(s // ts,),
        in_specs=[row(D_MODEL), row(D_MODEL), row(D_FF), row(D_FF), _resident((8, D_MODEL)), _resident((D_FF, D_MODEL))],
        out_specs=[row(D_FF), row(D_FF), _resident((D_FF, D_MODEL)), pl.BlockSpec((8, D_MODEL), lambda i: (0, 0))],
        out_shape=[wide, wide, jax.ShapeDtypeStruct((D_FF, D_MODEL), F32), jax.ShapeDtypeStruct((8, D_MODEL), F32)],
        exchanges=exchanges)


def _ffn_bwd_in(dx2, x1, dgt, dup, vec, w_gate_t, w_up_t, name, exchanges=()):
    s = x1.shape[0]
    ts = _seq_tile(s)

    def body(dx2_ref, x1_ref, dgt_ref, dup_ref, vec_ref, wg_ref, wu_ref, dx1_ref, cs_ref):
        @pl.when(pl.program_id(0) == 0)
        def _():
            cs_ref[...] = jnp.zeros_like(cs_ref)

        g_pre, one_scale = vec_ref[0:1, :], vec_ref[1:2, :]
        dh2 = _dot(dgt_ref[...], wg_ref[...]) + _dot(dup_ref[...], wu_ref[...])
        x1_t = x1_ref[...]
        r1 = _rsqrt_ms(x1_t)
        x_hat = x1_t * r1
        dx1_ref[...] = dx2_ref[...] + _norm_bwd(dh2 * g_pre * one_scale, x_hat, r1)
        _accumulate_rows(cs_ref, [_colsum(dh2), _colsum(dh2 * (x_hat * g_pre)), _colsum(dh2 * one_scale * x_hat)])

    row = lambda w: pl.BlockSpec((ts, w), lambda i: (i, 0))
    return _call(
        body, [dx2, x1, dgt, dup, vec, w_gate_t, w_up_t], name=name, grid=(s // ts,),
        in_specs=[row(D_MODEL), row(D_MODEL), row(D_FF), row(D_FF), _resident((8, D_MODEL)),
                  _resident((D_FF, D_MODEL)), _resident((D_FF, D_MODEL))],
        out_specs=[row(D_MODEL), pl.BlockSpec((8, D_MODEL), lambda i: (0, 0))],
        out_shape=[jax.ShapeDtypeStruct((s, D_MODEL), F32), jax.ShapeDtypeStruct((8, D_MODEL), F32)],
        exchanges=exchanges)


def _weight_grad(a, b, name, exchanges=()):
    s, m = a.shape
    n = b.shape[1]
    tk = min(s, 2 * SEQ_TILE)
    steps = s // tk

    def body(a_ref, b_ref, o_ref, acc_ref):
        i = pl.program_id(0)

        @pl.when(i == 0)
        def _():
            acc_ref[...] = jnp.zeros_like(acc_ref)

        b_t = b_ref[...]
        for c in range(m // FF_CHUNK):
            rows = slice(c * FF_CHUNK, (c + 1) * FF_CHUNK)
            acc_ref[rows, :] += _dot(a_ref[:, rows], b_t, TN)

        @pl.when(i == steps - 1)
        def _():
            o_ref[...] = acc_ref[...].astype(BF16)

    (grad,), ex_outs = _call(
        body, [a, b], name=name, grid=(steps,),
        in_specs=[pl.BlockSpec((tk, m), lambda i: (i, 0)), pl.BlockSpec((tk, n), lambda i: (i, 0))],
        out_specs=[_resident((m, n))],
        out_shape=[jax.ShapeDtypeStruct((m, n), BF16)],
        scratch_shapes=[pltpu.VMEM((m, n), F32)],
        exchanges=exchanges)
    return grad, ex_outs


def _out_bwd(dx1, mix, ao, u, pool_w, psc, w_out, vec, name, exchanges=()):
    s = dx1.shape[0]
    ts = _seq_tile(s)
    nt = s // ts
    hb = ts // POOL_HALO
    ng = len(POOL_WINDOWS)

    def body(dx1_ref, mix_ref, ao_ref, u_ref, uh_ref, pw_ref, psc_ref, w_ref, vec_ref,
             do_ref, du_ref, dw_out_ref, cs_ref, dpw_ref, dpsc_ref, carry_ref, dw_ref):
        i = pl.program_id(0)
        tile = nt - 1 - i

        @pl.when(i == 0)
        def _():
            dw_ref[...] = jnp.zeros_like(dw_ref)
            cs_ref[...] = jnp.zeros_like(cs_ref)
            dpw_ref[...] = jnp.zeros_like(dpw_ref)
            dpsc_ref[...] = jnp.zeros_like(dpsc_ref)
            carry_ref[...] = jnp.zeros_like(carry_ref)

        dx1_t, mix_t = dx1_ref[...], mix_ref[...]
        gate, g_post = vec_ref[0:1, :], vec_ref[1:2, :]
        rm = _rsqrt_ms(mix_t)
        m_hat = mix_t * rm
        dmix = _norm_bwd(dx1_t * gate * g_post, m_hat, rm).astype(BF16)
        _accumulate_rows(cs_ref, [_colsum(dx1_t * (m_hat * g_post)), _colsum(dx1_t * gate * m_hat)])
        d_cat = _dot(dmix, w_ref[...], NT)
        do_ref[...] = d_cat[:, 0:ATTN_WIDTH].astype(BF16)

        u_t = u_ref[...]
        halo = jnp.where(tile > 0, uh_ref[...], 0.0)
        counts = _pool_counts(tile, ts)
        cat = [ao_ref[...]]
        for gi in range(ng):
            cols = slice(gi * POOL_GROUP_WIDTH, (gi + 1) * POOL_GROUP_WIDTH)
            scale = psc_ref[0:1, cols]
            pooled = _pooled(u_t, halo, counts, gi).astype(BF16)
            og = _dot(pooled, pw_ref[gi])
            cat.append((og * scale).astype(BF16))
            d_out = d_cat[:, ATTN_WIDTH + gi * POOL_GROUP_WIDTH:ATTN_WIDTH + (gi + 1) * POOL_GROUP_WIDTH]
            dpsc_ref[0:1, cols] += _colsum(d_out * og)
            d_og = (d_out * scale).astype(BF16)
            dpw_ref[gi] += _dot(pooled, d_og, TN)
            d_pooled = _dot(d_og, pw_ref[gi], NT)
            spread = d_pooled / counts[gi]
            acc = jnp.concatenate([spread, carry_ref[:, cols]], axis=0)
            shift = 1
            while shift < POOL_WINDOWS[gi]:
                acc = acc + pltpu.roll(acc, ts + POOL_HALO - shift, 0)
                shift *= 2
            du_ref[:, cols] = acc[0:ts, :] - d_pooled
            carry_ref[:, cols] = spread[0:POOL_HALO, :]
        dw_ref[...] += _dot(jnp.concatenate(cat, axis=1), dmix, TN)

        @pl.when(i == nt - 1)
        def _():
            dw_out_ref[...] = dw_ref[...].astype(BF16)

    row = lambda w: pl.BlockSpec((ts, w), lambda i: (nt - 1 - i, 0))
    fixed = lambda shape: pl.BlockSpec(shape, lambda i: (0,) * len(shape))
    return _call(
        body, [dx1, mix, ao, u, u, pool_w, psc, w_out, vec], name=name, grid=(nt,),
        in_specs=[row(D_MODEL), row(D_MODEL), row(ATTN_WIDTH), row(POOL_WIDTH),
                  pl.BlockSpec((POOL_HALO, POOL_WIDTH), lambda i: (jnp.maximum((nt - 1 - i) * hb - 1, 0), 0)),
                  _resident(pool_w.shape), _resident((8, POOL_WIDTH)), _resident((D_MODEL, D_MODEL)),
                  _resident((8, D_MODEL))],
        out_specs=[row(ATTN_WIDTH), row(POOL_WIDTH), _resident((D_MODEL, D_MODEL)), fixed((8, D_MODEL)),
                   fixed(pool_w.shape), fixed((8, POOL_WIDTH))],
        out_shape=[jax.ShapeDtypeStruct((s, ATTN_WIDTH), BF16), jax.ShapeDtypeStruct((s, POOL_WIDTH), F32),
                   jax.ShapeDtypeStruct((D_MODEL, D_MODEL), BF16),
                   jax.ShapeDtypeStruct((8, D_MODEL), F32), jax.ShapeDtypeStruct(pool_w.shape, F32),
                   jax.ShapeDtypeStruct((8, POOL_WIDTH), F32)],
        scratch_shapes=[pltpu.VMEM((POOL_HALO, POOL_WIDTH), F32), pltpu.VMEM((D_MODEL, D_MODEL), F32)],
        exchanges=exchanges)


def _in_bwd(dq, dkc, dkp, dvc, dvp, du, x, dx1, vec, w_in_t, rot, name, exchanges=()):
    s = x.shape[0]
    ts = _seq_tile(s)
    nt = s // ts
    bpt = ts // BLOCK

    def body(dq_ref, dkc_ref, dkp_ref, dkn_ref, dvc_ref, dvp_ref, dvn_ref, du_ref, x_ref, dx1_ref, vec_ref, w_ref,
             cos_ref, lo_ref, hi_ref, dx_ref, dw_out_ref, cs_ref, db_ref, dw_ref):
        i = pl.program_id(0)

        @pl.when(i == 0)
        def _():
            dw_ref[...] = jnp.zeros_like(dw_ref)
            cs_ref[...] = jnp.zeros_like(cs_ref)
            db_ref[...] = jnp.zeros_like(db_ref)

        cos, lo, hi = cos_ref[...], lo_ref[...], hi_ref[...]

        def with_next_block(cur_ref, prev_ref, next_ref):
            nxt = jnp.where(i < nt - 1, next_ref[...], 0.0)
            later = nxt if bpt == 1 else jnp.concatenate([prev_ref[BLOCK:, :], nxt], axis=0)
            return cur_ref[...] + later

        pieces = [_rotate_bwd(dq_ref[:, j * LANES:(j + 1) * LANES], cos, lo, hi) for j in range(ATTN_WIDTH // LANES)]
        pieces.append(_rotate_bwd(with_next_block(dkc_ref, dkp_ref, dkn_ref), cos, lo, hi))
        pieces.append(with_next_block(dvc_ref, dvp_ref, dvn_ref))
        pieces.append(du_ref[...])
        dproj = jnp.concatenate(pieces, axis=1)
        db_ref[0:1, :] += _colsum(dproj)
        dproj_b = dproj.astype(BF16)
        dh = _dot(dproj_b, w_ref[...])

        xt = x_ref[...]
        g_pre, one_scale = vec_ref[0:1, :], vec_ref[1:2, :]
        r = _rsqrt_ms(xt)
        x_hat = xt * r
        dw_ref[...] += _dot(dproj_b, (x_hat * g_pre * one_scale + vec_ref[2:3, :]).astype(BF16), TN)
        dx_ref[...] = dx1_ref[...] + _norm_bwd(dh * g_pre * one_scale, x_hat, r)
        _accumulate_rows(cs_ref, [_colsum(dh), _colsum(dh * (x_hat * g_pre)), _colsum(dh * one_scale * x_hat)])

        @pl.when(i == nt - 1)
        def _():
            dw_out_ref[...] = dw_ref[...].astype(BF16)

    row = lambda w: pl.BlockSpec((ts, w), lambda i: (i, 0))
    nxt = pl.BlockSpec((BLOCK, KV_WIDTH), lambda i: (jnp.minimum((i + 1) * bpt, s // BLOCK - 1), 0))
    fixed = lambda shape: pl.BlockSpec(shape, lambda i: (0,) * len(shape))
    return _call(
        body, [dq, dkc, dkp, dkp, dvc, dvp, dvp, du, x, dx1, vec, w_in_t, *rot], name=name, grid=(nt,),
        in_specs=[row(ATTN_WIDTH), row(KV_WIDTH), row(KV_WIDTH), nxt, row(KV_WIDTH), row(KV_WIDTH), nxt,
                  row(POOL_WIDTH), row(D_MODEL), row(D_MODEL), _resident((8, D_MODEL)), _resident((IN_WIDTH, D_MODEL)),
                  row(LANES), row(LANES), row(LANES)],
        out_specs=[row(D_MODEL), _resident((IN_WIDTH, D_MODEL)), fixed((8, D_MODEL)), fixed((8, IN_WIDTH))],
        out_shape=[jax.ShapeDtypeStruct((s, D_MODEL), F32), jax.ShapeDtypeStruct((IN_WIDTH, D_MODEL), BF16),
                   jax.ShapeDtypeStruct((8, D_MODEL), F32), jax.ShapeDtypeStruct((8, IN_WIDTH), F32)],
        scratch_shapes=[pltpu.VMEM((IN_WIDTH, D_MODEL), F32)],
        exchanges=exchanges)


def _row_tile(rows):
    for t in (512, 352, 256, 176, 160, 128, 64, 32, 16, 8):
        if rows % t == 0:
            return t
    raise ValueError(f"no row tile for {rows} rows")


def _sum_parts(parts, first, name):
    n, r, c = parts.shape
    tr = _row_tile(r)

    def body(p_ref, o_ref):
        acc = p_ref[first].astype(F32)
        for k in range(n):
            if k != first:
                acc = acc + p_ref[k].astype(F32)
        o_ref[...] = acc

    return pl.pallas_call(
        body, name=name, grid=(r // tr,),
        in_specs=[pl.BlockSpec((n, tr, c), lambda i: (0, i, 0))],
        out_specs=pl.BlockSpec((tr, c), lambda i: (i, 0)),
        out_shape=jax.ShapeDtypeStruct((r, c), F32),
        compiler_params=_cparams(1),
    )(parts)


def _adam_step(w, g, m, v):
    m_new = ADAM_B1 * m + (1.0 - ADAM_B1) * g
    v_new = ADAM_B2 * v + (1.0 - ADAM_B2) * (g * g)
    m_hat = m_new / (1.0 - ADAM_B1 ** ADAM_STEP)
    v_hat = v_new / (1.0 - ADAM_B2 ** ADAM_STEP)
    return -ADAM_LR * (m_hat / (jnp.sqrt(v_hat) + ADAM_EPS) + ADAM_WD * w), m_new, v_new


def _sum_adamw(parts, w, m, v, name):
    depth, r, c = w.shape
    tr = _row_tile(r)

    def body(*refs):
        p_refs = refs[:depth]
        w_ref, m_ref, v_ref, g_ref, d_ref, nm_ref, nv_ref = refs[depth:]
        g = None
        for layer, p_ref in enumerate(p_refs):
            acc = p_ref[_Scatter.OWN].astype(F32)
            for k in range(N_DEV - 1):
                acc = acc + p_ref[k].astype(F32)
            g = acc if g is None else jnp.where(pl.program_id(0) == layer, acc, g)
        g_ref[0] = g
        d_ref[0], nm_ref[0], nv_ref[0] = _adam_step(w_ref[0], g, m_ref[0], v_ref[0])

    spec = pl.BlockSpec((1, tr, c), lambda l, i: (l, i, 0))
    shape = jax.ShapeDtypeStruct((depth, r, c), F32)
    return pl.pallas_call(
        body, name=name, grid=(depth, r // tr),
        in_specs=[pl.BlockSpec((N_DEV, tr, c), lambda l, i: (0, i, 0))] * depth + [spec] * 3,
        out_specs=[spec] * 4, out_shape=[shape] * 4, compiler_params=_cparams(2),
    )(*parts, w, m, v)


def _adamw(w, g, m, v, name):
    r, c = w.shape
    tr = _row_tile(r) if r % 8 == 0 else r

    def body(w_ref, g_ref, m_ref, v_ref, d_ref, nm_ref, nv_ref):
        d_ref[...], nm_ref[...], nv_ref[...] = _adam_step(w_ref[...], g_ref[...], m_ref[...], v_ref[...])

    spec = pl.BlockSpec((tr, c), lambda i: (i, 0))
    shape = jax.ShapeDtypeStruct((r, c), F32)
    return pl.pallas_call(
        body, name=name, grid=(r // tr,), in_specs=[spec] * 4, out_specs=[spec] * 3, out_shape=[shape] * 3,
        compiler_params=_cparams(1),
    )(w, g, m, v)


LAYER_ROWS = 80


def _pack_layer(dmod, gains, b_in, pool_scale, sinks, pool_w):
    misc = jnp.concatenate([pool_scale, sinks, jnp.zeros((D_MODEL - POOL_WIDTH - N_HEADS,), F32)])
    parts = [dmod.reshape(N_MOD, D_MODEL), gains, jnp.pad(b_in, (0, 2 * D_MODEL - IN_WIDTH)).reshape(2, D_MODEL),
             misc.reshape(1, D_MODEL), pool_w.reshape(-1, D_MODEL)]
    packed = jnp.concatenate(parts, axis=0)
    return jnp.pad(packed, ((0, LAYER_ROWS - packed.shape[0]), (0, 0)))


def _unpack_layer(p):
    r = N_MOD + 4
    b_in = p[r:r + 2].reshape(2 * D_MODEL)[:IN_WIDTH]
    pool_w = p[r + 3:r + 3 + 64].reshape(len(POOL_WINDOWS), POOL_GROUP_WIDTH, POOL_GROUP_WIDTH)
    return (p[0:N_MOD].reshape(N_MOD * D_MODEL), p[N_MOD:r], b_in, p[r + 2, :POOL_WIDTH],
            p[r + 2, POOL_WIDTH:POOL_WIDTH + N_HEADS], pool_w)


def _rotary_tables(positions):
    inv_freq = ROPE_THETA ** (-jnp.arange(0, ROT_DIM, 2, dtype=F32) / ROT_DIM)
    half = ROT_DIM // 2
    head_freq = jnp.concatenate([inv_freq, inv_freq, jnp.zeros((HEAD_DIM - ROT_DIM,), F32)])
    lane_freq = jnp.concatenate([head_freq, head_freq])
    ang = positions.reshape(-1).astype(F32)[:, None] * lane_freq[None, :]
    cos, sin = jnp.cos(ang), jnp.sin(ang)
    in_head = lax.broadcasted_iota(jnp.int32, (1, LANES), 1) % HEAD_DIM
    return cos, jnp.where(in_head < half, -sin, 0.0), jnp.where((in_head >= half) & (in_head < ROT_DIM), sin, 0.0)


def kernel(x, c, positions, ada_w, ada_b, w_in, b_in, sinks, pool_w, pool_scale, w_out, w_gate, w_up, w_down, g_pre_mix, g_post_mix, g_pre_ffn, g_post_ffn, loss_target, m_ada_w, m_ada_b, m_w_in, m_b_in, m_sinks, m_pool_w, m_pool_scale, m_w_out, m_w_gate, m_w_up, m_w_down, m_g_pre_mix, m_g_post_mix, m_g_pre_ffn, m_g_post_ffn, v_ada_w, v_ada_b, v_w_in, v_b_in, v_sinks, v_pool_w, v_pool_scale, v_w_out, v_w_gate, v_w_up, v_w_down, v_g_pre_mix, v_g_post_mix, v_g_pre_ffn, v_g_post_ffn):
    me = _index(_mesh_place())
    x0 = x[0]
    target = loss_target[0]
    rot = _rotary_tables(positions)
    ada_cols = ada_w.shape[2]

    shards = []
    for l in range(DEPTH):
        shards += [w_in[l].T.astype(BF16), w_out[l].astype(BF16), w_gate[l].T.astype(BF16), w_up[l].T.astype(BF16),
                   w_down[l].astype(BF16)]
    whole = lambda gathered: [g.reshape(-1, D_MODEL) for g in gathered]
    full = [None] * (5 * DEPTH)
    first = _exchanges_alone([_Gather(shards[0:2] + [jnp.broadcast_to(c, (8, D_MODEL))])], "gather_first")[0]
    full[0:2] = whole(first[0:2])

    mod_part, c_act = _mod_fwd(first[2][:, 0, :], ada_w)
    mod_all = _allgather_vmem(mod_part.reshape(DEPTH * N_DEV, ada_cols), "gather_mod")
    mod_mine = lax.dynamic_slice_in_dim(mod_all.reshape(N_DEV, DEPTH, N_DEV, ada_cols), me, 1, axis=2)[:, :, 0, :]
    mod = jnp.transpose(mod_mine, (1, 0, 2)).reshape(DEPTH, N_MOD * D_MODEL) + ada_b
    mod = mod.reshape(DEPTH, N_MOD, D_MODEL)

    hosted = {("fwd_in", 0): (_Gather(shards[2:3]), [2]), ("attn_fwd", 0): (_Gather(shards[3:4]), [3]),
              ("fwd_out", 0): (_Gather(shards[4:5]), [4]),
              ("ffn_fwd", 0): (_Gather(shards[5:10]), [5, 6, 7, 8, 9])}

    def take(kind, l, ex_outs):
        if (kind, l) in hosted:
            for slot, g in zip(hosted[kind, l][1], whole(ex_outs[0])):
                full[slot] = g

    def beside(kind, l):
        return [hosted[kind, l][0]] if (kind, l) in hosted else []

    pool_w_b = pool_w.astype(BF16)

    saved = []
    xl = x0
    for l in range(DEPTH):
        vec_in = _rows(g_pre_mix[l], 1.0 + mod[l, 1], mod[l, 0], width=D_MODEL)
        vec_out = _rows(mod[l, 2], g_post_mix[l], width=D_MODEL)
        vec_ffn = _rows(g_pre_ffn[l], 1.0 + mod[l, 4], mod[l, 3], mod[l, 5], g_post_ffn[l], width=D_MODEL)
        psc = _rows(pool_scale[l], width=POOL_WIDTH)
        (q, k, v, u), ex = _fwd_in(xl, vec_in, full[5 * l], _rows(b_in[l], width=IN_WIDTH), rot, f"fwd_in_{l}",
                                   beside("fwd_in", l))
        take("fwd_in", l, ex)
        (ao, lse), ex = _attn_fwd_t(sinks[l], q, k, v, f"attn_fwd_{l}", beside("attn_fwd", l))
        take("attn_fwd", l, ex)
        (x1, mix), ex = _fwd_out(ao, u, pool_w_b[l], psc, full[5 * l + 1], xl, vec_out, f"fwd_out_{l}",
                                 beside("fwd_out", l))
        take("fwd_out", l, ex)
        (x2, f, gt, up, h2, *loss_part), ex = _ffn_fwd(
            x1, vec_ffn, full[5 * l + 2], full[5 * l + 3], full[5 * l + 4], f"ffn_fwd_{l}", beside("ffn_fwd", l),
            target=target if l == DEPTH - 1 else None)
        take("ffn_fwd", l, ex)
        saved.append((xl, q, k, v, u, ao, lse, x1, mix, f, gt, up, h2, vec_in, vec_out, vec_ffn, psc))
        xl = x2

    dx = xl
    loss = lax.psum(loss_part[0][0, 0], ("x", "y", "c"))

    stacks = [None] * (5 * DEPTH)
    parts = [None] * (5 * DEPTH)
    small_all = [None] * DEPTH
    ready_stacks, ready_small = [], []

    def leaving(row_budget):
        exs, notes = [], []
        taken = []
        while ready_stacks and stacks[ready_stacks[0]].shape[1] <= row_budget:
            row_budget -= stacks[ready_stacks[0]].shape[1]
            taken.append(ready_stacks.pop(0))
        if taken:
            exs.append(_Scatter([stacks[a] for a in taken]))
            notes.append(("stacks", taken))
        if ready_small and row_budget >= 0:
            exs.append(_Gather([ready_small[0][1]]))
            notes.append(("small", ready_small[0][0]))
            ready_small.clear()
        return exs, notes

    def arrived(notes, ex_outs):
        for (kind, what), outs in zip(notes, ex_outs):
            if kind == "stacks":
                for a, o in zip(what, outs):
                    parts[a] = o
            else:
                small_all[what] = outs[0]

    def grad_ready(a, grad):
        stacks[a] = grad.reshape(N_DEV, -1, D_MODEL)
        ready_stacks.append(a)

    def hosting(row_budget, kernel_fn, *args):
        exs, notes = leaving(row_budget)
        outs, ex_outs = kernel_fn(*args, exs)
        arrived(notes, ex_outs)
        return outs

    for l in reversed(range(DEPTH)):
        w_in_t, w_out_f, w_gate_t, w_up_t, w_down_f = full[5 * l:5 * l + 5]
        xin, q, k, v, u, ao, lse, x1, mix, f, gt, up, h2, vec_in, vec_out, vec_ffn, psc = saved[l]
        dgt, dup, d_w_down, cs_a = hosting(600, _ffn_bwd_act, dx, f, gt, up, vec_ffn, w_down_f, f"ffn_bwd_act_{l}")
        grad_ready(5 * l + 4, d_w_down.astype(BF16))
        grad_ready(5 * l + 2, hosting(-1, _weight_grad, dgt, h2, f"grad_w_gate_{l}"))
        grad_ready(5 * l + 3, hosting(-1, _weight_grad, dup, h2, f"grad_w_up_{l}"))
        dx1, cs_f = hosting(720, _ffn_bwd_in, dx, x1, dgt, dup, vec_ffn, w_gate_t, w_up_t, f"ffn_bwd_in_{l}")
        do, du, d_w_out, cs_o, dpw, dpsc = hosting(360, _out_bwd, dx1, mix, ao, u, pool_w_b[l], psc, w_out_f, vec_out,
                                                   f"out_bwd_{l}")
        grad_ready(5 * l + 1, d_w_out)
        dq, dkc, dkp, dvc, dvp, dsk = hosting(200, _attn_bwd_t, sinks[l], q, do, lse, k, v, f"attn_bwd_{l}")
        dx, d_w_in_t, cs_i, db = hosting(-1, _in_bwd, dq, dkc, dkp, dvc, dvp, du, xin, dx1, vec_in, w_in_t, rot,
                                         f"in_bwd_{l}")
        grad_ready(5 * l, d_w_in_t)
        d_mod = jnp.concatenate([cs_i[0], cs_i[1], cs_o[0], cs_f[0], cs_f[1], cs_a[0]])
        d_gain = jnp.stack([cs_i[2], cs_o[1], cs_f[2], cs_a[1]])
        ready_small.append((l, _pack_layer(d_mod, d_gain, db[0], dpsc[0], dsk[:, 0], dpw)))
    grad_x = dx[None]
    exs, notes = leaving(N_DEV * D_FF)
    arrived(notes, _exchanges_alone(exs, "exchange_last"))

    layer_sums = [_unpack_layer(_sum_parts(small_all[l], 0, f"sum_small_{l}")) for l in range(DEPTH)]
    g_ada_b, g_gains, g_b_in, g_pool_scale, g_sinks, g_pool_w = (
        jnp.stack([layer_sums[l][j] for l in range(DEPTH)], axis=1 if j == 1 else 0) for j in range(6))

    dmod_all = jnp.stack([small_all[l][:, 0:N_MOD, :].reshape(N_DEV, N_DEV, ada_cols) for l in range(DEPTH)])
    dmod_cols = lax.dynamic_slice_in_dim(dmod_all, me, 1, axis=2)[:, :, 0, :]
    g_ada_w = _ada_grad(c_act, dmod_cols)

    slot = {"w_in": 0, "w_out": 1, "w_gate": 2, "w_up": 3, "w_down": 4}
    held_transposed = ("w_in", "w_gate", "w_up")
    swap = lambda a: jnp.transpose(a, (0, 2, 1))

    def step(w, g, m, v, name):
        if name in slot:
            mine = [parts[5 * l + slot[name]] for l in range(DEPTH)]
            if name in held_transposed:
                return [swap(o) for o in _sum_adamw(mine, swap(w), swap(m), swap(v), f"adamw_{name}")]
            return _sum_adamw(mine, w, m, v, f"adamw_{name}")
        flat = lambda a: a.reshape(-1, a.shape[-1])
        return [g] + [o.reshape(w.shape) for o in _adamw(flat(w), flat(g), flat(m), flat(v), f"adamw_{name}")]

    in_order = [
        ("ada_w", ada_w, g_ada_w, m_ada_w, v_ada_w), ("ada_b", ada_b, g_ada_b, m_ada_b, v_ada_b),
        ("w_in", w_in, None, m_w_in, v_w_in), ("b_in", b_in, g_b_in, m_b_in, v_b_in),
        ("sinks", sinks, g_sinks, m_sinks, v_sinks), ("pool_w", pool_w, g_pool_w, m_pool_w, v_pool_w),
        ("pool_scale", pool_scale, g_pool_scale, m_pool_scale, v_pool_scale),
        ("w_out", w_out, None, m_w_out, v_w_out), ("w_gate", w_gate, None, m_w_gate, v_w_gate),
        ("w_up", w_up, None, m_w_up, v_w_up), ("w_down", w_down, None, m_w_down, v_w_down),
        ("g_pre_mix", g_pre_mix, g_gains[0], m_g_pre_mix, v_g_pre_mix),
        ("g_post_mix", g_post_mix, g_gains[1], m_g_post_mix, v_g_post_mix),
        ("g_pre_ffn", g_pre_ffn, g_gains[2], m_g_pre_ffn, v_g_pre_ffn),
        ("g_post_ffn", g_post_ffn, g_gains[3], m_g_post_ffn, v_g_post_ffn)]
    steps = [step(w, g, m, v, n) for n, w, g, m, v in in_order]
    grads, deltas, new_m, new_v = ([s[j] for s in steps] for j in range(4))
    return (loss, grad_x, *grads, *deltas, *new_m, *new_v)
```

```python
import jax
import jax.numpy as jnp
from jax import lax
from jax.experimental import pallas as pl
from jax.experimental.pallas import tpu as pltpu

F32 = jnp.float32
BF16 = jnp.bfloat16

N_DEV = 8
DEPTH = 2
D_MODEL = 1024
HEAD_DIM = 64
N_HEADS = 8
N_KV_HEADS = 2
GROUP = N_HEADS // N_KV_HEADS
ATTN_WIDTH = N_HEADS * HEAD_DIM
KV_WIDTH = N_KV_HEADS * HEAD_DIM
POOL_WIDTH = 512
POOL_WINDOWS = (2, 4, 8, 16)
POOL_GROUP_WIDTH = 128
POOL_HALO = 16
IN_WIDTH = ATTN_WIDTH + 2 * KV_WIDTH + POOL_WIDTH
D_FF = 2816
N_MOD = 6
BLOCK = 128
ROT_DIM = 16
ROPE_THETA = 500000.0
EPS = 1e-6
NEG_INF = -1e30
Q_SCALE = HEAD_DIM ** -0.5

ADAM_LR = 0.001
ADAM_B1 = 0.9
ADAM_B2 = 0.999
ADAM_EPS = 1e-08
ADAM_WD = 0.01
ADAM_STEP = 10

LANES = 128
SEQ_TILE = 512
ATTN_BLOCKS = 4
FF_CHUNK = 256
VMEM_LIMIT = 56 * 1024 * 1024
MESH = pl.DeviceIdType.MESH

NT = (((1,), (1,)), ((), ()))
TN = (((0,), (0,)), ((), ()))


def _dot(a, b, dims=None):
    if dims is None:
        return jnp.dot(a, b, preferred_element_type=F32)
    return lax.dot_general(a, b, dims, preferred_element_type=F32)


def _cparams(n_axes):
    return pltpu.CompilerParams(dimension_semantics=("arbitrary",) * n_axes, vmem_limit_bytes=VMEM_LIMIT)


def _resident(shape):
    zeros = (0,) * len(shape)
    return pl.BlockSpec(shape, lambda *_: zeros, pipeline_mode=pl.Buffered(1))


def _rows(*vectors, width):
    rows = [jnp.reshape(v, (1, width)).astype(F32) for v in vectors]
    rows.append(jnp.zeros((8 - len(rows), width), F32))
    return jnp.concatenate(rows, axis=0)


def _rsqrt_ms(x):
    return lax.rsqrt(jnp.mean(x * x, axis=-1, keepdims=True) + EPS)


def _colsum(x):
    return jnp.sum(x, axis=0, keepdims=True)


def _seq_tile(s, tiles=1):
    return min(s, tiles * SEQ_TILE)


def _mesh_place():
    x, y, c = lax.axis_index("x"), lax.axis_index("y"), lax.axis_index("c")
    return x, y, c


def _flip(place, k):
    x, y, c = place
    return (1 - x if k & 4 else x, 1 - y if k & 2 else y, 1 - c if k & 1 else c)


def _index(place):
    x, y, c = place
    return 4 * x + 2 * y + c


def _allgather_vmem(block, name):
    r, c = block.shape

    def body(x_ref, out_ref, send_sems, recv_sems, local_sem):
        me = _mesh_place()
        mine = pltpu.make_async_copy(x_ref, out_ref.at[_index(me)], local_sem)
        mine.start()

        def copy(k):
            return pltpu.make_async_remote_copy(
                src_ref=x_ref, dst_ref=out_ref.at[_index(me)], send_sem=send_sems.at[k - 1], recv_sem=recv_sems.at[k - 1],
                device_id=_flip(me, k), device_id_type=MESH)

        def arrival(k):
            return pltpu.make_async_remote_copy(
                src_ref=x_ref, dst_ref=out_ref.at[_index(_flip(me, k))], send_sem=send_sems.at[k - 1],
                recv_sem=recv_sems.at[k - 1], device_id=_flip(me, k), device_id_type=MESH)

        for k in range(1, N_DEV):
            copy(k).start()
        for k in range(1, N_DEV):
            arrival(k).wait_recv()
        for k in range(1, N_DEV):
            copy(k).wait_send()
        mine.wait()

    return pl.pallas_call(
        body, name=name,
        out_shape=jax.ShapeDtypeStruct((N_DEV, r, c), block.dtype),
        in_specs=[pl.BlockSpec(memory_space=pltpu.VMEM)],
        out_specs=pl.BlockSpec(memory_space=pltpu.VMEM),
        scratch_shapes=[pltpu.SemaphoreType.DMA((N_DEV - 1,)), pltpu.SemaphoreType.DMA((N_DEV - 1,)),
                        pltpu.SemaphoreType.DMA],
    )(block)


class _Gather:
    def __init__(self, shards):
        n = len(shards)
        self.operands = list(shards)
        self.out_shape = [jax.ShapeDtypeStruct((N_DEV,) + s.shape, s.dtype) for s in shards]
        self.scratch = [pltpu.SemaphoreType.DMA((N_DEV - 1, n)), pltpu.SemaphoreType.DMA((N_DEV - 1, n)),
                        pltpu.SemaphoreType.DMA((n,))]

    def _copies(self, x_refs, out_refs, sems):
        send_sems, recv_sems, local_sems = sems
        n = len(x_refs)
        x, y, c = _mesh_place()
        me, sibling = (x, y, c), (x, y, 1 - c)
        chips = [(1 - x, y), (x, 1 - y), (1 - x, 1 - y)]

        def copy(k, a, block, to, from_input=False):
            rows = out_refs[a].at[_index(block)]
            return pltpu.make_async_remote_copy(
                src_ref=x_refs[a] if from_input else rows, dst_ref=rows,
                send_sem=send_sems.at[k, a], recv_sem=recv_sems.at[k, a], device_id=to, device_id_type=MESH)

        mine = [pltpu.make_async_copy(x_refs[a], out_refs[a].at[_index(me)], local_sems.at[a]) for a in range(n)]
        first = [copy(0, a, me, sibling, from_input=True) for a in range(n)]
        first += [copy(1 + j, a, me, (*chip, c), from_input=True) for j, chip in enumerate(chips) for a in range(n)]
        over_ici = [copy(1 + j, a, (*chip, c), me) for j, chip in enumerate(chips) for a in range(n)]
        passed = [copy(4 + j, a, (*chip, c), sibling) for j, chip in enumerate(chips) for a in range(n)]
        from_sibling = [copy(0, a, sibling, me) for a in range(n)]
        from_sibling += [copy(4 + j, a, (*chip, 1 - c), me) for j, chip in enumerate(chips) for a in range(n)]
        return mine, first, over_ici, passed, from_sibling

    def begin(self, x_refs, out_refs, sems):
        mine, first, _, _, _ = self._copies(x_refs, out_refs, sems)
        for cp in mine + first:
            cp.start()

    def middle(self, x_refs, out_refs, sems):
        _, _, over_ici, passed, _ = self._copies(x_refs, out_refs, sems)
        for arrived, onward in zip(over_ici, passed):
            arrived.wait_recv()
            onward.start()

    def end(self, x_refs, out_refs, sems):
        mine, first, _, passed, from_sibling = self._copies(x_refs, out_refs, sems)
        for cp in from_sibling:
            cp.wait_recv()
        for cp in first + passed:
            cp.wait_send()
        for cp in mine:
            cp.wait()


class _Scatter:
    OWN = N_DEV - 1

    def __init__(self, stacks):
        n = len(stacks)
        self.operands = list(stacks)
        self.out_shape = [jax.ShapeDtypeStruct(s.shape, s.dtype) for s in stacks]
        self.scratch = [pltpu.SemaphoreType.DMA((N_DEV - 1, n)), pltpu.SemaphoreType.DMA((N_DEV - 1, n)),
                        pltpu.SemaphoreType.DMA((n,))]

    def _copies(self, g_refs, out_refs, sems):
        send_sems, recv_sems, local_sems = sems
        me = _mesh_place()

        def copy(k, a):
            peer = _flip(me, k)
            return pltpu.make_async_remote_copy(
                src_ref=g_refs[a].at[_index(peer)], dst_ref=out_refs[a].at[k - 1],
                send_sem=send_sems.at[k - 1, a], recv_sem=recv_sems.at[k - 1, a], device_id=peer, device_id_type=MESH)

        mine = [pltpu.make_async_copy(g_refs[a].at[_index(me)], out_refs[a].at[self.OWN], local_sems.at[a])
                for a in range(len(g_refs))]
        return mine, [copy(k, a) for k in range(1, N_DEV) for a in range(len(g_refs))]

    def begin(self, g_refs, out_refs, sems):
        mine, copies = self._copies(g_refs, out_refs, sems)
        for cp in mine + copies:
            cp.start()

    def middle(self, g_refs, out_refs, sems):
        pass

    def end(self, g_refs, out_refs, sems):
        mine, copies = self._copies(g_refs, out_refs, sems)
        for cp in copies:
            cp.wait_recv()
        for cp in copies:
            cp.wait_send()
        for cp in mine:
            cp.wait()


def _call(body, args, *, name, grid, in_specs, out_specs, out_shape, scratch_shapes=(), exchanges=()):
    if not exchanges:
        outs = pl.pallas_call(body, name=name, grid=grid, in_specs=in_specs, out_specs=out_specs, out_shape=out_shape,
                              scratch_shapes=list(scratch_shapes), compiler_params=_cparams(len(grid)))(*args)
        return outs, []
    (steps,) = grid
    n_in, n_out, n_scr = len(in_specs), len(out_specs), len(scratch_shapes)
    ex_in = [len(ex.operands) for ex in exchanges]
    ex_out = [len(ex.out_shape) for ex in exchanges]
    ex_scr = [len(ex.scratch) for ex in exchanges]

    def split(refs, counts):
        parts, pos = [], 0
        for cnt in counts:
            parts.append(refs[pos:pos + cnt])
            pos += cnt
        return parts

    def wrapped(*refs):
        ins, xin, outs, xout, scr, xscr = split(refs, [n_in, sum(ex_in), n_out, sum(ex_out), n_scr, sum(ex_scr)])
        bound = list(zip(exchanges, split(xin, ex_in), split(xout, ex_out), split(xscr, ex_scr)))
        step = pl.program_id(0)

        def phase(method, at):
            @pl.when(step == at)
            def _():
                for ex, i_refs, o_refs, sems in bound:
                    getattr(ex, method)(i_refs, o_refs, sems)

        phase("begin", 0)
        phase("middle", (3 * steps) // 4)
        body(*ins, *outs, *scr)
        phase("end", steps - 1)

    any_spec = pl.BlockSpec(memory_space=pl.ANY)
    results = pl.pallas_call(
        wrapped, name=name, grid=grid,
        in_specs=list(in_specs) + [any_spec] * sum(ex_in),
        out_specs=list(out_specs) + [any_spec] * sum(ex_out),
        out_shape=list(out_shape) + [s for ex in exchanges for s in ex.out_shape],
        scratch_shapes=list(scratch_shapes) + [s for ex in exchanges for s in ex.scratch],
        compiler_params=_cparams(1),
    )(*args, *[a for ex in exchanges for a in ex.operands])
    return results[:n_out], split(results[n_out:], ex_out)


def _exchanges_alone(exchanges, name):
    def body(flag_ref):
        flag_ref[...] = jnp.zeros_like(flag_ref)

    _, ex_outs = _call(body, [], name=name, grid=(1,), in_specs=[], out_specs=[pl.BlockSpec((8, LANES), lambda i: (0, 0))],
                       out_shape=[jax.ShapeDtypeStruct((8, LANES), F32)], exchanges=exchanges)
    return ex_outs


def _mod_fwd(c_all, ada_w):
    cols = ada_w.shape[2]

    def body(c_ref, w_ref, mp_ref, act_ref):
        c = c_ref[...]
        act = c * jax.nn.sigmoid(c)
        act_ref[...] = act
        mp_ref[0] = _dot(act.astype(BF16), w_ref[0].astype(BF16))

    return pl.pallas_call(
        body, name="mod_fwd", grid=(DEPTH,),
        in_specs=[pl.BlockSpec((N_DEV, D_MODEL), lambda l: (0, 0)),
                  pl.BlockSpec((1, D_MODEL, cols), lambda l: (l, 0, 0))],
        out_specs=[pl.BlockSpec((1, N_DEV, cols), lambda l: (l, 0, 0)),
                   pl.BlockSpec((N_DEV, D_MODEL), lambda l: (0, 0))],
        out_shape=[jax.ShapeDtypeStruct((DEPTH, N_DEV, cols), F32), jax.ShapeDtypeStruct((N_DEV, D_MODEL), F32)],
        compiler_params=_cparams(1),
    )(c_all, ada_w)


def _ada_grad(c_act, dmod_cols):
    cols = dmod_cols.shape[2]

    def body(act_ref, dm_ref, g_ref):
        g_ref[0] = _dot(act_ref[...].astype(BF16), dm_ref[0].astype(BF16), TN)

    return pl.pallas_call(
        body, name="ada_grad", grid=(DEPTH,),
        in_specs=[pl.BlockSpec((N_DEV, D_MODEL), lambda l: (0, 0)),
                  pl.BlockSpec((1, N_DEV, cols), lambda l: (l, 0, 0))],
        out_specs=pl.BlockSpec((1, D_MODEL, cols), lambda l: (l, 0, 0)),
        out_shape=jax.ShapeDtypeStruct((DEPTH, D_MODEL, cols), F32),
        compiler_params=_cparams(1),
    )(c_act, dmod_cols)


def _rotate(t, cos, sin_lo, sin_hi):
    return t * cos + pltpu.roll(t, LANES - 8, 1) * sin_lo + pltpu.roll(t, 8, 1) * sin_hi


def _rotate_bwd(t, cos, sin_lo, sin_hi):
    return t * cos + pltpu.roll(t * sin_lo, 8, 1) + pltpu.roll(t * sin_hi, LANES - 8, 1)


def _fwd_in(x, vec, w_in_t, b_in, rot, name, exchanges=()):
    s = x.shape[0]
    ts = _seq_tile(s, 2)

    def body(x_ref, vec_ref, w_ref, b_ref, cos_ref, lo_ref, hi_ref, q_ref, k_ref, v_ref, u_ref):
        xt = x_ref[...]
        h = xt * _rsqrt_ms(xt) * vec_ref[0:1, :] * vec_ref[1:2, :] + vec_ref[2:3, :]
        proj = _dot(h.astype(BF16), w_ref[...], NT) + b_ref[0:1, :]
        cos, lo, hi = cos_ref[...], lo_ref[...], hi_ref[...]
        for j in range(ATTN_WIDTH // LANES):
            q_ref[:, j * LANES:(j + 1) * LANES] = (
                _rotate(proj[:, j * LANES:(j + 1) * LANES], cos, lo, hi) * Q_SCALE).astype(BF16)
        k_ref[...] = _rotate(proj[:, ATTN_WIDTH:ATTN_WIDTH + KV_WIDTH], cos, lo, hi).astype(BF16)
        v_ref[...] = proj[:, ATTN_WIDTH + KV_WIDTH:ATTN_WIDTH + 2 * KV_WIDTH].astype(BF16)
        u_ref[...] = proj[:, ATTN_WIDTH + 2 * KV_WIDTH:]

    row = lambda w: pl.BlockSpec((ts, w), lambda i: (i, 0))
    return _call(
        body, [x, vec, w_in_t, b_in, *rot], name=name, grid=(s // ts,),
        in_specs=[row(D_MODEL), _resident((8, D_MODEL)), _resident((IN_WIDTH, D_MODEL)), _resident((8, IN_WIDTH)),
                  row(LANES), row(LANES), row(LANES)],
        out_specs=[row(ATTN_WIDTH), row(KV_WIDTH), row(KV_WIDTH), row(POOL_WIDTH)],
        out_shape=[jax.ShapeDtypeStruct((s, ATTN_WIDTH), BF16), jax.ShapeDtypeStruct((s, KV_WIDTH), BF16),
                   jax.ShapeDtypeStruct((s, KV_WIDTH), BF16), jax.ShapeDtypeStruct((s, POOL_WIDTH), F32)],
        exchanges=exchanges)


def _band_mask_t(first_block):
    key = lax.broadcasted_iota(jnp.int32, (2 * BLOCK, 2 * BLOCK), 0)
    qry = lax.broadcasted_iota(jnp.int32, (2 * BLOCK, 2 * BLOCK), 1) & (BLOCK - 1)
    d = key - qry
    return (d >= 1) & (d <= BLOCK) & ((key >= BLOCK) | jnp.logical_not(first_block))


def _placed(both, kh, b):
    lane = lax.broadcasted_iota(jnp.int32, (1, LANES), 1)
    src = both if kh == b else pltpu.roll(both, HEAD_DIM, 1)
    return jnp.where((lane >= b * HEAD_DIM) & (lane < (b + 1) * HEAD_DIM), src, jnp.zeros_like(src))


def _pair(ref, rows, kh, r):
    j = 2 * kh + r
    return ref[rows, j * LANES:(j + 1) * LANES]


def _attn_specs(s):
    qb = min(ATTN_BLOCKS, s // BLOCK)
    cur = lambda w: pl.BlockSpec((qb * BLOCK, w), lambda i: (i, 0))
    prev = lambda w: pl.BlockSpec((BLOCK, w), lambda i: (jnp.maximum(i * qb - 1, 0), 0))
    return qb, cur, prev, pl.BlockSpec((N_HEADS, qb * BLOCK), lambda i: (0, i))


def _kv_window(prev_ref, cur_ref, jb):
    before = prev_ref[...] if jb == 0 else cur_ref[(jb - 1) * BLOCK:jb * BLOCK, :]
    return jnp.concatenate([before, cur_ref[jb * BLOCK:(jb + 1) * BLOCK, :]], axis=0)


def _sink_row(sink_ref, kh, b):
    lane = lax.broadcasted_iota(jnp.int32, (1, 2 * BLOCK), 1)
    return jnp.where(lane < BLOCK, sink_ref[GROUP * kh + b], sink_ref[GROUP * kh + 2 + b])


def _attn_fwd_t(sinks, q, k, v, name, exchanges=()):
    s = q.shape[0]
    qb, cur, prev, lse_spec = _attn_specs(s)

    def block(jb, sink_ref, q_ref, kc_ref, kp_ref, vc_ref, vp_ref, o_ref, l_ref):
        rows = slice(jb * BLOCK, (jb + 1) * BLOCK)
        valid = _band_mask_t(pl.program_id(0) == 0 if jb == 0 else False)
        k_both, v_both = _kv_window(kp_ref, kc_ref, jb), _kv_window(vp_ref, vc_ref, jb)
        for kh in range(N_KV_HEADS):
            qg = jnp.concatenate([_pair(q_ref, rows, kh, 0), _pair(q_ref, rows, kh, 1)], axis=0)
            keys = jnp.concatenate([_placed(k_both, kh, 0), _placed(k_both, kh, 1)], axis=0)
            st = _dot(keys, qg, NT)
            out = jnp.zeros((2 * BLOCK, LANES), F32)
            for b in range(2):
                sc = jnp.where(valid, st[b * 2 * BLOCK:(b + 1) * 2 * BLOCK, :], NEG_INF)
                sink = _sink_row(sink_ref, kh, b)
                m = jnp.maximum(jnp.max(sc, axis=0, keepdims=True), sink)
                p = jnp.exp(sc - m)
                den = jnp.sum(p, axis=0, keepdims=True) + jnp.exp(sink - m)
                out = out + _dot((p * (1.0 / den)).astype(BF16), _placed(v_both, kh, b), TN)
                lse = m + jnp.log(den)
                for r in range(2):
                    h = GROUP * kh + 2 * r + b
                    l_ref[h:h + 1, rows] = lse[:, r * BLOCK:(r + 1) * BLOCK]
            for r in range(2):
                j = 2 * kh + r
                o_ref[rows, j * LANES:(j + 1) * LANES] = out[r * BLOCK:(r + 1) * BLOCK, :].astype(BF16)

    def body(*refs):
        for jb in range(qb):
            block(jb, *refs)

    return _call(
        body, [sinks, q, k, k, v, v], name=name, grid=(s // (qb * BLOCK),),
        in_specs=[pl.BlockSpec(memory_space=pltpu.SMEM), cur(ATTN_WIDTH), cur(KV_WIDTH), prev(KV_WIDTH),
                  cur(KV_WIDTH), prev(KV_WIDTH)],
        out_specs=[cur(ATTN_WIDTH), lse_spec],
        out_shape=[jax.ShapeDtypeStruct((s, ATTN_WIDTH), BF16), jax.ShapeDtypeStruct((N_HEADS, s), F32)],
        exchanges=exchanges)


def _attn_bwd_t(sinks, q, do, lse, k, v, name, exchanges=()):
    s = q.shape[0]
    qb, cur, prev, lse_spec = _attn_specs(s)

    def block(jb, sink_ref, q_ref, do_ref, l_ref, kc_ref, kp_ref, vc_ref, vp_ref,
              dq_ref, dkc_ref, dkp_ref, dvc_ref, dvp_ref, dsink_ref):
        rows = slice(jb * BLOCK, (jb + 1) * BLOCK)
        valid = _band_mask_t(pl.program_id(0) == 0 if jb == 0 else False)
        k_both, v_both = _kv_window(kp_ref, kc_ref, jb), _kv_window(vp_ref, vc_ref, jb)
        lane = lax.broadcasted_iota(jnp.int32, (1, LANES), 1)
        col = lax.broadcasted_iota(jnp.int32, (1, 2 * BLOCK), 1)
        dk_heads, dv_heads = [], []
        for kh in range(N_KV_HEADS):
            qg = jnp.concatenate([_pair(q_ref, rows, kh, 0), _pair(q_ref, rows, kh, 1)], axis=0)
            dog = jnp.concatenate([_pair(do_ref, rows, kh, 0), _pair(do_ref, rows, kh, 1)], axis=0)
            k_placed = [_placed(k_both, kh, b) for b in range(2)]
            v_placed = [_placed(v_both, kh, b) for b in range(2)]
            st = _dot(jnp.concatenate(k_placed, axis=0), qg, NT)
            dpt = _dot(jnp.concatenate(v_placed, axis=0), dog, NT)
            dqg = jnp.zeros((2 * BLOCK, LANES), F32)
            dk_b, dv_b = [], []
            for b in range(2):
                part = slice(b * 2 * BLOCK, (b + 1) * 2 * BLOCK)
                heads = [GROUP * kh + 2 * r + b for r in range(2)]
                lse_row = jnp.concatenate([l_ref[h:h + 1, rows] for h in heads], axis=1)
                p = jnp.where(valid, jnp.exp(st[part, :] - lse_row), 0.0)
                dp = dpt[part, :]
                delta = jnp.sum(p * dp, axis=0, keepdims=True)
                sink_pull = jnp.exp(_sink_row(sink_ref, kh, b) - lse_row) * delta
                dsink_ref[heads[0]:heads[0] + 1, :] += -jnp.sum(jnp.where(col < BLOCK, sink_pull, 0.0))
                dsink_ref[heads[1]:heads[1] + 1, :] += -jnp.sum(jnp.where(col < BLOCK, 0.0, sink_pull))
                ds = (p * (dp - delta)).astype(BF16)
                dqg = dqg + _dot(ds, k_placed[b], TN)
                dk_b.append(_dot(ds, qg))
                dv_b.append(_dot(p.astype(BF16), dog))
            for parts, total in ((dk_b, dk_heads), (dv_b, dv_heads)):
                kept = jnp.where(lane < HEAD_DIM, parts[0], parts[1])
                total.append(kept + pltpu.roll(kept, HEAD_DIM, 1))
            for r in range(2):
                j = 2 * kh + r
                dq_ref[rows, j * LANES:(j + 1) * LANES] = dqg[r * BLOCK:(r + 1) * BLOCK, :] * Q_SCALE
        dk = jnp.where(lane < HEAD_DIM, dk_heads[0], dk_heads[1])
        dv = jnp.where(lane < HEAD_DIM, dv_heads[0], dv_heads[1])
        dkp_ref[rows, :] = dk[0:BLOCK, :]
        dkc_ref[rows, :] = dk[BLOCK:, :]
        dvp_ref[rows, :] = dv[0:BLOCK, :]
        dvc_ref[rows, :] = dv[BLOCK:, :]

    def body(*refs):
        @pl.when(pl.program_id(0) == 0)
        def _():
            refs[-1][...] = jnp.zeros_like(refs[-1])

        for jb in range(qb):
            block(jb, *refs)

    kv = jax.ShapeDtypeStruct((s, KV_WIDTH), F32)
    return _call(
        body, [sinks, q, do, lse, k, k, v, v], name=name, grid=(s // (qb * BLOCK),),
        in_specs=[pl.BlockSpec(memory_space=pltpu.SMEM), cur(ATTN_WIDTH), cur(ATTN_WIDTH), lse_spec,
                  cur(KV_WIDTH), prev(KV_WIDTH), cur(KV_WIDTH), prev(KV_WIDTH)],
        out_specs=[cur(ATTN_WIDTH), cur(KV_WIDTH), cur(KV_WIDTH), cur(KV_WIDTH), cur(KV_WIDTH),
                   pl.BlockSpec((8, LANES), lambda i: (0, 0))],
        out_shape=[jax.ShapeDtypeStruct((s, ATTN_WIDTH), F32), kv, kv, kv, kv, jax.ShapeDtypeStruct((8, LANES), F32)],
        exchanges=exchanges)


def _pool_counts(tile, ts):
    t = (tile * ts + lax.broadcasted_iota(jnp.int32, (ts, 1), 0) + 1).astype(F32)
    return [jnp.minimum(t, float(w)) for w in POOL_WINDOWS]


def _pooled(u, halo, counts, gi):
    cols = slice(gi * POOL_GROUP_WIDTH, (gi + 1) * POOL_GROUP_WIDTH)
    acc = jnp.concatenate([halo[:, cols], u[:, cols]], axis=0)
    shift = 1
    while shift < POOL_WINDOWS[gi]:
        acc = acc + pltpu.roll(acc, shift, 0)
        shift *= 2
    return acc[POOL_HALO:, :] / counts[gi] - u[:, cols]


def _fwd_out(ao, u, pool_w, psc, w_out, x, vec, name, exchanges=()):
    s = x.shape[0]
    ts = _seq_tile(s, 2)
    hb = ts // POOL_HALO

    def body(ao_ref, u_ref, uh_ref, pw_ref, psc_ref, w_ref, x_ref, vec_ref, x1_ref, mix_ref):
        i = pl.program_id(0)
        u_t = u_ref[...]
        halo = jnp.where(i > 0, uh_ref[...], 0.0)
        counts = _pool_counts(i, ts)
        cat = [ao_ref[...]]
        for gi in range(len(POOL_WINDOWS)):
            cols = slice(gi * POOL_GROUP_WIDTH, (gi + 1) * POOL_GROUP_WIDTH)
            og = _dot(_pooled(u_t, halo, counts, gi).astype(BF16), pw_ref[gi]) * psc_ref[0:1, cols]
            cat.append(og.astype(BF16))
        mix = _dot(jnp.concatenate(cat, axis=1), w_ref[...])
        mix_ref[...] = mix
        x1_ref[...] = x_ref[...] + vec_ref[0:1, :] * (mix * _rsqrt_ms(mix) * vec_ref[1:2, :])

    row = lambda w: pl.BlockSpec((ts, w), lambda i: (i, 0))
    return _call(
        body, [ao, u, u, pool_w, psc, w_out, x, vec], name=name, grid=(s // ts,),
        in_specs=[row(ATTN_WIDTH), row(POOL_WIDTH),
                  pl.BlockSpec((POOL_HALO, POOL_WIDTH), lambda i: (jnp.maximum(i * hb - 1, 0), 0)),
                  _resident(pool_w.shape), _resident((8, POOL_WIDTH)), _resident((D_MODEL, D_MODEL)),
                  row(D_MODEL), _resident((8, D_MODEL))],
        out_specs=[row(D_MODEL), row(D_MODEL)],
        out_shape=[jax.ShapeDtypeStruct((s, D_MODEL), F32), jax.ShapeDtypeStruct((s, D_MODEL), F32)],
        exchanges=exchanges)


def _ffn_fwd(x1, vec, w_gate_t, w_up_t, w_down, name, exchanges=(), target=None):
    s = x1.shape[0]
    ts = _seq_tile(s)

    def body(x_ref, *refs):
        if target is None:
            vec_ref, wg_ref, wu_ref, wd_ref, x2_ref, f_ref, gt_ref, up_ref, h2_ref = refs
        else:
            t_ref, vec_ref, wg_ref, wu_ref, wd_ref, x2_ref, f_ref, gt_ref, up_ref, h2_ref, loss_ref = refs
        xt = x_ref[...]
        h2 = (xt * _rsqrt_ms(xt) * vec_ref[0:1, :] * vec_ref[1:2, :] + vec_ref[2:3, :]).astype(BF16)
        h2_ref[...] = h2
        f = jnp.zeros((ts, D_MODEL), F32)
        for c in range(D_FF // FF_CHUNK):
            cols = slice(c * FF_CHUNK, (c + 1) * FF_CHUNK)
            g = _dot(h2, wg_ref[cols, :], NT)
            up = _dot(h2, wu_ref[cols, :], NT)
            act = ((g * jax.nn.sigmoid(g)) * up).astype(BF16)
            gt_ref[:, cols] = g.astype(BF16)
            up_ref[:, cols] = up.astype(BF16)
            f = f + _dot(act, wd_ref[cols, :])
        f_ref[...] = f
        x2 = xt + vec_ref[3:4, :] * (f * _rsqrt_ms(f) * vec_ref[4:5, :])
        if target is None:
            x2_ref[...] = x2
        else:
            @pl.when(pl.program_id(0) == 0)
            def _():
                loss_ref[...] = jnp.zeros_like(loss_ref)

            diff = x2 - t_ref[...]
            x2_ref[...] = diff / D_MODEL
            loss_ref[...] += 0.5 * jnp.sum(jnp.mean(diff * diff, axis=-1, keepdims=True))

    row = lambda w: pl.BlockSpec((ts, w), lambda i: (i, 0))
    wide = jax.ShapeDtypeStruct((s, D_FF), BF16)
    with_loss = target is not None
    return _call(
        body, [x1] + ([target] if with_loss else []) + [vec, w_gate_t, w_up_t, w_down], name=name, grid=(s // ts,),
        in_specs=[row(D_MODEL)] * (2 if with_loss else 1) + [
            _resident((8, D_MODEL)), _resident((D_FF, D_MODEL)), _resident((D_FF, D_MODEL)), _resident((D_FF, D_MODEL))],
        out_specs=[row(D_MODEL), row(D_MODEL), row(D_FF), row(D_FF), row(D_MODEL)] + (
            [pl.BlockSpec((8, LANES), lambda i: (0, 0))] if with_loss else []),
        out_shape=[jax.ShapeDtypeStruct((s, D_MODEL), F32), jax.ShapeDtypeStruct((s, D_MODEL), F32), wide, wide,
                   jax.ShapeDtypeStruct((s, D_MODEL), BF16)] + (
            [jax.ShapeDtypeStruct((8, LANES), F32)] if with_loss else []),
        exchanges=exchanges)


def _norm_bwd(dy_hat, x_hat, r):
    return r * (dy_hat - x_hat * jnp.mean(dy_hat * x_hat, axis=-1, keepdims=True))


def _accumulate_rows(ref, rows):
    for j, val in enumerate(rows):
        ref[j:j + 1, :] += val


def _ffn_bwd_act(dx2, f, gt, up, vec, w_down, name, exchanges=()):
    s = dx2.shape[0]
    ts = _seq_tile(s)

    def body(dx2_ref, f_ref, gt_ref, up_ref, vec_ref, wd_ref, dgt_ref, dup_ref, dwd_ref, cs_ref):
        @pl.when(pl.program_id(0) == 0)
        def _():
            cs_ref[...] = jnp.zeros_like(cs_ref)
            dwd_ref[...] = jnp.zeros_like(dwd_ref)

        dx2_t, f_t = dx2_ref[...], f_ref[...]
        gate, g_post = vec_ref[3:4, :], vec_ref[4:5, :]
        rf = _rsqrt_ms(f_t)
        f_hat = f_t * rf
        df = _norm_bwd(dx2_t * gate * g_post, f_hat, rf).astype(BF16)
        _accumulate_rows(cs_ref, [_colsum(dx2_t * (f_hat * g_post)), _colsum(dx2_t * gate * f_hat)])
        for c in range(D_FF // FF_CHUNK):
            cols = slice(c * FF_CHUNK, (c + 1) * FF_CHUNK)
            dact = _dot(df, wd_ref[cols, :], NT)
            g, u_t = gt_ref[:, cols].astype(F32), up_ref[:, cols].astype(F32)
            sg = 0.5 * jnp.tanh(0.5 * g) + 0.5
            silu = g * sg
            dgt_ref[:, cols] = (dact * u_t * (sg * (1.0 + g * (1.0 - sg)))).astype(BF16)
            dup_ref[:, cols] = (dact * silu).astype(BF16)
            dwd_ref[cols, :] += _dot((silu * u_t).astype(BF16), df, TN)

    row = lambda w: pl.BlockSpec((ts, w), lambda i: (i, 0))
    wide = jax.ShapeDtypeStruct((s, D_FF), BF16)
    return _call(
        body, [dx2, f, gt, up, vec, w_down], name=name, grid=(s // ts,),
        in_specs=[row(D_MODEL), row(D_MODEL), row(D_FF), row(D_FF), _resident((8, D_MODEL)), _resident((D_FF, D_MODEL))],
        out_specs=[row(D_FF), row(D_FF), _resident((D_FF, D_MODEL)), pl.BlockSpec((8, D_MODEL), lambda i: (0, 0))],
        out_shape=[wide, wide, jax.ShapeDtypeStruct((D_FF, D_MODEL), F32), jax.ShapeDtypeStruct((8, D_MODEL), F32)],
        exchanges=exchanges)


def _ffn_bwd_in(dx2, x1, dgt, dup, vec, w_gate_t, w_up_t, name, exchanges=()):
    s = x1.shape[0]
    ts = _seq_tile(s)

    def body(dx2_ref, x1_ref, dgt_ref, dup_ref, vec_ref, wg_ref, wu_ref, dx1_ref, cs_ref):
        @pl.when(pl.program_id(0) == 0)
        def _():
            cs_ref[...] = jnp.zeros_like(cs_ref)

        g_pre, one_scale = vec_ref[0:1, :], vec_ref[1:2, :]
        dh2 = _dot(dgt_ref[...], wg_ref[...]) + _dot(dup_ref[...], wu_ref[...])
        x1_t = x1_ref[...]
        r1 = _rsqrt_ms(x1_t)
        x_hat = x1_t * r1
        dx1_ref[...] = dx2_ref[...] + _norm_bwd(dh2 * g_pre * one_scale, x_hat, r1)
        _accumulate_rows(cs_ref, [_colsum(dh2), _colsum(dh2 * (x_hat * g_pre)), _colsum(dh2 * one_scale * x_hat)])

    row = lambda w: pl.BlockSpec((ts, w), lambda i: (i, 0))
    return _call(
        body, [dx2, x1, dgt, dup, vec, w_gate_t, w_up_t], name=name, grid=(s // ts,),
        in_specs=[row(D_MODEL), row(D_MODEL), row(D_FF), row(D_FF), _resident((8, D_MODEL)),
                  _resident((D_FF, D_MODEL)), _resident((D_FF, D_MODEL))],
        out_specs=[row(D_MODEL), pl.BlockSpec((8, D_MODEL), lambda i: (0, 0))],
        out_shape=[jax.ShapeDtypeStruct((s, D_MODEL), F32), jax.ShapeDtypeStruct((8, D_MODEL), F32)],
        exchanges=exchanges)


def _weight_grad(a, b, name, exchanges=()):
    s, m = a.shape
    n = b.shape[1]
    tk = min(s, 2 * SEQ_TILE)
    steps = s // tk

    def body(a_ref, b_ref, o_ref, acc_ref):
        i = pl.program_id(0)

        @pl.when(i == 0)
        def _():
            acc_ref[...] = jnp.zeros_like(acc_ref)

        b_t = b_ref[...]
        for c in range(m // FF_CHUNK):
            rows = slice(c * FF_CHUNK, (c + 1) * FF_CHUNK)
            acc_ref[rows, :] += _dot(a_ref[:, rows], b_t, TN)

        @pl.when(i == steps - 1)
        def _():
            o_ref[...] = acc_ref[...].astype(BF16)

    (grad,), ex_outs = _call(
        body, [a, b], name=name, grid=(steps,),
        in_specs=[pl.BlockSpec((tk, m), lambda i: (i, 0)), pl.BlockSpec((tk, n), lambda i: (i, 0))],
        out_specs=[_resident((m, n))],
        out_shape=[jax.ShapeDtypeStruct((m, n), BF16)],
        scratch_shapes=[pltpu.VMEM((m, n), F32)],
        exchanges=exchanges)
    return grad, ex_outs


def _out_bwd(dx1, mix, ao, u, pool_w, psc, w_out, vec, name, exchanges=()):
    s = dx1.shape[0]
    ts = _seq_tile(s)
    nt = s // ts
    hb = ts // POOL_HALO
    ng = len(POOL_WINDOWS)

    def body(dx1_ref, mix_ref, ao_ref, u_ref, uh_ref, pw_ref, psc_ref, w_ref, vec_ref,
             do_ref, du_ref, dw_out_ref, cs_ref, dpw_ref, dpsc_ref, carry_ref, dw_ref):
        i = pl.program_id(0)
        tile = nt - 1 - i

        @pl.when(i == 0)
        def _():
            dw_ref[...] = jnp.zeros_like(dw_ref)
            cs_ref[...] = jnp.zeros_like(cs_ref)
            dpw_ref[...] = jnp.zeros_like(dpw_ref)
            dpsc_ref[...] = jnp.zeros_like(dpsc_ref)
            carry_ref[...] = jnp.zeros_like(carry_ref)

        dx1_t, mix_t = dx1_ref[...], mix_ref[...]
        gate, g_post = vec_ref[0:1, :], vec_ref[1:2, :]
        rm = _rsqrt_ms(mix_t)
        m_hat = mix_t * rm
        dmix = _norm_bwd(dx1_t * gate * g_post, m_hat, rm).astype(BF16)
        _accumulate_rows(cs_ref, [_colsum(dx1_t * (m_hat * g_post)), _colsum(dx1_t * gate * m_hat)])
        d_cat = _dot(dmix, w_ref[...], NT)
        do_ref[...] = d_cat[:, 0:ATTN_WIDTH].astype(BF16)

        u_t = u_ref[...]
        halo = jnp.where(tile > 0, uh_ref[...], 0.0)
        counts = _pool_counts(tile, ts)
        cat = [ao_ref[...]]
        for gi in range(ng):
            cols = slice(gi * POOL_GROUP_WIDTH, (gi + 1) * POOL_GROUP_WIDTH)
            scale = psc_ref[0:1, cols]
            pooled = _pooled(u_t, halo, counts, gi).astype(BF16)
            og = _dot(pooled, pw_ref[gi])
            cat.append((og * scale).astype(BF16))
            d_out = d_cat[:, ATTN_WIDTH + gi * POOL_GROUP_WIDTH:ATTN_WIDTH + (gi + 1) * POOL_GROUP_WIDTH]
            dpsc_ref[0:1, cols] += _colsum(d_out * og)
            d_og = (d_out * scale).astype(BF16)
            dpw_ref[gi] += _dot(pooled, d_og, TN)
            d_pooled = _dot(d_og, pw_ref[gi], NT)
            spread = d_pooled / counts[gi]
            acc = jnp.concatenate([spread, carry_ref[:, cols]], axis=0)
            shift = 1
            while shift < POOL_WINDOWS[gi]:
                acc = acc + pltpu.roll(acc, ts + POOL_HALO - shift, 0)
                shift *= 2
            du_ref[:, cols] = acc[0:ts, :] - d_pooled
            carry_ref[:, cols] = spread[0:POOL_HALO, :]
        dw_ref[...] += _dot(jnp.concatenate(cat, axis=1), dmix, TN)

        @pl.when(i == nt - 1)
        def _():
            dw_out_ref[...] = dw_ref[...].astype(BF16)

    row = lambda w: pl.BlockSpec((ts, w), lambda i: (nt - 1 - i, 0))
    fixed = lambda shape: pl.BlockSpec(shape, lambda i: (0,) * len(shape))
    return _call(
        body, [dx1, mix, ao, u, u, pool_w, psc, w_out, vec], name=name, grid=(nt,),
        in_specs=[row(D_MODEL), row(D_MODEL), row(ATTN_WIDTH), row(POOL_WIDTH),
                  pl.BlockSpec((POOL_HALO, POOL_WIDTH), lambda i: (jnp.maximum((nt - 1 - i) * hb - 1, 0), 0)),
                  _resident(pool_w.shape), _resident((8, POOL_WIDTH)), _resident((D_MODEL, D_MODEL)),
                  _resident((8, D_MODEL))],
        out_specs=[row(ATTN_WIDTH), row(POOL_WIDTH), _resident((D_MODEL, D_MODEL)), fixed((8, D_MODEL)),
                   fixed(pool_w.shape), fixed((8, POOL_WIDTH))],
        out_shape=[jax.ShapeDtypeStruct((s, ATTN_WIDTH), BF16), jax.ShapeDtypeStruct((s, POOL_WIDTH), F32),
                   jax.ShapeDtypeStruct((D_MODEL, D_MODEL), BF16),
                   jax.ShapeDtypeStruct((8, D_MODEL), F32), jax.ShapeDtypeStruct(pool_w.shape, F32),
                   jax.ShapeDtypeStruct((8, POOL_WIDTH), F32)],
        scratch_shapes=[pltpu.VMEM((POOL_HALO, POOL_WIDTH), F32), pltpu.VMEM((D_MODEL, D_MODEL), F32)],
        exchanges=exchanges)


def _in_bwd(dq, dkc, dkp, dvc, dvp, du, x, dx1, vec, w_in_t, rot, name, exchanges=()):
    s = x.shape[0]
    ts = _seq_tile(s)
    nt = s // ts
    bpt = ts // BLOCK

    def body(dq_ref, dkc_ref, dkp_ref, dkn_ref, dvc_ref, dvp_ref, dvn_ref, du_ref, x_ref, dx1_ref, vec_ref, w_ref,
             cos_ref, lo_ref, hi_ref, dx_ref, dw_out_ref, cs_ref, db_ref, dw_ref):
        i = pl.program_id(0)

        @pl.when(i == 0)
        def _():
            dw_ref[...] = jnp.zeros_like(dw_ref)
            cs_ref[...] = jnp.zeros_like(cs_ref)
            db_ref[...] = jnp.zeros_like(db_ref)

        cos, lo, hi = cos_ref[...], lo_ref[...], hi_ref[...]

        def with_next_block(cur_ref, prev_ref, next_ref):
            nxt = jnp.where(i < nt - 1, next_ref[...], 0.0)
            later = nxt if bpt == 1 else jnp.concatenate([prev_ref[BLOCK:, :], nxt], axis=0)
            return cur_ref[...] + later

        pieces = [_rotate_bwd(dq_ref[:, j * LANES:(j + 1) * LANES], cos, lo, hi) for j in range(ATTN_WIDTH // LANES)]
        pieces.append(_rotate_bwd(with_next_block(dkc_ref, dkp_ref, dkn_ref), cos, lo, hi))
        pieces.append(with_next_block(dvc_ref, dvp_ref, dvn_ref))
        pieces.append(du_ref[...])
        dproj = jnp.concatenate(pieces, axis=1)
        db_ref[0:1, :] += _colsum(dproj)
        dproj_b = dproj.astype(BF16)
        dh = _dot(dproj_b, w_ref[...])

        xt = x_ref[...]
        g_pre, one_scale = vec_ref[0:1, :], vec_ref[1:2, :]
        r = _rsqrt_ms(xt)
        x_hat = xt * r
        dw_ref[...] += _dot(dproj_b, (x_hat * g_pre * one_scale + vec_ref[2:3, :]).astype(BF16), TN)
        dx_ref[...] = dx1_ref[...] + _norm_bwd(dh * g_pre * one_scale, x_hat, r)
        _accumulate_rows(cs_ref, [_colsum(dh), _colsum(dh * (x_hat * g_pre)), _colsum(dh * one_scale * x_hat)])

        @pl.when(i == nt - 1)
        def _():
            dw_out_ref[...] = dw_ref[...].astype(BF16)

    row = lambda w: pl.BlockSpec((ts, w), lambda i: (i, 0))
    nxt = pl.BlockSpec((BLOCK, KV_WIDTH), lambda i: (jnp.minimum((i + 1) * bpt, s // BLOCK - 1), 0))
    fixed = lambda shape: pl.BlockSpec(shape, lambda i: (0,) * len(shape))
    return _call(
        body, [dq, dkc, dkp, dkp, dvc, dvp, dvp, du, x, dx1, vec, w_in_t, *rot], name=name, grid=(nt,),
        in_specs=[row(ATTN_WIDTH), row(KV_WIDTH), row(KV_WIDTH), nxt, row(KV_WIDTH), row(KV_WIDTH), nxt,
                  row(POOL_WIDTH), row(D_MODEL), row(D_MODEL), _resident((8, D_MODEL)), _resident((IN_WIDTH, D_MODEL)),
                  row(LANES), row(LANES), row(LANES)],
        out_specs=[row(D_MODEL), _resident((IN_WIDTH, D_MODEL)), fixed((8, D_MODEL)), fixed((8, IN_WIDTH))],
        out_shape=[jax.ShapeDtypeStruct((s, D_MODEL), F32), jax.ShapeDtypeStruct((IN_WIDTH, D_MODEL), BF16),
                   jax.ShapeDtypeStruct((8, D_MODEL), F32), jax.ShapeDtypeStruct((8, IN_WIDTH), F32)],
        scratch_shapes=[pltpu.VMEM((IN_WIDTH, D_MODEL), F32)],
        exchanges=exchanges)


def _row_tile(rows):
    for t in (512, 352, 256, 176, 160, 128, 64, 32, 16, 8):
        if rows % t == 0:
            return t
    raise ValueError(f"no row tile for {rows} rows")


def _sum_parts(parts, first, name):
    n, r, c = parts.shape
    tr = _row_tile(r)

    def body(p_ref, o_ref):
        acc = p_ref[first].astype(F32)
        for k in range(n):
            if k != first:
                acc = acc + p_ref[k].astype(F32)
        o_ref[...] = acc

    return pl.pallas_call(
        body, name=name, grid=(r // tr,),
        in_specs=[pl.BlockSpec((n, tr, c), lambda i: (0, i, 0))],
        out_specs=pl.BlockSpec((tr, c), lambda i: (i, 0)),
        out_shape=jax.ShapeDtypeStruct((r, c), F32),
        compiler_params=_cparams(1),
    )(parts)


def _adam_step(w, g, m, v):
    m_new = ADAM_B1 * m + (1.0 - ADAM_B1) * g
    v_new = ADAM_B2 * v + (1.0 - ADAM_B2) * (g * g)
    m_hat = m_new / (1.0 - ADAM_B1 ** ADAM_STEP)
    v_hat = v_new / (1.0 - ADAM_B2 ** ADAM_STEP)
    return -ADAM_LR * (m_hat / (jnp.sqrt(v_hat) + ADAM_EPS) + ADAM_WD * w), m_new, v_new


def _sum_adamw(parts, w, m, v, name):
    depth, r, c = w.shape
    tr = _row_tile(r)

    def body(*refs):
        p_refs = refs[:depth]
        w_ref, m_ref, v_ref, g_ref, d_ref, nm_ref, nv_ref = refs[depth:]
        g = None
        for layer, p_ref in enumerate(p_refs):
            acc = p_ref[_Scatter.OWN].astype(F32)
            for k in range(N_DEV - 1):
                acc = acc + p_ref[k].astype(F32)
            g = acc if g is None else jnp.where(pl.program_id(0) == layer, acc, g)
        g_ref[0] = g
        d_ref[0], nm_ref[0], nv_ref[0] = _adam_step(w_ref[0], g, m_ref[0], v_ref[0])

    spec = pl.BlockSpec((1, tr, c), lambda l, i: (l, i, 0))
    shape = jax.ShapeDtypeStruct((depth, r, c), F32)
    return pl.pallas_call(
        body, name=name, grid=(depth, r // tr),
        in_specs=[pl.BlockSpec((N_DEV, tr, c), lambda l, i: (0, i, 0))] * depth + [spec] * 3,
        out_specs=[spec] * 4, out_shape=[shape] * 4, compiler_params=_cparams(2),
    )(*parts, w, m, v)


def _adamw(w, g, m, v, name):
    r, c = w.shape
    tr = _row_tile(r) if r % 8 == 0 else r

    def body(w_ref, g_ref, m_ref, v_ref, d_ref, nm_ref, nv_ref):
        d_ref[...], nm_ref[...], nv_ref[...] = _adam_step(w_ref[...], g_ref[...], m_ref[...], v_ref[...])

    spec = pl.BlockSpec((tr, c), lambda i: (i, 0))
    shape = jax.ShapeDtypeStruct((r, c), F32)
    return pl.pallas_call(
        body, name=name, grid=(r // tr,), in_specs=[spec] * 4, out_specs=[spec] * 3, out_shape=[shape] * 3,
        compiler_params=_cparams(1),
    )(w, g, m, v)


LAYER_ROWS = 80


def _pack_layer(dmod, gains, b_in, pool_scale, sinks, pool_w):
    misc = jnp.concatenate([pool_scale, sinks, jnp.zeros((D_MODEL - POOL_WIDTH - N_HEADS,), F32)])
    parts = [dmod.reshape(N_MOD, D_MODEL), gains, jnp.pad(b_in, (0, 2 * D_MODEL - IN_WIDTH)).reshape(2, D_MODEL),
             misc.reshape(1, D_MODEL), pool_w.reshape(-1, D_MODEL)]
    packed = jnp.concatenate(parts, axis=0)
    return jnp.pad(packed, ((0, LAYER_ROWS - packed.shape[0]), (0, 0)))


def _unpack_layer(p):
    r = N_MOD + 4
    b_in = p[r:r + 2].reshape(2 * D_MODEL)[:IN_WIDTH]
    pool_w = p[r + 3:r + 3 + 64].reshape(len(POOL_WINDOWS), POOL_GROUP_WIDTH, POOL_GROUP_WIDTH)
    return (p[0:N_MOD].reshape(N_MOD * D_MODEL), p[N_MOD:r], b_in, p[r + 2, :POOL_WIDTH],
            p[r + 2, POOL_WIDTH:POOL_WIDTH + N_HEADS], pool_w)


def _rotary_tables(positions):
    inv_freq = ROPE_THETA ** (-jnp.arange(0, ROT_DIM, 2, dtype=F32) / ROT_DIM)
    half = ROT_DIM // 2
    head_freq = jnp.concatenate([inv_freq, inv_freq, jnp.zeros((HEAD_DIM - ROT_DIM,), F32)])
    lane_freq = jnp.concatenate([head_freq, head_freq])
    ang = positions.reshape(-1).astype(F32)[:, None] * lane_freq[None, :]
    cos, sin = jnp.cos(ang), jnp.sin(ang)
    in_head = lax.broadcasted_iota(jnp.int32, (1, LANES), 1) % HEAD_DIM
    return cos, jnp.where(in_head < half, -sin, 0.0), jnp.where((in_head >= half) & (in_head < ROT_DIM), sin, 0.0)


def kernel(x, c, positions, ada_w, ada_b, w_in, b_in, sinks, pool_w, pool_scale, w_out, w_gate, w_up, w_down, g_pre_mix, g_post_mix, g_pre_ffn, g_post_ffn, loss_target, m_ada_w, m_ada_b, m_w_in, m_b_in, m_sinks, m_pool_w, m_pool_scale, m_w_out, m_w_gate, m_w_up, m_w_down, m_g_pre_mix, m_g_post_mix, m_g_pre_ffn, m_g_post_ffn, v_ada_w, v_ada_b, v_w_in, v_b_in, v_sinks, v_pool_w, v_pool_scale, v_w_out, v_w_gate, v_w_up, v_w_down, v_g_pre_mix, v_g_post_mix, v_g_pre_ffn, v_g_post_ffn):
    me = _index(_mesh_place())
    x0 = x[0]
    target = loss_target[0]
    rot = _rotary_tables(positions)
    ada_cols = ada_w.shape[2]

    shards = []
    for l in range(DEPTH):
        shards += [w_in[l].T.astype(BF16), w_out[l].astype(BF16), w_gate[l].T.astype(BF16), w_up[l].T.astype(BF16),
                   w_down[l].astype(BF16)]
    whole = lambda gathered: [g.reshape(-1, D_MODEL) for g in gathered]
    full = [None] * (5 * DEPTH)
    first = _exchanges_alone([_Gather(shards[0:2] + [jnp.broadcast_to(c, (8, D_MODEL))])], "gather_first")[0]
    full[0:2] = whole(first[0:2])

    mod_part, c_act = _mod_fwd(first[2][:, 0, :], ada_w)
    mod_all = _allgather_vmem(mod_part.reshape(DEPTH * N_DEV, ada_cols), "gather_mod")
    mod_mine = lax.dynamic_slice_in_dim(mod_all.reshape(N_DEV, DEPTH, N_DEV, ada_cols), me, 1, axis=2)[:, :, 0, :]
    mod = jnp.transpose(mod_mine, (1, 0, 2)).reshape(DEPTH, N_MOD * D_MODEL) + ada_b
    mod = mod.reshape(DEPTH, N_MOD, D_MODEL)

    hosted = {("fwd_in", 0): (_Gather(shards[2:3]), [2]), ("attn_fwd", 0): (_Gather(shards[3:4]), [3]),
              ("fwd_out", 0): (_Gather(shards[4:5]), [4]),
              ("ffn_fwd", 0): (_Gather(shards[5:10]), [5, 6, 7, 8, 9])}

    def take(kind, l, ex_outs):
        if (kind, l) in hosted:
            for slot, g in zip(hosted[kind, l][1], whole(ex_outs[0])):
                full[slot] = g

    def beside(kind, l):
        return [hosted[kind, l][0]] if (kind, l) in hosted else []

    pool_w_b = pool_w.astype(BF16)

    saved = []
    xl = x0
    for l in range(DEPTH):
        vec_in = _rows(g_pre_mix[l], 1.0 + mod[l, 1], mod[l, 0], width=D_MODEL)
        vec_out = _rows(mod[l, 2], g_post_mix[l], width=D_MODEL)
        vec_ffn = _rows(g_pre_ffn[l], 1.0 + mod[l, 4], mod[l, 3], mod[l, 5], g_post_ffn[l], width=D_MODEL)
        psc = _rows(pool_scale[l], width=POOL_WIDTH)
        (q, k, v, u), ex = _fwd_in(xl, vec_in, full[5 * l], _rows(b_in[l], width=IN_WIDTH), rot, f"fwd_in_{l}",
                                   beside("fwd_in", l))
        take("fwd_in", l, ex)
        (ao, lse), ex = _attn_fwd_t(sinks[l], q, k, v, f"attn_fwd_{l}", beside("attn_fwd", l))
        take("attn_fwd", l, ex)
        (x1, mix), ex = _fwd_out(ao, u, pool_w_b[l], psc, full[5 * l + 1], xl, vec_out, f"fwd_out_{l}",
                                 beside("fwd_out", l))
        take("fwd_out", l, ex)
        (x2, f, gt, up, h2, *loss_part), ex = _ffn_fwd(
            x1, vec_ffn, full[5 * l + 2], full[5 * l + 3], full[5 * l + 4], f"ffn_fwd_{l}", beside("ffn_fwd", l),
            target=target if l == DEPTH - 1 else None)
        take("ffn_fwd", l, ex)
        saved.append((xl, q, k, v, u, ao, lse, x1, mix, f, gt, up, h2, vec_in, vec_out, vec_ffn, psc))
        xl = x2

    dx = xl
    loss = lax.psum(loss_part[0][0, 0], ("x", "y", "c"))

    stacks = [None] * (5 * DEPTH)
    parts = [None] * (5 * DEPTH)
    small_all = [None] * DEPTH
    ready_stacks, ready_small = [], []

    def leaving(row_budget):
        exs, notes = [], []
        taken = []
        while ready_stacks and stacks[ready_stacks[0]].shape[1] <= row_budget:
            row_budget -= stacks[ready_stacks[0]].shape[1]
            taken.append(ready_stacks.pop(0))
        if taken:
            exs.append(_Scatter([stacks[a] for a in taken]))
            notes.append(("stacks", taken))
        if ready_small and row_budget >= 0:
            exs.append(_Gather([ready_small[0][1]]))
            notes.append(("small", ready_small[0][0]))
            ready_small.clear()
        return exs, notes

    def arrived(notes, ex_outs):
        for (kind, what), outs in zip(notes, ex_outs):
            if kind == "stacks":
                for a, o in zip(what, outs):
                    parts[a] = o
            else:
                small_all[what] = outs[0]

    def grad_ready(a, grad):
        stacks[a] = grad.reshape(N_DEV, -1, D_MODEL)
        ready_stacks.append(a)

    def hosting(row_budget, kernel_fn, *args):
        exs, notes = leaving(row_budget)
        outs, ex_outs = kernel_fn(*args, exs)
        arrived(notes, ex_outs)
        return outs

    for l in reversed(range(DEPTH)):
        w_in_t, w_out_f, w_gate_t, w_up_t, w_down_f = full[5 * l:5 * l + 5]
        xin, q, k, v, u, ao, lse, x1, mix, f, gt, up, h2, vec_in, vec_out, vec_ffn, psc = saved[l]
        last = l == 0
        dgt, dup, d_w_down, cs_a = hosting(600, _ffn_bwd_act, dx, f, gt, up, vec_ffn, w_down_f, f"ffn_bwd_act_{l}")
        grad_ready(5 * l + 4, d_w_down.astype(BF16))
        grad_ready(5 * l + 2, hosting(-1, _weight_grad, dgt, h2, f"grad_w_gate_{l}"))
        grad_ready(5 * l + 3, hosting(-1, _weight_grad, dup, h2, f"grad_w_up_{l}"))
        dx1, cs_f = hosting(400, _ffn_bwd_in, dx, x1, dgt, dup, vec_ffn, w_gate_t, w_up_t, f"ffn_bwd_in_{l}")
        do, du, d_w_out, cs_o, dpw, dpsc = hosting(360, _out_bwd, dx1, mix, ao, u, pool_w_b[l], psc, w_out_f, vec_out,
                                                   f"out_bwd_{l}")
        grad_ready(5 * l + 1, d_w_out)
        dq, dkc, dkp, dvc, dvp, dsk = hosting(480 if last else 400, _attn_bwd_t, sinks[l], q, do, lse, k, v,
                                              f"attn_bwd_{l}")
        dx, d_w_in_t, cs_i, db = hosting(-1 if last else 250, _in_bwd, dq, dkc, dkp, dvc, dvp, du, xin, dx1, vec_in,
                                         w_in_t, rot, f"in_bwd_{l}")
        grad_ready(5 * l, d_w_in_t)
        d_mod = jnp.concatenate([cs_i[0], cs_i[1], cs_o[0], cs_f[0], cs_f[1], cs_a[0]])
        d_gain = jnp.stack([cs_i[2], cs_o[1], cs_f[2], cs_a[1]])
        ready_small.append((l, _pack_layer(d_mod, d_gain, db[0], dpsc[0], dsk[:, 0], dpw)))
    grad_x = dx[None]
    exs, notes = leaving(N_DEV * D_FF)
    arrived(notes, _exchanges_alone(exs, "exchange_last"))

    layer_sums = [_unpack_layer(_sum_parts(small_all[l], 0, f"sum_small_{l}")) for l in range(DEPTH)]
    g_ada_b, g_gains, g_b_in, g_pool_scale, g_sinks, g_pool_w = (
        jnp.stack([layer_sums[l][j] for l in range(DEPTH)], axis=1 if j == 1 else 0) for j in range(6))

    dmod_all = jnp.stack([small_all[l][:, 0:N_MOD, :].reshape(N_DEV, N_DEV, ada_cols) for l in range(DEPTH)])
    dmod_cols = lax.dynamic_slice_in_dim(dmod_all, me, 1, axis=2)[:, :, 0, :]
    g_ada_w = _ada_grad(c_act, dmod_cols)

    slot = {"w_in": 0, "w_out": 1, "w_gate": 2, "w_up": 3, "w_down": 4}
    held_transposed = ("w_in", "w_gate", "w_up")
    swap = lambda a: jnp.transpose(a, (0, 2, 1))

    def step(w, g, m, v, name):
        if name in slot:
            mine = [parts[5 * l + slot[name]] for l in range(DEPTH)]
            if name in held_transposed:
                return [swap(o) for o in _sum_adamw(mine, swap(w), swap(m), swap(v), f"adamw_{name}")]
            return _sum_adamw(mine, w, m, v, f"adamw_{name}")
        flat = lambda a: a.reshape(-1, a.shape[-1])
        return [g] + [o.reshape(w.shape) for o in _adamw(flat(w), flat(g), flat(m), flat(v), f"adamw_{name}")]

    in_order = [
        ("ada_w", ada_w, g_ada_w, m_ada_w, v_ada_w), ("ada_b", ada_b, g_ada_b, m_ada_b, v_ada_b),
        ("w_in", w_in, None, m_w_in, v_w_in), ("b_in", b_in, g_b_in, m_b_in, v_b_in),
        ("sinks", sinks, g_sinks, m_sinks, v_sinks), ("pool_w", pool_w, g_pool_w, m_pool_w, v_pool_w),
        ("pool_scale", pool_scale, g_pool_scale, m_pool_scale, v_pool_scale),
        ("w_out", w_out, None, m_w_out, v_w_out), ("w_gate", w_gate, None, m_w_gate, v_w_gate),
        ("w_up", w_up, None, m_w_up, v_w_up), ("w_down", w_down, None, m_w_down, v_w_down),
        ("g_pre_mix", g_pre_mix, g_gains[0], m_g_pre_mix, v_g_pre_mix),
        ("g_post_mix", g_post_mix, g_gains[1], m_g_post_mix, v_g_post_mix),
        ("g_pre_ffn", g_pre_ffn, g_gains[2], m_g_pre_ffn, v_g_pre_ffn),
        ("g_post_ffn", g_post_ffn, g_gains[3], m_g_post_ffn, v_g_post_ffn)]
    steps = [step(w, g, m, v, n) for n, w, g, m, v in in_order]
    grads, deltas, new_m, new_v = ([s[j] for s in steps] for j in range(4))
    return (loss, grad_x, *grads, *deltas, *new_m, *new_v)
```

```python
import jax
import jax.numpy as jnp
from jax import lax
from jax.experimental import pallas as pl
from jax.experimental.pallas import tpu as pltpu

F32 = jnp.float32
BF16 = jnp.bfloat16

N_DEV = 8
DEPTH = 2
D_MODEL = 1024
HEAD_DIM = 64
N_HEADS = 8
N_KV_HEADS = 2
GROUP = N_HEADS // N_KV_HEADS
ATTN_WIDTH = N_HEADS * HEAD_DIM
KV_WIDTH = N_KV_HEADS * HEAD_DIM
POOL_WIDTH = 512
POOL_WINDOWS = (2, 4, 8, 16)
POOL_GROUP_WIDTH = 128
POOL_HALO = 16
IN_WIDTH = ATTN_WIDTH + 2 * KV_WIDTH + POOL_WIDTH
D_FF = 2816
N_MOD = 6
BLOCK = 128
ROT_DIM = 16
ROPE_THETA = 500000.0
EPS = 1e-6
NEG_INF = -1e30
Q_SCALE = HEAD_DIM ** -0.5

ADAM_LR = 0.001
ADAM_B1 = 0.9
ADAM_B2 = 0.999
ADAM_EPS = 1e-08
ADAM_WD = 0.01
ADAM_STEP = 10

LANES = 128
SEQ_TILE = 512
ATTN_BLOCKS = 4
FF_CHUNK = 256
VMEM_LIMIT = 56 * 1024 * 1024
MESH = pl.DeviceIdType.MESH

NT = (((1,), (1,)), ((), ()))
TN = (((0,), (0,)), ((), ()))


def _dot(a, b, dims=None):
    if dims is None:
        return jnp.dot(a, b, preferred_element_type=F32)
    return lax.dot_general(a, b, dims, preferred_element_type=F32)


def _cparams(n_axes):
    return pltpu.CompilerParams(dimension_semantics=("arbitrary",) * n_axes, vmem_limit_bytes=VMEM_LIMIT)


def _resident(shape):
    zeros = (0,) * len(shape)
    return pl.BlockSpec(shape, lambda *_: zeros, pipeline_mode=pl.Buffered(1))


def _rows(*vectors, width):
    rows = [jnp.reshape(v, (1, width)).astype(F32) for v in vectors]
    rows.append(jnp.zeros((8 - len(rows), width), F32))
    return jnp.concatenate(rows, axis=0)


def _rsqrt_ms(x):
    return lax.rsqrt(jnp.mean(x * x, axis=-1, keepdims=True) + EPS)


def _colsum(x):
    return jnp.sum(x, axis=0, keepdims=True)


def _seq_tile(s, tiles=1):
    return min(s, tiles * SEQ_TILE)


def _mesh_place():
    x, y, c = lax.axis_index("x"), lax.axis_index("y"), lax.axis_index("c")
    return x, y, c


def _flip(place, k):
    x, y, c = place
    return (1 - x if k & 4 else x, 1 - y if k & 2 else y, 1 - c if k & 1 else c)


def _index(place):
    x, y, c = place
    return 4 * x + 2 * y + c


def _allgather_vmem(block, name):
    r, c = block.shape

    def body(x_ref, out_ref, send_sems, recv_sems, local_sem):
        me = _mesh_place()
        mine = pltpu.make_async_copy(x_ref, out_ref.at[_index(me)], local_sem)
        mine.start()

        def copy(k):
            return pltpu.make_async_remote_copy(
                src_ref=x_ref, dst_ref=out_ref.at[_index(me)], send_sem=send_sems.at[k - 1], recv_sem=recv_sems.at[k - 1],
                device_id=_flip(me, k), device_id_type=MESH)

        def arrival(k):
            return pltpu.make_async_remote_copy(
                src_ref=x_ref, dst_ref=out_ref.at[_index(_flip(me, k))], send_sem=send_sems.at[k - 1],
                recv_sem=recv_sems.at[k - 1], device_id=_flip(me, k), device_id_type=MESH)

        for k in range(1, N_DEV):
            copy(k).start()
        for k in range(1, N_DEV):
            arrival(k).wait_recv()
        for k in range(1, N_DEV):
            copy(k).wait_send()
        mine.wait()

    return pl.pallas_call(
        body, name=name,
        out_shape=jax.ShapeDtypeStruct((N_DEV, r, c), block.dtype),
        in_specs=[pl.BlockSpec(memory_space=pltpu.VMEM)],
        out_specs=pl.BlockSpec(memory_space=pltpu.VMEM),
        scratch_shapes=[pltpu.SemaphoreType.DMA((N_DEV - 1,)), pltpu.SemaphoreType.DMA((N_DEV - 1,)),
                        pltpu.SemaphoreType.DMA],
    )(block)


class _Gather:
    def __init__(self, shards):
        n = len(shards)
        self.operands = list(shards)
        self.out_shape = [jax.ShapeDtypeStruct((N_DEV,) + s.shape, s.dtype) for s in shards]
        self.scratch = [pltpu.SemaphoreType.DMA((N_DEV - 1, n)), pltpu.SemaphoreType.DMA((N_DEV - 1, n)),
                        pltpu.SemaphoreType.DMA((n,))]

    def _copies(self, x_refs, out_refs, sems):
        send_sems, recv_sems, local_sems = sems
        n = len(x_refs)
        x, y, c = _mesh_place()
        me, sibling = (x, y, c), (x, y, 1 - c)
        chips = [(1 - x, y), (x, 1 - y), (1 - x, 1 - y)]

        def copy(k, a, block, to, from_input=False):
            rows = out_refs[a].at[_index(block)]
            return pltpu.make_async_remote_copy(
                src_ref=x_refs[a] if from_input else rows, dst_ref=rows,
                send_sem=send_sems.at[k, a], recv_sem=recv_sems.at[k, a], device_id=to, device_id_type=MESH)

        mine = [pltpu.make_async_copy(x_refs[a], out_refs[a].at[_index(me)], local_sems.at[a]) for a in range(n)]
        first = [copy(0, a, me, sibling, from_input=True) for a in range(n)]
        first += [copy(1 + j, a, me, (*chip, c), from_input=True) for j, chip in enumerate(chips) for a in range(n)]
        over_ici = [copy(1 + j, a, (*chip, c), me) for j, chip in enumerate(chips) for a in range(n)]
        passed = [copy(4 + j, a, (*chip, c), sibling) for j, chip in enumerate(chips) for a in range(n)]
        from_sibling = [copy(0, a, sibling, me) for a in range(n)]
        from_sibling += [copy(4 + j, a, (*chip, 1 - c), me) for j, chip in enumerate(chips) for a in range(n)]
        return mine, first, over_ici, passed, from_sibling

    def begin(self, x_refs, out_refs, sems):
        mine, first, _, _, _ = self._copies(x_refs, out_refs, sems)
        for cp in mine + first:
            cp.start()

    def middle(self, x_refs, out_refs, sems):
        _, _, over_ici, passed, _ = self._copies(x_refs, out_refs, sems)
        for arrived, onward in zip(over_ici, passed):
            arrived.wait_recv()
            onward.start()

    def end(self, x_refs, out_refs, sems):
        mine, first, _, passed, from_sibling = self._copies(x_refs, out_refs, sems)
        for cp in from_sibling:
            cp.wait_recv()
        for cp in first + passed:
            cp.wait_send()
        for cp in mine:
            cp.wait()


class _Scatter:
    OWN = N_DEV - 1

    def __init__(self, stacks):
        n = len(stacks)
        self.operands = list(stacks)
        self.out_shape = [jax.ShapeDtypeStruct(s.shape, s.dtype) for s in stacks]
        self.scratch = [pltpu.SemaphoreType.DMA((N_DEV - 1, n)), pltpu.SemaphoreType.DMA((N_DEV - 1, n)),
                        pltpu.SemaphoreType.DMA((n,))]

    def _copies(self, g_refs, out_refs, sems):
        send_sems, recv_sems, local_sems = sems
        me = _mesh_place()

        def copy(k, a):
            peer = _flip(me, k)
            return pltpu.make_async_remote_copy(
                src_ref=g_refs[a].at[_index(peer)], dst_ref=out_refs[a].at[k - 1],
                send_sem=send_sems.at[k - 1, a], recv_sem=recv_sems.at[k - 1, a], device_id=peer, device_id_type=MESH)

        mine = [pltpu.make_async_copy(g_refs[a].at[_index(me)], out_refs[a].at[self.OWN], local_sems.at[a])
                for a in range(len(g_refs))]
        return mine, [copy(k, a) for k in range(1, N_DEV) for a in range(len(g_refs))]

    def begin(self, g_refs, out_refs, sems):
        mine, copies = self._copies(g_refs, out_refs, sems)
        for cp in mine + copies:
            cp.start()

    def middle(self, g_refs, out_refs, sems):
        pass

    def end(self, g_refs, out_refs, sems):
        mine, copies = self._copies(g_refs, out_refs, sems)
        for cp in copies:
            cp.wait_recv()
        for cp in copies:
            cp.wait_send()
        for cp in mine:
            cp.wait()


def _call(body, args, *, name, grid, in_specs, out_specs, out_shape, scratch_shapes=(), exchanges=()):
    if not exchanges:
        outs = pl.pallas_call(body, name=name, grid=grid, in_specs=in_specs, out_specs=out_specs, out_shape=out_shape,
                              scratch_shapes=list(scratch_shapes), compiler_params=_cparams(len(grid)))(*args)
        return outs, []
    (steps,) = grid
    n_in, n_out, n_scr = len(in_specs), len(out_specs), len(scratch_shapes)
    ex_in = [len(ex.operands) for ex in exchanges]
    ex_out = [len(ex.out_shape) for ex in exchanges]
    ex_scr = [len(ex.scratch) for ex in exchanges]

    def split(refs, counts):
        parts, pos = [], 0
        for cnt in counts:
            parts.append(refs[pos:pos + cnt])
            pos += cnt
        return parts

    def wrapped(*refs):
        ins, xin, outs, xout, scr, xscr = split(refs, [n_in, sum(ex_in), n_out, sum(ex_out), n_scr, sum(ex_scr)])
        bound = list(zip(exchanges, split(xin, ex_in), split(xout, ex_out), split(xscr, ex_scr)))
        step = pl.program_id(0)

        def phase(method, at):
            @pl.when(step == at)
            def _():
                for ex, i_refs, o_refs, sems in bound:
                    getattr(ex, method)(i_refs, o_refs, sems)

        phase("begin", 0)
        phase("middle", (3 * steps) // 4)
        body(*ins, *outs, *scr)
        phase("end", steps - 1)

    any_spec = pl.BlockSpec(memory_space=pl.ANY)
    results = pl.pallas_call(
        wrapped, name=name, grid=grid,
        in_specs=list(in_specs) + [any_spec] * sum(ex_in),
        out_specs=list(out_specs) + [any_spec] * sum(ex_out),
        out_shape=list(out_shape) + [s for ex in exchanges for s in ex.out_shape],
        scratch_shapes=list(scratch_shapes) + [s for ex in exchanges for s in ex.scratch],
        compiler_params=_cparams(1),
    )(*args, *[a for ex in exchanges for a in ex.operands])
    return results[:n_out], split(results[n_out:], ex_out)


def _exchanges_alone(exchanges, name):
    def body(flag_ref):
        flag_ref[...] = jnp.zeros_like(flag_ref)

    _, ex_outs = _call(body, [], name=name, grid=(1,), in_specs=[], out_specs=[pl.BlockSpec((8, LANES), lambda i: (0, 0))],
                       out_shape=[jax.ShapeDtypeStruct((8, LANES), F32)], exchanges=exchanges)
    return ex_outs


def _mod_fwd(c_all, ada_w):
    cols = ada_w.shape[2]

    def body(c_ref, w_ref, mp_ref, act_ref):
        c = c_ref[...]
        act = c * jax.nn.sigmoid(c)
        act_ref[...] = act
        mp_ref[0] = _dot(act.astype(BF16), w_ref[0].astype(BF16))

    return pl.pallas_call(
        body, name="mod_fwd", grid=(DEPTH,),
        in_specs=[pl.BlockSpec((N_DEV, D_MODEL), lambda l: (0, 0)),
                  pl.BlockSpec((1, D_MODEL, cols), lambda l: (l, 0, 0))],
        out_specs=[pl.BlockSpec((1, N_DEV, cols), lambda l: (l, 0, 0)),
                   pl.BlockSpec((N_DEV, D_MODEL), lambda l: (0, 0))],
        out_shape=[jax.ShapeDtypeStruct((DEPTH, N_DEV, cols), F32), jax.ShapeDtypeStruct((N_DEV, D_MODEL), F32)],
        compiler_params=_cparams(1),
    )(c_all, ada_w)


def _ada_grad(c_act, dmod_cols):
    cols = dmod_cols.shape[2]

    def body(act_ref, dm_ref, g_ref):
        g_ref[0] = _dot(act_ref[...].astype(BF16), dm_ref[0].astype(BF16), TN)

    return pl.pallas_call(
        body, name="ada_grad", grid=(DEPTH,),
        in_specs=[pl.BlockSpec((N_DEV, D_MODEL), lambda l: (0, 0)),
                  pl.BlockSpec((1, N_DEV, cols), lambda l: (l, 0, 0))],
        out_specs=pl.BlockSpec((1, D_MODEL, cols), lambda l: (l, 0, 0)),
        out_shape=jax.ShapeDtypeStruct((DEPTH, D_MODEL, cols), F32),
        compiler_params=_cparams(1),
    )(c_act, dmod_cols)


def _rotary_terms(cos, sin):
    in_head = lax.broadcasted_iota(jnp.int32, (1, LANES), 1) % HEAD_DIM
    half = ROT_DIM // 2
    return cos, jnp.where(in_head < half, -sin, 0.0), jnp.where((in_head >= half) & (in_head < ROT_DIM), sin, 0.0)


def _rotate(t, cos, sin_lo, sin_hi):
    return t * cos + pltpu.roll(t, LANES - 8, 1) * sin_lo + pltpu.roll(t, 8, 1) * sin_hi


def _rotate_bwd(t, cos, sin_lo, sin_hi):
    return t * cos + pltpu.roll(t * sin_lo, 8, 1) + pltpu.roll(t * sin_hi, LANES - 8, 1)


def _fwd_in(x, vec, w_in_t, b_in, rot, name, exchanges=()):
    s = x.shape[0]
    ts = _seq_tile(s, 2)

    def body(x_ref, vec_ref, w_ref, b_ref, cos_ref, sin_ref, q_ref, k_ref, v_ref, u_ref):
        xt = x_ref[...]
        h = xt * _rsqrt_ms(xt) * vec_ref[0:1, :] * vec_ref[1:2, :] + vec_ref[2:3, :]
        proj = _dot(h.astype(BF16), w_ref[...], NT) + b_ref[0:1, :]
        cos, lo, hi = _rotary_terms(cos_ref[...], sin_ref[...])
        for j in range(ATTN_WIDTH // LANES):
            q_ref[:, j * LANES:(j + 1) * LANES] = (
                _rotate(proj[:, j * LANES:(j + 1) * LANES], cos, lo, hi) * Q_SCALE).astype(BF16)
        k_ref[...] = _rotate(proj[:, ATTN_WIDTH:ATTN_WIDTH + KV_WIDTH], cos, lo, hi).astype(BF16)
        v_ref[...] = proj[:, ATTN_WIDTH + KV_WIDTH:ATTN_WIDTH + 2 * KV_WIDTH].astype(BF16)
        u_ref[...] = proj[:, ATTN_WIDTH + 2 * KV_WIDTH:]

    row = lambda w: pl.BlockSpec((ts, w), lambda i: (i, 0))
    return _call(
        body, [x, vec, w_in_t, b_in, *rot], name=name, grid=(s // ts,),
        in_specs=[row(D_MODEL), _resident((8, D_MODEL)), _resident((IN_WIDTH, D_MODEL)), _resident((8, IN_WIDTH)),
                  row(LANES), row(LANES)],
        out_specs=[row(ATTN_WIDTH), row(KV_WIDTH), row(KV_WIDTH), row(POOL_WIDTH)],
        out_shape=[jax.ShapeDtypeStruct((s, ATTN_WIDTH), BF16), jax.ShapeDtypeStruct((s, KV_WIDTH), BF16),
                   jax.ShapeDtypeStruct((s, KV_WIDTH), BF16), jax.ShapeDtypeStruct((s, POOL_WIDTH), F32)],
        exchanges=exchanges)


def _band_mask_t(first_block):
    key = lax.broadcasted_iota(jnp.int32, (2 * BLOCK, 2 * BLOCK), 0)
    qry = lax.broadcasted_iota(jnp.int32, (2 * BLOCK, 2 * BLOCK), 1) & (BLOCK - 1)
    d = key - qry
    return (d >= 1) & (d <= BLOCK) & ((key >= BLOCK) | jnp.logical_not(first_block))


def _placed(both, kh, b):
    lane = lax.broadcasted_iota(jnp.int32, (1, LANES), 1)
    src = both if kh == b else pltpu.roll(both, HEAD_DIM, 1)
    return jnp.where((lane >= b * HEAD_DIM) & (lane < (b + 1) * HEAD_DIM), src, jnp.zeros_like(src))


def _pair(ref, rows, kh, r):
    j = 2 * kh + r
    return ref[rows, j * LANES:(j + 1) * LANES]


def _attn_specs(s):
    qb = min(ATTN_BLOCKS, s // BLOCK)
    cur = lambda w: pl.BlockSpec((qb * BLOCK, w), lambda i: (i, 0))
    prev = lambda w: pl.BlockSpec((BLOCK, w), lambda i: (jnp.maximum(i * qb - 1, 0), 0))
    return qb, cur, prev, pl.BlockSpec((N_HEADS, qb * BLOCK), lambda i: (0, i))


def _kv_window(prev_ref, cur_ref, jb):
    before = prev_ref[...] if jb == 0 else cur_ref[(jb - 1) * BLOCK:jb * BLOCK, :]
    return jnp.concatenate([before, cur_ref[jb * BLOCK:(jb + 1) * BLOCK, :]], axis=0)


def _sink_row(sink_ref, kh, b):
    lane = lax.broadcasted_iota(jnp.int32, (1, 2 * BLOCK), 1)
    return jnp.where(lane < BLOCK, sink_ref[GROUP * kh + b], sink_ref[GROUP * kh + 2 + b])


def _attn_fwd_t(sinks, q, k, v, name, exchanges=()):
    s = q.shape[0]
    qb, cur, prev, lse_spec = _attn_specs(s)

    def block(jb, sink_ref, q_ref, kc_ref, kp_ref, vc_ref, vp_ref, o_ref, l_ref):
        rows = slice(jb * BLOCK, (jb + 1) * BLOCK)
        valid = _band_mask_t(pl.program_id(0) == 0 if jb == 0 else False)
        k_both, v_both = _kv_window(kp_ref, kc_ref, jb), _kv_window(vp_ref, vc_ref, jb)
        for kh in range(N_KV_HEADS):
            qg = jnp.concatenate([_pair(q_ref, rows, kh, 0), _pair(q_ref, rows, kh, 1)], axis=0)
            keys = jnp.concatenate([_placed(k_both, kh, 0), _placed(k_both, kh, 1)], axis=0)
            st = _dot(keys, qg, NT)
            out = jnp.zeros((2 * BLOCK, LANES), F32)
            for b in range(2):
                sc = jnp.where(valid, st[b * 2 * BLOCK:(b + 1) * 2 * BLOCK, :], NEG_INF)
                sink = _sink_row(sink_ref, kh, b)
                m = jnp.maximum(jnp.max(sc, axis=0, keepdims=True), sink)
                p = jnp.exp(sc - m)
                den = jnp.sum(p, axis=0, keepdims=True) + jnp.exp(sink - m)
                out = out + _dot((p * (1.0 / den)).astype(BF16), _placed(v_both, kh, b), TN)
                lse = m + jnp.log(den)
                for r in range(2):
                    h = GROUP * kh + 2 * r + b
                    l_ref[h:h + 1, rows] = lse[:, r * BLOCK:(r + 1) * BLOCK]
            for r in range(2):
                j = 2 * kh + r
                o_ref[rows, j * LANES:(j + 1) * LANES] = out[r * BLOCK:(r + 1) * BLOCK, :].astype(BF16)

    def body(*refs):
        for jb in range(qb):
            block(jb, *refs)

    return _call(
        body, [sinks, q, k, k, v, v], name=name, grid=(s // (qb * BLOCK),),
        in_specs=[pl.BlockSpec(memory_space=pltpu.SMEM), cur(ATTN_WIDTH), cur(KV_WIDTH), prev(KV_WIDTH),
                  cur(KV_WIDTH), prev(KV_WIDTH)],
        out_specs=[cur(ATTN_WIDTH), lse_spec],
        out_shape=[jax.ShapeDtypeStruct((s, ATTN_WIDTH), BF16), jax.ShapeDtypeStruct((N_HEADS, s), F32)],
        exchanges=exchanges)


def _attn_bwd_t(sinks, q, do, lse, k, v, name, exchanges=()):
    s = q.shape[0]
    qb, cur, prev, lse_spec = _attn_specs(s)

    def block(jb, sink_ref, q_ref, do_ref, l_ref, kc_ref, kp_ref, vc_ref, vp_ref,
              dq_ref, dkc_ref, dkp_ref, dvc_ref, dvp_ref, dsink_ref):
        rows = slice(jb * BLOCK, (jb + 1) * BLOCK)
        valid = _band_mask_t(pl.program_id(0) == 0 if jb == 0 else False)
        k_both, v_both = _kv_window(kp_ref, kc_ref, jb), _kv_window(vp_ref, vc_ref, jb)
        lane = lax.broadcasted_iota(jnp.int32, (1, LANES), 1)
        col = lax.broadcasted_iota(jnp.int32, (1, 2 * BLOCK), 1)
        dk_heads, dv_heads = [], []
        for kh in range(N_KV_HEADS):
            qg = jnp.concatenate([_pair(q_ref, rows, kh, 0), _pair(q_ref, rows, kh, 1)], axis=0)
            dog = jnp.concatenate([_pair(do_ref, rows, kh, 0), _pair(do_ref, rows, kh, 1)], axis=0)
            k_placed = [_placed(k_both, kh, b) for b in range(2)]
            v_placed = [_placed(v_both, kh, b) for b in range(2)]
            st = _dot(jnp.concatenate(k_placed, axis=0), qg, NT)
            dpt = _dot(jnp.concatenate(v_placed, axis=0), dog, NT)
            dqg = jnp.zeros((2 * BLOCK, LANES), F32)
            dk_b, dv_b = [], []
            for b in range(2):
                part = slice(b * 2 * BLOCK, (b + 1) * 2 * BLOCK)
                heads = [GROUP * kh + 2 * r + b for r in range(2)]
                lse_row = jnp.concatenate([l_ref[h:h + 1, rows] for h in heads], axis=1)
                p = jnp.where(valid, jnp.exp(st[part, :] - lse_row), 0.0)
                dp = dpt[part, :]
                delta = jnp.sum(p * dp, axis=0, keepdims=True)
                sink_pull = jnp.exp(_sink_row(sink_ref, kh, b) - lse_row) * delta
                dsink_ref[heads[0]:heads[0] + 1, :] += -jnp.sum(jnp.where(col < BLOCK, sink_pull, 0.0))
                dsink_ref[heads[1]:heads[1] + 1, :] += -jnp.sum(jnp.where(col < BLOCK, 0.0, sink_pull))
                ds = (p * (dp - delta)).astype(BF16)
                dqg = dqg + _dot(ds, k_placed[b], TN)
                dk_b.append(_dot(ds, qg))
                dv_b.append(_dot(p.astype(BF16), dog))
            for parts, total in ((dk_b, dk_heads), (dv_b, dv_heads)):
                kept = jnp.where(lane < HEAD_DIM, parts[0], parts[1])
                total.append(kept + pltpu.roll(kept, HEAD_DIM, 1))
            for r in range(2):
                j = 2 * kh + r
                dq_ref[rows, j * LANES:(j + 1) * LANES] = dqg[r * BLOCK:(r + 1) * BLOCK, :] * Q_SCALE
        dk = jnp.where(lane < HEAD_DIM, dk_heads[0], dk_heads[1])
        dv = jnp.where(lane < HEAD_DIM, dv_heads[0], dv_heads[1])
        dkp_ref[rows, :] = dk[0:BLOCK, :]
        dkc_ref[rows, :] = dk[BLOCK:, :]
        dvp_ref[rows, :] = dv[0:BLOCK, :]
        dvc_ref[rows, :] = dv[BLOCK:, :]

    def body(*refs):
        @pl.when(pl.program_id(0) == 0)
        def _():
            refs[-1][...] = jnp.zeros_like(refs[-1])

        for jb in range(qb):
            block(jb, *refs)

    kv = jax.ShapeDtypeStruct((s, KV_WIDTH), F32)
    return _call(
        body, [sinks, q, do, lse, k, k, v, v], name=name, grid=(s // (qb * BLOCK),),
        in_specs=[pl.BlockSpec(memory_space=pltpu.SMEM), cur(ATTN_WIDTH), cur(ATTN_WIDTH), lse_spec,
                  cur(KV_WIDTH), prev(KV_WIDTH), cur(KV_WIDTH), prev(KV_WIDTH)],
        out_specs=[cur(ATTN_WIDTH), cur(KV_WIDTH), cur(KV_WIDTH), cur(KV_WIDTH), cur(KV_WIDTH),
                   pl.BlockSpec((8, LANES), lambda i: (0, 0))],
        out_shape=[jax.ShapeDtypeStruct((s, ATTN_WIDTH), F32), kv, kv, kv, kv, jax.ShapeDtypeStruct((8, LANES), F32)],
        exchanges=exchanges)


def _pool_counts(tile, ts):
    t = (tile * ts + lax.broadcasted_iota(jnp.int32, (ts, 1), 0) + 1).astype(F32)
    return [jnp.minimum(t, float(w)) for w in POOL_WINDOWS]


def _pooled(u, halo, counts, gi):
    cols = slice(gi * POOL_GROUP_WIDTH, (gi + 1) * POOL_GROUP_WIDTH)
    acc = jnp.concatenate([halo[:, cols], u[:, cols]], axis=0)
    shift = 1
    while shift < POOL_WINDOWS[gi]:
        acc = acc + pltpu.roll(acc, shift, 0)
        shift *= 2
    return acc[POOL_HALO:, :] / counts[gi] - u[:, cols]


def _fwd_out(ao, u, pool_w, psc, w_out, x, vec, name, exchanges=()):
    s = x.shape[0]
    ts = _seq_tile(s, 2)
    hb = ts // POOL_HALO

    def body(ao_ref, u_ref, uh_ref, pw_ref, psc_ref, w_ref, x_ref, vec_ref, x1_ref, mix_ref):
        i = pl.program_id(0)
        u_t = u_ref[...]
        halo = jnp.where(i > 0, uh_ref[...], 0.0)
        counts = _pool_counts(i, ts)
        cat = [ao_ref[...]]
        for gi in range(len(POOL_WINDOWS)):
            cols = slice(gi * POOL_GROUP_WIDTH, (gi + 1) * POOL_GROUP_WIDTH)
            og = _dot(_pooled(u_t, halo, counts, gi).astype(BF16), pw_ref[gi]) * psc_ref[0:1, cols]
            cat.append(og.astype(BF16))
        mix = _dot(jnp.concatenate(cat, axis=1), w_ref[...])
        mix_ref[...] = mix
        x1_ref[...] = x_ref[...] + vec_ref[0:1, :] * (mix * _rsqrt_ms(mix) * vec_ref[1:2, :])

    row = lambda w: pl.BlockSpec((ts, w), lambda i: (i, 0))
    return _call(
        body, [ao, u, u, pool_w, psc, w_out, x, vec], name=name, grid=(s // ts,),
        in_specs=[row(ATTN_WIDTH), row(POOL_WIDTH),
                  pl.BlockSpec((POOL_HALO, POOL_WIDTH), lambda i: (jnp.maximum(i * hb - 1, 0), 0)),
                  _resident(pool_w.shape), _resident((8, POOL_WIDTH)), _resident((D_MODEL, D_MODEL)),
                  row(D_MODEL), _resident((8, D_MODEL))],
        out_specs=[row(D_MODEL), row(D_MODEL)],
        out_shape=[jax.ShapeDtypeStruct((s, D_MODEL), F32), jax.ShapeDtypeStruct((s, D_MODEL), F32)],
        exchanges=exchanges)


def _ffn_fwd(x1, vec, w_gate_t, w_up_t, w_down, name, exchanges=(), target=None):
    s = x1.shape[0]
    ts = _seq_tile(s)

    def body(x_ref, *refs):
        if target is None:
            vec_ref, wg_ref, wu_ref, wd_ref, x2_ref, f_ref, gt_ref, up_ref, h2_ref = refs
        else:
            t_ref, vec_ref, wg_ref, wu_ref, wd_ref, x2_ref, f_ref, gt_ref, up_ref, h2_ref, loss_ref = refs
        xt = x_ref[...]
        h2 = (xt * _rsqrt_ms(xt) * vec_ref[0:1, :] * vec_ref[1:2, :] + vec_ref[2:3, :]).astype(BF16)
        h2_ref[...] = h2
        f = jnp.zeros((ts, D_MODEL), F32)
        for c in range(D_FF // FF_CHUNK):
            cols = slice(c * FF_CHUNK, (c + 1) * FF_CHUNK)
            g = _dot(h2, wg_ref[cols, :], NT)
            up = _dot(h2, wu_ref[cols, :], NT)
            act = ((g * jax.nn.sigmoid(g)) * up).astype(BF16)
            gt_ref[:, cols] = g.astype(BF16)
            up_ref[:, cols] = up.astype(BF16)
            f = f + _dot(act, wd_ref[cols, :])
        f_ref[...] = f
        x2 = xt + vec_ref[3:4, :] * (f * _rsqrt_ms(f) * vec_ref[4:5, :])
        if target is None:
            x2_ref[...] = x2
        else:
            @pl.when(pl.program_id(0) == 0)
            def _():
                loss_ref[...] = jnp.zeros_like(loss_ref)

            diff = x2 - t_ref[...]
            x2_ref[...] = diff / D_MODEL
            loss_ref[...] += 0.5 * jnp.sum(jnp.mean(diff * diff, axis=-1, keepdims=True))

    row = lambda w: pl.BlockSpec((ts, w), lambda i: (i, 0))
    wide = jax.ShapeDtypeStruct((s, D_FF), BF16)
    with_loss = target is not None
    return _call(
        body, [x1] + ([target] if with_loss else []) + [vec, w_gate_t, w_up_t, w_down], name=name, grid=(s // ts,),
        in_specs=[row(D_MODEL)] * (2 if with_loss else 1) + [
            _resident((8, D_MODEL)), _resident((D_FF, D_MODEL)), _resident((D_FF, D_MODEL)), _resident((D_FF, D_MODEL))],
        out_specs=[row(D_MODEL), row(D_MODEL), row(D_FF), row(D_FF), row(D_MODEL)] + (
            [pl.BlockSpec((8, LANES), lambda i: (0, 0))] if with_loss else []),
        out_shape=[jax.ShapeDtypeStruct((s, D_MODEL), F32), jax.ShapeDtypeStruct((s, D_MODEL), F32), wide, wide,
                   jax.ShapeDtypeStruct((s, D_MODEL), BF16)] + (
            [jax.ShapeDtypeStruct((8, LANES), F32)] if with_loss else []),
        exchanges=exchanges)


def _norm_bwd(dy_hat, x_hat, r):
    return r * (dy_hat - x_hat * jnp.mean(dy_hat * x_hat, axis=-1, keepdims=True))


def _accumulate_rows(ref, rows):
    for j, val in enumerate(rows):
        ref[j:j + 1, :] += val


def _ffn_bwd_act(dx2, f, gt, up, vec, w_down, name, exchanges=()):
    s = dx2.shape[0]
    ts = _seq_tile(s)

    def body(dx2_ref, f_ref, gt_ref, up_ref, vec_ref, wd_ref, dgt_ref, dup_ref, dwd_ref, cs_ref):
        @pl.when(pl.program_id(0) == 0)
        def _():
            cs_ref[...] = jnp.zeros_like(cs_ref)
            dwd_ref[...] = jnp.zeros_like(dwd_ref)

        dx2_t, f_t = dx2_ref[...], f_ref[...]
        gate, g_post = vec_ref[3:4, :], vec_ref[4:5, :]
        rf = _rsqrt_ms(f_t)
        f_hat = f_t * rf
        df = _norm_bwd(dx2_t * gate * g_post, f_hat, rf).astype(BF16)
        _accumulate_rows(cs_ref, [_colsum(dx2_t * (f_hat * g_post)), _colsum(dx2_t * gate * f_hat)])
        for c in range(D_FF // FF_CHUNK):
            cols = slice(c * FF_CHUNK, (c + 1) * FF_CHUNK)
            dact = _dot(df, wd_ref[cols, :], NT)
            g, u_t = gt_ref[:, cols].astype(F32), up_ref[:, cols].astype(F32)
            sg = 0.5 * jnp.tanh(0.5 * g) + 0.5
            silu = g * sg
            dgt_ref[:, cols] = (dact * u_t * (sg * (1.0 + g * (1.0 - sg)))).astype(BF16)
            dup_ref[:, cols] = (dact * silu).astype(BF16)
            dwd_ref[cols, :] += _dot((silu * u_t).astype(BF16), df, TN)

    row = lambda w: pl.BlockSpec((ts, w), lambda i: (i, 0))
    wide = jax.ShapeDtypeStruct((s, D_FF), BF16)
    return _call(
        body, [dx2, f, gt, up, vec, w_down], name=name, grid=(s // ts,),
        in_specs=[row(D_MODEL), row(D_MODEL), row(D_FF), row(D_FF), _resident((8, D_MODEL)), _resident((D_FF, D_MODEL))],
        out_specs=[row(D_FF), row(D_FF), _resident((D_FF, D_MODEL)), pl.BlockSpec((8, D_MODEL), lambda i: (0, 0))],
        out_shape=[wide, wide, jax.ShapeDtypeStruct((D_FF, D_MODEL), F32), jax.ShapeDtypeStruct((8, D_MODEL), F32)],
        exchanges=exchanges)


def _ffn_bwd_in(dx2, x1, dgt, dup, vec, w_gate_t, w_up_t, name, exchanges=()):
    s = x1.shape[0]
    ts = _seq_tile(s)

    def body(dx2_ref, x1_ref, dgt_ref, dup_ref, vec_ref, wg_ref, wu_ref, dx1_ref, cs_ref):
        @pl.when(pl.program_id(0) == 0)
        def _():
            cs_ref[...] = jnp.zeros_like(cs_ref)

        g_pre, one_scale = vec_ref[0:1, :], vec_ref[1:2, :]
        dh2 = _dot(dgt_ref[...], wg_ref[...]) + _dot(dup_ref[...], wu_ref[...])
        x1_t = x1_ref[...]
        r1 = _rsqrt_ms(x1_t)
        x_hat = x1_t * r1
        dx1_ref[...] = dx2_ref[...] + _norm_bwd(dh2 * g_pre * one_scale, x_hat, r1)
        _accumulate_rows(cs_ref, [_colsum(dh2), _colsum(dh2 * (x_hat * g_pre)), _colsum(dh2 * one_scale * x_hat)])

    row = lambda w: pl.BlockSpec((ts, w), lambda i: (i, 0))
    return _call(
        body, [dx2, x1, dgt, dup, vec, w_gate_t, w_up_t], name=name, grid=(s // ts,),
        in_specs=[row(D_MODEL), row(D_MODEL), row(D_FF), row(D_FF), _resident((8, D_MODEL)),
                  _resident((D_FF, D_MODEL)), _resident((D_FF, D_MODEL))],
        out_specs=[row(D_MODEL), pl.BlockSpec((8, D_MODEL), lambda i: (0, 0))],
        out_shape=[jax.ShapeDtypeStruct((s, D_MODEL), F32), jax.ShapeDtypeStruct((8, D_MODEL), F32)],
        exchanges=exchanges)


def _weight_grad(a, b, name, exchanges=()):
    s, m = a.shape
    n = b.shape[1]
    tk = min(s, 2 * SEQ_TILE)
    steps = s // tk

    def body(a_ref, b_ref, o_ref, acc_ref):
        i = pl.program_id(0)

        @pl.when(i == 0)
        def _():
            acc_ref[...] = jnp.zeros_like(acc_ref)

        b_t = b_ref[...]
        for c in range(m // FF_CHUNK):
            rows = slice(c * FF_CHUNK, (c + 1) * FF_CHUNK)
            acc_ref[rows, :] += _dot(a_ref[:, rows], b_t, TN)

        @pl.when(i == steps - 1)
        def _():
            o_ref[...] = acc_ref[...].astype(BF16)

    (grad,), ex_outs = _call(
        body, [a, b], name=name, grid=(steps,),
        in_specs=[pl.BlockSpec((tk, m), lambda i: (i, 0)), pl.BlockSpec((tk, n), lambda i: (i, 0))],
        out_specs=[_resident((m, n))],
        out_shape=[jax.ShapeDtypeStruct((m, n), BF16)],
        scratch_shapes=[pltpu.VMEM((m, n), F32)],
        exchanges=exchanges)
    return grad, ex_outs


def _out_bwd(dx1, mix, ao, u, pool_w, psc, w_out, vec, name, exchanges=()):
    s = dx1.shape[0]
    ts = _seq_tile(s)
    nt = s // ts
    hb = ts // POOL_HALO
    ng = len(POOL_WINDOWS)

    def body(dx1_ref, mix_ref, ao_ref, u_ref, uh_ref, pw_ref, psc_ref, w_ref, vec_ref,
             do_ref, du_ref, dw_out_ref, cs_ref, dpw_ref, dpsc_ref, carry_ref, dw_ref):
        i = pl.program_id(0)
        tile = nt - 1 - i

        @pl.when(i == 0)
        def _():
            dw_ref[...] = jnp.zeros_like(dw_ref)
            cs_ref[...] = jnp.zeros_like(cs_ref)
            dpw_ref[...] = jnp.zeros_like(dpw_ref)
            dpsc_ref[...] = jnp.zeros_like(dpsc_ref)
            carry_ref[...] = jnp.zeros_like(carry_ref)

        dx1_t, mix_t = dx1_ref[...], mix_ref[...]
        gate, g_post = vec_ref[0:1, :], vec_ref[1:2, :]
        rm = _rsqrt_ms(mix_t)
        m_hat = mix_t * rm
        dmix = _norm_bwd(dx1_t * gate * g_post, m_hat, rm).astype(BF16)
        _accumulate_rows(cs_ref, [_colsum(dx1_t * (m_hat * g_post)), _colsum(dx1_t * gate * m_hat)])
        d_cat = _dot(dmix, w_ref[...], NT)
        do_ref[...] = d_cat[:, 0:ATTN_WIDTH].astype(BF16)

        u_t = u_ref[...]
        halo = jnp.where(tile > 0, uh_ref[...], 0.0)
        counts = _pool_counts(tile, ts)
        cat = [ao_ref[...]]
        for gi in range(ng):
            cols = slice(gi * POOL_GROUP_WIDTH, (gi + 1) * POOL_GROUP_WIDTH)
            scale = psc_ref[0:1, cols]
            pooled = _pooled(u_t, halo, counts, gi).astype(BF16)
            og = _dot(pooled, pw_ref[gi])
            cat.append((og * scale).astype(BF16))
            d_out = d_cat[:, ATTN_WIDTH + gi * POOL_GROUP_WIDTH:ATTN_WIDTH + (gi + 1) * POOL_GROUP_WIDTH]
            dpsc_ref[0:1, cols] += _colsum(d_out * og)
            d_og = (d_out * scale).astype(BF16)
            dpw_ref[gi] += _dot(pooled, d_og, TN)
            d_pooled = _dot(d_og, pw_ref[gi], NT)
            spread = d_pooled / counts[gi]
            acc = jnp.concatenate([spread, carry_ref[:, cols]], axis=0)
            shift = 1
            while shift < POOL_WINDOWS[gi]:
                acc = acc + pltpu.roll(acc, ts + POOL_HALO - shift, 0)
                shift *= 2
            du_ref[:, cols] = acc[0:ts, :] - d_pooled
            carry_ref[:, cols] = spread[0:POOL_HALO, :]
        dw_ref[...] += _dot(jnp.concatenate(cat, axis=1), dmix, TN)

        @pl.when(i == nt - 1)
        def _():
            dw_out_ref[...] = dw_ref[...].astype(BF16)

    row = lambda w: pl.BlockSpec((ts, w), lambda i: (nt - 1 - i, 0))
    fixed = lambda shape: pl.BlockSpec(shape, lambda i: (0,) * len(shape))
    return _call(
        body, [dx1, mix, ao, u, u, pool_w, psc, w_out, vec], name=name, grid=(nt,),
        in_specs=[row(D_MODEL), row(D_MODEL), row(ATTN_WIDTH), row(POOL_WIDTH),
                  pl.BlockSpec((POOL_HALO, POOL_WIDTH), lambda i: (jnp.maximum((nt - 1 - i) * hb - 1, 0), 0)),
                  _resident(pool_w.shape), _resident((8, POOL_WIDTH)), _resident((D_MODEL, D_MODEL)),
                  _resident((8, D_MODEL))],
        out_specs=[row(ATTN_WIDTH), row(POOL_WIDTH), _resident((D_MODEL, D_MODEL)), fixed((8, D_MODEL)),
                   fixed(pool_w.shape), fixed((8, POOL_WIDTH))],
        out_shape=[jax.ShapeDtypeStruct((s, ATTN_WIDTH), BF16), jax.ShapeDtypeStruct((s, POOL_WIDTH), F32),
                   jax.ShapeDtypeStruct((D_MODEL, D_MODEL), BF16),
                   jax.ShapeDtypeStruct((8, D_MODEL), F32), jax.ShapeDtypeStruct(pool_w.shape, F32),
                   jax.ShapeDtypeStruct((8, POOL_WIDTH), F32)],
        scratch_shapes=[pltpu.VMEM((POOL_HALO, POOL_WIDTH), F32), pltpu.VMEM((D_MODEL, D_MODEL), F32)],
        exchanges=exchanges)


def _in_bwd(dq, dkc, dkp, dvc, dvp, du, x, dx1, vec, w_in_t, rot, name, exchanges=()):
    s = x.shape[0]
    ts = _seq_tile(s)
    nt = s // ts
    bpt = ts // BLOCK

    def body(dq_ref, dkc_ref, dkp_ref, dkn_ref, dvc_ref, dvp_ref, dvn_ref, du_ref, x_ref, dx1_ref, vec_ref, w_ref,
             cos_ref, sin_ref, dx_ref, dw_out_ref, cs_ref, db_ref, dw_ref):
        i = pl.program_id(0)

        @pl.when(i == 0)
        def _():
            dw_ref[...] = jnp.zeros_like(dw_ref)
            cs_ref[...] = jnp.zeros_like(cs_ref)
            db_ref[...] = jnp.zeros_like(db_ref)

        cos, lo, hi = _rotary_terms(cos_ref[...], sin_ref[...])

        def with_next_block(cur_ref, prev_ref, next_ref):
            nxt = jnp.where(i < nt - 1, next_ref[...], 0.0)
            later = nxt if bpt == 1 else jnp.concatenate([prev_ref[BLOCK:, :], nxt], axis=0)
            return cur_ref[...] + later

        pieces = [_rotate_bwd(dq_ref[:, j * LANES:(j + 1) * LANES], cos, lo, hi) for j in range(ATTN_WIDTH // LANES)]
        pieces.append(_rotate_bwd(with_next_block(dkc_ref, dkp_ref, dkn_ref), cos, lo, hi))
        pieces.append(with_next_block(dvc_ref, dvp_ref, dvn_ref))
        pieces.append(du_ref[...])
        dproj = jnp.concatenate(pieces, axis=1)
        db_ref[0:1, :] += _colsum(dproj)
        dproj_b = dproj.astype(BF16)
        dh = _dot(dproj_b, w_ref[...])

        xt = x_ref[...]
        g_pre, one_scale = vec_ref[0:1, :], vec_ref[1:2, :]
        r = _rsqrt_ms(xt)
        x_hat = xt * r
        dw_ref[...] += _dot(dproj_b, (x_hat * g_pre * one_scale + vec_ref[2:3, :]).astype(BF16), TN)
        dx_ref[...] = dx1_ref[...] + _norm_bwd(dh * g_pre * one_scale, x_hat, r)
        _accumulate_rows(cs_ref, [_colsum(dh), _colsum(dh * (x_hat * g_pre)), _colsum(dh * one_scale * x_hat)])

        @pl.when(i == nt - 1)
        def _():
            dw_out_ref[...] = dw_ref[...].astype(BF16)

    row = lambda w: pl.BlockSpec((ts, w), lambda i: (i, 0))
    nxt = pl.BlockSpec((BLOCK, KV_WIDTH), lambda i: (jnp.minimum((i + 1) * bpt, s // BLOCK - 1), 0))
    fixed = lambda shape: pl.BlockSpec(shape, lambda i: (0,) * len(shape))
    return _call(
        body, [dq, dkc, dkp, dkp, dvc, dvp, dvp, du, x, dx1, vec, w_in_t, *rot], name=name, grid=(nt,),
        in_specs=[row(ATTN_WIDTH), row(KV_WIDTH), row(KV_WIDTH), nxt, row(KV_WIDTH), row(KV_WIDTH), nxt,
                  row(POOL_WIDTH), row(D_MODEL), row(D_MODEL), _resident((8, D_MODEL)), _resident((IN_WIDTH, D_MODEL)),
                  row(LANES), row(LANES)],
        out_specs=[row(D_MODEL), _resident((IN_WIDTH, D_MODEL)), fixed((8, D_MODEL)), fixed((8, IN_WIDTH))],
        out_shape=[jax.ShapeDtypeStruct((s, D_MODEL), F32), jax.ShapeDtypeStruct((IN_WIDTH, D_MODEL), BF16),
                   jax.ShapeDtypeStruct((8, D_MODEL), F32), jax.ShapeDtypeStruct((8, IN_WIDTH), F32)],
        scratch_shapes=[pltpu.VMEM((IN_WIDTH, D_MODEL), F32)],
        exchanges=exchanges)


def _row_tile(rows):
    for t in (512, 352, 256, 176, 160, 128, 64, 32, 16, 8):
        if rows % t == 0:
            return t
    raise ValueError(f"no row tile for {rows} rows")


def _sum_parts(parts, first, name):
    n, r, c = parts.shape
    tr = _row_tile(r)

    def body(p_ref, o_ref):
        acc = p_ref[first].astype(F32)
        for k in range(n):
            if k != first:
                acc = acc + p_ref[k].astype(F32)
        o_ref[...] = acc

    return pl.pallas_call(
        body, name=name, grid=(r // tr,),
        in_specs=[pl.BlockSpec((n, tr, c), lambda i: (0, i, 0))],
        out_specs=pl.BlockSpec((tr, c), lambda i: (i, 0)),
        out_shape=jax.ShapeDtypeStruct((r, c), F32),
        compiler_params=_cparams(1),
    )(parts)


def _adam_step(w, g, m, v):
    m_new = ADAM_B1 * m + (1.0 - ADAM_B1) * g
    v_new = ADAM_B2 * v + (1.0 - ADAM_B2) * (g * g)
    m_hat = m_new / (1.0 - ADAM_B1 ** ADAM_STEP)
    v_hat = v_new / (1.0 - ADAM_B2 ** ADAM_STEP)
    return -ADAM_LR * (m_hat / (jnp.sqrt(v_hat) + ADAM_EPS) + ADAM_WD * w), m_new, v_new


def _sum_adamw(parts, w, m, v, name):
    depth, r, c = w.shape
    tr = _row_tile(r)

    def body(*refs):
        p_refs = refs[:depth]
        w_ref, m_ref, v_ref, g_ref, d_ref, nm_ref, nv_ref = refs[depth:]
        g = None
        for layer, p_ref in enumerate(p_refs):
            acc = p_ref[_Scatter.OWN].astype(F32)
            for k in range(N_DEV - 1):
                acc = acc + p_ref[k].astype(F32)
            g = acc if g is None else jnp.where(pl.program_id(0) == layer, acc, g)
        g_ref[0] = g
        d_ref[0], nm_ref[0], nv_ref[0] = _adam_step(w_ref[0], g, m_ref[0], v_ref[0])

    spec = pl.BlockSpec((1, tr, c), lambda l, i: (l, i, 0))
    shape = jax.ShapeDtypeStruct((depth, r, c), F32)
    return pl.pallas_call(
        body, name=name, grid=(depth, r // tr),
        in_specs=[pl.BlockSpec((N_DEV, tr, c), lambda l, i: (0, i, 0))] * depth + [spec] * 3,
        out_specs=[spec] * 4, out_shape=[shape] * 4, compiler_params=_cparams(2),
    )(*parts, w, m, v)


def _adamw(w, g, m, v, name):
    r, c = w.shape
    tr = _row_tile(r) if r % 8 == 0 else r

    def body(w_ref, g_ref, m_ref, v_ref, d_ref, nm_ref, nv_ref):
        d_ref[...], nm_ref[...], nv_ref[...] = _adam_step(w_ref[...], g_ref[...], m_ref[...], v_ref[...])

    spec = pl.BlockSpec((tr, c), lambda i: (i, 0))
    shape = jax.ShapeDtypeStruct((r, c), F32)
    return pl.pallas_call(
        body, name=name, grid=(r // tr,), in_specs=[spec] * 4, out_specs=[spec] * 3, out_shape=[shape] * 3,
        compiler_params=_cparams(1),
    )(w, g, m, v)


LAYER_ROWS = 80


def _pack_layer(dmod, gains, b_in, pool_scale, sinks, pool_w, loss):
    misc = jnp.concatenate([pool_scale, sinks, jnp.reshape(loss, (1,)),
                            jnp.zeros((D_MODEL - POOL_WIDTH - N_HEADS - 1,), F32)])
    parts = [dmod.reshape(N_MOD, D_MODEL), gains, jnp.pad(b_in, (0, 2 * D_MODEL - IN_WIDTH)).reshape(2, D_MODEL),
             misc.reshape(1, D_MODEL), pool_w.reshape(-1, D_MODEL)]
    packed = jnp.concatenate(parts, axis=0)
    return jnp.pad(packed, ((0, LAYER_ROWS - packed.shape[0]), (0, 0)))


def _unpack_layer(p):
    r = N_MOD + 4
    b_in = p[r:r + 2].reshape(2 * D_MODEL)[:IN_WIDTH]
    pool_w = p[r + 3:r + 3 + 64].reshape(len(POOL_WINDOWS), POOL_GROUP_WIDTH, POOL_GROUP_WIDTH)
    return (p[0:N_MOD].reshape(N_MOD * D_MODEL), p[N_MOD:r], b_in, p[r + 2, :POOL_WIDTH],
            p[r + 2, POOL_WIDTH:POOL_WIDTH + N_HEADS], pool_w, p[r + 2, POOL_WIDTH + N_HEADS])


def _rotary_tables(positions):
    inv_freq = ROPE_THETA ** (-jnp.arange(0, ROT_DIM, 2, dtype=F32) / ROT_DIM)
    head_freq = jnp.concatenate([inv_freq, inv_freq, jnp.zeros((HEAD_DIM - ROT_DIM,), F32)])
    lane_freq = jnp.concatenate([head_freq, head_freq])
    ang = positions.reshape(-1).astype(F32)[:, None] * lane_freq[None, :]
    return jnp.cos(ang), jnp.sin(ang)


def kernel(x, c, positions, ada_w, ada_b, w_in, b_in, sinks, pool_w, pool_scale, w_out, w_gate, w_up, w_down, g_pre_mix, g_post_mix, g_pre_ffn, g_post_ffn, loss_target, m_ada_w, m_ada_b, m_w_in, m_b_in, m_sinks, m_pool_w, m_pool_scale, m_w_out, m_w_gate, m_w_up, m_w_down, m_g_pre_mix, m_g_post_mix, m_g_pre_ffn, m_g_post_ffn, v_ada_w, v_ada_b, v_w_in, v_b_in, v_sinks, v_pool_w, v_pool_scale, v_w_out, v_w_gate, v_w_up, v_w_down, v_g_pre_mix, v_g_post_mix, v_g_pre_ffn, v_g_post_ffn):
    me = _index(_mesh_place())
    x0 = x[0]
    target = loss_target[0]
    rot = _rotary_tables(positions)
    ada_cols = ada_w.shape[2]

    shards = []
    for l in range(DEPTH):
        shards += [w_in[l].T.astype(BF16), w_out[l].astype(BF16), w_gate[l].T.astype(BF16), w_up[l].T.astype(BF16),
                   w_down[l].astype(BF16)]
    whole = lambda gathered: [g.reshape(-1, D_MODEL) for g in gathered]
    full = [None] * (5 * DEPTH)
    first = _exchanges_alone([_Gather(shards[0:2] + [jnp.broadcast_to(c, (8, D_MODEL))])], "gather_first")[0]
    full[0:2] = whole(first[0:2])

    mod_part, c_act = _mod_fwd(first[2][:, 0, :], ada_w)
    mod_all = _allgather_vmem(mod_part.reshape(DEPTH * N_DEV, ada_cols), "gather_mod")
    mod_mine = lax.dynamic_slice_in_dim(mod_all.reshape(N_DEV, DEPTH, N_DEV, ada_cols), me, 1, axis=2)[:, :, 0, :]
    mod = jnp.transpose(mod_mine, (1, 0, 2)).reshape(DEPTH, N_MOD * D_MODEL) + ada_b
    mod = mod.reshape(DEPTH, N_MOD, D_MODEL)

    hosted = {("fwd_in", 0): (_Gather(shards[2:3]), [2]), ("attn_fwd", 0): (_Gather(shards[3:4]), [3]),
              ("fwd_out", 0): (_Gather(shards[4:5]), [4]),
              ("ffn_fwd", 0): (_Gather(shards[5:10]), [5, 6, 7, 8, 9])}

    def take(kind, l, ex_outs):
        if (kind, l) in hosted:
            for slot, g in zip(hosted[kind, l][1], whole(ex_outs[0])):
                full[slot] = g

    def beside(kind, l):
        return [hosted[kind, l][0]] if (kind, l) in hosted else []

    pool_w_b = pool_w.astype(BF16)

    saved = []
    xl = x0
    for l in range(DEPTH):
        vec_in = _rows(g_pre_mix[l], 1.0 + mod[l, 1], mod[l, 0], width=D_MODEL)
        vec_out = _rows(mod[l, 2], g_post_mix[l], width=D_MODEL)
        vec_ffn = _rows(g_pre_ffn[l], 1.0 + mod[l, 4], mod[l, 3], mod[l, 5], g_post_ffn[l], width=D_MODEL)
        psc = _rows(pool_scale[l], width=POOL_WIDTH)
        (q, k, v, u), ex = _fwd_in(xl, vec_in, full[5 * l], _rows(b_in[l], width=IN_WIDTH), rot, f"fwd_in_{l}",
                                   beside("fwd_in", l))
        take("fwd_in", l, ex)
        (ao, lse), ex = _attn_fwd_t(sinks[l], q, k, v, f"attn_fwd_{l}", beside("attn_fwd", l))
        take("attn_fwd", l, ex)
        (x1, mix), ex = _fwd_out(ao, u, pool_w_b[l], psc, full[5 * l + 1], xl, vec_out, f"fwd_out_{l}",
                                 beside("fwd_out", l))
        take("fwd_out", l, ex)
        (x2, f, gt, up, h2, *loss_part), ex = _ffn_fwd(
            x1, vec_ffn, full[5 * l + 2], full[5 * l + 3], full[5 * l + 4], f"ffn_fwd_{l}", beside("ffn_fwd", l),
            target=target if l == DEPTH - 1 else None)
        take("ffn_fwd", l, ex)
        saved.append((xl, q, k, v, u, ao, lse, x1, mix, f, gt, up, h2, vec_in, vec_out, vec_ffn, psc))
        xl = x2

    dx = xl
    my_loss = loss_part[0][0, 0]

    stacks = [None] * (5 * DEPTH)
    parts = [None] * (5 * DEPTH)
    small_all = [None] * DEPTH
    ready_stacks, ready_small = [], []

    def leaving(row_budget):
        exs, notes = [], []
        taken = []
        while ready_stacks and stacks[ready_stacks[0]].shape[1] <= row_budget:
            row_budget -= stacks[ready_stacks[0]].shape[1]
            taken.append(ready_stacks.pop(0))
        if taken:
            exs.append(_Scatter([stacks[a] for a in taken]))
            notes.append(("stacks", taken))
        if ready_small and row_budget >= 0:
            exs.append(_Gather([ready_small[0][1]]))
            notes.append(("small", ready_small[0][0]))
            ready_small.clear()
        return exs, notes

    def arrived(notes, ex_outs):
        for (kind, what), outs in zip(notes, ex_outs):
            if kind == "stacks":
                for a, o in zip(what, outs):
                    parts[a] = o
            else:
                small_all[what] = outs[0]

    def grad_ready(a, grad):
        stacks[a] = grad.reshape(N_DEV, -1, D_MODEL)
        ready_stacks.append(a)

    def hosting(row_budget, kernel_fn, *args):
        exs, notes = leaving(row_budget)
        outs, ex_outs = kernel_fn(*args, exs)
        arrived(notes, ex_outs)
        return outs

    for l in reversed(range(DEPTH)):
        w_in_t, w_out_f, w_gate_t, w_up_t, w_down_f = full[5 * l:5 * l + 5]
        xin, q, k, v, u, ao, lse, x1, mix, f, gt, up, h2, vec_in, vec_out, vec_ffn, psc = saved[l]
        last = l == 0
        dgt, dup, d_w_down, cs_a = hosting(600, _ffn_bwd_act, dx, f, gt, up, vec_ffn, w_down_f, f"ffn_bwd_act_{l}")
        grad_ready(5 * l + 4, d_w_down.astype(BF16))
        grad_ready(5 * l + 2, hosting(-1, _weight_grad, dgt, h2, f"grad_w_gate_{l}"))
        grad_ready(5 * l + 3, hosting(-1, _weight_grad, dup, h2, f"grad_w_up_{l}"))
        dx1, cs_f = hosting(400, _ffn_bwd_in, dx, x1, dgt, dup, vec_ffn, w_gate_t, w_up_t, f"ffn_bwd_in_{l}")
        do, du, d_w_out, cs_o, dpw, dpsc = hosting(360, _out_bwd, dx1, mix, ao, u, pool_w_b[l], psc, w_out_f, vec_out,
                                                   f"out_bwd_{l}")
        grad_ready(5 * l + 1, d_w_out)
        dq, dkc, dkp, dvc, dvp, dsk = hosting(480 if last else 400, _attn_bwd_t, sinks[l], q, do, lse, k, v,
                                              f"attn_bwd_{l}")
        dx, d_w_in_t, cs_i, db = hosting(-1 if last else 250, _in_bwd, dq, dkc, dkp, dvc, dvp, du, xin, dx1, vec_in,
                                         w_in_t, rot, f"in_bwd_{l}")
        grad_ready(5 * l, d_w_in_t)
        d_mod = jnp.concatenate([cs_i[0], cs_i[1], cs_o[0], cs_f[0], cs_f[1], cs_a[0]])
        d_gain = jnp.stack([cs_i[2], cs_o[1], cs_f[2], cs_a[1]])
        ready_small.append((l, _pack_layer(d_mod, d_gain, db[0], dpsc[0], dsk[:, 0], dpw,
                                           my_loss if l == DEPTH - 1 else jnp.zeros((), F32))))
    grad_x = dx[None]
    exs, notes = leaving(N_DEV * D_FF)
    arrived(notes, _exchanges_alone(exs, "exchange_last"))

    layer_sums = [_unpack_layer(_sum_parts(small_all[l], 0, f"sum_small_{l}")) for l in range(DEPTH)]
    g_ada_b, g_gains, g_b_in, g_pool_scale, g_sinks, g_pool_w = (
        jnp.stack([layer_sums[l][j] for l in range(DEPTH)], axis=1 if j == 1 else 0) for j in range(6))
    loss = layer_sums[DEPTH - 1][6]

    dmod_all = jnp.stack([small_all[l][:, 0:N_MOD, :].reshape(N_DEV, N_DEV, ada_cols) for l in range(DEPTH)])
    dmod_cols = lax.dynamic_slice_in_dim(dmod_all, me, 1, axis=2)[:, :, 0, :]
    g_ada_w = _ada_grad(c_act, dmod_cols)

    slot = {"w_in": 0, "w_out": 1, "w_gate": 2, "w_up": 3, "w_down": 4}
    held_transposed = ("w_in", "w_gate", "w_up")
    swap = lambda a: jnp.transpose(a, (0, 2, 1))

    def step(w, g, m, v, name):
        if name in slot:
            mine = [parts[5 * l + slot[name]] for l in range(DEPTH)]
            if name in held_transposed:
                return [swap(o) for o in _sum_adamw(mine, swap(w), swap(m), swap(v), f"adamw_{name}")]
            return _sum_adamw(mine, w, m, v, f"adamw_{name}")
        flat = lambda a: a.reshape(-1, a.shape[-1])
        return [g] + [o.reshape(w.shape) for o in _adamw(flat(w), flat(g), flat(m), flat(v), f"adamw_{name}")]

    in_order = [
        ("ada_w", ada_w, g_ada_w, m_ada_w, v_ada_w), ("ada_b", ada_b, g_ada_b, m_ada_b, v_ada_b),
        ("w_in", w_in, None, m_w_in, v_w_in), ("b_in", b_in, g_b_in, m_b_in, v_b_in),
        ("sinks", sinks, g_sinks, m_sinks, v_sinks), ("pool_w", pool_w, g_pool_w, m_pool_w, v_pool_w),
        ("pool_scale", pool_scale, g_pool_scale, m_pool_scale, v_pool_scale),
        ("w_out", w_out, None, m_w_out, v_w_out), ("w_gate", w_gate, None, m_w_gate, v_w_gate),
        ("w_up", w_up, None, m_w_up, v_w_up), ("w_down", w_down, None, m_w_down, v_w_down),
        ("g_pre_mix", g_pre_mix, g_gains[0], m_g_pre_mix, v_g_pre_mix),
        ("g_post_mix", g_post_mix, g_gains[1], m_g_post_mix, v_g_post_mix),
        ("g_pre_ffn", g_pre_ffn, g_gains[2], m_g_pre_ffn, v_g_pre_ffn),
        ("g_post_ffn", g_post_ffn, g_gains[3], m_g_post_ffn, v_g_post_ffn)]
    steps = [step(w, g, m, v, n) for n, w, g, m, v in in_order]
    grads, deltas, new_m, new_v = ([s[j] for s in steps] for j in range(4))
    return (loss, grad_x, *grads, *deltas, *new_m, *new_v)
```

```python
import jax
import jax.numpy as jnp
from jax import lax
from jax.experimental import pallas as pl
from jax.experimental.pallas import tpu as pltpu

F32 = jnp.float32
BF16 = jnp.bfloat16

N_DEV = 8
DEPTH = 2
D_MODEL = 1024
HEAD_DIM = 64
N_HEADS = 8
N_KV_HEADS = 2
GROUP = N_HEADS // N_KV_HEADS
ATTN_WIDTH = N_HEADS * HEAD_DIM
KV_WIDTH = N_KV_HEADS * HEAD_DIM
POOL_WIDTH = 512
POOL_WINDOWS = (2, 4, 8, 16)
POOL_GROUP_WIDTH = 128
POOL_HALO = 16
IN_WIDTH = ATTN_WIDTH + 2 * KV_WIDTH + POOL_WIDTH
D_FF = 2816
N_MOD = 6
BLOCK = 128
ROT_DIM = 16
ROPE_THETA = 500000.0
EPS = 1e-6
NEG_INF = -1e30
Q_SCALE = HEAD_DIM ** -0.5

ADAM_LR = 0.001
ADAM_B1 = 0.9
ADAM_B2 = 0.999
ADAM_EPS = 1e-08
ADAM_WD = 0.01
ADAM_STEP = 10

LANES = 128
SEQ_TILE = 512
ATTN_BLOCKS = 8
FF_CHUNK = 256
VMEM_LIMIT = 56 * 1024 * 1024
MESH = pl.DeviceIdType.MESH

NT = (((1,), (1,)), ((), ()))
TN = (((0,), (0,)), ((), ()))


def _dot(a, b, dims=None):
    if dims is None:
        return jnp.dot(a, b, preferred_element_type=F32)
    return lax.dot_general(a, b, dims, preferred_element_type=F32)


def _cparams(n_axes):
    return pltpu.CompilerParams(dimension_semantics=("arbitrary",) * n_axes, vmem_limit_bytes=VMEM_LIMIT)


def _resident(shape):
    zeros = (0,) * len(shape)
    return pl.BlockSpec(shape, lambda *_: zeros, pipeline_mode=pl.Buffered(1))


def _rows(*vectors, width):
    rows = [jnp.reshape(v, (1, width)).astype(F32) for v in vectors]
    rows.append(jnp.zeros((8 - len(rows), width), F32))
    return jnp.concatenate(rows, axis=0)


def _rsqrt_ms(x):
    return lax.rsqrt(jnp.mean(x * x, axis=-1, keepdims=True) + EPS)


def _colsum(x):
    return jnp.sum(x, axis=0, keepdims=True)


def _seq_tile(s, tiles=1):
    return min(s, tiles * SEQ_TILE)


def _mesh_place():
    x, y, c = lax.axis_index("x"), lax.axis_index("y"), lax.axis_index("c")
    return x, y, c


def _flip(place, k):
    x, y, c = place
    return (1 - x if k & 4 else x, 1 - y if k & 2 else y, 1 - c if k & 1 else c)


def _index(place):
    x, y, c = place
    return 4 * x + 2 * y + c


def _allgather_vmem(block, name):
    r, c = block.shape

    def body(x_ref, out_ref, send_sems, recv_sems, local_sem):
        me = _mesh_place()
        mine = pltpu.make_async_copy(x_ref, out_ref.at[_index(me)], local_sem)
        mine.start()

        def copy(k):
            return pltpu.make_async_remote_copy(
                src_ref=x_ref, dst_ref=out_ref.at[_index(me)], send_sem=send_sems.at[k - 1], recv_sem=recv_sems.at[k - 1],
                device_id=_flip(me, k), device_id_type=MESH)

        def arrival(k):
            return pltpu.make_async_remote_copy(
                src_ref=x_ref, dst_ref=out_ref.at[_index(_flip(me, k))], send_sem=send_sems.at[k - 1],
                recv_sem=recv_sems.at[k - 1], device_id=_flip(me, k), device_id_type=MESH)

        for k in range(1, N_DEV):
            copy(k).start()
        for k in range(1, N_DEV):
            arrival(k).wait_recv()
        for k in range(1, N_DEV):
            copy(k).wait_send()
        mine.wait()

    return pl.pallas_call(
        body, name=name,
        out_shape=jax.ShapeDtypeStruct((N_DEV, r, c), block.dtype),
        in_specs=[pl.BlockSpec(memory_space=pltpu.VMEM)],
        out_specs=pl.BlockSpec(memory_space=pltpu.VMEM),
        scratch_shapes=[pltpu.SemaphoreType.DMA((N_DEV - 1,)), pltpu.SemaphoreType.DMA((N_DEV - 1,)),
                        pltpu.SemaphoreType.DMA],
    )(block)


class _Gather:
    def __init__(self, shards):
        n = len(shards)
        self.operands = list(shards)
        self.out_shape = [jax.ShapeDtypeStruct((N_DEV,) + s.shape, s.dtype) for s in shards]
        self.scratch = [pltpu.SemaphoreType.DMA((N_DEV - 1, n)), pltpu.SemaphoreType.DMA((N_DEV - 1, n)),
                        pltpu.SemaphoreType.DMA((n,))]

    def _copies(self, x_refs, out_refs, sems):
        send_sems, recv_sems, local_sems = sems
        n = len(x_refs)
        x, y, c = _mesh_place()
        me, sibling = (x, y, c), (x, y, 1 - c)
        chips = [(1 - x, y), (x, 1 - y), (1 - x, 1 - y)]

        def copy(k, a, block, to, from_input=False):
            rows = out_refs[a].at[_index(block)]
            return pltpu.make_async_remote_copy(
                src_ref=x_refs[a] if from_input else rows, dst_ref=rows,
                send_sem=send_sems.at[k, a], recv_sem=recv_sems.at[k, a], device_id=to, device_id_type=MESH)

        mine = [pltpu.make_async_copy(x_refs[a], out_refs[a].at[_index(me)], local_sems.at[a]) for a in range(n)]
        first = [copy(0, a, me, sibling, from_input=True) for a in range(n)]
        first += [copy(1 + j, a, me, (*chip, c), from_input=True) for j, chip in enumerate(chips) for a in range(n)]
        over_ici = [copy(1 + j, a, (*chip, c), me) for j, chip in enumerate(chips) for a in range(n)]
        passed = [copy(4 + j, a, (*chip, c), sibling) for j, chip in enumerate(chips) for a in range(n)]
        from_sibling = [copy(0, a, sibling, me) for a in range(n)]
        from_sibling += [copy(4 + j, a, (*chip, 1 - c), me) for j, chip in enumerate(chips) for a in range(n)]
        return mine, first, over_ici, passed, from_sibling

    def begin(self, x_refs, out_refs, sems):
        mine, first, _, _, _ = self._copies(x_refs, out_refs, sems)
        for cp in mine + first:
            cp.start()

    def middle(self, x_refs, out_refs, sems):
        _, _, over_ici, passed, _ = self._copies(x_refs, out_refs, sems)
        for arrived, onward in zip(over_ici, passed):
            arrived.wait_recv()
            onward.start()

    def end(self, x_refs, out_refs, sems):
        mine, first, _, passed, from_sibling = self._copies(x_refs, out_refs, sems)
        for cp in from_sibling:
            cp.wait_recv()
        for cp in first + passed:
            cp.wait_send()
        for cp in mine:
            cp.wait()


class _Scatter:
    OWN = N_DEV - 1

    def __init__(self, stacks):
        n = len(stacks)
        self.operands = list(stacks)
        self.out_shape = [jax.ShapeDtypeStruct(s.shape, s.dtype) for s in stacks]
        self.scratch = [pltpu.SemaphoreType.DMA((N_DEV - 1, n)), pltpu.SemaphoreType.DMA((N_DEV - 1, n)),
                        pltpu.SemaphoreType.DMA((n,))]

    def _copies(self, g_refs, out_refs, sems):
        send_sems, recv_sems, local_sems = sems
        me = _mesh_place()

        def copy(k, a):
            peer = _flip(me, k)
            return pltpu.make_async_remote_copy(
                src_ref=g_refs[a].at[_index(peer)], dst_ref=out_refs[a].at[k - 1],
                send_sem=send_sems.at[k - 1, a], recv_sem=recv_sems.at[k - 1, a], device_id=peer, device_id_type=MESH)

        mine = [pltpu.make_async_copy(g_refs[a].at[_index(me)], out_refs[a].at[self.OWN], local_sems.at[a])
                for a in range(len(g_refs))]
        return mine, [copy(k, a) for k in range(1, N_DEV) for a in range(len(g_refs))]

    def begin(self, g_refs, out_refs, sems):
        mine, copies = self._copies(g_refs, out_refs, sems)
        for cp in mine + copies:
            cp.start()

    def middle(self, g_refs, out_refs, sems):
        pass

    def end(self, g_refs, out_refs, sems):
        mine, copies = self._copies(g_refs, out_refs, sems)
        for cp in copies:
            cp.wait_recv()
        for cp in copies:
            cp.wait_send()
        for cp in mine:
            cp.wait()


def _call(body, args, *, name, grid, in_specs, out_specs, out_shape, scratch_shapes=(), exchanges=()):
    if not exchanges:
        outs = pl.pallas_call(body, name=name, grid=grid, in_specs=in_specs, out_specs=out_specs, out_shape=out_shape,
                              scratch_shapes=list(scratch_shapes), compiler_params=_cparams(len(grid)))(*args)
        return outs, []
    (steps,) = grid
    n_in, n_out, n_scr = len(in_specs), len(out_specs), len(scratch_shapes)
    ex_in = [len(ex.operands) for ex in exchanges]
    ex_out = [len(ex.out_shape) for ex in exchanges]
    ex_scr = [len(ex.scratch) for ex in exchanges]

    def split(refs, counts):
        parts, pos = [], 0
        for cnt in counts:
            parts.append(refs[pos:pos + cnt])
            pos += cnt
        return parts

    def wrapped(*refs):
        ins, xin, outs, xout, scr, xscr = split(refs, [n_in, sum(ex_in), n_out, sum(ex_out), n_scr, sum(ex_scr)])
        bound = list(zip(exchanges, split(xin, ex_in), split(xout, ex_out), split(xscr, ex_scr)))
        step = pl.program_id(0)

        def phase(method, at):
            @pl.when(step == at)
            def _():
                for ex, i_refs, o_refs, sems in bound:
                    getattr(ex, method)(i_refs, o_refs, sems)

        phase("begin", 0)
        phase("middle", (3 * steps) // 4)
        body(*ins, *outs, *scr)
        phase("end", steps - 1)

    any_spec = pl.BlockSpec(memory_space=pl.ANY)
    results = pl.pallas_call(
        wrapped, name=name, grid=grid,
        in_specs=list(in_specs) + [any_spec] * sum(ex_in),
        out_specs=list(out_specs) + [any_spec] * sum(ex_out),
        out_shape=list(out_shape) + [s for ex in exchanges for s in ex.out_shape],
        scratch_shapes=list(scratch_shapes) + [s for ex in exchanges for s in ex.scratch],
        compiler_params=_cparams(1),
    )(*args, *[a for ex in exchanges for a in ex.operands])
    return results[:n_out], split(results[n_out:], ex_out)


def _exchanges_alone(exchanges, name):
    def body(flag_ref):
        flag_ref[...] = jnp.zeros_like(flag_ref)

    _, ex_outs = _call(body, [], name=name, grid=(1,), in_specs=[], out_specs=[pl.BlockSpec((8, LANES), lambda i: (0, 0))],
                       out_shape=[jax.ShapeDtypeStruct((8, LANES), F32)], exchanges=exchanges)
    return ex_outs


def _mod_fwd(c_all, ada_w):
    cols = ada_w.shape[2]

    def body(c_ref, w_ref, mp_ref, act_ref):
        c = c_ref[...]
        act = c * jax.nn.sigmoid(c)
        act_ref[...] = act
        mp_ref[0] = _dot(act.astype(BF16), w_ref[0].astype(BF16))

    return pl.pallas_call(
        body, name="mod_fwd", grid=(DEPTH,),
        in_specs=[pl.BlockSpec((N_DEV, D_MODEL), lambda l: (0, 0)),
                  pl.BlockSpec((1, D_MODEL, cols), lambda l: (l, 0, 0))],
        out_specs=[pl.BlockSpec((1, N_DEV, cols), lambda l: (l, 0, 0)),
                   pl.BlockSpec((N_DEV, D_MODEL), lambda l: (0, 0))],
        out_shape=[jax.ShapeDtypeStruct((DEPTH, N_DEV, cols), F32), jax.ShapeDtypeStruct((N_DEV, D_MODEL), F32)],
        compiler_params=_cparams(1),
    )(c_all, ada_w)


def _ada_grad(c_act, dmod_cols):
    cols = dmod_cols.shape[2]

    def body(act_ref, dm_ref, g_ref):
        g_ref[0] = _dot(act_ref[...].astype(BF16), dm_ref[0].astype(BF16), TN)

    return pl.pallas_call(
        body, name="ada_grad", grid=(DEPTH,),
        in_specs=[pl.BlockSpec((N_DEV, D_MODEL), lambda l: (0, 0)),
                  pl.BlockSpec((1, N_DEV, cols), lambda l: (l, 0, 0))],
        out_specs=pl.BlockSpec((1, D_MODEL, cols), lambda l: (l, 0, 0)),
        out_shape=jax.ShapeDtypeStruct((DEPTH, D_MODEL, cols), F32),
        compiler_params=_cparams(1),
    )(c_act, dmod_cols)


def _rotary_terms(cos, sin):
    in_head = lax.broadcasted_iota(jnp.int32, (1, LANES), 1) % HEAD_DIM
    half = ROT_DIM // 2
    return cos, jnp.where(in_head < half, -sin, 0.0), jnp.where((in_head >= half) & (in_head < ROT_DIM), sin, 0.0)


def _rotate(t, cos, sin_lo, sin_hi):
    return t * cos + pltpu.roll(t, LANES - 8, 1) * sin_lo + pltpu.roll(t, 8, 1) * sin_hi


def _rotate_bwd(t, cos, sin_lo, sin_hi):
    return t * cos + pltpu.roll(t * sin_lo, 8, 1) + pltpu.roll(t * sin_hi, LANES - 8, 1)


def _fwd_in(x, vec, w_in_t, b_in, rot, name, exchanges=()):
    s = x.shape[0]
    ts = _seq_tile(s, 2)

    def body(x_ref, vec_ref, w_ref, b_ref, cos_ref, sin_ref, q_ref, k_ref, v_ref, u_ref):
        xt = x_ref[...]
        h = xt * _rsqrt_ms(xt) * vec_ref[0:1, :] * vec_ref[1:2, :] + vec_ref[2:3, :]
        proj = _dot(h.astype(BF16), w_ref[...], NT) + b_ref[0:1, :]
        cos, lo, hi = _rotary_terms(cos_ref[...], sin_ref[...])
        for j in range(ATTN_WIDTH // LANES):
            q_ref[:, j * LANES:(j + 1) * LANES] = (
                _rotate(proj[:, j * LANES:(j + 1) * LANES], cos, lo, hi) * Q_SCALE).astype(BF16)
        k_ref[...] = _rotate(proj[:, ATTN_WIDTH:ATTN_WIDTH + KV_WIDTH], cos, lo, hi).astype(BF16)
        v_ref[...] = proj[:, ATTN_WIDTH + KV_WIDTH:ATTN_WIDTH + 2 * KV_WIDTH].astype(BF16)
        u_ref[...] = proj[:, ATTN_WIDTH + 2 * KV_WIDTH:]

    row = lambda w: pl.BlockSpec((ts, w), lambda i: (i, 0))
    return _call(
        body, [x, vec, w_in_t, b_in, *rot], name=name, grid=(s // ts,),
        in_specs=[row(D_MODEL), _resident((8, D_MODEL)), _resident((IN_WIDTH, D_MODEL)), _resident((8, IN_WIDTH)),
                  row(LANES), row(LANES)],
        out_specs=[row(ATTN_WIDTH), row(KV_WIDTH), row(KV_WIDTH), row(POOL_WIDTH)],
        out_shape=[jax.ShapeDtypeStruct((s, ATTN_WIDTH), BF16), jax.ShapeDtypeStruct((s, KV_WIDTH), BF16),
                   jax.ShapeDtypeStruct((s, KV_WIDTH), BF16), jax.ShapeDtypeStruct((s, POOL_WIDTH), F32)],
        exchanges=exchanges)


def _band_mask_t(first_block):
    key = lax.broadcasted_iota(jnp.int32, (2 * BLOCK, 2 * BLOCK), 0)
    qry = lax.broadcasted_iota(jnp.int32, (2 * BLOCK, 2 * BLOCK), 1) & (BLOCK - 1)
    d = key - qry
    return (d >= 1) & (d <= BLOCK) & ((key >= BLOCK) | jnp.logical_not(first_block))


def _placed(both, kh, b):
    lane = lax.broadcasted_iota(jnp.int32, (1, LANES), 1)
    src = both if kh == b else pltpu.roll(both, HEAD_DIM, 1)
    return jnp.where((lane >= b * HEAD_DIM) & (lane < (b + 1) * HEAD_DIM), src, jnp.zeros_like(src))


def _pair(ref, rows, kh, r):
    j = 2 * kh + r
    return ref[rows, j * LANES:(j + 1) * LANES]


def _attn_specs(s):
    qb = min(ATTN_BLOCKS, s // BLOCK)
    cur = lambda w: pl.BlockSpec((qb * BLOCK, w), lambda i: (i, 0))
    prev = lambda w: pl.BlockSpec((BLOCK, w), lambda i: (jnp.maximum(i * qb - 1, 0), 0))
    return qb, cur, prev, pl.BlockSpec((N_HEADS, qb * BLOCK), lambda i: (0, i))


def _kv_window(prev_ref, cur_ref, jb):
    before = prev_ref[...] if jb == 0 else cur_ref[(jb - 1) * BLOCK:jb * BLOCK, :]
    return jnp.concatenate([before, cur_ref[jb * BLOCK:(jb + 1) * BLOCK, :]], axis=0)


def _sink_row(sink_ref, kh, b):
    lane = lax.broadcasted_iota(jnp.int32, (1, 2 * BLOCK), 1)
    return jnp.where(lane < BLOCK, sink_ref[GROUP * kh + b], sink_ref[GROUP * kh + 2 + b])


def _attn_fwd_t(sinks, q, k, v, name, exchanges=()):
    s = q.shape[0]
    qb, cur, prev, lse_spec = _attn_specs(s)

    def block(jb, sink_ref, q_ref, kc_ref, kp_ref, vc_ref, vp_ref, o_ref, l_ref):
        rows = slice(jb * BLOCK, (jb + 1) * BLOCK)
        valid = _band_mask_t(pl.program_id(0) == 0 if jb == 0 else False)
        k_both, v_both = _kv_window(kp_ref, kc_ref, jb), _kv_window(vp_ref, vc_ref, jb)
        for kh in range(N_KV_HEADS):
            qg = jnp.concatenate([_pair(q_ref, rows, kh, 0), _pair(q_ref, rows, kh, 1)], axis=0)
            keys = jnp.concatenate([_placed(k_both, kh, 0), _placed(k_both, kh, 1)], axis=0)
            st = _dot(keys, qg, NT)
            out = jnp.zeros((2 * BLOCK, LANES), F32)
            for b in range(2):
                sc = jnp.where(valid, st[b * 2 * BLOCK:(b + 1) * 2 * BLOCK, :], NEG_INF)
                sink = _sink_row(sink_ref, kh, b)
                m = jnp.maximum(jnp.max(sc, axis=0, keepdims=True), sink)
                p = jnp.exp(sc - m)
                den = jnp.sum(p, axis=0, keepdims=True) + jnp.exp(sink - m)
                out = out + _dot((p * (1.0 / den)).astype(BF16), _placed(v_both, kh, b), TN)
                lse = m + jnp.log(den)
                for r in range(2):
                    h = GROUP * kh + 2 * r + b
                    l_ref[h:h + 1, rows] = lse[:, r * BLOCK:(r + 1) * BLOCK]
            for r in range(2):
                j = 2 * kh + r
                o_ref[rows, j * LANES:(j + 1) * LANES] = out[r * BLOCK:(r + 1) * BLOCK, :].astype(BF16)

    def body(*refs):
        for jb in range(qb):
            block(jb, *refs)

    return _call(
        body, [sinks, q, k, k, v, v], name=name, grid=(s // (qb * BLOCK),),
        in_specs=[pl.BlockSpec(memory_space=pltpu.SMEM), cur(ATTN_WIDTH), cur(KV_WIDTH), prev(KV_WIDTH),
                  cur(KV_WIDTH), prev(KV_WIDTH)],
        out_specs=[cur(ATTN_WIDTH), lse_spec],
        out_shape=[jax.ShapeDtypeStruct((s, ATTN_WIDTH), BF16), jax.ShapeDtypeStruct((N_HEADS, s), F32)],
        exchanges=exchanges)


def _attn_bwd_t(sinks, q, do, lse, k, v, name, exchanges=()):
    s = q.shape[0]
    qb, cur, prev, lse_spec = _attn_specs(s)

    def block(jb, sink_ref, q_ref, do_ref, l_ref, kc_ref, kp_ref, vc_ref, vp_ref,
              dq_ref, dkc_ref, dkp_ref, dvc_ref, dvp_ref, dsink_ref):
        rows = slice(jb * BLOCK, (jb + 1) * BLOCK)
        valid = _band_mask_t(pl.program_id(0) == 0 if jb == 0 else False)
        k_both, v_both = _kv_window(kp_ref, kc_ref, jb), _kv_window(vp_ref, vc_ref, jb)
        lane = lax.broadcasted_iota(jnp.int32, (1, LANES), 1)
        col = lax.broadcasted_iota(jnp.int32, (1, 2 * BLOCK), 1)
        dk_heads, dv_heads = [], []
        for kh in range(N_KV_HEADS):
            qg = jnp.concatenate([_pair(q_ref, rows, kh, 0), _pair(q_ref, rows, kh, 1)], axis=0)
            dog = jnp.concatenate([_pair(do_ref, rows, kh, 0), _pair(do_ref, rows, kh, 1)], axis=0)
            k_placed = [_placed(k_both, kh, b) for b in range(2)]
            v_placed = [_placed(v_both, kh, b) for b in range(2)]
            st = _dot(jnp.concatenate(k_placed, axis=0), qg, NT)
            dpt = _dot(jnp.concatenate(v_placed, axis=0), dog, NT)
            dqg = jnp.zeros((2 * BLOCK, LANES), F32)
            dk_b, dv_b = [], []
            for b in range(2):
                part = slice(b * 2 * BLOCK, (b + 1) * 2 * BLOCK)
                heads = [GROUP * kh + 2 * r + b for r in range(2)]
                lse_row = jnp.concatenate([l_ref[h:h + 1, rows] for h in heads], axis=1)
                p = jnp.where(valid, jnp.exp(st[part, :] - lse_row), 0.0)
                dp = dpt[part, :]
                delta = jnp.sum(p * dp, axis=0, keepdims=True)
                sink_pull = jnp.exp(_sink_row(sink_ref, kh, b) - lse_row) * delta
                dsink_ref[heads[0]:heads[0] + 1, :] += -jnp.sum(jnp.where(col < BLOCK, sink_pull, 0.0))
                dsink_ref[heads[1]:heads[1] + 1, :] += -jnp.sum(jnp.where(col < BLOCK, 0.0, sink_pull))
                ds = (p * (dp - delta)).astype(BF16)
                dqg = dqg + _dot(ds, k_placed[b], TN)
                dk_b.append(_dot(ds, qg))
                dv_b.append(_dot(p.astype(BF16), dog))
            for parts, total in ((dk_b, dk_heads), (dv_b, dv_heads)):
                kept = jnp.where(lane < HEAD_DIM, parts[0], parts[1])
                total.append(kept + pltpu.roll(kept, HEAD_DIM, 1))
            for r in range(2):
                j = 2 * kh + r
                dq_ref[rows, j * LANES:(j + 1) * LANES] = dqg[r * BLOCK:(r + 1) * BLOCK, :] * Q_SCALE
        dk = jnp.where(lane < HEAD_DIM, dk_heads[0], dk_heads[1])
        dv = jnp.where(lane < HEAD_DIM, dv_heads[0], dv_heads[1])
        dkp_ref[rows, :] = dk[0:BLOCK, :]
        dkc_ref[rows, :] = dk[BLOCK:, :]
        dvp_ref[rows, :] = dv[0:BLOCK, :]
        dvc_ref[rows, :] = dv[BLOCK:, :]

    def body(*refs):
        @pl.when(pl.program_id(0) == 0)
        def _():
            refs[-1][...] = jnp.zeros_like(refs[-1])

        for jb in range(qb):
            block(jb, *refs)

    kv = jax.ShapeDtypeStruct((s, KV_WIDTH), F32)
    return _call(
        body, [sinks, q, do, lse, k, k, v, v], name=name, grid=(s // (qb * BLOCK),),
        in_specs=[pl.BlockSpec(memory_space=pltpu.SMEM), cur(ATTN_WIDTH), cur(ATTN_WIDTH), lse_spec,
                  cur(KV_WIDTH), prev(KV_WIDTH), cur(KV_WIDTH), prev(KV_WIDTH)],
        out_specs=[cur(ATTN_WIDTH), cur(KV_WIDTH), cur(KV_WIDTH), cur(KV_WIDTH), cur(KV_WIDTH),
                   pl.BlockSpec((8, LANES), lambda i: (0, 0))],
        out_shape=[jax.ShapeDtypeStruct((s, ATTN_WIDTH), F32), kv, kv, kv, kv, jax.ShapeDtypeStruct((8, LANES), F32)],
        exchanges=exchanges)


def _pool_counts(tile, ts):
    t = (tile * ts + lax.broadcasted_iota(jnp.int32, (ts, 1), 0) + 1).astype(F32)
    return [jnp.minimum(t, float(w)) for w in POOL_WINDOWS]


def _pooled(u, halo, counts, gi):
    cols = slice(gi * POOL_GROUP_WIDTH, (gi + 1) * POOL_GROUP_WIDTH)
    acc = jnp.concatenate([halo[:, cols], u[:, cols]], axis=0)
    shift = 1
    while shift < POOL_WINDOWS[gi]:
        acc = acc + pltpu.roll(acc, shift, 0)
        shift *= 2
    return acc[POOL_HALO:, :] / counts[gi] - u[:, cols]


def _fwd_out(ao, u, pool_w, psc, w_out, x, vec, name, exchanges=()):
    s = x.shape[0]
    ts = _seq_tile(s, 2)
    hb = ts // POOL_HALO

    def body(ao_ref, u_ref, uh_ref, pw_ref, psc_ref, w_ref, x_ref, vec_ref, x1_ref, mix_ref):
        i = pl.program_id(0)
        u_t = u_ref[...]
        halo = jnp.where(i > 0, uh_ref[...], 0.0)
        counts = _pool_counts(i, ts)
        cat = [ao_ref[...]]
        for gi in range(len(POOL_WINDOWS)):
            cols = slice(gi * POOL_GROUP_WIDTH, (gi + 1) * POOL_GROUP_WIDTH)
            og = _dot(_pooled(u_t, halo, counts, gi).astype(BF16), pw_ref[gi]) * psc_ref[0:1, cols]
            cat.append(og.astype(BF16))
        mix = _dot(jnp.concatenate(cat, axis=1), w_ref[...])
        mix_ref[...] = mix
        x1_ref[...] = x_ref[...] + vec_ref[0:1, :] * (mix * _rsqrt_ms(mix) * vec_ref[1:2, :])

    row = lambda w: pl.BlockSpec((ts, w), lambda i: (i, 0))
    return _call(
        body, [ao, u, u, pool_w, psc, w_out, x, vec], name=name, grid=(s // ts,),
        in_specs=[row(ATTN_WIDTH), row(POOL_WIDTH),
                  pl.BlockSpec((POOL_HALO, POOL_WIDTH), lambda i: (jnp.maximum(i * hb - 1, 0), 0)),
                  _resident(pool_w.shape), _resident((8, POOL_WIDTH)), _resident((D_MODEL, D_MODEL)),
                  row(D_MODEL), _resident((8, D_MODEL))],
        out_specs=[row(D_MODEL), row(D_MODEL)],
        out_shape=[jax.ShapeDtypeStruct((s, D_MODEL), F32), jax.ShapeDtypeStruct((s, D_MODEL), F32)],
        exchanges=exchanges)


def _ffn_fwd(x1, vec, w_gate_t, w_up_t, w_down, name, exchanges=(), target=None):
    s = x1.shape[0]
    ts = _seq_tile(s)

    def body(x_ref, *refs):
        if target is None:
            vec_ref, wg_ref, wu_ref, wd_ref, x2_ref, f_ref, gt_ref, up_ref, h2_ref = refs
        else:
            t_ref, vec_ref, wg_ref, wu_ref, wd_ref, x2_ref, f_ref, gt_ref, up_ref, h2_ref, loss_ref = refs
        xt = x_ref[...]
        h2 = (xt * _rsqrt_ms(xt) * vec_ref[0:1, :] * vec_ref[1:2, :] + vec_ref[2:3, :]).astype(BF16)
        h2_ref[...] = h2
        f = jnp.zeros((ts, D_MODEL), F32)
        for c in range(D_FF // FF_CHUNK):
            cols = slice(c * FF_CHUNK, (c + 1) * FF_CHUNK)
            g = _dot(h2, wg_ref[cols, :], NT)
            up = _dot(h2, wu_ref[cols, :], NT)
            act = ((g * jax.nn.sigmoid(g)) * up).astype(BF16)
            gt_ref[:, cols] = g.astype(BF16)
            up_ref[:, cols] = up.astype(BF16)
            f = f + _dot(act, wd_ref[cols, :])
        f_ref[...] = f
        x2 = xt + vec_ref[3:4, :] * (f * _rsqrt_ms(f) * vec_ref[4:5, :])
        if target is None:
            x2_ref[...] = x2
        else:
            @pl.when(pl.program_id(0) == 0)
            def _():
                loss_ref[...] = jnp.zeros_like(loss_ref)

            diff = x2 - t_ref[...]
            x2_ref[...] = diff / D_MODEL
            loss_ref[...] += 0.5 * jnp.sum(jnp.mean(diff * diff, axis=-1, keepdims=True))

    row = lambda w: pl.BlockSpec((ts, w), lambda i: (i, 0))
    wide = jax.ShapeDtypeStruct((s, D_FF), BF16)
    with_loss = target is not None
    return _call(
        body, [x1] + ([target] if with_loss else []) + [vec, w_gate_t, w_up_t, w_down], name=name, grid=(s // ts,),
        in_specs=[row(D_MODEL)] * (2 if with_loss else 1) + [
            _resident((8, D_MODEL)), _resident((D_FF, D_MODEL)), _resident((D_FF, D_MODEL)), _resident((D_FF, D_MODEL))],
        out_specs=[row(D_MODEL), row(D_MODEL), row(D_FF), row(D_FF), row(D_MODEL)] + (
            [pl.BlockSpec((8, LANES), lambda i: (0, 0))] if with_loss else []),
        out_shape=[jax.ShapeDtypeStruct((s, D_MODEL), F32), jax.ShapeDtypeStruct((s, D_MODEL), F32), wide, wide,
                   jax.ShapeDtypeStruct((s, D_MODEL), BF16)] + (
            [jax.ShapeDtypeStruct((8, LANES), F32)] if with_loss else []),
        exchanges=exchanges)


def _norm_bwd(dy_hat, x_hat, r):
    return r * (dy_hat - x_hat * jnp.mean(dy_hat * x_hat, axis=-1, keepdims=True))


def _accumulate_rows(ref, rows):
    for j, val in enumerate(rows):
        ref[j:j + 1, :] += val


def _ffn_bwd_act(dx2, f, gt, up, vec, w_down, name, exchanges=()):
    s = dx2.shape[0]
    ts = _seq_tile(s)

    def body(dx2_ref, f_ref, gt_ref, up_ref, vec_ref, wd_ref, dgt_ref, dup_ref, dwd_ref, cs_ref):
        @pl.when(pl.program_id(0) == 0)
        def _():
            cs_ref[...] = jnp.zeros_like(cs_ref)
            dwd_ref[...] = jnp.zeros_like(dwd_ref)

        dx2_t, f_t = dx2_ref[...], f_ref[...]
        gate, g_post = vec_ref[3:4, :], vec_ref[4:5, :]
        rf = _rsqrt_ms(f_t)
        f_hat = f_t * rf
        df = _norm_bwd(dx2_t * gate * g_post, f_hat, rf).astype(BF16)
        _accumulate_rows(cs_ref, [_colsum(dx2_t * (f_hat * g_post)), _colsum(dx2_t * gate * f_hat)])
        for c in range(D_FF // FF_CHUNK):
            cols = slice(c * FF_CHUNK, (c + 1) * FF_CHUNK)
            dact = _dot(df, wd_ref[cols, :], NT)
            g, u_t = gt_ref[:, cols].astype(F32), up_ref[:, cols].astype(F32)
            sg = 0.5 * jnp.tanh(0.5 * g) + 0.5
            silu = g * sg
            dgt_ref[:, cols] = (dact * u_t * (sg * (1.0 + g * (1.0 - sg)))).astype(BF16)
            dup_ref[:, cols] = (dact * silu).astype(BF16)
            dwd_ref[cols, :] += _dot((silu * u_t).astype(BF16), df, TN)

    row = lambda w: pl.BlockSpec((ts, w), lambda i: (i, 0))
    wide = jax.ShapeDtypeStruct((s, D_FF), BF16)
    return _call(
        body, [dx2, f, gt, up, vec, w_down], name=name, grid=(s // ts,),
        in_specs=[row(D_MODEL), row(D_MODEL), row(D_FF), row(D_FF), _resident((8, D_MODEL)), _resident((D_FF, D_MODEL))],
        out_specs=[row(D_FF), row(D_FF), _resident((D_FF, D_MODEL)), pl.BlockSpec((8, D_MODEL), lambda i: (0, 0))],
        out_shape=[wide, wide, jax.ShapeDtypeStruct((D_FF, D_MODEL), F32), jax.ShapeDtypeStruct((8, D_MODEL), F32)],
        exchanges=exchanges)


def _ffn_bwd_in(dx2, x1, dgt, dup, vec, w_gate_t, w_up_t, name, exchanges=()):
    s = x1.shape[0]
    ts = _seq_tile(s)

    def body(dx2_ref, x1_ref, dgt_ref, dup_ref, vec_ref, wg_ref, wu_ref, dx1_ref, cs_ref):
        @pl.when(pl.program_id(0) == 0)
        def _():
            cs_ref[...] = jnp.zeros_like(cs_ref)

        g_pre, one_scale = vec_ref[0:1, :], vec_ref[1:2, :]
        dh2 = _dot(dgt_ref[...], wg_ref[...]) + _dot(dup_ref[...], wu_ref[...])
        x1_t = x1_ref[...]
        r1 = _rsqrt_ms(x1_t)
        x_hat = x1_t * r1
        dx1_ref[...] = dx2_ref[...] + _norm_bwd(dh2 * g_pre * one_scale, x_hat, r1)
        _accumulate_rows(cs_ref, [_colsum(dh2), _colsum(dh2 * (x_hat * g_pre)), _colsum(dh2 * one_scale * x_hat)])

    row = lambda w: pl.BlockSpec((ts, w), lambda i: (i, 0))
    return _call(
        body, [dx2, x1, dgt, dup, vec, w_gate_t, w_up_t], name=name, grid=(s // ts,),
        in_specs=[row(D_MODEL), row(D_MODEL), row(D_FF), row(D_FF), _resident((8, D_MODEL)),
                  _resident((D_FF, D_MODEL)), _resident((D_FF, D_MODEL))],
        out_specs=[row(D_MODEL), pl.BlockSpec((8, D_MODEL), lambda i: (0, 0))],
        out_shape=[jax.ShapeDtypeStruct((s, D_MODEL), F32), jax.ShapeDtypeStruct((8, D_MODEL), F32)],
        exchanges=exchanges)


def _weight_grad(a, b, name, exchanges=()):
    s, m = a.shape
    n = b.shape[1]
    tk = min(s, 4 * SEQ_TILE)
    steps = s // tk

    def body(a_ref, b_ref, o_ref, acc_ref):
        i = pl.program_id(0)

        @pl.when(i == 0)
        def _():
            acc_ref[...] = jnp.zeros_like(acc_ref)

        b_t = b_ref[...]
        for c in range(m // FF_CHUNK):
            rows = slice(c * FF_CHUNK, (c + 1) * FF_CHUNK)
            acc_ref[rows, :] += _dot(a_ref[:, rows], b_t, TN)

        @pl.when(i == steps - 1)
        def _():
            o_ref[...] = acc_ref[...].astype(BF16)

    (grad,), ex_outs = _call(
        body, [a, b], name=name, grid=(steps,),
        in_specs=[pl.BlockSpec((tk, m), lambda i: (i, 0)), pl.BlockSpec((tk, n), lambda i: (i, 0))],
        out_specs=[_resident((m, n))],
        out_shape=[jax.ShapeDtypeStruct((m, n), BF16)],
        scratch_shapes=[pltpu.VMEM((m, n), F32)],
        exchanges=exchanges)
    return grad, ex_outs


def _out_bwd(dx1, mix, ao, u, pool_w, psc, w_out, vec, name, exchanges=()):
    s = dx1.shape[0]
    ts = _seq_tile(s)
    nt = s // ts
    hb = ts // POOL_HALO
    ng = len(POOL_WINDOWS)

    def body(dx1_ref, mix_ref, ao_ref, u_ref, uh_ref, pw_ref, psc_ref, w_ref, vec_ref,
             do_ref, du_ref, dw_out_ref, cs_ref, dpw_ref, dpsc_ref, carry_ref, dw_ref):
        i = pl.program_id(0)
        tile = nt - 1 - i

        @pl.when(i == 0)
        def _():
            dw_ref[...] = jnp.zeros_like(dw_ref)
            cs_ref[...] = jnp.zeros_like(cs_ref)
            dpw_ref[...] = jnp.zeros_like(dpw_ref)
            dpsc_ref[...] = jnp.zeros_like(dpsc_ref)
            carry_ref[...] = jnp.zeros_like(carry_ref)

        dx1_t, mix_t = dx1_ref[...], mix_ref[...]
        gate, g_post = vec_ref[0:1, :], vec_ref[1:2, :]
        rm = _rsqrt_ms(mix_t)
        m_hat = mix_t * rm
        dmix = _norm_bwd(dx1_t * gate * g_post, m_hat, rm).astype(BF16)
        _accumulate_rows(cs_ref, [_colsum(dx1_t * (m_hat * g_post)), _colsum(dx1_t * gate * m_hat)])
        d_cat = _dot(dmix, w_ref[...], NT)
        do_ref[...] = d_cat[:, 0:ATTN_WIDTH].astype(BF16)

        u_t = u_ref[...]
        halo = jnp.where(tile > 0, uh_ref[...], 0.0)
        counts = _pool_counts(tile, ts)
        cat = [ao_ref[...]]
        for gi in range(ng):
            cols = slice(gi * POOL_GROUP_WIDTH, (gi + 1) * POOL_GROUP_WIDTH)
            scale = psc_ref[0:1, cols]
            pooled = _pooled(u_t, halo, counts, gi).astype(BF16)
            og = _dot(pooled, pw_ref[gi])
            cat.append((og * scale).astype(BF16))
            d_out = d_cat[:, ATTN_WIDTH + gi * POOL_GROUP_WIDTH:ATTN_WIDTH + (gi + 1) * POOL_GROUP_WIDTH]
            dpsc_ref[0:1, cols] += _colsum(d_out * og)
            d_og = (d_out * scale).astype(BF16)
            dpw_ref[gi] += _dot(pooled, d_og, TN)
            d_pooled = _dot(d_og, pw_ref[gi], NT)
            spread = d_pooled / counts[gi]
            acc = jnp.concatenate([spread, carry_ref[:, cols]], axis=0)
            shift = 1
            while shift < POOL_WINDOWS[gi]:
                acc = acc + pltpu.roll(acc, ts + POOL_HALO - shift, 0)
                shift *= 2
            du_ref[:, cols] = acc[0:ts, :] - d_pooled
            carry_ref[:, cols] = spread[0:POOL_HALO, :]
        dw_ref[...] += _dot(jnp.concatenate(cat, axis=1), dmix, TN)

        @pl.when(i == nt - 1)
        def _():
            dw_out_ref[...] = dw_ref[...].astype(BF16)

    row = lambda w: pl.BlockSpec((ts, w), lambda i: (nt - 1 - i, 0))
    fixed = lambda shape: pl.BlockSpec(shape, lambda i: (0,) * len(shape))
    return _call(
        body, [dx1, mix, ao, u, u, pool_w, psc, w_out, vec], name=name, grid=(nt,),
        in_specs=[row(D_MODEL), row(D_MODEL), row(ATTN_WIDTH), row(POOL_WIDTH),
                  pl.BlockSpec((POOL_HALO, POOL_WIDTH), lambda i: (jnp.maximum((nt - 1 - i) * hb - 1, 0), 0)),
                  _resident(pool_w.shape), _resident((8, POOL_WIDTH)), _resident((D_MODEL, D_MODEL)),
                  _resident((8, D_MODEL))],
        out_specs=[row(ATTN_WIDTH), row(POOL_WIDTH), _resident((D_MODEL, D_MODEL)), fixed((8, D_MODEL)),
                   fixed(pool_w.shape), fixed((8, POOL_WIDTH))],
        out_shape=[jax.ShapeDtypeStruct((s, ATTN_WIDTH), BF16), jax.ShapeDtypeStruct((s, POOL_WIDTH), F32),
                   jax.ShapeDtypeStruct((D_MODEL, D_MODEL), BF16),
                   jax.ShapeDtypeStruct((8, D_MODEL), F32), jax.ShapeDtypeStruct(pool_w.shape, F32),
                   jax.ShapeDtypeStruct((8, POOL_WIDTH), F32)],
        scratch_shapes=[pltpu.VMEM((POOL_HALO, POOL_WIDTH), F32), pltpu.VMEM((D_MODEL, D_MODEL), F32)],
        exchanges=exchanges)


def _in_bwd(dq, dkc, dkp, dvc, dvp, du, x, dx1, vec, w_in_t, rot, name, exchanges=()):
    s = x.shape[0]
    ts = _seq_tile(s)
    nt = s // ts
    bpt = ts // BLOCK

    def body(dq_ref, dkc_ref, dkp_ref, dkn_ref, dvc_ref, dvp_ref, dvn_ref, du_ref, x_ref, dx1_ref, vec_ref, w_ref,
             cos_ref, sin_ref, dx_ref, dw_out_ref, cs_ref, db_ref, dw_ref):
        i = pl.program_id(0)

        @pl.when(i == 0)
        def _():
            dw_ref[...] = jnp.zeros_like(dw_ref)
            cs_ref[...] = jnp.zeros_like(cs_ref)
            db_ref[...] = jnp.zeros_like(db_ref)

        cos, lo, hi = _rotary_terms(cos_ref[...], sin_ref[...])

        def with_next_block(cur_ref, prev_ref, next_ref):
            nxt = jnp.where(i < nt - 1, next_ref[...], 0.0)
            later = nxt if bpt == 1 else jnp.concatenate([prev_ref[BLOCK:, :], nxt], axis=0)
            return cur_ref[...] + later

        pieces = [_rotate_bwd(dq_ref[:, j * LANES:(j + 1) * LANES], cos, lo, hi) for j in range(ATTN_WIDTH // LANES)]
        pieces.append(_rotate_bwd(with_next_block(dkc_ref, dkp_ref, dkn_ref), cos, lo, hi))
        pieces.append(with_next_block(dvc_ref, dvp_ref, dvn_ref))
        pieces.append(du_ref[...])
        dproj = jnp.concatenate(pieces, axis=1)
        db_ref[0:1, :] += _colsum(dproj)
        dproj_b = dproj.astype(BF16)
        dh = _dot(dproj_b, w_ref[...])

        xt = x_ref[...]
        g_pre, one_scale = vec_ref[0:1, :], vec_ref[1:2, :]
        r = _rsqrt_ms(xt)
        x_hat = xt * r
        dw_ref[...] += _dot(dproj_b, (x_hat * g_pre * one_scale + vec_ref[2:3, :]).astype(BF16), TN)
        dx_ref[...] = dx1_ref[...] + _norm_bwd(dh * g_pre * one_scale, x_hat, r)
        _accumulate_rows(cs_ref, [_colsum(dh), _colsum(dh * (x_hat * g_pre)), _colsum(dh * one_scale * x_hat)])

        @pl.when(i == nt - 1)
        def _():
            dw_out_ref[...] = dw_ref[...].astype(BF16)

    row = lambda w: pl.BlockSpec((ts, w), lambda i: (i, 0))
    nxt = pl.BlockSpec((BLOCK, KV_WIDTH), lambda i: (jnp.minimum((i + 1) * bpt, s // BLOCK - 1), 0))
    fixed = lambda shape: pl.BlockSpec(shape, lambda i: (0,) * len(shape))
    return _call(
        body, [dq, dkc, dkp, dkp, dvc, dvp, dvp, du, x, dx1, vec, w_in_t, *rot], name=name, grid=(nt,),
        in_specs=[row(ATTN_WIDTH), row(KV_WIDTH), row(KV_WIDTH), nxt, row(KV_WIDTH), row(KV_WIDTH), nxt,
                  row(POOL_WIDTH), row(D_MODEL), row(D_MODEL), _resident((8, D_MODEL)), _resident((IN_WIDTH, D_MODEL)),
                  row(LANES), row(LANES)],
        out_specs=[row(D_MODEL), _resident((IN_WIDTH, D_MODEL)), fixed((8, D_MODEL)), fixed((8, IN_WIDTH))],
        out_shape=[jax.ShapeDtypeStruct((s, D_MODEL), F32), jax.ShapeDtypeStruct((IN_WIDTH, D_MODEL), BF16),
                   jax.ShapeDtypeStruct((8, D_MODEL), F32), jax.ShapeDtypeStruct((8, IN_WIDTH), F32)],
        scratch_shapes=[pltpu.VMEM((IN_WIDTH, D_MODEL), F32)],
        exchanges=exchanges)


def _row_tile(rows):
    for t in (512, 352, 256, 176, 160, 128, 64, 32, 16, 8):
        if rows % t == 0:
            return t
    raise ValueError(f"no row tile for {rows} rows")


def _sum_parts(parts, first, name):
    n, r, c = parts.shape
    tr = _row_tile(r)

    def body(p_ref, o_ref):
        acc = p_ref[first].astype(F32)
        for k in range(n):
            if k != first:
                acc = acc + p_ref[k].astype(F32)
        o_ref[...] = acc

    return pl.pallas_call(
        body, name=name, grid=(r // tr,),
        in_specs=[pl.BlockSpec((n, tr, c), lambda i: (0, i, 0))],
        out_specs=pl.BlockSpec((tr, c), lambda i: (i, 0)),
        out_shape=jax.ShapeDtypeStruct((r, c), F32),
        compiler_params=_cparams(1),
    )(parts)


def _adam_step(w, g, m, v):
    m_new = ADAM_B1 * m + (1.0 - ADAM_B1) * g
    v_new = ADAM_B2 * v + (1.0 - ADAM_B2) * (g * g)
    m_hat = m_new / (1.0 - ADAM_B1 ** ADAM_STEP)
    v_hat = v_new / (1.0 - ADAM_B2 ** ADAM_STEP)
    return -ADAM_LR * (m_hat / (jnp.sqrt(v_hat) + ADAM_EPS) + ADAM_WD * w), m_new, v_new


def _sum_adamw(parts, w, m, v, name):
    depth, r, c = w.shape
    tr = _row_tile(r)

    def body(*refs):
        p_refs = refs[:depth]
        w_ref, m_ref, v_ref, g_ref, d_ref, nm_ref, nv_ref = refs[depth:]
        g = None
        for layer, p_ref in enumerate(p_refs):
            acc = p_ref[_Scatter.OWN].astype(F32)
            for k in range(N_DEV - 1):
                acc = acc + p_ref[k].astype(F32)
            g = acc if g is None else jnp.where(pl.program_id(0) == layer, acc, g)
        g_ref[0] = g
        d_ref[0], nm_ref[0], nv_ref[0] = _adam_step(w_ref[0], g, m_ref[0], v_ref[0])

    spec = pl.BlockSpec((1, tr, c), lambda l, i: (l, i, 0))
    shape = jax.ShapeDtypeStruct((depth, r, c), F32)
    return pl.pallas_call(
        body, name=name, grid=(depth, r // tr),
        in_specs=[pl.BlockSpec((N_DEV, tr, c), lambda l, i: (0, i, 0))] * depth + [spec] * 3,
        out_specs=[spec] * 4, out_shape=[shape] * 4, compiler_params=_cparams(2),
    )(*parts, w, m, v)


def _adamw(w, g, m, v, name):
    r, c = w.shape
    tr = _row_tile(r) if r % 8 == 0 else r

    def body(w_ref, g_ref, m_ref, v_ref, d_ref, nm_ref, nv_ref):
        d_ref[...], nm_ref[...], nv_ref[...] = _adam_step(w_ref[...], g_ref[...], m_ref[...], v_ref[...])

    spec = pl.BlockSpec((tr, c), lambda i: (i, 0))
    shape = jax.ShapeDtypeStruct((r, c), F32)
    return pl.pallas_call(
        body, name=name, grid=(r // tr,), in_specs=[spec] * 4, out_specs=[spec] * 3, out_shape=[shape] * 3,
        compiler_params=_cparams(1),
    )(w, g, m, v)


LAYER_ROWS = 80


def _pack_layer(dmod, gains, b_in, pool_scale, sinks, pool_w, loss):
    misc = jnp.concatenate([pool_scale, sinks, jnp.reshape(loss, (1,)),
                            jnp.zeros((D_MODEL - POOL_WIDTH - N_HEADS - 1,), F32)])
    parts = [dmod.reshape(N_MOD, D_MODEL), gains, jnp.pad(b_in, (0, 2 * D_MODEL - IN_WIDTH)).reshape(2, D_MODEL),
             misc.reshape(1, D_MODEL), pool_w.reshape(-1, D_MODEL)]
    packed = jnp.concatenate(parts, axis=0)
    return jnp.pad(packed, ((0, LAYER_ROWS - packed.shape[0]), (0, 0)))


def _unpack_layer(p):
    r = N_MOD + 4
    b_in = p[r:r + 2].reshape(2 * D_MODEL)[:IN_WIDTH]
    pool_w = p[r + 3:r + 3 + 64].reshape(len(POOL_WINDOWS), POOL_GROUP_WIDTH, POOL_GROUP_WIDTH)
    return (p[0:N_MOD].reshape(N_MOD * D_MODEL), p[N_MOD:r], b_in, p[r + 2, :POOL_WIDTH],
            p[r + 2, POOL_WIDTH:POOL_WIDTH + N_HEADS], pool_w, p[r + 2, POOL_WIDTH + N_HEADS])


def _rotary_tables(positions):
    inv_freq = ROPE_THETA ** (-jnp.arange(0, ROT_DIM, 2, dtype=F32) / ROT_DIM)
    head_freq = jnp.concatenate([inv_freq, inv_freq, jnp.zeros((HEAD_DIM - ROT_DIM,), F32)])
    lane_freq = jnp.concatenate([head_freq, head_freq])
    ang = positions.reshape(-1).astype(F32)[:, None] * lane_freq[None, :]
    return jnp.cos(ang), jnp.sin(ang)


def kernel(x, c, positions, ada_w, ada_b, w_in, b_in, sinks, pool_w, pool_scale, w_out, w_gate, w_up, w_down, g_pre_mix, g_post_mix, g_pre_ffn, g_post_ffn, loss_target, m_ada_w, m_ada_b, m_w_in, m_b_in, m_sinks, m_pool_w, m_pool_scale, m_w_out, m_w_gate, m_w_up, m_w_down, m_g_pre_mix, m_g_post_mix, m_g_pre_ffn, m_g_post_ffn, v_ada_w, v_ada_b, v_w_in, v_b_in, v_sinks, v_pool_w, v_pool_scale, v_w_out, v_w_gate, v_w_up, v_w_down, v_g_pre_mix, v_g_post_mix, v_g_pre_ffn, v_g_post_ffn):
    me = _index(_mesh_place())
    x0 = x[0]
    target = loss_target[0]
    rot = _rotary_tables(positions)
    ada_cols = ada_w.shape[2]

    shards = []
    for l in range(DEPTH):
        shards += [w_in[l].T.astype(BF16), w_out[l].astype(BF16), w_gate[l].T.astype(BF16), w_up[l].T.astype(BF16),
                   w_down[l].astype(BF16)]
    whole = lambda gathered: [g.reshape(-1, D_MODEL) for g in gathered]
    full = [None] * (5 * DEPTH)
    first = _exchanges_alone([_Gather(shards[0:2] + [jnp.broadcast_to(c, (8, D_MODEL))])], "gather_first")[0]
    full[0:2] = whole(first[0:2])

    mod_part, c_act = _mod_fwd(first[2][:, 0, :], ada_w)
    mod_all = _allgather_vmem(mod_part.reshape(DEPTH * N_DEV, ada_cols), "gather_mod")
    mod_mine = lax.dynamic_slice_in_dim(mod_all.reshape(N_DEV, DEPTH, N_DEV, ada_cols), me, 1, axis=2)[:, :, 0, :]
    mod = jnp.transpose(mod_mine, (1, 0, 2)).reshape(DEPTH, N_MOD * D_MODEL) + ada_b
    mod = mod.reshape(DEPTH, N_MOD, D_MODEL)

    hosted = {("fwd_in", 0): (_Gather(shards[2:3]), [2]), ("attn_fwd", 0): (_Gather(shards[3:4]), [3]),
              ("fwd_out", 0): (_Gather(shards[4:5]), [4]),
              ("ffn_fwd", 0): (_Gather(shards[5:10]), [5, 6, 7, 8, 9])}

    def take(kind, l, ex_outs):
        if (kind, l) in hosted:
            for slot, g in zip(hosted[kind, l][1], whole(ex_outs[0])):
                full[slot] = g

    def beside(kind, l):
        return [hosted[kind, l][0]] if (kind, l) in hosted else []

    pool_w_b = pool_w.astype(BF16)

    saved = []
    xl = x0
    for l in range(DEPTH):
        vec_in = _rows(g_pre_mix[l], 1.0 + mod[l, 1], mod[l, 0], width=D_MODEL)
        vec_out = _rows(mod[l, 2], g_post_mix[l], width=D_MODEL)
        vec_ffn = _rows(g_pre_ffn[l], 1.0 + mod[l, 4], mod[l, 3], mod[l, 5], g_post_ffn[l], width=D_MODEL)
        psc = _rows(pool_scale[l], width=POOL_WIDTH)
        (q, k, v, u), ex = _fwd_in(xl, vec_in, full[5 * l], _rows(b_in[l], width=IN_WIDTH), rot, f"fwd_in_{l}",
                                   beside("fwd_in", l))
        take("fwd_in", l, ex)
        (ao, lse), ex = _attn_fwd_t(sinks[l], q, k, v, f"attn_fwd_{l}", beside("attn_fwd", l))
        take("attn_fwd", l, ex)
        (x1, mix), ex = _fwd_out(ao, u, pool_w_b[l], psc, full[5 * l + 1], xl, vec_out, f"fwd_out_{l}",
                                 beside("fwd_out", l))
        take("fwd_out", l, ex)
        (x2, f, gt, up, h2, *loss_part), ex = _ffn_fwd(
            x1, vec_ffn, full[5 * l + 2], full[5 * l + 3], full[5 * l + 4], f"ffn_fwd_{l}", beside("ffn_fwd", l),
            target=target if l == DEPTH - 1 else None)
        take("ffn_fwd", l, ex)
        saved.append((xl, q, k, v, u, ao, lse, x1, mix, f, gt, up, h2, vec_in, vec_out, vec_ffn, psc))
        xl = x2

    dx = xl
    my_loss = loss_part[0][0, 0]

    stacks = [None] * (5 * DEPTH)
    parts = [None] * (5 * DEPTH)
    small_all = [None] * DEPTH
    ready_stacks, ready_small = [], []

    def leaving(row_budget):
        exs, notes = [], []
        taken = []
        while ready_stacks and stacks[ready_stacks[0]].shape[1] <= row_budget:
            row_budget -= stacks[ready_stacks[0]].shape[1]
            taken.append(ready_stacks.pop(0))
        if taken:
            exs.append(_Scatter([stacks[a] for a in taken]))
            notes.append(("stacks", taken))
        if ready_small and row_budget >= 0:
            exs.append(_Gather([ready_small[0][1]]))
            notes.append(("small", ready_small[0][0]))
            ready_small.clear()
        return exs, notes

    def arrived(notes, ex_outs):
        for (kind, what), outs in zip(notes, ex_outs):
            if kind == "stacks":
                for a, o in zip(what, outs):
                    parts[a] = o
            else:
                small_all[what] = outs[0]

    def grad_ready(a, grad):
        stacks[a] = grad.reshape(N_DEV, -1, D_MODEL)
        ready_stacks.append(a)

    def hosting(row_budget, kernel_fn, *args):
        exs, notes = leaving(row_budget)
        outs, ex_outs = kernel_fn(*args, exs)
        arrived(notes, ex_outs)
        return outs

    for l in reversed(range(DEPTH)):
        w_in_t, w_out_f, w_gate_t, w_up_t, w_down_f = full[5 * l:5 * l + 5]
        xin, q, k, v, u, ao, lse, x1, mix, f, gt, up, h2, vec_in, vec_out, vec_ffn, psc = saved[l]
        last = l == 0
        dgt, dup, d_w_down, cs_a = hosting(600, _ffn_bwd_act, dx, f, gt, up, vec_ffn, w_down_f, f"ffn_bwd_act_{l}")
        grad_ready(5 * l + 4, d_w_down.astype(BF16))
        grad_ready(5 * l + 2, hosting(-1, _weight_grad, dgt, h2, f"grad_w_gate_{l}"))
        grad_ready(5 * l + 3, hosting(-1, _weight_grad, dup, h2, f"grad_w_up_{l}"))
        dx1, cs_f = hosting(400, _ffn_bwd_in, dx, x1, dgt, dup, vec_ffn, w_gate_t, w_up_t, f"ffn_bwd_in_{l}")
        do, du, d_w_out, cs_o, dpw, dpsc = hosting(360, _out_bwd, dx1, mix, ao, u, pool_w_b[l], psc, w_out_f, vec_out,
                                                   f"out_bwd_{l}")
        grad_ready(5 * l + 1, d_w_out)
        dq, dkc, dkp, dvc, dvp, dsk = hosting(480 if last else 400, _attn_bwd_t, sinks[l], q, do, lse, k, v,
                                              f"attn_bwd_{l}")
        dx, d_w_in_t, cs_i, db = hosting(-1 if last else 250, _in_bwd, dq, dkc, dkp, dvc, dvp, du, xin, dx1, vec_in,
                                         w_in_t, rot, f"in_bwd_{l}")
        grad_ready(5 * l, d_w_in_t)
        d_mod = jnp.concatenate([cs_i[0], cs_i[1], cs_o[0], cs_f[0], cs_f[1], cs_a[0]])
        d_gain = jnp.stack([cs_i[2], cs_o[1], cs_f[2], cs_a[1]])
        ready_small.append((l, _pack_layer(d_mod, d_gain, db[0], dpsc[0], dsk[:, 0], dpw,
                                           my_loss if l == DEPTH - 1 else jnp.zeros((), F32))))
    grad_x = dx[None]
    exs, notes = leaving(N_DEV * D_FF)
    arrived(notes, _exchanges_alone(exs, "exchange_last"))

    layer_sums = [_unpack_layer(_sum_parts(small_all[l], 0, f"sum_small_{l}")) for l in range(DEPTH)]
    g_ada_b, g_gains, g_b_in, g_pool_scale, g_sinks, g_pool_w = (
        jnp.stack([layer_sums[l][j] for l in range(DEPTH)], axis=1 if j == 1 else 0) for j in range(6))
    loss = layer_sums[DEPTH - 1][6]

    dmod_all = jnp.stack([small_all[l][:, 0:N_MOD, :].reshape(N_DEV, N_DEV, ada_cols) for l in range(DEPTH)])
    dmod_cols = lax.dynamic_slice_in_dim(dmod_all, me, 1, axis=2)[:, :, 0, :]
    g_ada_w = _ada_grad(c_act, dmod_cols)

    slot = {"w_in": 0, "w_out": 1, "w_gate": 2, "w_up": 3, "w_down": 4}
    held_transposed = ("w_in", "w_gate", "w_up")
    swap = lambda a: jnp.transpose(a, (0, 2, 1))

    def step(w, g, m, v, name):
        if name in slot:
            mine = [parts[5 * l + slot[name]] for l in range(DEPTH)]
            if name in held_transposed:
                return [swap(o) for o in _sum_adamw(mine, swap(w), swap(m), swap(v), f"adamw_{name}")]
            return _sum_adamw(mine, w, m, v, f"adamw_{name}")
        flat = lambda a: a.reshape(-1, a.shape[-1])
        return [g] + [o.reshape(w.shape) for o in _adamw(flat(w), flat(g), flat(m), flat(v), f"adamw_{name}")]

    in_order = [
        ("ada_w", ada_w, g_ada_w, m_ada_w, v_ada_w), ("ada_b", ada_b, g_ada_b, m_ada_b, v_ada_b),
        ("w_in", w_in, None, m_w_in, v_w_in), ("b_in", b_in, g_b_in, m_b_in, v_b_in),
        ("sinks", sinks, g_sinks, m_sinks, v_sinks), ("pool_w", pool_w, g_pool_w, m_pool_w, v_pool_w),
        ("pool_scale", pool_scale, g_pool_scale, m_pool_scale, v_pool_scale),
        ("w_out", w_out, None, m_w_out, v_w_out), ("w_gate", w_gate, None, m_w_gate, v_w_gate),
        ("w_up", w_up, None, m_w_up, v_w_up), ("w_down", w_down, None, m_w_down, v_w_down),
        ("g_pre_mix", g_pre_mix, g_gains[0], m_g_pre_mix, v_g_pre_mix),
        ("g_post_mix", g_post_mix, g_gains[1], m_g_post_mix, v_g_post_mix),
        ("g_pre_ffn", g_pre_ffn, g_gains[2], m_g_pre_ffn, v_g_pre_ffn),
        ("g_post_ffn", g_post_ffn, g_gains[3], m_g_post_ffn, v_g_post_ffn)]
    steps = [step(w, g, m, v, n) for n, w, g, m, v in in_order]
    grads, deltas, new_m, new_v = ([s[j] for s in steps] for j in range(4))
    return (loss, grad_x, *grads, *deltas, *new_m, *new_v)
```

```python
import jax
import jax.numpy as jnp
from jax import lax
from jax.experimental import pallas as pl
from jax.experimental.pallas import tpu as pltpu

F32 = jnp.float32
BF16 = jnp.bfloat16

N_DEV = 8
DEPTH = 2
D_MODEL = 1024
HEAD_DIM = 64
N_HEADS = 8
N_KV_HEADS = 2
GROUP = N_HEADS // N_KV_HEADS
ATTN_WIDTH = N_HEADS * HEAD_DIM
KV_WIDTH = N_KV_HEADS * HEAD_DIM
POOL_WIDTH = 512
POOL_WINDOWS = (2, 4, 8, 16)
POOL_GROUP_WIDTH = 128
POOL_HALO = 16
IN_WIDTH = ATTN_WIDTH + 2 * KV_WIDTH + POOL_WIDTH
D_FF = 2816
N_MOD = 6
BLOCK = 128
ROT_DIM = 16
ROPE_THETA = 500000.0
EPS = 1e-6
NEG_INF = -1e30
Q_SCALE = HEAD_DIM ** -0.5

ADAM_LR = 0.001
ADAM_B1 = 0.9
ADAM_B2 = 0.999
ADAM_EPS = 1e-08
ADAM_WD = 0.01
ADAM_STEP = 10

LANES = 128
SEQ_TILE = 512
ATTN_BLOCKS = 8
FF_CHUNK = 256
VMEM_LIMIT = 56 * 1024 * 1024
MESH = pl.DeviceIdType.MESH

NT = (((1,), (1,)), ((), ()))
TN = (((0,), (0,)), ((), ()))


def _dot(a, b, dims=None):
    if dims is None:
        return jnp.dot(a, b, preferred_element_type=F32)
    return lax.dot_general(a, b, dims, preferred_element_type=F32)


def _cparams(n_axes):
    return pltpu.CompilerParams(dimension_semantics=("arbitrary",) * n_axes, vmem_limit_bytes=VMEM_LIMIT)


def _resident(shape):
    zeros = (0,) * len(shape)
    return pl.BlockSpec(shape, lambda *_: zeros, pipeline_mode=pl.Buffered(1))


def _rows(*vectors, width):
    rows = [jnp.reshape(v, (1, width)).astype(F32) for v in vectors]
    rows.append(jnp.zeros((8 - len(rows), width), F32))
    return jnp.concatenate(rows, axis=0)


def _rsqrt_ms(x):
    return lax.rsqrt(jnp.mean(x * x, axis=-1, keepdims=True) + EPS)


def _colsum(x):
    return jnp.sum(x, axis=0, keepdims=True)


def _seq_tile(s, tiles=1):
    return min(s, tiles * SEQ_TILE)


def _mesh_place():
    x, y, c = lax.axis_index("x"), lax.axis_index("y"), lax.axis_index("c")
    return x, y, c


def _flip(place, k):
    x, y, c = place
    return (1 - x if k & 4 else x, 1 - y if k & 2 else y, 1 - c if k & 1 else c)


def _index(place):
    x, y, c = place
    return 4 * x + 2 * y + c


def _allgather_vmem(block, name):
    r, c = block.shape

    def body(x_ref, out_ref, send_sems, recv_sems, local_sem):
        me = _mesh_place()
        mine = pltpu.make_async_copy(x_ref, out_ref.at[_index(me)], local_sem)
        mine.start()

        def copy(k):
            return pltpu.make_async_remote_copy(
                src_ref=x_ref, dst_ref=out_ref.at[_index(me)], send_sem=send_sems.at[k - 1], recv_sem=recv_sems.at[k - 1],
                device_id=_flip(me, k), device_id_type=MESH)

        def arrival(k):
            return pltpu.make_async_remote_copy(
                src_ref=x_ref, dst_ref=out_ref.at[_index(_flip(me, k))], send_sem=send_sems.at[k - 1],
                recv_sem=recv_sems.at[k - 1], device_id=_flip(me, k), device_id_type=MESH)

        for k in range(1, N_DEV):
            copy(k).start()
        for k in range(1, N_DEV):
            arrival(k).wait_recv()
        for k in range(1, N_DEV):
            copy(k).wait_send()
        mine.wait()

    return pl.pallas_call(
        body, name=name,
        out_shape=jax.ShapeDtypeStruct((N_DEV, r, c), block.dtype),
        in_specs=[pl.BlockSpec(memory_space=pltpu.VMEM)],
        out_specs=pl.BlockSpec(memory_space=pltpu.VMEM),
        scratch_shapes=[pltpu.SemaphoreType.DMA((N_DEV - 1,)), pltpu.SemaphoreType.DMA((N_DEV - 1,)),
                        pltpu.SemaphoreType.DMA],
    )(block)


class _Gather:
    def __init__(self, shards):
        n = len(shards)
        self.operands = list(shards)
        self.out_shape = [jax.ShapeDtypeStruct((N_DEV,) + s.shape, s.dtype) for s in shards]
        self.scratch = [pltpu.SemaphoreType.DMA((N_DEV - 1, n)), pltpu.SemaphoreType.DMA((N_DEV - 1, n)),
                        pltpu.SemaphoreType.DMA((n,))]

    def _copies(self, x_refs, out_refs, sems):
        send_sems, recv_sems, local_sems = sems
        n = len(x_refs)
        x, y, c = _mesh_place()
        me, sibling = (x, y, c), (x, y, 1 - c)
        chips = [(1 - x, y), (x, 1 - y), (1 - x, 1 - y)]

        def copy(k, a, block, to, from_input=False):
            rows = out_refs[a].at[_index(block)]
            return pltpu.make_async_remote_copy(
                src_ref=x_refs[a] if from_input else rows, dst_ref=rows,
                send_sem=send_sems.at[k, a], recv_sem=recv_sems.at[k, a], device_id=to, device_id_type=MESH)

        mine = [pltpu.make_async_copy(x_refs[a], out_refs[a].at[_index(me)], local_sems.at[a]) for a in range(n)]
        first = [copy(0, a, me, sibling, from_input=True) for a in range(n)]
        first += [copy(1 + j, a, me, (*chip, c), from_input=True) for j, chip in enumerate(chips) for a in range(n)]
        over_ici = [copy(1 + j, a, (*chip, c), me) for j, chip in enumerate(chips) for a in range(n)]
        passed = [copy(4 + j, a, (*chip, c), sibling) for j, chip in enumerate(chips) for a in range(n)]
        from_sibling = [copy(0, a, sibling, me) for a in range(n)]
        from_sibling += [copy(4 + j, a, (*chip, 1 - c), me) for j, chip in enumerate(chips) for a in range(n)]
        return mine, first, over_ici, passed, from_sibling

    def begin(self, x_refs, out_refs, sems):
        mine, first, _, _, _ = self._copies(x_refs, out_refs, sems)
        for cp in mine + first:
            cp.start()

    def middle(self, x_refs, out_refs, sems):
        _, _, over_ici, passed, _ = self._copies(x_refs, out_refs, sems)
        for arrived, onward in zip(over_ici, passed):
            arrived.wait_recv()
            onward.start()

    def end(self, x_refs, out_refs, sems):
        mine, first, _, passed, from_sibling = self._copies(x_refs, out_refs, sems)
        for cp in from_sibling:
            cp.wait_recv()
        for cp in first + passed:
            cp.wait_send()
        for cp in mine:
            cp.wait()


class _Scatter:
    OWN = N_DEV - 1

    def __init__(self, stacks):
        n = len(stacks)
        self.operands = list(stacks)
        self.out_shape = [jax.ShapeDtypeStruct(s.shape, s.dtype) for s in stacks]
        self.scratch = [pltpu.SemaphoreType.DMA((N_DEV - 1, n)), pltpu.SemaphoreType.DMA((N_DEV - 1, n)),
                        pltpu.SemaphoreType.DMA((n,))]

    def _copies(self, g_refs, out_refs, sems):
        send_sems, recv_sems, local_sems = sems
        me = _mesh_place()

        def copy(k, a):
            peer = _flip(me, k)
            return pltpu.make_async_remote_copy(
                src_ref=g_refs[a].at[_index(peer)], dst_ref=out_refs[a].at[k - 1],
                send_sem=send_sems.at[k - 1, a], recv_sem=recv_sems.at[k - 1, a], device_id=peer, device_id_type=MESH)

        mine = [pltpu.make_async_copy(g_refs[a].at[_index(me)], out_refs[a].at[self.OWN], local_sems.at[a])
                for a in range(len(g_refs))]
        return mine, [copy(k, a) for k in range(1, N_DEV) for a in range(len(g_refs))]

    def begin(self, g_refs, out_refs, sems):
        mine, copies = self._copies(g_refs, out_refs, sems)
        for cp in mine + copies:
            cp.start()

    def middle(self, g_refs, out_refs, sems):
        pass

    def end(self, g_refs, out_refs, sems):
        mine, copies = self._copies(g_refs, out_refs, sems)
        for cp in copies:
            cp.wait_recv()
        for cp in copies:
            cp.wait_send()
        for cp in mine:
            cp.wait()


def _call(body, args, *, name, grid, in_specs, out_specs, out_shape, scratch_shapes=(), exchanges=()):
    if not exchanges:
        outs = pl.pallas_call(body, name=name, grid=grid, in_specs=in_specs, out_specs=out_specs, out_shape=out_shape,
                              scratch_shapes=list(scratch_shapes), compiler_params=_cparams(len(grid)))(*args)
        return outs, []
    (steps,) = grid
    n_in, n_out, n_scr = len(in_specs), len(out_specs), len(scratch_shapes)
    ex_in = [len(ex.operands) for ex in exchanges]
    ex_out = [len(ex.out_shape) for ex in exchanges]
    ex_scr = [len(ex.scratch) for ex in exchanges]

    def split(refs, counts):
        parts, pos = [], 0
        for cnt in counts:
            parts.append(refs[pos:pos + cnt])
            pos += cnt
        return parts

    def wrapped(*refs):
        ins, xin, outs, xout, scr, xscr = split(refs, [n_in, sum(ex_in), n_out, sum(ex_out), n_scr, sum(ex_scr)])
        bound = list(zip(exchanges, split(xin, ex_in), split(xout, ex_out), split(xscr, ex_scr)))
        step = pl.program_id(0)

        def phase(method, at):
            @pl.when(step == at)
            def _():
                for ex, i_refs, o_refs, sems in bound:
                    getattr(ex, method)(i_refs, o_refs, sems)

        phase("begin", 0)
        phase("middle", (3 * steps) // 4)
        body(*ins, *outs, *scr)
        phase("end", steps - 1)

    any_spec = pl.BlockSpec(memory_space=pl.ANY)
    results = pl.pallas_call(
        wrapped, name=name, grid=grid,
        in_specs=list(in_specs) + [any_spec] * sum(ex_in),
        out_specs=list(out_specs) + [any_spec] * sum(ex_out),
        out_shape=list(out_shape) + [s for ex in exchanges for s in ex.out_shape],
        scratch_shapes=list(scratch_shapes) + [s for ex in exchanges for s in ex.scratch],
        compiler_params=_cparams(1),
    )(*args, *[a for ex in exchanges for a in ex.operands])
    return results[:n_out], split(results[n_out:], ex_out)


def _exchanges_alone(exchanges, name):
    def body(flag_ref):
        flag_ref[...] = jnp.zeros_like(flag_ref)

    _, ex_outs = _call(body, [], name=name, grid=(1,), in_specs=[], out_specs=[pl.BlockSpec((8, LANES), lambda i: (0, 0))],
                       out_shape=[jax.ShapeDtypeStruct((8, LANES), F32)], exchanges=exchanges)
    return ex_outs


def _mod_fwd(c_all, ada_w):
    cols = ada_w.shape[2]

    def body(c_ref, w_ref, mp_ref, act_ref):
        c = c_ref[...]
        act = c * jax.nn.sigmoid(c)
        act_ref[...] = act
        mp_ref[0] = _dot(act.astype(BF16), w_ref[0].astype(BF16))

    return pl.pallas_call(
        body, name="mod_fwd", grid=(DEPTH,),
        in_specs=[pl.BlockSpec((N_DEV, D_MODEL), lambda l: (0, 0)),
                  pl.BlockSpec((1, D_MODEL, cols), lambda l: (l, 0, 0))],
        out_specs=[pl.BlockSpec((1, N_DEV, cols), lambda l: (l, 0, 0)),
                   pl.BlockSpec((N_DEV, D_MODEL), lambda l: (0, 0))],
        out_shape=[jax.ShapeDtypeStruct((DEPTH, N_DEV, cols), F32), jax.ShapeDtypeStruct((N_DEV, D_MODEL), F32)],
        compiler_params=_cparams(1),
    )(c_all, ada_w)


def _ada_grad(c_act, dmod_cols):
    cols = dmod_cols.shape[2]

    def body(act_ref, dm_ref, g_ref):
        g_ref[0] = _dot(act_ref[...].astype(BF16), dm_ref[0].astype(BF16), TN)

    return pl.pallas_call(
        body, name="ada_grad", grid=(DEPTH,),
        in_specs=[pl.BlockSpec((N_DEV, D_MODEL), lambda l: (0, 0)),
                  pl.BlockSpec((1, N_DEV, cols), lambda l: (l, 0, 0))],
        out_specs=pl.BlockSpec((1, D_MODEL, cols), lambda l: (l, 0, 0)),
        out_shape=jax.ShapeDtypeStruct((DEPTH, D_MODEL, cols), F32),
        compiler_params=_cparams(1),
    )(c_act, dmod_cols)


def _rotary_terms(cos, sin):
    in_head = lax.broadcasted_iota(jnp.int32, (1, LANES), 1) % HEAD_DIM
    half = ROT_DIM // 2
    return cos, jnp.where(in_head < half, -sin, 0.0), jnp.where((in_head >= half) & (in_head < ROT_DIM), sin, 0.0)


def _rotate(t, cos, sin_lo, sin_hi):
    return t * cos + pltpu.roll(t, LANES - 8, 1) * sin_lo + pltpu.roll(t, 8, 1) * sin_hi


def _rotate_bwd(t, cos, sin_lo, sin_hi):
    return t * cos + pltpu.roll(t * sin_lo, 8, 1) + pltpu.roll(t * sin_hi, LANES - 8, 1)


def _fwd_in(x, vec, w_in_t, b_in, rot, name, exchanges=()):
    s = x.shape[0]
    ts = _seq_tile(s, 2)

    def body(x_ref, vec_ref, w_ref, b_ref, cos_ref, sin_ref, q_ref, k_ref, v_ref, u_ref):
        xt = x_ref[...]
        h = xt * _rsqrt_ms(xt) * vec_ref[0:1, :] * vec_ref[1:2, :] + vec_ref[2:3, :]
        proj = _dot(h.astype(BF16), w_ref[...], NT) + b_ref[0:1, :]
        cos, lo, hi = _rotary_terms(cos_ref[...], sin_ref[...])
        for j in range(ATTN_WIDTH // LANES):
            q_ref[:, j * LANES:(j + 1) * LANES] = (
                _rotate(proj[:, j * LANES:(j + 1) * LANES], cos, lo, hi) * Q_SCALE).astype(BF16)
        k_ref[...] = _rotate(proj[:, ATTN_WIDTH:ATTN_WIDTH + KV_WIDTH], cos, lo, hi).astype(BF16)
        v_ref[...] = proj[:, ATTN_WIDTH + KV_WIDTH:ATTN_WIDTH + 2 * KV_WIDTH].astype(BF16)
        u_ref[...] = proj[:, ATTN_WIDTH + 2 * KV_WIDTH:]

    row = lambda w: pl.BlockSpec((ts, w), lambda i: (i, 0))
    return _call(
        body, [x, vec, w_in_t, b_in, *rot], name=name, grid=(s // ts,),
        in_specs=[row(D_MODEL), _resident((8, D_MODEL)), _resident((IN_WIDTH, D_MODEL)), _resident((8, IN_WIDTH)),
                  row(LANES), row(LANES)],
        out_specs=[row(ATTN_WIDTH), row(KV_WIDTH), row(KV_WIDTH), row(POOL_WIDTH)],
        out_shape=[jax.ShapeDtypeStruct((s, ATTN_WIDTH), BF16), jax.ShapeDtypeStruct((s, KV_WIDTH), BF16),
                   jax.ShapeDtypeStruct((s, KV_WIDTH), BF16), jax.ShapeDtypeStruct((s, POOL_WIDTH), F32)],
        exchanges=exchanges)


def _band_mask_t(first_block):
    key = lax.broadcasted_iota(jnp.int32, (2 * BLOCK, 2 * BLOCK), 0)
    qry = lax.broadcasted_iota(jnp.int32, (2 * BLOCK, 2 * BLOCK), 1) & (BLOCK - 1)
    d = key - qry
    return (d >= 1) & (d <= BLOCK) & ((key >= BLOCK) | jnp.logical_not(first_block))


def _placed(both, kh, b):
    lane = lax.broadcasted_iota(jnp.int32, (1, LANES), 1)
    src = both if kh == b else pltpu.roll(both, HEAD_DIM, 1)
    return jnp.where((lane >= b * HEAD_DIM) & (lane < (b + 1) * HEAD_DIM), src, jnp.zeros_like(src))


def _pair(ref, rows, kh, r):
    j = 2 * kh + r
    return ref[rows, j * LANES:(j + 1) * LANES]


def _attn_specs(s):
    qb = min(ATTN_BLOCKS, s // BLOCK)
    cur = lambda w: pl.BlockSpec((qb * BLOCK, w), lambda i: (i, 0))
    prev = lambda w: pl.BlockSpec((BLOCK, w), lambda i: (jnp.maximum(i * qb - 1, 0), 0))
    return qb, cur, prev, pl.BlockSpec((N_HEADS, qb * BLOCK), lambda i: (0, i))


def _kv_window(prev_ref, cur_ref, jb):
    before = prev_ref[...] if jb == 0 else cur_ref[(jb - 1) * BLOCK:jb * BLOCK, :]
    return jnp.concatenate([before, cur_ref[jb * BLOCK:(jb + 1) * BLOCK, :]], axis=0)


def _sink_row(sink_ref, kh, b):
    lane = lax.broadcasted_iota(jnp.int32, (1, 2 * BLOCK), 1)
    return jnp.where(lane < BLOCK, sink_ref[GROUP * kh + b], sink_ref[GROUP * kh + 2 + b])


def _attn_fwd_t(sinks, q, k, v, name, exchanges=()):
    s = q.shape[0]
    qb, cur, prev, lse_spec = _attn_specs(s)

    def block(jb, sink_ref, q_ref, kc_ref, kp_ref, vc_ref, vp_ref, o_ref, l_ref):
        rows = slice(jb * BLOCK, (jb + 1) * BLOCK)
        valid = _band_mask_t(pl.program_id(0) == 0 if jb == 0 else False)
        k_both, v_both = _kv_window(kp_ref, kc_ref, jb), _kv_window(vp_ref, vc_ref, jb)
        for kh in range(N_KV_HEADS):
            qg = jnp.concatenate([_pair(q_ref, rows, kh, 0), _pair(q_ref, rows, kh, 1)], axis=0)
            keys = jnp.concatenate([_placed(k_both, kh, 0), _placed(k_both, kh, 1)], axis=0)
            st = _dot(keys, qg, NT)
            out = jnp.zeros((2 * BLOCK, LANES), F32)
            for b in range(2):
                sc = jnp.where(valid, st[b * 2 * BLOCK:(b + 1) * 2 * BLOCK, :], NEG_INF)
                sink = _sink_row(sink_ref, kh, b)
                m = jnp.maximum(jnp.max(sc, axis=0, keepdims=True), sink)
                p = jnp.exp(sc - m)
                den = jnp.sum(p, axis=0, keepdims=True) + jnp.exp(sink - m)
                out = out + _dot((p * (1.0 / den)).astype(BF16), _placed(v_both, kh, b), TN)
                lse = m + jnp.log(den)
                for r in range(2):
                    h = GROUP * kh + 2 * r + b
                    l_ref[h:h + 1, rows] = lse[:, r * BLOCK:(r + 1) * BLOCK]
            for r in range(2):
                j = 2 * kh + r
                o_ref[rows, j * LANES:(j + 1) * LANES] = out[r * BLOCK:(r + 1) * BLOCK, :].astype(BF16)

    def body(*refs):
        for jb in range(qb):
            block(jb, *refs)

    return _call(
        body, [sinks, q, k, k, v, v], name=name, grid=(s // (qb * BLOCK),),
        in_specs=[pl.BlockSpec(memory_space=pltpu.SMEM), cur(ATTN_WIDTH), cur(KV_WIDTH), prev(KV_WIDTH),
                  cur(KV_WIDTH), prev(KV_WIDTH)],
        out_specs=[cur(ATTN_WIDTH), lse_spec],
        out_shape=[jax.ShapeDtypeStruct((s, ATTN_WIDTH), BF16), jax.ShapeDtypeStruct((N_HEADS, s), F32)],
        exchanges=exchanges)


def _attn_bwd_t(sinks, q, do, lse, k, v, name, exchanges=()):
    s = q.shape[0]
    qb, cur, prev, lse_spec = _attn_specs(s)

    def block(jb, sink_ref, q_ref, do_ref, l_ref, kc_ref, kp_ref, vc_ref, vp_ref,
              dq_ref, dkc_ref, dkp_ref, dvc_ref, dvp_ref, dsink_ref):
        rows = slice(jb * BLOCK, (jb + 1) * BLOCK)
        valid = _band_mask_t(pl.program_id(0) == 0 if jb == 0 else False)
        k_both, v_both = _kv_window(kp_ref, kc_ref, jb), _kv_window(vp_ref, vc_ref, jb)
        lane = lax.broadcasted_iota(jnp.int32, (1, LANES), 1)
        col = lax.broadcasted_iota(jnp.int32, (1, 2 * BLOCK), 1)
        dk_heads, dv_heads = [], []
        for kh in range(N_KV_HEADS):
            qg = jnp.concatenate([_pair(q_ref, rows, kh, 0), _pair(q_ref, rows, kh, 1)], axis=0)
            dog = jnp.concatenate([_pair(do_ref, rows, kh, 0), _pair(do_ref, rows, kh, 1)], axis=0)
            k_placed = [_placed(k_both, kh, b) for b in range(2)]
            v_placed = [_placed(v_both, kh, b) for b in range(2)]
            st = _dot(jnp.concatenate(k_placed, axis=0), qg, NT)
            dpt = _dot(jnp.concatenate(v_placed, axis=0), dog, NT)
            dqg = jnp.zeros((2 * BLOCK, LANES), F32)
            dk_b, dv_b = [], []
            for b in range(2):
                part = slice(b * 2 * BLOCK, (b + 1) * 2 * BLOCK)
                heads = [GROUP * kh + 2 * r + b for r in range(2)]
                lse_row = jnp.concatenate([l_ref[h:h + 1, rows] for h in heads], axis=1)
                p = jnp.where(valid, jnp.exp(st[part, :] - lse_row), 0.0)
                dp = dpt[part, :]
                delta = jnp.sum(p * dp, axis=0, keepdims=True)
                sink_pull = jnp.exp(_sink_row(sink_ref, kh, b) - lse_row) * delta
                dsink_ref[heads[0]:heads[0] + 1, :] += -jnp.sum(jnp.where(col < BLOCK, sink_pull, 0.0))
                dsink_ref[heads[1]:heads[1] + 1, :] += -jnp.sum(jnp.where(col < BLOCK, 0.0, sink_pull))
                ds = (p * (dp - delta)).astype(BF16)
                dqg = dqg + _dot(ds, k_placed[b], TN)
                dk_b.append(_dot(ds, qg))
                dv_b.append(_dot(p.astype(BF16), dog))
            for parts, total in ((dk_b, dk_heads), (dv_b, dv_heads)):
                kept = jnp.where(lane < HEAD_DIM, parts[0], parts[1])
                total.append(kept + pltpu.roll(kept, HEAD_DIM, 1))
            for r in range(2):
                j = 2 * kh + r
                dq_ref[rows, j * LANES:(j + 1) * LANES] = dqg[r * BLOCK:(r + 1) * BLOCK, :] * Q_SCALE
        dk = jnp.where(lane < HEAD_DIM, dk_heads[0], dk_heads[1])
        dv = jnp.where(lane < HEAD_DIM, dv_heads[0], dv_heads[1])
        dkp_ref[rows, :] = dk[0:BLOCK, :]
        dkc_ref[rows, :] = dk[BLOCK:, :]
        dvp_ref[rows, :] = dv[0:BLOCK, :]
        dvc_ref[rows, :] = dv[BLOCK:, :]

    def body(*refs):
        @pl.when(pl.program_id(0) == 0)
        def _():
            refs[-1][...] = jnp.zeros_like(refs[-1])

        for jb in range(qb):
            block(jb, *refs)

    kv = jax.ShapeDtypeStruct((s, KV_WIDTH), F32)
    return _call(
        body, [sinks, q, do, lse, k, k, v, v], name=name, grid=(s // (qb * BLOCK),),
        in_specs=[pl.BlockSpec(memory_space=pltpu.SMEM), cur(ATTN_WIDTH), cur(ATTN_WIDTH), lse_spec,
                  cur(KV_WIDTH), prev(KV_WIDTH), cur(KV_WIDTH), prev(KV_WIDTH)],
        out_specs=[cur(ATTN_WIDTH), cur(KV_WIDTH), cur(KV_WIDTH), cur(KV_WIDTH), cur(KV_WIDTH),
                   pl.BlockSpec((8, LANES), lambda i: (0, 0))],
        out_shape=[jax.ShapeDtypeStruct((s, ATTN_WIDTH), F32), kv, kv, kv, kv, jax.ShapeDtypeStruct((8, LANES), F32)],
        exchanges=exchanges)


def _pool_counts(tile, ts):
    t = (tile * ts + lax.broadcasted_iota(jnp.int32, (ts, 1), 0) + 1).astype(F32)
    return [jnp.minimum(t, float(w)) for w in POOL_WINDOWS]


def _pooled(u, halo, counts, gi):
    cols = slice(gi * POOL_GROUP_WIDTH, (gi + 1) * POOL_GROUP_WIDTH)
    acc = jnp.concatenate([halo[:, cols], u[:, cols]], axis=0)
    shift = 1
    while shift < POOL_WINDOWS[gi]:
        acc = acc + pltpu.roll(acc, shift, 0)
        shift *= 2
    return acc[POOL_HALO:, :] / counts[gi] - u[:, cols]


def _fwd_out(ao, u, pool_w, psc, w_out, x, vec, name, exchanges=()):
    s = x.shape[0]
    ts = _seq_tile(s, 2)
    hb = ts // POOL_HALO

    def body(ao_ref, u_ref, uh_ref, pw_ref, psc_ref, w_ref, x_ref, vec_ref, x1_ref, mix_ref):
        i = pl.program_id(0)
        u_t = u_ref[...]
        halo = jnp.where(i > 0, uh_ref[...], 0.0)
        counts = _pool_counts(i, ts)
        cat = [ao_ref[...]]
        for gi in range(len(POOL_WINDOWS)):
            cols = slice(gi * POOL_GROUP_WIDTH, (gi + 1) * POOL_GROUP_WIDTH)
            og = _dot(_pooled(u_t, halo, counts, gi).astype(BF16), pw_ref[gi]) * psc_ref[0:1, cols]
            cat.append(og.astype(BF16))
        mix = _dot(jnp.concatenate(cat, axis=1), w_ref[...])
        mix_ref[...] = mix
        x1_ref[...] = x_ref[...] + vec_ref[0:1, :] * (mix * _rsqrt_ms(mix) * vec_ref[1:2, :])

    row = lambda w: pl.BlockSpec((ts, w), lambda i: (i, 0))
    return _call(
        body, [ao, u, u, pool_w, psc, w_out, x, vec], name=name, grid=(s // ts,),
        in_specs=[row(ATTN_WIDTH), row(POOL_WIDTH),
                  pl.BlockSpec((POOL_HALO, POOL_WIDTH), lambda i: (jnp.maximum(i * hb - 1, 0), 0)),
                  _resident(pool_w.shape), _resident((8, POOL_WIDTH)), _resident((D_MODEL, D_MODEL)),
                  row(D_MODEL), _resident((8, D_MODEL))],
        out_specs=[row(D_MODEL), row(D_MODEL)],
        out_shape=[jax.ShapeDtypeStruct((s, D_MODEL), F32), jax.ShapeDtypeStruct((s, D_MODEL), F32)],
        exchanges=exchanges)


def _ffn_fwd(x1, vec, w_gate_t, w_up_t, w_down, name, exchanges=(), target=None):
    s = x1.shape[0]
    ts = _seq_tile(s)

    def body(x_ref, *refs):
        if target is None:
            vec_ref, wg_ref, wu_ref, wd_ref, x2_ref, f_ref, gt_ref, up_ref, h2_ref = refs
        else:
            t_ref, vec_ref, wg_ref, wu_ref, wd_ref, x2_ref, f_ref, gt_ref, up_ref, h2_ref, loss_ref = refs
        xt = x_ref[...]
        h2 = (xt * _rsqrt_ms(xt) * vec_ref[0:1, :] * vec_ref[1:2, :] + vec_ref[2:3, :]).astype(BF16)
        h2_ref[...] = h2
        f = jnp.zeros((ts, D_MODEL), F32)
        for c in range(D_FF // FF_CHUNK):
            cols = slice(c * FF_CHUNK, (c + 1) * FF_CHUNK)
            g = _dot(h2, wg_ref[cols, :], NT)
            up = _dot(h2, wu_ref[cols, :], NT)
            act = ((g * jax.nn.sigmoid(g)) * up).astype(BF16)
            gt_ref[:, cols] = g.astype(BF16)
            up_ref[:, cols] = up.astype(BF16)
            f = f + _dot(act, wd_ref[cols, :])
        f_ref[...] = f
        x2 = xt + vec_ref[3:4, :] * (f * _rsqrt_ms(f) * vec_ref[4:5, :])
        if target is None:
            x2_ref[...] = x2
        else:
            @pl.when(pl.program_id(0) == 0)
            def _():
                loss_ref[...] = jnp.zeros_like(loss_ref)

            diff = x2 - t_ref[...]
            x2_ref[...] = diff / D_MODEL
            loss_ref[...] += 0.5 * jnp.sum(jnp.mean(diff * diff, axis=-1, keepdims=True))

    row = lambda w: pl.BlockSpec((ts, w), lambda i: (i, 0))
    wide = jax.ShapeDtypeStruct((s, D_FF), BF16)
    with_loss = target is not None
    return _call(
        body, [x1] + ([target] if with_loss else []) + [vec, w_gate_t, w_up_t, w_down], name=name, grid=(s // ts,),
        in_specs=[row(D_MODEL)] * (2 if with_loss else 1) + [
            _resident((8, D_MODEL)), _resident((D_FF, D_MODEL)), _resident((D_FF, D_MODEL)), _resident((D_FF, D_MODEL))],
        out_specs=[row(D_MODEL), row(D_MODEL), row(D_FF), row(D_FF), row(D_MODEL)] + (
            [pl.BlockSpec((8, LANES), lambda i: (0, 0))] if with_loss else []),
        out_shape=[jax.ShapeDtypeStruct((s, D_MODEL), F32), jax.ShapeDtypeStruct((s, D_MODEL), F32), wide, wide,
                   jax.ShapeDtypeStruct((s, D_MODEL), BF16)] + (
            [jax.ShapeDtypeStruct((8, LANES), F32)] if with_loss else []),
        exchanges=exchanges)


def _norm_bwd(dy_hat, x_hat, r):
    return r * (dy_hat - x_hat * jnp.mean(dy_hat * x_hat, axis=-1, keepdims=True))


def _accumulate_rows(ref, rows):
    for j, val in enumerate(rows):
        ref[j:j + 1, :] += val


def _ffn_bwd_act(dx2, f, gt, up, vec, w_down, name, exchanges=()):
    s = dx2.shape[0]
    ts = _seq_tile(s)

    def body(dx2_ref, f_ref, gt_ref, up_ref, vec_ref, wd_ref, dgt_ref, dup_ref, dwd_ref, cs_ref):
        @pl.when(pl.program_id(0) == 0)
        def _():
            cs_ref[...] = jnp.zeros_like(cs_ref)
            dwd_ref[...] = jnp.zeros_like(dwd_ref)

        dx2_t, f_t = dx2_ref[...], f_ref[...]
        gate, g_post = vec_ref[3:4, :], vec_ref[4:5, :]
        rf = _rsqrt_ms(f_t)
        f_hat = f_t * rf
        df = _norm_bwd(dx2_t * gate * g_post, f_hat, rf).astype(BF16)
        _accumulate_rows(cs_ref, [_colsum(dx2_t * (f_hat * g_post)), _colsum(dx2_t * gate * f_hat)])
        for c in range(D_FF // FF_CHUNK):
            cols = slice(c * FF_CHUNK, (c + 1) * FF_CHUNK)
            dact = _dot(df, wd_ref[cols, :], NT)
            g, u_t = gt_ref[:, cols].astype(F32), up_ref[:, cols].astype(F32)
            sg = 0.5 * jnp.tanh(0.5 * g) + 0.5
            silu = g * sg
            dgt_ref[:, cols] = (dact * u_t * (sg * (1.0 + g * (1.0 - sg)))).astype(BF16)
            dup_ref[:, cols] = (dact * silu).astype(BF16)
            dwd_ref[cols, :] += _dot((silu * u_t).astype(BF16), df, TN)

    row = lambda w: pl.BlockSpec((ts, w), lambda i: (i, 0))
    wide = jax.ShapeDtypeStruct((s, D_FF), BF16)
    return _call(
        body, [dx2, f, gt, up, vec, w_down], name=name, grid=(s // ts,),
        in_specs=[row(D_MODEL), row(D_MODEL), row(D_FF), row(D_FF), _resident((8, D_MODEL)), _resident((D_FF, D_MODEL))],
        out_specs=[row(D_FF), row(D_FF), _resident((D_FF, D_MODEL)), pl.BlockSpec((8, D_MODEL), lambda i: (0, 0))],
        out_shape=[wide, wide, jax.ShapeDtypeStruct((D_FF, D_MODEL), F32), jax.ShapeDtypeStruct((8, D_MODEL), F32)],
        exchanges=exchanges)


def _ffn_bwd_in(dx2, x1, dgt, dup, vec, w_gate_t, w_up_t, name, exchanges=()):
    s = x1.shape[0]
    ts = _seq_tile(s)

    def body(dx2_ref, x1_ref, dgt_ref, dup_ref, vec_ref, wg_ref, wu_ref, dx1_ref, cs_ref):
        @pl.when(pl.program_id(0) == 0)
        def _():
            cs_ref[...] = jnp.zeros_like(cs_ref)

        g_pre, one_scale = vec_ref[0:1, :], vec_ref[1:2, :]
        dh2 = _dot(dgt_ref[...], wg_ref[...]) + _dot(dup_ref[...], wu_ref[...])
        x1_t = x1_ref[...]
        r1 = _rsqrt_ms(x1_t)
        x_hat = x1_t * r1
        dx1_ref[...] = dx2_ref[...] + _norm_bwd(dh2 * g_pre * one_scale, x_hat, r1)
        _accumulate_rows(cs_ref, [_colsum(dh2), _colsum(dh2 * (x_hat * g_pre)), _colsum(dh2 * one_scale * x_hat)])

    row = lambda w: pl.BlockSpec((ts, w), lambda i: (i, 0))
    return _call(
        body, [dx2, x1, dgt, dup, vec, w_gate_t, w_up_t], name=name, grid=(s // ts,),
        in_specs=[row(D_MODEL), row(D_MODEL), row(D_FF), row(D_FF), _resident((8, D_MODEL)),
                  _resident((D_FF, D_MODEL)), _resident((D_FF, D_MODEL))],
        out_specs=[row(D_MODEL), pl.BlockSpec((8, D_MODEL), lambda i: (0, 0))],
        out_shape=[jax.ShapeDtypeStruct((s, D_MODEL), F32), jax.ShapeDtypeStruct((8, D_MODEL), F32)],
        exchanges=exchanges)


def _weight_grad(a, b, name, exchanges=()):
    s, m = a.shape
    n = b.shape[1]
    tk = min(s, 4 * SEQ_TILE)
    steps = s // tk

    def body(a_ref, b_ref, o_ref, acc_ref):
        i = pl.program_id(0)

        @pl.when(i == 0)
        def _():
            acc_ref[...] = jnp.zeros_like(acc_ref)

        b_t = b_ref[...]
        for c in range(m // FF_CHUNK):
            rows = slice(c * FF_CHUNK, (c + 1) * FF_CHUNK)
            acc_ref[rows, :] += _dot(a_ref[:, rows], b_t, TN)

        @pl.when(i == steps - 1)
        def _():
            o_ref[...] = acc_ref[...].astype(BF16)

    (grad,), ex_outs = _call(
        body, [a, b], name=name, grid=(steps,),
        in_specs=[pl.BlockSpec((tk, m), lambda i: (i, 0)), pl.BlockSpec((tk, n), lambda i: (i, 0))],
        out_specs=[_resident((m, n))],
        out_shape=[jax.ShapeDtypeStruct((m, n), BF16)],
        scratch_shapes=[pltpu.VMEM((m, n), F32)],
        exchanges=exchanges)
    return grad, ex_outs


def _out_bwd(dx1, mix, ao, u, pool_w, psc, w_out, vec, name, exchanges=()):
    s = dx1.shape[0]
    ts = _seq_tile(s, 2)
    nt = s // ts
    hb = ts // POOL_HALO
    ng = len(POOL_WINDOWS)

    def body(dx1_ref, mix_ref, ao_ref, u_ref, uh_ref, pw_ref, psc_ref, w_ref, vec_ref,
             do_ref, du_ref, dw_out_ref, cs_ref, dpw_ref, dpsc_ref, carry_ref, dw_ref):
        i = pl.program_id(0)
        tile = nt - 1 - i

        @pl.when(i == 0)
        def _():
            dw_ref[...] = jnp.zeros_like(dw_ref)
            cs_ref[...] = jnp.zeros_like(cs_ref)
            dpw_ref[...] = jnp.zeros_like(dpw_ref)
            dpsc_ref[...] = jnp.zeros_like(dpsc_ref)
            carry_ref[...] = jnp.zeros_like(carry_ref)

        dx1_t, mix_t = dx1_ref[...], mix_ref[...]
        gate, g_post = vec_ref[0:1, :], vec_ref[1:2, :]
        rm = _rsqrt_ms(mix_t)
        m_hat = mix_t * rm
        dmix = _norm_bwd(dx1_t * gate * g_post, m_hat, rm).astype(BF16)
        _accumulate_rows(cs_ref, [_colsum(dx1_t * (m_hat * g_post)), _colsum(dx1_t * gate * m_hat)])
        d_cat = _dot(dmix, w_ref[...], NT)
        do_ref[...] = d_cat[:, 0:ATTN_WIDTH].astype(BF16)

        u_t = u_ref[...]
        halo = jnp.where(tile > 0, uh_ref[...], 0.0)
        counts = _pool_counts(tile, ts)
        cat = [ao_ref[...]]
        for gi in range(ng):
            cols = slice(gi * POOL_GROUP_WIDTH, (gi + 1) * POOL_GROUP_WIDTH)
            scale = psc_ref[0:1, cols]
            pooled = _pooled(u_t, halo, counts, gi).astype(BF16)
            og = _dot(pooled, pw_ref[gi])
            cat.append((og * scale).astype(BF16))
            d_out = d_cat[:, ATTN_WIDTH + gi * POOL_GROUP_WIDTH:ATTN_WIDTH + (gi + 1) * POOL_GROUP_WIDTH]
            dpsc_ref[0:1, cols] += _colsum(d_out * og)
            d_og = (d_out * scale).astype(BF16)
            dpw_ref[gi] += _dot(pooled, d_og, TN)
            d_pooled = _dot(d_og, pw_ref[gi], NT)
            spread = d_pooled / counts[gi]
            acc = jnp.concatenate([spread, carry_ref[:, cols]], axis=0)
            shift = 1
            while shift < POOL_WINDOWS[gi]:
                acc = acc + pltpu.roll(acc, ts + POOL_HALO - shift, 0)
                shift *= 2
            du_ref[:, cols] = acc[0:ts, :] - d_pooled
            carry_ref[:, cols] = spread[0:POOL_HALO, :]
        dw_ref[...] += _dot(jnp.concatenate(cat, axis=1), dmix, TN)

        @pl.when(i == nt - 1)
        def _():
            dw_out_ref[...] = dw_ref[...].astype(BF16)

    row = lambda w: pl.BlockSpec((ts, w), lambda i: (nt - 1 - i, 0))
    fixed = lambda shape: pl.BlockSpec(shape, lambda i: (0,) * len(shape))
    return _call(
        body, [dx1, mix, ao, u, u, pool_w, psc, w_out, vec], name=name, grid=(nt,),
        in_specs=[row(D_MODEL), row(D_MODEL), row(ATTN_WIDTH), row(POOL_WIDTH),
                  pl.BlockSpec((POOL_HALO, POOL_WIDTH), lambda i: (jnp.maximum((nt - 1 - i) * hb - 1, 0), 0)),
                  _resident(pool_w.shape), _resident((8, POOL_WIDTH)), _resident((D_MODEL, D_MODEL)),
                  _resident((8, D_MODEL))],
        out_specs=[row(ATTN_WIDTH), row(POOL_WIDTH), _resident((D_MODEL, D_MODEL)), fixed((8, D_MODEL)),
                   fixed(pool_w.shape), fixed((8, POOL_WIDTH))],
        out_shape=[jax.ShapeDtypeStruct((s, ATTN_WIDTH), BF16), jax.ShapeDtypeStruct((s, POOL_WIDTH), F32),
                   jax.ShapeDtypeStruct((D_MODEL, D_MODEL), BF16),
                   jax.ShapeDtypeStruct((8, D_MODEL), F32), jax.ShapeDtypeStruct(pool_w.shape, F32),
                   jax.ShapeDtypeStruct((8, POOL_WIDTH), F32)],
        scratch_shapes=[pltpu.VMEM((POOL_HALO, POOL_WIDTH), F32), pltpu.VMEM((D_MODEL, D_MODEL), F32)],
        exchanges=exchanges)


def _in_bwd(dq, dkc, dkp, dvc, dvp, du, x, dx1, vec, w_in_t, rot, name, exchanges=()):
    s = x.shape[0]
    ts = _seq_tile(s)
    nt = s // ts
    bpt = ts // BLOCK

    def body(dq_ref, dkc_ref, dkp_ref, dkn_ref, dvc_ref, dvp_ref, dvn_ref, du_ref, x_ref, dx1_ref, vec_ref, w_ref,
             cos_ref, sin_ref, dx_ref, dw_out_ref, cs_ref, db_ref, dw_ref):
        i = pl.program_id(0)

        @pl.when(i == 0)
        def _():
            dw_ref[...] = jnp.zeros_like(dw_ref)
            cs_ref[...] = jnp.zeros_like(cs_ref)
            db_ref[...] = jnp.zeros_like(db_ref)

        cos, lo, hi = _rotary_terms(cos_ref[...], sin_ref[...])

        def with_next_block(cur_ref, prev_ref, next_ref):
            nxt = jnp.where(i < nt - 1, next_ref[...], 0.0)
            later = nxt if bpt == 1 else jnp.concatenate([prev_ref[BLOCK:, :], nxt], axis=0)
            return cur_ref[...] + later

        pieces = [_rotate_bwd(dq_ref[:, j * LANES:(j + 1) * LANES], cos, lo, hi) for j in range(ATTN_WIDTH // LANES)]
        pieces.append(_rotate_bwd(with_next_block(dkc_ref, dkp_ref, dkn_ref), cos, lo, hi))
        pieces.append(with_next_block(dvc_ref, dvp_ref, dvn_ref))
        pieces.append(du_ref[...])
        dproj = jnp.concatenate(pieces, axis=1)
        db_ref[0:1, :] += _colsum(dproj)
        dproj_b = dproj.astype(BF16)
        dh = _dot(dproj_b, w_ref[...])

        xt = x_ref[...]
        g_pre, one_scale = vec_ref[0:1, :], vec_ref[1:2, :]
        r = _rsqrt_ms(xt)
        x_hat = xt * r
        dw_ref[...] += _dot(dproj_b, (x_hat * g_pre * one_scale + vec_ref[2:3, :]).astype(BF16), TN)
        dx_ref[...] = dx1_ref[...] + _norm_bwd(dh * g_pre * one_scale, x_hat, r)
        _accumulate_rows(cs_ref, [_colsum(dh), _colsum(dh * (x_hat * g_pre)), _colsum(dh * one_scale * x_hat)])

        @pl.when(i == nt - 1)
        def _():
            dw_out_ref[...] = dw_ref[...].astype(BF16)

    row = lambda w: pl.BlockSpec((ts, w), lambda i: (i, 0))
    nxt = pl.BlockSpec((BLOCK, KV_WIDTH), lambda i: (jnp.minimum((i + 1) * bpt, s // BLOCK - 1), 0))
    fixed = lambda shape: pl.BlockSpec(shape, lambda i: (0,) * len(shape))
    return _call(
        body, [dq, dkc, dkp, dkp, dvc, dvp, dvp, du, x, dx1, vec, w_in_t, *rot], name=name, grid=(nt,),
        in_specs=[row(ATTN_WIDTH), row(KV_WIDTH), row(KV_WIDTH), nxt, row(KV_WIDTH), row(KV_WIDTH), nxt,
                  row(POOL_WIDTH), row(D_MODEL), row(D_MODEL), _resident((8, D_MODEL)), _resident((IN_WIDTH, D_MODEL)),
                  row(LANES), row(LANES)],
        out_specs=[row(D_MODEL), _resident((IN_WIDTH, D_MODEL)), fixed((8, D_MODEL)), fixed((8, IN_WIDTH))],
        out_shape=[jax.ShapeDtypeStruct((s, D_MODEL), F32), jax.ShapeDtypeStruct((IN_WIDTH, D_MODEL), BF16),
                   jax.ShapeDtypeStruct((8, D_MODEL), F32), jax.ShapeDtypeStruct((8, IN_WIDTH), F32)],
        scratch_shapes=[pltpu.VMEM((IN_WIDTH, D_MODEL), F32)],
        exchanges=exchanges)


def _row_tile(rows):
    for t in (512, 352, 256, 176, 160, 128, 64, 32, 16, 8):
        if rows % t == 0:
            return t
    raise ValueError(f"no row tile for {rows} rows")


def _sum_parts(parts, first, name):
    n, r, c = parts.shape
    tr = _row_tile(r)

    def body(p_ref, o_ref):
        acc = p_ref[first].astype(F32)
        for k in range(n):
            if k != first:
                acc = acc + p_ref[k].astype(F32)
        o_ref[...] = acc

    return pl.pallas_call(
        body, name=name, grid=(r // tr,),
        in_specs=[pl.BlockSpec((n, tr, c), lambda i: (0, i, 0))],
        out_specs=pl.BlockSpec((tr, c), lambda i: (i, 0)),
        out_shape=jax.ShapeDtypeStruct((r, c), F32),
        compiler_params=_cparams(1),
    )(parts)


def _adam_step(w, g, m, v):
    m_new = ADAM_B1 * m + (1.0 - ADAM_B1) * g
    v_new = ADAM_B2 * v + (1.0 - ADAM_B2) * (g * g)
    m_hat = m_new / (1.0 - ADAM_B1 ** ADAM_STEP)
    v_hat = v_new / (1.0 - ADAM_B2 ** ADAM_STEP)
    return -ADAM_LR * (m_hat / (jnp.sqrt(v_hat) + ADAM_EPS) + ADAM_WD * w), m_new, v_new


def _sum_adamw(parts, w, m, v, name):
    depth, r, c = w.shape
    tr = _row_tile(r)

    def body(*refs):
        p_refs = refs[:depth]
        w_ref, m_ref, v_ref, g_ref, d_ref, nm_ref, nv_ref = refs[depth:]
        g = None
        for layer, p_ref in enumerate(p_refs):
            acc = p_ref[_Scatter.OWN].astype(F32)
            for k in range(N_DEV - 1):
                acc = acc + p_ref[k].astype(F32)
            g = acc if g is None else jnp.where(pl.program_id(0) == layer, acc, g)
        g_ref[0] = g
        d_ref[0], nm_ref[0], nv_ref[0] = _adam_step(w_ref[0], g, m_ref[0], v_ref[0])

    spec = pl.BlockSpec((1, tr, c), lambda l, i: (l, i, 0))
    shape = jax.ShapeDtypeStruct((depth, r, c), F32)
    return pl.pallas_call(
        body, name=name, grid=(depth, r // tr),
        in_specs=[pl.BlockSpec((N_DEV, tr, c), lambda l, i: (0, i, 0))] * depth + [spec] * 3,
        out_specs=[spec] * 4, out_shape=[shape] * 4, compiler_params=_cparams(2),
    )(*parts, w, m, v)


def _adamw(w, g, m, v, name):
    r, c = w.shape
    tr = _row_tile(r) if r % 8 == 0 else r

    def body(w_ref, g_ref, m_ref, v_ref, d_ref, nm_ref, nv_ref):
        d_ref[...], nm_ref[...], nv_ref[...] = _adam_step(w_ref[...], g_ref[...], m_ref[...], v_ref[...])

    spec = pl.BlockSpec((tr, c), lambda i: (i, 0))
    shape = jax.ShapeDtypeStruct((r, c), F32)
    return pl.pallas_call(
        body, name=name, grid=(r // tr,), in_specs=[spec] * 4, out_specs=[spec] * 3, out_shape=[shape] * 3,
        compiler_params=_cparams(1),
    )(w, g, m, v)


LAYER_ROWS = 80


def _pack_layer(dmod, gains, b_in, pool_scale, sinks, pool_w, loss):
    misc = jnp.concatenate([pool_scale, sinks, jnp.reshape(loss, (1,)),
                            jnp.zeros((D_MODEL - POOL_WIDTH - N_HEADS - 1,), F32)])
    parts = [dmod.reshape(N_MOD, D_MODEL), gains, jnp.pad(b_in, (0, 2 * D_MODEL - IN_WIDTH)).reshape(2, D_MODEL),
             misc.reshape(1, D_MODEL), pool_w.reshape(-1, D_MODEL)]
    packed = jnp.concatenate(parts, axis=0)
    return jnp.pad(packed, ((0, LAYER_ROWS - packed.shape[0]), (0, 0)))


def _unpack_layer(p):
    r = N_MOD + 4
    b_in = p[r:r + 2].reshape(2 * D_MODEL)[:IN_WIDTH]
    pool_w = p[r + 3:r + 3 + 64].reshape(len(POOL_WINDOWS), POOL_GROUP_WIDTH, POOL_GROUP_WIDTH)
    return (p[0:N_MOD].reshape(N_MOD * D_MODEL), p[N_MOD:r], b_in, p[r + 2, :POOL_WIDTH],
            p[r + 2, POOL_WIDTH:POOL_WIDTH + N_HEADS], pool_w, p[r + 2, POOL_WIDTH + N_HEADS])


def _rotary_tables(positions):
    inv_freq = ROPE_THETA ** (-jnp.arange(0, ROT_DIM, 2, dtype=F32) / ROT_DIM)
    head_freq = jnp.concatenate([inv_freq, inv_freq, jnp.zeros((HEAD_DIM - ROT_DIM,), F32)])
    lane_freq = jnp.concatenate([head_freq, head_freq])
    ang = positions.reshape(-1).astype(F32)[:, None] * lane_freq[None, :]
    return jnp.cos(ang), jnp.sin(ang)


def kernel(x, c, positions, ada_w, ada_b, w_in, b_in, sinks, pool_w, pool_scale, w_out, w_gate, w_up, w_down, g_pre_mix, g_post_mix, g_pre_ffn, g_post_ffn, loss_target, m_ada_w, m_ada_b, m_w_in, m_b_in, m_sinks, m_pool_w, m_pool_scale, m_w_out, m_w_gate, m_w_up, m_w_down, m_g_pre_mix, m_g_post_mix, m_g_pre_ffn, m_g_post_ffn, v_ada_w, v_ada_b, v_w_in, v_b_in, v_sinks, v_pool_w, v_pool_scale, v_w_out, v_w_gate, v_w_up, v_w_down, v_g_pre_mix, v_g_post_mix, v_g_pre_ffn, v_g_post_ffn):
    me = _index(_mesh_place())
    x0 = x[0]
    target = loss_target[0]
    rot = _rotary_tables(positions)
    ada_cols = ada_w.shape[2]

    shards = []
    for l in range(DEPTH):
        shards += [w_in[l].T.astype(BF16), w_out[l].astype(BF16), w_gate[l].T.astype(BF16), w_up[l].T.astype(BF16),
                   w_down[l].astype(BF16)]
    whole = lambda gathered: [g.reshape(-1, D_MODEL) for g in gathered]
    full = [None] * (5 * DEPTH)
    first = _exchanges_alone([_Gather(shards[0:2] + [jnp.broadcast_to(c, (8, D_MODEL))])], "gather_first")[0]
    full[0:2] = whole(first[0:2])

    mod_part, c_act = _mod_fwd(first[2][:, 0, :], ada_w)
    mod_all = _allgather_vmem(mod_part.reshape(DEPTH * N_DEV, ada_cols), "gather_mod")
    mod_mine = lax.dynamic_slice_in_dim(mod_all.reshape(N_DEV, DEPTH, N_DEV, ada_cols), me, 1, axis=2)[:, :, 0, :]
    mod = jnp.transpose(mod_mine, (1, 0, 2)).reshape(DEPTH, N_MOD * D_MODEL) + ada_b
    mod = mod.reshape(DEPTH, N_MOD, D_MODEL)

    hosted = {("fwd_in", 0): (_Gather(shards[2:3]), [2]), ("attn_fwd", 0): (_Gather(shards[3:4]), [3]),
              ("fwd_out", 0): (_Gather(shards[4:5]), [4]),
              ("ffn_fwd", 0): (_Gather(shards[5:10]), [5, 6, 7, 8, 9])}

    def take(kind, l, ex_outs):
        if (kind, l) in hosted:
            for slot, g in zip(hosted[kind, l][1], whole(ex_outs[0])):
                full[slot] = g

    def beside(kind, l):
        return [hosted[kind, l][0]] if (kind, l) in hosted else []

    pool_w_b = pool_w.astype(BF16)

    saved = []
    xl = x0
    for l in range(DEPTH):
        vec_in = _rows(g_pre_mix[l], 1.0 + mod[l, 1], mod[l, 0], width=D_MODEL)
        vec_out = _rows(mod[l, 2], g_post_mix[l], width=D_MODEL)
        vec_ffn = _rows(g_pre_ffn[l], 1.0 + mod[l, 4], mod[l, 3], mod[l, 5], g_post_ffn[l], width=D_MODEL)
        psc = _rows(pool_scale[l], width=POOL_WIDTH)
        (q, k, v, u), ex = _fwd_in(xl, vec_in, full[5 * l], _rows(b_in[l], width=IN_WIDTH), rot, f"fwd_in_{l}",
                                   beside("fwd_in", l))
        take("fwd_in", l, ex)
        (ao, lse), ex = _attn_fwd_t(sinks[l], q, k, v, f"attn_fwd_{l}", beside("attn_fwd", l))
        take("attn_fwd", l, ex)
        (x1, mix), ex = _fwd_out(ao, u, pool_w_b[l], psc, full[5 * l + 1], xl, vec_out, f"fwd_out_{l}",
                                 beside("fwd_out", l))
        take("fwd_out", l, ex)
        (x2, f, gt, up, h2, *loss_part), ex = _ffn_fwd(
            x1, vec_ffn, full[5 * l + 2], full[5 * l + 3], full[5 * l + 4], f"ffn_fwd_{l}", beside("ffn_fwd", l),
            target=target if l == DEPTH - 1 else None)
        take("ffn_fwd", l, ex)
        saved.append((xl, q, k, v, u, ao, lse, x1, mix, f, gt, up, h2, vec_in, vec_out, vec_ffn, psc))
        xl = x2

    dx = xl
    my_loss = loss_part[0][0, 0]

    stacks = [None] * (5 * DEPTH)
    parts = [None] * (5 * DEPTH)
    small_all = [None] * DEPTH
    ready_stacks, ready_small = [], []

    def leaving(row_budget):
        exs, notes = [], []
        taken = []
        while ready_stacks and stacks[ready_stacks[0]].shape[1] <= row_budget:
            row_budget -= stacks[ready_stacks[0]].shape[1]
            taken.append(ready_stacks.pop(0))
        if taken:
            exs.append(_Scatter([stacks[a] for a in taken]))
            notes.append(("stacks", taken))
        if ready_small and row_budget >= 0:
            exs.append(_Gather([ready_small[0][1]]))
            notes.append(("small", ready_small[0][0]))
            ready_small.clear()
        return exs, notes

    def arrived(notes, ex_outs):
        for (kind, what), outs in zip(notes, ex_outs):
            if kind == "stacks":
                for a, o in zip(what, outs):
                    parts[a] = o
            else:
                small_all[what] = outs[0]

    def grad_ready(a, grad):
        stacks[a] = grad.reshape(N_DEV, -1, D_MODEL)
        ready_stacks.append(a)

    def hosting(row_budget, kernel_fn, *args):
        exs, notes = leaving(row_budget)
        outs, ex_outs = kernel_fn(*args, exs)
        arrived(notes, ex_outs)
        return outs

    for l in reversed(range(DEPTH)):
        w_in_t, w_out_f, w_gate_t, w_up_t, w_down_f = full[5 * l:5 * l + 5]
        xin, q, k, v, u, ao, lse, x1, mix, f, gt, up, h2, vec_in, vec_out, vec_ffn, psc = saved[l]
        last = l == 0
        dgt, dup, d_w_down, cs_a = hosting(600, _ffn_bwd_act, dx, f, gt, up, vec_ffn, w_down_f, f"ffn_bwd_act_{l}")
        grad_ready(5 * l + 4, d_w_down.astype(BF16))
        grad_ready(5 * l + 2, hosting(-1, _weight_grad, dgt, h2, f"grad_w_gate_{l}"))
        grad_ready(5 * l + 3, hosting(-1, _weight_grad, dup, h2, f"grad_w_up_{l}"))
        dx1, cs_f = hosting(400, _ffn_bwd_in, dx, x1, dgt, dup, vec_ffn, w_gate_t, w_up_t, f"ffn_bwd_in_{l}")
        do, du, d_w_out, cs_o, dpw, dpsc = hosting(360, _out_bwd, dx1, mix, ao, u, pool_w_b[l], psc, w_out_f, vec_out,
                                                   f"out_bwd_{l}")
        grad_ready(5 * l + 1, d_w_out)
        dq, dkc, dkp, dvc, dvp, dsk = hosting(480 if last else 400, _attn_bwd_t, sinks[l], q, do, lse, k, v,
                                              f"attn_bwd_{l}")
        dx, d_w_in_t, cs_i, db = hosting(-1 if last else 250, _in_bwd, dq, dkc, dkp, dvc, dvp, du, xin, dx1, vec_in,
                                         w_in_t, rot, f"in_bwd_{l}")
        grad_ready(5 * l, d_w_in_t)
        d_mod = jnp.concatenate([cs_i[0], cs_i[1], cs_o[0], cs_f[0], cs_f[1], cs_a[0]])
        d_gain = jnp.stack([cs_i[2], cs_o[1], cs_f[2], cs_a[1]])
        ready_small.append((l, _pack_layer(d_mod, d_gain, db[0], dpsc[0], dsk[:, 0], dpw,
                                           my_loss if l == DEPTH - 1 else jnp.zeros((), F32))))
    grad_x = dx[None]
    exs, notes = leaving(N_DEV * D_FF)
    arrived(notes, _exchanges_alone(exs, "exchange_last"))

    layer_sums = [_unpack_layer(_sum_parts(small_all[l], 0, f"sum_small_{l}")) for l in range(DEPTH)]
    g_ada_b, g_gains, g_b_in, g_pool_scale, g_sinks, g_pool_w = (
        jnp.stack([layer_sums[l][j] for l in range(DEPTH)], axis=1 if j == 1 else 0) for j in range(6))
    loss = layer_sums[DEPTH - 1][6]

    dmod_all = jnp.stack([small_all[l][:, 0:N_MOD, :].reshape(N_DEV, N_DEV, ada_cols) for l in range(DEPTH)])
    dmod_cols = lax.dynamic_slice_in_dim(dmod_all, me, 1, axis=2)[:, :, 0, :]
    g_ada_w = _ada_grad(c_act, dmod_cols)

    slot = {"w_in": 0, "w_out": 1, "w_gate": 2, "w_up": 3, "w_down": 4}
    held_transposed = ("w_in", "w_gate", "w_up")
    swap = lambda a: jnp.transpose(a, (0, 2, 1))

    def step(w, g, m, v, name):
        if name in slot:
            mine = [parts[5 * l + slot[name]] for l in range(DEPTH)]
            if name in held_transposed:
                return [swap(o) for o in _sum_adamw(mine, swap(w), swap(m), swap(v), f"adamw_{name}")]
            return _sum_adamw(mine, w, m, v, f"adamw_{name}")
        flat = lambda a: a.reshape(-1, a.shape[-1])
        return [g] + [o.reshape(w.shape) for o in _adamw(flat(w), flat(g), flat(m), flat(v), f"adamw_{name}")]

    in_order = [
        ("ada_w", ada_w, g_ada_w, m_ada_w, v_ada_w), ("ada_b", ada_b, g_ada_b, m_ada_b, v_ada_b),
        ("w_in", w_in, None, m_w_in, v_w_in), ("b_in", b_in, g_b_in, m_b_in, v_b_in),
        ("sinks", sinks, g_sinks, m_sinks, v_sinks), ("pool_w", pool_w, g_pool_w, m_pool_w, v_pool_w),
        ("pool_scale", pool_scale, g_pool_scale, m_pool_scale, v_pool_scale),
        ("w_out", w_out, None, m_w_out, v_w_out), ("w_gate", w_gate, None, m_w_gate, v_w_gate),
        ("w_up", w_up, None, m_w_up, v_w_up), ("w_down", w_down, None, m_w_down, v_w_down),
        ("g_pre_mix", g_pre_mix, g_gains[0], m_g_pre_mix, v_g_pre_mix),
        ("g_post_mix", g_post_mix, g_gains[1], m_g_post_mix, v_g_post_mix),
        ("g_pre_ffn", g_pre_ffn, g_gains[2], m_g_pre_ffn, v_g_pre_ffn),
        ("g_post_ffn", g_post_ffn, g_gains[3], m_g_post_ffn, v_g_post_ffn)]
    steps = [step(w, g, m, v, n) for n, w, g, m, v in in_order]
    grads, deltas, new_m, new_v = ([s[j] for s in steps] for j in range(4))
    return (loss, grad_x, *grads, *deltas, *new_m, *new_v)
```

```python
import jax
import jax.numpy as jnp
from jax import lax
from jax.experimental import pallas as pl
from jax.experimental.pallas import tpu as pltpu

F32 = jnp.float32
BF16 = jnp.bfloat16

N_DEV = 8
DEPTH = 2
D_MODEL = 1024
HEAD_DIM = 64
N_HEADS = 8
N_KV_HEADS = 2
GROUP = N_HEADS // N_KV_HEADS
ATTN_WIDTH = N_HEADS * HEAD_DIM
KV_WIDTH = N_KV_HEADS * HEAD_DIM
POOL_WIDTH = 512
POOL_WINDOWS = (2, 4, 8, 16)
POOL_GROUP_WIDTH = 128
POOL_HALO = 16
IN_WIDTH = ATTN_WIDTH + 2 * KV_WIDTH + POOL_WIDTH
D_FF = 2816
N_MOD = 6
BLOCK = 128
ROT_DIM = 16
ROPE_THETA = 500000.0
EPS = 1e-6
NEG_INF = -1e30
Q_SCALE = HEAD_DIM ** -0.5

ADAM_LR = 0.001
ADAM_B1 = 0.9
ADAM_B2 = 0.999
ADAM_EPS = 1e-08
ADAM_WD = 0.01
ADAM_STEP = 10

LANES = 128
SEQ_TILE = 512
ATTN_BLOCKS = 8
FF_CHUNK = 256
VMEM_LIMIT = 56 * 1024 * 1024
MESH = pl.DeviceIdType.MESH

NT = (((1,), (1,)), ((), ()))
TN = (((0,), (0,)), ((), ()))


def _dot(a, b, dims=None):
    if dims is None:
        return jnp.dot(a, b, preferred_element_type=F32)
    return lax.dot_general(a, b, dims, preferred_element_type=F32)


def _cparams(n_axes):
    return pltpu.CompilerParams(dimension_semantics=("arbitrary",) * n_axes, vmem_limit_bytes=VMEM_LIMIT)


def _resident(shape):
    zeros = (0,) * len(shape)
    return pl.BlockSpec(shape, lambda *_: zeros, pipeline_mode=pl.Buffered(1))


def _rows(*vectors, width):
    rows = [jnp.reshape(v, (1, width)).astype(F32) for v in vectors]
    rows.append(jnp.zeros((8 - len(rows), width), F32))
    return jnp.concatenate(rows, axis=0)


def _rsqrt_ms(x):
    return lax.rsqrt(jnp.mean(x * x, axis=-1, keepdims=True) + EPS)


def _colsum(x):
    return jnp.sum(x, axis=0, keepdims=True)


def _seq_tile(s, tiles=1):
    return min(s, tiles * SEQ_TILE)


def _mesh_place():
    x, y, c = lax.axis_index("x"), lax.axis_index("y"), lax.axis_index("c")
    return x, y, c


def _flip(place, k):
    x, y, c = place
    return (1 - x if k & 4 else x, 1 - y if k & 2 else y, 1 - c if k & 1 else c)


def _index(place):
    x, y, c = place
    return 4 * x + 2 * y + c


def _allgather_vmem(block, name):
    r, c = block.shape

    def body(x_ref, out_ref, send_sems, recv_sems, local_sem):
        me = _mesh_place()
        mine = pltpu.make_async_copy(x_ref, out_ref.at[_index(me)], local_sem)
        mine.start()

        def copy(k):
            return pltpu.make_async_remote_copy(
                src_ref=x_ref, dst_ref=out_ref.at[_index(me)], send_sem=send_sems.at[k - 1], recv_sem=recv_sems.at[k - 1],
                device_id=_flip(me, k), device_id_type=MESH)

        def arrival(k):
            return pltpu.make_async_remote_copy(
                src_ref=x_ref, dst_ref=out_ref.at[_index(_flip(me, k))], send_sem=send_sems.at[k - 1],
                recv_sem=recv_sems.at[k - 1], device_id=_flip(me, k), device_id_type=MESH)

        for k in range(1, N_DEV):
            copy(k).start()
        for k in range(1, N_DEV):
            arrival(k).wait_recv()
        for k in range(1, N_DEV):
            copy(k).wait_send()
        mine.wait()

    return pl.pallas_call(
        body, name=name,
        out_shape=jax.ShapeDtypeStruct((N_DEV, r, c), block.dtype),
        in_specs=[pl.BlockSpec(memory_space=pltpu.VMEM)],
        out_specs=pl.BlockSpec(memory_space=pltpu.VMEM),
        scratch_shapes=[pltpu.SemaphoreType.DMA((N_DEV - 1,)), pltpu.SemaphoreType.DMA((N_DEV - 1,)),
                        pltpu.SemaphoreType.DMA],
    )(block)


class _Gather:
    def __init__(self, shards):
        n = len(shards)
        self.operands = list(shards)
        self.out_shape = [jax.ShapeDtypeStruct((N_DEV,) + s.shape, s.dtype) for s in shards]
        self.scratch = [pltpu.SemaphoreType.DMA((N_DEV - 1, n)), pltpu.SemaphoreType.DMA((N_DEV - 1, n)),
                        pltpu.SemaphoreType.DMA((n,))]

    def _copies(self, x_refs, out_refs, sems):
        send_sems, recv_sems, local_sems = sems
        n = len(x_refs)
        x, y, c = _mesh_place()
        me, sibling = (x, y, c), (x, y, 1 - c)
        chips = [(1 - x, y), (x, 1 - y), (1 - x, 1 - y)]

        def copy(k, a, block, to, from_input=False):
            rows = out_refs[a].at[_index(block)]
            return pltpu.make_async_remote_copy(
                src_ref=x_refs[a] if from_input else rows, dst_ref=rows,
                send_sem=send_sems.at[k, a], recv_sem=recv_sems.at[k, a], device_id=to, device_id_type=MESH)

        mine = [pltpu.make_async_copy(x_refs[a], out_refs[a].at[_index(me)], local_sems.at[a]) for a in range(n)]
        first = [copy(0, a, me, sibling, from_input=True) for a in range(n)]
        first += [copy(1 + j, a, me, (*chip, c), from_input=True) for j, chip in enumerate(chips) for a in range(n)]
        over_ici = [copy(1 + j, a, (*chip, c), me) for j, chip in enumerate(chips) for a in range(n)]
        passed = [copy(4 + j, a, (*chip, c), sibling) for j, chip in enumerate(chips) for a in range(n)]
        from_sibling = [copy(0, a, sibling, me) for a in range(n)]
        from_sibling += [copy(4 + j, a, (*chip, 1 - c), me) for j, chip in enumerate(chips) for a in range(n)]
        return mine, first, over_ici, passed, from_sibling

    def begin(self, x_refs, out_refs, sems):
        mine, first, _, _, _ = self._copies(x_refs, out_refs, sems)
        for cp in mine + first:
            cp.start()

    def middle(self, x_refs, out_refs, sems):
        _, _, over_ici, passed, _ = self._copies(x_refs, out_refs, sems)
        for arrived, onward in zip(over_ici, passed):
            arrived.wait_recv()
            onward.start()

    def end(self, x_refs, out_refs, sems):
        mine, first, _, passed, from_sibling = self._copies(x_refs, out_refs, sems)
        for cp in from_sibling:
            cp.wait_recv()
        for cp in first + passed:
            cp.wait_send()
        for cp in mine:
            cp.wait()


class _Scatter:
    OWN = N_DEV - 1

    def __init__(self, stacks):
        n = len(stacks)
        self.operands = list(stacks)
        self.out_shape = [jax.ShapeDtypeStruct(s.shape, s.dtype) for s in stacks]
        self.scratch = [pltpu.SemaphoreType.DMA((N_DEV - 1, n)), pltpu.SemaphoreType.DMA((N_DEV - 1, n)),
                        pltpu.SemaphoreType.DMA((n,))]

    def _copies(self, g_refs, out_refs, sems):
        send_sems, recv_sems, local_sems = sems
        me = _mesh_place()

        def copy(k, a):
            peer = _flip(me, k)
            return pltpu.make_async_remote_copy(
                src_ref=g_refs[a].at[_index(peer)], dst_ref=out_refs[a].at[k - 1],
                send_sem=send_sems.at[k - 1, a], recv_sem=recv_sems.at[k - 1, a], device_id=peer, device_id_type=MESH)

        mine = [pltpu.make_async_copy(g_refs[a].at[_index(me)], out_refs[a].at[self.OWN], local_sems.at[a])
                for a in range(len(g_refs))]
        return mine, [copy(k, a) for k in range(1, N_DEV) for a in range(len(g_refs))]

    def begin(self, g_refs, out_refs, sems):
        mine, copies = self._copies(g_refs, out_refs, sems)
        for cp in mine + copies:
            cp.start()

    def middle(self, g_refs, out_refs, sems):
        pass

    def end(self, g_refs, out_refs, sems):
        mine, copies = self._copies(g_refs, out_refs, sems)
        for cp in copies:
            cp.wait_recv()
        for cp in copies:
            cp.wait_send()
        for cp in mine:
            cp.wait()


def _call(body, args, *, name, grid, in_specs, out_specs, out_shape, scratch_shapes=(), exchanges=()):
    if not exchanges:
        outs = pl.pallas_call(body, name=name, grid=grid, in_specs=in_specs, out_specs=out_specs, out_shape=out_shape,
                              scratch_shapes=list(scratch_shapes), compiler_params=_cparams(len(grid)))(*args)
        return outs, []
    (steps,) = grid
    n_in, n_out, n_scr = len(in_specs), len(out_specs), len(scratch_shapes)
    ex_in = [len(ex.operands) for ex in exchanges]
    ex_out = [len(ex.out_shape) for ex in exchanges]
    ex_scr = [len(ex.scratch) for ex in exchanges]

    def split(refs, counts):
        parts, pos = [], 0
        for cnt in counts:
            parts.append(refs[pos:pos + cnt])
            pos += cnt
        return parts

    def wrapped(*refs):
        ins, xin, outs, xout, scr, xscr = split(refs, [n_in, sum(ex_in), n_out, sum(ex_out), n_scr, sum(ex_scr)])
        bound = list(zip(exchanges, split(xin, ex_in), split(xout, ex_out), split(xscr, ex_scr)))
        step = pl.program_id(0)

        def phase(method, at):
            @pl.when(step == at)
            def _():
                for ex, i_refs, o_refs, sems in bound:
                    getattr(ex, method)(i_refs, o_refs, sems)

        phase("begin", 0)
        phase("middle", (3 * steps) // 4)
        body(*ins, *outs, *scr)
        phase("end", steps - 1)

    any_spec = pl.BlockSpec(memory_space=pl.ANY)
    results = pl.pallas_call(
        wrapped, name=name, grid=grid,
        in_specs=list(in_specs) + [any_spec] * sum(ex_in),
        out_specs=list(out_specs) + [any_spec] * sum(ex_out),
        out_shape=list(out_shape) + [s for ex in exchanges for s in ex.out_shape],
        scratch_shapes=list(scratch_shapes) + [s for ex in exchanges for s in ex.scratch],
        compiler_params=_cparams(1),
    )(*args, *[a for ex in exchanges for a in ex.operands])
    return results[:n_out], split(results[n_out:], ex_out)


def _exchanges_alone(exchanges, name):
    def body(flag_ref):
        flag_ref[...] = jnp.zeros_like(flag_ref)

    _, ex_outs = _call(body, [], name=name, grid=(1,), in_specs=[], out_specs=[pl.BlockSpec((8, LANES), lambda i: (0, 0))],
                       out_shape=[jax.ShapeDtypeStruct((8, LANES), F32)], exchanges=exchanges)
    return ex_outs


def _mod_fwd(c_all, ada_w):
    cols = ada_w.shape[2]

    def body(c_ref, w_ref, mp_ref, act_ref):
        c = c_ref[...]
        act = c * jax.nn.sigmoid(c)
        act_ref[...] = act
        mp_ref[0] = _dot(act.astype(BF16), w_ref[0].astype(BF16))

    return pl.pallas_call(
        body, name="mod_fwd", grid=(DEPTH,),
        in_specs=[pl.BlockSpec((N_DEV, D_MODEL), lambda l: (0, 0)),
                  pl.BlockSpec((1, D_MODEL, cols), lambda l: (l, 0, 0))],
        out_specs=[pl.BlockSpec((1, N_DEV, cols), lambda l: (l, 0, 0)),
                   pl.BlockSpec((N_DEV, D_MODEL), lambda l: (0, 0))],
        out_shape=[jax.ShapeDtypeStruct((DEPTH, N_DEV, cols), F32), jax.ShapeDtypeStruct((N_DEV, D_MODEL), F32)],
        compiler_params=_cparams(1),
    )(c_all, ada_w)


def _ada_grad(c_act, dmod_cols):
    cols = dmod_cols.shape[2]

    def body(act_ref, dm_ref, g_ref):
        g_ref[0] = _dot(act_ref[...].astype(BF16), dm_ref[0].astype(BF16), TN)

    return pl.pallas_call(
        body, name="ada_grad", grid=(DEPTH,),
        in_specs=[pl.BlockSpec((N_DEV, D_MODEL), lambda l: (0, 0)),
                  pl.BlockSpec((1, N_DEV, cols), lambda l: (l, 0, 0))],
        out_specs=pl.BlockSpec((1, D_MODEL, cols), lambda l: (l, 0, 0)),
        out_shape=jax.ShapeDtypeStruct((DEPTH, D_MODEL, cols), F32),
        compiler_params=_cparams(1),
    )(c_act, dmod_cols)


def _rotary_terms(cos, sin):
    in_head = lax.broadcasted_iota(jnp.int32, (1, LANES), 1) % HEAD_DIM
    half = ROT_DIM // 2
    return cos, jnp.where(in_head < half, -sin, 0.0), jnp.where((in_head >= half) & (in_head < ROT_DIM), sin, 0.0)


def _rotate(t, cos, sin_lo, sin_hi):
    return t * cos + pltpu.roll(t, LANES - 8, 1) * sin_lo + pltpu.roll(t, 8, 1) * sin_hi


def _rotate_bwd(t, cos, sin_lo, sin_hi):
    return t * cos + pltpu.roll(t * sin_lo, 8, 1) + pltpu.roll(t * sin_hi, LANES - 8, 1)


def _fwd_in(x, vec, w_in_t, b_in, rot, name, exchanges=()):
    s = x.shape[0]
    ts = _seq_tile(s, 2)

    def body(x_ref, vec_ref, w_ref, b_ref, cos_ref, sin_ref, q_ref, k_ref, v_ref, u_ref):
        xt = x_ref[...]
        h = xt * _rsqrt_ms(xt) * vec_ref[0:1, :] * vec_ref[1:2, :] + vec_ref[2:3, :]
        proj = _dot(h.astype(BF16), w_ref[...], NT) + b_ref[0:1, :]
        cos, lo, hi = _rotary_terms(cos_ref[...], sin_ref[...])
        for j in range(ATTN_WIDTH // LANES):
            q_ref[:, j * LANES:(j + 1) * LANES] = (
                _rotate(proj[:, j * LANES:(j + 1) * LANES], cos, lo, hi) * Q_SCALE).astype(BF16)
        k_ref[...] = _rotate(proj[:, ATTN_WIDTH:ATTN_WIDTH + KV_WIDTH], cos, lo, hi).astype(BF16)
        v_ref[...] = proj[:, ATTN_WIDTH + KV_WIDTH:ATTN_WIDTH + 2 * KV_WIDTH].astype(BF16)
        u_ref[...] = proj[:, ATTN_WIDTH + 2 * KV_WIDTH:]

    row = lambda w: pl.BlockSpec((ts, w), lambda i: (i, 0))
    return _call(
        body, [x, vec, w_in_t, b_in, *rot], name=name, grid=(s // ts,),
        in_specs=[row(D_MODEL), _resident((8, D_MODEL)), _resident((IN_WIDTH, D_MODEL)), _resident((8, IN_WIDTH)),
                  row(LANES), row(LANES)],
        out_specs=[row(ATTN_WIDTH), row(KV_WIDTH), row(KV_WIDTH), row(POOL_WIDTH)],
        out_shape=[jax.ShapeDtypeStruct((s, ATTN_WIDTH), BF16), jax.ShapeDtypeStruct((s, KV_WIDTH), BF16),
                   jax.ShapeDtypeStruct((s, KV_WIDTH), BF16), jax.ShapeDtypeStruct((s, POOL_WIDTH), F32)],
        exchanges=exchanges)


def _band_mask_t(first_block):
    key = lax.broadcasted_iota(jnp.int32, (2 * BLOCK, 2 * BLOCK), 0)
    qry = lax.broadcasted_iota(jnp.int32, (2 * BLOCK, 2 * BLOCK), 1) & (BLOCK - 1)
    d = key - qry
    return (d >= 1) & (d <= BLOCK) & ((key >= BLOCK) | jnp.logical_not(first_block))


def _placed(both, kh, b):
    lane = lax.broadcasted_iota(jnp.int32, (1, LANES), 1)
    src = both if kh == b else pltpu.roll(both, HEAD_DIM, 1)
    return jnp.where((lane >= b * HEAD_DIM) & (lane < (b + 1) * HEAD_DIM), src, jnp.zeros_like(src))


def _pair(ref, rows, kh, r):
    j = 2 * kh + r
    return ref[rows, j * LANES:(j + 1) * LANES]


def _attn_specs(s):
    qb = min(ATTN_BLOCKS, s // BLOCK)
    cur = lambda w: pl.BlockSpec((qb * BLOCK, w), lambda i: (i, 0))
    prev = lambda w: pl.BlockSpec((BLOCK, w), lambda i: (jnp.maximum(i * qb - 1, 0), 0))
    return qb, cur, prev, pl.BlockSpec((N_HEADS, qb * BLOCK), lambda i: (0, i))


def _kv_window(prev_ref, cur_ref, jb):
    before = prev_ref[...] if jb == 0 else cur_ref[(jb - 1) * BLOCK:jb * BLOCK, :]
    return jnp.concatenate([before, cur_ref[jb * BLOCK:(jb + 1) * BLOCK, :]], axis=0)


def _sink_row(sink_ref, kh, b):
    lane = lax.broadcasted_iota(jnp.int32, (1, 2 * BLOCK), 1)
    return jnp.where(lane < BLOCK, sink_ref[GROUP * kh + b], sink_ref[GROUP * kh + 2 + b])


def _attn_fwd_t(sinks, q, k, v, name, exchanges=()):
    s = q.shape[0]
    qb, cur, prev, lse_spec = _attn_specs(s)

    def block(jb, sink_ref, q_ref, kc_ref, kp_ref, vc_ref, vp_ref, o_ref, l_ref):
        rows = slice(jb * BLOCK, (jb + 1) * BLOCK)
        valid = _band_mask_t(pl.program_id(0) == 0 if jb == 0 else False)
        k_both, v_both = _kv_window(kp_ref, kc_ref, jb), _kv_window(vp_ref, vc_ref, jb)
        for kh in range(N_KV_HEADS):
            qg = jnp.concatenate([_pair(q_ref, rows, kh, 0), _pair(q_ref, rows, kh, 1)], axis=0)
            keys = jnp.concatenate([_placed(k_both, kh, 0), _placed(k_both, kh, 1)], axis=0)
            st = _dot(keys, qg, NT)
            out = jnp.zeros((2 * BLOCK, LANES), F32)
            for b in range(2):
                sc = jnp.where(valid, st[b * 2 * BLOCK:(b + 1) * 2 * BLOCK, :], NEG_INF)
                sink = _sink_row(sink_ref, kh, b)
                m = jnp.maximum(jnp.max(sc, axis=0, keepdims=True), sink)
                p = jnp.exp(sc - m)
                den = jnp.sum(p, axis=0, keepdims=True) + jnp.exp(sink - m)
                out = out + _dot((p * (1.0 / den)).astype(BF16), _placed(v_both, kh, b), TN)
                lse = m + jnp.log(den)
                for r in range(2):
                    h = GROUP * kh + 2 * r + b
                    l_ref[h:h + 1, rows] = lse[:, r * BLOCK:(r + 1) * BLOCK]
            for r in range(2):
                j = 2 * kh + r
                o_ref[rows, j * LANES:(j + 1) * LANES] = out[r * BLOCK:(r + 1) * BLOCK, :].astype(BF16)

    def body(*refs):
        for jb in range(qb):
            block(jb, *refs)

    return _call(
        body, [sinks, q, k, k, v, v], name=name, grid=(s // (qb * BLOCK),),
        in_specs=[pl.BlockSpec(memory_space=pltpu.SMEM), cur(ATTN_WIDTH), cur(KV_WIDTH), prev(KV_WIDTH),
                  cur(KV_WIDTH), prev(KV_WIDTH)],
        out_specs=[cur(ATTN_WIDTH), lse_spec],
        out_shape=[jax.ShapeDtypeStruct((s, ATTN_WIDTH), BF16), jax.ShapeDtypeStruct((N_HEADS, s), F32)],
        exchanges=exchanges)


def _attn_bwd_t(sinks, q, do, lse, k, v, name, exchanges=()):
    s = q.shape[0]
    qb, cur, prev, lse_spec = _attn_specs(s)

    def block(jb, sink_ref, q_ref, do_ref, l_ref, kc_ref, kp_ref, vc_ref, vp_ref,
              dq_ref, dkc_ref, dkp_ref, dvc_ref, dvp_ref, dsink_ref):
        rows = slice(jb * BLOCK, (jb + 1) * BLOCK)
        valid = _band_mask_t(pl.program_id(0) == 0 if jb == 0 else False)
        k_both, v_both = _kv_window(kp_ref, kc_ref, jb), _kv_window(vp_ref, vc_ref, jb)
        lane = lax.broadcasted_iota(jnp.int32, (1, LANES), 1)
        col = lax.broadcasted_iota(jnp.int32, (1, 2 * BLOCK), 1)
        dk_heads, dv_heads = [], []
        for kh in range(N_KV_HEADS):
            qg = jnp.concatenate([_pair(q_ref, rows, kh, 0), _pair(q_ref, rows, kh, 1)], axis=0)
            dog = jnp.concatenate([_pair(do_ref, rows, kh, 0), _pair(do_ref, rows, kh, 1)], axis=0)
            k_placed = [_placed(k_both, kh, b) for b in range(2)]
            v_placed = [_placed(v_both, kh, b) for b in range(2)]
            st = _dot(jnp.concatenate(k_placed, axis=0), qg, NT)
            dpt = _dot(jnp.concatenate(v_placed, axis=0), dog, NT)
            dqg = jnp.zeros((2 * BLOCK, LANES), F32)
            dk_b, dv_b = [], []
            for b in range(2):
                part = slice(b * 2 * BLOCK, (b + 1) * 2 * BLOCK)
                heads = [GROUP * kh + 2 * r + b for r in range(2)]
                lse_row = jnp.concatenate([l_ref[h:h + 1, rows] for h in heads], axis=1)
                p = jnp.where(valid, jnp.exp(st[part, :] - lse_row), 0.0)
                dp = dpt[part, :]
                delta = jnp.sum(p * dp, axis=0, keepdims=True)
                sink_pull = jnp.exp(_sink_row(sink_ref, kh, b) - lse_row) * delta
                dsink_ref[heads[0]:heads[0] + 1, :] += -jnp.sum(jnp.where(col < BLOCK, sink_pull, 0.0))
                dsink_ref[heads[1]:heads[1] + 1, :] += -jnp.sum(jnp.where(col < BLOCK, 0.0, sink_pull))
                ds = (p * (dp - delta)).astype(BF16)
                dqg = dqg + _dot(ds, k_placed[b], TN)
                dk_b.append(_dot(ds, qg))
                dv_b.append(_dot(p.astype(BF16), dog))
            for parts, total in ((dk_b, dk_heads), (dv_b, dv_heads)):
                kept = jnp.where(lane < HEAD_DIM, parts[0], parts[1])
                total.append(kept + pltpu.roll(kept, HEAD_DIM, 1))
            for r in range(2):
                j = 2 * kh + r
                dq_ref[rows, j * LANES:(j + 1) * LANES] = dqg[r * BLOCK:(r + 1) * BLOCK, :] * Q_SCALE
        dk = jnp.where(lane < HEAD_DIM, dk_heads[0], dk_heads[1])
        dv = jnp.where(lane < HEAD_DIM, dv_heads[0], dv_heads[1])
        dkp_ref[rows, :] = dk[0:BLOCK, :]
        dkc_ref[rows, :] = dk[BLOCK:, :]
        dvp_ref[rows, :] = dv[0:BLOCK, :]
        dvc_ref[rows, :] = dv[BLOCK:, :]

    def body(*refs):
        @pl.when(pl.program_id(0) == 0)
        def _():
            refs[-1][...] = jnp.zeros_like(refs[-1])

        for jb in range(qb):
            block(jb, *refs)

    kv = jax.ShapeDtypeStruct((s, KV_WIDTH), F32)
    return _call(
        body, [sinks, q, do, lse, k, k, v, v], name=name, grid=(s // (qb * BLOCK),),
        in_specs=[pl.BlockSpec(memory_space=pltpu.SMEM), cur(ATTN_WIDTH), cur(ATTN_WIDTH), lse_spec,
                  cur(KV_WIDTH), prev(KV_WIDTH), cur(KV_WIDTH), prev(KV_WIDTH)],
        out_specs=[cur(ATTN_WIDTH), cur(KV_WIDTH), cur(KV_WIDTH), cur(KV_WIDTH), cur(KV_WIDTH),
                   pl.BlockSpec((8, LANES), lambda i: (0, 0))],
        out_shape=[jax.ShapeDtypeStruct((s, ATTN_WIDTH), F32), kv, kv, kv, kv, jax.ShapeDtypeStruct((8, LANES), F32)],
        exchanges=exchanges)


def _pool_counts(tile, ts):
    t = (tile * ts + lax.broadcasted_iota(jnp.int32, (ts, 1), 0) + 1).astype(F32)
    return [jnp.minimum(t, float(w)) for w in POOL_WINDOWS]


def _pooled(u, halo, counts, gi):
    cols = slice(gi * POOL_GROUP_WIDTH, (gi + 1) * POOL_GROUP_WIDTH)
    acc = jnp.concatenate([halo[:, cols], u[:, cols]], axis=0)
    shift = 1
    while shift < POOL_WINDOWS[gi]:
        acc = acc + pltpu.roll(acc, shift, 0)
        shift *= 2
    return acc[POOL_HALO:, :] / counts[gi] - u[:, cols]


def _fwd_out(ao, u, pool_w, psc, w_out, x, vec, name, exchanges=()):
    s = x.shape[0]
    ts = _seq_tile(s, 2)
    hb = ts // POOL_HALO

    def body(ao_ref, u_ref, uh_ref, pw_ref, psc_ref, w_ref, x_ref, vec_ref, x1_ref, mix_ref):
        i = pl.program_id(0)
        u_t = u_ref[...]
        halo = jnp.where(i > 0, uh_ref[...], 0.0)
        counts = _pool_counts(i, ts)
        cat = [ao_ref[...]]
        for gi in range(len(POOL_WINDOWS)):
            cols = slice(gi * POOL_GROUP_WIDTH, (gi + 1) * POOL_GROUP_WIDTH)
            og = _dot(_pooled(u_t, halo, counts, gi).astype(BF16), pw_ref[gi]) * psc_ref[0:1, cols]
            cat.append(og.astype(BF16))
        mix = _dot(jnp.concatenate(cat, axis=1), w_ref[...])
        mix_ref[...] = mix
        x1_ref[...] = x_ref[...] + vec_ref[0:1, :] * (mix * _rsqrt_ms(mix) * vec_ref[1:2, :])

    row = lambda w: pl.BlockSpec((ts, w), lambda i: (i, 0))
    return _call(
        body, [ao, u, u, pool_w, psc, w_out, x, vec], name=name, grid=(s // ts,),
        in_specs=[row(ATTN_WIDTH), row(POOL_WIDTH),
                  pl.BlockSpec((POOL_HALO, POOL_WIDTH), lambda i: (jnp.maximum(i * hb - 1, 0), 0)),
                  _resident(pool_w.shape), _resident((8, POOL_WIDTH)), _resident((D_MODEL, D_MODEL)),
                  row(D_MODEL), _resident((8, D_MODEL))],
        out_specs=[row(D_MODEL), row(D_MODEL)],
        out_shape=[jax.ShapeDtypeStruct((s, D_MODEL), F32), jax.ShapeDtypeStruct((s, D_MODEL), F32)],
        exchanges=exchanges)


def _ffn_fwd(x1, vec, w_gate_t, w_up_t, w_down, name, exchanges=(), target=None):
    s = x1.shape[0]
    ts = _seq_tile(s)

    def body(x_ref, *refs):
        if target is None:
            vec_ref, wg_ref, wu_ref, wd_ref, x2_ref, f_ref, gt_ref, up_ref, h2_ref, act_ref = refs
        else:
            t_ref, vec_ref, wg_ref, wu_ref, wd_ref, x2_ref, f_ref, gt_ref, up_ref, h2_ref, loss_ref, act_ref = refs
        xt = x_ref[...]
        h2 = (xt * _rsqrt_ms(xt) * vec_ref[0:1, :] * vec_ref[1:2, :] + vec_ref[2:3, :]).astype(BF16)
        h2_ref[...] = h2
        for c in range(D_FF // FF_CHUNK):
            cols = slice(c * FF_CHUNK, (c + 1) * FF_CHUNK)
            g = _dot(h2, wg_ref[cols, :], NT)
            up = _dot(h2, wu_ref[cols, :], NT)
            gt_ref[:, cols] = g.astype(BF16)
            up_ref[:, cols] = up.astype(BF16)
            act_ref[:, cols] = ((g * jax.nn.sigmoid(g)) * up).astype(BF16)
        f = _dot(act_ref[...], wd_ref[...])
        f_ref[...] = f
        x2 = xt + vec_ref[3:4, :] * (f * _rsqrt_ms(f) * vec_ref[4:5, :])
        if target is None:
            x2_ref[...] = x2
        else:
            @pl.when(pl.program_id(0) == 0)
            def _():
                loss_ref[...] = jnp.zeros_like(loss_ref)

            diff = x2 - t_ref[...]
            x2_ref[...] = diff / D_MODEL
            loss_ref[...] += 0.5 * jnp.sum(jnp.mean(diff * diff, axis=-1, keepdims=True))

    row = lambda w: pl.BlockSpec((ts, w), lambda i: (i, 0))
    wide = jax.ShapeDtypeStruct((s, D_FF), BF16)
    with_loss = target is not None
    return _call(
        body, [x1] + ([target] if with_loss else []) + [vec, w_gate_t, w_up_t, w_down], name=name, grid=(s // ts,),
        in_specs=[row(D_MODEL)] * (2 if with_loss else 1) + [
            _resident((8, D_MODEL)), _resident((D_FF, D_MODEL)), _resident((D_FF, D_MODEL)), _resident((D_FF, D_MODEL))],
        out_specs=[row(D_MODEL), row(D_MODEL), row(D_FF), row(D_FF), row(D_MODEL)] + (
            [pl.BlockSpec((8, LANES), lambda i: (0, 0))] if with_loss else []),
        out_shape=[jax.ShapeDtypeStruct((s, D_MODEL), F32), jax.ShapeDtypeStruct((s, D_MODEL), F32), wide, wide,
                   jax.ShapeDtypeStruct((s, D_MODEL), BF16)] + (
            [jax.ShapeDtypeStruct((8, LANES), F32)] if with_loss else []),
        scratch_shapes=[pltpu.VMEM((ts, D_FF), BF16)],
        exchanges=exchanges)


def _norm_bwd(dy_hat, x_hat, r):
    return r * (dy_hat - x_hat * jnp.mean(dy_hat * x_hat, axis=-1, keepdims=True))


def _accumulate_rows(ref, rows):
    for j, val in enumerate(rows):
        ref[j:j + 1, :] += val


def _ffn_bwd_act(dx2, f, gt, up, vec, w_down, name, exchanges=()):
    s = dx2.shape[0]
    ts = _seq_tile(s)

    def body(dx2_ref, f_ref, gt_ref, up_ref, vec_ref, wd_ref, dgt_ref, dup_ref, dwd_ref, cs_ref):
        @pl.when(pl.program_id(0) == 0)
        def _():
            cs_ref[...] = jnp.zeros_like(cs_ref)
            dwd_ref[...] = jnp.zeros_like(dwd_ref)

        dx2_t, f_t = dx2_ref[...], f_ref[...]
        gate, g_post = vec_ref[3:4, :], vec_ref[4:5, :]
        rf = _rsqrt_ms(f_t)
        f_hat = f_t * rf
        df = _norm_bwd(dx2_t * gate * g_post, f_hat, rf).astype(BF16)
        _accumulate_rows(cs_ref, [_colsum(dx2_t * (f_hat * g_post)), _colsum(dx2_t * gate * f_hat)])
        for c in range(D_FF // FF_CHUNK):
            cols = slice(c * FF_CHUNK, (c + 1) * FF_CHUNK)
            dact = _dot(df, wd_ref[cols, :], NT)
            g, u_t = gt_ref[:, cols].astype(F32), up_ref[:, cols].astype(F32)
            sg = 0.5 * jnp.tanh(0.5 * g) + 0.5
            silu = g * sg
            dgt_ref[:, cols] = (dact * u_t * (sg * (1.0 + g * (1.0 - sg)))).astype(BF16)
            dup_ref[:, cols] = (dact * silu).astype(BF16)
            dwd_ref[cols, :] += _dot((silu * u_t).astype(BF16), df, TN)

    row = lambda w: pl.BlockSpec((ts, w), lambda i: (i, 0))
    wide = jax.ShapeDtypeStruct((s, D_FF), BF16)
    return _call(
        body, [dx2, f, gt, up, vec, w_down], name=name, grid=(s // ts,),
        in_specs=[row(D_MODEL), row(D_MODEL), row(D_FF), row(D_FF), _resident((8, D_MODEL)), _resident((D_FF, D_MODEL))],
        out_specs=[row(D_FF), row(D_FF), _resident((D_FF, D_MODEL)), pl.BlockSpec((8, D_MODEL), lambda i: (0, 0))],
        out_shape=[wide, wide, jax.ShapeDtypeStruct((D_FF, D_MODEL), F32), jax.ShapeDtypeStruct((8, D_MODEL), F32)],
        exchanges=exchanges)


def _ffn_bwd_in(dx2, x1, dgt, dup, vec, w_gate_t, w_up_t, name, exchanges=()):
    s = x1.shape[0]
    ts = _seq_tile(s)

    def body(dx2_ref, x1_ref, dgt_ref, dup_ref, vec_ref, wg_ref, wu_ref, dx1_ref, cs_ref):
        @pl.when(pl.program_id(0) == 0)
        def _():
            cs_ref[...] = jnp.zeros_like(cs_ref)

        g_pre, one_scale = vec_ref[0:1, :], vec_ref[1:2, :]
        dh2 = _dot(dgt_ref[...], wg_ref[...]) + _dot(dup_ref[...], wu_ref[...])
        x1_t = x1_ref[...]
        r1 = _rsqrt_ms(x1_t)
        x_hat = x1_t * r1
        dx1_ref[...] = dx2_ref[...] + _norm_bwd(dh2 * g_pre * one_scale, x_hat, r1)
        _accumulate_rows(cs_ref, [_colsum(dh2), _colsum(dh2 * (x_hat * g_pre)), _colsum(dh2 * one_scale * x_hat)])

    row = lambda w: pl.BlockSpec((ts, w), lambda i: (i, 0))
    return _call(
        body, [dx2, x1, dgt, dup, vec, w_gate_t, w_up_t], name=name, grid=(s // ts,),
        in_specs=[row(D_MODEL), row(D_MODEL), row(D_FF), row(D_FF), _resident((8, D_MODEL)),
                  _resident((D_FF, D_MODEL)), _resident((D_FF, D_MODEL))],
        out_specs=[row(D_MODEL), pl.BlockSpec((8, D_MODEL), lambda i: (0, 0))],
        out_shape=[jax.ShapeDtypeStruct((s, D_MODEL), F32), jax.ShapeDtypeStruct((8, D_MODEL), F32)],
        exchanges=exchanges)


def _weight_grad(a, b, name, exchanges=()):
    s, m = a.shape
    n = b.shape[1]
    tk = min(s, 4 * SEQ_TILE)
    steps = s // tk

    def body(a_ref, b_ref, o_ref, acc_ref):
        i = pl.program_id(0)

        @pl.when(i == 0)
        def _():
            acc_ref[...] = jnp.zeros_like(acc_ref)

        b_t = b_ref[...]
        for c in range(m // FF_CHUNK):
            rows = slice(c * FF_CHUNK, (c + 1) * FF_CHUNK)
            acc_ref[rows, :] += _dot(a_ref[:, rows], b_t, TN)

        @pl.when(i == steps - 1)
        def _():
            o_ref[...] = acc_ref[...].astype(BF16)

    (grad,), ex_outs = _call(
        body, [a, b], name=name, grid=(steps,),
        in_specs=[pl.BlockSpec((tk, m), lambda i: (i, 0)), pl.BlockSpec((tk, n), lambda i: (i, 0))],
        out_specs=[_resident((m, n))],
        out_shape=[jax.ShapeDtypeStruct((m, n), BF16)],
        scratch_shapes=[pltpu.VMEM((m, n), F32)],
        exchanges=exchanges)
    return grad, ex_outs


def _out_bwd(dx1, mix, ao, u, pool_w, psc, w_out, vec, name, exchanges=()):
    s = dx1.shape[0]
    ts = _seq_tile(s)
    nt = s // ts
    hb = ts // POOL_HALO
    ng = len(POOL_WINDOWS)

    def body(dx1_ref, mix_ref, ao_ref, u_ref, uh_ref, pw_ref, psc_ref, w_ref, vec_ref,
             do_ref, du_ref, dw_out_ref, cs_ref, dpw_ref, dpsc_ref, carry_ref, dw_ref):
        i = pl.program_id(0)
        tile = nt - 1 - i

        @pl.when(i == 0)
        def _():
            dw_ref[...] = jnp.zeros_like(dw_ref)
            cs_ref[...] = jnp.zeros_like(cs_ref)
            dpw_ref[...] = jnp.zeros_like(dpw_ref)
            dpsc_ref[...] = jnp.zeros_like(dpsc_ref)
            carry_ref[...] = jnp.zeros_like(carry_ref)

        dx1_t, mix_t = dx1_ref[...], mix_ref[...]
        gate, g_post = vec_ref[0:1, :], vec_ref[1:2, :]
        rm = _rsqrt_ms(mix_t)
        m_hat = mix_t * rm
        dmix = _norm_bwd(dx1_t * gate * g_post, m_hat, rm).astype(BF16)
        _accumulate_rows(cs_ref, [_colsum(dx1_t * (m_hat * g_post)), _colsum(dx1_t * gate * m_hat)])
        d_cat = _dot(dmix, w_ref[...], NT)
        do_ref[...] = d_cat[:, 0:ATTN_WIDTH].astype(BF16)

        u_t = u_ref[...]
        halo = jnp.where(tile > 0, uh_ref[...], 0.0)
        counts = _pool_counts(tile, ts)
        cat = [ao_ref[...]]
        for gi in range(ng):
            cols = slice(gi * POOL_GROUP_WIDTH, (gi + 1) * POOL_GROUP_WIDTH)
            scale = psc_ref[0:1, cols]
            pooled = _pooled(u_t, halo, counts, gi).astype(BF16)
            og = _dot(pooled, pw_ref[gi])
            cat.append((og * scale).astype(BF16))
            d_out = d_cat[:, ATTN_WIDTH + gi * POOL_GROUP_WIDTH:ATTN_WIDTH + (gi + 1) * POOL_GROUP_WIDTH]
            dpsc_ref[0:1, cols] += _colsum(d_out * og)
            d_og = (d_out * scale).astype(BF16)
            dpw_ref[gi] += _dot(pooled, d_og, TN)
            d_pooled = _dot(d_og, pw_ref[gi], NT)
            spread = d_pooled / counts[gi]
            acc = jnp.concatenate([spread, carry_ref[:, cols]], axis=0)
            shift = 1
            while shift < POOL_WINDOWS[gi]:
                acc = acc + pltpu.roll(acc, ts + POOL_HALO - shift, 0)
                shift *= 2
            du_ref[:, cols] = acc[0:ts, :] - d_pooled
            carry_ref[:, cols] = spread[0:POOL_HALO, :]
        dw_ref[...] += _dot(jnp.concatenate(cat, axis=1), dmix, TN)

        @pl.when(i == nt - 1)
        def _():
            dw_out_ref[...] = dw_ref[...].astype(BF16)

    row = lambda w: pl.BlockSpec((ts, w), lambda i: (nt - 1 - i, 0))
    fixed = lambda shape: pl.BlockSpec(shape, lambda i: (0,) * len(shape))
    return _call(
        body, [dx1, mix, ao, u, u, pool_w, psc, w_out, vec], name=name, grid=(nt,),
        in_specs=[row(D_MODEL), row(D_MODEL), row(ATTN_WIDTH), row(POOL_WIDTH),
                  pl.BlockSpec((POOL_HALO, POOL_WIDTH), lambda i: (jnp.maximum((nt - 1 - i) * hb - 1, 0), 0)),
                  _resident(pool_w.shape), _resident((8, POOL_WIDTH)), _resident((D_MODEL, D_MODEL)),
                  _resident((8, D_MODEL))],
        out_specs=[row(ATTN_WIDTH), row(POOL_WIDTH), _resident((D_MODEL, D_MODEL)), fixed((8, D_MODEL)),
                   fixed(pool_w.shape), fixed((8, POOL_WIDTH))],
        out_shape=[jax.ShapeDtypeStruct((s, ATTN_WIDTH), BF16), jax.ShapeDtypeStruct((s, POOL_WIDTH), F32),
                   jax.ShapeDtypeStruct((D_MODEL, D_MODEL), BF16),
                   jax.ShapeDtypeStruct((8, D_MODEL), F32), jax.ShapeDtypeStruct(pool_w.shape, F32),
                   jax.ShapeDtypeStruct((8, POOL_WIDTH), F32)],
        scratch_shapes=[pltpu.VMEM((POOL_HALO, POOL_WIDTH), F32), pltpu.VMEM((D_MODEL, D_MODEL), F32)],
        exchanges=exchanges)


def _in_bwd(dq, dkc, dkp, dvc, dvp, du, x, dx1, vec, w_in_t, rot, name, exchanges=()):
    s = x.shape[0]
    ts = _seq_tile(s)
    nt = s // ts
    bpt = ts // BLOCK

    def body(dq_ref, dkc_ref, dkp_ref, dkn_ref, dvc_ref, dvp_ref, dvn_ref, du_ref, x_ref, dx1_ref, vec_ref, w_ref,
             cos_ref, sin_ref, dx_ref, dw_out_ref, cs_ref, db_ref, dw_ref):
        i = pl.program_id(0)

        @pl.when(i == 0)
        def _():
            dw_ref[...] = jnp.zeros_like(dw_ref)
            cs_ref[...] = jnp.zeros_like(cs_ref)
            db_ref[...] = jnp.zeros_like(db_ref)

        cos, lo, hi = _rotary_terms(cos_ref[...], sin_ref[...])

        def with_next_block(cur_ref, prev_ref, next_ref):
            nxt = jnp.where(i < nt - 1, next_ref[...], 0.0)
            later = nxt if bpt == 1 else jnp.concatenate([prev_ref[BLOCK:, :], nxt], axis=0)
            return cur_ref[...] + later

        pieces = [_rotate_bwd(dq_ref[:, j * LANES:(j + 1) * LANES], cos, lo, hi) for j in range(ATTN_WIDTH // LANES)]
        pieces.append(_rotate_bwd(with_next_block(dkc_ref, dkp_ref, dkn_ref), cos, lo, hi))
        pieces.append(with_next_block(dvc_ref, dvp_ref, dvn_ref))
        pieces.append(du_ref[...])
        dproj = jnp.concatenate(pieces, axis=1)
        db_ref[0:1, :] += _colsum(dproj)
        dproj_b = dproj.astype(BF16)
        dh = _dot(dproj_b, w_ref[...])

        xt = x_ref[...]
        g_pre, one_scale = vec_ref[0:1, :], vec_ref[1:2, :]
        r = _rsqrt_ms(xt)
        x_hat = xt * r
        dw_ref[...] += _dot(dproj_b, (x_hat * g_pre * one_scale + vec_ref[2:3, :]).astype(BF16), TN)
        dx_ref[...] = dx1_ref[...] + _norm_bwd(dh * g_pre * one_scale, x_hat, r)
        _accumulate_rows(cs_ref, [_colsum(dh), _colsum(dh * (x_hat * g_pre)), _colsum(dh * one_scale * x_hat)])

        @pl.when(i == nt - 1)
        def _():
            dw_out_ref[...] = dw_ref[...].astype(BF16)

    row = lambda w: pl.BlockSpec((ts, w), lambda i: (i, 0))
    nxt = pl.BlockSpec((BLOCK, KV_WIDTH), lambda i: (jnp.minimum((i + 1) * bpt, s // BLOCK - 1), 0))
    fixed = lambda shape: pl.BlockSpec(shape, lambda i: (0,) * len(shape))
    return _call(
        body, [dq, dkc, dkp, dkp, dvc, dvp, dvp, du, x, dx1, vec, w_in_t, *rot], name=name, grid=(nt,),
        in_specs=[row(ATTN_WIDTH), row(KV_WIDTH), row(KV_WIDTH), nxt, row(KV_WIDTH), row(KV_WIDTH), nxt,
                  row(POOL_WIDTH), row(D_MODEL), row(D_MODEL), _resident((8, D_MODEL)), _resident((IN_WIDTH, D_MODEL)),
                  row(LANES), row(LANES)],
        out_specs=[row(D_MODEL), _resident((IN_WIDTH, D_MODEL)), fixed((8, D_MODEL)), fixed((8, IN_WIDTH))],
        out_shape=[jax.ShapeDtypeStruct((s, D_MODEL), F32), jax.ShapeDtypeStruct((IN_WIDTH, D_MODEL), BF16),
                   jax.ShapeDtypeStruct((8, D_MODEL), F32), jax.ShapeDtypeStruct((8, IN_WIDTH), F32)],
        scratch_shapes=[pltpu.VMEM((IN_WIDTH, D_MODEL), F32)],
        exchanges=exchanges)


def _row_tile(rows):
    for t in (512, 352, 256, 176, 160, 128, 64, 32, 16, 8):
        if rows % t == 0:
            return t
    raise ValueError(f"no row tile for {rows} rows")


def _sum_parts(parts, first, name):
    n, r, c = parts.shape
    tr = _row_tile(r)

    def body(p_ref, o_ref):
        acc = p_ref[first].astype(F32)
        for k in range(n):
            if k != first:
                acc = acc + p_ref[k].astype(F32)
        o_ref[...] = acc

    return pl.pallas_call(
        body, name=name, grid=(r // tr,),
        in_specs=[pl.BlockSpec((n, tr, c), lambda i: (0, i, 0))],
        out_specs=pl.BlockSpec((tr, c), lambda i: (i, 0)),
        out_shape=jax.ShapeDtypeStruct((r, c), F32),
        compiler_params=_cparams(1),
    )(parts)


def _adam_step(w, g, m, v):
    m_new = ADAM_B1 * m + (1.0 - ADAM_B1) * g
    v_new = ADAM_B2 * v + (1.0 - ADAM_B2) * (g * g)
    m_hat = m_new / (1.0 - ADAM_B1 ** ADAM_STEP)
    v_hat = v_new / (1.0 - ADAM_B2 ** ADAM_STEP)
    return -ADAM_LR * (m_hat / (jnp.sqrt(v_hat) + ADAM_EPS) + ADAM_WD * w), m_new, v_new


def _sum_adamw(parts, w, m, v, name):
    depth, r, c = w.shape
    tr = _row_tile(r)

    def body(*refs):
        p_refs = refs[:depth]
        w_ref, m_ref, v_ref, g_ref, d_ref, nm_ref, nv_ref = refs[depth:]
        g = None
        for layer, p_ref in enumerate(p_refs):
            acc = p_ref[_Scatter.OWN].astype(F32)
            for k in range(N_DEV - 1):
                acc = acc + p_ref[k].astype(F32)
            g = acc if g is None else jnp.where(pl.program_id(0) == layer, acc, g)
        g_ref[0] = g
        d_ref[0], nm_ref[0], nv_ref[0] = _adam_step(w_ref[0], g, m_ref[0], v_ref[0])

    spec = pl.BlockSpec((1, tr, c), lambda l, i: (l, i, 0))
    shape = jax.ShapeDtypeStruct((depth, r, c), F32)
    return pl.pallas_call(
        body, name=name, grid=(depth, r // tr),
        in_specs=[pl.BlockSpec((N_DEV, tr, c), lambda l, i: (0, i, 0))] * depth + [spec] * 3,
        out_specs=[spec] * 4, out_shape=[shape] * 4, compiler_params=_cparams(2),
    )(*parts, w, m, v)


def _adamw(w, g, m, v, name):
    r, c = w.shape
    tr = _row_tile(r) if r % 8 == 0 else r

    def body(w_ref, g_ref, m_ref, v_ref, d_ref, nm_ref, nv_ref):
        d_ref[...], nm_ref[...], nv_ref[...] = _adam_step(w_ref[...], g_ref[...], m_ref[...], v_ref[...])

    spec = pl.BlockSpec((tr, c), lambda i: (i, 0))
    shape = jax.ShapeDtypeStruct((r, c), F32)
    return pl.pallas_call(
        body, name=name, grid=(r // tr,), in_specs=[spec] * 4, out_specs=[spec] * 3, out_shape=[shape] * 3,
        compiler_params=_cparams(1),
    )(w, g, m, v)


LAYER_ROWS = 80


def _pack_layer(dmod, gains, b_in, pool_scale, sinks, pool_w, loss):
    misc = jnp.concatenate([pool_scale, sinks, jnp.reshape(loss, (1,)),
                            jnp.zeros((D_MODEL - POOL_WIDTH - N_HEADS - 1,), F32)])
    parts = [dmod.reshape(N_MOD, D_MODEL), gains, jnp.pad(b_in, (0, 2 * D_MODEL - IN_WIDTH)).reshape(2, D_MODEL),
             misc.reshape(1, D_MODEL), pool_w.reshape(-1, D_MODEL)]
    packed = jnp.concatenate(parts, axis=0)
    return jnp.pad(packed, ((0, LAYER_ROWS - packed.shape[0]), (0, 0)))


def _unpack_layer(p):
    r = N_MOD + 4
    b_in = p[r:r + 2].reshape(2 * D_MODEL)[:IN_WIDTH]
    pool_w = p[r + 3:r + 3 + 64].reshape(len(POOL_WINDOWS), POOL_GROUP_WIDTH, POOL_GROUP_WIDTH)
    return (p[0:N_MOD].reshape(N_MOD * D_MODEL), p[N_MOD:r], b_in, p[r + 2, :POOL_WIDTH],
            p[r + 2, POOL_WIDTH:POOL_WIDTH + N_HEADS], pool_w, p[r + 2, POOL_WIDTH + N_HEADS])


def _rotary_tables(positions):
    inv_freq = ROPE_THETA ** (-jnp.arange(0, ROT_DIM, 2, dtype=F32) / ROT_DIM)
    head_freq = jnp.concatenate([inv_freq, inv_freq, jnp.zeros((HEAD_DIM - ROT_DIM,), F32)])
    lane_freq = jnp.concatenate([head_freq, head_freq])
    ang = positions.reshape(-1).astype(F32)[:, None] * lane_freq[None, :]
    return jnp.cos(ang), jnp.sin(ang)


def kernel(x, c, positions, ada_w, ada_b, w_in, b_in, sinks, pool_w, pool_scale, w_out, w_gate, w_up, w_down, g_pre_mix, g_post_mix, g_pre_ffn, g_post_ffn, loss_target, m_ada_w, m_ada_b, m_w_in, m_b_in, m_sinks, m_pool_w, m_pool_scale, m_w_out, m_w_gate, m_w_up, m_w_down, m_g_pre_mix, m_g_post_mix, m_g_pre_ffn, m_g_post_ffn, v_ada_w, v_ada_b, v_w_in, v_b_in, v_sinks, v_pool_w, v_pool_scale, v_w_out, v_w_gate, v_w_up, v_w_down, v_g_pre_mix, v_g_post_mix, v_g_pre_ffn, v_g_post_ffn):
    me = _index(_mesh_place())
    x0 = x[0]
    target = loss_target[0]
    rot = _rotary_tables(positions)
    ada_cols = ada_w.shape[2]

    shards = []
    for l in range(DEPTH):
        shards += [w_in[l].T.astype(BF16), w_out[l].astype(BF16), w_gate[l].T.astype(BF16), w_up[l].T.astype(BF16),
                   w_down[l].astype(BF16)]
    whole = lambda gathered: [g.reshape(-1, D_MODEL) for g in gathered]
    full = [None] * (5 * DEPTH)
    first = _exchanges_alone([_Gather(shards[0:2] + [jnp.broadcast_to(c, (8, D_MODEL))])], "gather_first")[0]
    full[0:2] = whole(first[0:2])

    mod_part, c_act = _mod_fwd(first[2][:, 0, :], ada_w)
    mod_all = _allgather_vmem(mod_part.reshape(DEPTH * N_DEV, ada_cols), "gather_mod")
    mod_mine = lax.dynamic_slice_in_dim(mod_all.reshape(N_DEV, DEPTH, N_DEV, ada_cols), me, 1, axis=2)[:, :, 0, :]
    mod = jnp.transpose(mod_mine, (1, 0, 2)).reshape(DEPTH, N_MOD * D_MODEL) + ada_b
    mod = mod.reshape(DEPTH, N_MOD, D_MODEL)

    hosted = {("fwd_in", 0): (_Gather(shards[2:3]), [2]), ("attn_fwd", 0): (_Gather(shards[3:4]), [3]),
              ("fwd_out", 0): (_Gather(shards[4:5]), [4]),
              ("ffn_fwd", 0): (_Gather(shards[5:10]), [5, 6, 7, 8, 9])}

    def take(kind, l, ex_outs):
        if (kind, l) in hosted:
            for slot, g in zip(hosted[kind, l][1], whole(ex_outs[0])):
                full[slot] = g

    def beside(kind, l):
        return [hosted[kind, l][0]] if (kind, l) in hosted else []

    pool_w_b = pool_w.astype(BF16)

    saved = []
    xl = x0
    for l in range(DEPTH):
        vec_in = _rows(g_pre_mix[l], 1.0 + mod[l, 1], mod[l, 0], width=D_MODEL)
        vec_out = _rows(mod[l, 2], g_post_mix[l], width=D_MODEL)
        vec_ffn = _rows(g_pre_ffn[l], 1.0 + mod[l, 4], mod[l, 3], mod[l, 5], g_post_ffn[l], width=D_MODEL)
        psc = _rows(pool_scale[l], width=POOL_WIDTH)
        (q, k, v, u), ex = _fwd_in(xl, vec_in, full[5 * l], _rows(b_in[l], width=IN_WIDTH), rot, f"fwd_in_{l}",
                                   beside("fwd_in", l))
        take("fwd_in", l, ex)
        (ao, lse), ex = _attn_fwd_t(sinks[l], q, k, v, f"attn_fwd_{l}", beside("attn_fwd", l))
        take("attn_fwd", l, ex)
        (x1, mix), ex = _fwd_out(ao, u, pool_w_b[l], psc, full[5 * l + 1], xl, vec_out, f"fwd_out_{l}",
                                 beside("fwd_out", l))
        take("fwd_out", l, ex)
        (x2, f, gt, up, h2, *loss_part), ex = _ffn_fwd(
            x1, vec_ffn, full[5 * l + 2], full[5 * l + 3], full[5 * l + 4], f"ffn_fwd_{l}", beside("ffn_fwd", l),
            target=target if l == DEPTH - 1 else None)
        take("ffn_fwd", l, ex)
        saved.append((xl, q, k, v, u, ao, lse, x1, mix, f, gt, up, h2, vec_in, vec_out, vec_ffn, psc))
        xl = x2

    dx = xl
    my_loss = loss_part[0][0, 0]

    stacks = [None] * (5 * DEPTH)
    parts = [None] * (5 * DEPTH)
    small_all = [None] * DEPTH
    ready_stacks, ready_small = [], []

    def leaving(row_budget):
        exs, notes = [], []
        taken = []
        while ready_stacks and stacks[ready_stacks[0]].shape[1] <= row_budget:
            row_budget -= stacks[ready_stacks[0]].shape[1]
            taken.append(ready_stacks.pop(0))
        if taken:
            exs.append(_Scatter([stacks[a] for a in taken]))
            notes.append(("stacks", taken))
        if ready_small and row_budget >= 0:
            exs.append(_Gather([ready_small[0][1]]))
            notes.append(("small", ready_small[0][0]))
            ready_small.clear()
        return exs, notes

    def arrived(notes, ex_outs):
        for (kind, what), outs in zip(notes, ex_outs):
            if kind == "stacks":
                for a, o in zip(what, outs):
                    parts[a] = o
            else:
                small_all[what] = outs[0]

    def grad_ready(a, grad):
        stacks[a] = grad.reshape(N_DEV, -1, D_MODEL)
        ready_stacks.append(a)

    def hosting(row_budget, kernel_fn, *args):
        exs, notes = leaving(row_budget)
        outs, ex_outs = kernel_fn(*args, exs)
        arrived(notes, ex_outs)
        return outs

    for l in reversed(range(DEPTH)):
        w_in_t, w_out_f, w_gate_t, w_up_t, w_down_f = full[5 * l:5 * l + 5]
        xin, q, k, v, u, ao, lse, x1, mix, f, gt, up, h2, vec_in, vec_out, vec_ffn, psc = saved[l]
        last = l == 0
        dgt, dup, d_w_down, cs_a = hosting(600, _ffn_bwd_act, dx, f, gt, up, vec_ffn, w_down_f, f"ffn_bwd_act_{l}")
        grad_ready(5 * l + 4, d_w_down.astype(BF16))
        grad_ready(5 * l + 2, hosting(-1, _weight_grad, dgt, h2, f"grad_w_gate_{l}"))
        grad_ready(5 * l + 3, hosting(-1, _weight_grad, dup, h2, f"grad_w_up_{l}"))
        dx1, cs_f = hosting(400, _ffn_bwd_in, dx, x1, dgt, dup, vec_ffn, w_gate_t, w_up_t, f"ffn_bwd_in_{l}")
        do, du, d_w_out, cs_o, dpw, dpsc = hosting(360, _out_bwd, dx1, mix, ao, u, pool_w_b[l], psc, w_out_f, vec_out,
                                                   f"out_bwd_{l}")
        grad_ready(5 * l + 1, d_w_out)
        dq, dkc, dkp, dvc, dvp, dsk = hosting(480 if last else 400, _attn_bwd_t, sinks[l], q, do, lse, k, v,
                                              f"attn_bwd_{l}")
        dx, d_w_in_t, cs_i, db = hosting(-1 if last else 250, _in_bwd, dq, dkc, dkp, dvc, dvp, du, xin, dx1, vec_in,
                                         w_in_t, rot, f"in_bwd_{l}")
        grad_ready(5 * l, d_w_in_t)
        d_mod = jnp.concatenate([cs_i[0], cs_i[1], cs_o[0], cs_f[0], cs_f[1], cs_a[0]])
        d_gain = jnp.stack([cs_i[2], cs_o[1], cs_f[2], cs_a[1]])
        ready_small.append((l, _pack_layer(d_mod, d_gain, db[0], dpsc[0], dsk[:, 0], dpw,
                                           my_loss if l == DEPTH - 1 else jnp.zeros((), F32))))
    grad_x = dx[None]
    exs, notes = leaving(N_DEV * D_FF)
    arrived(notes, _exchanges_alone(exs, "exchange_last"))

    layer_sums = [_unpack_layer(_sum_parts(small_all[l], 0, f"sum_small_{l}")) for l in range(DEPTH)]
    g_ada_b, g_gains, g_b_in, g_pool_scale, g_sinks, g_pool_w = (
        jnp.stack([layer_sums[l][j] for l in range(DEPTH)], axis=1 if j == 1 else 0) for j in range(6))
    loss = layer_sums[DEPTH - 1][6]

    dmod_all = jnp.stack([small_all[l][:, 0:N_MOD, :].reshape(N_DEV, N_DEV, ada_cols) for l in range(DEPTH)])
    dmod_cols = lax.dynamic_slice_in_dim(dmod_all, me, 1, axis=2)[:, :, 0, :]
    g_ada_w = _ada_grad(c_act, dmod_cols)

    slot = {"w_in": 0, "w_out": 1, "w_gate": 2, "w_up": 3, "w_down": 4}
    held_transposed = ("w_in", "w_gate", "w_up")
    swap = lambda a: jnp.transpose(a, (0, 2, 1))

    def step(w, g, m, v, name):
        if name in slot:
            mine = [parts[5 * l + slot[name]] for l in range(DEPTH)]
            if name in held_transposed:
                return [swap(o) for o in _sum_adamw(mine, swap(w), swap(m), swap(v), f"adamw_{name}")]
            return _sum_adamw(mine, w, m, v, f"adamw_{name}")
        flat = lambda a: a.reshape(-1, a.shape[-1])
        return [g] + [o.reshape(w.shape) for o in _adamw(flat(w), flat(g), flat(m), flat(v), f"adamw_{name}")]

    in_order = [
        ("ada_w", ada_w, g_ada_w, m_ada_w, v_ada_w), ("ada_b", ada_b, g_ada_b, m_ada_b, v_ada_b),
        ("w_in", w_in, None, m_w_in, v_w_in), ("b_in", b_in, g_b_in, m_b_in, v_b_in),
        ("sinks", sinks, g_sinks, m_sinks, v_sinks), ("pool_w", pool_w, g_pool_w, m_pool_w, v_pool_w),
        ("pool_scale", pool_scale, g_pool_scale, m_pool_scale, v_pool_scale),
        ("w_out", w_out, None, m_w_out, v_w_out), ("w_gate", w_gate, None, m_w_gate, v_w_gate),
        ("w_up", w_up, None, m_w_up, v_w_up), ("w_down", w_down, None, m_w_down, v_w_down),
        ("g_pre_mix", g_pre_mix, g_gains[0], m_g_pre_mix, v_g_pre_mix),
        ("g_post_mix", g_post_mix, g_gains[1], m_g_post_mix, v_g_post_mix),
        ("g_pre_ffn", g_pre_ffn, g_gains[2], m_g_pre_ffn, v_g_pre_ffn),
        ("g_post_ffn", g_post_ffn, g_gains[3], m_g_post_ffn, v_g_post_ffn)]
    steps = [step(w, g, m, v, n) for n, w, g, m, v in in_order]
    grads, deltas, new_m, new_v = ([s[j] for s in steps] for j in range(4))
    return (loss, grad_x, *grads, *deltas, *new_m, *new_v)
```

```python
import jax
import jax.numpy as jnp
from jax import lax
from jax.experimental import pallas as pl
from jax.experimental.pallas import tpu as pltpu

F32 = jnp.float32
BF16 = jnp.bfloat16

N_DEV = 8
DEPTH = 2
D_MODEL = 1024
HEAD_DIM = 64
N_HEADS = 8
N_KV_HEADS = 2
GROUP = N_HEADS // N_KV_HEADS
ATTN_WIDTH = N_HEADS * HEAD_DIM
KV_WIDTH = N_KV_HEADS * HEAD_DIM
POOL_WIDTH = 512
POOL_WINDOWS = (2, 4, 8, 16)
POOL_GROUP_WIDTH = 128
POOL_HALO = 16
IN_WIDTH = ATTN_WIDTH + 2 * KV_WIDTH + POOL_WIDTH
D_FF = 2816
N_MOD = 6
BLOCK = 128
ROT_DIM = 16
ROPE_THETA = 500000.0
EPS = 1e-6
NEG_INF = -1e30
Q_SCALE = HEAD_DIM ** -0.5

ADAM_LR = 0.001
ADAM_B1 = 0.9
ADAM_B2 = 0.999
ADAM_EPS = 1e-08
ADAM_WD = 0.01
ADAM_STEP = 10

LANES = 128
SEQ_TILE = 512
ATTN_BLOCKS = 8
FF_CHUNK = 256
VMEM_LIMIT = 56 * 1024 * 1024
MESH = pl.DeviceIdType.MESH

NT = (((1,), (1,)), ((), ()))
TN = (((0,), (0,)), ((), ()))


def _dot(a, b, dims=None):
    if dims is None:
        return jnp.dot(a, b, preferred_element_type=F32)
    return lax.dot_general(a, b, dims, preferred_element_type=F32)


def _cparams(n_axes):
    return pltpu.CompilerParams(dimension_semantics=("arbitrary",) * n_axes, vmem_limit_bytes=VMEM_LIMIT)


def _resident(shape):
    zeros = (0,) * len(shape)
    return pl.BlockSpec(shape, lambda *_: zeros, pipeline_mode=pl.Buffered(1))


def _rows(*vectors, width):
    rows = [jnp.reshape(v, (1, width)).astype(F32) for v in vectors]
    rows.append(jnp.zeros((8 - len(rows), width), F32))
    return jnp.concatenate(rows, axis=0)


def _rsqrt_ms(x):
    return lax.rsqrt(jnp.mean(x * x, axis=-1, keepdims=True) + EPS)


def _colsum(x):
    return jnp.sum(x, axis=0, keepdims=True)


def _seq_tile(s, tiles=1):
    return min(s, tiles * SEQ_TILE)


def _mesh_place():
    x, y, c = lax.axis_index("x"), lax.axis_index("y"), lax.axis_index("c")
    return x, y, c


def _flip(place, k):
    x, y, c = place
    return (1 - x if k & 4 else x, 1 - y if k & 2 else y, 1 - c if k & 1 else c)


def _index(place):
    x, y, c = place
    return 4 * x + 2 * y + c


def _allgather_vmem(block, name):
    r, c = block.shape

    def body(x_ref, out_ref, send_sems, recv_sems, local_sem):
        me = _mesh_place()
        mine = pltpu.make_async_copy(x_ref, out_ref.at[_index(me)], local_sem)
        mine.start()

        def copy(k):
            return pltpu.make_async_remote_copy(
                src_ref=x_ref, dst_ref=out_ref.at[_index(me)], send_sem=send_sems.at[k - 1], recv_sem=recv_sems.at[k - 1],
                device_id=_flip(me, k), device_id_type=MESH)

        def arrival(k):
            return pltpu.make_async_remote_copy(
                src_ref=x_ref, dst_ref=out_ref.at[_index(_flip(me, k))], send_sem=send_sems.at[k - 1],
                recv_sem=recv_sems.at[k - 1], device_id=_flip(me, k), device_id_type=MESH)

        for k in range(1, N_DEV):
            copy(k).start()
        for k in range(1, N_DEV):
            arrival(k).wait_recv()
        for k in range(1, N_DEV):
            copy(k).wait_send()
        mine.wait()

    return pl.pallas_call(
        body, name=name,
        out_shape=jax.ShapeDtypeStruct((N_DEV, r, c), block.dtype),
        in_specs=[pl.BlockSpec(memory_space=pltpu.VMEM)],
        out_specs=pl.BlockSpec(memory_space=pltpu.VMEM),
        scratch_shapes=[pltpu.SemaphoreType.DMA((N_DEV - 1,)), pltpu.SemaphoreType.DMA((N_DEV - 1,)),
                        pltpu.SemaphoreType.DMA],
    )(block)


class _Gather:
    def __init__(self, shards):
        n = len(shards)
        self.operands = list(shards)
        self.out_shape = [jax.ShapeDtypeStruct((N_DEV,) + s.shape, s.dtype) for s in shards]
        self.scratch = [pltpu.SemaphoreType.DMA((N_DEV - 1, n)), pltpu.SemaphoreType.DMA((N_DEV - 1, n)),
                        pltpu.SemaphoreType.DMA((n,))]

    def _copies(self, x_refs, out_refs, sems):
        send_sems, recv_sems, local_sems = sems
        n = len(x_refs)
        x, y, c = _mesh_place()
        me, sibling = (x, y, c), (x, y, 1 - c)
        chips = [(1 - x, y), (x, 1 - y), (1 - x, 1 - y)]

        def copy(k, a, block, to, from_input=False):
            rows = out_refs[a].at[_index(block)]
            return pltpu.make_async_remote_copy(
                src_ref=x_refs[a] if from_input else rows, dst_ref=rows,
                send_sem=send_sems.at[k, a], recv_sem=recv_sems.at[k, a], device_id=to, device_id_type=MESH)

        mine = [pltpu.make_async_copy(x_refs[a], out_refs[a].at[_index(me)], local_sems.at[a]) for a in range(n)]
        first = [copy(0, a, me, sibling, from_input=True) for a in range(n)]
        first += [copy(1 + j, a, me, (*chip, c), from_input=True) for j, chip in enumerate(chips) for a in range(n)]
        over_ici = [copy(1 + j, a, (*chip, c), me) for j, chip in enumerate(chips) for a in range(n)]
        passed = [copy(4 + j, a, (*chip, c), sibling) for j, chip in enumerate(chips) for a in range(n)]
        from_sibling = [copy(0, a, sibling, me) for a in range(n)]
        from_sibling += [copy(4 + j, a, (*chip, 1 - c), me) for j, chip in enumerate(chips) for a in range(n)]
        return mine, first, over_ici, passed, from_sibling

    def begin(self, x_refs, out_refs, sems):
        mine, first, _, _, _ = self._copies(x_refs, out_refs, sems)
        for cp in mine + first:
            cp.start()

    def middle(self, x_refs, out_refs, sems):
        _, _, over_ici, passed, _ = self._copies(x_refs, out_refs, sems)
        for arrived, onward in zip(over_ici, passed):
            arrived.wait_recv()
            onward.start()

    def end(self, x_refs, out_refs, sems):
        mine, first, _, passed, from_sibling = self._copies(x_refs, out_refs, sems)
        for cp in from_sibling:
            cp.wait_recv()
        for cp in first + passed:
            cp.wait_send()
        for cp in mine:
            cp.wait()


class _Scatter:
    OWN = N_DEV - 1

    def __init__(self, stacks):
        n = len(stacks)
        self.operands = list(stacks)
        self.out_shape = [jax.ShapeDtypeStruct(s.shape, s.dtype) for s in stacks]
        self.scratch = [pltpu.SemaphoreType.DMA((N_DEV - 1, n)), pltpu.SemaphoreType.DMA((N_DEV - 1, n)),
                        pltpu.SemaphoreType.DMA((n,))]

    def _copies(self, g_refs, out_refs, sems):
        send_sems, recv_sems, local_sems = sems
        me = _mesh_place()

        def copy(k, a):
            peer = _flip(me, k)
            return pltpu.make_async_remote_copy(
                src_ref=g_refs[a].at[_index(peer)], dst_ref=out_refs[a].at[k - 1],
                send_sem=send_sems.at[k - 1, a], recv_sem=recv_sems.at[k - 1, a], device_id=peer, device_id_type=MESH)

        mine = [pltpu.make_async_copy(g_refs[a].at[_index(me)], out_refs[a].at[self.OWN], local_sems.at[a])
                for a in range(len(g_refs))]
        return mine, [copy(k, a) for k in range(1, N_DEV) for a in range(len(g_refs))]

    def begin(self, g_refs, out_refs, sems):
        mine, copies = self._copies(g_refs, out_refs, sems)
        for cp in mine + copies:
            cp.start()

    def middle(self, g_refs, out_refs, sems):
        pass

    def end(self, g_refs, out_refs, sems):
        mine, copies = self._copies(g_refs, out_refs, sems)
        for cp in copies:
            cp.wait_recv()
        for cp in copies:
            cp.wait_send()
        for cp in mine:
            cp.wait()


def _call(body, args, *, name, grid, in_specs, out_specs, out_shape, scratch_shapes=(), exchanges=()):
    if not exchanges:
        outs = pl.pallas_call(body, name=name, grid=grid, in_specs=in_specs, out_specs=out_specs, out_shape=out_shape,
                              scratch_shapes=list(scratch_shapes), compiler_params=_cparams(len(grid)))(*args)
        return outs, []
    (steps,) = grid
    n_in, n_out, n_scr = len(in_specs), len(out_specs), len(scratch_shapes)
    ex_in = [len(ex.operands) for ex in exchanges]
    ex_out = [len(ex.out_shape) for ex in exchanges]
    ex_scr = [len(ex.scratch) for ex in exchanges]

    def split(refs, counts):
        parts, pos = [], 0
        for cnt in counts:
            parts.append(refs[pos:pos + cnt])
            pos += cnt
        return parts

    def wrapped(*refs):
        ins, xin, outs, xout, scr, xscr = split(refs, [n_in, sum(ex_in), n_out, sum(ex_out), n_scr, sum(ex_scr)])
        bound = list(zip(exchanges, split(xin, ex_in), split(xout, ex_out), split(xscr, ex_scr)))
        step = pl.program_id(0)

        def phase(method, at):
            @pl.when(step == at)
            def _():
                for ex, i_refs, o_refs, sems in bound:
                    getattr(ex, method)(i_refs, o_refs, sems)

        phase("begin", 0)
        phase("middle", (3 * steps) // 4)
        body(*ins, *outs, *scr)
        phase("end", steps - 1)

    any_spec = pl.BlockSpec(memory_space=pl.ANY)
    results = pl.pallas_call(
        wrapped, name=name, grid=grid,
        in_specs=list(in_specs) + [any_spec] * sum(ex_in),
        out_specs=list(out_specs) + [any_spec] * sum(ex_out),
        out_shape=list(out_shape) + [s for ex in exchanges for s in ex.out_shape],
        scratch_shapes=list(scratch_shapes) + [s for ex in exchanges for s in ex.scratch],
        compiler_params=_cparams(1),
    )(*args, *[a for ex in exchanges for a in ex.operands])
    return results[:n_out], split(results[n_out:], ex_out)


def _exchanges_alone(exchanges, name):
    def body(flag_ref):
        flag_ref[...] = jnp.zeros_like(flag_ref)

    _, ex_outs = _call(body, [], name=name, grid=(1,), in_specs=[], out_specs=[pl.BlockSpec((8, LANES), lambda i: (0, 0))],
                       out_shape=[jax.ShapeDtypeStruct((8, LANES), F32)], exchanges=exchanges)
    return ex_outs


def _mod_fwd(c_all, ada_w):
    cols = ada_w.shape[2]

    def body(c_ref, w_ref, mp_ref, act_ref):
        c = c_ref[...]
        act = c * jax.nn.sigmoid(c)
        act_ref[...] = act
        mp_ref[0] = _dot(act.astype(BF16), w_ref[0].astype(BF16))

    return pl.pallas_call(
        body, name="mod_fwd", grid=(DEPTH,),
        in_specs=[pl.BlockSpec((N_DEV, D_MODEL), lambda l: (0, 0)),
                  pl.BlockSpec((1, D_MODEL, cols), lambda l: (l, 0, 0))],
        out_specs=[pl.BlockSpec((1, N_DEV, cols), lambda l: (l, 0, 0)),
                   pl.BlockSpec((N_DEV, D_MODEL), lambda l: (0, 0))],
        out_shape=[jax.ShapeDtypeStruct((DEPTH, N_DEV, cols), F32), jax.ShapeDtypeStruct((N_DEV, D_MODEL), F32)],
        compiler_params=_cparams(1),
    )(c_all, ada_w)


def _ada_grad(c_act, dmod_cols):
    cols = dmod_cols.shape[2]

    def body(act_ref, dm_ref, g_ref):
        g_ref[0] = _dot(act_ref[...].astype(BF16), dm_ref[0].astype(BF16), TN)

    return pl.pallas_call(
        body, name="ada_grad", grid=(DEPTH,),
        in_specs=[pl.BlockSpec((N_DEV, D_MODEL), lambda l: (0, 0)),
                  pl.BlockSpec((1, N_DEV, cols), lambda l: (l, 0, 0))],
        out_specs=pl.BlockSpec((1, D_MODEL, cols), lambda l: (l, 0, 0)),
        out_shape=jax.ShapeDtypeStruct((DEPTH, D_MODEL, cols), F32),
        compiler_params=_cparams(1),
    )(c_act, dmod_cols)


def _rotary_terms(cos, sin):
    in_head = lax.broadcasted_iota(jnp.int32, (1, LANES), 1) % HEAD_DIM
    half = ROT_DIM // 2
    return cos, jnp.where(in_head < half, -sin, 0.0), jnp.where((in_head >= half) & (in_head < ROT_DIM), sin, 0.0)


def _rotate(t, cos, sin_lo, sin_hi):
    return t * cos + pltpu.roll(t, LANES - 8, 1) * sin_lo + pltpu.roll(t, 8, 1) * sin_hi


def _rotate_bwd(t, cos, sin_lo, sin_hi):
    return t * cos + pltpu.roll(t * sin_lo, 8, 1) + pltpu.roll(t * sin_hi, LANES - 8, 1)


def _fwd_in(x, vec, w_in_t, b_in, rot, name, exchanges=()):
    s = x.shape[0]
    ts = _seq_tile(s, 2)

    def body(x_ref, vec_ref, w_ref, b_ref, cos_ref, sin_ref, q_ref, k_ref, v_ref, u_ref):
        xt = x_ref[...]
        h = xt * _rsqrt_ms(xt) * vec_ref[0:1, :] * vec_ref[1:2, :] + vec_ref[2:3, :]
        proj = _dot(h.astype(BF16), w_ref[...], NT) + b_ref[0:1, :]
        cos, lo, hi = _rotary_terms(cos_ref[...], sin_ref[...])
        for j in range(ATTN_WIDTH // LANES):
            q_ref[:, j * LANES:(j + 1) * LANES] = (
                _rotate(proj[:, j * LANES:(j + 1) * LANES], cos, lo, hi) * Q_SCALE).astype(BF16)
        k_ref[...] = _rotate(proj[:, ATTN_WIDTH:ATTN_WIDTH + KV_WIDTH], cos, lo, hi).astype(BF16)
        v_ref[...] = proj[:, ATTN_WIDTH + KV_WIDTH:ATTN_WIDTH + 2 * KV_WIDTH].astype(BF16)
        u_ref[...] = proj[:, ATTN_WIDTH + 2 * KV_WIDTH:]

    row = lambda w: pl.BlockSpec((ts, w), lambda i: (i, 0))
    return _call(
        body, [x, vec, w_in_t, b_in, *rot], name=name, grid=(s // ts,),
        in_specs=[row(D_MODEL), _resident((8, D_MODEL)), _resident((IN_WIDTH, D_MODEL)), _resident((8, IN_WIDTH)),
                  row(LANES), row(LANES)],
        out_specs=[row(ATTN_WIDTH), row(KV_WIDTH), row(KV_WIDTH), row(POOL_WIDTH)],
        out_shape=[jax.ShapeDtypeStruct((s, ATTN_WIDTH), BF16), jax.ShapeDtypeStruct((s, KV_WIDTH), BF16),
                   jax.ShapeDtypeStruct((s, KV_WIDTH), BF16), jax.ShapeDtypeStruct((s, POOL_WIDTH), F32)],
        exchanges=exchanges)


def _band_mask_t(first_block):
    key = lax.broadcasted_iota(jnp.int32, (2 * BLOCK, 2 * BLOCK), 0)
    qry = lax.broadcasted_iota(jnp.int32, (2 * BLOCK, 2 * BLOCK), 1) & (BLOCK - 1)
    d = key - qry
    return (d >= 1) & (d <= BLOCK) & ((key >= BLOCK) | jnp.logical_not(first_block))


def _placed(both, kh, b):
    lane = lax.broadcasted_iota(jnp.int32, (1, LANES), 1)
    src = both if kh == b else pltpu.roll(both, HEAD_DIM, 1)
    return jnp.where((lane >= b * HEAD_DIM) & (lane < (b + 1) * HEAD_DIM), src, jnp.zeros_like(src))


def _pair(ref, rows, kh, r):
    j = 2 * kh + r
    return ref[rows, j * LANES:(j + 1) * LANES]


def _attn_specs(s):
    qb = min(ATTN_BLOCKS, s // BLOCK)
    cur = lambda w: pl.BlockSpec((qb * BLOCK, w), lambda i: (i, 0))
    prev = lambda w: pl.BlockSpec((BLOCK, w), lambda i: (jnp.maximum(i * qb - 1, 0), 0))
    return qb, cur, prev, pl.BlockSpec((N_HEADS, qb * BLOCK), lambda i: (0, i))


def _kv_window(prev_ref, cur_ref, jb):
    before = prev_ref[...] if jb == 0 else cur_ref[(jb - 1) * BLOCK:jb * BLOCK, :]
    return jnp.concatenate([before, cur_ref[jb * BLOCK:(jb + 1) * BLOCK, :]], axis=0)


def _sink_row(sink_ref, kh, b):
    lane = lax.broadcasted_iota(jnp.int32, (1, 2 * BLOCK), 1)
    return jnp.where(lane < BLOCK, sink_ref[GROUP * kh + b], sink_ref[GROUP * kh + 2 + b])


def _attn_fwd_t(sinks, q, k, v, name, exchanges=()):
    s = q.shape[0]
    qb, cur, prev, lse_spec = _attn_specs(s)

    def block(jb, sink_ref, q_ref, kc_ref, kp_ref, vc_ref, vp_ref, o_ref, l_ref):
        rows = slice(jb * BLOCK, (jb + 1) * BLOCK)
        valid = _band_mask_t(pl.program_id(0) == 0 if jb == 0 else False)
        k_both, v_both = _kv_window(kp_ref, kc_ref, jb), _kv_window(vp_ref, vc_ref, jb)
        for kh in range(N_KV_HEADS):
            qg = jnp.concatenate([_pair(q_ref, rows, kh, 0), _pair(q_ref, rows, kh, 1)], axis=0)
            keys = jnp.concatenate([_placed(k_both, kh, 0), _placed(k_both, kh, 1)], axis=0)
            st = _dot(keys, qg, NT)
            out = jnp.zeros((2 * BLOCK, LANES), F32)
            for b in range(2):
                sc = jnp.where(valid, st[b * 2 * BLOCK:(b + 1) * 2 * BLOCK, :], NEG_INF)
                sink = _sink_row(sink_ref, kh, b)
                m = jnp.maximum(jnp.max(sc, axis=0, keepdims=True), sink)
                p = jnp.exp(sc - m)
                den = jnp.sum(p, axis=0, keepdims=True) + jnp.exp(sink - m)
                out = out + _dot((p * (1.0 / den)).astype(BF16), _placed(v_both, kh, b), TN)
                lse = m + jnp.log(den)
                for r in range(2):
                    h = GROUP * kh + 2 * r + b
                    l_ref[h:h + 1, rows] = lse[:, r * BLOCK:(r + 1) * BLOCK]
            for r in range(2):
                j = 2 * kh + r
                o_ref[rows, j * LANES:(j + 1) * LANES] = out[r * BLOCK:(r + 1) * BLOCK, :].astype(BF16)

    def body(*refs):
        for jb in range(qb):
            block(jb, *refs)

    return _call(
        body, [sinks, q, k, k, v, v], name=name, grid=(s // (qb * BLOCK),),
        in_specs=[pl.BlockSpec(memory_space=pltpu.SMEM), cur(ATTN_WIDTH), cur(KV_WIDTH), prev(KV_WIDTH),
                  cur(KV_WIDTH), prev(KV_WIDTH)],
        out_specs=[cur(ATTN_WIDTH), lse_spec],
        out_shape=[jax.ShapeDtypeStruct((s, ATTN_WIDTH), BF16), jax.ShapeDtypeStruct((N_HEADS, s), F32)],
        exchanges=exchanges)


def _attn_bwd_t(sinks, q, do, lse, k, v, name, exchanges=()):
    s = q.shape[0]
    qb, cur, prev, lse_spec = _attn_specs(s)

    def block(jb, sink_ref, q_ref, do_ref, l_ref, kc_ref, kp_ref, vc_ref, vp_ref,
              dq_ref, dkc_ref, dkp_ref, dvc_ref, dvp_ref, dsink_ref):
        rows = slice(jb * BLOCK, (jb + 1) * BLOCK)
        valid = _band_mask_t(pl.program_id(0) == 0 if jb == 0 else False)
        k_both, v_both = _kv_window(kp_ref, kc_ref, jb), _kv_window(vp_ref, vc_ref, jb)
        lane = lax.broadcasted_iota(jnp.int32, (1, LANES), 1)
        col = lax.broadcasted_iota(jnp.int32, (1, 2 * BLOCK), 1)
        dk_heads, dv_heads = [], []
        for kh in range(N_KV_HEADS):
            qg = jnp.concatenate([_pair(q_ref, rows, kh, 0), _pair(q_ref, rows, kh, 1)], axis=0)
            dog = jnp.concatenate([_pair(do_ref, rows, kh, 0), _pair(do_ref, rows, kh, 1)], axis=0)
            k_placed = [_placed(k_both, kh, b) for b in range(2)]
            v_placed = [_placed(v_both, kh, b) for b in range(2)]
            st = _dot(jnp.concatenate(k_placed, axis=0), qg, NT)
            dpt = _dot(jnp.concatenate(v_placed, axis=0), dog, NT)
            dqg = jnp.zeros((2 * BLOCK, LANES), F32)
            dk_b, dv_b = [], []
            for b in range(2):
                part = slice(b * 2 * BLOCK, (b + 1) * 2 * BLOCK)
                heads = [GROUP * kh + 2 * r + b for r in range(2)]
                lse_row = jnp.concatenate([l_ref[h:h + 1, rows] for h in heads], axis=1)
                p = jnp.where(valid, jnp.exp(st[part, :] - lse_row), 0.0)
                dp = dpt[part, :]
                delta = jnp.sum(p * dp, axis=0, keepdims=True)
                sink_pull = jnp.exp(_sink_row(sink_ref, kh, b) - lse_row) * delta
                dsink_ref[heads[0]:heads[0] + 1, :] += -jnp.sum(jnp.where(col < BLOCK, sink_pull, 0.0))
                dsink_ref[heads[1]:heads[1] + 1, :] += -jnp.sum(jnp.where(col < BLOCK, 0.0, sink_pull))
                ds = (p * (dp - delta)).astype(BF16)
                dqg = dqg + _dot(ds, k_placed[b], TN)
                dk_b.append(_dot(ds, qg))
                dv_b.append(_dot(p.astype(BF16), dog))
            for parts, total in ((dk_b, dk_heads), (dv_b, dv_heads)):
                kept = jnp.where(lane < HEAD_DIM, parts[0], parts[1])
                total.append(kept + pltpu.roll(kept, HEAD_DIM, 1))
            for r in range(2):
                j = 2 * kh + r
                dq_ref[rows, j * LANES:(j + 1) * LANES] = dqg[r * BLOCK:(r + 1) * BLOCK, :] * Q_SCALE
        dk = jnp.where(lane < HEAD_DIM, dk_heads[0], dk_heads[1])
        dv = jnp.where(lane < HEAD_DIM, dv_heads[0], dv_heads[1])
        dkp_ref[rows, :] = dk[0:BLOCK, :]
        dkc_ref[rows, :] = dk[BLOCK:, :]
        dvp_ref[rows, :] = dv[0:BLOCK, :]
        dvc_ref[rows, :] = dv[BLOCK:, :]

    def body(*refs):
        @pl.when(pl.program_id(0) == 0)
        def _():
            refs[-1][...] = jnp.zeros_like(refs[-1])

        for jb in range(qb):
            block(jb, *refs)

    kv = jax.ShapeDtypeStruct((s, KV_WIDTH), F32)
    return _call(
        body, [sinks, q, do, lse, k, k, v, v], name=name, grid=(s // (qb * BLOCK),),
        in_specs=[pl.BlockSpec(memory_space=pltpu.SMEM), cur(ATTN_WIDTH), cur(ATTN_WIDTH), lse_spec,
                  cur(KV_WIDTH), prev(KV_WIDTH), cur(KV_WIDTH), prev(KV_WIDTH)],
        out_specs=[cur(ATTN_WIDTH), cur(KV_WIDTH), cur(KV_WIDTH), cur(KV_WIDTH), cur(KV_WIDTH),
                   pl.BlockSpec((8, LANES), lambda i: (0, 0))],
        out_shape=[jax.ShapeDtypeStruct((s, ATTN_WIDTH), F32), kv, kv, kv, kv, jax.ShapeDtypeStruct((8, LANES), F32)],
        exchanges=exchanges)


def _pool_counts(tile, ts):
    t = (tile * ts + lax.broadcasted_iota(jnp.int32, (ts, 1), 0) + 1).astype(F32)
    return [jnp.minimum(t, float(w)) for w in POOL_WINDOWS]


def _pooled(u, halo, counts, gi):
    cols = slice(gi * POOL_GROUP_WIDTH, (gi + 1) * POOL_GROUP_WIDTH)
    acc = jnp.concatenate([halo[:, cols], u[:, cols]], axis=0)
    shift = 1
    while shift < POOL_WINDOWS[gi]:
        acc = acc + pltpu.roll(acc, shift, 0)
        shift *= 2
    return acc[POOL_HALO:, :] / counts[gi] - u[:, cols]


def _fwd_out(ao, u, pool_w, psc, w_out, x, vec, name, exchanges=()):
    s = x.shape[0]
    ts = _seq_tile(s, 2)
    hb = ts // POOL_HALO

    def body(ao_ref, u_ref, uh_ref, pw_ref, psc_ref, w_ref, x_ref, vec_ref, x1_ref, mix_ref):
        i = pl.program_id(0)
        u_t = u_ref[...]
        halo = jnp.where(i > 0, uh_ref[...], 0.0)
        counts = _pool_counts(i, ts)
        cat = [ao_ref[...]]
        for gi in range(len(POOL_WINDOWS)):
            cols = slice(gi * POOL_GROUP_WIDTH, (gi + 1) * POOL_GROUP_WIDTH)
            og = _dot(_pooled(u_t, halo, counts, gi).astype(BF16), pw_ref[gi]) * psc_ref[0:1, cols]
            cat.append(og.astype(BF16))
        mix = _dot(jnp.concatenate(cat, axis=1), w_ref[...])
        mix_ref[...] = mix
        x1_ref[...] = x_ref[...] + vec_ref[0:1, :] * (mix * _rsqrt_ms(mix) * vec_ref[1:2, :])

    row = lambda w: pl.BlockSpec((ts, w), lambda i: (i, 0))
    return _call(
        body, [ao, u, u, pool_w, psc, w_out, x, vec], name=name, grid=(s // ts,),
        in_specs=[row(ATTN_WIDTH), row(POOL_WIDTH),
                  pl.BlockSpec((POOL_HALO, POOL_WIDTH), lambda i: (jnp.maximum(i * hb - 1, 0), 0)),
                  _resident(pool_w.shape), _resident((8, POOL_WIDTH)), _resident((D_MODEL, D_MODEL)),
                  row(D_MODEL), _resident((8, D_MODEL))],
        out_specs=[row(D_MODEL), row(D_MODEL)],
        out_shape=[jax.ShapeDtypeStruct((s, D_MODEL), F32), jax.ShapeDtypeStruct((s, D_MODEL), F32)],
        exchanges=exchanges)


def _ffn_fwd(x1, vec, w_gate_t, w_up_t, w_down, name, exchanges=(), target=None):
    s = x1.shape[0]
    ts = _seq_tile(s)

    def body(x_ref, *refs):
        if target is None:
            vec_ref, wg_ref, wu_ref, wd_ref, x2_ref, f_ref, gt_ref, up_ref, h2_ref = refs
        else:
            t_ref, vec_ref, wg_ref, wu_ref, wd_ref, x2_ref, f_ref, gt_ref, up_ref, h2_ref, loss_ref = refs
        xt = x_ref[...]
        h2 = (xt * _rsqrt_ms(xt) * vec_ref[0:1, :] * vec_ref[1:2, :] + vec_ref[2:3, :]).astype(BF16)
        h2_ref[...] = h2
        f = jnp.zeros((ts, D_MODEL), F32)
        for c in range(D_FF // FF_CHUNK):
            cols = slice(c * FF_CHUNK, (c + 1) * FF_CHUNK)
            g = _dot(h2, wg_ref[cols, :], NT)
            up = _dot(h2, wu_ref[cols, :], NT)
            act = ((g * jax.nn.sigmoid(g)) * up).astype(BF16)
            gt_ref[:, cols] = g.astype(BF16)
            up_ref[:, cols] = up.astype(BF16)
            f = f + _dot(act, wd_ref[cols, :])
        f_ref[...] = f
        x2 = xt + vec_ref[3:4, :] * (f * _rsqrt_ms(f) * vec_ref[4:5, :])
        if target is None:
            x2_ref[...] = x2
        else:
            @pl.when(pl.program_id(0) == 0)
            def _():
                loss_ref[...] = jnp.zeros_like(loss_ref)

            diff = x2 - t_ref[...]
            x2_ref[...] = diff / D_MODEL
            loss_ref[...] += 0.5 * jnp.sum(jnp.mean(diff * diff, axis=-1, keepdims=True))

    row = lambda w: pl.BlockSpec((ts, w), lambda i: (i, 0))
    wide = jax.ShapeDtypeStruct((s, D_FF), BF16)
    with_loss = target is not None
    return _call(
        body, [x1] + ([target] if with_loss else []) + [vec, w_gate_t, w_up_t, w_down], name=name, grid=(s // ts,),
        in_specs=[row(D_MODEL)] * (2 if with_loss else 1) + [
            _resident((8, D_MODEL)), _resident((D_FF, D_MODEL)), _resident((D_FF, D_MODEL)), _resident((D_FF, D_MODEL))],
        out_specs=[row(D_MODEL), row(D_MODEL), row(D_FF), row(D_FF), row(D_MODEL)] + (
            [pl.BlockSpec((8, LANES), lambda i: (0, 0))] if with_loss else []),
        out_shape=[jax.ShapeDtypeStruct((s, D_MODEL), F32), jax.ShapeDtypeStruct((s, D_MODEL), F32), wide, wide,
                   jax.ShapeDtypeStruct((s, D_MODEL), BF16)] + (
            [jax.ShapeDtypeStruct((8, LANES), F32)] if with_loss else []),
        exchanges=exchanges)


def _norm_bwd(dy_hat, x_hat, r):
    return r * (dy_hat - x_hat * jnp.mean(dy_hat * x_hat, axis=-1, keepdims=True))


def _accumulate_rows(ref, rows):
    for j, val in enumerate(rows):
        ref[j:j + 1, :] += val


def _ffn_bwd_act(dx2, f, gt, up, vec, w_down, name, exchanges=()):
    s = dx2.shape[0]
    ts = _seq_tile(s)

    def body(dx2_ref, f_ref, gt_ref, up_ref, vec_ref, wd_ref, dgt_ref, dup_ref, dwd_ref, cs_ref):
        @pl.when(pl.program_id(0) == 0)
        def _():
            cs_ref[...] = jnp.zeros_like(cs_ref)
            dwd_ref[...] = jnp.zeros_like(dwd_ref)

        dx2_t, f_t = dx2_ref[...], f_ref[...]
        gate, g_post = vec_ref[3:4, :], vec_ref[4:5, :]
        rf = _rsqrt_ms(f_t)
        f_hat = f_t * rf
        df = _norm_bwd(dx2_t * gate * g_post, f_hat, rf).astype(BF16)
        _accumulate_rows(cs_ref, [_colsum(dx2_t * (f_hat * g_post)), _colsum(dx2_t * gate * f_hat)])
        for c in range(D_FF // FF_CHUNK):
            cols = slice(c * FF_CHUNK, (c + 1) * FF_CHUNK)
            dact = _dot(df, wd_ref[cols, :], NT)
            g, u_t = gt_ref[:, cols].astype(F32), up_ref[:, cols].astype(F32)
            sg = 0.5 * jnp.tanh(0.5 * g) + 0.5
            silu = g * sg
            dgt_ref[:, cols] = (dact * u_t * (sg * (1.0 + g * (1.0 - sg)))).astype(BF16)
            dup_ref[:, cols] = (dact * silu).astype(BF16)
            dwd_ref[cols, :] += _dot((silu * u_t).astype(BF16), df, TN)

    row = lambda w: pl.BlockSpec((ts, w), lambda i: (i, 0))
    wide = jax.ShapeDtypeStruct((s, D_FF), BF16)
    return _call(
        body, [dx2, f, gt, up, vec, w_down], name=name, grid=(s // ts,),
        in_specs=[row(D_MODEL), row(D_MODEL), row(D_FF), row(D_FF), _resident((8, D_MODEL)), _resident((D_FF, D_MODEL))],
        out_specs=[row(D_FF), row(D_FF), _resident((D_FF, D_MODEL)), pl.BlockSpec((8, D_MODEL), lambda i: (0, 0))],
        out_shape=[wide, wide, jax.ShapeDtypeStruct((D_FF, D_MODEL), F32), jax.ShapeDtypeStruct((8, D_MODEL), F32)],
        exchanges=exchanges)


def _ffn_bwd_in(dx2, x1, dgt, dup, vec, w_gate_t, w_up_t, name, exchanges=()):
    s = x1.shape[0]
    ts = _seq_tile(s)

    def body(dx2_ref, x1_ref, dgt_ref, dup_ref, vec_ref, wg_ref, wu_ref, dx1_ref, cs_ref):
        @pl.when(pl.program_id(0) == 0)
        def _():
            cs_ref[...] = jnp.zeros_like(cs_ref)

        g_pre, one_scale = vec_ref[0:1, :], vec_ref[1:2, :]
        dh2 = _dot(dgt_ref[...], wg_ref[...]) + _dot(dup_ref[...], wu_ref[...])
        x1_t = x1_ref[...]
        r1 = _rsqrt_ms(x1_t)
        x_hat = x1_t * r1
        dx1_ref[...] = dx2_ref[...] + _norm_bwd(dh2 * g_pre * one_scale, x_hat, r1)
        _accumulate_rows(cs_ref, [_colsum(dh2), _colsum(dh2 * (x_hat * g_pre)), _colsum(dh2 * one_scale * x_hat)])

    row = lambda w: pl.BlockSpec((ts, w), lambda i: (i, 0))
    return _call(
        body, [dx2, x1, dgt, dup, vec, w_gate_t, w_up_t], name=name, grid=(s // ts,),
        in_specs=[row(D_MODEL), row(D_MODEL), row(D_FF), row(D_FF), _resident((8, D_MODEL)),
                  _resident((D_FF, D_MODEL)), _resident((D_FF, D_MODEL))],
        out_specs=[row(D_MODEL), pl.BlockSpec((8, D_MODEL), lambda i: (0, 0))],
        out_shape=[jax.ShapeDtypeStruct((s, D_MODEL), F32), jax.ShapeDtypeStruct((8, D_MODEL), F32)],
        exchanges=exchanges)


def _weight_grad(a, b, name, exchanges=()):
    s, m = a.shape
    n = b.shape[1]
    tk = min(s, 4 * SEQ_TILE)
    steps = s // tk

    def body(a_ref, b_ref, o_ref, acc_ref):
        i = pl.program_id(0)

        @pl.when(i == 0)
        def _():
            acc_ref[...] = jnp.zeros_like(acc_ref)

        b_t = b_ref[...]
        for c in range(m // FF_CHUNK):
            rows = slice(c * FF_CHUNK, (c + 1) * FF_CHUNK)
            acc_ref[rows, :] += _dot(a_ref[:, rows], b_t, TN)

        @pl.when(i == steps - 1)
        def _():
            o_ref[...] = acc_ref[...].astype(BF16)

    (grad,), ex_outs = _call(
        body, [a, b], name=name, grid=(steps,),
        in_specs=[pl.BlockSpec((tk, m), lambda i: (i, 0)), pl.BlockSpec((tk, n), lambda i: (i, 0))],
        out_specs=[_resident((m, n))],
        out_shape=[jax.ShapeDtypeStruct((m, n), BF16)],
        scratch_shapes=[pltpu.VMEM((m, n), F32)],
        exchanges=exchanges)
    return grad, ex_outs


def _out_bwd(dx1, mix, ao, u, pool_w, psc, w_out, vec, name, exchanges=()):
    s = dx1.shape[0]
    ts = _seq_tile(s)
    nt = s // ts
    hb = ts // POOL_HALO
    ng = len(POOL_WINDOWS)

    def body(dx1_ref, mix_ref, ao_ref, u_ref, uh_ref, pw_ref, psc_ref, w_ref, vec_ref,
             do_ref, du_ref, dw_out_ref, cs_ref, dpw_ref, dpsc_ref, carry_ref, dw_ref):
        i = pl.program_id(0)
        tile = nt - 1 - i

        @pl.when(i == 0)
        def _():
            dw_ref[...] = jnp.zeros_like(dw_ref)
            cs_ref[...] = jnp.zeros_like(cs_ref)
            dpw_ref[...] = jnp.zeros_like(dpw_ref)
            dpsc_ref[...] = jnp.zeros_like(dpsc_ref)
            carry_ref[...] = jnp.zeros_like(carry_ref)

        dx1_t, mix_t = dx1_ref[...], mix_ref[...]
        gate, g_post = vec_ref[0:1, :], vec_ref[1:2, :]
        rm = _rsqrt_ms(mix_t)
        m_hat = mix_t * rm
        dmix = _norm_bwd(dx1_t * gate * g_post, m_hat, rm).astype(BF16)
        _accumulate_rows(cs_ref, [_colsum(dx1_t * (m_hat * g_post)), _colsum(dx1_t * gate * m_hat)])
        d_cat = _dot(dmix, w_ref[...], NT)
        do_ref[...] = d_cat[:, 0:ATTN_WIDTH].astype(BF16)

        u_t = u_ref[...]
        halo = jnp.where(tile > 0, uh_ref[...], 0.0)
        counts = _pool_counts(tile, ts)
        cat = [ao_ref[...]]
        for gi in range(ng):
            cols = slice(gi * POOL_GROUP_WIDTH, (gi + 1) * POOL_GROUP_WIDTH)
            scale = psc_ref[0:1, cols]
            pooled = _pooled(u_t, halo, counts, gi).astype(BF16)
            og = _dot(pooled, pw_ref[gi])
            cat.append((og * scale).astype(BF16))
            d_out = d_cat[:, ATTN_WIDTH + gi * POOL_GROUP_WIDTH:ATTN_WIDTH + (gi + 1) * POOL_GROUP_WIDTH]
            dpsc_ref[0:1, cols] += _colsum(d_out * og)
            d_og = (d_out * scale).astype(BF16)
            dpw_ref[gi] += _dot(pooled, d_og, TN)
            d_pooled = _dot(d_og, pw_ref[gi], NT)
            spread = d_pooled / counts[gi]
            acc = jnp.concatenate([spread, carry_ref[:, cols]], axis=0)
            shift = 1
            while shift < POOL_WINDOWS[gi]:
                acc = acc + pltpu.roll(acc, ts + POOL_HALO - shift, 0)
                shift *= 2
            du_ref[:, cols] = acc[0:ts, :] - d_pooled
            carry_ref[:, cols] = spread[0:POOL_HALO, :]
        dw_ref[...] += _dot(jnp.concatenate(cat, axis=1), dmix, TN)

        @pl.when(i == nt - 1)
        def _():
            dw_out_ref[...] = dw_ref[...].astype(BF16)

    row = lambda w: pl.BlockSpec((ts, w), lambda i: (nt - 1 - i, 0))
    fixed = lambda shape: pl.BlockSpec(shape, lambda i: (0,) * len(shape))
    return _call(
        body, [dx1, mix, ao, u, u, pool_w, psc, w_out, vec], name=name, grid=(nt,),
        in_specs=[row(D_MODEL), row(D_MODEL), row(ATTN_WIDTH), row(POOL_WIDTH),
                  pl.BlockSpec((POOL_HALO, POOL_WIDTH), lambda i: (jnp.maximum((nt - 1 - i) * hb - 1, 0), 0)),
                  _resident(pool_w.shape), _resident((8, POOL_WIDTH)), _resident((D_MODEL, D_MODEL)),
                  _resident((8, D_MODEL))],
        out_specs=[row(ATTN_WIDTH), row(POOL_WIDTH), _resident((D_MODEL, D_MODEL)), fixed((8, D_MODEL)),
                   fixed(pool_w.shape), fixed((8, POOL_WIDTH))],
        out_shape=[jax.ShapeDtypeStruct((s, ATTN_WIDTH), BF16), jax.ShapeDtypeStruct((s, POOL_WIDTH), F32),
                   jax.ShapeDtypeStruct((D_MODEL, D_MODEL), BF16),
                   jax.ShapeDtypeStruct((8, D_MODEL), F32), jax.ShapeDtypeStruct(pool_w.shape, F32),
                   jax.ShapeDtypeStruct((8, POOL_WIDTH), F32)],
        scratch_shapes=[pltpu.VMEM((POOL_HALO, POOL_WIDTH), F32), pltpu.VMEM((D_MODEL, D_MODEL), F32)],
        exchanges=exchanges)


def _in_bwd(dq, dkc, dkp, dvc, dvp, du, x, dx1, vec, w_in_t, rot, name, exchanges=()):
    s = x.shape[0]
    ts = _seq_tile(s)
    nt = s // ts
    bpt = ts // BLOCK

    def body(dq_ref, dkc_ref, dkp_ref, dkn_ref, dvc_ref, dvp_ref, dvn_ref, du_ref, x_ref, dx1_ref, vec_ref, w_ref,
             cos_ref, sin_ref, dx_ref, dw_out_ref, cs_ref, db_ref, dw_ref):
        i = pl.program_id(0)

        @pl.when(i == 0)
        def _():
            dw_ref[...] = jnp.zeros_like(dw_ref)
            cs_ref[...] = jnp.zeros_like(cs_ref)
            db_ref[...] = jnp.zeros_like(db_ref)

        cos, lo, hi = _rotary_terms(cos_ref[...], sin_ref[...])

        def with_next_block(cur_ref, prev_ref, next_ref):
            nxt = jnp.where(i < nt - 1, next_ref[...], 0.0)
            later = nxt if bpt == 1 else jnp.concatenate([prev_ref[BLOCK:, :], nxt], axis=0)
            return cur_ref[...] + later

        pieces = [_rotate_bwd(dq_ref[:, j * LANES:(j + 1) * LANES], cos, lo, hi) for j in range(ATTN_WIDTH // LANES)]
        pieces.append(_rotate_bwd(with_next_block(dkc_ref, dkp_ref, dkn_ref), cos, lo, hi))
        pieces.append(with_next_block(dvc_ref, dvp_ref, dvn_ref))
        pieces.append(du_ref[...])
        dproj = jnp.concatenate(pieces, axis=1)
        db_ref[0:1, :] += _colsum(dproj)
        dproj_b = dproj.astype(BF16)
        dh = _dot(dproj_b, w_ref[...])

        xt = x_ref[...]
        g_pre, one_scale = vec_ref[0:1, :], vec_ref[1:2, :]
        r = _rsqrt_ms(xt)
        x_hat = xt * r
        dw_ref[...] += _dot(dproj_b, (x_hat * g_pre * one_scale + vec_ref[2:3, :]).astype(BF16), TN)
        dx_ref[...] = dx1_ref[...] + _norm_bwd(dh * g_pre * one_scale, x_hat, r)
        _accumulate_rows(cs_ref, [_colsum(dh), _colsum(dh * (x_hat * g_pre)), _colsum(dh * one_scale * x_hat)])

        @pl.when(i == nt - 1)
        def _():
            dw_out_ref[...] = dw_ref[...].astype(BF16)

    row = lambda w: pl.BlockSpec((ts, w), lambda i: (i, 0))
    nxt = pl.BlockSpec((BLOCK, KV_WIDTH), lambda i: (jnp.minimum((i + 1) * bpt, s // BLOCK - 1), 0))
    fixed = lambda shape: pl.BlockSpec(shape, lambda i: (0,) * len(shape))
    return _call(
        body, [dq, dkc, dkp, dkp, dvc, dvp, dvp, du, x, dx1, vec, w_in_t, *rot], name=name, grid=(nt,),
        in_specs=[row(ATTN_WIDTH), row(KV_WIDTH), row(KV_WIDTH), nxt, row(KV_WIDTH), row(KV_WIDTH), nxt,
                  row(POOL_WIDTH), row(D_MODEL), row(D_MODEL), _resident((8, D_MODEL)), _resident((IN_WIDTH, D_MODEL)),
                  row(LANES), row(LANES)],
        out_specs=[row(D_MODEL), _resident((IN_WIDTH, D_MODEL)), fixed((8, D_MODEL)), fixed((8, IN_WIDTH))],
        out_shape=[jax.ShapeDtypeStruct((s, D_MODEL), F32), jax.ShapeDtypeStruct((IN_WIDTH, D_MODEL), BF16),
                   jax.ShapeDtypeStruct((8, D_MODEL), F32), jax.ShapeDtypeStruct((8, IN_WIDTH), F32)],
        scratch_shapes=[pltpu.VMEM((IN_WIDTH, D_MODEL), F32)],
        exchanges=exchanges)


def _row_tile(rows):
    for t in (512, 352, 256, 176, 160, 128, 64, 32, 16, 8):
        if rows % t == 0:
            return t
    raise ValueError(f"no row tile for {rows} rows")


def _sum_parts(parts, first, name):
    n, r, c = parts.shape
    tr = _row_tile(r)

    def body(p_ref, o_ref):
        acc = p_ref[first].astype(F32)
        for k in range(n):
            if k != first:
                acc = acc + p_ref[k].astype(F32)
        o_ref[...] = acc

    return pl.pallas_call(
        body, name=name, grid=(r // tr,),
        in_specs=[pl.BlockSpec((n, tr, c), lambda i: (0, i, 0))],
        out_specs=pl.BlockSpec((tr, c), lambda i: (i, 0)),
        out_shape=jax.ShapeDtypeStruct((r, c), F32),
        compiler_params=_cparams(1),
    )(parts)


def _adam_step(w, g, m, v):
    m_new = ADAM_B1 * m + (1.0 - ADAM_B1) * g
    v_new = ADAM_B2 * v + (1.0 - ADAM_B2) * (g * g)
    m_hat = m_new / (1.0 - ADAM_B1 ** ADAM_STEP)
    v_hat = v_new / (1.0 - ADAM_B2 ** ADAM_STEP)
    return -ADAM_LR * (m_hat / (jnp.sqrt(v_hat) + ADAM_EPS) + ADAM_WD * w), m_new, v_new


def _sum_adamw(parts, w, m, v, name):
    depth, r, c = w.shape
    tr = _row_tile(r)

    def body(*refs):
        p_refs = refs[:depth]
        w_ref, m_ref, v_ref, g_ref, d_ref, nm_ref, nv_ref = refs[depth:]
        g = None
        for layer, p_ref in enumerate(p_refs):
            acc = p_ref[_Scatter.OWN].astype(F32)
            for k in range(N_DEV - 1):
                acc = acc + p_ref[k].astype(F32)
            g = acc if g is None else jnp.where(pl.program_id(0) == layer, acc, g)
        g_ref[0] = g
        d_ref[0], nm_ref[0], nv_ref[0] = _adam_step(w_ref[0], g, m_ref[0], v_ref[0])

    spec = pl.BlockSpec((1, tr, c), lambda l, i: (l, i, 0))
    shape = jax.ShapeDtypeStruct((depth, r, c), F32)
    return pl.pallas_call(
        body, name=name, grid=(depth, r // tr),
        in_specs=[pl.BlockSpec((N_DEV, tr, c), lambda l, i: (0, i, 0))] * depth + [spec] * 3,
        out_specs=[spec] * 4, out_shape=[shape] * 4, compiler_params=_cparams(2),
    )(*parts, w, m, v)


def _adamw(w, g, m, v, name):
    r, c = w.shape
    tr = _row_tile(r) if r % 8 == 0 else r

    def body(w_ref, g_ref, m_ref, v_ref, d_ref, nm_ref, nv_ref):
        d_ref[...], nm_ref[...], nv_ref[...] = _adam_step(w_ref[...], g_ref[...], m_ref[...], v_ref[...])

    spec = pl.BlockSpec((tr, c), lambda i: (i, 0))
    shape = jax.ShapeDtypeStruct((r, c), F32)
    return pl.pallas_call(
        body, name=name, grid=(r // tr,), in_specs=[spec] * 4, out_specs=[spec] * 3, out_shape=[shape] * 3,
        compiler_params=_cparams(1),
    )(w, g, m, v)


LAYER_ROWS = 80


def _pack_layer(dmod, gains, b_in, pool_scale, sinks, pool_w, loss):
    misc = jnp.concatenate([pool_scale, sinks, jnp.reshape(loss, (1,)),
                            jnp.zeros((D_MODEL - POOL_WIDTH - N_HEADS - 1,), F32)])
    parts = [dmod.reshape(N_MOD, D_MODEL), gains, jnp.pad(b_in, (0, 2 * D_MODEL - IN_WIDTH)).reshape(2, D_MODEL),
             misc.reshape(1, D_MODEL), pool_w.reshape(-1, D_MODEL)]
    packed = jnp.concatenate(parts, axis=0)
    return jnp.pad(packed, ((0, LAYER_ROWS - packed.shape[0]), (0, 0)))


def _unpack_layer(p):
    r = N_MOD + 4
    b_in = p[r:r + 2].reshape(2 * D_MODEL)[:IN_WIDTH]
    pool_w = p[r + 3:r + 3 + 64].reshape(len(POOL_WINDOWS), POOL_GROUP_WIDTH, POOL_GROUP_WIDTH)
    return (p[0:N_MOD].reshape(N_MOD * D_MODEL), p[N_MOD:r], b_in, p[r + 2, :POOL_WIDTH],
            p[r + 2, POOL_WIDTH:POOL_WIDTH + N_HEADS], pool_w, p[r + 2, POOL_WIDTH + N_HEADS])


def _rotary_tables(positions):
    inv_freq = ROPE_THETA ** (-jnp.arange(0, ROT_DIM, 2, dtype=F32) / ROT_DIM)
    head_freq = jnp.concatenate([inv_freq, inv_freq, jnp.zeros((HEAD_DIM - ROT_DIM,), F32)])
    lane_freq = jnp.concatenate([head_freq, head_freq])
    ang = positions.reshape(-1).astype(F32)[:, None] * lane_freq[None, :]
    return jnp.cos(ang), jnp.sin(ang)


def kernel(x, c, positions, ada_w, ada_b, w_in, b_in, sinks, pool_w, pool_scale, w_out, w_gate, w_up, w_down, g_pre_mix, g_post_mix, g_pre_ffn, g_post_ffn, loss_target, m_ada_w, m_ada_b, m_w_in, m_b_in, m_sinks, m_pool_w, m_pool_scale, m_w_out, m_w_gate, m_w_up, m_w_down, m_g_pre_mix, m_g_post_mix, m_g_pre_ffn, m_g_post_ffn, v_ada_w, v_ada_b, v_w_in, v_b_in, v_sinks, v_pool_w, v_pool_scale, v_w_out, v_w_gate, v_w_up, v_w_down, v_g_pre_mix, v_g_post_mix, v_g_pre_ffn, v_g_post_ffn):
    me = _index(_mesh_place())
    x0 = x[0]
    target = loss_target[0]
    rot = _rotary_tables(positions)
    ada_cols = ada_w.shape[2]

    shards = []
    for l in range(DEPTH):
        shards += [w_in[l].T.astype(BF16), w_out[l].astype(BF16), w_gate[l].T.astype(BF16), w_up[l].T.astype(BF16),
                   w_down[l].astype(BF16)]
    whole = lambda gathered: [g.reshape(-1, D_MODEL) for g in gathered]
    full = [None] * (5 * DEPTH)
    first = _exchanges_alone([_Gather(shards[0:1] + [jnp.broadcast_to(c, (8, D_MODEL))])], "gather_first")[0]
    full[0:1] = whole(first[0:1])

    mod_part, c_act = _mod_fwd(first[1][:, 0, :], ada_w)
    mod_all = _allgather_vmem(mod_part.reshape(DEPTH * N_DEV, ada_cols), "gather_mod")
    mod_mine = lax.dynamic_slice_in_dim(mod_all.reshape(N_DEV, DEPTH, N_DEV, ada_cols), me, 1, axis=2)[:, :, 0, :]
    mod = jnp.transpose(mod_mine, (1, 0, 2)).reshape(DEPTH, N_MOD * D_MODEL) + ada_b
    mod = mod.reshape(DEPTH, N_MOD, D_MODEL)

    hosted = {("fwd_in", 0): (_Gather(shards[2:3]), [2]), ("attn_fwd", 0): (_Gather([shards[3], shards[1]]), [3, 1]),
              ("fwd_out", 0): (_Gather(shards[4:5]), [4]),
              ("ffn_fwd", 0): (_Gather(shards[5:10]), [5, 6, 7, 8, 9])}

    def take(kind, l, ex_outs):
        if (kind, l) in hosted:
            for slot, g in zip(hosted[kind, l][1], whole(ex_outs[0])):
                full[slot] = g

    def beside(kind, l):
        return [hosted[kind, l][0]] if (kind, l) in hosted else []

    pool_w_b = pool_w.astype(BF16)

    saved = []
    xl = x0
    for l in range(DEPTH):
        vec_in = _rows(g_pre_mix[l], 1.0 + mod[l, 1], mod[l, 0], width=D_MODEL)
        vec_out = _rows(mod[l, 2], g_post_mix[l], width=D_MODEL)
        vec_ffn = _rows(g_pre_ffn[l], 1.0 + mod[l, 4], mod[l, 3], mod[l, 5], g_post_ffn[l], width=D_MODEL)
        psc = _rows(pool_scale[l], width=POOL_WIDTH)
        (q, k, v, u), ex = _fwd_in(xl, vec_in, full[5 * l], _rows(b_in[l], width=IN_WIDTH), rot, f"fwd_in_{l}",
                                   beside("fwd_in", l))
        take("fwd_in", l, ex)
        (ao, lse), ex = _attn_fwd_t(sinks[l], q, k, v, f"attn_fwd_{l}", beside("attn_fwd", l))
        take("attn_fwd", l, ex)
        (x1, mix), ex = _fwd_out(ao, u, pool_w_b[l], psc, full[5 * l + 1], xl, vec_out, f"fwd_out_{l}",
                                 beside("fwd_out", l))
        take("fwd_out", l, ex)
        (x2, f, gt, up, h2, *loss_part), ex = _ffn_fwd(
            x1, vec_ffn, full[5 * l + 2], full[5 * l + 3], full[5 * l + 4], f"ffn_fwd_{l}", beside("ffn_fwd", l),
            target=target if l == DEPTH - 1 else None)
        take("ffn_fwd", l, ex)
        saved.append((xl, q, k, v, u, ao, lse, x1, mix, f, gt, up, h2, vec_in, vec_out, vec_ffn, psc))
        xl = x2

    dx = xl
    my_loss = loss_part[0][0, 0]

    stacks = [None] * (5 * DEPTH)
    parts = [None] * (5 * DEPTH)
    small_all = [None] * DEPTH
    ready_stacks, ready_small = [], []

    def leaving(row_budget):
        exs, notes = [], []
        taken = []
        while ready_stacks and stacks[ready_stacks[0]].shape[1] <= row_budget:
            row_budget -= stacks[ready_stacks[0]].shape[1]
            taken.append(ready_stacks.pop(0))
        if taken:
            exs.append(_Scatter([stacks[a] for a in taken]))
            notes.append(("stacks", taken))
        if ready_small and row_budget >= 0:
            exs.append(_Gather([ready_small[0][1]]))
            notes.append(("small", ready_small[0][0]))
            ready_small.clear()
        return exs, notes

    def arrived(notes, ex_outs):
        for (kind, what), outs in zip(notes, ex_outs):
            if kind == "stacks":
                for a, o in zip(what, outs):
                    parts[a] = o
            else:
                small_all[what] = outs[0]

    def grad_ready(a, grad):
        stacks[a] = grad.reshape(N_DEV, -1, D_MODEL)
        ready_stacks.append(a)

    def hosting(row_budget, kernel_fn, *args):
        exs, notes = leaving(row_budget)
        outs, ex_outs = kernel_fn(*args, exs)
        arrived(notes, ex_outs)
        return outs

    for l in reversed(range(DEPTH)):
        w_in_t, w_out_f, w_gate_t, w_up_t, w_down_f = full[5 * l:5 * l + 5]
        xin, q, k, v, u, ao, lse, x1, mix, f, gt, up, h2, vec_in, vec_out, vec_ffn, psc = saved[l]
        last = l == 0
        dgt, dup, d_w_down, cs_a = hosting(600, _ffn_bwd_act, dx, f, gt, up, vec_ffn, w_down_f, f"ffn_bwd_act_{l}")
        grad_ready(5 * l + 4, d_w_down.astype(BF16))
        grad_ready(5 * l + 2, hosting(-1, _weight_grad, dgt, h2, f"grad_w_gate_{l}"))
        grad_ready(5 * l + 3, hosting(-1, _weight_grad, dup, h2, f"grad_w_up_{l}"))
        dx1, cs_f = hosting(400, _ffn_bwd_in, dx, x1, dgt, dup, vec_ffn, w_gate_t, w_up_t, f"ffn_bwd_in_{l}")
        do, du, d_w_out, cs_o, dpw, dpsc = hosting(360, _out_bwd, dx1, mix, ao, u, pool_w_b[l], psc, w_out_f, vec_out,
                                                   f"out_bwd_{l}")
        grad_ready(5 * l + 1, d_w_out)
        dq, dkc, dkp, dvc, dvp, dsk = hosting(480 if last else 400, _attn_bwd_t, sinks[l], q, do, lse, k, v,
                                              f"attn_bwd_{l}")
        dx, d_w_in_t, cs_i, db = hosting(-1 if last else 250, _in_bwd, dq, dkc, dkp, dvc, dvp, du, xin, dx1, vec_in,
                                         w_in_t, rot, f"in_bwd_{l}")
        grad_ready(5 * l, d_w_in_t)
        d_mod = jnp.concatenate([cs_i[0], cs_i[1], cs_o[0], cs_f[0], cs_f[1], cs_a[0]])
        d_gain = jnp.stack([cs_i[2], cs_o[1], cs_f[2], cs_a[1]])
        ready_small.append((l, _pack_layer(d_mod, d_gain, db[0], dpsc[0], dsk[:, 0], dpw,
                                           my_loss if l == DEPTH - 1 else jnp.zeros((), F32))))
    grad_x = dx[None]
    exs, notes = leaving(N_DEV * D_FF)
    arrived(notes, _exchanges_alone(exs, "exchange_last"))

    layer_sums = [_unpack_layer(_sum_parts(small_all[l], 0, f"sum_small_{l}")) for l in range(DEPTH)]
    g_ada_b, g_gains, g_b_in, g_pool_scale, g_sinks, g_pool_w = (
        jnp.stack([layer_sums[l][j] for l in range(DEPTH)], axis=1 if j == 1 else 0) for j in range(6))
    loss = layer_sums[DEPTH - 1][6]

    dmod_all = jnp.stack([small_all[l][:, 0:N_MOD, :].reshape(N_DEV, N_DEV, ada_cols) for l in range(DEPTH)])
    dmod_cols = lax.dynamic_slice_in_dim(dmod_all, me, 1, axis=2)[:, :, 0, :]
    g_ada_w = _ada_grad(c_act, dmod_cols)

    slot = {"w_in": 0, "w_out": 1, "w_gate": 2, "w_up": 3, "w_down": 4}
    held_transposed = ("w_in", "w_gate", "w_up")
    swap = lambda a: jnp.transpose(a, (0, 2, 1))

    def step(w, g, m, v, name):
        if name in slot:
            mine = [parts[5 * l + slot[name]] for l in range(DEPTH)]
            if name in held_transposed:
                return [swap(o) for o in _sum_adamw(mine, swap(w), swap(m), swap(v), f"adamw_{name}")]
            return _sum_adamw(mine, w, m, v, f"adamw_{name}")
        flat = lambda a: a.reshape(-1, a.shape[-1])
        return [g] + [o.reshape(w.shape) for o in _adamw(flat(w), flat(g), flat(m), flat(v), f"adamw_{name}")]

    in_order = [
        ("ada_w", ada_w, g_ada_w, m_ada_w, v_ada_w), ("ada_b", ada_b, g_ada_b, m_ada_b, v_ada_b),
        ("w_in", w_in, None, m_w_in, v_w_in), ("b_in", b_in, g_b_in, m_b_in, v_b_in),
        ("sinks", sinks, g_sinks, m_sinks, v_sinks), ("pool_w", pool_w, g_pool_w, m_pool_w, v_pool_w),
        ("pool_scale", pool_scale, g_pool_scale, m_pool_scale, v_pool_scale),
        ("w_out", w_out, None, m_w_out, v_w_out), ("w_gate", w_gate, None, m_w_gate, v_w_gate),
        ("w_up", w_up, None, m_w_up, v_w_up), ("w_down", w_down, None, m_w_down, v_w_down),
        ("g_pre_mix", g_pre_mix, g_gains[0], m_g_pre_mix, v_g_pre_mix),
        ("g_post_mix", g_post_mix, g_gains[1], m_g_post_mix, v_g_post_mix),
        ("g_pre_ffn", g_pre_ffn, g_gains[2], m_g_pre_ffn, v_g_pre_ffn),
        ("g_post_ffn", g_post_ffn, g_gains[3], m_g_post_ffn, v_g_post_ffn)]
    steps = [step(w, g, m, v, n) for n, w, g, m, v in in_order]
    grads, deltas, new_m, new_v = ([s[j] for s in steps] for j in range(4))
    return (loss, grad_x, *grads, *deltas, *new_m, *new_v)
```
